```python
import jax, jax.numpy as jnp
from jax import lax
import numpy as np

D_MODEL = 1024
BATCH = 8
SEQ = 2048
DEPTH = 1

MIX_WIDTH = D_MODEL
CONV_WIDTH = MIX_WIDTH // 2
CONV_HEADS = 8
CONV_KERNEL = 31
POOL_WIDTH = MIX_WIDTH - CONV_WIDTH
POOL_WINDOWS = (2, 4, 8, 16)
POOL_GROUPS = len(POOL_WINDOWS)
POOL_GROUP_DIM = POOL_WIDTH // POOL_GROUPS
IN_WIDTH = 2 * CONV_WIDTH + POOL_WIDTH
D_FF = ((8 * D_MODEL // 3 + 255) // 256) * 256
N_MOD = 6
EPS = 1e-6

kernel_name = "hybrid_conv_pool_adaln_block"


def rmsnorm(x, g):
    xf = x.astype(jnp.float32)
    y = xf * lax.rsqrt(jnp.mean(xf * xf, axis=-1, keepdims=True) + EPS)
    return (y * g.astype(jnp.float32)).astype(x.dtype)


def layernorm(x, g, b):
    xf = x.astype(jnp.float32)
    mu = jnp.mean(xf, axis=-1, keepdims=True)
    var = jnp.mean(jnp.square(xf - mu), axis=-1, keepdims=True)
    y = (xf - mu) * lax.rsqrt(var + EPS)
    return (y * g.astype(jnp.float32) + b.astype(jnp.float32)).astype(x.dtype)


def conv_mixer(u, dw_w, dw_b, ln_g, ln_b, w_pw):
    a, g = jnp.split(u, 2, axis=-1)
    h = a * jax.nn.sigmoid(g)
    h = lax.conv_general_dilated(
        h, dw_w[:, None, :].astype(h.dtype), window_strides=(1,),
        padding=[(CONV_KERNEL - 1, 0)],
        dimension_numbers=('NWC', 'WIO', 'NWC'),
        feature_group_count=CONV_WIDTH) + dw_b
    h = layernorm(h, ln_g, ln_b)
    h = jax.nn.silu(h)
    return h @ w_pw


def pool_mixer(v, w_group, scale):
    B, S, _ = v.shape
    vg = v.astype(jnp.float32).reshape(B, S, POOL_GROUPS, POOL_GROUP_DIM)
    cs = jnp.cumsum(vg, axis=1)
    t = jnp.arange(S)
    pooled = []
    for gi, w in enumerate(POOL_WINDOWS):
        c_g = cs[:, :, gi]
        shifted = jnp.pad(c_g, ((0, 0), (w, 0), (0, 0)))[:, :S]
        cnt = jnp.minimum(t + 1, w).astype(jnp.float32)
        pooled.append((c_g - shifted) / cnt[None, :, None])
    p = (jnp.stack(pooled, axis=2) - vg).astype(v.dtype)
    y = jnp.einsum('bsgc,gcd->bsgd', p, w_group).reshape(B, S, POOL_WIDTH)
    return y * scale


def _fwd_setup_inputs(seed: int = 0) -> dict:
    key = jax.random.key(seed)
    ks = jax.random.split(key, 20)
    L, D = DEPTH, D_MODEL
    n = lambda k, shp, s: jax.random.normal(k, shp, jnp.float32) * s
    return {
        "x": n(ks[0], (BATCH, SEQ, D), 1.0),
        "c": n(ks[1], (BATCH, D), 1.0),
        "w_ada": n(ks[2], (L, D, N_MOD * D), 0.3 * D ** -0.5),
        "b_ada": n(ks[3], (L, N_MOD * D), 0.01),
        "g_norm1": 1.0 + n(ks[4], (L, D), 0.05),
        "w_in": n(ks[5], (L, D, IN_WIDTH), D ** -0.5),
        "dw_w": n(ks[6], (L, CONV_KERNEL, CONV_WIDTH), CONV_KERNEL ** -0.5),
        "dw_b": n(ks[7], (L, CONV_WIDTH), 0.01),
        "conv_ln_g": 1.0 + n(ks[8], (L, CONV_WIDTH), 0.05),
        "conv_ln_b": n(ks[9], (L, CONV_WIDTH), 0.01),
        "w_conv_pw": n(ks[10], (L, CONV_WIDTH, CONV_WIDTH), CONV_WIDTH ** -0.5),
        "w_pool_group": n(ks[11], (L, POOL_GROUPS, POOL_GROUP_DIM, POOL_GROUP_DIM), POOL_GROUP_DIM ** -0.5),
        "pool_scale": 1.0 + n(ks[12], (L, POOL_WIDTH), 0.1),
        "w_out": n(ks[13], (L, MIX_WIDTH, D), MIX_WIDTH ** -0.5),
        "g_norm2": 1.0 + n(ks[14], (L, D), 0.05),
        "w_ffn_gate": n(ks[15], (L, D, D_FF), D ** -0.5),
        "w_ffn_up": n(ks[16], (L, D, D_FF), D ** -0.5),
        "w_ffn_down": n(ks[17], (L, D_FF, D), D_FF ** -0.5),
        "g_final": 1.0 + n(ks[18], (D,), 0.05),
    }


def _fwd_reference(x, c, w_ada, b_ada, g_norm1, w_in, dw_w, dw_b, conv_ln_g, conv_ln_b,
              w_conv_pw, w_pool_group, pool_scale, w_out, g_norm2,
              w_ffn_gate, w_ffn_up, w_ffn_down, g_final):
    c_act = jax.nn.silu(c)
    for l in range(DEPTH):
        mod = c_act @ w_ada[l] + b_ada[l]
        sh1, sc1, gt1, sh2, sc2, gt2 = [m[:, None, :] for m in jnp.split(mod, N_MOD, axis=-1)]

        h = rmsnorm(x, g_norm1[l]) * (1 + sc1) + sh1
        u = h @ w_in[l]
        u_conv = u[..., :2 * CONV_WIDTH]
        u_pool = u[..., 2 * CONV_WIDTH:]
        y_conv = conv_mixer(u_conv, dw_w[l], dw_b[l], conv_ln_g[l], conv_ln_b[l], w_conv_pw[l])
        y_pool = pool_mixer(u_pool, w_pool_group[l], pool_scale[l])
        y = jnp.concatenate([y_conv, y_pool], axis=-1) @ w_out[l]
        x = x + gt1 * y

        h = rmsnorm(x, g_norm2[l]) * (1 + sc2) + sh2
        f = (jax.nn.silu(h @ w_ffn_gate[l]) * (h @ w_ffn_up[l])) @ w_ffn_down[l]
        x = x + gt2 * f
    return rmsnorm(x, g_final)


import jax as _jax
import jax.numpy as _jnp

TWIN_FORMAT = 'train_step'
FWD_PARAMS = ['x', 'c', 'w_ada', 'b_ada', 'g_norm1', 'w_in', 'dw_w', 'dw_b', 'conv_ln_g', 'conv_ln_b', 'w_conv_pw', 'w_pool_group', 'pool_scale', 'w_out', 'g_norm2', 'w_ffn_gate', 'w_ffn_up', 'w_ffn_down', 'g_final']
TWIN_WEIGHTS = ['w_ada', 'b_ada', 'g_norm1', 'w_in', 'dw_w', 'dw_b', 'conv_ln_g', 'conv_ln_b', 'w_conv_pw', 'w_pool_group', 'pool_scale', 'w_out', 'g_norm2', 'w_ffn_gate', 'w_ffn_up', 'w_ffn_down', 'g_final']
TWIN_DIFF_INPUT = 'x'
TWIN_INPUTS = ['x', 'c', 'w_ada', 'b_ada', 'g_norm1', 'w_in', 'dw_w', 'dw_b', 'conv_ln_g', 'conv_ln_b', 'w_conv_pw', 'w_pool_group', 'pool_scale', 'w_out', 'g_norm2', 'w_ffn_gate', 'w_ffn_up', 'w_ffn_down', 'g_final', 'loss_target', 'm_w_ada', 'm_b_ada', 'm_g_norm1', 'm_w_in', 'm_dw_w', 'm_dw_b', 'm_conv_ln_g', 'm_conv_ln_b', 'm_w_conv_pw', 'm_w_pool_group', 'm_pool_scale', 'm_w_out', 'm_g_norm2', 'm_w_ffn_gate', 'm_w_ffn_up', 'm_w_ffn_down', 'm_g_final', 'v_w_ada', 'v_b_ada', 'v_g_norm1', 'v_w_in', 'v_dw_w', 'v_dw_b', 'v_conv_ln_g', 'v_conv_ln_b', 'v_w_conv_pw', 'v_w_pool_group', 'v_pool_scale', 'v_w_out', 'v_g_norm2', 'v_w_ffn_gate', 'v_w_ffn_up', 'v_w_ffn_down', 'v_g_final']
TWIN_OUTPUTS = ['loss', 'grad_x', 'grad_w_ada', 'grad_b_ada', 'grad_g_norm1', 'grad_w_in', 'grad_dw_w', 'grad_dw_b', 'grad_conv_ln_g', 'grad_conv_ln_b', 'grad_w_conv_pw', 'grad_w_pool_group', 'grad_pool_scale', 'grad_w_out', 'grad_g_norm2', 'grad_w_ffn_gate', 'grad_w_ffn_up', 'grad_w_ffn_down', 'grad_g_final', 'delta_w_ada', 'delta_b_ada', 'delta_g_norm1', 'delta_w_in', 'delta_dw_w', 'delta_dw_b', 'delta_conv_ln_g', 'delta_conv_ln_b', 'delta_w_conv_pw', 'delta_w_pool_group', 'delta_pool_scale', 'delta_w_out', 'delta_g_norm2', 'delta_w_ffn_gate', 'delta_w_ffn_up', 'delta_w_ffn_down', 'delta_g_final', 'new_m_w_ada', 'new_m_b_ada', 'new_m_g_norm1', 'new_m_w_in', 'new_m_dw_w', 'new_m_dw_b', 'new_m_conv_ln_g', 'new_m_conv_ln_b', 'new_m_w_conv_pw', 'new_m_w_pool_group', 'new_m_pool_scale', 'new_m_w_out', 'new_m_g_norm2', 'new_m_w_ffn_gate', 'new_m_w_ffn_up', 'new_m_w_ffn_down', 'new_m_g_final', 'new_v_w_ada', 'new_v_b_ada', 'new_v_g_norm1', 'new_v_w_in', 'new_v_dw_w', 'new_v_dw_b', 'new_v_conv_ln_g', 'new_v_conv_ln_b', 'new_v_w_conv_pw', 'new_v_w_pool_group', 'new_v_pool_scale', 'new_v_w_out', 'new_v_g_norm2', 'new_v_w_ffn_gate', 'new_v_w_ffn_up', 'new_v_w_ffn_down', 'new_v_g_final']
TWIN_LEAF_KINDS = {'loss': 'loss', 'grad_x': 'grad_x', 'grad_w_ada': 'grad_w', 'grad_b_ada': 'grad_w', 'grad_g_norm1': 'grad_w', 'grad_w_in': 'grad_w', 'grad_dw_w': 'grad_w', 'grad_dw_b': 'grad_w', 'grad_conv_ln_g': 'grad_w', 'grad_conv_ln_b': 'grad_w', 'grad_w_conv_pw': 'grad_w', 'grad_w_pool_group': 'grad_w', 'grad_pool_scale': 'grad_w', 'grad_w_out': 'grad_w', 'grad_g_norm2': 'grad_w', 'grad_w_ffn_gate': 'grad_w', 'grad_w_ffn_up': 'grad_w', 'grad_w_ffn_down': 'grad_w', 'grad_g_final': 'grad_w', 'delta_w_ada': 'delta_w', 'delta_b_ada': 'delta_w', 'delta_g_norm1': 'delta_w', 'delta_w_in': 'delta_w', 'delta_dw_w': 'delta_w', 'delta_dw_b': 'delta_w', 'delta_conv_ln_g': 'delta_w', 'delta_conv_ln_b': 'delta_w', 'delta_w_conv_pw': 'delta_w', 'delta_w_pool_group': 'delta_w', 'delta_pool_scale': 'delta_w', 'delta_w_out': 'delta_w', 'delta_g_norm2': 'delta_w', 'delta_w_ffn_gate': 'delta_w', 'delta_w_ffn_up': 'delta_w', 'delta_w_ffn_down': 'delta_w', 'delta_g_final': 'delta_w', 'new_m_w_ada': 'new_m', 'new_m_b_ada': 'new_m', 'new_m_g_norm1': 'new_m', 'new_m_w_in': 'new_m', 'new_m_dw_w': 'new_m', 'new_m_dw_b': 'new_m', 'new_m_conv_ln_g': 'new_m', 'new_m_conv_ln_b': 'new_m', 'new_m_w_conv_pw': 'new_m', 'new_m_w_pool_group': 'new_m', 'new_m_pool_scale': 'new_m', 'new_m_w_out': 'new_m', 'new_m_g_norm2': 'new_m', 'new_m_w_ffn_gate': 'new_m', 'new_m_w_ffn_up': 'new_m', 'new_m_w_ffn_down': 'new_m', 'new_m_g_final': 'new_m', 'new_v_w_ada': 'new_v', 'new_v_b_ada': 'new_v', 'new_v_g_norm1': 'new_v', 'new_v_w_in': 'new_v', 'new_v_dw_w': 'new_v', 'new_v_dw_b': 'new_v', 'new_v_conv_ln_g': 'new_v', 'new_v_conv_ln_b': 'new_v', 'new_v_w_conv_pw': 'new_v', 'new_v_w_pool_group': 'new_v', 'new_v_pool_scale': 'new_v', 'new_v_w_out': 'new_v', 'new_v_g_norm2': 'new_v', 'new_v_w_ffn_gate': 'new_v', 'new_v_w_ffn_up': 'new_v', 'new_v_w_ffn_down': 'new_v', 'new_v_g_final': 'new_v'}


def _forward(args):
    return _fwd_reference(*[args[k] for k in FWD_PARAMS])


def _output_shape():
    out = _jax.eval_shape(lambda: _forward(_fwd_setup_inputs(0)))
    return out.shape, out.dtype

N_MICROBATCH = 1
ADAM_LR = 0.001
ADAM_B1 = 0.9
ADAM_B2 = 0.999
ADAM_EPS = 1e-08
ADAM_WD = 0.01
ADAM_STEP = 10
PER_EXAMPLE_BATCH_AXIS = {'x': 0, 'c': 0, 'loss_target': 0}
SHARED_INPUTS = []
_WEIGHT_DTYPES = {'w_ada': _jnp.float32, 'b_ada': _jnp.float32, 'g_norm1': _jnp.float32, 'w_in': _jnp.float32, 'dw_w': _jnp.float32, 'dw_b': _jnp.float32, 'conv_ln_g': _jnp.float32, 'conv_ln_b': _jnp.float32, 'w_conv_pw': _jnp.float32, 'w_pool_group': _jnp.float32, 'pool_scale': _jnp.float32, 'w_out': _jnp.float32, 'g_norm2': _jnp.float32, 'w_ffn_gate': _jnp.float32, 'w_ffn_up': _jnp.float32, 'w_ffn_down': _jnp.float32, 'g_final': _jnp.float32}
MOMENT_SCALE = {'w_ada': 3.736054e-02, 'b_ada': 6.394355e-02, 'g_norm1': 1.830084e-02, 'w_in': 1.512462e-02, 'dw_w': 1.449891e-02, 'dw_b': 2.901904e-02, 'conv_ln_g': 1.703243e-02, 'conv_ln_b': 1.651105e-02, 'w_conv_pw': 1.416781e-02, 'w_pool_group': 2.121337e-02, 'pool_scale': 2.029817e-02, 'w_out': 1.802178e-02, 'g_norm2': 2.185492e-02, 'w_ffn_gate': 9.177627e-03, 'w_ffn_up': 8.900568e-03, 'w_ffn_down': 1.475872e-02, 'g_final': 1.603212e+01}


def _to_microbatches(a, axis):
    t = _jnp.moveaxis(a, axis, 0)
    t = t.reshape((N_MICROBATCH, t.shape[0] // N_MICROBATCH) + t.shape[1:])
    return _jnp.moveaxis(t, 1, axis + 1)


def setup_inputs(seed: int = 0) -> dict:
    inp = _fwd_setup_inputs(seed)
    key = _jax.random.fold_in(_jax.random.key(seed), 7919)
    shape, _ = _output_shape()
    out = dict(inp)
    out["loss_target"] = _jax.random.normal(_jax.random.fold_in(key, 0), shape, _jnp.float32)
    for i, name in enumerate(TWIN_WEIGHTS):
        w = inp[name].astype(_jnp.float32)
        if MOMENT_SCALE is None:
            s = _jnp.sqrt(_jnp.mean(_jnp.square(w)) + 1e-30)
        else:
            s = MOMENT_SCALE[name]
        km, kv = _jax.random.split(_jax.random.fold_in(key, i + 1))
        out[name] = w
        out["m_" + name] = s * _jax.random.normal(km, w.shape, _jnp.float32)
        out["v_" + name] = (s * s) * _jax.random.uniform(kv, w.shape, _jnp.float32, 0.5, 1.5)
    if N_MICROBATCH > 1:
        for name, axis in PER_EXAMPLE_BATCH_AXIS.items():
            out[name] = _to_microbatches(out[name], axis)
    return {'x': out['x'], 'c': out['c'], 'w_ada': out['w_ada'], 'b_ada': out['b_ada'], 'g_norm1': out['g_norm1'], 'w_in': out['w_in'], 'dw_w': out['dw_w'], 'dw_b': out['dw_b'], 'conv_ln_g': out['conv_ln_g'], 'conv_ln_b': out['conv_ln_b'], 'w_conv_pw': out['w_conv_pw'], 'w_pool_group': out['w_pool_group'], 'pool_scale': out['pool_scale'], 'w_out': out['w_out'], 'g_norm2': out['g_norm2'], 'w_ffn_gate': out['w_ffn_gate'], 'w_ffn_up': out['w_ffn_up'], 'w_ffn_down': out['w_ffn_down'], 'g_final': out['g_final'], 'loss_target': out['loss_target'], 'm_w_ada': out['m_w_ada'], 'm_b_ada': out['m_b_ada'], 'm_g_norm1': out['m_g_norm1'], 'm_w_in': out['m_w_in'], 'm_dw_w': out['m_dw_w'], 'm_dw_b': out['m_dw_b'], 'm_conv_ln_g': out['m_conv_ln_g'], 'm_conv_ln_b': out['m_conv_ln_b'], 'm_w_conv_pw': out['m_w_conv_pw'], 'm_w_pool_group': out['m_w_pool_group'], 'm_pool_scale': out['m_pool_scale'], 'm_w_out': out['m_w_out'], 'm_g_norm2': out['m_g_norm2'], 'm_w_ffn_gate': out['m_w_ffn_gate'], 'm_w_ffn_up': out['m_w_ffn_up'], 'm_w_ffn_down': out['m_w_ffn_down'], 'm_g_final': out['m_g_final'], 'v_w_ada': out['v_w_ada'], 'v_b_ada': out['v_b_ada'], 'v_g_norm1': out['v_g_norm1'], 'v_w_in': out['v_w_in'], 'v_dw_w': out['v_dw_w'], 'v_dw_b': out['v_dw_b'], 'v_conv_ln_g': out['v_conv_ln_g'], 'v_conv_ln_b': out['v_conv_ln_b'], 'v_w_conv_pw': out['v_w_conv_pw'], 'v_w_pool_group': out['v_w_pool_group'], 'v_pool_scale': out['v_pool_scale'], 'v_w_out': out['v_w_out'], 'v_g_norm2': out['v_g_norm2'], 'v_w_ffn_gate': out['v_w_ffn_gate'], 'v_w_ffn_up': out['v_w_ffn_up'], 'v_w_ffn_down': out['v_w_ffn_down'], 'v_g_final': out['v_g_final']}


def _loss(weights, diff, rest, loss_target):
    with _jax.named_scope("forward"):
        args = {**rest, TWIN_DIFF_INPUT: diff, **{k: w.astype(_WEIGHT_DTYPES[k]) for k, w in weights.items()}}
        y = _forward(args)
    with _jax.named_scope("loss_head"):
        err = _jnp.square(y.astype(_jnp.float32) - loss_target)
        return 0.5 * _jnp.sum(_jnp.mean(err, axis=-1)) if err.ndim else 0.5 * err


def _adamw(w, g, m, v):
    m = ADAM_B1 * m + (1.0 - ADAM_B1) * g
    v = ADAM_B2 * v + (1.0 - ADAM_B2) * _jnp.square(g)
    m_hat = m / (1.0 - ADAM_B1 ** ADAM_STEP)
    v_hat = v / (1.0 - ADAM_B2 ** ADAM_STEP)
    delta = -ADAM_LR * (m_hat / (_jnp.sqrt(v_hat) + ADAM_EPS) + ADAM_WD * w)
    return delta, m, v


def reference(x, c, w_ada, b_ada, g_norm1, w_in, dw_w, dw_b, conv_ln_g, conv_ln_b, w_conv_pw, w_pool_group, pool_scale, w_out, g_norm2, w_ffn_gate, w_ffn_up, w_ffn_down, g_final, loss_target, m_w_ada, m_b_ada, m_g_norm1, m_w_in, m_dw_w, m_dw_b, m_conv_ln_g, m_conv_ln_b, m_w_conv_pw, m_w_pool_group, m_pool_scale, m_w_out, m_g_norm2, m_w_ffn_gate, m_w_ffn_up, m_w_ffn_down, m_g_final, v_w_ada, v_b_ada, v_g_norm1, v_w_in, v_dw_w, v_dw_b, v_conv_ln_g, v_conv_ln_b, v_w_conv_pw, v_w_pool_group, v_pool_scale, v_w_out, v_g_norm2, v_w_ffn_gate, v_w_ffn_up, v_w_ffn_down, v_g_final):
    given = dict(x=x, c=c, w_ada=w_ada, b_ada=b_ada, g_norm1=g_norm1, w_in=w_in, dw_w=dw_w, dw_b=dw_b, conv_ln_g=conv_ln_g, conv_ln_b=conv_ln_b, w_conv_pw=w_conv_pw, w_pool_group=w_pool_group, pool_scale=pool_scale, w_out=w_out, g_norm2=g_norm2, w_ffn_gate=w_ffn_gate, w_ffn_up=w_ffn_up, w_ffn_down=w_ffn_down, g_final=g_final, loss_target=loss_target, m_w_ada=m_w_ada, m_b_ada=m_b_ada, m_g_norm1=m_g_norm1, m_w_in=m_w_in, m_dw_w=m_dw_w, m_dw_b=m_dw_b, m_conv_ln_g=m_conv_ln_g, m_conv_ln_b=m_conv_ln_b, m_w_conv_pw=m_w_conv_pw, m_w_pool_group=m_w_pool_group, m_pool_scale=m_pool_scale, m_w_out=m_w_out, m_g_norm2=m_g_norm2, m_w_ffn_gate=m_w_ffn_gate, m_w_ffn_up=m_w_ffn_up, m_w_ffn_down=m_w_ffn_down, m_g_final=m_g_final, v_w_ada=v_w_ada, v_b_ada=v_b_ada, v_g_norm1=v_g_norm1, v_w_in=v_w_in, v_dw_w=v_dw_w, v_dw_b=v_dw_b, v_conv_ln_g=v_conv_ln_g, v_conv_ln_b=v_conv_ln_b, v_w_conv_pw=v_w_conv_pw, v_w_pool_group=v_w_pool_group, v_pool_scale=v_pool_scale, v_w_out=v_w_out, v_g_norm2=v_g_norm2, v_w_ffn_gate=v_w_ffn_gate, v_w_ffn_up=v_w_ffn_up, v_w_ffn_down=v_w_ffn_down, v_g_final=v_g_final)
    weights = {n: given[n] for n in TWIN_WEIGHTS}
    shared = {n: given[n] for n in SHARED_INPUTS}
    per_example = {n: given[n] for n in ['x', 'c']}
    grad_fn = _jax.value_and_grad(_loss, argnums=(0, 1))

    def one_microbatch(ex, loss_target):
        ex = dict(ex)
        diff = ex.pop(TWIN_DIFF_INPUT)
        return grad_fn(weights, diff, {**shared, **ex}, loss_target)

    if N_MICROBATCH == 1:
        loss, (grad_w, grad_x) = one_microbatch(per_example, given["loss_target"])
    else:
        def body(carry, xs):
            loss_sum, grad_sum = carry
            l_k, (gw_k, gx_k) = one_microbatch(xs[0], xs[1])
            with _jax.named_scope("update"):
                return (loss_sum + l_k, _jax.tree.map(_jnp.add, grad_sum, gw_k)), gx_k

        init = (_jnp.zeros((), _jnp.float32), _jax.tree.map(_jnp.zeros_like, weights))
        (loss, grad_w), grad_x = _jax.lax.scan(body, init, (per_example, given["loss_target"]))
    with _jax.named_scope("update"):
        delta_w, new_m, new_v = {}, {}, {}
        for n in TWIN_WEIGHTS:
            delta_w[n], new_m[n], new_v[n] = _adamw(weights[n], grad_w[n], given["m_" + n], given["v_" + n])
    return (loss, grad_x, *[grad_w[n] for n in TWIN_WEIGHTS], *[delta_w[n] for n in TWIN_WEIGHTS],
            *[new_m[n] for n in TWIN_WEIGHTS], *[new_v[n] for n in TWIN_WEIGHTS])
```

```python
import functools

import jax
import jax.numpy as jnp
from jax import lax
from jax.experimental import pallas as pl
from jax.experimental.pallas import tpu as pltpu

F32 = jnp.float32
MXU_DTYPE = jnp.bfloat16
EPS = 1e-6

D_MODEL = 1024
CONV_W = 512
POOL_W = 512
CONV_K = 31
POOL_WINDOWS = (2, 4, 8, 16)
POOL_G = 128
IN_W = 2 * CONV_W + POOL_W
N_CHIP = 4
N_DEV = 8
CONV_HALO = 32
POOL_HALO = 16

ADAM_LR = 0.001
ADAM_B1 = 0.9
ADAM_B2 = 0.999
ADAM_EPS = 1e-08
ADAM_WD = 0.01
ADAM_STEP = 10

MESH = pl.DeviceIdType.MESH
ANY = pl.BlockSpec(memory_space=pl.ANY)
VMEM = pl.BlockSpec(memory_space=pltpu.VMEM)


def _dot(a, b):
    return jnp.dot(a.astype(MXU_DTYPE), b.astype(MXU_DTYPE), preferred_element_type=F32)


def _dot_nt(a, b):
    return lax.dot_general(a.astype(MXU_DTYPE), b.astype(MXU_DTYPE), (((1,), (1,)), ((), ())),
                           preferred_element_type=F32)


def _dot_tn(a, b):
    return lax.dot_general(a.astype(MXU_DTYPE), b.astype(MXU_DTYPE), (((0,), (0,)), ((), ())),
                           preferred_element_type=F32)


def _sigmoid(v):
    return 1.0 / (1.0 + jnp.exp(-v))


def _full(shape):
    n = len(shape)
    return pl.BlockSpec(shape, lambda *_: (0,) * n)


def _token_tile(s):
    return 256 if s % 256 == 0 else s


def _cast_weights(arrs):
    def body(*refs):
        n = len(refs) // 2
        for i in range(n):
            refs[n + i][...] = refs[i][...].astype(MXU_DTYPE)

    return pl.pallas_call(
        body, name="cast_weights",
        out_shape=[jax.ShapeDtypeStruct(a.shape, MXU_DTYPE) for a in arrs],
        in_specs=[VMEM] * len(arrs), out_specs=[VMEM] * len(arrs),
    )(*arrs)


def _mixer_fwd(x, mod, g1, w_in, dww, dwb, lng, lnb, w_pw, wg, pscale, w_out):
    s = x.shape[0]
    ts = _token_tile(s)
    nt = s // ts

    def body(x_ref, mod_ref, g1_ref, win_ref, dww_ref, dwb_ref, lng_ref, lnb_ref, wpw_ref, wg_ref, ps_ref,
             wout_ref, x2_ref, y_ref, u_ref, z_ref, rstd_ref, p_ref, ycat_ref, gpad, vpad):
        i = pl.program_id(0)

        @pl.when(i == 0)
        def _():
            gpad[0:CONV_HALO, :] = jnp.zeros((CONV_HALO, CONV_W), F32)
            vpad[0:POOL_HALO, :] = jnp.zeros((POOL_HALO, POOL_W), F32)

        xt = x_ref[...]
        sh1 = mod_ref[0:1, :]
        sc1 = mod_ref[1:2, :]
        gt1 = mod_ref[2:3, :]
        r1 = lax.rsqrt(jnp.mean(xt * xt, axis=-1, keepdims=True) + EPS)
        h1 = (xt * r1 * g1_ref[...]) * (1.0 + sc1) + sh1
        h1b = h1.astype(MXU_DTYPE)
        u = jnp.concatenate([_dot(h1b, win_ref[j]) for j in range(N_CHIP)], axis=1)
        u_ref[...] = u
        a = u[:, :CONV_W]
        g = u[:, CONV_W:2 * CONV_W]
        v = u[:, 2 * CONV_W:]

        gpad[CONV_HALO:CONV_HALO + ts, :] = a * _sigmoid(g)
        cv = jnp.broadcast_to(dwb_ref[...], (ts, CONV_W))
        off = CONV_HALO - (CONV_K - 1)
        for k in range(CONV_K):
            cv = cv + dww_ref[k:k + 1, :] * gpad[off + k:off + k + ts, :]
        gpad[0:CONV_HALO, :] = gpad[ts:ts + CONV_HALO, :]

        mu = jnp.mean(cv, axis=-1, keepdims=True)
        cc = cv - mu
        rstd = lax.rsqrt(jnp.mean(cc * cc, axis=-1, keepdims=True) + EPS)
        z = cc * rstd
        z_ref[...] = z
        rstd_ref[...] = rstd
        ln = z * lng_ref[...] + lnb_ref[...]
        sw = ln * _sigmoid(ln)
        yconv = _dot(sw, wpw_ref[...])

        vpad[POOL_HALO:POOL_HALO + ts, :] = v
        t = i * ts + lax.broadcasted_iota(jnp.int32, (ts, 1), 0)
        ps, ypool = [], []
        for gi, w in enumerate(POOL_WINDOWS):
            cols = slice(gi * POOL_G, (gi + 1) * POOL_G)
            acc = vpad[POOL_HALO:POOL_HALO + ts, cols]
            for d in range(1, w):
                acc = acc + vpad[POOL_HALO - d:POOL_HALO - d + ts, cols]
            cnt = jnp.minimum(t + 1, w).astype(F32)
            pg = (acc / cnt - v[:, cols]).astype(MXU_DTYPE)
            ps.append(pg)
            ypool.append(_dot(pg, wg_ref[gi]))
        vpad[0:POOL_HALO, :] = vpad[ts:ts + POOL_HALO, :]
        p_ref[...] = jnp.concatenate(ps, axis=1)
        ypool = jnp.concatenate(ypool, axis=1) * ps_ref[...]

        ycat = jnp.concatenate([yconv, ypool], axis=1).astype(MXU_DTYPE)
        ycat_ref[...] = ycat
        y = _dot(ycat, wout_ref[...])
        y_ref[...] = y
        x2_ref[...] = xt + gt1 * y

    tile = lambda w: pl.BlockSpec((ts, w), lambda i: (i, 0))
    return pl.pallas_call(
        body, name="mixer_fwd", grid=(nt,),
        in_specs=[tile(D_MODEL), _full(mod.shape), _full(g1.shape), _full(w_in.shape), _full(dww.shape),
                  _full(dwb.shape), _full(lng.shape), _full(lnb.shape), _full(w_pw.shape), _full(wg.shape),
                  _full(pscale.shape), _full(w_out.shape)],
        out_specs=[tile(D_MODEL), tile(D_MODEL), tile(IN_W), tile(CONV_W), tile(1), tile(POOL_W), tile(D_MODEL)],
        out_shape=[jax.ShapeDtypeStruct((s, D_MODEL), F32), jax.ShapeDtypeStruct((s, D_MODEL), F32),
                   jax.ShapeDtypeStruct((s, IN_W), F32), jax.ShapeDtypeStruct((s, CONV_W), F32),
                   jax.ShapeDtypeStruct((s, 1), F32), jax.ShapeDtypeStruct((s, POOL_W), MXU_DTYPE),
                   jax.ShapeDtypeStruct((s, D_MODEL), MXU_DTYPE)],
        scratch_shapes=[pltpu.VMEM((ts + CONV_HALO, CONV_W), F32), pltpu.VMEM((ts + POOL_HALO, POOL_W), F32)],
        compiler_params=pltpu.CompilerParams(dimension_semantics=("arbitrary",)),
    )(x, mod, g1, w_in, dww, dwb, lng, lnb, w_pw, wg, pscale, w_out)


def _ffn(x2, tgt, mod, g2, gf, w_gate, w_up, w_down):
    s = x2.shape[0]
    ts = _token_tile(s)
    nt = s // ts
    fb = w_gate.shape[2]

    def body(x2_ref, tgt_ref, mod_ref, g2_ref, gf_ref, wgt_ref, wup_ref, wdn_ref,
             dx2_ref, h2_ref, df_ref, act_ref, dgg_ref, duu_ref, vec_ref, gg_s, uu_s):
        i = pl.program_id(0)

        @pl.when(i == 0)
        def _():
            vec_ref[...] = jnp.zeros(vec_ref.shape, F32)

        x2t = x2_ref[...]
        sh2 = mod_ref[3:4, :]
        sc2 = mod_ref[4:5, :]
        gt2 = mod_ref[5:6, :]
        g2v = g2_ref[...]
        gfv = gf_ref[...]
        r2 = lax.rsqrt(jnp.mean(x2t * x2t, axis=-1, keepdims=True) + EPS)
        xh2 = x2t * r2
        n2 = xh2 * g2v
        h2b = (n2 * (1.0 + sc2) + sh2).astype(MXU_DTYPE)
        h2_ref[...] = h2b
        f = jnp.zeros((ts, D_MODEL), F32)
        for j in range(N_CHIP):
            gg = _dot(h2b, wgt_ref[j])
            uu = _dot(h2b, wup_ref[j])
            gg_s[j] = gg
            uu_s[j] = uu
            actb = (gg * _sigmoid(gg) * uu).astype(MXU_DTYPE)
            act_ref[j] = actb
            f = f + _dot(actb, wdn_ref[j])
        x3 = x2t + gt2 * f
        r3 = lax.rsqrt(jnp.mean(x3 * x3, axis=-1, keepdims=True) + EPS)
        xh3 = x3 * r3
        diff = xh3 * gfv - tgt_ref[...]
        dout = diff * (1.0 / D_MODEL)
        dn3 = dout * gfv
        dx3 = r3 * (dn3 - xh3 * jnp.mean(dn3 * xh3, axis=-1, keepdims=True))
        dfb = (dx3 * gt2).astype(MXU_DTYPE)
        df_ref[...] = dfb
        dh2 = jnp.zeros((ts, D_MODEL), F32)
        for j in range(N_CHIP):
            dact = _dot_nt(dfb, wdn_ref[j])
            gg = gg_s[j]
            uu = uu_s[j]
            sg = _sigmoid(gg)
            duu = (dact * (gg * sg)).astype(MXU_DTYPE)
            dgg = (dact * uu * (sg * (1.0 + gg * (1.0 - sg)))).astype(MXU_DTYPE)
            duu_ref[j] = duu
            dgg_ref[j] = dgg
            dh2 = dh2 + _dot_nt(dgg, wgt_ref[j]) + _dot_nt(duu, wup_ref[j])
        dn2 = dh2 * (1.0 + sc2)
        dxh2 = dn2 * g2v
        dx2_ref[...] = dx3 + r2 * (dxh2 - xh2 * jnp.mean(dxh2 * xh2, axis=-1, keepdims=True))

        col = lambda a: jnp.sum(a, axis=0, keepdims=True)
        vec_ref[0:1, :] += col(dout * xh3)
        vec_ref[1:2, :] += col(dx3 * f)
        vec_ref[2:3, :] += col(dh2)
        vec_ref[3:4, :] += col(dh2 * n2)
        vec_ref[4:5, :] += col(dn2 * xh2)
        vec_ref[5:6, :] += col(diff * diff)

    tile = lambda w: pl.BlockSpec((ts, w), lambda i: (i, 0))
    tile3 = pl.BlockSpec((N_CHIP, ts, fb), lambda i: (0, i, 0))
    once = lambda a: pl.BlockSpec(a.shape, lambda i: (0,) * a.ndim, pipeline_mode=pl.Buffered(1))
    hid = jax.ShapeDtypeStruct((N_CHIP, s, fb), MXU_DTYPE)
    return pl.pallas_call(
        body, name="ffn", grid=(nt,),
        in_specs=[tile(D_MODEL), tile(D_MODEL), _full(mod.shape), _full(g2.shape), _full(gf.shape),
                  once(w_gate), once(w_up), once(w_down)],
        out_specs=[tile(D_MODEL), tile(D_MODEL), tile(D_MODEL), tile3, tile3, tile3, _full((8, D_MODEL))],
        out_shape=[jax.ShapeDtypeStruct((s, D_MODEL), F32), jax.ShapeDtypeStruct((s, D_MODEL), MXU_DTYPE),
                   jax.ShapeDtypeStruct((s, D_MODEL), MXU_DTYPE), hid, hid, hid,
                   jax.ShapeDtypeStruct((8, D_MODEL), F32)],
        scratch_shapes=[pltpu.VMEM((N_CHIP, ts, fb), F32), pltpu.VMEM((N_CHIP, ts, fb), F32)],
        compiler_params=pltpu.CompilerParams(dimension_semantics=("arbitrary",)),
    )(x2, tgt, mod, g2, gf, w_gate, w_up, w_down)


def _mixer_bwd(dx2, x, y, u, z, rstd, p, mod, g1, w_in, dww, lng, lnb, w_pw, wg, pscale, w_out):
    s = x.shape[0]
    ts = _token_tile(s)
    nt = s // ts

    def body(dx2_ref, x_ref, y_ref, u_ref, z_ref, rstd_ref, p_ref, mod_ref, g1_ref, win_ref, dww_ref, lng_ref,
             lnb_ref, wpw_ref, wg_ref, ps_ref, wout_ref,
             gx_ref, h1_ref, du_ref, dy_ref, sw_ref, dyc_ref, dyp_ref, vd_ref, vc_ref, ddw_ref, dcpad, dppad):
        i = pl.program_id(0)
        tix = nt - 1 - i

        @pl.when(i == 0)
        def _():
            vd_ref[...] = jnp.zeros(vd_ref.shape, F32)
            vc_ref[...] = jnp.zeros(vc_ref.shape, F32)
            ddw_ref[...] = jnp.zeros(ddw_ref.shape, F32)
            dcpad[ts:ts + CONV_HALO, :] = jnp.zeros((CONV_HALO, CONV_W), F32)
            dppad[ts:ts + POOL_HALO, :] = jnp.zeros((POOL_HALO, POOL_W), F32)

        col = lambda a: jnp.sum(a, axis=0, keepdims=True)
        sh1 = mod_ref[0:1, :]
        sc1 = mod_ref[1:2, :]
        gt1 = mod_ref[2:3, :]
        dx2t = dx2_ref[...]
        vd_ref[0:1, :] += col(dx2t * y_ref[...])
        dyb = (dx2t * gt1).astype(MXU_DTYPE)
        dy_ref[...] = dyb
        dycat = _dot_nt(dyb, wout_ref[...])
        dyconv = dycat[:, :CONV_W]
        dypool = dycat[:, CONV_W:]

        pt = p_ref[...]
        t = tix * ts + lax.broadcasted_iota(jnp.int32, (ts, 1), 0)
        psc = ps_ref[...]
        dypb = (dypool * psc).astype(MXU_DTYPE)
        dyp_ref[...] = dypb
        dps, ypre = [], []
        for gi, w in enumerate(POOL_WINDOWS):
            cols = slice(gi * POOL_G, (gi + 1) * POOL_G)
            ypre.append(_dot(pt[:, cols], wg_ref[gi]))
            dpg = _dot_nt(dypb[:, cols], wg_ref[gi])
            dps.append(dpg)
            cnt = jnp.minimum(t + 1, w).astype(F32)
            dppad[0:ts, cols] = dpg / cnt
        vc_ref[0:1, :] += col(dypool * jnp.concatenate(ypre, axis=1))
        dvs = []
        for gi, w in enumerate(POOL_WINDOWS):
            cols = slice(gi * POOL_G, (gi + 1) * POOL_G)
            acc = dppad[0:ts, cols]
            for d in range(1, w):
                acc = acc + dppad[d:d + ts, cols]
            dvs.append(acc - dps[gi])
        dv = jnp.concatenate(dvs, axis=1)
        dppad[ts:ts + POOL_HALO, :] = dppad[0:POOL_HALO, :]

        zt = z_ref[...]
        lngv = lng_ref[...]
        ln = zt * lngv + lnb_ref[...]
        sg = _sigmoid(ln)
        swb = (ln * sg).astype(MXU_DTYPE)
        sw_ref[...] = swb
        dycb = dyconv.astype(MXU_DTYPE)
        dyc_ref[...] = dycb
        dln = _dot_nt(dycb, wpw_ref[...]) * (sg * (1.0 + ln * (1.0 - sg)))
        vc_ref[1:2, :] += col(dln * zt)
        vc_ref[2:3, :] += col(dln)
        dz = dln * lngv
        dcv = rstd_ref[...] * (dz - jnp.mean(dz, axis=-1, keepdims=True)
                               - zt * jnp.mean(dz * zt, axis=-1, keepdims=True))
        vc_ref[3:4, :] += col(dcv)
        dcpad[0:ts, :] = dcv
        ut = u_ref[...]
        a = ut[:, :CONV_W]
        g = ut[:, CONV_W:2 * CONV_W]
        sgg = _sigmoid(g)
        glu = a * sgg
        dglu = jnp.zeros((ts, CONV_W), F32)
        for k in range(CONV_K):
            sh = dcpad[CONV_K - 1 - k:CONV_K - 1 - k + ts, :]
            dglu = dglu + dww_ref[k:k + 1, :] * sh
            ddw_ref[k:k + 1, :] += col(glu * sh)
        dcpad[ts:ts + CONV_HALO, :] = dcpad[0:CONV_HALO, :]
        da = dglu * sgg
        dg = dglu * a * sgg * (1.0 - sgg)
        dub = jnp.concatenate([da, dg, dv], axis=1).astype(MXU_DTYPE)
        du_ref[...] = dub
        cw = IN_W // N_CHIP
        dh1 = jnp.zeros((ts, D_MODEL), F32)
        for j in range(N_CHIP):
            dh1 = dh1 + _dot_nt(dub[:, j * cw:(j + 1) * cw], win_ref[j])

        xt = x_ref[...]
        g1v = g1_ref[...]
        r1 = lax.rsqrt(jnp.mean(xt * xt, axis=-1, keepdims=True) + EPS)
        xh1 = xt * r1
        n1 = xh1 * g1v
        h1_ref[...] = (n1 * (1.0 + sc1) + sh1).astype(MXU_DTYPE)
        vd_ref[1:2, :] += col(dh1)
        vd_ref[2:3, :] += col(dh1 * n1)
        dn1 = dh1 * (1.0 + sc1)
        vd_ref[3:4, :] += col(dn1 * xh1)
        dxh = dn1 * g1v
        gx_ref[...] = dx2t + r1 * (dxh - xh1 * jnp.mean(dxh * xh1, axis=-1, keepdims=True))

    tile = lambda w: pl.BlockSpec((ts, w), lambda i: (nt - 1 - i, 0))
    bf = lambda w: jax.ShapeDtypeStruct((s, w), MXU_DTYPE)
    return pl.pallas_call(
        body, name="mixer_bwd", grid=(nt,),
        in_specs=[tile(D_MODEL), tile(D_MODEL), tile(D_MODEL), tile(IN_W), tile(CONV_W), tile(1), tile(POOL_W),
                  _full(mod.shape), _full(g1.shape), _full(w_in.shape), _full(dww.shape), _full(lng.shape),
                  _full(lnb.shape), _full(w_pw.shape), _full(wg.shape), _full(pscale.shape), _full(w_out.shape)],
        out_specs=[tile(D_MODEL), tile(D_MODEL), tile(IN_W), tile(D_MODEL), tile(CONV_W), tile(CONV_W),
                   tile(POOL_W), _full((8, D_MODEL)), _full((8, CONV_W)), _full((32, CONV_W))],
        out_shape=[jax.ShapeDtypeStruct((s, D_MODEL), F32), bf(D_MODEL), bf(IN_W), bf(D_MODEL), bf(CONV_W),
                   bf(CONV_W), bf(POOL_W), jax.ShapeDtypeStruct((8, D_MODEL), F32),
                   jax.ShapeDtypeStruct((8, CONV_W), F32), jax.ShapeDtypeStruct((32, CONV_W), F32)],
        scratch_shapes=[pltpu.VMEM((ts + CONV_HALO, CONV_W), F32), pltpu.VMEM((ts + POOL_HALO, POOL_W), F32)],
        compiler_params=pltpu.CompilerParams(dimension_semantics=("arbitrary",)),
    )(dx2, x, y, u, z, rstd, p, mod, g1, w_in, dww, lng, lnb, w_pw, wg, pscale, w_out)


def _dw(name, a, a_spec, b, b_spec, nb, mb, nbk):
    def body(a_ref, b_ref, o_ref):
        av = a_ref[...]
        bv = b_ref[...]
        av = av.reshape(av.shape[-2:])
        bv = bv.reshape(bv.shape[-2:])
        o_ref[0] = _dot_tn(av, bv)

    return pl.pallas_call(
        body, name=name, grid=(nb,),
        in_specs=[a_spec, b_spec],
        out_specs=pl.BlockSpec((1, mb, nbk), lambda j: (j, 0, 0)),
        out_shape=jax.ShapeDtypeStruct((nb, mb, nbk), F32),
        compiler_params=pltpu.CompilerParams(dimension_semantics=("arbitrary",)),
    )(a, b)


def _sample_grads(x, tgt, mod, g1, g2, gf, w_in, dww, dwb, lng, lnb, w_pw, wg, pscale, w_out, w_gate, w_up, w_down):
    s = x.shape[0]
    fb = w_gate.shape[2]
    x2, y, u, z, rstd, p, ycat = _mixer_fwd(x, mod, g1, w_in, dww, dwb, lng, lnb, w_pw, wg, pscale, w_out)
    dx2, h2, df, act, dgg, duu, vec_f = _ffn(x2, tgt, mod, g2, gf, w_gate, w_up, w_down)
    whole = lambda w: pl.BlockSpec((s, w), lambda j: (0, 0))
    cols = lambda w: pl.BlockSpec((s, w), lambda j: (0, j))
    hid = pl.BlockSpec((1, s, fb), lambda j: (j, 0, 0))
    g_gate = _dw("dw_gate", h2, whole(D_MODEL), dgg, hid, N_CHIP, D_MODEL, fb)
    g_up = _dw("dw_up", h2, whole(D_MODEL), duu, hid, N_CHIP, D_MODEL, fb)
    g_down = _dw("dw_down", act, hid, df, whole(D_MODEL), N_CHIP, fb, D_MODEL)
    gx, h1, du, dy, sw, dyc, dyp, vec_d, vec_c, ddw = _mixer_bwd(
        dx2, x, y, u, z, rstd, p, mod, g1, w_in, dww, lng, lnb, w_pw, wg, pscale, w_out)
    g_out = _dw("dw_out", ycat, cols(D_MODEL // N_CHIP), dy, whole(D_MODEL), N_CHIP, D_MODEL // N_CHIP, D_MODEL)
    g_pw = _dw("dw_pw", sw, cols(CONV_W // N_CHIP), dyc, whole(CONV_W), N_CHIP, CONV_W // N_CHIP, CONV_W)
    g_wg = _dw("dw_wg", p, cols(POOL_G), dyp, cols(POOL_G), len(POOL_WINDOWS), POOL_G, POOL_G)
    g_in = _dw("dw_in", h1, whole(D_MODEL), du, cols(IN_W // N_CHIP), N_CHIP, D_MODEL, IN_W // N_CHIP)
    return gx, (g_in, g_pw, g_out, g_gate, g_up, g_down), (vec_f, vec_d, vec_c, ddw, g_wg)


def _place():
    return lax.axis_index("x"), lax.axis_index("y"), lax.axis_index("c")


def _flip(x, y, r):
    return ((1 - x) if r & 2 else x, (1 - y) if r & 1 else y)


def _remote(src, dst, send_sem, recv_sem, dev):
    return pltpu.make_async_remote_copy(src_ref=src, dst_ref=dst, send_sem=send_sem, recv_sem=recv_sem,
                                        device_id=dev, device_id_type=MESH)


def _ada_fwd(c, w_ada, b4):
    nc = w_ada.shape[1]

    def body(c_ref, w_ref, b4_ref, mod_ref, cact_ref, call, part, parts, send1, recv1, send2, recv2):
        x, y, cc = _place()
        b = 4 * x + 2 * y + cc
        j = 2 * x + y
        call[b] = c_ref[...]
        sends = []
        for r in range(1, N_DEV):
            dev = ((1 - x) if r & 4 else x, (1 - y) if r & 2 else y, (1 - cc) if r & 1 else cc)
            cp = _remote(call.at[b], call.at[b], send1.at[r - 1], recv1.at[r - 1], dev)
            cp.start()
            sends.append(cp)
        for r in range(1, N_DEV):
            src_b = lax.bitwise_xor(b, r)
            _remote(call.at[src_b], call.at[src_b], send1.at[r - 1], recv1.at[r - 1], (x, y, cc)).wait_recv()
        for cp in sends:
            cp.wait_send()
        for i in range(N_DEV):
            ci = call[i]
            cact_ref[i:i + 1, :] = ci * _sigmoid(ci)
        part[...] = jnp.dot(cact_ref[...], w_ref[...], preferred_element_type=F32, precision=lax.Precision.HIGHEST)
        sends = []
        for r in range(1, N_CHIP):
            kx, ky = _flip(x, y, r)
            cp = _remote(part, parts.at[j], send2.at[r - 1], recv2.at[r - 1], (kx, ky, cc))
            cp.start()
            sends.append(cp)
        parts[j] = part[...]
        for r in range(1, N_CHIP):
            kx, ky = _flip(x, y, r)
            kj = 2 * kx + ky
            _remote(part, parts.at[kj], send2.at[r - 1], recv2.at[r - 1], (x, y, cc)).wait_recv()
        for cp in sends:
            cp.wait_send()
        mine = lax.broadcasted_iota(jnp.int32, (N_DEV, 1), 0) == b
        for k in range(N_CHIP):
            row = jnp.sum(jnp.where(mine, parts[k], 0.0), axis=0, keepdims=True)
            mod_ref[k:k + 1, :] = row + b4_ref[k:k + 1, :]

    return pl.pallas_call(
        body, name="ada_fwd",
        out_shape=[jax.ShapeDtypeStruct((N_CHIP, nc), F32), jax.ShapeDtypeStruct((N_DEV, D_MODEL), F32)],
        in_specs=[VMEM, VMEM, VMEM], out_specs=[VMEM, VMEM],
        scratch_shapes=[pltpu.VMEM((N_DEV, 1, D_MODEL), F32), pltpu.VMEM((N_DEV, nc), F32),
                        pltpu.VMEM((N_CHIP, N_DEV, nc), F32),
                        pltpu.SemaphoreType.DMA((N_DEV - 1,)), pltpu.SemaphoreType.DMA((N_DEV - 1,)),
                        pltpu.SemaphoreType.DMA((N_CHIP - 1,)), pltpu.SemaphoreType.DMA((N_CHIP - 1,))],
    )(c, w_ada, b4)


def _gather_engine(items, send, recv, lsem):
    x, y, cc = _place()
    j = 2 * x + y
    sib = (x, y, 1 - cc)
    started, locals_ = [], []
    for a, it in enumerate(items):
        lc = pltpu.make_async_copy(it["local"][0], it["local"][1], lsem.at[a])
        lc.start()
        locals_.append(lc)
        if it["sibling"]:
            cp = _remote(it["src"], it["dst"](j, cc), send.at[a, 0], recv.at[a, 0], sib)
            cp.start()
            started.append(cp)
        for r in range(1, N_CHIP):
            kx, ky = _flip(x, y, r)
            cp = _remote(it["src"], it["dst"](j, cc), send.at[a, r], recv.at[a, r], (kx, ky, cc))
            cp.start()
            started.append(cp)
    for a, it in enumerate(items):
        for r in range(1, N_CHIP):
            kx, ky = _flip(x, y, r)
            got = it["dst"](2 * kx + ky, cc)
            _remote(got, got, send.at[a, r], recv.at[a, r], sib).wait_recv()
            if it["forward"]:
                cp = _remote(got, got, send.at[a, 3 + r], recv.at[a, 3 + r], sib)
                cp.start()
                started.append(cp)
    for a, it in enumerate(items):
        if it["sibling"]:
            got = it["dst"](j, 1 - cc)
            _remote(got, got, send.at[a, 0], recv.at[a, 0], sib).wait_recv()
        if it["forward"]:
            for r in range(1, N_CHIP):
                kx, ky = _flip(x, y, r)
                got = it["dst"](2 * kx + ky, 1 - cc)
                _remote(got, got, send.at[a, 3 + r], recv.at[a, 3 + r], sib).wait_recv()
    for cp in started:
        cp.wait_send()
    for lc in locals_:
        lc.wait()


def _gather_weights(shards, dww):
    n = len(shards)

    def body(*refs):
        ins, outs = refs[:n + 1], refs[n + 1:2 * n + 2]
        send, recv, lsem = refs[2 * n + 2:]
        x, y, cc = _place()
        j = 2 * x + y
        items = []
        for a in range(n):
            h = shards[a].shape[0] // 2
            dst = functools.partial(lambda o, h, kj, pc: o.at[kj, pl.ds(pc * h, h), :], outs[a], h)
            items.append(dict(src=ins[a].at[pl.ds(cc * h, h), :], dst=dst, local=(ins[a], outs[a].at[j]),
                              sibling=False, forward=True))
        items.append(dict(src=ins[n], dst=functools.partial(lambda o, kj, pc: o.at[kj], outs[n]),
                          local=(ins[n], outs[n].at[j]), sibling=False, forward=False))
        _gather_engine(items, send, recv, lsem)

    arrs = list(shards) + [dww]
    return pl.pallas_call(
        body, name="gather_weights",
        out_shape=[jax.ShapeDtypeStruct((N_CHIP,) + a.shape, a.dtype) for a in arrs],
        in_specs=[ANY] * (n + 1), out_specs=[ANY] * (n + 1),
        scratch_shapes=[pltpu.SemaphoreType.DMA((n + 1, 7)), pltpu.SemaphoreType.DMA((n + 1, 7)),
                        pltpu.SemaphoreType.DMA((n + 1,))],
    )(*arrs)


def _gather_small(arrs):
    n = len(arrs)

    def body(*refs):
        ins, outs = refs[:n], refs[n:2 * n]
        send, recv, lsem = refs[2 * n:]
        x, y, cc = _place()
        items = []
        for a in range(n):
            dst = functools.partial(lambda o, kj, pc: o.at[2 * kj + pc], outs[a])
            items.append(dict(src=ins[a], dst=dst, local=(ins[a], outs[a].at[4 * x + 2 * y + cc]),
                              sibling=True, forward=True))
        _gather_engine(items, send, recv, lsem)

    return pl.pallas_call(
        body, name="gather_small",
        out_shape=[jax.ShapeDtypeStruct((N_DEV,) + a.shape, a.dtype) for a in arrs],
        in_specs=[ANY] * n, out_specs=[ANY] * n,
        scratch_shapes=[pltpu.SemaphoreType.DMA((n, 7)), pltpu.SemaphoreType.DMA((n, 7)),
                        pltpu.SemaphoreType.DMA((n,))],
    )(*arrs)


def _sibling_halves(gs):
    n = len(gs)

    def body(*refs):
        ins, outs = refs[:n], refs[n:2 * n]
        send, recv = refs[2 * n:]
        x, y, cc = _place()
        cps = []
        for a in range(n):
            h = gs[a].shape[1] // 2
            cp = _remote(ins[a].at[:, pl.ds((1 - cc) * h, h), :], outs[a], send.at[a], recv.at[a], (x, y, 1 - cc))
            cp.start()
            cps.append(cp)
        for cp in cps:
            cp.wait_recv()
        for cp in cps:
            cp.wait_send()

    return pl.pallas_call(
        body, name="sibling_halves",
        out_shape=[jax.ShapeDtypeStruct((N_CHIP, g.shape[1] // 2, g.shape[2]), F32) for g in gs],
        in_specs=[ANY] * n, out_specs=[ANY] * n,
        scratch_shapes=[pltpu.SemaphoreType.DMA((n,)), pltpu.SemaphoreType.DMA((n,))],
    )(*gs)


def _chip_partials(place, gs, rs):
    n = len(gs)

    def body(pref, *refs):
        g_refs, r_refs = refs[:n], refs[n:2 * n]
        pb_refs, own_refs = refs[2 * n:3 * n], refs[3 * n:]
        jj = pl.program_id(0)
        for a in range(n):
            sm = g_refs[a][0] + r_refs[a][0]
            pb_refs[a][0] = sm.astype(MXU_DTYPE)

            @pl.when(jj == pref[1])
            def _(a=a, sm=sm):
                own_refs[a][...] = sm

    halves = [(g.shape[1] // 2, g.shape[2]) for g in gs]
    in_specs = [pl.BlockSpec((1, h, w), lambda jj, pref: (jj, pref[0], 0)) for h, w in halves]
    in_specs += [pl.BlockSpec((1, h, w), lambda jj, pref: (jj, 0, 0)) for h, w in halves]
    out_specs = [pl.BlockSpec((1, h, w), lambda jj, pref: (jj, 0, 0)) for h, w in halves]
    out_specs += [pl.BlockSpec((h, w), lambda jj, pref: (0, 0)) for h, w in halves]
    out = pl.pallas_call(
        body, name="chip_partials",
        grid_spec=pltpu.PrefetchScalarGridSpec(num_scalar_prefetch=1, grid=(N_CHIP,), in_specs=in_specs,
                                               out_specs=out_specs),
        out_shape=[jax.ShapeDtypeStruct((N_CHIP, h, w), MXU_DTYPE) for h, w in halves]
        + [jax.ShapeDtypeStruct((h, w), F32) for h, w in halves],
        compiler_params=pltpu.CompilerParams(dimension_semantics=("arbitrary",)),
    )(place, *gs, *rs)
    return out[:n], out[n:]


def _exchange_partials(pbs):
    n = len(pbs)

    def body(*refs):
        ins, outs = refs[:n], refs[n:2 * n]
        send, recv = refs[2 * n:]
        x, y, cc = _place()
        cps = []
        for a in range(n):
            for r in range(1, N_CHIP):
                kx, ky = _flip(x, y, r)
                cp = _remote(ins[a].at[2 * kx + ky], outs[a].at[r - 1], send.at[a, r - 1], recv.at[a, r - 1],
                             (kx, ky, cc))
                cp.start()
                cps.append(cp)
        for cp in cps:
            cp.wait_recv()
        for cp in cps:
            cp.wait_send()

    return pl.pallas_call(
        body, name="exchange_partials",
        out_shape=[jax.ShapeDtypeStruct((N_CHIP - 1,) + p.shape[1:], p.dtype) for p in pbs],
        in_specs=[ANY] * n, out_specs=[ANY] * n,
        scratch_shapes=[pltpu.SemaphoreType.DMA((n, N_CHIP - 1)), pltpu.SemaphoreType.DMA((n, N_CHIP - 1))],
    )(*pbs)


def _sum_partials(owns, recvd):
    n = len(owns)

    def body(*refs):
        o_refs, r_refs, out_refs = refs[:n], refs[n:2 * n], refs[2 * n:]
        for a in range(n):
            acc = o_refs[a][...]
            for r in range(N_CHIP - 1):
                acc = acc + r_refs[a][r].astype(F32)
            out_refs[a][...] = acc

    return pl.pallas_call(
        body, name="sum_partials",
        out_shape=[jax.ShapeDtypeStruct(o.shape, F32) for o in owns],
        in_specs=[VMEM] * (2 * n), out_specs=[VMEM] * n,
    )(*owns, *recvd)


def _join_halves(halves):
    n = len(halves)

    def body(*refs):
        ins, outs = refs[:n], refs[n:2 * n]
        send, recv, lsem = refs[2 * n:]
        x, y, cc = _place()
        cps, lcs = [], []
        for a in range(n):
            h = halves[a].shape[0]
            mine = outs[a].at[pl.ds(cc * h, h), :]
            lc = pltpu.make_async_copy(ins[a], mine, lsem.at[a])
            lc.start()
            lcs.append(lc)
            cp = _remote(ins[a], mine, send.at[a], recv.at[a], (x, y, 1 - cc))
            cp.start()
            cps.append(cp)
        for a in range(n):
            h = halves[a].shape[0]
            other = outs[a].at[pl.ds((1 - cc) * h, h), :]
            _remote(ins[a], other, send.at[a], recv.at[a], (x, y, 1 - cc)).wait_recv()
        for cp in cps:
            cp.wait_send()
        for lc in lcs:
            lc.wait()

    return pl.pallas_call(
        body, name="join_halves",
        out_shape=[jax.ShapeDtypeStruct((2 * hf.shape[0], hf.shape[1]), F32) for hf in halves],
        in_specs=[ANY] * n, out_specs=[ANY] * n,
        scratch_shapes=[pltpu.SemaphoreType.DMA((n,)), pltpu.SemaphoreType.DMA((n,)), pltpu.SemaphoreType.DMA((n,))],
    )(*halves)


def _adamw_math(w, g, m, v):
    m = ADAM_B1 * m + (1.0 - ADAM_B1) * g
    v = ADAM_B2 * v + (1.0 - ADAM_B2) * (g * g)
    m_hat = m / (1.0 - ADAM_B1 ** ADAM_STEP)
    v_hat = v / (1.0 - ADAM_B2 ** ADAM_STEP)
    delta = -ADAM_LR * (m_hat / (jnp.sqrt(v_hat) + ADAM_EPS) + ADAM_WD * w)
    return delta, m, v


def _row_tile(rows):
    for t in (512, 352, 256, 128):
        if rows % t == 0:
            return t
    return rows


def _adamw(name, w, g, m, v):
    rows, cols = w.shape
    tr = _row_tile(rows)

    def body(w_ref, g_ref, m_ref, v_ref, d_ref, nm_ref, nv_ref):
        d_ref[...], nm_ref[...], nv_ref[...] = _adamw_math(w_ref[...], g_ref[...], m_ref[...], v_ref[...])

    spec = pl.BlockSpec((tr, cols), lambda i: (i, 0))
    return pl.pallas_call(
        body, name=name, grid=(rows // tr,),
        in_specs=[spec] * 4, out_specs=[spec] * 3,
        out_shape=[jax.ShapeDtypeStruct(w.shape, F32)] * 3,
        compiler_params=pltpu.CompilerParams(dimension_semantics=("arbitrary",)),
    )(w, g, m, v)


def _adamw_ada(place, cact, dmod, w, m, v):
    rows, cols = w.shape
    tr = _row_tile(rows)

    def body(pref, ca_ref, dm_ref, w_ref, m_ref, v_ref, g_ref, d_ref, nm_ref, nv_ref):
        g = lax.dot_general(ca_ref[...], dm_ref[...], (((0,), (0,)), ((), ())), preferred_element_type=F32,
                            precision=lax.Precision.HIGHEST)
        g_ref[...] = g
        d_ref[...], nm_ref[...], nv_ref[...] = _adamw_math(w_ref[...], g, m_ref[...], v_ref[...])

    spec = pl.BlockSpec((tr, cols), lambda i, pref: (i, 0))
    return pl.pallas_call(
        body, name="adamw_ada",
        grid_spec=pltpu.PrefetchScalarGridSpec(
            num_scalar_prefetch=1, grid=(rows // tr,),
            in_specs=[pl.BlockSpec((N_DEV, tr), lambda i, pref: (0, i)),
                      pl.BlockSpec((N_DEV, cols), lambda i, pref: (0, pref[1])), spec, spec, spec],
            out_specs=[spec] * 4),
        out_shape=[jax.ShapeDtypeStruct(w.shape, F32)] * 4,
        compiler_params=pltpu.CompilerParams(dimension_semantics=("arbitrary",)),
    )(place, cact, dmod, w, m, v)


def _adamw_small(place, vf_all, vd_all, vc_all, ddw_all, gwg_all, wmv):
    nw = len(wmv)
    flat = [a for t in wmv for a in t]

    def body(pref, vf_ref, vd_ref, vc_ref, ddw_ref, gwg_ref, *refs):
        w_refs = refs[:3 * nw]
        loss_ref, dmod_ref = refs[3 * nw], refs[3 * nw + 1]
        o_refs = refs[3 * nw + 2:]
        j = pref[1]

        def total(ref):
            acc = ref[0]
            for b in range(1, N_DEV):
                acc = acc + ref[b]
            return acc

        vf, vd, vc, ddw, gwg = total(vf_ref), total(vd_ref), total(vc_ref), total(ddw_ref), total(gwg_ref)
        loss_ref[...] = (0.5 / D_MODEL) * jnp.sum(vf[5:6, :], axis=1, keepdims=True)
        order = ((vd_ref, 1), (vd_ref, 2), (vd_ref, 0), (vf_ref, 2), (vf_ref, 3), (vf_ref, 1))
        for b in range(N_DEV):
            for q, (ref, row) in enumerate(order):
                dmod_ref[b:b + 1, q * D_MODEL:(q + 1) * D_MODEL] = ref[b, row:row + 1, :]
        dm = dmod_ref[...]
        g_bada = dm[0:1, :]
        for b in range(1, N_DEV):
            g_bada = g_bada + dm[b:b + 1, :]
        g_dww = jnp.zeros((32, POOL_G), F32)
        for k in range(N_CHIP):
            g_dww = g_dww + jnp.where(j == k, ddw[:, k * POOL_G:(k + 1) * POOL_G], 0.0)
        grads = [g_bada, vd[3:4, :], g_dww, vc[3:4, :], vc[1:2, :], vc[2:3, :], gwg, vc[0:1, :], vf[4:5, :],
                 vf[0:1, :]]
        for i, g in enumerate(grads):
            w_ref, m_ref, v_ref = w_refs[3 * i:3 * i + 3]
            d, nm, nv = _adamw_math(w_ref[...], g, m_ref[...], v_ref[...])
            o_refs[4 * i][...] = g
            o_refs[4 * i + 1][...] = d
            o_refs[4 * i + 2][...] = nm
            o_refs[4 * i + 3][...] = nv

    gathered = [vf_all, vd_all, vc_all, ddw_all, gwg_all]
    outs = [jax.ShapeDtypeStruct((1, 1), F32), jax.ShapeDtypeStruct((N_DEV, 6 * D_MODEL), F32)]
    for w, _, _ in wmv:
        outs += [jax.ShapeDtypeStruct(w.shape, F32)] * 4
    full = lambda a: pl.BlockSpec(a.shape, lambda i, pref: (0,) * a.ndim)
    res = pl.pallas_call(
        body, name="adamw_small",
        grid_spec=pltpu.PrefetchScalarGridSpec(
            num_scalar_prefetch=1, grid=(1,),
            in_specs=[full(a) for a in gathered + flat], out_specs=[full(o) for o in outs]),
        out_shape=outs,
        compiler_params=pltpu.CompilerParams(dimension_semantics=("arbitrary",)),
    )(place, *gathered, *flat)
    return res[0], res[1], [res[2 + 4 * i:6 + 4 * i] for i in range(nw)]


def kernel(x, c, w_ada, b_ada, g_norm1, w_in, dw_w, dw_b, conv_ln_g, conv_ln_b, w_conv_pw, w_pool_group, pool_scale, w_out, g_norm2, w_ffn_gate, w_ffn_up, w_ffn_down, g_final, loss_target, m_w_ada, m_b_ada, m_g_norm1, m_w_in, m_dw_w, m_dw_b, m_conv_ln_g, m_conv_ln_b, m_w_conv_pw, m_w_pool_group, m_pool_scale, m_w_out, m_g_norm2, m_w_ffn_gate, m_w_ffn_up, m_w_ffn_down, m_g_final, v_w_ada, v_b_ada, v_g_norm1, v_w_in, v_dw_w, v_dw_b, v_conv_ln_g, v_conv_ln_b, v_w_conv_pw, v_w_pool_group, v_pool_scale, v_w_out, v_g_norm2, v_w_ffn_gate, v_w_ffn_up, v_w_ffn_down, v_g_final):
    xi, yi, ci = _place()
    place = jnp.stack([ci, 2 * xi + yi]).astype(jnp.int32)
    n_ada = w_ada.shape[2]

    mod4, cact = _ada_fwd(c, w_ada[0], b_ada.reshape(N_CHIP, n_ada))
    mod = mod4.reshape(6, D_MODEL)

    big = [w_in[0], w_conv_pw[0], w_out[0], w_ffn_gate[0], w_ffn_up[0], w_ffn_down[0]]
    *shards, wg_b = _cast_weights(big + [w_pool_group[0]])
    win_g, wpw_g, wout_g, wgate_g, wup_g, wdown_g, dww_g = _gather_weights(shards, dw_w[0])
    dww_full = jnp.pad(jnp.concatenate([dww_g[k] for k in range(N_CHIP)], axis=1), ((0, 1), (0, 0)))

    gx, gs, (vec_f, vec_d, vec_c, ddw, g_wg) = _sample_grads(
        x[0], loss_target[0], mod, g_norm1, g_norm2, g_final.reshape(1, D_MODEL), win_g, dww_full, dw_b,
        conv_ln_g, conv_ln_b, wpw_g.reshape(CONV_W, CONV_W), wg_b, pool_scale, wout_g.reshape(D_MODEL, D_MODEL),
        wgate_g, wup_g, wdown_g)

    rs = _sibling_halves(gs)
    pbs, owns = _chip_partials(place, gs, rs)
    recvd = _exchange_partials(pbs)
    halves = _sum_partials(owns, recvd)
    g_in, g_pw, g_out, g_gate, g_up, g_down = _join_halves(halves)

    vf_all, vd_all, vc_all, ddw_all, gwg_all = _gather_small([vec_f, vec_d, vec_c, ddw, g_wg])
    pad_rows = lambda a: jnp.pad(a[0], ((0, 1), (0, 0)))
    row = lambda a: a.reshape(1, -1)
    small = [(b_ada, m_b_ada, v_b_ada), (g_norm1, m_g_norm1, v_g_norm1),
             (pad_rows(dw_w), pad_rows(m_dw_w), pad_rows(v_dw_w)), (dw_b, m_dw_b, v_dw_b),
             (conv_ln_g, m_conv_ln_g, v_conv_ln_g), (conv_ln_b, m_conv_ln_b, v_conv_ln_b),
             (w_pool_group[0], m_w_pool_group[0], v_w_pool_group[0]), (pool_scale, m_pool_scale, v_pool_scale),
             (g_norm2, m_g_norm2, v_g_norm2), (row(g_final), row(m_g_final), row(v_g_final))]
    loss, dmod, small_out = _adamw_small(place, vf_all, vd_all, vc_all, ddw_all, gwg_all, small)
    (o_bada, o_g1, o_dww, o_dwb, o_lng, o_lnb, o_wg, o_ps, o_g2, o_gf) = small_out
    o_dww = [a[:CONV_K] for a in o_dww]
    o_gf = [a.reshape(D_MODEL) for a in o_gf]
    lead = lambda outs: [a[None] for a in outs]

    o_ada = _adamw_ada(place, cact, dmod, w_ada[0], m_w_ada[0], v_w_ada[0])
    upd = lambda name, w, g, m, v: [g] + list(_adamw(name, w[0], g, m[0], v[0]))
    o_in = upd("adamw_in", w_in, g_in, m_w_in, v_w_in)
    o_pw = upd("adamw_pw", w_conv_pw, g_pw, m_w_conv_pw, v_w_conv_pw)
    o_out = upd("adamw_out", w_out, g_out, m_w_out, v_w_out)
    o_gate = upd("adamw_gate", w_ffn_gate, g_gate, m_w_ffn_gate, v_w_ffn_gate)
    o_up = upd("adamw_up", w_ffn_up, g_up, m_w_ffn_up, v_w_ffn_up)
    o_down = upd("adamw_down", w_ffn_down, g_down, m_w_ffn_down, v_w_ffn_down)

    per_weight = [lead(o_ada), o_bada, o_g1, lead(o_in), lead(o_dww), o_dwb, o_lng, o_lnb, lead(o_pw), lead(o_wg),
                  o_ps, lead(o_out), o_g2, lead(o_gate), lead(o_up), lead(o_down), o_gf]
    result = [loss.reshape(()), gx[None]]
    for kind in range(4):
        result += [o[kind] for o in per_weight]
    return tuple(result)
```

```python
import functools

import jax
import jax.numpy as jnp
from jax import lax
from jax.experimental import pallas as pl
from jax.experimental.pallas import tpu as pltpu

F32 = jnp.float32
MXU_DTYPE = jnp.bfloat16
EPS = 1e-6

D_MODEL = 1024
CONV_W = 512
POOL_W = 512
CONV_K = 31
POOL_WINDOWS = (2, 4, 8, 16)
POOL_G = 128
IN_W = 2 * CONV_W + POOL_W
N_CHIP = 4
N_DEV = 8
CONV_HALO = 32
POOL_HALO = 16

ADAM_LR = 0.001
ADAM_B1 = 0.9
ADAM_B2 = 0.999
ADAM_EPS = 1e-08
ADAM_WD = 0.01
ADAM_STEP = 10

MESH = pl.DeviceIdType.MESH
ANY = pl.BlockSpec(memory_space=pl.ANY)
VMEM = pl.BlockSpec(memory_space=pltpu.VMEM)


def _dot(a, b):
    return jnp.dot(a.astype(MXU_DTYPE), b.astype(MXU_DTYPE), preferred_element_type=F32)


def _dot_nt(a, b):
    return lax.dot_general(a.astype(MXU_DTYPE), b.astype(MXU_DTYPE), (((1,), (1,)), ((), ())),
                           preferred_element_type=F32)


def _dot_tn(a, b):
    return lax.dot_general(a.astype(MXU_DTYPE), b.astype(MXU_DTYPE), (((0,), (0,)), ((), ())),
                           preferred_element_type=F32)


def _sigmoid(v):
    return 1.0 / (1.0 + jnp.exp(-v))


def _full(shape):
    n = len(shape)
    return pl.BlockSpec(shape, lambda *_: (0,) * n)


def _token_tile(s):
    return 256 if s % 256 == 0 else s


def _cast_weights(place, shards, dww, wg):
    n = len(shards)

    def body(pref, *refs):
        ins, outs = refs[:n + 2], refs[n + 2:]
        for a in range(n):
            outs[a][0] = ins[a][...].astype(MXU_DTYPE)
        outs[n][0] = ins[n][...]
        outs[n + 1][...] = ins[n + 1][...].astype(MXU_DTYPE)

    full = lambda a: pl.BlockSpec(a.shape, lambda i, pref: (0,) * a.ndim)
    slot = lambda a: pl.BlockSpec((1,) + a.shape, lambda i, pref: (pref[1],) + (0,) * a.ndim)
    arrs = list(shards) + [dww, wg]
    return pl.pallas_call(
        body, name="cast_weights",
        grid_spec=pltpu.PrefetchScalarGridSpec(
            num_scalar_prefetch=1, grid=(1,), in_specs=[full(a) for a in arrs],
            out_specs=[slot(a) for a in arrs[:n + 1]] + [full(wg)]),
        out_shape=[jax.ShapeDtypeStruct((N_CHIP,) + a.shape, MXU_DTYPE) for a in shards]
        + [jax.ShapeDtypeStruct((N_CHIP,) + dww.shape, F32), jax.ShapeDtypeStruct(wg.shape, MXU_DTYPE)],
        compiler_params=pltpu.CompilerParams(dimension_semantics=("arbitrary",)),
    )(place, *arrs)


def _mixer_fwd(x, mod, g1, w_in, dww, dwb, lng, lnb, w_pw, wg, pscale, w_out):
    s = x.shape[0]
    ts = _token_tile(s)
    nt = s // ts

    def body(x_ref, mod_ref, g1_ref, win_ref, dww_ref, dwb_ref, lng_ref, lnb_ref, wpw_ref, wg_ref, ps_ref,
             wout_ref, x2_ref, y_ref, u_ref, z_ref, rstd_ref, p_ref, ycat_ref, gpad, vpad):
        i = pl.program_id(0)

        @pl.when(i == 0)
        def _():
            gpad[0:CONV_HALO, :] = jnp.zeros((CONV_HALO, CONV_W), F32)
            vpad[0:POOL_HALO, :] = jnp.zeros((POOL_HALO, POOL_W), F32)

        xt = x_ref[...]
        sh1 = mod_ref[0:1, :]
        sc1 = mod_ref[1:2, :]
        gt1 = mod_ref[2:3, :]
        r1 = lax.rsqrt(jnp.mean(xt * xt, axis=-1, keepdims=True) + EPS)
        h1 = (xt * r1 * g1_ref[...]) * (1.0 + sc1) + sh1
        h1b = h1.astype(MXU_DTYPE)
        u = jnp.concatenate([_dot(h1b, win_ref[j]) for j in range(N_CHIP)], axis=1)
        u_ref[...] = u
        a = u[:, :CONV_W]
        g = u[:, CONV_W:2 * CONV_W]
        v = u[:, 2 * CONV_W:]

        gpad[CONV_HALO:CONV_HALO + ts, :] = a * _sigmoid(g)
        cv = jnp.broadcast_to(dwb_ref[...], (ts, CONV_W))
        off = CONV_HALO - (CONV_K - 1)
        for k in range(CONV_K):
            cv = cv + dww_ref[k:k + 1, :] * gpad[off + k:off + k + ts, :]
        gpad[0:CONV_HALO, :] = gpad[ts:ts + CONV_HALO, :]

        mu = jnp.mean(cv, axis=-1, keepdims=True)
        cc = cv - mu
        rstd = lax.rsqrt(jnp.mean(cc * cc, axis=-1, keepdims=True) + EPS)
        z = cc * rstd
        z_ref[...] = z
        rstd_ref[...] = rstd
        ln = z * lng_ref[...] + lnb_ref[...]
        sw = ln * _sigmoid(ln)
        yconv = _dot(sw, wpw_ref[...])

        vpad[POOL_HALO:POOL_HALO + ts, :] = v
        t = i * ts + lax.broadcasted_iota(jnp.int32, (ts, 1), 0)
        ps, ypool = [], []
        for gi, w in enumerate(POOL_WINDOWS):
            cols = slice(gi * POOL_G, (gi + 1) * POOL_G)
            acc = vpad[POOL_HALO:POOL_HALO + ts, cols]
            for d in range(1, w):
                acc = acc + vpad[POOL_HALO - d:POOL_HALO - d + ts, cols]
            cnt = jnp.minimum(t + 1, w).astype(F32)
            pg = (acc / cnt - v[:, cols]).astype(MXU_DTYPE)
            ps.append(pg)
            ypool.append(_dot(pg, wg_ref[gi]))
        vpad[0:POOL_HALO, :] = vpad[ts:ts + POOL_HALO, :]
        p_ref[...] = jnp.concatenate(ps, axis=1)
        ypool = jnp.concatenate(ypool, axis=1) * ps_ref[...]

        ycat = jnp.concatenate([yconv, ypool], axis=1).astype(MXU_DTYPE)
        ycat_ref[...] = ycat
        y = _dot(ycat, wout_ref[...])
        y_ref[...] = y
        x2_ref[...] = xt + gt1 * y

    tile = lambda w: pl.BlockSpec((ts, w), lambda i: (i, 0))
    return pl.pallas_call(
        body, name="mixer_fwd", grid=(nt,),
        in_specs=[tile(D_MODEL), _full(mod.shape), _full(g1.shape), _full(w_in.shape), _full(dww.shape),
                  _full(dwb.shape), _full(lng.shape), _full(lnb.shape), _full(w_pw.shape), _full(wg.shape),
                  _full(pscale.shape), _full(w_out.shape)],
        out_specs=[tile(D_MODEL), tile(D_MODEL), tile(IN_W), tile(CONV_W), tile(1), tile(POOL_W), tile(D_MODEL)],
        out_shape=[jax.ShapeDtypeStruct((s, D_MODEL), F32), jax.ShapeDtypeStruct((s, D_MODEL), F32),
                   jax.ShapeDtypeStruct((s, IN_W), F32), jax.ShapeDtypeStruct((s, CONV_W), F32),
                   jax.ShapeDtypeStruct((s, 1), F32), jax.ShapeDtypeStruct((s, POOL_W), MXU_DTYPE),
                   jax.ShapeDtypeStruct((s, D_MODEL), MXU_DTYPE)],
        scratch_shapes=[pltpu.VMEM((ts + CONV_HALO, CONV_W), F32), pltpu.VMEM((ts + POOL_HALO, POOL_W), F32)],
        compiler_params=pltpu.CompilerParams(dimension_semantics=("arbitrary",)),
    )(x, mod, g1, w_in, dww, dwb, lng, lnb, w_pw, wg, pscale, w_out)


def _ffn(x2, tgt, mod, g2, gf, w_gate, w_up, w_down):
    s = x2.shape[0]
    ts = _token_tile(s)
    nt = s // ts
    fb = w_gate.shape[1]

    def body(x2_ref, tgt_ref, mod_ref, g2_ref, gf_ref, wgt_ref, wup_ref, wdn_ref,
             dx2_ref, h2_ref, df_ref, act_ref, dgg_ref, duu_ref, vec_ref, gg_s, uu_s):
        i = pl.program_id(0)

        @pl.when(i == 0)
        def _():
            vec_ref[...] = jnp.zeros(vec_ref.shape, F32)

        x2t = x2_ref[...]
        sh2 = mod_ref[3:4, :]
        sc2 = mod_ref[4:5, :]
        gt2 = mod_ref[5:6, :]
        g2v = g2_ref[...]
        gfv = gf_ref[...]
        r2 = lax.rsqrt(jnp.mean(x2t * x2t, axis=-1, keepdims=True) + EPS)
        xh2 = x2t * r2
        n2 = xh2 * g2v
        h2b = (n2 * (1.0 + sc2) + sh2).astype(MXU_DTYPE)
        h2_ref[...] = h2b
        f = jnp.zeros((ts, D_MODEL), F32)
        for j in range(N_CHIP):
            gg = _dot_nt(h2b, wgt_ref[j])
            uu = _dot_nt(h2b, wup_ref[j])
            gg_s[j] = gg
            uu_s[j] = uu
            actb = (gg * _sigmoid(gg) * uu).astype(MXU_DTYPE)
            act_ref[j] = actb
            f = f + _dot(actb, wdn_ref[j])
        x3 = x2t + gt2 * f
        r3 = lax.rsqrt(jnp.mean(x3 * x3, axis=-1, keepdims=True) + EPS)
        xh3 = x3 * r3
        diff = xh3 * gfv - tgt_ref[...]
        dout = diff * (1.0 / D_MODEL)
        dn3 = dout * gfv
        dx3 = r3 * (dn3 - xh3 * jnp.mean(dn3 * xh3, axis=-1, keepdims=True))
        dfb = (dx3 * gt2).astype(MXU_DTYPE)
        df_ref[...] = dfb
        dh2 = jnp.zeros((ts, D_MODEL), F32)
        for j in range(N_CHIP):
            dact = _dot_nt(dfb, wdn_ref[j])
            gg = gg_s[j]
            uu = uu_s[j]
            sg = _sigmoid(gg)
            duu = (dact * (gg * sg)).astype(MXU_DTYPE)
            dgg = (dact * uu * (sg * (1.0 + gg * (1.0 - sg)))).astype(MXU_DTYPE)
            duu_ref[j] = duu
            dgg_ref[j] = dgg
            dh2 = dh2 + _dot(dgg, wgt_ref[j]) + _dot(duu, wup_ref[j])
        dn2 = dh2 * (1.0 + sc2)
        dxh2 = dn2 * g2v
        dx2_ref[...] = dx3 + r2 * (dxh2 - xh2 * jnp.mean(dxh2 * xh2, axis=-1, keepdims=True))

        col = lambda a: jnp.sum(a, axis=0, keepdims=True)
        vec_ref[0:1, :] += col(dout * xh3)
        vec_ref[1:2, :] += col(dx3 * f)
        vec_ref[2:3, :] += col(dh2)
        vec_ref[3:4, :] += col(dh2 * n2)
        vec_ref[4:5, :] += col(dn2 * xh2)
        vec_ref[5:6, :] += col(diff * diff)

    tile = lambda w: pl.BlockSpec((ts, w), lambda i: (i, 0))
    tile3 = pl.BlockSpec((N_CHIP, ts, fb), lambda i: (0, i, 0))
    once = lambda a: pl.BlockSpec(a.shape, lambda i: (0,) * a.ndim, pipeline_mode=pl.Buffered(1))
    hid = jax.ShapeDtypeStruct((N_CHIP, s, fb), MXU_DTYPE)
    return pl.pallas_call(
        body, name="ffn", grid=(nt,),
        in_specs=[tile(D_MODEL), tile(D_MODEL), _full(mod.shape), _full(g2.shape), _full(gf.shape),
                  once(w_gate), once(w_up), once(w_down)],
        out_specs=[tile(D_MODEL), tile(D_MODEL), tile(D_MODEL), tile3, tile3, tile3, _full((8, D_MODEL))],
        out_shape=[jax.ShapeDtypeStruct((s, D_MODEL), F32), jax.ShapeDtypeStruct((s, D_MODEL), MXU_DTYPE),
                   jax.ShapeDtypeStruct((s, D_MODEL), MXU_DTYPE), hid, hid, hid,
                   jax.ShapeDtypeStruct((8, D_MODEL), F32)],
        scratch_shapes=[pltpu.VMEM((N_CHIP, ts, fb), F32), pltpu.VMEM((N_CHIP, ts, fb), F32)],
        compiler_params=pltpu.CompilerParams(dimension_semantics=("arbitrary",)),
    )(x2, tgt, mod, g2, gf, w_gate, w_up, w_down)


def _mixer_bwd(dx2, x, y, u, z, rstd, p, mod, g1, w_in, dww, lng, lnb, w_pw, wg, pscale, w_out):
    s = x.shape[0]
    ts = _token_tile(s)
    nt = s // ts

    def body(dx2_ref, x_ref, y_ref, u_ref, z_ref, rstd_ref, p_ref, mod_ref, g1_ref, win_ref, dww_ref, lng_ref,
             lnb_ref, wpw_ref, wg_ref, ps_ref, wout_ref,
             gx_ref, h1_ref, du_ref, dy_ref, sw_ref, dyc_ref, dyp_ref, vd_ref, vc_ref, ddw_ref, dcpad, dppad):
        i = pl.program_id(0)
        tix = nt - 1 - i

        @pl.when(i == 0)
        def _():
            vd_ref[...] = jnp.zeros(vd_ref.shape, F32)
            vc_ref[...] = jnp.zeros(vc_ref.shape, F32)
            ddw_ref[...] = jnp.zeros(ddw_ref.shape, F32)
            dcpad[ts:ts + CONV_HALO, :] = jnp.zeros((CONV_HALO, CONV_W), F32)
            dppad[ts:ts + POOL_HALO, :] = jnp.zeros((POOL_HALO, POOL_W), F32)

        col = lambda a: jnp.sum(a, axis=0, keepdims=True)
        sh1 = mod_ref[0:1, :]
        sc1 = mod_ref[1:2, :]
        gt1 = mod_ref[2:3, :]
        dx2t = dx2_ref[...]
        vd_ref[0:1, :] += col(dx2t * y_ref[...])
        dyb = (dx2t * gt1).astype(MXU_DTYPE)
        dy_ref[...] = dyb
        dycat = _dot_nt(dyb, wout_ref[...])
        dyconv = dycat[:, :CONV_W]
        dypool = dycat[:, CONV_W:]

        pt = p_ref[...]
        t = tix * ts + lax.broadcasted_iota(jnp.int32, (ts, 1), 0)
        psc = ps_ref[...]
        dypb = (dypool * psc).astype(MXU_DTYPE)
        dyp_ref[...] = dypb
        dps, ypre = [], []
        for gi, w in enumerate(POOL_WINDOWS):
            cols = slice(gi * POOL_G, (gi + 1) * POOL_G)
            ypre.append(_dot(pt[:, cols], wg_ref[gi]))
            dpg = _dot_nt(dypb[:, cols], wg_ref[gi])
            dps.append(dpg)
            cnt = jnp.minimum(t + 1, w).astype(F32)
            dppad[0:ts, cols] = dpg / cnt
        vc_ref[0:1, :] += col(dypool * jnp.concatenate(ypre, axis=1))
        dvs = []
        for gi, w in enumerate(POOL_WINDOWS):
            cols = slice(gi * POOL_G, (gi + 1) * POOL_G)
            acc = dppad[0:ts, cols]
            for d in range(1, w):
                acc = acc + dppad[d:d + ts, cols]
            dvs.append(acc - dps[gi])
        dv = jnp.concatenate(dvs, axis=1)
        dppad[ts:ts + POOL_HALO, :] = dppad[0:POOL_HALO, :]

        zt = z_ref[...]
        lngv = lng_ref[...]
        ln = zt * lngv + lnb_ref[...]
        sg = _sigmoid(ln)
        swb = (ln * sg).astype(MXU_DTYPE)
        sw_ref[...] = swb
        dycb = dyconv.astype(MXU_DTYPE)
        dyc_ref[...] = dycb
        dln = _dot_nt(dycb, wpw_ref[...]) * (sg * (1.0 + ln * (1.0 - sg)))
        vc_ref[1:2, :] += col(dln * zt)
        vc_ref[2:3, :] += col(dln)
        dz = dln * lngv
        dcv = rstd_ref[...] * (dz - jnp.mean(dz, axis=-1, keepdims=True)
                               - zt * jnp.mean(dz * zt, axis=-1, keepdims=True))
        vc_ref[3:4, :] += col(dcv)
        dcpad[0:ts, :] = dcv
        ut = u_ref[...]
        a = ut[:, :CONV_W]
        g = ut[:, CONV_W:2 * CONV_W]
        sgg = _sigmoid(g)
        glu = a * sgg
        dglu = jnp.zeros((ts, CONV_W), F32)
        for k in range(CONV_K):
            sh = dcpad[CONV_K - 1 - k:CONV_K - 1 - k + ts, :]
            dglu = dglu + dww_ref[k:k + 1, :] * sh
            ddw_ref[k:k + 1, :] += col(glu * sh)
        dcpad[ts:ts + CONV_HALO, :] = dcpad[0:CONV_HALO, :]
        da = dglu * sgg
        dg = dglu * a * sgg * (1.0 - sgg)
        dub = jnp.concatenate([da, dg, dv], axis=1).astype(MXU_DTYPE)
        du_ref[...] = dub
        cw = IN_W // N_CHIP
        dh1 = jnp.zeros((ts, D_MODEL), F32)
        for j in range(N_CHIP):
            dh1 = dh1 + _dot_nt(dub[:, j * cw:(j + 1) * cw], win_ref[j])

        xt = x_ref[...]
        g1v = g1_ref[...]
        r1 = lax.rsqrt(jnp.mean(xt * xt, axis=-1, keepdims=True) + EPS)
        xh1 = xt * r1
        n1 = xh1 * g1v
        h1_ref[...] = (n1 * (1.0 + sc1) + sh1).astype(MXU_DTYPE)
        vd_ref[1:2, :] += col(dh1)
        vd_ref[2:3, :] += col(dh1 * n1)
        dn1 = dh1 * (1.0 + sc1)
        vd_ref[3:4, :] += col(dn1 * xh1)
        dxh = dn1 * g1v
        gx_ref[...] = dx2t + r1 * (dxh - xh1 * jnp.mean(dxh * xh1, axis=-1, keepdims=True))

    tile = lambda w: pl.BlockSpec((ts, w), lambda i: (nt - 1 - i, 0))
    bf = lambda w: jax.ShapeDtypeStruct((s, w), MXU_DTYPE)
    return pl.pallas_call(
        body, name="mixer_bwd", grid=(nt,),
        in_specs=[tile(D_MODEL), tile(D_MODEL), tile(D_MODEL), tile(IN_W), tile(CONV_W), tile(1), tile(POOL_W),
                  _full(mod.shape), _full(g1.shape), _full(w_in.shape), _full(dww.shape), _full(lng.shape),
                  _full(lnb.shape), _full(w_pw.shape), _full(wg.shape), _full(pscale.shape), _full(w_out.shape)],
        out_specs=[tile(D_MODEL), tile(D_MODEL), tile(IN_W), tile(D_MODEL), tile(CONV_W), tile(CONV_W),
                   tile(POOL_W), _full((8, D_MODEL)), _full((8, CONV_W)), _full((32, CONV_W))],
        out_shape=[jax.ShapeDtypeStruct((s, D_MODEL), F32), bf(D_MODEL), bf(IN_W), bf(D_MODEL), bf(CONV_W),
                   bf(CONV_W), bf(POOL_W), jax.ShapeDtypeStruct((8, D_MODEL), F32),
                   jax.ShapeDtypeStruct((8, CONV_W), F32), jax.ShapeDtypeStruct((32, CONV_W), F32)],
        scratch_shapes=[pltpu.VMEM((ts + CONV_HALO, CONV_W), F32), pltpu.VMEM((ts + POOL_HALO, POOL_W), F32)],
        compiler_params=pltpu.CompilerParams(dimension_semantics=("arbitrary",)),
    )(dx2, x, y, u, z, rstd, p, mod, g1, w_in, dww, lng, lnb, w_pw, wg, pscale, w_out)


def _dw(name, a, a_spec, b, b_spec, nb, mb, nbk):
    def body(a_ref, b_ref, o_ref):
        av = a_ref[...]
        bv = b_ref[...]
        av = av.reshape(av.shape[-2:])
        bv = bv.reshape(bv.shape[-2:])
        o_ref[0] = _dot_tn(av, bv)

    return pl.pallas_call(
        body, name=name, grid=(nb,),
        in_specs=[a_spec, b_spec],
        out_specs=pl.BlockSpec((1, mb, nbk), lambda j: (j, 0, 0)),
        out_shape=jax.ShapeDtypeStruct((nb, mb, nbk), F32),
        compiler_params=pltpu.CompilerParams(dimension_semantics=("arbitrary",)),
    )(a, b)


def _sample_grads(x, tgt, mod, g1, g2, gf, w_in, dww, dwb, lng, lnb, w_pw, wg, pscale, w_out, w_gate, w_up, w_down):
    s = x.shape[0]
    fb = w_gate.shape[1]
    x2, y, u, z, rstd, p, ycat = _mixer_fwd(x, mod, g1, w_in, dww, dwb, lng, lnb, w_pw, wg, pscale, w_out)
    dx2, h2, df, act, dgg, duu, vec_f = _ffn(x2, tgt, mod, g2, gf, w_gate, w_up, w_down)
    whole = lambda w: pl.BlockSpec((s, w), lambda j: (0, 0))
    cols = lambda w: pl.BlockSpec((s, w), lambda j: (0, j))
    hid = pl.BlockSpec((1, s, fb), lambda j: (j, 0, 0))
    g_gate = _dw("dw_gate", dgg, hid, h2, whole(D_MODEL), N_CHIP, fb, D_MODEL)
    g_up = _dw("dw_up", duu, hid, h2, whole(D_MODEL), N_CHIP, fb, D_MODEL)
    g_down = _dw("dw_down", act, hid, df, whole(D_MODEL), N_CHIP, fb, D_MODEL)
    gx, h1, du, dy, sw, dyc, dyp, vec_d, vec_c, ddw = _mixer_bwd(
        dx2, x, y, u, z, rstd, p, mod, g1, w_in, dww, lng, lnb, w_pw, wg, pscale, w_out)
    g_out = _dw("dw_out", ycat, cols(D_MODEL // N_CHIP), dy, whole(D_MODEL), N_CHIP, D_MODEL // N_CHIP, D_MODEL)
    g_pw = _dw("dw_pw", sw, cols(CONV_W // N_CHIP), dyc, whole(CONV_W), N_CHIP, CONV_W // N_CHIP, CONV_W)
    g_wg = _dw("dw_wg", p, cols(POOL_G), dyp, cols(POOL_G), len(POOL_WINDOWS), POOL_G, POOL_G)
    g_in = _dw("dw_in", h1, whole(D_MODEL), du, cols(IN_W // N_CHIP), N_CHIP, D_MODEL, IN_W // N_CHIP)
    return gx, (g_in, g_pw, g_out, g_gate, g_up, g_down), (vec_f, vec_d, vec_c, ddw, g_wg)


def _place():
    return lax.axis_index("x"), lax.axis_index("y"), lax.axis_index("c")


def _flip(x, y, r):
    return ((1 - x) if r & 2 else x, (1 - y) if r & 1 else y)


def _remote(src, dst, send_sem, recv_sem, dev):
    return pltpu.make_async_remote_copy(src_ref=src, dst_ref=dst, send_sem=send_sem, recv_sem=recv_sem,
                                        device_id=dev, device_id_type=MESH)


def _ada_fwd(c, w_ada, b4):
    nc = w_ada.shape[1]

    def body(c_ref, w_ref, b4_ref, mod_ref, cact_ref, call, part, parts, send1, recv1, send2, recv2):
        x, y, cc = _place()
        b = 4 * x + 2 * y + cc
        j = 2 * x + y
        call[b] = c_ref[...]
        sends = []
        for r in range(1, N_DEV):
            dev = ((1 - x) if r & 4 else x, (1 - y) if r & 2 else y, (1 - cc) if r & 1 else cc)
            cp = _remote(call.at[b], call.at[b], send1.at[r - 1], recv1.at[r - 1], dev)
            cp.start()
            sends.append(cp)
        for r in range(1, N_DEV):
            src_b = lax.bitwise_xor(b, r)
            _remote(call.at[src_b], call.at[src_b], send1.at[r - 1], recv1.at[r - 1], (x, y, cc)).wait_recv()
        for cp in sends:
            cp.wait_send()
        for i in range(N_DEV):
            ci = call[i]
            cact_ref[i:i + 1, :] = ci * _sigmoid(ci)
        part[...] = jnp.dot(cact_ref[...], w_ref[...], preferred_element_type=F32, precision=lax.Precision.HIGHEST)
        sends = []
        for r in range(1, N_CHIP):
            kx, ky = _flip(x, y, r)
            cp = _remote(part, parts.at[j], send2.at[r - 1], recv2.at[r - 1], (kx, ky, cc))
            cp.start()
            sends.append(cp)
        parts[j] = part[...]
        for r in range(1, N_CHIP):
            kx, ky = _flip(x, y, r)
            kj = 2 * kx + ky
            _remote(part, parts.at[kj], send2.at[r - 1], recv2.at[r - 1], (x, y, cc)).wait_recv()
        for cp in sends:
            cp.wait_send()
        mine = lax.broadcasted_iota(jnp.int32, (N_DEV, 1), 0) == b
        for k in range(N_CHIP):
            row = jnp.sum(jnp.where(mine, parts[k], 0.0), axis=0, keepdims=True)
            mod_ref[k:k + 1, :] = row + b4_ref[k:k + 1, :]

    return pl.pallas_call(
        body, name="ada_fwd",
        out_shape=[jax.ShapeDtypeStruct((N_CHIP, nc), F32), jax.ShapeDtypeStruct((N_DEV, D_MODEL), F32)],
        in_specs=[VMEM, VMEM, VMEM], out_specs=[VMEM, VMEM],
        scratch_shapes=[pltpu.VMEM((N_DEV, 1, D_MODEL), F32), pltpu.VMEM((N_DEV, nc), F32),
                        pltpu.VMEM((N_CHIP, N_DEV, nc), F32),
                        pltpu.SemaphoreType.DMA((N_DEV - 1,)), pltpu.SemaphoreType.DMA((N_DEV - 1,)),
                        pltpu.SemaphoreType.DMA((N_CHIP - 1,)), pltpu.SemaphoreType.DMA((N_CHIP - 1,))],
    )(c, w_ada, b4)


def _gather_engine(items, send, recv, lsem):
    x, y, cc = _place()
    j = 2 * x + y
    sib = (x, y, 1 - cc)
    started, locals_ = [], []
    for a, it in enumerate(items):
        if it["local"] is not None:
            src, stage, dst = it["local"]
            lc = pltpu.make_async_copy(src, stage, lsem.at[a])
            lc.start()
            lc.wait()
            lc = pltpu.make_async_copy(stage, dst, lsem.at[a])
            lc.start()
            locals_.append(lc)
        if it["sibling"]:
            cp = _remote(it["src"], it["dst"](j, cc), send.at[a, 0], recv.at[a, 0], sib)
            cp.start()
            started.append(cp)
        for r in range(1, N_CHIP):
            kx, ky = _flip(x, y, r)
            cp = _remote(it["src"], it["dst"](j, cc), send.at[a, r], recv.at[a, r], (kx, ky, cc))
            cp.start()
            started.append(cp)
    for a, it in enumerate(items):
        for r in range(1, N_CHIP):
            kx, ky = _flip(x, y, r)
            got = it["dst"](2 * kx + ky, cc)
            _remote(got, got, send.at[a, r], recv.at[a, r], sib).wait_recv()
            if it["forward"]:
                cp = _remote(got, got, send.at[a, 3 + r], recv.at[a, 3 + r], sib)
                cp.start()
                started.append(cp)
    for a, it in enumerate(items):
        if it["sibling"]:
            got = it["dst"](j, 1 - cc)
            _remote(got, got, send.at[a, 0], recv.at[a, 0], sib).wait_recv()
        if it["forward"]:
            for r in range(1, N_CHIP):
                kx, ky = _flip(x, y, r)
                got = it["dst"](2 * kx + ky, 1 - cc)
                _remote(got, got, send.at[a, 3 + r], recv.at[a, 3 + r], sib).wait_recv()
    for cp in started:
        cp.wait_send()
    for lc in locals_:
        lc.wait()


def _gather_weights(bufs, dww):
    n = len(bufs)

    def body(*refs):
        outs = refs[n + 1:2 * n + 2]
        send, recv, lsem = refs[2 * n + 2:]
        x, y, cc = _place()
        j = 2 * x + y
        items = []
        for a in range(n):
            h = bufs[a].shape[1] // 2
            dst = functools.partial(lambda o, h, kj, pc: o.at[kj, pl.ds(pc * h, h), :], outs[a], h)
            items.append(dict(src=dst(j, cc), dst=dst, local=None, sibling=False, forward=True))
        whole = functools.partial(lambda o, kj, pc: o.at[kj], outs[n])
        items.append(dict(src=whole(j, cc), dst=whole, local=None, sibling=False, forward=False))
        _gather_engine(items, send, recv, lsem)

    arrs = list(bufs) + [dww]
    return pl.pallas_call(
        body, name="gather_weights",
        out_shape=[jax.ShapeDtypeStruct(a.shape, a.dtype) for a in arrs],
        in_specs=[ANY] * (n + 1), out_specs=[ANY] * (n + 1),
        input_output_aliases={i: i for i in range(n + 1)},
        scratch_shapes=[pltpu.SemaphoreType.DMA((n + 1, 7)), pltpu.SemaphoreType.DMA((n + 1, 7)),
                        pltpu.SemaphoreType.DMA((n + 1,))],
    )(*arrs)


def _gather_small(arrs):
    n = len(arrs)

    def body(*refs):
        ins, outs = refs[:n], refs[n:2 * n]
        stages = refs[2 * n:3 * n]
        send, recv, lsem = refs[3 * n:]
        x, y, cc = _place()
        items = []
        for a in range(n):
            dst = functools.partial(lambda o, kj, pc: o.at[2 * kj + pc], outs[a])
            items.append(dict(src=ins[a], dst=dst, local=(ins[a], stages[a], outs[a].at[4 * x + 2 * y + cc]),
                              sibling=True, forward=True))
        _gather_engine(items, send, recv, lsem)

    return pl.pallas_call(
        body, name="gather_small",
        out_shape=[jax.ShapeDtypeStruct((N_DEV,) + a.shape, a.dtype) for a in arrs],
        in_specs=[ANY] * n, out_specs=[ANY] * n,
        scratch_shapes=[pltpu.VMEM(a.shape, a.dtype) for a in arrs]
        + [pltpu.SemaphoreType.DMA((n, 7)), pltpu.SemaphoreType.DMA((n, 7)), pltpu.SemaphoreType.DMA((n,))],
    )(*arrs)


def _sibling_halves(gs):
    n = len(gs)

    def body(*refs):
        ins, outs = refs[:n], refs[n:2 * n]
        send, recv = refs[2 * n:]
        x, y, cc = _place()
        cps = []
        for a in range(n):
            h = gs[a].shape[1] // 2
            cp = _remote(ins[a].at[:, pl.ds((1 - cc) * h, h), :], outs[a], send.at[a], recv.at[a], (x, y, 1 - cc))
            cp.start()
            cps.append(cp)
        for cp in cps:
            cp.wait_recv()
        for cp in cps:
            cp.wait_send()

    return pl.pallas_call(
        body, name="sibling_halves",
        out_shape=[jax.ShapeDtypeStruct((N_CHIP, g.shape[1] // 2, g.shape[2]), F32) for g in gs],
        in_specs=[ANY] * n, out_specs=[ANY] * n,
        scratch_shapes=[pltpu.SemaphoreType.DMA((n,)), pltpu.SemaphoreType.DMA((n,))],
    )(*gs)


def _chip_partials(place, gs, rs):
    n = len(gs)

    def body(pref, *refs):
        g_refs, r_refs = refs[:n], refs[n:2 * n]
        pb_refs, own_refs = refs[2 * n:3 * n], refs[3 * n:]
        jj = pl.program_id(0)
        for a in range(n):
            sm = g_refs[a][0] + r_refs[a][0]
            pb_refs[a][0] = sm.astype(MXU_DTYPE)

            @pl.when(jj == pref[1])
            def _(a=a, sm=sm):
                own_refs[a][...] = sm

    halves = [(g.shape[1] // 2, g.shape[2]) for g in gs]
    in_specs = [pl.BlockSpec((1, h, w), lambda jj, pref: (jj, pref[0], 0)) for h, w in halves]
    in_specs += [pl.BlockSpec((1, h, w), lambda jj, pref: (jj, 0, 0)) for h, w in halves]
    out_specs = [pl.BlockSpec((1, h, w), lambda jj, pref: (jj, 0, 0)) for h, w in halves]
    out_specs += [pl.BlockSpec((h, w), lambda jj, pref: (0, 0)) for h, w in halves]
    out = pl.pallas_call(
        body, name="chip_partials",
        grid_spec=pltpu.PrefetchScalarGridSpec(num_scalar_prefetch=1, grid=(N_CHIP,), in_specs=in_specs,
                                               out_specs=out_specs),
        out_shape=[jax.ShapeDtypeStruct((N_CHIP, h, w), MXU_DTYPE) for h, w in halves]
        + [jax.ShapeDtypeStruct((h, w), F32) for h, w in halves],
        compiler_params=pltpu.CompilerParams(dimension_semantics=("arbitrary",)),
    )(place, *gs, *rs)
    return out[:n], out[n:]


def _exchange_partials(pbs):
    n = len(pbs)

    def body(*refs):
        ins, outs = refs[:n], refs[n:2 * n]
        send, recv = refs[2 * n:]
        x, y, cc = _place()
        cps = []
        for a in range(n):
            for r in range(1, N_CHIP):
                kx, ky = _flip(x, y, r)
                cp = _remote(ins[a].at[2 * kx + ky], outs[a].at[r - 1], send.at[a, r - 1], recv.at[a, r - 1],
                             (kx, ky, cc))
                cp.start()
                cps.append(cp)
        for cp in cps:
            cp.wait_recv()
        for cp in cps:
            cp.wait_send()

    return pl.pallas_call(
        body, name="exchange_partials",
        out_shape=[jax.ShapeDtypeStruct((N_CHIP - 1,) + p.shape[1:], p.dtype) for p in pbs],
        in_specs=[ANY] * n, out_specs=[ANY] * n,
        scratch_shapes=[pltpu.SemaphoreType.DMA((n, N_CHIP - 1)), pltpu.SemaphoreType.DMA((n, N_CHIP - 1))],
    )(*pbs)


def _sum_partials(place, owns, recvd):
    n = len(owns)

    def body(pref, *refs):
        o_refs, r_refs, out_refs = refs[:n], refs[n:2 * n], refs[2 * n:]
        for a in range(n):
            acc = o_refs[a][...]
            for r in range(N_CHIP - 1):
                acc = acc + r_refs[a][r].astype(F32)
            out_refs[a][...] = acc

    full = lambda a: pl.BlockSpec(a.shape, lambda i, pref: (0,) * a.ndim)
    return pl.pallas_call(
        body, name="sum_partials",
        grid_spec=pltpu.PrefetchScalarGridSpec(
            num_scalar_prefetch=1, grid=(1,), in_specs=[full(a) for a in list(owns) + list(recvd)],
            out_specs=[pl.BlockSpec(o.shape, lambda i, pref: (pref[0], 0)) for o in owns]),
        out_shape=[jax.ShapeDtypeStruct((2 * o.shape[0], o.shape[1]), F32) for o in owns],
        compiler_params=pltpu.CompilerParams(dimension_semantics=("arbitrary",)),
    )(place, *owns, *recvd)


def _join_halves(fulls):
    n = len(fulls)

    def body(*refs):
        outs = refs[n:2 * n]
        send, recv = refs[2 * n:]
        x, y, cc = _place()
        cps = []
        for a in range(n):
            h = fulls[a].shape[0] // 2
            mine = outs[a].at[pl.ds(cc * h, h), :]
            cp = _remote(mine, mine, send.at[a], recv.at[a], (x, y, 1 - cc))
            cp.start()
            cps.append(cp)
        for a in range(n):
            h = fulls[a].shape[0] // 2
            other = outs[a].at[pl.ds((1 - cc) * h, h), :]
            _remote(other, other, send.at[a], recv.at[a], (x, y, 1 - cc)).wait_recv()
        for cp in cps:
            cp.wait_send()

    return pl.pallas_call(
        body, name="join_halves",
        out_shape=[jax.ShapeDtypeStruct(f.shape, F32) for f in fulls],
        in_specs=[ANY] * n, out_specs=[ANY] * n,
        input_output_aliases={i: i for i in range(n)},
        scratch_shapes=[pltpu.SemaphoreType.DMA((n,)), pltpu.SemaphoreType.DMA((n,))],
    )(*fulls)


def _adamw_math(w, g, m, v):
    m = ADAM_B1 * m + (1.0 - ADAM_B1) * g
    v = ADAM_B2 * v + (1.0 - ADAM_B2) * (g * g)
    m_hat = m / (1.0 - ADAM_B1 ** ADAM_STEP)
    v_hat = v / (1.0 - ADAM_B2 ** ADAM_STEP)
    delta = -ADAM_LR * (m_hat / (jnp.sqrt(v_hat) + ADAM_EPS) + ADAM_WD * w)
    return delta, m, v


def _row_tile(rows):
    for t in (512, 352, 256, 128):
        if rows % t == 0:
            return t
    return rows


def _adamw(name, w, g, m, v):
    rows, cols = w.shape
    tr = _row_tile(rows)

    def body(w_ref, g_ref, m_ref, v_ref, d_ref, nm_ref, nv_ref):
        d_ref[...], nm_ref[...], nv_ref[...] = _adamw_math(w_ref[...], g_ref[...], m_ref[...], v_ref[...])

    spec = pl.BlockSpec((tr, cols), lambda i: (i, 0))
    return pl.pallas_call(
        body, name=name, grid=(rows // tr,),
        in_specs=[spec] * 4, out_specs=[spec] * 3,
        out_shape=[jax.ShapeDtypeStruct(w.shape, F32)] * 3,
        compiler_params=pltpu.CompilerParams(dimension_semantics=("arbitrary",)),
    )(w, g, m, v)


def _adamw_ada(place, cact, dmod, w, m, v):
    rows, cols = w.shape
    tr = _row_tile(rows)

    def body(pref, ca_ref, dm_ref, w_ref, m_ref, v_ref, g_ref, d_ref, nm_ref, nv_ref):
        g = lax.dot_general(ca_ref[...], dm_ref[...], (((0,), (0,)), ((), ())), preferred_element_type=F32,
                            precision=lax.Precision.HIGHEST)
        g_ref[...] = g
        d_ref[...], nm_ref[...], nv_ref[...] = _adamw_math(w_ref[...], g, m_ref[...], v_ref[...])

    spec = pl.BlockSpec((tr, cols), lambda i, pref: (i, 0))
    return pl.pallas_call(
        body, name="adamw_ada",
        grid_spec=pltpu.PrefetchScalarGridSpec(
            num_scalar_prefetch=1, grid=(rows // tr,),
            in_specs=[pl.BlockSpec((N_DEV, tr), lambda i, pref: (0, i)),
                      pl.BlockSpec((N_DEV, cols), lambda i, pref: (0, pref[1])), spec, spec, spec],
            out_specs=[spec] * 4),
        out_shape=[jax.ShapeDtypeStruct(w.shape, F32)] * 4,
        compiler_params=pltpu.CompilerParams(dimension_semantics=("arbitrary",)),
    )(place, cact, dmod, w, m, v)


def _adamw_small(place, vf_all, vd_all, vc_all, ddw_all, gwg_all, wmv):
    nw = len(wmv)
    flat = [a for t in wmv for a in t]

    def body(pref, vf_ref, vd_ref, vc_ref, ddw_ref, gwg_ref, *refs):
        w_refs = refs[:3 * nw]
        loss_ref, dmod_ref = refs[3 * nw], refs[3 * nw + 1]
        o_refs = refs[3 * nw + 2:]
        j = pref[1]

        def total(ref):
            acc = ref[0]
            for b in range(1, N_DEV):
                acc = acc + ref[b]
            return acc

        vf, vd, vc, ddw, gwg = total(vf_ref), total(vd_ref), total(vc_ref), total(ddw_ref), total(gwg_ref)
        loss_ref[...] = (0.5 / D_MODEL) * jnp.sum(vf[5:6, :], axis=1, keepdims=True)
        order = ((vd_ref, 1), (vd_ref, 2), (vd_ref, 0), (vf_ref, 2), (vf_ref, 3), (vf_ref, 1))
        for b in range(N_DEV):
            for q, (ref, row) in enumerate(order):
                dmod_ref[b:b + 1, q * D_MODEL:(q + 1) * D_MODEL] = ref[b, row:row + 1, :]
        dm = dmod_ref[...]
        g_bada = dm[0:1, :]
        for b in range(1, N_DEV):
            g_bada = g_bada + dm[b:b + 1, :]
        g_dww = jnp.zeros((32, POOL_G), F32)
        for k in range(N_CHIP):
            g_dww = g_dww + jnp.where(j == k, ddw[:, k * POOL_G:(k + 1) * POOL_G], 0.0)
        grads = [g_bada, vd[3:4, :], g_dww, vc[3:4, :], vc[1:2, :], vc[2:3, :], gwg, vc[0:1, :], vf[4:5, :],
                 vf[0:1, :]]
        for i, g in enumerate(grads):
            w_ref, m_ref, v_ref = w_refs[3 * i:3 * i + 3]
            d, nm, nv = _adamw_math(w_ref[...], g, m_ref[...], v_ref[...])
            o_refs[4 * i][...] = g
            o_refs[4 * i + 1][...] = d
            o_refs[4 * i + 2][...] = nm
            o_refs[4 * i + 3][...] = nv

    gathered = [vf_all, vd_all, vc_all, ddw_all, gwg_all]
    outs = [jax.ShapeDtypeStruct((1, 1), F32), jax.ShapeDtypeStruct((N_DEV, 6 * D_MODEL), F32)]
    for w, _, _ in wmv:
        outs += [jax.ShapeDtypeStruct(w.shape, F32)] * 4
    full = lambda a: pl.BlockSpec(a.shape, lambda i, pref: (0,) * a.ndim)
    res = pl.pallas_call(
        body, name="adamw_small",
        grid_spec=pltpu.PrefetchScalarGridSpec(
            num_scalar_prefetch=1, grid=(1,),
            in_specs=[full(a) for a in gathered + flat], out_specs=[full(o) for o in outs]),
        out_shape=outs,
        compiler_params=pltpu.CompilerParams(dimension_semantics=("arbitrary",)),
    )(place, *gathered, *flat)
    return res[0], res[1], [res[2 + 4 * i:6 + 4 * i] for i in range(nw)]


def kernel(x, c, w_ada, b_ada, g_norm1, w_in, dw_w, dw_b, conv_ln_g, conv_ln_b, w_conv_pw, w_pool_group, pool_scale, w_out, g_norm2, w_ffn_gate, w_ffn_up, w_ffn_down, g_final, loss_target, m_w_ada, m_b_ada, m_g_norm1, m_w_in, m_dw_w, m_dw_b, m_conv_ln_g, m_conv_ln_b, m_w_conv_pw, m_w_pool_group, m_pool_scale, m_w_out, m_g_norm2, m_w_ffn_gate, m_w_ffn_up, m_w_ffn_down, m_g_final, v_w_ada, v_b_ada, v_g_norm1, v_w_in, v_dw_w, v_dw_b, v_conv_ln_g, v_conv_ln_b, v_w_conv_pw, v_w_pool_group, v_pool_scale, v_w_out, v_g_norm2, v_w_ffn_gate, v_w_ffn_up, v_w_ffn_down, v_g_final):
    xi, yi, ci = _place()
    place = jnp.stack([ci, 2 * xi + yi]).astype(jnp.int32)
    n_ada = w_ada.shape[2]

    mod4, cact = _ada_fwd(c, w_ada[0], b_ada.reshape(N_CHIP, n_ada))
    mod = mod4.reshape(6, D_MODEL)

    tr = lambda a: jnp.transpose(a[0])
    big = [w_in[0], w_conv_pw[0], w_out[0], tr(w_ffn_gate), tr(w_ffn_up), w_ffn_down[0]]
    *bufs, dww_buf, wg_b = _cast_weights(place, big, dw_w[0], w_pool_group[0])
    win_g, wpw_g, wout_g, wgate_g, wup_g, wdown_g, dww_g = _gather_weights(bufs, dww_buf)
    dww_full = jnp.pad(jnp.concatenate([dww_g[k] for k in range(N_CHIP)], axis=1), ((0, 1), (0, 0)))

    gx, gs, (vec_f, vec_d, vec_c, ddw, g_wg) = _sample_grads(
        x[0], loss_target[0], mod, g_norm1, g_norm2, g_final.reshape(1, D_MODEL), win_g, dww_full, dw_b,
        conv_ln_g, conv_ln_b, wpw_g.reshape(CONV_W, CONV_W), wg_b, pool_scale, wout_g.reshape(D_MODEL, D_MODEL),
        wgate_g, wup_g, wdown_g)

    rs = _sibling_halves(gs)
    pbs, owns = _chip_partials(place, gs, rs)
    recvd = _exchange_partials(pbs)
    g_in, g_pw, g_out, g_gate, g_up, g_down = _join_halves(_sum_partials(place, owns, recvd))

    vf_all, vd_all, vc_all, ddw_all, gwg_all = _gather_small([vec_f, vec_d, vec_c, ddw, g_wg])
    pad_rows = lambda a: jnp.pad(a[0], ((0, 1), (0, 0)))
    row = lambda a: a.reshape(1, -1)
    small = [(b_ada, m_b_ada, v_b_ada), (g_norm1, m_g_norm1, v_g_norm1),
             (pad_rows(dw_w), pad_rows(m_dw_w), pad_rows(v_dw_w)), (dw_b, m_dw_b, v_dw_b),
             (conv_ln_g, m_conv_ln_g, v_conv_ln_g), (conv_ln_b, m_conv_ln_b, v_conv_ln_b),
             (w_pool_group[0], m_w_pool_group[0], v_w_pool_group[0]), (pool_scale, m_pool_scale, v_pool_scale),
             (g_norm2, m_g_norm2, v_g_norm2), (row(g_final), row(m_g_final), row(v_g_final))]
    loss, dmod, small_out = _adamw_small(place, vf_all, vd_all, vc_all, ddw_all, gwg_all, small)
    (o_bada, o_g1, o_dww, o_dwb, o_lng, o_lnb, o_wg, o_ps, o_g2, o_gf) = small_out
    o_dww = [a[:CONV_K] for a in o_dww]
    o_gf = [a.reshape(D_MODEL) for a in o_gf]
    lead = lambda outs: [a[None] for a in outs]

    o_ada = _adamw_ada(place, cact, dmod, w_ada[0], m_w_ada[0], v_w_ada[0])
    upd = lambda name, w, g, m, v: [g] + list(_adamw(name, w[0], g, m[0], v[0]))
    o_in = upd("adamw_in", w_in, g_in, m_w_in, v_w_in)
    o_pw = upd("adamw_pw", w_conv_pw, g_pw, m_w_conv_pw, v_w_conv_pw)
    o_out = upd("adamw_out", w_out, g_out, m_w_out, v_w_out)
    upd_t = lambda name, w, g, m, v: [jnp.transpose(o) for o in [g] + list(_adamw(name, tr(w), g, tr(m), tr(v)))]
    o_gate = upd_t("adamw_gate", w_ffn_gate, g_gate, m_w_ffn_gate, v_w_ffn_gate)
    o_up = upd_t("adamw_up", w_ffn_up, g_up, m_w_ffn_up, v_w_ffn_up)
    o_down = upd("adamw_down", w_ffn_down, g_down, m_w_ffn_down, v_w_ffn_down)

    per_weight = [lead(o_ada), o_bada, o_g1, lead(o_in), lead(o_dww), o_dwb, o_lng, o_lnb, lead(o_pw), lead(o_wg),
                  o_ps, lead(o_out), o_g2, lead(o_gate), lead(o_up), lead(o_down), o_gf]
    result = [loss.reshape(()), gx[None]]
    for kind in range(4):
        result += [o[kind] for o in per_weight]
    return tuple(result)
```

```python
import functools

import jax
import jax.numpy as jnp
from jax import lax
from jax.experimental import pallas as pl
from jax.experimental.pallas import tpu as pltpu

F32 = jnp.float32
MXU_DTYPE = jnp.bfloat16
EPS = 1e-6

D_MODEL = 1024
CONV_W = 512
POOL_W = 512
CONV_K = 31
POOL_WINDOWS = (2, 4, 8, 16)
POOL_G = 128
IN_W = 2 * CONV_W + POOL_W
N_CHIP = 4
N_DEV = 8
CONV_HALO = 32
POOL_HALO = 16

ADAM_LR = 0.001
ADAM_B1 = 0.9
ADAM_B2 = 0.999
ADAM_EPS = 1e-08
ADAM_WD = 0.01
ADAM_STEP = 10

MESH = pl.DeviceIdType.MESH
ANY = pl.BlockSpec(memory_space=pl.ANY)
VMEM = pl.BlockSpec(memory_space=pltpu.VMEM)


def _dot(a, b):
    return jnp.dot(a.astype(MXU_DTYPE), b.astype(MXU_DTYPE), preferred_element_type=F32)


def _dot_nt(a, b):
    return lax.dot_general(a.astype(MXU_DTYPE), b.astype(MXU_DTYPE), (((1,), (1,)), ((), ())),
                           preferred_element_type=F32)


def _dot_tn(a, b):
    return lax.dot_general(a.astype(MXU_DTYPE), b.astype(MXU_DTYPE), (((0,), (0,)), ((), ())),
                           preferred_element_type=F32)


def _sigmoid(v):
    return 1.0 / (1.0 + jnp.exp(-v))


def _full(shape):
    n = len(shape)
    return pl.BlockSpec(shape, lambda *_: (0,) * n)


def _token_tile(s):
    return 256 if s % 256 == 0 else s


def _place():
    return lax.axis_index("x"), lax.axis_index("y"), lax.axis_index("c")


def _flip(x, y, r):
    return ((1 - x) if r & 2 else x, (1 - y) if r & 1 else y)


def _remote(src, dst, send_sem, recv_sem, dev):
    return pltpu.make_async_remote_copy(src_ref=src, dst_ref=dst, send_sem=send_sem, recv_sem=recv_sem,
                                        device_id=dev, device_id_type=MESH)


class _Comm:
    def __init__(self, ins, outs, aliases, scratch, start, finish):
        self.ins, self.outs, self.aliases, self.scratch = list(ins), list(outs), dict(aliases), list(scratch)
        self.start, self.finish = start, finish


def _call(body, *, name, grid, in_specs, out_specs, out_shape, args, scratch_shapes=(), prefetch=(), comm=None):
    n_pre, n_in, n_out, n_scr = len(prefetch), len(in_specs), len(out_specs), len(scratch_shapes)
    c_ins = comm.ins if comm else []
    c_outs = comm.outs if comm else []
    c_scr = comm.scratch if comm else []
    last = grid[0] - 1

    def wrapped(*refs):
        pre, refs = refs[:n_pre], refs[n_pre:]
        ins, cin = refs[:n_in], refs[n_in:n_in + len(c_ins)]
        refs = refs[n_in + len(c_ins):]
        outs, cout = refs[:n_out], refs[n_out:n_out + len(c_outs)]
        refs = refs[n_out + len(c_outs):]
        scr, cscr = refs[:n_scr], refs[n_scr:]
        step = pl.program_id(0)
        if comm:
            @pl.when(step == 0)
            def _():
                comm.start(cin, cout, cscr)

        body(*pre, *ins, *outs, *scr)
        if comm:
            @pl.when(step == last)
            def _():
                comm.finish(cin, cout, cscr)

    aliases = {n_pre + n_in + a: n_out + b for a, b in (comm.aliases if comm else {}).items()}
    res = pl.pallas_call(
        wrapped, name=name,
        grid_spec=pltpu.PrefetchScalarGridSpec(
            num_scalar_prefetch=n_pre, grid=grid, in_specs=list(in_specs) + [ANY] * len(c_ins),
            out_specs=list(out_specs) + [ANY] * len(c_outs), scratch_shapes=list(scratch_shapes) + list(c_scr)),
        out_shape=list(out_shape) + list(c_outs),
        input_output_aliases=aliases,
        compiler_params=pltpu.CompilerParams(dimension_semantics=("arbitrary",)),
    )(*prefetch, *args, *c_ins)
    return res[:n_out], res[n_out:]


def _comm_only(name, comm):
    return _call(lambda: None, name=name, grid=(1,), in_specs=[], out_specs=[], out_shape=[], args=[], comm=comm)[1]


def _gather_phases(make_items):
    def own_sends(items, send, recv):
        x, y, cc = _place()
        j = 2 * x + y
        cps = []
        for a, it in enumerate(items):
            if it["sibling"]:
                cps.append(_remote(it["src"], it["dst"](j, cc), send.at[a, 0], recv.at[a, 0], (x, y, 1 - cc)))
            for r in range(1, N_CHIP):
                kx, ky = _flip(x, y, r)
                cps.append(_remote(it["src"], it["dst"](j, cc), send.at[a, r], recv.at[a, r], (kx, ky, cc)))
        return cps

    def start(ins, outs, scr):
        items = make_items(ins, outs, scr)
        send, recv, lsem = scr[-3:]
        for a, it in enumerate(items):
            if it["local"] is not None:
                src, stage, dst = it["local"]
                lc = pltpu.make_async_copy(src, stage, lsem.at[a])
                lc.start()
                lc.wait()
                pltpu.make_async_copy(stage, dst, lsem.at[a]).start()
        for cp in own_sends(items, send, recv):
            cp.start()

    def finish(ins, outs, scr):
        items = make_items(ins, outs, scr)
        send, recv, lsem = scr[-3:]
        x, y, cc = _place()
        j = 2 * x + y
        sib = (x, y, 1 - cc)
        forwards = []
        for a, it in enumerate(items):
            for r in range(1, N_CHIP):
                kx, ky = _flip(x, y, r)
                got = it["dst"](2 * kx + ky, cc)
                _remote(got, got, send.at[a, r], recv.at[a, r], sib).wait_recv()
                if it["forward"]:
                    cp = _remote(got, got, send.at[a, 3 + r], recv.at[a, 3 + r], sib)
                    cp.start()
                    forwards.append(cp)
        for a, it in enumerate(items):
            if it["sibling"]:
                got = it["dst"](j, 1 - cc)
                _remote(got, got, send.at[a, 0], recv.at[a, 0], sib).wait_recv()
            if it["forward"]:
                for r in range(1, N_CHIP):
                    kx, ky = _flip(x, y, r)
                    got = it["dst"](2 * kx + ky, 1 - cc)
                    _remote(got, got, send.at[a, 3 + r], recv.at[a, 3 + r], sib).wait_recv()
        for cp in own_sends(items, send, recv) + forwards:
            cp.wait_send()
        for a, it in enumerate(items):
            if it["local"] is not None:
                src, stage, dst = it["local"]
                pltpu.make_async_copy(stage, dst, lsem.at[a]).wait()

    return start, finish


def _gather_sems(n):
    return [pltpu.SemaphoreType.DMA((n, 7)), pltpu.SemaphoreType.DMA((n, 7)), pltpu.SemaphoreType.DMA((n,))]


def _weights_gather(bufs, split):
    def make_items(ins, outs, scr):
        x, y, cc = _place()
        j = 2 * x + y
        items = []
        for a, buf in enumerate(bufs):
            if split[a]:
                h = buf.shape[1] // 2
                dst = functools.partial(lambda o, h, kj, pc: o.at[kj, pl.ds(pc * h, h), :], outs[a], h)
            else:
                dst = functools.partial(lambda o, kj, pc: o.at[kj], outs[a])
            items.append(dict(src=dst(j, cc), dst=dst, local=None, sibling=False, forward=split[a]))
        return items

    start, finish = _gather_phases(make_items)
    n = len(bufs)
    return _Comm(bufs, [jax.ShapeDtypeStruct(b.shape, b.dtype) for b in bufs], {i: i for i in range(n)},
                 _gather_sems(n), start, finish)


def _small_gather(arrs):
    n = len(arrs)

    def make_items(ins, outs, scr):
        x, y, cc = _place()
        items = []
        for a in range(n):
            dst = functools.partial(lambda o, kj, pc: o.at[2 * kj + pc], outs[a])
            items.append(dict(src=ins[a], dst=dst, local=(ins[a], scr[a], outs[a].at[4 * x + 2 * y + cc]),
                              sibling=True, forward=True))
        return items

    start, finish = _gather_phases(make_items)
    return _Comm(arrs, [jax.ShapeDtypeStruct((N_DEV,) + a.shape, a.dtype) for a in arrs], {},
                 [pltpu.VMEM(a.shape, a.dtype) for a in arrs] + _gather_sems(n), start, finish)


def _direct_phases(copies):
    def start(ins, outs, scr):
        for cp in copies(ins, outs, *scr):
            cp.start()

    def finish(ins, outs, scr):
        cps = copies(ins, outs, *scr)
        for cp in cps:
            cp.wait_recv()
        for cp in cps:
            cp.wait_send()

    return start, finish


def _sibling_halves(gs):
    n = len(gs)

    def copies(ins, outs, send, recv):
        x, y, cc = _place()
        cps = []
        for a in range(n):
            h = gs[a].shape[1] // 2
            cps.append(_remote(ins[a].at[:, pl.ds((1 - cc) * h, h), :], outs[a], send.at[a], recv.at[a],
                               (x, y, 1 - cc)))
        return cps

    start, finish = _direct_phases(copies)
    return _Comm(gs, [jax.ShapeDtypeStruct((N_CHIP, g.shape[1] // 2, g.shape[2]), F32) for g in gs], {},
                 [pltpu.SemaphoreType.DMA((n,)), pltpu.SemaphoreType.DMA((n,))], start, finish)


def _exchange_partials(pbs):
    n = len(pbs)

    def copies(ins, outs, send, recv):
        x, y, cc = _place()
        cps = []
        for a in range(n):
            for r in range(1, N_CHIP):
                kx, ky = _flip(x, y, r)
                cps.append(_remote(ins[a].at[2 * kx + ky], outs[a].at[r - 1], send.at[a, r - 1], recv.at[a, r - 1],
                                   (kx, ky, cc)))
        return cps

    start, finish = _direct_phases(copies)
    return _Comm(pbs, [jax.ShapeDtypeStruct((N_CHIP - 1,) + p.shape[1:], p.dtype) for p in pbs], {},
                 [pltpu.SemaphoreType.DMA((n, N_CHIP - 1)), pltpu.SemaphoreType.DMA((n, N_CHIP - 1))], start, finish)


def _join_halves(fulls):
    n = len(fulls)

    def copies(ins, outs, send, recv):
        x, y, cc = _place()
        cps = []
        for a in range(n):
            h = fulls[a].shape[0] // 2
            mine = outs[a].at[pl.ds(cc * h, h), :]
            cps.append(_remote(mine, mine, send.at[a], recv.at[a], (x, y, 1 - cc)))
        return cps

    start, finish = _direct_phases(copies)
    return _Comm(fulls, [jax.ShapeDtypeStruct(f.shape, F32) for f in fulls], {i: i for i in range(n)},
                 [pltpu.SemaphoreType.DMA((n,)), pltpu.SemaphoreType.DMA((n,))], start, finish)


def _cast_weights(place, shards, dww, wg):
    n = len(shards)

    def body(pref, *refs):
        ins, outs = refs[:n + 2], refs[n + 2:]
        for a in range(n):
            outs[a][0] = ins[a][...].astype(MXU_DTYPE)
        outs[n][0] = ins[n][...]
        outs[n + 1][...] = ins[n + 1][...].astype(MXU_DTYPE)

    full = lambda a: pl.BlockSpec(a.shape, lambda i, pref: (0,) * a.ndim)
    slot = lambda a: pl.BlockSpec((1,) + a.shape, lambda i, pref: (pref[1],) + (0,) * a.ndim)
    arrs = list(shards) + [dww, wg]
    return pl.pallas_call(
        body, name="cast_weights",
        grid_spec=pltpu.PrefetchScalarGridSpec(
            num_scalar_prefetch=1, grid=(1,), in_specs=[full(a) for a in arrs],
            out_specs=[slot(a) for a in arrs[:n + 1]] + [full(wg)]),
        out_shape=[jax.ShapeDtypeStruct((N_CHIP,) + a.shape, MXU_DTYPE) for a in shards]
        + [jax.ShapeDtypeStruct((N_CHIP,) + dww.shape, F32), jax.ShapeDtypeStruct(wg.shape, MXU_DTYPE)],
        compiler_params=pltpu.CompilerParams(dimension_semantics=("arbitrary",)),
    )(place, *arrs)


def _mixer_fwd(x, mod, g1, w_in, dww, dwb, lng, lnb, w_pw, wg, pscale, w_out, comm=None):
    s = x.shape[0]
    ts = _token_tile(s)
    nt = s // ts

    def body(x_ref, mod_ref, g1_ref, win_ref, dww_ref, dwb_ref, lng_ref, lnb_ref, wpw_ref, wg_ref, ps_ref,
             wout_ref, x2_ref, y_ref, u_ref, z_ref, rstd_ref, p_ref, ycat_ref, gpad, vpad):
        i = pl.program_id(0)

        @pl.when(i == 0)
        def _():
            gpad[0:CONV_HALO, :] = jnp.zeros((CONV_HALO, CONV_W), F32)
            vpad[0:POOL_HALO, :] = jnp.zeros((POOL_HALO, POOL_W), F32)

        xt = x_ref[...]
        sh1 = mod_ref[0:1, :]
        sc1 = mod_ref[1:2, :]
        gt1 = mod_ref[2:3, :]
        r1 = lax.rsqrt(jnp.mean(xt * xt, axis=-1, keepdims=True) + EPS)
        h1 = (xt * r1 * g1_ref[...]) * (1.0 + sc1) + sh1
        h1b = h1.astype(MXU_DTYPE)
        u = jnp.concatenate([_dot(h1b, win_ref[j]) for j in range(N_CHIP)], axis=1)
        u_ref[...] = u
        a = u[:, :CONV_W]
        g = u[:, CONV_W:2 * CONV_W]
        v = u[:, 2 * CONV_W:]

        gpad[CONV_HALO:CONV_HALO + ts, :] = a * _sigmoid(g)
        cv = jnp.broadcast_to(dwb_ref[...], (ts, CONV_W))
        off = CONV_HALO - (CONV_K - 1)
        for k in range(CONV_K):
            cv = cv + dww_ref[k:k + 1, :] * gpad[off + k:off + k + ts, :]
        gpad[0:CONV_HALO, :] = gpad[ts:ts + CONV_HALO, :]

        mu = jnp.mean(cv, axis=-1, keepdims=True)
        cc = cv - mu
        rstd = lax.rsqrt(jnp.mean(cc * cc, axis=-1, keepdims=True) + EPS)
        z = cc * rstd
        z_ref[...] = z
        rstd_ref[...] = rstd
        ln = z * lng_ref[...] + lnb_ref[...]
        sw = ln * _sigmoid(ln)
        yconv = _dot(sw, wpw_ref[...])

        vpad[POOL_HALO:POOL_HALO + ts, :] = v
        t = i * ts + lax.broadcasted_iota(jnp.int32, (ts, 1), 0)
        ps, ypool = [], []
        for gi, w in enumerate(POOL_WINDOWS):
            cols = slice(gi * POOL_G, (gi + 1) * POOL_G)
            acc = vpad[POOL_HALO:POOL_HALO + ts, cols]
            for d in range(1, w):
                acc = acc + vpad[POOL_HALO - d:POOL_HALO - d + ts, cols]
            cnt = jnp.minimum(t + 1, w).astype(F32)
            pg = (acc / cnt - v[:, cols]).astype(MXU_DTYPE)
            ps.append(pg)
            ypool.append(_dot(pg, wg_ref[gi]))
        vpad[0:POOL_HALO, :] = vpad[ts:ts + POOL_HALO, :]
        p_ref[...] = jnp.concatenate(ps, axis=1)
        ypool = jnp.concatenate(ypool, axis=1) * ps_ref[...]

        ycat = jnp.concatenate([yconv, ypool], axis=1).astype(MXU_DTYPE)
        ycat_ref[...] = ycat
        y = _dot(ycat, wout_ref[...])
        y_ref[...] = y
        x2_ref[...] = xt + gt1 * y

    tile = lambda w: pl.BlockSpec((ts, w), lambda i: (i, 0))
    return _call(
        body, name="mixer_fwd", grid=(nt,),
        in_specs=[tile(D_MODEL), _full(mod.shape), _full(g1.shape), _full(w_in.shape), _full(dww.shape),
                  _full(dwb.shape), _full(lng.shape), _full(lnb.shape), _full(w_pw.shape), _full(wg.shape),
                  _full(pscale.shape), _full(w_out.shape)],
        out_specs=[tile(D_MODEL), tile(D_MODEL), tile(IN_W), tile(CONV_W), tile(1), tile(POOL_W), tile(D_MODEL)],
        out_shape=[jax.ShapeDtypeStruct((s, D_MODEL), F32), jax.ShapeDtypeStruct((s, D_MODEL), F32),
                   jax.ShapeDtypeStruct((s, IN_W), F32), jax.ShapeDtypeStruct((s, CONV_W), F32),
                   jax.ShapeDtypeStruct((s, 1), F32), jax.ShapeDtypeStruct((s, POOL_W), MXU_DTYPE),
                   jax.ShapeDtypeStruct((s, D_MODEL), MXU_DTYPE)],
        scratch_shapes=[pltpu.VMEM((ts + CONV_HALO, CONV_W), F32), pltpu.VMEM((ts + POOL_HALO, POOL_W), F32)],
        args=(x, mod, g1, w_in, dww, dwb, lng, lnb, w_pw, wg, pscale, w_out), comm=comm)


def _ffn(x2, tgt, mod, g2, gf, w_gate, w_up, w_down):
    s = x2.shape[0]
    ts = _token_tile(s)
    nt = s // ts
    fb = w_gate.shape[1]

    def body(x2_ref, tgt_ref, mod_ref, g2_ref, gf_ref, wgt_ref, wup_ref, wdn_ref,
             dx2_ref, h2_ref, df_ref, act_ref, dgg_ref, duu_ref, vec_ref, gg_s, uu_s):
        i = pl.program_id(0)

        @pl.when(i == 0)
        def _():
            vec_ref[...] = jnp.zeros(vec_ref.shape, F32)

        x2t = x2_ref[...]
        sh2 = mod_ref[3:4, :]
        sc2 = mod_ref[4:5, :]
        gt2 = mod_ref[5:6, :]
        g2v = g2_ref[...]
        gfv = gf_ref[...]
        r2 = lax.rsqrt(jnp.mean(x2t * x2t, axis=-1, keepdims=True) + EPS)
        xh2 = x2t * r2
        n2 = xh2 * g2v
        h2b = (n2 * (1.0 + sc2) + sh2).astype(MXU_DTYPE)
        h2_ref[...] = h2b
        f = jnp.zeros((ts, D_MODEL), F32)
        for j in range(N_CHIP):
            gg = _dot_nt(h2b, wgt_ref[j])
            uu = _dot_nt(h2b, wup_ref[j])
            gg_s[j] = gg
            uu_s[j] = uu
            actb = (gg * _sigmoid(gg) * uu).astype(MXU_DTYPE)
            act_ref[j] = actb
            f = f + _dot(actb, wdn_ref[j])
        x3 = x2t + gt2 * f
        r3 = lax.rsqrt(jnp.mean(x3 * x3, axis=-1, keepdims=True) + EPS)
        xh3 = x3 * r3
        diff = xh3 * gfv - tgt_ref[...]
        dout = diff * (1.0 / D_MODEL)
        dn3 = dout * gfv
        dx3 = r3 * (dn3 - xh3 * jnp.mean(dn3 * xh3, axis=-1, keepdims=True))
        dfb = (dx3 * gt2).astype(MXU_DTYPE)
        df_ref[...] = dfb
        dh2 = jnp.zeros((ts, D_MODEL), F32)
        for j in range(N_CHIP):
            dact = _dot_nt(dfb, wdn_ref[j])
            gg = gg_s[j]
            uu = uu_s[j]
            sg = _sigmoid(gg)
            duu = (dact * (gg * sg)).astype(MXU_DTYPE)
            dgg = (dact * uu * (sg * (1.0 + gg * (1.0 - sg)))).astype(MXU_DTYPE)
            duu_ref[j] = duu
            dgg_ref[j] = dgg
            dh2 = dh2 + _dot(dgg, wgt_ref[j]) + _dot(duu, wup_ref[j])
        dn2 = dh2 * (1.0 + sc2)
        dxh2 = dn2 * g2v
        dx2_ref[...] = dx3 + r2 * (dxh2 - xh2 * jnp.mean(dxh2 * xh2, axis=-1, keepdims=True))

        col = lambda a: jnp.sum(a, axis=0, keepdims=True)
        vec_ref[0:1, :] += col(dout * xh3)
        vec_ref[1:2, :] += col(dx3 * f)
        vec_ref[2:3, :] += col(dh2)
        vec_ref[3:4, :] += col(dh2 * n2)
        vec_ref[4:5, :] += col(dn2 * xh2)
        vec_ref[5:6, :] += col(diff * diff)

    tile = lambda w: pl.BlockSpec((ts, w), lambda i: (i, 0))
    tile3 = pl.BlockSpec((N_CHIP, ts, fb), lambda i: (0, i, 0))
    once = lambda a: pl.BlockSpec(a.shape, lambda i: (0,) * a.ndim, pipeline_mode=pl.Buffered(1))
    hid = jax.ShapeDtypeStruct((N_CHIP, s, fb), MXU_DTYPE)
    return pl.pallas_call(
        body, name="ffn", grid=(nt,),
        in_specs=[tile(D_MODEL), tile(D_MODEL), _full(mod.shape), _full(g2.shape), _full(gf.shape),
                  once(w_gate), once(w_up), once(w_down)],
        out_specs=[tile(D_MODEL), tile(D_MODEL), tile(D_MODEL), tile3, tile3, tile3, _full((8, D_MODEL))],
        out_shape=[jax.ShapeDtypeStruct((s, D_MODEL), F32), jax.ShapeDtypeStruct((s, D_MODEL), MXU_DTYPE),
                   jax.ShapeDtypeStruct((s, D_MODEL), MXU_DTYPE), hid, hid, hid,
                   jax.ShapeDtypeStruct((8, D_MODEL), F32)],
        scratch_shapes=[pltpu.VMEM((N_CHIP, ts, fb), F32), pltpu.VMEM((N_CHIP, ts, fb), F32)],
        compiler_params=pltpu.CompilerParams(dimension_semantics=("arbitrary",)),
    )(x2, tgt, mod, g2, gf, w_gate, w_up, w_down)


def _mixer_bwd(dx2, x, y, u, z, rstd, p, mod, g1, w_in, dww, lng, lnb, w_pw, wg, pscale, w_out, comm=None):
    s = x.shape[0]
    ts = _token_tile(s)
    nt = s // ts

    def body(dx2_ref, x_ref, y_ref, u_ref, z_ref, rstd_ref, p_ref, mod_ref, g1_ref, win_ref, dww_ref, lng_ref,
             lnb_ref, wpw_ref, wg_ref, ps_ref, wout_ref,
             gx_ref, h1_ref, du_ref, dy_ref, sw_ref, dyc_ref, dyp_ref, vd_ref, vc_ref, ddw_ref, dcpad, dppad):
        i = pl.program_id(0)
        tix = nt - 1 - i

        @pl.when(i == 0)
        def _():
            vd_ref[...] = jnp.zeros(vd_ref.shape, F32)
            vc_ref[...] = jnp.zeros(vc_ref.shape, F32)
            ddw_ref[...] = jnp.zeros(ddw_ref.shape, F32)
            dcpad[ts:ts + CONV_HALO, :] = jnp.zeros((CONV_HALO, CONV_W), F32)
            dppad[ts:ts + POOL_HALO, :] = jnp.zeros((POOL_HALO, POOL_W), F32)

        col = lambda a: jnp.sum(a, axis=0, keepdims=True)
        sh1 = mod_ref[0:1, :]
        sc1 = mod_ref[1:2, :]
        gt1 = mod_ref[2:3, :]
        dx2t = dx2_ref[...]
        vd_ref[0:1, :] += col(dx2t * y_ref[...])
        dyb = (dx2t * gt1).astype(MXU_DTYPE)
        dy_ref[...] = dyb
        dycat = _dot_nt(dyb, wout_ref[...])
        dyconv = dycat[:, :CONV_W]
        dypool = dycat[:, CONV_W:]

        pt = p_ref[...]
        t = tix * ts + lax.broadcasted_iota(jnp.int32, (ts, 1), 0)
        psc = ps_ref[...]
        dypb = (dypool * psc).astype(MXU_DTYPE)
        dyp_ref[...] = dypb
        dps, ypre = [], []
        for gi, w in enumerate(POOL_WINDOWS):
            cols = slice(gi * POOL_G, (gi + 1) * POOL_G)
            ypre.append(_dot(pt[:, cols], wg_ref[gi]))
            dpg = _dot_nt(dypb[:, cols], wg_ref[gi])
            dps.append(dpg)
            cnt = jnp.minimum(t + 1, w).astype(F32)
            dppad[0:ts, cols] = dpg / cnt
        vc_ref[0:1, :] += col(dypool * jnp.concatenate(ypre, axis=1))
        dvs = []
        for gi, w in enumerate(POOL_WINDOWS):
            cols = slice(gi * POOL_G, (gi + 1) * POOL_G)
            acc = dppad[0:ts, cols]
            for d in range(1, w):
                acc = acc + dppad[d:d + ts, cols]
            dvs.append(acc - dps[gi])
        dv = jnp.concatenate(dvs, axis=1)
        dppad[ts:ts + POOL_HALO, :] = dppad[0:POOL_HALO, :]

        zt = z_ref[...]
        lngv = lng_ref[...]
        ln = zt * lngv + lnb_ref[...]
        sg = _sigmoid(ln)
        swb = (ln * sg).astype(MXU_DTYPE)
        sw_ref[...] = swb
        dycb = dyconv.astype(MXU_DTYPE)
        dyc_ref[...] = dycb
        dln = _dot_nt(dycb, wpw_ref[...]) * (sg * (1.0 + ln * (1.0 - sg)))
        vc_ref[1:2, :] += col(dln * zt)
        vc_ref[2:3, :] += col(dln)
        dz = dln * lngv
        dcv = rstd_ref[...] * (dz - jnp.mean(dz, axis=-1, keepdims=True)
                               - zt * jnp.mean(dz * zt, axis=-1, keepdims=True))
        vc_ref[3:4, :] += col(dcv)
        dcpad[0:ts, :] = dcv
        ut = u_ref[...]
        a = ut[:, :CONV_W]
        g = ut[:, CONV_W:2 * CONV_W]
        sgg = _sigmoid(g)
        glu = a * sgg
        dglu = jnp.zeros((ts, CONV_W), F32)
        for k in range(CONV_K):
            sh = dcpad[CONV_K - 1 - k:CONV_K - 1 - k + ts, :]
            dglu = dglu + dww_ref[k:k + 1, :] * sh
            ddw_ref[k:k + 1, :] += col(glu * sh)
        dcpad[ts:ts + CONV_HALO, :] = dcpad[0:CONV_HALO, :]
        da = dglu * sgg
        dg = dglu * a * sgg * (1.0 - sgg)
        dub = jnp.concatenate([da, dg, dv], axis=1).astype(MXU_DTYPE)
        du_ref[...] = dub
        cw = IN_W // N_CHIP
        dh1 = jnp.zeros((ts, D_MODEL), F32)
        for j in range(N_CHIP):
            dh1 = dh1 + _dot_nt(dub[:, j * cw:(j + 1) * cw], win_ref[j])

        xt = x_ref[...]
        g1v = g1_ref[...]
        r1 = lax.rsqrt(jnp.mean(xt * xt, axis=-1, keepdims=True) + EPS)
        xh1 = xt * r1
        n1 = xh1 * g1v
        h1_ref[...] = (n1 * (1.0 + sc1) + sh1).astype(MXU_DTYPE)
        vd_ref[1:2, :] += col(dh1)
        vd_ref[2:3, :] += col(dh1 * n1)
        dn1 = dh1 * (1.0 + sc1)
        vd_ref[3:4, :] += col(dn1 * xh1)
        dxh = dn1 * g1v
        gx_ref[...] = dx2t + r1 * (dxh - xh1 * jnp.mean(dxh * xh1, axis=-1, keepdims=True))

    tile = lambda w: pl.BlockSpec((ts, w), lambda i: (nt - 1 - i, 0))
    bf = lambda w: jax.ShapeDtypeStruct((s, w), MXU_DTYPE)
    return _call(
        body, name="mixer_bwd", grid=(nt,),
        in_specs=[tile(D_MODEL), tile(D_MODEL), tile(D_MODEL), tile(IN_W), tile(CONV_W), tile(1), tile(POOL_W),
                  _full(mod.shape), _full(g1.shape), _full(w_in.shape), _full(dww.shape), _full(lng.shape),
                  _full(lnb.shape), _full(w_pw.shape), _full(wg.shape), _full(pscale.shape), _full(w_out.shape)],
        out_specs=[tile(D_MODEL), tile(D_MODEL), tile(IN_W), tile(D_MODEL), tile(CONV_W), tile(CONV_W),
                   tile(POOL_W), _full((8, D_MODEL)), _full((8, CONV_W)), _full((32, CONV_W))],
        out_shape=[jax.ShapeDtypeStruct((s, D_MODEL), F32), bf(D_MODEL), bf(IN_W), bf(D_MODEL), bf(CONV_W),
                   bf(CONV_W), bf(POOL_W), jax.ShapeDtypeStruct((8, D_MODEL), F32),
                   jax.ShapeDtypeStruct((8, CONV_W), F32), jax.ShapeDtypeStruct((32, CONV_W), F32)],
        scratch_shapes=[pltpu.VMEM((ts + CONV_HALO, CONV_W), F32), pltpu.VMEM((ts + POOL_HALO, POOL_W), F32)],
        args=(dx2, x, y, u, z, rstd, p, mod, g1, w_in, dww, lng, lnb, w_pw, wg, pscale, w_out), comm=comm)


def _dw(name, a, a_spec, b, b_spec, nb, mb, nbk, comm=None):
    def body(a_ref, b_ref, o_ref):
        av = a_ref[...]
        bv = b_ref[...]
        av = av.reshape(av.shape[-2:])
        bv = bv.reshape(bv.shape[-2:])
        o_ref[0] = _dot_tn(av, bv)

    (out,), rest = _call(
        body, name=name, grid=(nb,), in_specs=[a_spec, b_spec],
        out_specs=[pl.BlockSpec((1, mb, nbk), lambda j: (j, 0, 0))],
        out_shape=[jax.ShapeDtypeStruct((nb, mb, nbk), F32)], args=(a, b), comm=comm)
    return out, rest


def _ada_fwd(c, w_ada, b4, comm=None):
    nc = w_ada.shape[1]

    def body(c_ref, w_ref, b4_ref, mod_ref, cact_ref, call, part, parts, send1, recv1, send2, recv2):
        x, y, cc = _place()
        b = 4 * x + 2 * y + cc
        j = 2 * x + y
        call[b] = c_ref[...]
        sends = []
        for r in range(1, N_DEV):
            dev = ((1 - x) if r & 4 else x, (1 - y) if r & 2 else y, (1 - cc) if r & 1 else cc)
            cp = _remote(call.at[b], call.at[b], send1.at[r - 1], recv1.at[r - 1], dev)
            cp.start()
            sends.append(cp)
        for r in range(1, N_DEV):
            src_b = lax.bitwise_xor(b, r)
            _remote(call.at[src_b], call.at[src_b], send1.at[r - 1], recv1.at[r - 1], (x, y, cc)).wait_recv()
        for cp in sends:
            cp.wait_send()
        for i in range(N_DEV):
            ci = call[i]
            cact_ref[i:i + 1, :] = ci * _sigmoid(ci)
        part[...] = jnp.dot(cact_ref[...], w_ref[...], preferred_element_type=F32, precision=lax.Precision.HIGHEST)
        sends = []
        for r in range(1, N_CHIP):
            kx, ky = _flip(x, y, r)
            cp = _remote(part, parts.at[j], send2.at[r - 1], recv2.at[r - 1], (kx, ky, cc))
            cp.start()
            sends.append(cp)
        parts[j] = part[...]
        for r in range(1, N_CHIP):
            kx, ky = _flip(x, y, r)
            kj = 2 * kx + ky
            _remote(part, parts.at[kj], send2.at[r - 1], recv2.at[r - 1], (x, y, cc)).wait_recv()
        for cp in sends:
            cp.wait_send()
        mine = lax.broadcasted_iota(jnp.int32, (N_DEV, 1), 0) == b
        for k in range(N_CHIP):
            row = jnp.sum(jnp.where(mine, parts[k], 0.0), axis=0, keepdims=True)
            mod_ref[k:k + 1, :] = row + b4_ref[k:k + 1, :]

    return _call(
        body, name="ada_fwd", grid=(1,),
        in_specs=[VMEM, VMEM, VMEM], out_specs=[VMEM, VMEM],
        out_shape=[jax.ShapeDtypeStruct((N_CHIP, nc), F32), jax.ShapeDtypeStruct((N_DEV, D_MODEL), F32)],
        scratch_shapes=[pltpu.VMEM((N_DEV, 1, D_MODEL), F32), pltpu.VMEM((N_DEV, nc), F32),
                        pltpu.VMEM((N_CHIP, N_DEV, nc), F32),
                        pltpu.SemaphoreType.DMA((N_DEV - 1,)), pltpu.SemaphoreType.DMA((N_DEV - 1,)),
                        pltpu.SemaphoreType.DMA((N_CHIP - 1,)), pltpu.SemaphoreType.DMA((N_CHIP - 1,))],
        args=(c, w_ada, b4), comm=comm)


def _chip_partials(name, place, gs, rs, comm=None):
    n = len(gs)

    def body(pref, *refs):
        g_refs, r_refs = refs[:n], refs[n:2 * n]
        pb_refs, own_refs = refs[2 * n:3 * n], refs[3 * n:]
        jj = pl.program_id(0)
        for a in range(n):
            sm = g_refs[a][0] + r_refs[a][0]
            pb_refs[a][0] = sm.astype(MXU_DTYPE)

            @pl.when(jj == pref[1])
            def _(a=a, sm=sm):
                own_refs[a][...] = sm

    halves = [(g.shape[1] // 2, g.shape[2]) for g in gs]
    in_specs = [pl.BlockSpec((1, h, w), lambda jj, pref: (jj, pref[0], 0)) for h, w in halves]
    in_specs += [pl.BlockSpec((1, h, w), lambda jj, pref: (jj, 0, 0)) for h, w in halves]
    out_specs = [pl.BlockSpec((1, h, w), lambda jj, pref: (jj, 0, 0)) for h, w in halves]
    out_specs += [pl.BlockSpec((h, w), lambda jj, pref: (0, 0)) for h, w in halves]
    out, rest = _call(
        body, name=name, grid=(N_CHIP,), in_specs=in_specs, out_specs=out_specs,
        out_shape=[jax.ShapeDtypeStruct((N_CHIP, h, w), MXU_DTYPE) for h, w in halves]
        + [jax.ShapeDtypeStruct((h, w), F32) for h, w in halves],
        args=(*gs, *rs), prefetch=(place,), comm=comm)
    return (out[:n], out[n:]), rest


def _sum_partials(place, owns, recvd):
    n = len(owns)

    def body(pref, *refs):
        o_refs, r_refs, out_refs = refs[:n], refs[n:2 * n], refs[2 * n:]
        for a in range(n):
            acc = o_refs[a][...]
            for r in range(N_CHIP - 1):
                acc = acc + r_refs[a][r].astype(F32)
            out_refs[a][...] = acc

    full = lambda a: pl.BlockSpec(a.shape, lambda i, pref: (0,) * a.ndim)
    return pl.pallas_call(
        body, name="sum_partials",
        grid_spec=pltpu.PrefetchScalarGridSpec(
            num_scalar_prefetch=1, grid=(1,), in_specs=[full(a) for a in list(owns) + list(recvd)],
            out_specs=[pl.BlockSpec(o.shape, lambda i, pref: (pref[0], 0)) for o in owns]),
        out_shape=[jax.ShapeDtypeStruct((2 * o.shape[0], o.shape[1]), F32) for o in owns],
        compiler_params=pltpu.CompilerParams(dimension_semantics=("arbitrary",)),
    )(place, *owns, *recvd)


def _adamw_math(w, g, m, v):
    m = ADAM_B1 * m + (1.0 - ADAM_B1) * g
    v = ADAM_B2 * v + (1.0 - ADAM_B2) * (g * g)
    m_hat = m / (1.0 - ADAM_B1 ** ADAM_STEP)
    v_hat = v / (1.0 - ADAM_B2 ** ADAM_STEP)
    delta = -ADAM_LR * (m_hat / (jnp.sqrt(v_hat) + ADAM_EPS) + ADAM_WD * w)
    return delta, m, v


def _row_tile(rows):
    for t in (512, 352, 256, 128):
        if rows % t == 0:
            return t
    return rows


def _adamw(name, w, g, m, v):
    rows, cols = w.shape
    tr = _row_tile(rows)

    def body(w_ref, g_ref, m_ref, v_ref, d_ref, nm_ref, nv_ref):
        d_ref[...], nm_ref[...], nv_ref[...] = _adamw_math(w_ref[...], g_ref[...], m_ref[...], v_ref[...])

    spec = pl.BlockSpec((tr, cols), lambda i: (i, 0))
    return pl.pallas_call(
        body, name=name, grid=(rows // tr,),
        in_specs=[spec] * 4, out_specs=[spec] * 3,
        out_shape=[jax.ShapeDtypeStruct(w.shape, F32)] * 3,
        compiler_params=pltpu.CompilerParams(dimension_semantics=("arbitrary",)),
    )(w, g, m, v)


def _adamw_ada(place, cact, dmod, w, m, v):
    rows, cols = w.shape
    tr = _row_tile(rows)

    def body(pref, ca_ref, dm_ref, w_ref, m_ref, v_ref, g_ref, d_ref, nm_ref, nv_ref):
        g = lax.dot_general(ca_ref[...], dm_ref[...], (((0,), (0,)), ((), ())), preferred_element_type=F32,
                            precision=lax.Precision.HIGHEST)
        g_ref[...] = g
        d_ref[...], nm_ref[...], nv_ref[...] = _adamw_math(w_ref[...], g, m_ref[...], v_ref[...])

    spec = pl.BlockSpec((tr, cols), lambda i, pref: (i, 0))
    return pl.pallas_call(
        body, name="adamw_ada",
        grid_spec=pltpu.PrefetchScalarGridSpec(
            num_scalar_prefetch=1, grid=(rows // tr,),
            in_specs=[pl.BlockSpec((N_DEV, tr), lambda i, pref: (0, i)),
                      pl.BlockSpec((N_DEV, cols), lambda i, pref: (0, pref[1])), spec, spec, spec],
            out_specs=[spec] * 4),
        out_shape=[jax.ShapeDtypeStruct(w.shape, F32)] * 4,
        compiler_params=pltpu.CompilerParams(dimension_semantics=("arbitrary",)),
    )(place, cact, dmod, w, m, v)


def _adamw_small(place, vf_all, vd_all, vc_all, ddw_all, gwg_all, wmv):
    nw = len(wmv)
    flat = [a for t in wmv for a in t]

    def body(pref, vf_ref, vd_ref, vc_ref, ddw_ref, gwg_ref, *refs):
        w_refs = refs[:3 * nw]
        loss_ref, dmod_ref = refs[3 * nw], refs[3 * nw + 1]
        o_refs = refs[3 * nw + 2:]
        j = pref[1]

        def total(ref):
            acc = ref[0]
            for b in range(1, N_DEV):
                acc = acc + ref[b]
            return acc

        vf, vd, vc, ddw, gwg = total(vf_ref), total(vd_ref), total(vc_ref), total(ddw_ref), total(gwg_ref)
        loss_ref[...] = (0.5 / D_MODEL) * jnp.sum(vf[5:6, :], axis=1, keepdims=True)
        order = ((vd_ref, 1), (vd_ref, 2), (vd_ref, 0), (vf_ref, 2), (vf_ref, 3), (vf_ref, 1))
        for b in range(N_DEV):
            for q, (ref, row) in enumerate(order):
                dmod_ref[b:b + 1, q * D_MODEL:(q + 1) * D_MODEL] = ref[b, row:row + 1, :]
        dm = dmod_ref[...]
        g_bada = dm[0:1, :]
        for b in range(1, N_DEV):
            g_bada = g_bada + dm[b:b + 1, :]
        g_dww = jnp.zeros((32, POOL_G), F32)
        for k in range(N_CHIP):
            g_dww = g_dww + jnp.where(j == k, ddw[:, k * POOL_G:(k + 1) * POOL_G], 0.0)
        grads = [g_bada, vd[3:4, :], g_dww, vc[3:4, :], vc[1:2, :], vc[2:3, :], gwg, vc[0:1, :], vf[4:5, :],
                 vf[0:1, :]]
        for i, g in enumerate(grads):
            w_ref, m_ref, v_ref = w_refs[3 * i:3 * i + 3]
            d, nm, nv = _adamw_math(w_ref[...], g, m_ref[...], v_ref[...])
            o_refs[4 * i][...] = g
            o_refs[4 * i + 1][...] = d
            o_refs[4 * i + 2][...] = nm
            o_refs[4 * i + 3][...] = nv

    gathered = [vf_all, vd_all, vc_all, ddw_all, gwg_all]
    outs = [jax.ShapeDtypeStruct((1, 1), F32), jax.ShapeDtypeStruct((N_DEV, 6 * D_MODEL), F32)]
    for w, _, _ in wmv:
        outs += [jax.ShapeDtypeStruct(w.shape, F32)] * 4
    full = lambda a: pl.BlockSpec(a.shape, lambda i, pref: (0,) * a.ndim)
    res = pl.pallas_call(
        body, name="adamw_small",
        grid_spec=pltpu.PrefetchScalarGridSpec(
            num_scalar_prefetch=1, grid=(1,),
            in_specs=[full(a) for a in gathered + flat], out_specs=[full(o) for o in outs]),
        out_shape=outs,
        compiler_params=pltpu.CompilerParams(dimension_semantics=("arbitrary",)),
    )(place, *gathered, *flat)
    return res[0], res[1], [res[2 + 4 * i:6 + 4 * i] for i in range(nw)]


def kernel(x, c, w_ada, b_ada, g_norm1, w_in, dw_w, dw_b, conv_ln_g, conv_ln_b, w_conv_pw, w_pool_group, pool_scale, w_out, g_norm2, w_ffn_gate, w_ffn_up, w_ffn_down, g_final, loss_target, m_w_ada, m_b_ada, m_g_norm1, m_w_in, m_dw_w, m_dw_b, m_conv_ln_g, m_conv_ln_b, m_w_conv_pw, m_w_pool_group, m_pool_scale, m_w_out, m_g_norm2, m_w_ffn_gate, m_w_ffn_up, m_w_ffn_down, m_g_final, v_w_ada, v_b_ada, v_g_norm1, v_w_in, v_dw_w, v_dw_b, v_conv_ln_g, v_conv_ln_b, v_w_conv_pw, v_w_pool_group, v_pool_scale, v_w_out, v_g_norm2, v_w_ffn_gate, v_w_ffn_up, v_w_ffn_down, v_g_final):
    xi, yi, ci = _place()
    place = jnp.stack([ci, 2 * xi + yi]).astype(jnp.int32)
    n_ada = w_ada.shape[2]

    tr = lambda a: jnp.transpose(a[0])
    big = [w_in[0], w_conv_pw[0], w_out[0], tr(w_ffn_gate), tr(w_ffn_up), w_ffn_down[0]]
    b_in, b_pw, b_out, b_gate, b_up, b_down, b_dww, wg_b = _cast_weights(place, big, dw_w[0], w_pool_group[0])

    (mod4, cact), (win_g, wpw_g, wout_g, dww_g) = _ada_fwd(
        c, w_ada[0], b_ada.reshape(N_CHIP, n_ada),
        comm=_weights_gather([b_in, b_pw, b_out, b_dww], [True, True, True, False]))
    mod = mod4.reshape(6, D_MODEL)
    dww_full = jnp.pad(jnp.concatenate([dww_g[k] for k in range(N_CHIP)], axis=1), ((0, 1), (0, 0)))
    w_pw = wpw_g.reshape(CONV_W, CONV_W)
    w_o = wout_g.reshape(D_MODEL, D_MODEL)
    xs, tgt, gf = x[0], loss_target[0], g_final.reshape(1, D_MODEL)
    s = xs.shape[0]
    fb = b_gate.shape[1]

    (x2, y, u, z, rstd, p, ycat), (wgate_g, wup_g, wdown_g) = _mixer_fwd(
        xs, mod, g_norm1, win_g, dww_full, dw_b, conv_ln_g, conv_ln_b, w_pw, wg_b, pool_scale, w_o,
        comm=_weights_gather([b_gate, b_up, b_down], [True, True, True]))
    dx2, h2, df, act, dgg, duu, vec_f = _ffn(x2, tgt, mod, g_norm2, gf, wgate_g, wup_g, wdown_g)

    whole = lambda w: pl.BlockSpec((s, w), lambda j: (0, 0))
    cols = lambda w: pl.BlockSpec((s, w), lambda j: (0, j))
    hid = pl.BlockSpec((1, s, fb), lambda j: (j, 0, 0))
    c_gate, _ = _dw("dw_gate", dgg, hid, h2, whole(D_MODEL), N_CHIP, fb, D_MODEL)
    c_up, (r_gate,) = _dw("dw_up", duu, hid, h2, whole(D_MODEL), N_CHIP, fb, D_MODEL, comm=_sibling_halves([c_gate]))
    c_down, (r_up,) = _dw("dw_down", act, hid, df, whole(D_MODEL), N_CHIP, fb, D_MODEL, comm=_sibling_halves([c_up]))
    ((pb_gate, pb_up), (own_gate, own_up)), (r_down,) = _chip_partials(
        "partials_gate_up", place, [c_gate, c_up], [r_gate, r_up], comm=_sibling_halves([c_down]))
    ((pb_down,), (own_down,)), _ = _chip_partials("partials_down", place, [c_down], [r_down])

    (gx, h1, du, dy, sw, dyc, dyp, vec_d, vec_c, ddw), (rc_gate, rc_up, rc_down) = _mixer_bwd(
        dx2, xs, y, u, z, rstd, p, mod, g_norm1, win_g, dww_full, conv_ln_g, conv_ln_b, w_pw, wg_b, pool_scale, w_o,
        comm=_exchange_partials([pb_gate, pb_up, pb_down]))

    g_wg, _ = _dw("dw_wg", p, cols(POOL_G), dyp, cols(POOL_G), len(POOL_WINDOWS), POOL_G, POOL_G)
    c_out, _ = _dw("dw_out", ycat, cols(D_MODEL // N_CHIP), dy, whole(D_MODEL), N_CHIP, D_MODEL // N_CHIP, D_MODEL)
    c_pw, _ = _dw("dw_pw", sw, cols(CONV_W // N_CHIP), dyc, whole(CONV_W), N_CHIP, CONV_W // N_CHIP, CONV_W)
    c_in, (vf_all, vd_all, vc_all, ddw_all, gwg_all) = _dw(
        "dw_in", h1, whole(D_MODEL), du, cols(IN_W // N_CHIP), N_CHIP, D_MODEL, IN_W // N_CHIP,
        comm=_small_gather([vec_f, vec_d, vec_c, ddw, g_wg]))
    mix = [c_in, c_pw, c_out]
    rs_mix = _comm_only("sibling_mix", _sibling_halves(mix))
    (pbs_mix, owns_mix), _ = _chip_partials("partials_mix", place, mix, rs_mix)
    rc_mix = _comm_only("exchange_mix", _exchange_partials(pbs_mix))

    fulls = _sum_partials(place, list(owns_mix) + [own_gate, own_up, own_down],
                          list(rc_mix) + [rc_gate, rc_up, rc_down])
    g_in, g_pw, g_out, g_gate, g_up, g_down = _comm_only("join_halves", _join_halves(fulls))

    pad_rows = lambda a: jnp.pad(a[0], ((0, 1), (0, 0)))
    row = lambda a: a.reshape(1, -1)
    small = [(b_ada, m_b_ada, v_b_ada), (g_norm1, m_g_norm1, v_g_norm1),
             (pad_rows(dw_w), pad_rows(m_dw_w), pad_rows(v_dw_w)), (dw_b, m_dw_b, v_dw_b),
             (conv_ln_g, m_conv_ln_g, v_conv_ln_g), (conv_ln_b, m_conv_ln_b, v_conv_ln_b),
             (w_pool_group[0], m_w_pool_group[0], v_w_pool_group[0]), (pool_scale, m_pool_scale, v_pool_scale),
             (g_norm2, m_g_norm2, v_g_norm2), (row(g_final), row(m_g_final), row(v_g_final))]
    loss, dmod, small_out = _adamw_small(place, vf_all, vd_all, vc_all, ddw_all, gwg_all, small)
    (o_bada, o_g1, o_dww, o_dwb, o_lng, o_lnb, o_wg, o_ps, o_g2, o_gf) = small_out
    o_dww = [a[:CONV_K] for a in o_dww]
    o_gf = [a.reshape(D_MODEL) for a in o_gf]
    lead = lambda outs: [a[None] for a in outs]

    o_ada = _adamw_ada(place, cact, dmod, w_ada[0], m_w_ada[0], v_w_ada[0])
    upd = lambda name, w, g, m, v: [g] + list(_adamw(name, w[0], g, m[0], v[0]))
    o_in = upd("adamw_in", w_in, g_in, m_w_in, v_w_in)
    o_pw = upd("adamw_pw", w_conv_pw, g_pw, m_w_conv_pw, v_w_conv_pw)
    o_out = upd("adamw_out", w_out, g_out, m_w_out, v_w_out)
    upd_t = lambda name, w, g, m, v: [jnp.transpose(o) for o in [g] + list(_adamw(name, tr(w), g, tr(m), tr(v)))]
    o_gate = upd_t("adamw_gate", w_ffn_gate, g_gate, m_w_ffn_gate, v_w_ffn_gate)
    o_up = upd_t("adamw_up", w_ffn_up, g_up, m_w_ffn_up, v_w_ffn_up)
    o_down = upd("adamw_down", w_ffn_down, g_down, m_w_ffn_down, v_w_ffn_down)

    per_weight = [lead(o_ada), o_bada, o_g1, lead(o_in), lead(o_dww), o_dwb, o_lng, o_lnb, lead(o_pw), lead(o_wg),
                  o_ps, lead(o_out), o_g2, lead(o_gate), lead(o_up), lead(o_down), o_gf]
    result = [loss.reshape(()), gx[None]]
    for kind in range(4):
        result += [o[kind] for o in per_weight]
    return tuple(result)
```

```python
import functools

import jax
import jax.numpy as jnp
from jax import lax
from jax.experimental import pallas as pl
from jax.experimental.pallas import tpu as pltpu

F32 = jnp.float32
MXU_DTYPE = jnp.bfloat16
EPS = 1e-6

D_MODEL = 1024
CONV_W = 512
POOL_W = 512
CONV_K = 31
POOL_WINDOWS = (2, 4, 8, 16)
POOL_G = 128
IN_W = 2 * CONV_W + POOL_W
N_CHIP = 4
N_DEV = 8
CONV_HALO = 32
POOL_HALO = 16

ADAM_LR = 0.001
ADAM_B1 = 0.9
ADAM_B2 = 0.999
ADAM_EPS = 1e-08
ADAM_WD = 0.01
ADAM_STEP = 10

MESH = pl.DeviceIdType.MESH
ANY = pl.BlockSpec(memory_space=pl.ANY)
VMEM = pl.BlockSpec(memory_space=pltpu.VMEM)


def _dot(a, b):
    return jnp.dot(a.astype(MXU_DTYPE), b.astype(MXU_DTYPE), preferred_element_type=F32)


def _dot_nt(a, b):
    return lax.dot_general(a.astype(MXU_DTYPE), b.astype(MXU_DTYPE), (((1,), (1,)), ((), ())),
                           preferred_element_type=F32)


def _dot_tn(a, b):
    return lax.dot_general(a.astype(MXU_DTYPE), b.astype(MXU_DTYPE), (((0,), (0,)), ((), ())),
                           preferred_element_type=F32)


def _sigmoid(v):
    return 1.0 / (1.0 + jnp.exp(-v))


def _full(shape):
    n = len(shape)
    return pl.BlockSpec(shape, lambda *_: (0,) * n)


def _token_tile(s):
    return 256 if s % 256 == 0 else s


def _place():
    return lax.axis_index("x"), lax.axis_index("y"), lax.axis_index("c")


def _flip(x, y, r):
    return ((1 - x) if r & 2 else x, (1 - y) if r & 1 else y)


def _remote(src, dst, send_sem, recv_sem, dev):
    return pltpu.make_async_remote_copy(src_ref=src, dst_ref=dst, send_sem=send_sem, recv_sem=recv_sem,
                                        device_id=dev, device_id_type=MESH)


class _Comm:
    def __init__(self, ins, outs, aliases, scratch, start, finish):
        self.ins, self.outs, self.aliases, self.scratch = list(ins), list(outs), dict(aliases), list(scratch)
        self.start, self.finish = start, finish


def _both(a, b):
    na, nao, nas = len(a.ins), len(a.outs), len(a.scratch)
    aliases = dict(a.aliases)
    aliases.update({na + i: nao + o for i, o in b.aliases.items()})

    def start(ins, outs, scr):
        a.start(ins[:na], outs[:nao], scr[:nas])
        b.start(ins[na:], outs[nao:], scr[nas:])

    def finish(ins, outs, scr):
        a.finish(ins[:na], outs[:nao], scr[:nas])
        b.finish(ins[na:], outs[nao:], scr[nas:])

    return _Comm(a.ins + b.ins, a.outs + b.outs, aliases, a.scratch + b.scratch, start, finish)


def _call(body, *, name, grid, in_specs, out_specs, out_shape, args, scratch_shapes=(), prefetch=(), comm=None,
          body_starts=False):
    n_pre, n_in, n_out, n_scr = len(prefetch), len(in_specs), len(out_specs), len(scratch_shapes)
    c_ins = comm.ins if comm else []
    c_outs = comm.outs if comm else []
    c_scr = comm.scratch if comm else []
    last = grid[0] - 1

    def wrapped(*refs):
        pre, refs = refs[:n_pre], refs[n_pre:]
        ins, cin = refs[:n_in], refs[n_in:n_in + len(c_ins)]
        refs = refs[n_in + len(c_ins):]
        outs, cout = refs[:n_out], refs[n_out:n_out + len(c_outs)]
        refs = refs[n_out + len(c_outs):]
        scr, cscr = refs[:n_scr], refs[n_scr:]
        step = pl.program_id(0)
        if comm and not body_starts:
            @pl.when(step == 0)
            def _():
                comm.start(cin, cout, cscr)

        if body_starts:
            body(lambda: comm.start(cin, cout, cscr) if comm else None, *pre, *ins, *outs, *scr)
        else:
            body(*pre, *ins, *outs, *scr)
        if comm:
            @pl.when(step == last)
            def _():
                comm.finish(cin, cout, cscr)

    aliases = {n_pre + n_in + a: n_out + b for a, b in (comm.aliases if comm else {}).items()}
    res = pl.pallas_call(
        wrapped, name=name,
        grid_spec=pltpu.PrefetchScalarGridSpec(
            num_scalar_prefetch=n_pre, grid=grid, in_specs=list(in_specs) + [ANY] * len(c_ins),
            out_specs=list(out_specs) + [ANY] * len(c_outs), scratch_shapes=list(scratch_shapes) + list(c_scr)),
        out_shape=list(out_shape) + list(c_outs),
        input_output_aliases=aliases,
        compiler_params=pltpu.CompilerParams(dimension_semantics=("arbitrary",)),
    )(*prefetch, *args, *c_ins)
    return res[:n_out], res[n_out:]


def _comm_only(name, comm):
    return _call(lambda: None, name=name, grid=(1,), in_specs=[], out_specs=[], out_shape=[], args=[], comm=comm)[1]


def _gather_phases(make_items):
    def own_sends(items, send, recv):
        x, y, cc = _place()
        j = 2 * x + y
        cps = []
        for a, it in enumerate(items):
            if it["sibling"]:
                cps.append(_remote(it["src"], it["dst"](j, cc), send.at[a, 0], recv.at[a, 0], (x, y, 1 - cc)))
            for r in range(1, N_CHIP):
                kx, ky = _flip(x, y, r)
                cps.append(_remote(it["src"], it["dst"](j, cc), send.at[a, r], recv.at[a, r], (kx, ky, cc)))
        return cps

    def start(ins, outs, scr):
        items = make_items(ins, outs, scr)
        send, recv, lsem = scr[-3:]
        for a, it in enumerate(items):
            if it["local"] is not None:
                src, stage, dst = it["local"]
                lc = pltpu.make_async_copy(src, stage, lsem.at[a])
                lc.start()
                lc.wait()
                pltpu.make_async_copy(stage, dst, lsem.at[a]).start()
        for cp in own_sends(items, send, recv):
            cp.start()

    def finish(ins, outs, scr):
        items = make_items(ins, outs, scr)
        send, recv, lsem = scr[-3:]
        x, y, cc = _place()
        j = 2 * x + y
        sib = (x, y, 1 - cc)
        forwards = []
        for a, it in enumerate(items):
            for r in range(1, N_CHIP):
                kx, ky = _flip(x, y, r)
                got = it["dst"](2 * kx + ky, cc)
                _remote(got, got, send.at[a, r], recv.at[a, r], sib).wait_recv()
                if it["forward"]:
                    cp = _remote(got, got, send.at[a, 3 + r], recv.at[a, 3 + r], sib)
                    cp.start()
                    forwards.append(cp)
        for a, it in enumerate(items):
            if it["sibling"]:
                got = it["dst"](j, 1 - cc)
                _remote(got, got, send.at[a, 0], recv.at[a, 0], sib).wait_recv()
            if it["forward"]:
                for r in range(1, N_CHIP):
                    kx, ky = _flip(x, y, r)
                    got = it["dst"](2 * kx + ky, 1 - cc)
                    _remote(got, got, send.at[a, 3 + r], recv.at[a, 3 + r], sib).wait_recv()
        for cp in own_sends(items, send, recv) + forwards:
            cp.wait_send()
        for a, it in enumerate(items):
            if it["local"] is not None:
                src, stage, dst = it["local"]
                pltpu.make_async_copy(stage, dst, lsem.at[a]).wait()

    return start, finish


def _gather_sems(n):
    return [pltpu.SemaphoreType.DMA((n, 7)), pltpu.SemaphoreType.DMA((n, 7)), pltpu.SemaphoreType.DMA((n,))]


def _weights_gather(bufs, split):
    def make_items(ins, outs, scr):
        x, y, cc = _place()
        j = 2 * x + y
        items = []
        for a, buf in enumerate(bufs):
            if split[a]:
                h = buf.shape[1] // 2
                dst = functools.partial(lambda o, h, kj, pc: o.at[kj, pl.ds(pc * h, h), :], outs[a], h)
            else:
                dst = functools.partial(lambda o, kj, pc: o.at[kj], outs[a])
            items.append(dict(src=dst(j, cc), dst=dst, local=None, sibling=False, forward=split[a]))
        return items

    start, finish = _gather_phases(make_items)
    n = len(bufs)
    return _Comm(bufs, [jax.ShapeDtypeStruct(b.shape, b.dtype) for b in bufs], {i: i for i in range(n)},
                 _gather_sems(n), start, finish)


def _small_gather(arrs):
    n = len(arrs)

    def make_items(ins, outs, scr):
        x, y, cc = _place()
        items = []
        for a in range(n):
            dst = functools.partial(lambda o, kj, pc: o.at[2 * kj + pc], outs[a])
            items.append(dict(src=ins[a], dst=dst, local=(ins[a], scr[a], outs[a].at[4 * x + 2 * y + cc]),
                              sibling=True, forward=True))
        return items

    start, finish = _gather_phases(make_items)
    return _Comm(arrs, [jax.ShapeDtypeStruct((N_DEV,) + a.shape, a.dtype) for a in arrs], {},
                 [pltpu.VMEM(a.shape, a.dtype) for a in arrs] + _gather_sems(n), start, finish)


def _direct_phases(copies):
    def start(ins, outs, scr):
        for cp in copies(ins, outs, *scr):
            cp.start()

    def finish(ins, outs, scr):
        cps = copies(ins, outs, *scr)
        for cp in cps:
            cp.wait_recv()
        for cp in cps:
            cp.wait_send()

    return start, finish


def _sibling_halves(gs):
    n = len(gs)

    def copies(ins, outs, send, recv):
        x, y, cc = _place()
        cps = []
        for a in range(n):
            h = gs[a].shape[1] // 2
            cps.append(_remote(ins[a].at[:, pl.ds((1 - cc) * h, h), :], outs[a], send.at[a], recv.at[a],
                               (x, y, 1 - cc)))
        return cps

    start, finish = _direct_phases(copies)
    return _Comm(gs, [jax.ShapeDtypeStruct((N_CHIP, g.shape[1] // 2, g.shape[2]), F32) for g in gs], {},
                 [pltpu.SemaphoreType.DMA((n,)), pltpu.SemaphoreType.DMA((n,))], start, finish)


def _exchange_partials(pbs):
    n = len(pbs)

    def copies(ins, outs, send, recv):
        x, y, cc = _place()
        cps = []
        for a in range(n):
            for r in range(1, N_CHIP):
                kx, ky = _flip(x, y, r)
                cps.append(_remote(ins[a].at[2 * kx + ky], outs[a].at[r - 1], send.at[a, r - 1], recv.at[a, r - 1],
                                   (kx, ky, cc)))
        return cps

    start, finish = _direct_phases(copies)
    return _Comm(pbs, [jax.ShapeDtypeStruct((N_CHIP - 1,) + p.shape[1:], p.dtype) for p in pbs], {},
                 [pltpu.SemaphoreType.DMA((n, N_CHIP - 1)), pltpu.SemaphoreType.DMA((n, N_CHIP - 1))], start, finish)


def _join_halves(fulls):
    n = len(fulls)

    def copies(ins, outs, send, recv):
        x, y, cc = _place()
        cps = []
        for a in range(n):
            h = fulls[a].shape[0] // 2
            mine = outs[a].at[pl.ds(cc * h, h), :]
            cps.append(_remote(mine, mine, send.at[a], recv.at[a], (x, y, 1 - cc)))
        return cps

    start, finish = _direct_phases(copies)
    return _Comm(fulls, [jax.ShapeDtypeStruct(f.shape, F32) for f in fulls], {i: i for i in range(n)},
                 [pltpu.SemaphoreType.DMA((n,)), pltpu.SemaphoreType.DMA((n,))], start, finish)


def _cast_weights(place, shards, dww, wg):
    n = len(shards)

    def body(pref, *refs):
        ins, outs = refs[:n + 2], refs[n + 2:]
        for a in range(n):
            outs[a][0] = ins[a][...].astype(MXU_DTYPE)
        outs[n][0] = ins[n][...]
        outs[n + 1][...] = ins[n + 1][...].astype(MXU_DTYPE)

    full = lambda a: pl.BlockSpec(a.shape, lambda i, pref: (0,) * a.ndim)
    slot = lambda a: pl.BlockSpec((1,) + a.shape, lambda i, pref: (pref[1],) + (0,) * a.ndim)
    arrs = list(shards) + [dww, wg]
    return pl.pallas_call(
        body, name="cast_weights",
        grid_spec=pltpu.PrefetchScalarGridSpec(
            num_scalar_prefetch=1, grid=(1,), in_specs=[full(a) for a in arrs],
            out_specs=[slot(a) for a in arrs[:n + 1]] + [full(wg)]),
        out_shape=[jax.ShapeDtypeStruct((N_CHIP,) + a.shape, MXU_DTYPE) for a in shards]
        + [jax.ShapeDtypeStruct((N_CHIP,) + dww.shape, F32), jax.ShapeDtypeStruct(wg.shape, MXU_DTYPE)],
        compiler_params=pltpu.CompilerParams(dimension_semantics=("arbitrary",)),
    )(place, *arrs)


def _mixer_fwd(x, mod, g1, w_in, dww, dwb, lng, lnb, w_pw, wg, pscale, w_out, comm=None):
    s = x.shape[0]
    ts = _token_tile(s)
    nt = s // ts

    def body(x_ref, mod_ref, g1_ref, win_ref, dww_ref, dwb_ref, lng_ref, lnb_ref, wpw_ref, wg_ref, ps_ref,
             wout_ref, x2_ref, y_ref, u_ref, z_ref, rstd_ref, p_ref, ycat_ref, gpad, vpad):
        i = pl.program_id(0)

        @pl.when(i == 0)
        def _():
            gpad[0:CONV_HALO, :] = jnp.zeros((CONV_HALO, CONV_W), F32)
            vpad[0:POOL_HALO, :] = jnp.zeros((POOL_HALO, POOL_W), F32)

        xt = x_ref[...]
        sh1 = mod_ref[0:1, :]
        sc1 = mod_ref[1:2, :]
        gt1 = mod_ref[2:3, :]
        r1 = lax.rsqrt(jnp.mean(xt * xt, axis=-1, keepdims=True) + EPS)
        h1 = (xt * r1 * g1_ref[...]) * (1.0 + sc1) + sh1
        h1b = h1.astype(MXU_DTYPE)
        u = jnp.concatenate([_dot(h1b, win_ref[j]) for j in range(N_CHIP)], axis=1)
        u_ref[...] = u
        a = u[:, :CONV_W]
        g = u[:, CONV_W:2 * CONV_W]
        v = u[:, 2 * CONV_W:]

        gpad[CONV_HALO:CONV_HALO + ts, :] = a * _sigmoid(g)
        cv = jnp.broadcast_to(dwb_ref[...], (ts, CONV_W))
        off = CONV_HALO - (CONV_K - 1)
        for k in range(CONV_K):
            cv = cv + dww_ref[k:k + 1, :] * gpad[off + k:off + k + ts, :]
        gpad[0:CONV_HALO, :] = gpad[ts:ts + CONV_HALO, :]

        mu = jnp.mean(cv, axis=-1, keepdims=True)
        cc = cv - mu
        rstd = lax.rsqrt(jnp.mean(cc * cc, axis=-1, keepdims=True) + EPS)
        z = cc * rstd
        z_ref[...] = z
        rstd_ref[...] = rstd
        ln = z * lng_ref[...] + lnb_ref[...]
        sw = ln * _sigmoid(ln)
        yconv = _dot(sw, wpw_ref[...])

        vpad[POOL_HALO:POOL_HALO + ts, :] = v
        t = i * ts + lax.broadcasted_iota(jnp.int32, (ts, 1), 0)
        ps, ypool = [], []
        for gi, w in enumerate(POOL_WINDOWS):
            cols = slice(gi * POOL_G, (gi + 1) * POOL_G)
            acc = vpad[POOL_HALO:POOL_HALO + ts, cols]
            for d in range(1, w):
                acc = acc + vpad[POOL_HALO - d:POOL_HALO - d + ts, cols]
            cnt = jnp.minimum(t + 1, w).astype(F32)
            pg = (acc / cnt - v[:, cols]).astype(MXU_DTYPE)
            ps.append(pg)
            ypool.append(_dot(pg, wg_ref[gi]))
        vpad[0:POOL_HALO, :] = vpad[ts:ts + POOL_HALO, :]
        p_ref[...] = jnp.concatenate(ps, axis=1)
        ypool = jnp.concatenate(ypool, axis=1) * ps_ref[...]

        ycat = jnp.concatenate([yconv, ypool], axis=1).astype(MXU_DTYPE)
        ycat_ref[...] = ycat
        y = _dot(ycat, wout_ref[...])
        y_ref[...] = y
        x2_ref[...] = xt + gt1 * y

    tile = lambda w: pl.BlockSpec((ts, w), lambda i: (i, 0))
    return _call(
        body, name="mixer_fwd", grid=(nt,),
        in_specs=[tile(D_MODEL), _full(mod.shape), _full(g1.shape), _full(w_in.shape), _full(dww.shape),
                  _full(dwb.shape), _full(lng.shape), _full(lnb.shape), _full(w_pw.shape), _full(wg.shape),
                  _full(pscale.shape), _full(w_out.shape)],
        out_specs=[tile(D_MODEL), tile(D_MODEL), tile(IN_W), tile(CONV_W), tile(1), tile(POOL_W), tile(D_MODEL)],
        out_shape=[jax.ShapeDtypeStruct((s, D_MODEL), F32), jax.ShapeDtypeStruct((s, D_MODEL), F32),
                   jax.ShapeDtypeStruct((s, IN_W), F32), jax.ShapeDtypeStruct((s, CONV_W), F32),
                   jax.ShapeDtypeStruct((s, 1), F32), jax.ShapeDtypeStruct((s, POOL_W), MXU_DTYPE),
                   jax.ShapeDtypeStruct((s, D_MODEL), MXU_DTYPE)],
        scratch_shapes=[pltpu.VMEM((ts + CONV_HALO, CONV_W), F32), pltpu.VMEM((ts + POOL_HALO, POOL_W), F32)],
        args=(x, mod, g1, w_in, dww, dwb, lng, lnb, w_pw, wg, pscale, w_out), comm=comm)


def _ffn(x2, tgt, mod, g2, gf, w_gate, w_up, w_down):
    s = x2.shape[0]
    ts = _token_tile(s)
    nt = s // ts
    fb = w_gate.shape[1]

    def body(x2_ref, tgt_ref, mod_ref, g2_ref, gf_ref, wgt_ref, wup_ref, wdn_ref,
             dx2_ref, h2_ref, df_ref, act_ref, dgg_ref, duu_ref, vec_ref, gg_s, uu_s):
        i = pl.program_id(0)

        @pl.when(i == 0)
        def _():
            vec_ref[...] = jnp.zeros(vec_ref.shape, F32)

        x2t = x2_ref[...]
        sh2 = mod_ref[3:4, :]
        sc2 = mod_ref[4:5, :]
        gt2 = mod_ref[5:6, :]
        g2v = g2_ref[...]
        gfv = gf_ref[...]
        r2 = lax.rsqrt(jnp.mean(x2t * x2t, axis=-1, keepdims=True) + EPS)
        xh2 = x2t * r2
        n2 = xh2 * g2v
        h2b = (n2 * (1.0 + sc2) + sh2).astype(MXU_DTYPE)
        h2_ref[...] = h2b
        f = jnp.zeros((ts, D_MODEL), F32)
        for j in range(N_CHIP):
            gg = _dot_nt(h2b, wgt_ref[j])
            uu = _dot_nt(h2b, wup_ref[j])
            gg_s[j] = gg
            uu_s[j] = uu
            actb = (gg * _sigmoid(gg) * uu).astype(MXU_DTYPE)
            act_ref[j] = actb
            f = f + _dot(actb, wdn_ref[j])
        x3 = x2t + gt2 * f
        r3 = lax.rsqrt(jnp.mean(x3 * x3, axis=-1, keepdims=True) + EPS)
        xh3 = x3 * r3
        diff = xh3 * gfv - tgt_ref[...]
        dout = diff * (1.0 / D_MODEL)
        dn3 = dout * gfv
        dx3 = r3 * (dn3 - xh3 * jnp.mean(dn3 * xh3, axis=-1, keepdims=True))
        dfb = (dx3 * gt2).astype(MXU_DTYPE)
        df_ref[...] = dfb
        dh2 = jnp.zeros((ts, D_MODEL), F32)
        for j in range(N_CHIP):
            dact = _dot_nt(dfb, wdn_ref[j])
            gg = gg_s[j]
            uu = uu_s[j]
            sg = _sigmoid(gg)
            duu = (dact * (gg * sg)).astype(MXU_DTYPE)
            dgg = (dact * uu * (sg * (1.0 + gg * (1.0 - sg)))).astype(MXU_DTYPE)
            duu_ref[j] = duu
            dgg_ref[j] = dgg
            dh2 = dh2 + _dot(dgg, wgt_ref[j]) + _dot(duu, wup_ref[j])
        dn2 = dh2 * (1.0 + sc2)
        dxh2 = dn2 * g2v
        dx2_ref[...] = dx3 + r2 * (dxh2 - xh2 * jnp.mean(dxh2 * xh2, axis=-1, keepdims=True))

        col = lambda a: jnp.sum(a, axis=0, keepdims=True)
        vec_ref[0:1, :] += col(dout * xh3)
        vec_ref[1:2, :] += col(dx3 * f)
        vec_ref[2:3, :] += col(dh2)
        vec_ref[3:4, :] += col(dh2 * n2)
        vec_ref[4:5, :] += col(dn2 * xh2)
        vec_ref[5:6, :] += col(diff * diff)

    tile = lambda w: pl.BlockSpec((ts, w), lambda i: (i, 0))
    tile3 = pl.BlockSpec((N_CHIP, ts, fb), lambda i: (0, i, 0))
    once = lambda a: pl.BlockSpec(a.shape, lambda i: (0,) * a.ndim, pipeline_mode=pl.Buffered(1))
    hid = jax.ShapeDtypeStruct((N_CHIP, s, fb), MXU_DTYPE)
    return pl.pallas_call(
        body, name="ffn", grid=(nt,),
        in_specs=[tile(D_MODEL), tile(D_MODEL), _full(mod.shape), _full(g2.shape), _full(gf.shape),
                  once(w_gate), once(w_up), once(w_down)],
        out_specs=[tile(D_MODEL), tile(D_MODEL), tile(D_MODEL), tile3, tile3, tile3, _full((8, D_MODEL))],
        out_shape=[jax.ShapeDtypeStruct((s, D_MODEL), F32), jax.ShapeDtypeStruct((s, D_MODEL), MXU_DTYPE),
                   jax.ShapeDtypeStruct((s, D_MODEL), MXU_DTYPE), hid, hid, hid,
                   jax.ShapeDtypeStruct((8, D_MODEL), F32)],
        scratch_shapes=[pltpu.VMEM((N_CHIP, ts, fb), F32), pltpu.VMEM((N_CHIP, ts, fb), F32)],
        compiler_params=pltpu.CompilerParams(dimension_semantics=("arbitrary",)),
    )(x2, tgt, mod, g2, gf, w_gate, w_up, w_down)


def _mixer_bwd(dx2, x, y, u, z, rstd, p, mod, g1, w_in, dww, lng, lnb, w_pw, wg, pscale, w_out, comm=None):
    s = x.shape[0]
    ts = _token_tile(s)
    nt = s // ts

    def body(dx2_ref, x_ref, y_ref, u_ref, z_ref, rstd_ref, p_ref, mod_ref, g1_ref, win_ref, dww_ref, lng_ref,
             lnb_ref, wpw_ref, wg_ref, ps_ref, wout_ref,
             gx_ref, h1_ref, du_ref, dy_ref, sw_ref, dyc_ref, dyp_ref, vd_ref, vc_ref, ddw_ref, dcpad, dppad):
        i = pl.program_id(0)
        tix = nt - 1 - i

        @pl.when(i == 0)
        def _():
            vd_ref[...] = jnp.zeros(vd_ref.shape, F32)
            vc_ref[...] = jnp.zeros(vc_ref.shape, F32)
            ddw_ref[...] = jnp.zeros(ddw_ref.shape, F32)
            dcpad[ts:ts + CONV_HALO, :] = jnp.zeros((CONV_HALO, CONV_W), F32)
            dppad[ts:ts + POOL_HALO, :] = jnp.zeros((POOL_HALO, POOL_W), F32)

        col = lambda a: jnp.sum(a, axis=0, keepdims=True)
        sh1 = mod_ref[0:1, :]
        sc1 = mod_ref[1:2, :]
        gt1 = mod_ref[2:3, :]
        dx2t = dx2_ref[...]
        vd_ref[0:1, :] += col(dx2t * y_ref[...])
        dyb = (dx2t * gt1).astype(MXU_DTYPE)
        dy_ref[...] = dyb
        dycat = _dot_nt(dyb, wout_ref[...])
        dyconv = dycat[:, :CONV_W]
        dypool = dycat[:, CONV_W:]

        pt = p_ref[...]
        t = tix * ts + lax.broadcasted_iota(jnp.int32, (ts, 1), 0)
        psc = ps_ref[...]
        dypb = (dypool * psc).astype(MXU_DTYPE)
        dyp_ref[...] = dypb
        dps, ypre = [], []
        for gi, w in enumerate(POOL_WINDOWS):
            cols = slice(gi * POOL_G, (gi + 1) * POOL_G)
            ypre.append(_dot(pt[:, cols], wg_ref[gi]))
            dpg = _dot_nt(dypb[:, cols], wg_ref[gi])
            dps.append(dpg)
            cnt = jnp.minimum(t + 1, w).astype(F32)
            dppad[0:ts, cols] = dpg / cnt
        vc_ref[0:1, :] += col(dypool * jnp.concatenate(ypre, axis=1))
        dvs = []
        for gi, w in enumerate(POOL_WINDOWS):
            cols = slice(gi * POOL_G, (gi + 1) * POOL_G)
            acc = dppad[0:ts, cols]
            for d in range(1, w):
                acc = acc + dppad[d:d + ts, cols]
            dvs.append(acc - dps[gi])
        dv = jnp.concatenate(dvs, axis=1)
        dppad[ts:ts + POOL_HALO, :] = dppad[0:POOL_HALO, :]

        zt = z_ref[...]
        lngv = lng_ref[...]
        ln = zt * lngv + lnb_ref[...]
        sg = _sigmoid(ln)
        swb = (ln * sg).astype(MXU_DTYPE)
        sw_ref[...] = swb
        dycb = dyconv.astype(MXU_DTYPE)
        dyc_ref[...] = dycb
        dln = _dot_nt(dycb, wpw_ref[...]) * (sg * (1.0 + ln * (1.0 - sg)))
        vc_ref[1:2, :] += col(dln * zt)
        vc_ref[2:3, :] += col(dln)
        dz = dln * lngv
        dcv = rstd_ref[...] * (dz - jnp.mean(dz, axis=-1, keepdims=True)
                               - zt * jnp.mean(dz * zt, axis=-1, keepdims=True))
        vc_ref[3:4, :] += col(dcv)
        dcpad[0:ts, :] = dcv
        ut = u_ref[...]
        a = ut[:, :CONV_W]
        g = ut[:, CONV_W:2 * CONV_W]
        sgg = _sigmoid(g)
        glu = a * sgg
        dglu = jnp.zeros((ts, CONV_W), F32)
        for k in range(CONV_K):
            sh = dcpad[CONV_K - 1 - k:CONV_K - 1 - k + ts, :]
            dglu = dglu + dww_ref[k:k + 1, :] * sh
            ddw_ref[k:k + 1, :] += col(glu * sh)
        dcpad[ts:ts + CONV_HALO, :] = dcpad[0:CONV_HALO, :]
        da = dglu * sgg
        dg = dglu * a * sgg * (1.0 - sgg)
        dub = jnp.concatenate([da, dg, dv], axis=1).astype(MXU_DTYPE)
        du_ref[...] = dub
        cw = IN_W // N_CHIP
        dh1 = jnp.zeros((ts, D_MODEL), F32)
        for j in range(N_CHIP):
            dh1 = dh1 + _dot_nt(dub[:, j * cw:(j + 1) * cw], win_ref[j])

        xt = x_ref[...]
        g1v = g1_ref[...]
        r1 = lax.rsqrt(jnp.mean(xt * xt, axis=-1, keepdims=True) + EPS)
        xh1 = xt * r1
        n1 = xh1 * g1v
        h1_ref[...] = (n1 * (1.0 + sc1) + sh1).astype(MXU_DTYPE)
        vd_ref[1:2, :] += col(dh1)
        vd_ref[2:3, :] += col(dh1 * n1)
        dn1 = dh1 * (1.0 + sc1)
        vd_ref[3:4, :] += col(dn1 * xh1)
        dxh = dn1 * g1v
        gx_ref[...] = dx2t + r1 * (dxh - xh1 * jnp.mean(dxh * xh1, axis=-1, keepdims=True))

    tile = lambda w: pl.BlockSpec((ts, w), lambda i: (nt - 1 - i, 0))
    bf = lambda w: jax.ShapeDtypeStruct((s, w), MXU_DTYPE)
    return _call(
        body, name="mixer_bwd", grid=(nt,),
        in_specs=[tile(D_MODEL), tile(D_MODEL), tile(D_MODEL), tile(IN_W), tile(CONV_W), tile(1), tile(POOL_W),
                  _full(mod.shape), _full(g1.shape), _full(w_in.shape), _full(dww.shape), _full(lng.shape),
                  _full(lnb.shape), _full(w_pw.shape), _full(wg.shape), _full(pscale.shape), _full(w_out.shape)],
        out_specs=[tile(D_MODEL), tile(D_MODEL), tile(IN_W), tile(D_MODEL), tile(CONV_W), tile(CONV_W),
                   tile(POOL_W), _full((8, D_MODEL)), _full((8, CONV_W)), _full((32, CONV_W))],
        out_shape=[jax.ShapeDtypeStruct((s, D_MODEL), F32), bf(D_MODEL), bf(IN_W), bf(D_MODEL), bf(CONV_W),
                   bf(CONV_W), bf(POOL_W), jax.ShapeDtypeStruct((8, D_MODEL), F32),
                   jax.ShapeDtypeStruct((8, CONV_W), F32), jax.ShapeDtypeStruct((32, CONV_W), F32)],
        scratch_shapes=[pltpu.VMEM((ts + CONV_HALO, CONV_W), F32), pltpu.VMEM((ts + POOL_HALO, POOL_W), F32)],
        args=(dx2, x, y, u, z, rstd, p, mod, g1, w_in, dww, lng, lnb, w_pw, wg, pscale, w_out), comm=comm)


def _dw(name, a, a_spec, b, b_spec, nb, mb, nbk, comm=None):
    def body(a_ref, b_ref, o_ref):
        av = a_ref[...]
        bv = b_ref[...]
        av = av.reshape(av.shape[-2:])
        bv = bv.reshape(bv.shape[-2:])
        o_ref[0] = _dot_tn(av, bv)

    (out,), rest = _call(
        body, name=name, grid=(nb,), in_specs=[a_spec, b_spec],
        out_specs=[pl.BlockSpec((1, mb, nbk), lambda j: (j, 0, 0))],
        out_shape=[jax.ShapeDtypeStruct((nb, mb, nbk), F32)], args=(a, b), comm=comm)
    return out, rest


def _ada_fwd(c, w_ada, b4, comm=None):
    nc = w_ada.shape[1]

    def body(start_comm, c_ref, w_ref, b4_ref, mod_ref, cact_ref, call, part, parts, send1, recv1, send2, recv2):
        x, y, cc = _place()
        b = 4 * x + 2 * y + cc
        j = 2 * x + y
        call[b] = c_ref[...]
        sends = []
        for r in range(1, N_DEV):
            dev = ((1 - x) if r & 4 else x, (1 - y) if r & 2 else y, (1 - cc) if r & 1 else cc)
            cp = _remote(call.at[b], call.at[b], send1.at[r - 1], recv1.at[r - 1], dev)
            cp.start()
            sends.append(cp)
        for r in range(1, N_DEV):
            src_b = lax.bitwise_xor(b, r)
            _remote(call.at[src_b], call.at[src_b], send1.at[r - 1], recv1.at[r - 1], (x, y, cc)).wait_recv()
        for cp in sends:
            cp.wait_send()
        start_comm()
        for i in range(N_DEV):
            ci = call[i]
            cact_ref[i:i + 1, :] = ci * _sigmoid(ci)
        part[...] = jnp.dot(cact_ref[...], w_ref[...], preferred_element_type=F32, precision=lax.Precision.HIGHEST)
        sends = []
        for r in range(1, N_CHIP):
            kx, ky = _flip(x, y, r)
            cp = _remote(part, parts.at[j], send2.at[r - 1], recv2.at[r - 1], (kx, ky, cc))
            cp.start()
            sends.append(cp)
        parts[j] = part[...]
        for r in range(1, N_CHIP):
            kx, ky = _flip(x, y, r)
            kj = 2 * kx + ky
            _remote(part, parts.at[kj], send2.at[r - 1], recv2.at[r - 1], (x, y, cc)).wait_recv()
        for cp in sends:
            cp.wait_send()
        mine = lax.broadcasted_iota(jnp.int32, (N_DEV, 1), 0) == b
        for k in range(N_CHIP):
            row = jnp.sum(jnp.where(mine, parts[k], 0.0), axis=0, keepdims=True)
            mod_ref[k:k + 1, :] = row + b4_ref[k:k + 1, :]

    return _call(
        body, name="ada_fwd", grid=(1,),
        in_specs=[VMEM, VMEM, VMEM], out_specs=[VMEM, VMEM],
        out_shape=[jax.ShapeDtypeStruct((N_CHIP, nc), F32), jax.ShapeDtypeStruct((N_DEV, D_MODEL), F32)],
        scratch_shapes=[pltpu.VMEM((N_DEV, 1, D_MODEL), F32), pltpu.VMEM((N_DEV, nc), F32),
                        pltpu.VMEM((N_CHIP, N_DEV, nc), F32),
                        pltpu.SemaphoreType.DMA((N_DEV - 1,)), pltpu.SemaphoreType.DMA((N_DEV - 1,)),
                        pltpu.SemaphoreType.DMA((N_CHIP - 1,)), pltpu.SemaphoreType.DMA((N_CHIP - 1,))],
        args=(c, w_ada, b4), comm=comm, body_starts=True)


def _chip_partials(name, place, gs, rs, comm=None):
    n = len(gs)

    def body(pref, *refs):
        g_refs, r_refs = refs[:n], refs[n:2 * n]
        pb_refs, own_refs = refs[2 * n:3 * n], refs[3 * n:]
        jj = pl.program_id(0)
        for a in range(n):
            sm = g_refs[a][0] + r_refs[a][0]
            pb_refs[a][0] = sm.astype(MXU_DTYPE)

            @pl.when(jj == pref[1])
            def _(a=a, sm=sm):
                own_refs[a][...] = sm

    halves = [(g.shape[1] // 2, g.shape[2]) for g in gs]
    in_specs = [pl.BlockSpec((1, h, w), lambda jj, pref: (jj, pref[0], 0)) for h, w in halves]
    in_specs += [pl.BlockSpec((1, h, w), lambda jj, pref: (jj, 0, 0)) for h, w in halves]
    out_specs = [pl.BlockSpec((1, h, w), lambda jj, pref: (jj, 0, 0)) for h, w in halves]
    out_specs += [pl.BlockSpec((h, w), lambda jj, pref: (0, 0)) for h, w in halves]
    out, rest = _call(
        body, name=name, grid=(N_CHIP,), in_specs=in_specs, out_specs=out_specs,
        out_shape=[jax.ShapeDtypeStruct((N_CHIP, h, w), MXU_DTYPE) for h, w in halves]
        + [jax.ShapeDtypeStruct((h, w), F32) for h, w in halves],
        args=(*gs, *rs), prefetch=(place,), comm=comm)
    return (out[:n], out[n:]), rest


def _sum_partials(name, place, owns, recvd, comm=None):
    n = len(owns)

    def body(pref, *refs):
        o_refs, r_refs, out_refs = refs[:n], refs[n:2 * n], refs[2 * n:]
        for a in range(n):
            acc = o_refs[a][...]
            for r in range(N_CHIP - 1):
                acc = acc + r_refs[a][r].astype(F32)
            out_refs[a][...] = acc

    full = lambda a: pl.BlockSpec(a.shape, lambda i, pref: (0,) * a.ndim)
    return _call(
        body, name=name, grid=(1,), in_specs=[full(a) for a in list(owns) + list(recvd)],
        out_specs=[pl.BlockSpec(o.shape, lambda i, pref: (pref[0], 0)) for o in owns],
        out_shape=[jax.ShapeDtypeStruct((2 * o.shape[0], o.shape[1]), F32) for o in owns],
        args=(*owns, *recvd), prefetch=(place,), comm=comm)


def _adamw_math(w, g, m, v):
    m = ADAM_B1 * m + (1.0 - ADAM_B1) * g
    v = ADAM_B2 * v + (1.0 - ADAM_B2) * (g * g)
    m_hat = m / (1.0 - ADAM_B1 ** ADAM_STEP)
    v_hat = v / (1.0 - ADAM_B2 ** ADAM_STEP)
    delta = -ADAM_LR * (m_hat / (jnp.sqrt(v_hat) + ADAM_EPS) + ADAM_WD * w)
    return delta, m, v


def _row_tile(rows):
    for t in (512, 352, 256, 128):
        if rows % t == 0:
            return t
    return rows


def _adamw(name, wgmv, steps, comm=None):
    n = len(wgmv)

    def body(*refs):
        ins, outs = refs[:4 * n], refs[4 * n:]
        for i in range(n):
            w_ref, g_ref, m_ref, v_ref = ins[4 * i:4 * i + 4]
            d_ref, nm_ref, nv_ref = outs[3 * i:3 * i + 3]
            d_ref[...], nm_ref[...], nv_ref[...] = _adamw_math(w_ref[...], g_ref[...], m_ref[...], v_ref[...])

    in_specs, out_specs, out_shape, args = [], [], [], []
    for w, g, m, v in wgmv:
        rows, cols = w.shape
        spec = pl.BlockSpec((rows // steps, cols), lambda i: (i, 0))
        in_specs += [spec] * 4
        out_specs += [spec] * 3
        out_shape += [jax.ShapeDtypeStruct(w.shape, F32)] * 3
        args += [w, g, m, v]
    res, rest = _call(body, name=name, grid=(steps,), in_specs=in_specs, out_specs=out_specs, out_shape=out_shape,
                      args=args, comm=comm)
    return [res[3 * i:3 * i + 3] for i in range(n)], rest


def _adamw_ada(place, cact, dmod, w, m, v, comm=None):
    rows, cols = w.shape
    tr = _row_tile(rows)

    def body(pref, ca_ref, dm_ref, w_ref, m_ref, v_ref, g_ref, d_ref, nm_ref, nv_ref):
        g = lax.dot_general(ca_ref[...], dm_ref[...], (((0,), (0,)), ((), ())), preferred_element_type=F32,
                            precision=lax.Precision.HIGHEST)
        g_ref[...] = g
        d_ref[...], nm_ref[...], nv_ref[...] = _adamw_math(w_ref[...], g, m_ref[...], v_ref[...])

    spec = pl.BlockSpec((tr, cols), lambda i, pref: (i, 0))
    return _call(
        body, name="adamw_ada", grid=(rows // tr,),
        in_specs=[pl.BlockSpec((N_DEV, tr), lambda i, pref: (0, i)),
                  pl.BlockSpec((N_DEV, cols), lambda i, pref: (0, pref[1])), spec, spec, spec],
        out_specs=[spec] * 4, out_shape=[jax.ShapeDtypeStruct(w.shape, F32)] * 4,
        args=(cact, dmod, w, m, v), prefetch=(place,), comm=comm)


def _adamw_small(place, vf_all, vd_all, vc_all, ddw_all, gwg_all, wmv):
    nw = len(wmv)
    flat = [a for t in wmv for a in t]

    def body(pref, vf_ref, vd_ref, vc_ref, ddw_ref, gwg_ref, *refs):
        w_refs = refs[:3 * nw]
        loss_ref, dmod_ref = refs[3 * nw], refs[3 * nw + 1]
        o_refs = refs[3 * nw + 2:]
        j = pref[1]

        def total(ref):
            acc = ref[0]
            for b in range(1, N_DEV):
                acc = acc + ref[b]
            return acc

        vf, vd, vc, ddw, gwg = total(vf_ref), total(vd_ref), total(vc_ref), total(ddw_ref), total(gwg_ref)
        loss_ref[...] = (0.5 / D_MODEL) * jnp.sum(vf[5:6, :], axis=1, keepdims=True)
        order = ((vd_ref, 1), (vd_ref, 2), (vd_ref, 0), (vf_ref, 2), (vf_ref, 3), (vf_ref, 1))
        for b in range(N_DEV):
            for q, (ref, row) in enumerate(order):
                dmod_ref[b:b + 1, q * D_MODEL:(q + 1) * D_MODEL] = ref[b, row:row + 1, :]
        dm = dmod_ref[...]
        g_bada = dm[0:1, :]
        for b in range(1, N_DEV):
            g_bada = g_bada + dm[b:b + 1, :]
        g_dww = jnp.zeros((32, POOL_G), F32)
        for k in range(N_CHIP):
            g_dww = g_dww + jnp.where(j == k, ddw[:, k * POOL_G:(k + 1) * POOL_G], 0.0)
        grads = [g_bada, vd[3:4, :], g_dww, vc[3:4, :], vc[1:2, :], vc[2:3, :], gwg, vc[0:1, :], vf[4:5, :],
                 vf[0:1, :]]
        for i, g in enumerate(grads):
            w_ref, m_ref, v_ref = w_refs[3 * i:3 * i + 3]
            d, nm, nv = _adamw_math(w_ref[...], g, m_ref[...], v_ref[...])
            o_refs[4 * i][...] = g
            o_refs[4 * i + 1][...] = d
            o_refs[4 * i + 2][...] = nm
            o_refs[4 * i + 3][...] = nv

    gathered = [vf_all, vd_all, vc_all, ddw_all, gwg_all]
    outs = [jax.ShapeDtypeStruct((1, 1), F32), jax.ShapeDtypeStruct((N_DEV, 6 * D_MODEL), F32)]
    for w, _, _ in wmv:
        outs += [jax.ShapeDtypeStruct(w.shape, F32)] * 4
    full = lambda a: pl.BlockSpec(a.shape, lambda i, pref: (0,) * a.ndim)
    res = pl.pallas_call(
        body, name="adamw_small",
        grid_spec=pltpu.PrefetchScalarGridSpec(
            num_scalar_prefetch=1, grid=(1,),
            in_specs=[full(a) for a in gathered + flat], out_specs=[full(o) for o in outs]),
        out_shape=outs,
        compiler_params=pltpu.CompilerParams(dimension_semantics=("arbitrary",)),
    )(place, *gathered, *flat)
    return res[0], res[1], [res[2 + 4 * i:6 + 4 * i] for i in range(nw)]


def kernel(x, c, w_ada, b_ada, g_norm1, w_in, dw_w, dw_b, conv_ln_g, conv_ln_b, w_conv_pw, w_pool_group, pool_scale, w_out, g_norm2, w_ffn_gate, w_ffn_up, w_ffn_down, g_final, loss_target, m_w_ada, m_b_ada, m_g_norm1, m_w_in, m_dw_w, m_dw_b, m_conv_ln_g, m_conv_ln_b, m_w_conv_pw, m_w_pool_group, m_pool_scale, m_w_out, m_g_norm2, m_w_ffn_gate, m_w_ffn_up, m_w_ffn_down, m_g_final, v_w_ada, v_b_ada, v_g_norm1, v_w_in, v_dw_w, v_dw_b, v_conv_ln_g, v_conv_ln_b, v_w_conv_pw, v_w_pool_group, v_pool_scale, v_w_out, v_g_norm2, v_w_ffn_gate, v_w_ffn_up, v_w_ffn_down, v_g_final):
    xi, yi, ci = _place()
    place = jnp.stack([ci, 2 * xi + yi]).astype(jnp.int32)
    n_ada = w_ada.shape[2]

    tr = lambda a: jnp.transpose(a[0])
    big = [w_in[0], w_conv_pw[0], w_out[0], tr(w_ffn_gate), tr(w_ffn_up), w_ffn_down[0]]
    b_in, b_pw, b_out, b_gate, b_up, b_down, b_dww, wg_b = _cast_weights(place, big, dw_w[0], w_pool_group[0])

    (mod4, cact), (win_g, wpw_g, wout_g, dww_g) = _ada_fwd(
        c, w_ada[0], b_ada.reshape(N_CHIP, n_ada),
        comm=_weights_gather([b_in, b_pw, b_out, b_dww], [True, True, True, False]))
    mod = mod4.reshape(6, D_MODEL)
    dww_full = jnp.pad(jnp.concatenate([dww_g[k] for k in range(N_CHIP)], axis=1), ((0, 1), (0, 0)))
    w_pw = wpw_g.reshape(CONV_W, CONV_W)
    w_o = wout_g.reshape(D_MODEL, D_MODEL)
    xs, tgt, gf = x[0], loss_target[0], g_final.reshape(1, D_MODEL)
    s = xs.shape[0]
    fb = b_gate.shape[1]

    (x2, y, u, z, rstd, p, ycat), (wgate_g, wup_g, wdown_g) = _mixer_fwd(
        xs, mod, g_norm1, win_g, dww_full, dw_b, conv_ln_g, conv_ln_b, w_pw, wg_b, pool_scale, w_o,
        comm=_weights_gather([b_gate, b_up, b_down], [True, True, True]))
    dx2, h2, df, act, dgg, duu, vec_f = _ffn(x2, tgt, mod, g_norm2, gf, wgate_g, wup_g, wdown_g)

    whole = lambda w: pl.BlockSpec((s, w), lambda j: (0, 0))
    cols = lambda w: pl.BlockSpec((s, w), lambda j: (0, j))
    hid = pl.BlockSpec((1, s, fb), lambda j: (j, 0, 0))
    c_gate, _ = _dw("dw_gate", dgg, hid, h2, whole(D_MODEL), N_CHIP, fb, D_MODEL)
    c_up, (r_gate,) = _dw("dw_up", duu, hid, h2, whole(D_MODEL), N_CHIP, fb, D_MODEL, comm=_sibling_halves([c_gate]))
    ((pb_gate,), (own_gate,)), _ = _chip_partials("partials_gate", place, [c_gate], [r_gate])
    c_down, (r_up, rc_gate) = _dw("dw_down", act, hid, df, whole(D_MODEL), N_CHIP, fb, D_MODEL,
                                  comm=_both(_sibling_halves([c_up]), _exchange_partials([pb_gate])))
    ((pb_up,), (own_up,)), (r_down,) = _chip_partials("partials_up", place, [c_up], [r_up],
                                                      comm=_sibling_halves([c_down]))
    ((pb_down,), (own_down,)), _ = _chip_partials("partials_down", place, [c_down], [r_down])
    (gx, h1, du, dy, sw, dyc, dyp, vec_d, vec_c, ddw), (rc_up, rc_down) = _mixer_bwd(
        dx2, xs, y, u, z, rstd, p, mod, g_norm1, win_g, dww_full, conv_ln_g, conv_ln_b, w_pw, wg_b, pool_scale, w_o,
        comm=_exchange_partials([pb_up, pb_down]))

    g_wg, _ = _dw("dw_wg", p, cols(POOL_G), dyp, cols(POOL_G), len(POOL_WINDOWS), POOL_G, POOL_G)
    c_in, (vf_all, vd_all, vc_all, ddw_all, gwg_all) = _dw(
        "dw_in", h1, whole(D_MODEL), du, cols(IN_W // N_CHIP), N_CHIP, D_MODEL, IN_W // N_CHIP,
        comm=_small_gather([vec_f, vec_d, vec_c, ddw, g_wg]))
    c_out, (r_in,) = _dw("dw_out", ycat, cols(D_MODEL // N_CHIP), dy, whole(D_MODEL), N_CHIP, D_MODEL // N_CHIP,
                         D_MODEL, comm=_sibling_halves([c_in]))
    c_pw, (r_out,) = _dw("dw_pw", sw, cols(CONV_W // N_CHIP), dyc, whole(CONV_W), N_CHIP, CONV_W // N_CHIP, CONV_W,
                         comm=_sibling_halves([c_out]))

    ffn_fulls, (r_pw,) = _sum_partials("sum_ffn", place, [own_gate, own_up, own_down], [rc_gate, rc_up, rc_down],
                                       comm=_sibling_halves([c_pw]))
    (pbs_mix, owns_mix), (g_gate, g_up, g_down) = _chip_partials(
        "partials_mix", place, [c_in, c_pw, c_out], [r_in, r_pw, r_out], comm=_join_halves(ffn_fulls))

    pad_rows = lambda a: jnp.pad(a[0], ((0, 1), (0, 0)))
    row = lambda a: a.reshape(1, -1)
    small = [(b_ada, m_b_ada, v_b_ada), (g_norm1, m_g_norm1, v_g_norm1),
             (pad_rows(dw_w), pad_rows(m_dw_w), pad_rows(v_dw_w)), (dw_b, m_dw_b, v_dw_b),
             (conv_ln_g, m_conv_ln_g, v_conv_ln_g), (conv_ln_b, m_conv_ln_b, v_conv_ln_b),
             (w_pool_group[0], m_w_pool_group[0], v_w_pool_group[0]), (pool_scale, m_pool_scale, v_pool_scale),
             (g_norm2, m_g_norm2, v_g_norm2), (row(g_final), row(m_g_final), row(v_g_final))]
    loss, dmod, small_out = _adamw_small(place, vf_all, vd_all, vc_all, ddw_all, gwg_all, small)
    (o_bada, o_g1, o_dww, o_dwb, o_lng, o_lnb, o_wg, o_ps, o_g2, o_gf) = small_out
    o_dww = [a[:CONV_K] for a in o_dww]
    o_gf = [a.reshape(D_MODEL) for a in o_gf]
    lead = lambda outs: [a[None] for a in outs]

    (u_gate, u_up, u_down), rc_mix = _adamw(
        "adamw_ffn", [(tr(w_ffn_gate), g_gate, tr(m_w_ffn_gate), tr(v_w_ffn_gate)),
                      (tr(w_ffn_up), g_up, tr(m_w_ffn_up), tr(v_w_ffn_up)),
                      (w_ffn_down[0], g_down, m_w_ffn_down[0], v_w_ffn_down[0])],
        steps=4, comm=_exchange_partials(pbs_mix))
    o_gate = [jnp.transpose(o) for o in [g_gate] + list(u_gate)]
    o_up = [jnp.transpose(o) for o in [g_up] + list(u_up)]
    o_down = [g_down] + list(u_down)
    mix_fulls, _ = _sum_partials("sum_mix", place, owns_mix, rc_mix)
    o_ada, (g_in, g_pw, g_out) = _adamw_ada(place, cact, dmod, w_ada[0], m_w_ada[0], v_w_ada[0],
                                            comm=_join_halves(mix_fulls))
    (u_in, u_pw, u_out), _ = _adamw(
        "adamw_mix", [(w_in[0], g_in, m_w_in[0], v_w_in[0]), (w_conv_pw[0], g_pw, m_w_conv_pw[0], v_w_conv_pw[0]),
                      (w_out[0], g_out, m_w_out[0], v_w_out[0])], steps=4)
    o_in, o_pw, o_out = [g_in] + list(u_in), [g_pw] + list(u_pw), [g_out] + list(u_out)

    per_weight = [lead(o_ada), o_bada, o_g1, lead(o_in), lead(o_dww), o_dwb, o_lng, o_lnb, lead(o_pw), lead(o_wg),
                  o_ps, lead(o_out), o_g2, lead(o_gate), lead(o_up), lead(o_down), o_gf]
    result = [loss.reshape(()), gx[None]]
    for kind in range(4):
        result += [o[kind] for o in per_weight]
    return tuple(result)
```

```python
import functools

import jax
import jax.numpy as jnp
from jax import lax
from jax.experimental import pallas as pl
from jax.experimental.pallas import tpu as pltpu

F32 = jnp.float32
MXU_DTYPE = jnp.bfloat16
EPS = 1e-6

D_MODEL = 1024
CONV_W = 512
POOL_W = 512
CONV_K = 31
POOL_WINDOWS = (2, 4, 8, 16)
POOL_G = 128
IN_W = 2 * CONV_W + POOL_W
N_CHIP = 4
N_DEV = 8
CONV_HALO = 32
POOL_HALO = 16

ADAM_LR = 0.001
ADAM_B1 = 0.9
ADAM_B2 = 0.999
ADAM_EPS = 1e-08
ADAM_WD = 0.01
ADAM_STEP = 10

MESH = pl.DeviceIdType.MESH
ANY = pl.BlockSpec(memory_space=pl.ANY)
VMEM = pl.BlockSpec(memory_space=pltpu.VMEM)


def _dot(a, b):
    return jnp.dot(a.astype(MXU_DTYPE), b.astype(MXU_DTYPE), preferred_element_type=F32)


def _dot_nt(a, b):
    return lax.dot_general(a.astype(MXU_DTYPE), b.astype(MXU_DTYPE), (((1,), (1,)), ((), ())),
                           preferred_element_type=F32)


def _dot_tn(a, b):
    return lax.dot_general(a.astype(MXU_DTYPE), b.astype(MXU_DTYPE), (((0,), (0,)), ((), ())),
                           preferred_element_type=F32)


def _sigmoid(v):
    return 1.0 / (1.0 + jnp.exp(-v))


def _full(shape):
    n = len(shape)
    return pl.BlockSpec(shape, lambda *_: (0,) * n)


def _token_tile(s):
    return 256 if s % 256 == 0 else s


def _place():
    return lax.axis_index("x"), lax.axis_index("y"), lax.axis_index("c")


def _flip(x, y, r):
    return ((1 - x) if r & 2 else x, (1 - y) if r & 1 else y)


def _remote(src, dst, send_sem, recv_sem, dev):
    return pltpu.make_async_remote_copy(src_ref=src, dst_ref=dst, send_sem=send_sem, recv_sem=recv_sem,
                                        device_id=dev, device_id_type=MESH)


class _Comm:
    def __init__(self, ins, outs, aliases, scratch, start, finish):
        self.ins, self.outs, self.aliases, self.scratch = list(ins), list(outs), dict(aliases), list(scratch)
        self.start, self.finish = start, finish


def _both(a, b):
    na, nao, nas = len(a.ins), len(a.outs), len(a.scratch)
    aliases = dict(a.aliases)
    aliases.update({na + i: nao + o for i, o in b.aliases.items()})

    def start(ins, outs, scr):
        a.start(ins[:na], outs[:nao], scr[:nas])
        b.start(ins[na:], outs[nao:], scr[nas:])

    def finish(ins, outs, scr):
        a.finish(ins[:na], outs[:nao], scr[:nas])
        b.finish(ins[na:], outs[nao:], scr[nas:])

    return _Comm(a.ins + b.ins, a.outs + b.outs, aliases, a.scratch + b.scratch, start, finish)


def _call(body, *, name, grid, in_specs, out_specs, out_shape, args, scratch_shapes=(), prefetch=(), comm=None,
          body_starts=False, after=()):
    in_specs = list(in_specs) + [ANY] * len(after)
    args = list(args) + list(after)
    n_pre, n_in, n_out, n_scr = len(prefetch), len(in_specs), len(out_specs), len(scratch_shapes)
    n_body_in = n_in - len(after)
    c_ins = comm.ins if comm else []
    c_outs = comm.outs if comm else []
    c_scr = comm.scratch if comm else []
    last = grid[0] - 1

    def wrapped(*refs):
        pre, refs = refs[:n_pre], refs[n_pre:]
        ins, cin = refs[:n_body_in], refs[n_in:n_in + len(c_ins)]
        refs = refs[n_in + len(c_ins):]
        outs, cout = refs[:n_out], refs[n_out:n_out + len(c_outs)]
        refs = refs[n_out + len(c_outs):]
        scr, cscr = refs[:n_scr], refs[n_scr:]
        step = pl.program_id(0)
        if comm and not body_starts:
            @pl.when(step == 0)
            def _():
                comm.start(cin, cout, cscr)

        if body_starts:
            body(lambda: comm.start(cin, cout, cscr) if comm else None, *pre, *ins, *outs, *scr)
        else:
            body(*pre, *ins, *outs, *scr)
        if comm:
            @pl.when(step == last)
            def _():
                comm.finish(cin, cout, cscr)

    aliases = {n_pre + n_in + a: n_out + b for a, b in (comm.aliases if comm else {}).items()}
    res = pl.pallas_call(
        wrapped, name=name,
        grid_spec=pltpu.PrefetchScalarGridSpec(
            num_scalar_prefetch=n_pre, grid=grid, in_specs=list(in_specs) + [ANY] * len(c_ins),
            out_specs=list(out_specs) + [ANY] * len(c_outs), scratch_shapes=list(scratch_shapes) + list(c_scr)),
        out_shape=list(out_shape) + list(c_outs),
        input_output_aliases=aliases,
        compiler_params=pltpu.CompilerParams(dimension_semantics=("arbitrary",)),
    )(*prefetch, *args, *c_ins)
    return res[:n_out], res[n_out:]


def _comm_only(name, comm):
    return _call(lambda: None, name=name, grid=(1,), in_specs=[], out_specs=[], out_shape=[], args=[], comm=comm)[1]


def _gather_phases(make_items):
    def own_sends(items, send, recv):
        x, y, cc = _place()
        j = 2 * x + y
        cps = []
        for a, it in enumerate(items):
            if it["sibling"]:
                cps.append(_remote(it["src"], it["dst"](j, cc), send.at[a, 0], recv.at[a, 0], (x, y, 1 - cc)))
            for r in range(1, N_CHIP):
                kx, ky = _flip(x, y, r)
                cps.append(_remote(it["src"], it["dst"](j, cc), send.at[a, r], recv.at[a, r], (kx, ky, cc)))
        return cps

    def start(ins, outs, scr):
        items = make_items(ins, outs, scr)
        send, recv, lsem = scr[-3:]
        for a, it in enumerate(items):
            if it["local"] is not None:
                src, stage, dst = it["local"]
                lc = pltpu.make_async_copy(src, stage, lsem.at[a])
                lc.start()
                lc.wait()
                pltpu.make_async_copy(stage, dst, lsem.at[a]).start()
        for cp in own_sends(items, send, recv):
            cp.start()

    def finish(ins, outs, scr):
        items = make_items(ins, outs, scr)
        send, recv, lsem = scr[-3:]
        x, y, cc = _place()
        j = 2 * x + y
        sib = (x, y, 1 - cc)
        forwards = []
        for a, it in enumerate(items):
            for r in range(1, N_CHIP):
                kx, ky = _flip(x, y, r)
                got = it["dst"](2 * kx + ky, cc)
                _remote(got, got, send.at[a, r], recv.at[a, r], sib).wait_recv()
                if it["forward"]:
                    cp = _remote(got, got, send.at[a, 3 + r], recv.at[a, 3 + r], sib)
                    cp.start()
                    forwards.append(cp)
        for a, it in enumerate(items):
            if it["sibling"]:
                got = it["dst"](j, 1 - cc)
                _remote(got, got, send.at[a, 0], recv.at[a, 0], sib).wait_recv()
            if it["forward"]:
                for r in range(1, N_CHIP):
                    kx, ky = _flip(x, y, r)
                    got = it["dst"](2 * kx + ky, 1 - cc)
                    _remote(got, got, send.at[a, 3 + r], recv.at[a, 3 + r], sib).wait_recv()
        for cp in own_sends(items, send, recv) + forwards:
            cp.wait_send()
        for a, it in enumerate(items):
            if it["local"] is not None:
                src, stage, dst = it["local"]
                pltpu.make_async_copy(stage, dst, lsem.at[a]).wait()

    return start, finish


def _gather_sems(n):
    return [pltpu.SemaphoreType.DMA((n, 7)), pltpu.SemaphoreType.DMA((n, 7)), pltpu.SemaphoreType.DMA((n,))]


def _weights_gather(bufs, split):
    def make_items(ins, outs, scr):
        x, y, cc = _place()
        j = 2 * x + y
        items = []
        for a, buf in enumerate(bufs):
            if split[a]:
                h = buf.shape[1] // 2
                dst = functools.partial(lambda o, h, kj, pc: o.at[kj, pl.ds(pc * h, h), :], outs[a], h)
            else:
                dst = functools.partial(lambda o, kj, pc: o.at[kj], outs[a])
            items.append(dict(src=dst(j, cc), dst=dst, local=None, sibling=False, forward=split[a]))
        return items

    start, finish = _gather_phases(make_items)
    n = len(bufs)
    return _Comm(bufs, [jax.ShapeDtypeStruct(b.shape, b.dtype) for b in bufs], {i: i for i in range(n)},
                 _gather_sems(n), start, finish)


def _small_gather(arrs):
    n = len(arrs)

    def make_items(ins, outs, scr):
        x, y, cc = _place()
        items = []
        for a in range(n):
            dst = functools.partial(lambda o, kj, pc: o.at[2 * kj + pc], outs[a])
            items.append(dict(src=ins[a], dst=dst, local=(ins[a], scr[a], outs[a].at[4 * x + 2 * y + cc]),
                              sibling=True, forward=True))
        return items

    start, finish = _gather_phases(make_items)
    return _Comm(arrs, [jax.ShapeDtypeStruct((N_DEV,) + a.shape, a.dtype) for a in arrs], {},
                 [pltpu.VMEM(a.shape, a.dtype) for a in arrs] + _gather_sems(n), start, finish)


HBM = pl.BlockSpec(memory_space=pltpu.HBM)
SEM = pl.BlockSpec(memory_space=pltpu.SEMAPHORE)
DATAFLOW = pltpu.SideEffectType.DATAFLOW_SIDE_EFFECTING


class _SemGrid:
    def __init__(self, refs, cols):
        self.refs, self.cols = refs, cols

    @property
    def at(self):
        return self

    def __getitem__(self, idx):
        return self.refs[idx[0] * self.cols + idx[1]]


def _split_start(name, srcs, lands, sem_shape, copies):
    n, k = len(srcs), len(lands)
    ns = sem_shape[0] * sem_shape[1]

    def body(*refs):
        src_refs, land_refs = refs[:n], refs[n:n + k]
        send = _SemGrid(refs[n + k:n + k + ns], sem_shape[1])
        recv = _SemGrid(refs[n + k + ns:n + k + 2 * ns], sem_shape[1])
        token = refs[-1]
        for cp in copies(src_refs, land_refs, send, recv):
            cp.start()
        token[...] = jnp.zeros(token.shape, F32)

    hbm = lambda a: pltpu.with_memory_space_constraint(a, pltpu.HBM)
    zones = [lax.empty(l.shape, l.dtype) for l in lands]
    out = pl.pallas_call(
        body, name=name,
        out_shape=[pltpu.SemaphoreType.DMA(())] * (2 * ns)
        + [pltpu.HBM(a.shape, a.dtype) for a in list(srcs) + list(lands)] + [jax.ShapeDtypeStruct((8, 128), F32)],
        in_specs=[HBM] * (n + k), out_specs=[SEM] * (2 * ns) + [HBM] * (n + k) + [VMEM],
        input_output_aliases={i: 2 * ns + i for i in range(n + k)},
        compiler_params=pltpu.CompilerParams(has_side_effects=DATAFLOW),
    )(*[hbm(a) for a in srcs], *[hbm(z) for z in zones])
    return out[:-1], out[-1]


def _split_wait(name, state, n, sem_shape, copies, after):
    ns = sem_shape[0] * sem_shape[1]
    sems, bufs = state[:2 * ns], state[2 * ns:]
    k = len(bufs) - n

    def body(*refs):
        src_refs, land_refs = refs[:n], refs[n:n + k]
        send = _SemGrid(refs[n + k:n + k + ns], sem_shape[1])
        recv = _SemGrid(refs[n + k + ns:n + k + 2 * ns], sem_shape[1])
        cps = copies(src_refs, land_refs, send, recv)
        for cp in cps:
            cp.wait_send()
        for cp in cps:
            cp.wait_recv()

    out = pl.pallas_call(
        body, name=name,
        out_shape=[pltpu.HBM(a.shape, a.dtype) for a in bufs],
        in_specs=[HBM] * (n + k) + [SEM] * (2 * ns) + [ANY] * len(after), out_specs=[HBM] * (n + k),
        input_output_aliases={i: i for i in range(n + k)},
        compiler_params=pltpu.CompilerParams(has_side_effects=DATAFLOW),
    )(*bufs, *sems, *after)
    return out[n:]


def _direct_phases(copies):
    def start(ins, outs, scr):
        for cp in copies(ins, outs, *scr):
            cp.start()

    def finish(ins, outs, scr):
        cps = copies(ins, outs, *scr)
        for cp in cps:
            cp.wait_recv()
        for cp in cps:
            cp.wait_send()

    return start, finish


def _sibling_halves(gs):
    n = len(gs)

    def copies(ins, outs, send, recv):
        x, y, cc = _place()
        cps = []
        for a in range(n):
            h = gs[a].shape[1] // 2
            cps.append(_remote(ins[a].at[:, pl.ds((1 - cc) * h, h), :], outs[a], send.at[a], recv.at[a],
                               (x, y, 1 - cc)))
        return cps

    start, finish = _direct_phases(copies)
    return _Comm(gs, [jax.ShapeDtypeStruct((N_CHIP, g.shape[1] // 2, g.shape[2]), F32) for g in gs], {},
                 [pltpu.SemaphoreType.DMA((n,)), pltpu.SemaphoreType.DMA((n,))], start, finish)


def _exchange_parts(pbs):
    n = len(pbs)

    def copies(ins, outs, send, recv):
        x, y, cc = _place()
        cps = []
        for a in range(n):
            for r in range(1, N_CHIP):
                kx, ky = _flip(x, y, r)
                cps.append(_remote(ins[a].at[2 * kx + ky], outs[a].at[r - 1], send.at[a, r - 1], recv.at[a, r - 1],
                                   (kx, ky, cc)))
        return cps

    lands = [jax.ShapeDtypeStruct((N_CHIP - 1,) + p.shape[1:], p.dtype) for p in pbs]
    return lands, (n, N_CHIP - 1), copies


def _exchange_partials(pbs):
    lands, sem_shape, copies = _exchange_parts(pbs)
    start, finish = _direct_phases(copies)
    return _Comm(pbs, lands, {}, [pltpu.SemaphoreType.DMA(sem_shape), pltpu.SemaphoreType.DMA(sem_shape)],
                 start, finish)


def _join_halves(fulls):
    n = len(fulls)

    def copies(ins, outs, send, recv):
        x, y, cc = _place()
        cps = []
        for a in range(n):
            h = fulls[a].shape[0] // 2
            mine = outs[a].at[pl.ds(cc * h, h), :]
            cps.append(_remote(mine, mine, send.at[a], recv.at[a], (x, y, 1 - cc)))
        return cps

    start, finish = _direct_phases(copies)
    return _Comm(fulls, [jax.ShapeDtypeStruct(f.shape, F32) for f in fulls], {i: i for i in range(n)},
                 [pltpu.SemaphoreType.DMA((n,)), pltpu.SemaphoreType.DMA((n,))], start, finish)


def _cast_weights(place, shards, dww, wg):
    n = len(shards)

    def body(pref, *refs):
        ins, outs = refs[:n + 2], refs[n + 2:]
        for a in range(n):
            outs[a][0] = ins[a][...].astype(MXU_DTYPE)
        outs[n][0] = ins[n][...]
        outs[n + 1][...] = ins[n + 1][...].astype(MXU_DTYPE)

    full = lambda a: pl.BlockSpec(a.shape, lambda i, pref: (0,) * a.ndim)
    slot = lambda a: pl.BlockSpec((1,) + a.shape, lambda i, pref: (pref[1],) + (0,) * a.ndim)
    arrs = list(shards) + [dww, wg]
    return pl.pallas_call(
        body, name="cast_weights",
        grid_spec=pltpu.PrefetchScalarGridSpec(
            num_scalar_prefetch=1, grid=(1,), in_specs=[full(a) for a in arrs],
            out_specs=[slot(a) for a in arrs[:n + 1]] + [full(wg)]),
        out_shape=[jax.ShapeDtypeStruct((N_CHIP,) + a.shape, MXU_DTYPE) for a in shards]
        + [jax.ShapeDtypeStruct((N_CHIP,) + dww.shape, F32), jax.ShapeDtypeStruct(wg.shape, MXU_DTYPE)],
        compiler_params=pltpu.CompilerParams(dimension_semantics=("arbitrary",)),
    )(place, *arrs)


def _mixer_fwd(x, mod, g1, w_in, dww, dwb, lng, lnb, w_pw, wg, pscale, w_out, comm=None):
    s = x.shape[0]
    ts = _token_tile(s)
    nt = s // ts

    def body(x_ref, mod_ref, g1_ref, win_ref, dww_ref, dwb_ref, lng_ref, lnb_ref, wpw_ref, wg_ref, ps_ref,
             wout_ref, x2_ref, y_ref, u_ref, z_ref, rstd_ref, p_ref, ycat_ref, gpad, vpad):
        i = pl.program_id(0)

        @pl.when(i == 0)
        def _():
            gpad[0:CONV_HALO, :] = jnp.zeros((CONV_HALO, CONV_W), F32)
            vpad[0:POOL_HALO, :] = jnp.zeros((POOL_HALO, POOL_W), F32)

        xt = x_ref[...]
        sh1 = mod_ref[0:1, :]
        sc1 = mod_ref[1:2, :]
        gt1 = mod_ref[2:3, :]
        r1 = lax.rsqrt(jnp.mean(xt * xt, axis=-1, keepdims=True) + EPS)
        h1 = (xt * r1 * g1_ref[...]) * (1.0 + sc1) + sh1
        h1b = h1.astype(MXU_DTYPE)
        u = jnp.concatenate([_dot(h1b, win_ref[j]) for j in range(N_CHIP)], axis=1)
        u_ref[...] = u
        a = u[:, :CONV_W]
        g = u[:, CONV_W:2 * CONV_W]
        v = u[:, 2 * CONV_W:]

        gpad[CONV_HALO:CONV_HALO + ts, :] = a * _sigmoid(g)
        cv = jnp.broadcast_to(dwb_ref[...], (ts, CONV_W))
        off = CONV_HALO - (CONV_K - 1)
        for k in range(CONV_K):
            cv = cv + dww_ref[k:k + 1, :] * gpad[off + k:off + k + ts, :]
        gpad[0:CONV_HALO, :] = gpad[ts:ts + CONV_HALO, :]

        mu = jnp.mean(cv, axis=-1, keepdims=True)
        cc = cv - mu
        rstd = lax.rsqrt(jnp.mean(cc * cc, axis=-1, keepdims=True) + EPS)
        z = cc * rstd
        z_ref[...] = z
        rstd_ref[...] = rstd
        ln = z * lng_ref[...] + lnb_ref[...]
        sw = ln * _sigmoid(ln)
        yconv = _dot(sw, wpw_ref[...])

        vpad[POOL_HALO:POOL_HALO + ts, :] = v
        t = i * ts + lax.broadcasted_iota(jnp.int32, (ts, 1), 0)
        ps, ypool = [], []
        for gi, w in enumerate(POOL_WINDOWS):
            cols = slice(gi * POOL_G, (gi + 1) * POOL_G)
            acc = vpad[POOL_HALO:POOL_HALO + ts, cols]
            for d in range(1, w):
                acc = acc + vpad[POOL_HALO - d:POOL_HALO - d + ts, cols]
            cnt = jnp.minimum(t + 1, w).astype(F32)
            pg = (acc / cnt - v[:, cols]).astype(MXU_DTYPE)
            ps.append(pg)
            ypool.append(_dot(pg, wg_ref[gi]))
        vpad[0:POOL_HALO, :] = vpad[ts:ts + POOL_HALO, :]
        p_ref[...] = jnp.concatenate(ps, axis=1)
        ypool = jnp.concatenate(ypool, axis=1) * ps_ref[...]

        ycat = jnp.concatenate([yconv, ypool], axis=1).astype(MXU_DTYPE)
        ycat_ref[...] = ycat
        y = _dot(ycat, wout_ref[...])
        y_ref[...] = y
        x2_ref[...] = xt + gt1 * y

    tile = lambda w: pl.BlockSpec((ts, w), lambda i: (i, 0))
    return _call(
        body, name="mixer_fwd", grid=(nt,),
        in_specs=[tile(D_MODEL), _full(mod.shape), _full(g1.shape), _full(w_in.shape), _full(dww.shape),
                  _full(dwb.shape), _full(lng.shape), _full(lnb.shape), _full(w_pw.shape), _full(wg.shape),
                  _full(pscale.shape), _full(w_out.shape)],
        out_specs=[tile(D_MODEL), tile(D_MODEL), tile(IN_W), tile(CONV_W), tile(1), tile(POOL_W), tile(D_MODEL)],
        out_shape=[jax.ShapeDtypeStruct((s, D_MODEL), F32), jax.ShapeDtypeStruct((s, D_MODEL), F32),
                   jax.ShapeDtypeStruct((s, IN_W), F32), jax.ShapeDtypeStruct((s, CONV_W), F32),
                   jax.ShapeDtypeStruct((s, 1), F32), jax.ShapeDtypeStruct((s, POOL_W), MXU_DTYPE),
                   jax.ShapeDtypeStruct((s, D_MODEL), MXU_DTYPE)],
        scratch_shapes=[pltpu.VMEM((ts + CONV_HALO, CONV_W), F32), pltpu.VMEM((ts + POOL_HALO, POOL_W), F32)],
        args=(x, mod, g1, w_in, dww, dwb, lng, lnb, w_pw, wg, pscale, w_out), comm=comm)


def _ffn(x2, tgt, mod, g2, gf, w_gate, w_up, w_down):
    s = x2.shape[0]
    ts = _token_tile(s)
    nt = s // ts
    fb = w_gate.shape[1]

    def body(x2_ref, tgt_ref, mod_ref, g2_ref, gf_ref, wgt_ref, wup_ref, wdn_ref,
             dx2_ref, h2_ref, df_ref, act_ref, dgg_ref, duu_ref, vec_ref, gg_s, uu_s):
        i = pl.program_id(0)

        @pl.when(i == 0)
        def _():
            vec_ref[...] = jnp.zeros(vec_ref.shape, F32)

        x2t = x2_ref[...]
        sh2 = mod_ref[3:4, :]
        sc2 = mod_ref[4:5, :]
        gt2 = mod_ref[5:6, :]
        g2v = g2_ref[...]
        gfv = gf_ref[...]
        r2 = lax.rsqrt(jnp.mean(x2t * x2t, axis=-1, keepdims=True) + EPS)
        xh2 = x2t * r2
        n2 = xh2 * g2v
        h2b = (n2 * (1.0 + sc2) + sh2).astype(MXU_DTYPE)
        h2_ref[...] = h2b
        f = jnp.zeros((ts, D_MODEL), F32)
        for j in range(N_CHIP):
            gg = _dot_nt(h2b, wgt_ref[j])
            uu = _dot_nt(h2b, wup_ref[j])
            gg_s[j] = gg
            uu_s[j] = uu
            actb = (gg * _sigmoid(gg) * uu).astype(MXU_DTYPE)
            act_ref[j] = actb
            f = f + _dot(actb, wdn_ref[j])
        x3 = x2t + gt2 * f
        r3 = lax.rsqrt(jnp.mean(x3 * x3, axis=-1, keepdims=True) + EPS)
        xh3 = x3 * r3
        diff = xh3 * gfv - tgt_ref[...]
        dout = diff * (1.0 / D_MODEL)
        dn3 = dout * gfv
        dx3 = r3 * (dn3 - xh3 * jnp.mean(dn3 * xh3, axis=-1, keepdims=True))
        dfb = (dx3 * gt2).astype(MXU_DTYPE)
        df_ref[...] = dfb
        dh2 = jnp.zeros((ts, D_MODEL), F32)
        for j in range(N_CHIP):
            dact = _dot_nt(dfb, wdn_ref[j])
            gg = gg_s[j]
            uu = uu_s[j]
            sg = _sigmoid(gg)
            duu = (dact * (gg * sg)).astype(MXU_DTYPE)
            dgg = (dact * uu * (sg * (1.0 + gg * (1.0 - sg)))).astype(MXU_DTYPE)
            duu_ref[j] = duu
            dgg_ref[j] = dgg
            dh2 = dh2 + _dot(dgg, wgt_ref[j]) + _dot(duu, wup_ref[j])
        dn2 = dh2 * (1.0 + sc2)
        dxh2 = dn2 * g2v
        dx2_ref[...] = dx3 + r2 * (dxh2 - xh2 * jnp.mean(dxh2 * xh2, axis=-1, keepdims=True))

        col = lambda a: jnp.sum(a, axis=0, keepdims=True)
        vec_ref[0:1, :] += col(dout * xh3)
        vec_ref[1:2, :] += col(dx3 * f)
        vec_ref[2:3, :] += col(dh2)
        vec_ref[3:4, :] += col(dh2 * n2)
        vec_ref[4:5, :] += col(dn2 * xh2)
        vec_ref[5:6, :] += col(diff * diff)

    tile = lambda w: pl.BlockSpec((ts, w), lambda i: (i, 0))
    tile3 = pl.BlockSpec((N_CHIP, ts, fb), lambda i: (0, i, 0))
    once = lambda a: pl.BlockSpec(a.shape, lambda i: (0,) * a.ndim, pipeline_mode=pl.Buffered(1))
    hid = jax.ShapeDtypeStruct((N_CHIP, s, fb), MXU_DTYPE)
    return pl.pallas_call(
        body, name="ffn", grid=(nt,),
        in_specs=[tile(D_MODEL), tile(D_MODEL), _full(mod.shape), _full(g2.shape), _full(gf.shape),
                  once(w_gate), once(w_up), once(w_down)],
        out_specs=[tile(D_MODEL), tile(D_MODEL), tile(D_MODEL), tile3, tile3, tile3, _full((8, D_MODEL))],
        out_shape=[jax.ShapeDtypeStruct((s, D_MODEL), F32), jax.ShapeDtypeStruct((s, D_MODEL), MXU_DTYPE),
                   jax.ShapeDtypeStruct((s, D_MODEL), MXU_DTYPE), hid, hid, hid,
                   jax.ShapeDtypeStruct((8, D_MODEL), F32)],
        scratch_shapes=[pltpu.VMEM((N_CHIP, ts, fb), F32), pltpu.VMEM((N_CHIP, ts, fb), F32)],
        compiler_params=pltpu.CompilerParams(dimension_semantics=("arbitrary",)),
    )(x2, tgt, mod, g2, gf, w_gate, w_up, w_down)


def _mixer_bwd(dx2, x, y, u, z, rstd, p, mod, g1, w_in, dww, lng, lnb, w_pw, wg, pscale, w_out, comm=None):
    s = x.shape[0]
    ts = _token_tile(s)
    nt = s // ts

    def body(dx2_ref, x_ref, y_ref, u_ref, z_ref, rstd_ref, p_ref, mod_ref, g1_ref, win_ref, dww_ref, lng_ref,
             lnb_ref, wpw_ref, wg_ref, ps_ref, wout_ref,
             gx_ref, h1_ref, du_ref, dy_ref, sw_ref, dyc_ref, dyp_ref, vd_ref, vc_ref, ddw_ref, dcpad, dppad):
        i = pl.program_id(0)
        tix = nt - 1 - i

        @pl.when(i == 0)
        def _():
            vd_ref[...] = jnp.zeros(vd_ref.shape, F32)
            vc_ref[...] = jnp.zeros(vc_ref.shape, F32)
            ddw_ref[...] = jnp.zeros(ddw_ref.shape, F32)
            dcpad[ts:ts + CONV_HALO, :] = jnp.zeros((CONV_HALO, CONV_W), F32)
            dppad[ts:ts + POOL_HALO, :] = jnp.zeros((POOL_HALO, POOL_W), F32)

        col = lambda a: jnp.sum(a, axis=0, keepdims=True)
        sh1 = mod_ref[0:1, :]
        sc1 = mod_ref[1:2, :]
        gt1 = mod_ref[2:3, :]
        dx2t = dx2_ref[...]
        vd_ref[0:1, :] += col(dx2t * y_ref[...])
        dyb = (dx2t * gt1).astype(MXU_DTYPE)
        dy_ref[...] = dyb
        dycat = _dot_nt(dyb, wout_ref[...])
        dyconv = dycat[:, :CONV_W]
        dypool = dycat[:, CONV_W:]

        pt = p_ref[...]
        t = tix * ts + lax.broadcasted_iota(jnp.int32, (ts, 1), 0)
        psc = ps_ref[...]
        dypb = (dypool * psc).astype(MXU_DTYPE)
        dyp_ref[...] = dypb
        dps, ypre = [], []
        for gi, w in enumerate(POOL_WINDOWS):
            cols = slice(gi * POOL_G, (gi + 1) * POOL_G)
            ypre.append(_dot(pt[:, cols], wg_ref[gi]))
            dpg = _dot_nt(dypb[:, cols], wg_ref[gi])
            dps.append(dpg)
            cnt = jnp.minimum(t + 1, w).astype(F32)
            dppad[0:ts, cols] = dpg / cnt
        vc_ref[0:1, :] += col(dypool * jnp.concatenate(ypre, axis=1))
        dvs = []
        for gi, w in enumerate(POOL_WINDOWS):
            cols = slice(gi * POOL_G, (gi + 1) * POOL_G)
            acc = dppad[0:ts, cols]
            for d in range(1, w):
                acc = acc + dppad[d:d + ts, cols]
            dvs.append(acc - dps[gi])
        dv = jnp.concatenate(dvs, axis=1)
        dppad[ts:ts + POOL_HALO, :] = dppad[0:POOL_HALO, :]

        zt = z_ref[...]
        lngv = lng_ref[...]
        ln = zt * lngv + lnb_ref[...]
        sg = _sigmoid(ln)
        swb = (ln * sg).astype(MXU_DTYPE)
        sw_ref[...] = swb
        dycb = dyconv.astype(MXU_DTYPE)
        dyc_ref[...] = dycb
        dln = _dot_nt(dycb, wpw_ref[...]) * (sg * (1.0 + ln * (1.0 - sg)))
        vc_ref[1:2, :] += col(dln * zt)
        vc_ref[2:3, :] += col(dln)
        dz = dln * lngv
        dcv = rstd_ref[...] * (dz - jnp.mean(dz, axis=-1, keepdims=True)
                               - zt * jnp.mean(dz * zt, axis=-1, keepdims=True))
        vc_ref[3:4, :] += col(dcv)
        dcpad[0:ts, :] = dcv
        ut = u_ref[...]
        a = ut[:, :CONV_W]
        g = ut[:, CONV_W:2 * CONV_W]
        sgg = _sigmoid(g)
        glu = a * sgg
        dglu = jnp.zeros((ts, CONV_W), F32)
        for k in range(CONV_K):
            sh = dcpad[CONV_K - 1 - k:CONV_K - 1 - k + ts, :]
            dglu = dglu + dww_ref[k:k + 1, :] * sh
            ddw_ref[k:k + 1, :] += col(glu * sh)
        dcpad[ts:ts + CONV_HALO, :] = dcpad[0:CONV_HALO, :]
        da = dglu * sgg
        dg = dglu * a * sgg * (1.0 - sgg)
        dub = jnp.concatenate([da, dg, dv], axis=1).astype(MXU_DTYPE)
        du_ref[...] = dub
        cw = IN_W // N_CHIP
        dh1 = jnp.zeros((ts, D_MODEL), F32)
        for j in range(N_CHIP):
            dh1 = dh1 + _dot_nt(dub[:, j * cw:(j + 1) * cw], win_ref[j])

        xt = x_ref[...]
        g1v = g1_ref[...]
        r1 = lax.rsqrt(jnp.mean(xt * xt, axis=-1, keepdims=True) + EPS)
        xh1 = xt * r1
        n1 = xh1 * g1v
        h1_ref[...] = (n1 * (1.0 + sc1) + sh1).astype(MXU_DTYPE)
        vd_ref[1:2, :] += col(dh1)
        vd_ref[2:3, :] += col(dh1 * n1)
        dn1 = dh1 * (1.0 + sc1)
        vd_ref[3:4, :] += col(dn1 * xh1)
        dxh = dn1 * g1v
        gx_ref[...] = dx2t + r1 * (dxh - xh1 * jnp.mean(dxh * xh1, axis=-1, keepdims=True))

    tile = lambda w: pl.BlockSpec((ts, w), lambda i: (nt - 1 - i, 0))
    bf = lambda w: jax.ShapeDtypeStruct((s, w), MXU_DTYPE)
    return _call(
        body, name="mixer_bwd", grid=(nt,),
        in_specs=[tile(D_MODEL), tile(D_MODEL), tile(D_MODEL), tile(IN_W), tile(CONV_W), tile(1), tile(POOL_W),
                  _full(mod.shape), _full(g1.shape), _full(w_in.shape), _full(dww.shape), _full(lng.shape),
                  _full(lnb.shape), _full(w_pw.shape), _full(wg.shape), _full(pscale.shape), _full(w_out.shape)],
        out_specs=[tile(D_MODEL), tile(D_MODEL), tile(IN_W), tile(D_MODEL), tile(CONV_W), tile(CONV_W),
                   tile(POOL_W), _full((8, D_MODEL)), _full((8, CONV_W)), _full((32, CONV_W))],
        out_shape=[jax.ShapeDtypeStruct((s, D_MODEL), F32), bf(D_MODEL), bf(IN_W), bf(D_MODEL), bf(CONV_W),
                   bf(CONV_W), bf(POOL_W), jax.ShapeDtypeStruct((8, D_MODEL), F32),
                   jax.ShapeDtypeStruct((8, CONV_W), F32), jax.ShapeDtypeStruct((32, CONV_W), F32)],
        scratch_shapes=[pltpu.VMEM((ts + CONV_HALO, CONV_W), F32), pltpu.VMEM((ts + POOL_HALO, POOL_W), F32)],
        args=(dx2, x, y, u, z, rstd, p, mod, g1, w_in, dww, lng, lnb, w_pw, wg, pscale, w_out), comm=comm)


def _dw(name, a, a_spec, b, b_spec, nb, mb, nbk, comm=None):
    def body(a_ref, b_ref, o_ref):
        av = a_ref[...]
        bv = b_ref[...]
        av = av.reshape(av.shape[-2:])
        bv = bv.reshape(bv.shape[-2:])
        o_ref[0] = _dot_tn(av, bv)

    (out,), rest = _call(
        body, name=name, grid=(nb,), in_specs=[a_spec, b_spec],
        out_specs=[pl.BlockSpec((1, mb, nbk), lambda j: (j, 0, 0))],
        out_shape=[jax.ShapeDtypeStruct((nb, mb, nbk), F32)], args=(a, b), comm=comm)
    return out, rest


def _ada_fwd(c, w_ada, b4, comm=None):
    nc = w_ada.shape[1]

    def body(start_comm, c_ref, w_ref, b4_ref, mod_ref, cact_ref, call, part, parts, send1, recv1, send2, recv2):
        x, y, cc = _place()
        b = 4 * x + 2 * y + cc
        j = 2 * x + y
        call[b] = c_ref[...]
        sends = []
        for r in range(1, N_DEV):
            dev = ((1 - x) if r & 4 else x, (1 - y) if r & 2 else y, (1 - cc) if r & 1 else cc)
            cp = _remote(call.at[b], call.at[b], send1.at[r - 1], recv1.at[r - 1], dev)
            cp.start()
            sends.append(cp)
        for r in range(1, N_DEV):
            src_b = lax.bitwise_xor(b, r)
            _remote(call.at[src_b], call.at[src_b], send1.at[r - 1], recv1.at[r - 1], (x, y, cc)).wait_recv()
        for cp in sends:
            cp.wait_send()
        start_comm()
        for i in range(N_DEV):
            ci = call[i]
            cact_ref[i:i + 1, :] = ci * _sigmoid(ci)
        part[...] = jnp.dot(cact_ref[...], w_ref[...], preferred_element_type=F32, precision=lax.Precision.HIGHEST)
        sends = []
        for r in range(1, N_CHIP):
            kx, ky = _flip(x, y, r)
            cp = _remote(part, parts.at[j], send2.at[r - 1], recv2.at[r - 1], (kx, ky, cc))
            cp.start()
            sends.append(cp)
        parts[j] = part[...]
        for r in range(1, N_CHIP):
            kx, ky = _flip(x, y, r)
            kj = 2 * kx + ky
            _remote(part, parts.at[kj], send2.at[r - 1], recv2.at[r - 1], (x, y, cc)).wait_recv()
        for cp in sends:
            cp.wait_send()
        mine = lax.broadcasted_iota(jnp.int32, (N_DEV, 1), 0) == b
        for k in range(N_CHIP):
            row = jnp.sum(jnp.where(mine, parts[k], 0.0), axis=0, keepdims=True)
            mod_ref[k:k + 1, :] = row + b4_ref[k:k + 1, :]

    return _call(
        body, name="ada_fwd", grid=(1,),
        in_specs=[VMEM, VMEM, VMEM], out_specs=[VMEM, VMEM],
        out_shape=[jax.ShapeDtypeStruct((N_CHIP, nc), F32), jax.ShapeDtypeStruct((N_DEV, D_MODEL), F32)],
        scratch_shapes=[pltpu.VMEM((N_DEV, 1, D_MODEL), F32), pltpu.VMEM((N_DEV, nc), F32),
                        pltpu.VMEM((N_CHIP, N_DEV, nc), F32),
                        pltpu.SemaphoreType.DMA((N_DEV - 1,)), pltpu.SemaphoreType.DMA((N_DEV - 1,)),
                        pltpu.SemaphoreType.DMA((N_CHIP - 1,)), pltpu.SemaphoreType.DMA((N_CHIP - 1,))],
        args=(c, w_ada, b4), comm=comm, body_starts=True)


def _chip_partials(name, place, gs, rs, comm=None):
    n = len(gs)

    def body(pref, *refs):
        g_refs, r_refs = refs[:n], refs[n:2 * n]
        pb_refs, own_refs = refs[2 * n:3 * n], refs[3 * n:]
        jj = pl.program_id(0)
        for a in range(n):
            sm = g_refs[a][0] + r_refs[a][0]
            pb_refs[a][0] = sm.astype(MXU_DTYPE)

            @pl.when(jj == pref[1])
            def _(a=a, sm=sm):
                own_refs[a][...] = sm

    halves = [(g.shape[1] // 2, g.shape[2]) for g in gs]
    in_specs = [pl.BlockSpec((1, h, w), lambda jj, pref: (jj, pref[0], 0)) for h, w in halves]
    in_specs += [pl.BlockSpec((1, h, w), lambda jj, pref: (jj, 0, 0)) for h, w in halves]
    out_specs = [pl.BlockSpec((1, h, w), lambda jj, pref: (jj, 0, 0)) for h, w in halves]
    out_specs += [pl.BlockSpec((h, w), lambda jj, pref: (0, 0)) for h, w in halves]
    out, rest = _call(
        body, name=name, grid=(N_CHIP,), in_specs=in_specs, out_specs=out_specs,
        out_shape=[jax.ShapeDtypeStruct((N_CHIP, h, w), MXU_DTYPE) for h, w in halves]
        + [jax.ShapeDtypeStruct((h, w), F32) for h, w in halves],
        args=(*gs, *rs), prefetch=(place,), comm=comm)
    return (out[:n], out[n:]), rest


def _sum_partials(name, place, owns, recvd, comm=None):
    n = len(owns)

    def body(pref, *refs):
        o_refs, r_refs, out_refs = refs[:n], refs[n:2 * n], refs[2 * n:]
        for a in range(n):
            acc = o_refs[a][...]
            for r in range(N_CHIP - 1):
                acc = acc + r_refs[a][r].astype(F32)
            out_refs[a][...] = acc

    full = lambda a: pl.BlockSpec(a.shape, lambda i, pref: (0,) * a.ndim)
    return _call(
        body, name=name, grid=(1,), in_specs=[full(a) for a in list(owns) + list(recvd)],
        out_specs=[pl.BlockSpec(o.shape, lambda i, pref: (pref[0], 0)) for o in owns],
        out_shape=[jax.ShapeDtypeStruct((2 * o.shape[0], o.shape[1]), F32) for o in owns],
        args=(*owns, *recvd), prefetch=(place,), comm=comm)


def _adamw_math(w, g, m, v):
    m = ADAM_B1 * m + (1.0 - ADAM_B1) * g
    v = ADAM_B2 * v + (1.0 - ADAM_B2) * (g * g)
    m_hat = m / (1.0 - ADAM_B1 ** ADAM_STEP)
    v_hat = v / (1.0 - ADAM_B2 ** ADAM_STEP)
    delta = -ADAM_LR * (m_hat / (jnp.sqrt(v_hat) + ADAM_EPS) + ADAM_WD * w)
    return delta, m, v


def _row_tile(rows):
    for t in (512, 352, 256, 128):
        if rows % t == 0:
            return t
    return rows


def _adamw(name, wgmv, steps, after=()):
    n = len(wgmv)

    def body(*refs):
        ins, outs = refs[:4 * n], refs[4 * n:]
        for i in range(n):
            w_ref, g_ref, m_ref, v_ref = ins[4 * i:4 * i + 4]
            d_ref, nm_ref, nv_ref = outs[3 * i:3 * i + 3]
            d_ref[...], nm_ref[...], nv_ref[...] = _adamw_math(w_ref[...], g_ref[...], m_ref[...], v_ref[...])

    in_specs, out_specs, out_shape, args = [], [], [], []
    for w, g, m, v in wgmv:
        rows, cols = w.shape
        spec = pl.BlockSpec((rows // steps, cols), lambda i: (i, 0))
        in_specs += [spec] * 4
        out_specs += [spec] * 3
        out_shape += [jax.ShapeDtypeStruct(w.shape, F32)] * 3
        args += [w, g, m, v]
    res, _ = _call(body, name=name, grid=(steps,), in_specs=in_specs, out_specs=out_specs, out_shape=out_shape,
                   args=args, after=after)
    return [res[3 * i:3 * i + 3] for i in range(n)]


def _adamw_ada(place, cact, dmod, w, m, v, after=()):
    rows, cols = w.shape
    tr = _row_tile(rows)

    def body(pref, ca_ref, dm_ref, w_ref, m_ref, v_ref, g_ref, d_ref, nm_ref, nv_ref):
        g = lax.dot_general(ca_ref[...], dm_ref[...], (((0,), (0,)), ((), ())), preferred_element_type=F32,
                            precision=lax.Precision.HIGHEST)
        g_ref[...] = g
        d_ref[...], nm_ref[...], nv_ref[...] = _adamw_math(w_ref[...], g, m_ref[...], v_ref[...])

    spec = pl.BlockSpec((tr, cols), lambda i, pref: (i, 0))
    return _call(
        body, name="adamw_ada", grid=(rows // tr,),
        in_specs=[pl.BlockSpec((N_DEV, tr), lambda i, pref: (0, i)),
                  pl.BlockSpec((N_DEV, cols), lambda i, pref: (0, pref[1])), spec, spec, spec],
        out_specs=[spec] * 4, out_shape=[jax.ShapeDtypeStruct(w.shape, F32)] * 4,
        args=(cact, dmod, w, m, v), prefetch=(place,), after=after)[0]


def _adamw_small(place, vf_all, vd_all, vc_all, ddw_all, gwg_all, wmv):
    nw = len(wmv)
    flat = [a for t in wmv for a in t]

    def body(pref, vf_ref, vd_ref, vc_ref, ddw_ref, gwg_ref, *refs):
        w_refs = refs[:3 * nw]
        loss_ref, dmod_ref = refs[3 * nw], refs[3 * nw + 1]
        o_refs = refs[3 * nw + 2:]
        j = pref[1]

        def total(ref):
            acc = ref[0]
            for b in range(1, N_DEV):
                acc = acc + ref[b]
            return acc

        vf, vd, vc, ddw, gwg = total(vf_ref), total(vd_ref), total(vc_ref), total(ddw_ref), total(gwg_ref)
        loss_ref[...] = (0.5 / D_MODEL) * jnp.sum(vf[5:6, :], axis=1, keepdims=True)
        order = ((vd_ref, 1), (vd_ref, 2), (vd_ref, 0), (vf_ref, 2), (vf_ref, 3), (vf_ref, 1))
        for b in range(N_DEV):
            for q, (ref, row) in enumerate(order):
                dmod_ref[b:b + 1, q * D_MODEL:(q + 1) * D_MODEL] = ref[b, row:row + 1, :]
        dm = dmod_ref[...]
        g_bada = dm[0:1, :]
        for b in range(1, N_DEV):
            g_bada = g_bada + dm[b:b + 1, :]
        g_dww = jnp.zeros((32, POOL_G), F32)
        for k in range(N_CHIP):
            g_dww = g_dww + jnp.where(j == k, ddw[:, k * POOL_G:(k + 1) * POOL_G], 0.0)
        grads = [g_bada, vd[3:4, :], g_dww, vc[3:4, :], vc[1:2, :], vc[2:3, :], gwg, vc[0:1, :], vf[4:5, :],
                 vf[0:1, :]]
        for i, g in enumerate(grads):
            w_ref, m_ref, v_ref = w_refs[3 * i:3 * i + 3]
            d, nm, nv = _adamw_math(w_ref[...], g, m_ref[...], v_ref[...])
            o_refs[4 * i][...] = g
            o_refs[4 * i + 1][...] = d
            o_refs[4 * i + 2][...] = nm
            o_refs[4 * i + 3][...] = nv

    gathered = [vf_all, vd_all, vc_all, ddw_all, gwg_all]
    outs = [jax.ShapeDtypeStruct((1, 1), F32), jax.ShapeDtypeStruct((N_DEV, 6 * D_MODEL), F32)]
    for w, _, _ in wmv:
        outs += [jax.ShapeDtypeStruct(w.shape, F32)] * 4
    full = lambda a: pl.BlockSpec(a.shape, lambda i, pref: (0,) * a.ndim)
    res = pl.pallas_call(
        body, name="adamw_small",
        grid_spec=pltpu.PrefetchScalarGridSpec(
            num_scalar_prefetch=1, grid=(1,),
            in_specs=[full(a) for a in gathered + flat], out_specs=[full(o) for o in outs]),
        out_shape=outs,
        compiler_params=pltpu.CompilerParams(dimension_semantics=("arbitrary",)),
    )(place, *gathered, *flat)
    return res[0], res[1], [res[2 + 4 * i:6 + 4 * i] for i in range(nw)]


def kernel(x, c, w_ada, b_ada, g_norm1, w_in, dw_w, dw_b, conv_ln_g, conv_ln_b, w_conv_pw, w_pool_group, pool_scale, w_out, g_norm2, w_ffn_gate, w_ffn_up, w_ffn_down, g_final, loss_target, m_w_ada, m_b_ada, m_g_norm1, m_w_in, m_dw_w, m_dw_b, m_conv_ln_g, m_conv_ln_b, m_w_conv_pw, m_w_pool_group, m_pool_scale, m_w_out, m_g_norm2, m_w_ffn_gate, m_w_ffn_up, m_w_ffn_down, m_g_final, v_w_ada, v_b_ada, v_g_norm1, v_w_in, v_dw_w, v_dw_b, v_conv_ln_g, v_conv_ln_b, v_w_conv_pw, v_w_pool_group, v_pool_scale, v_w_out, v_g_norm2, v_w_ffn_gate, v_w_ffn_up, v_w_ffn_down, v_g_final):
    xi, yi, ci = _place()
    place = jnp.stack([ci, 2 * xi + yi]).astype(jnp.int32)
    n_ada = w_ada.shape[2]

    tr = lambda a: jnp.transpose(a[0])
    big = [w_in[0], w_conv_pw[0], w_out[0], tr(w_ffn_gate), tr(w_ffn_up), w_ffn_down[0]]
    b_in, b_pw, b_out, b_gate, b_up, b_down, b_dww, wg_b = _cast_weights(place, big, dw_w[0], w_pool_group[0])

    (mod4, cact), (win_g, wpw_g, wout_g, dww_g) = _ada_fwd(
        c, w_ada[0], b_ada.reshape(N_CHIP, n_ada),
        comm=_weights_gather([b_in, b_pw, b_out, b_dww], [True, True, True, False]))
    mod = mod4.reshape(6, D_MODEL)
    dww_full = jnp.pad(jnp.concatenate([dww_g[k] for k in range(N_CHIP)], axis=1), ((0, 1), (0, 0)))
    w_pw = wpw_g.reshape(CONV_W, CONV_W)
    w_o = wout_g.reshape(D_MODEL, D_MODEL)
    xs, tgt, gf = x[0], loss_target[0], g_final.reshape(1, D_MODEL)
    s = xs.shape[0]
    fb = b_gate.shape[1]

    (x2, y, u, z, rstd, p, ycat), (wgate_g, wup_g, wdown_g) = _mixer_fwd(
        xs, mod, g_norm1, win_g, dww_full, dw_b, conv_ln_g, conv_ln_b, w_pw, wg_b, pool_scale, w_o,
        comm=_weights_gather([b_gate, b_up, b_down], [True, True, True]))
    dx2, h2, df, act, dgg, duu, vec_f = _ffn(x2, tgt, mod, g_norm2, gf, wgate_g, wup_g, wdown_g)

    whole = lambda w: pl.BlockSpec((s, w), lambda j: (0, 0))
    cols = lambda w: pl.BlockSpec((s, w), lambda j: (0, j))
    hid = pl.BlockSpec((1, s, fb), lambda j: (j, 0, 0))
    c_gate, _ = _dw("dw_gate", dgg, hid, h2, whole(D_MODEL), N_CHIP, fb, D_MODEL)
    c_up, (r_gate,) = _dw("dw_up", duu, hid, h2, whole(D_MODEL), N_CHIP, fb, D_MODEL, comm=_sibling_halves([c_gate]))
    ((pb_gate,), (own_gate,)), _ = _chip_partials("partials_gate", place, [c_gate], [r_gate])
    c_down, (r_up, rc_gate) = _dw("dw_down", act, hid, df, whole(D_MODEL), N_CHIP, fb, D_MODEL,
                                  comm=_both(_sibling_halves([c_up]), _exchange_partials([pb_gate])))
    ((pb_up,), (own_up,)), (r_down,) = _chip_partials("partials_up", place, [c_up], [r_up],
                                                      comm=_sibling_halves([c_down]))
    ((pb_down,), (own_down,)), _ = _chip_partials("partials_down", place, [c_down], [r_down])
    (gx, h1, du, dy, sw, dyc, dyp, vec_d, vec_c, ddw), (rc_up, rc_down) = _mixer_bwd(
        dx2, xs, y, u, z, rstd, p, mod, g_norm1, win_g, dww_full, conv_ln_g, conv_ln_b, w_pw, wg_b, pool_scale, w_o,
        comm=_exchange_partials([pb_up, pb_down]))

    g_wg, _ = _dw("dw_wg", p, cols(POOL_G), dyp, cols(POOL_G), len(POOL_WINDOWS), POOL_G, POOL_G)
    c_in, (vf_all, vd_all, vc_all, ddw_all, gwg_all) = _dw(
        "dw_in", h1, whole(D_MODEL), du, cols(IN_W // N_CHIP), N_CHIP, D_MODEL, IN_W // N_CHIP,
        comm=_small_gather([vec_f, vec_d, vec_c, ddw, g_wg]))
    c_out, (r_in,) = _dw("dw_out", ycat, cols(D_MODEL // N_CHIP), dy, whole(D_MODEL), N_CHIP, D_MODEL // N_CHIP,
                         D_MODEL, comm=_sibling_halves([c_in]))
    c_pw, (r_out,) = _dw("dw_pw", sw, cols(CONV_W // N_CHIP), dyc, whole(CONV_W), N_CHIP, CONV_W // N_CHIP, CONV_W,
                         comm=_sibling_halves([c_out]))

    ffn_fulls, (r_pw,) = _sum_partials("sum_ffn", place, [own_gate, own_up, own_down], [rc_gate, rc_up, rc_down],
                                       comm=_sibling_halves([c_pw]))
    (pbs_mix, owns_mix), (g_gate, g_up, g_down) = _chip_partials(
        "partials_mix", place, [c_in, c_pw, c_out], [r_in, r_pw, r_out], comm=_join_halves(ffn_fulls))

    pad_rows = lambda a: jnp.pad(a[0], ((0, 1), (0, 0)))
    row = lambda a: a.reshape(1, -1)
    small = [(b_ada, m_b_ada, v_b_ada), (g_norm1, m_g_norm1, v_g_norm1),
             (pad_rows(dw_w), pad_rows(m_dw_w), pad_rows(v_dw_w)), (dw_b, m_dw_b, v_dw_b),
             (conv_ln_g, m_conv_ln_g, v_conv_ln_g), (conv_ln_b, m_conv_ln_b, v_conv_ln_b),
             (w_pool_group[0], m_w_pool_group[0], v_w_pool_group[0]), (pool_scale, m_pool_scale, v_pool_scale),
             (g_norm2, m_g_norm2, v_g_norm2), (row(g_final), row(m_g_final), row(v_g_final))]
    loss, dmod, small_out = _adamw_small(place, vf_all, vd_all, vc_all, ddw_all, gwg_all, small)
    (o_bada, o_g1, o_dww, o_dwb, o_lng, o_lnb, o_wg, o_ps, o_g2, o_gf) = small_out
    o_dww = [a[:CONV_K] for a in o_dww]
    o_gf = [a.reshape(D_MODEL) for a in o_gf]
    lead = lambda outs: [a[None] for a in outs]

    lands, sem_shape, copies = _exchange_parts(pbs_mix)
    state, token = _split_start("exchange_mix_start", pbs_mix, lands, sem_shape, copies)
    u_gate, u_up, u_down = _adamw(
        "adamw_ffn", [(tr(w_ffn_gate), g_gate, tr(m_w_ffn_gate), tr(v_w_ffn_gate)),
                      (tr(w_ffn_up), g_up, tr(m_w_ffn_up), tr(v_w_ffn_up)),
                      (w_ffn_down[0], g_down, m_w_ffn_down[0], v_w_ffn_down[0])],
        steps=4, after=(token,))
    o_gate = [jnp.transpose(o) for o in [g_gate] + list(u_gate)]
    o_up = [jnp.transpose(o) for o in [g_up] + list(u_up)]
    o_down = [g_down] + list(u_down)
    o_ada = _adamw_ada(place, cact, dmod, w_ada[0], m_w_ada[0], v_w_ada[0], after=(token,))
    rc_mix = _split_wait("exchange_mix_wait", state, len(pbs_mix), sem_shape, copies, after=(o_ada[1], u_down[0]))
    mix_fulls, _ = _sum_partials("sum_mix", place, owns_mix, rc_mix)
    g_in, g_pw, g_out = _comm_only("join_mix", _join_halves(mix_fulls))
    u_in, u_pw, u_out = _adamw(
        "adamw_mix", [(w_in[0], g_in, m_w_in[0], v_w_in[0]), (w_conv_pw[0], g_pw, m_w_conv_pw[0], v_w_conv_pw[0]),
                      (w_out[0], g_out, m_w_out[0], v_w_out[0])], steps=4)
    o_in, o_pw, o_out = [g_in] + list(u_in), [g_pw] + list(u_pw), [g_out] + list(u_out)

    per_weight = [lead(o_ada), o_bada, o_g1, lead(o_in), lead(o_dww), o_dwb, o_lng, o_lnb, lead(o_pw), lead(o_wg),
                  o_ps, lead(o_out), o_g2, lead(o_gate), lead(o_up), lead(o_down), o_gf]
    result = [loss.reshape(()), gx[None]]
    for kind in range(4):
        result += [o[kind] for o in per_weight]
    return tuple(result)
```

```python
import functools

import jax
import jax.numpy as jnp
from jax import lax
from jax.experimental import pallas as pl
from jax.experimental.pallas import tpu as pltpu

F32 = jnp.float32
MXU_DTYPE = jnp.bfloat16
EPS = 1e-6

D_MODEL = 1024
CONV_W = 512
POOL_W = 512
CONV_K = 31
POOL_WINDOWS = (2, 4, 8, 16)
POOL_G = 128
IN_W = 2 * CONV_W + POOL_W
N_CHIP = 4
N_DEV = 8
CONV_HALO = 32
POOL_HALO = 16

ADAM_LR = 0.001
ADAM_B1 = 0.9
ADAM_B2 = 0.999
ADAM_EPS = 1e-08
ADAM_WD = 0.01
ADAM_STEP = 10

MESH = pl.DeviceIdType.MESH
ANY = pl.BlockSpec(memory_space=pl.ANY)
VMEM = pl.BlockSpec(memory_space=pltpu.VMEM)


def _dot(a, b):
    return jnp.dot(a.astype(MXU_DTYPE), b.astype(MXU_DTYPE), preferred_element_type=F32)


def _dot_nt(a, b):
    return lax.dot_general(a.astype(MXU_DTYPE), b.astype(MXU_DTYPE), (((1,), (1,)), ((), ())),
                           preferred_element_type=F32)


def _dot_tn(a, b):
    return lax.dot_general(a.astype(MXU_DTYPE), b.astype(MXU_DTYPE), (((0,), (0,)), ((), ())),
                           preferred_element_type=F32)


def _sigmoid(v):
    return 1.0 / (1.0 + jnp.exp(-v))


def _full(shape):
    n = len(shape)
    return pl.BlockSpec(shape, lambda *_: (0,) * n)


def _token_tile(s):
    return 256 if s % 256 == 0 else s


def _place():
    return lax.axis_index("x"), lax.axis_index("y"), lax.axis_index("c")


def _flip(x, y, r):
    return ((1 - x) if r & 2 else x, (1 - y) if r & 1 else y)


def _remote(src, dst, send_sem, recv_sem, dev):
    return pltpu.make_async_remote_copy(src_ref=src, dst_ref=dst, send_sem=send_sem, recv_sem=recv_sem,
                                        device_id=dev, device_id_type=MESH)


class _Comm:
    def __init__(self, ins, outs, aliases, scratch, start, finish, mid=None):
        self.ins, self.outs, self.aliases, self.scratch = list(ins), list(outs), dict(aliases), list(scratch)
        self.start, self.finish = start, finish
        self.mid = mid


def _both(a, b):
    na, nao, nas = len(a.ins), len(a.outs), len(a.scratch)
    aliases = dict(a.aliases)
    aliases.update({na + i: nao + o for i, o in b.aliases.items()})

    def start(ins, outs, scr):
        a.start(ins[:na], outs[:nao], scr[:nas])
        b.start(ins[na:], outs[nao:], scr[nas:])

    def finish(ins, outs, scr):
        a.finish(ins[:na], outs[:nao], scr[:nas])
        b.finish(ins[na:], outs[nao:], scr[nas:])

    def mid(ins, outs, scr):
        if a.mid:
            a.mid(ins[:na], outs[:nao], scr[:nas])
        if b.mid:
            b.mid(ins[na:], outs[nao:], scr[nas:])

    return _Comm(a.ins + b.ins, a.outs + b.outs, aliases, a.scratch + b.scratch, start, finish,
                 mid if (a.mid or b.mid) else None)


def _call(body, *, name, grid, in_specs, out_specs, out_shape, args, scratch_shapes=(), prefetch=(), comm=None,
          body_starts=False, after=()):
    in_specs = list(in_specs) + [ANY] * len(after)
    args = list(args) + list(after)
    n_pre, n_in, n_out, n_scr = len(prefetch), len(in_specs), len(out_specs), len(scratch_shapes)
    n_body_in = n_in - len(after)
    c_ins = comm.ins if comm else []
    c_outs = comm.outs if comm else []
    c_scr = comm.scratch if comm else []
    last = grid[0] - 1

    def wrapped(*refs):
        pre, refs = refs[:n_pre], refs[n_pre:]
        ins, cin = refs[:n_body_in], refs[n_in:n_in + len(c_ins)]
        refs = refs[n_in + len(c_ins):]
        outs, cout = refs[:n_out], refs[n_out:n_out + len(c_outs)]
        refs = refs[n_out + len(c_outs):]
        scr, cscr = refs[:n_scr], refs[n_scr:]
        step = pl.program_id(0)
        if comm and not body_starts:
            @pl.when(step == 0)
            def _():
                comm.start(cin, cout, cscr)

        has_mid = comm is not None and comm.mid is not None
        mid_step = grid[0] // 2 if grid[0] >= 4 else None
        if has_mid and mid_step is not None:
            @pl.when(step == mid_step)
            def _():
                comm.mid(cin, cout, cscr)

        if body_starts:
            body(lambda: comm.start(cin, cout, cscr) if comm else None, *pre, *ins, *outs, *scr)
        else:
            body(*pre, *ins, *outs, *scr)
        if comm:
            @pl.when(step == last)
            def _():
                if has_mid and mid_step is None:
                    comm.mid(cin, cout, cscr)
                comm.finish(cin, cout, cscr)

    aliases = {n_pre + n_in + a: n_out + b for a, b in (comm.aliases if comm else {}).items()}
    res = pl.pallas_call(
        wrapped, name=name,
        grid_spec=pltpu.PrefetchScalarGridSpec(
            num_scalar_prefetch=n_pre, grid=grid, in_specs=list(in_specs) + [ANY] * len(c_ins),
            out_specs=list(out_specs) + [ANY] * len(c_outs), scratch_shapes=list(scratch_shapes) + list(c_scr)),
        out_shape=list(out_shape) + list(c_outs),
        input_output_aliases=aliases,
        compiler_params=pltpu.CompilerParams(dimension_semantics=("arbitrary",)),
    )(*prefetch, *args, *c_ins)
    return res[:n_out], res[n_out:]


def _comm_only(name, comm):
    return _call(lambda: None, name=name, grid=(1,), in_specs=[], out_specs=[], out_shape=[], args=[], comm=comm)[1]


def _gather_phases(make_items):
    def own_sends(items, send, recv):
        x, y, cc = _place()
        j = 2 * x + y
        cps = []
        for a, it in enumerate(items):
            if it["sibling"]:
                cps.append(_remote(it["src"], it["dst"](j, cc), send.at[a, 0], recv.at[a, 0], (x, y, 1 - cc)))
            for r in range(1, N_CHIP):
                kx, ky = _flip(x, y, r)
                cps.append(_remote(it["src"], it["dst"](j, cc), send.at[a, r], recv.at[a, r], (kx, ky, cc)))
        return cps

    def start(ins, outs, scr):
        items = make_items(ins, outs, scr)
        send, recv, lsem = scr[-3:]
        for a, it in enumerate(items):
            if it["local"] is not None:
                src, stage, dst = it["local"]
                lc = pltpu.make_async_copy(src, stage, lsem.at[a])
                lc.start()
                lc.wait()
                pltpu.make_async_copy(stage, dst, lsem.at[a]).start()
        for cp in own_sends(items, send, recv):
            cp.start()

    def finish(ins, outs, scr):
        items = make_items(ins, outs, scr)
        send, recv, lsem = scr[-3:]
        x, y, cc = _place()
        j = 2 * x + y
        sib = (x, y, 1 - cc)
        forwards = []
        for a, it in enumerate(items):
            for r in range(1, N_CHIP):
                kx, ky = _flip(x, y, r)
                got = it["dst"](2 * kx + ky, cc)
                _remote(got, got, send.at[a, r], recv.at[a, r], sib).wait_recv()
                if it["forward"]:
                    cp = _remote(got, got, send.at[a, 3 + r], recv.at[a, 3 + r], sib)
                    cp.start()
                    forwards.append(cp)
        for a, it in enumerate(items):
            if it["sibling"]:
                got = it["dst"](j, 1 - cc)
                _remote(got, got, send.at[a, 0], recv.at[a, 0], sib).wait_recv()
            if it["forward"]:
                for r in range(1, N_CHIP):
                    kx, ky = _flip(x, y, r)
                    got = it["dst"](2 * kx + ky, 1 - cc)
                    _remote(got, got, send.at[a, 3 + r], recv.at[a, 3 + r], sib).wait_recv()
        for cp in own_sends(items, send, recv) + forwards:
            cp.wait_send()
        for a, it in enumerate(items):
            if it["local"] is not None:
                src, stage, dst = it["local"]
                pltpu.make_async_copy(stage, dst, lsem.at[a]).wait()

    return start, finish


def _gather_sems(n):
    return [pltpu.SemaphoreType.DMA((n, 7)), pltpu.SemaphoreType.DMA((n, 7)), pltpu.SemaphoreType.DMA((n,))]


def _weights_gather(bufs, split):
    n = len(bufs)

    def ctx(outs):
        x, y, cc = _place()
        chips = dict(me=2 * x + y, y=2 * x + (1 - y), x=2 * (1 - x) + y, d=2 * (1 - x) + (1 - y))
        devs = dict(y=(x, 1 - y, cc), x=(1 - x, y, cc), d=(1 - x, 1 - y, cc), s=(x, y, 1 - cc))

        def piece(a, kj, pc, q=None):
            if not split[a]:
                return outs[a].at[kj]
            h = bufs[a].shape[1] // 2
            if q is None:
                return outs[a].at[kj, pl.ds(pc * h, h), :]
            return outs[a].at[kj, pl.ds(pc * h + q * (h // 2), h // 2), :]

        return cc, chips, devs, piece

    def directs(a, outs, send, recv):
        cc, chips, devs, piece = ctx(outs)
        if not split[a]:
            whole = piece(a, chips["me"], cc)
            return [_remote(whole, whole, send.at[a, k], recv.at[a, k], devs[t]) for k, t in ((0, "y"), (2, "x"), (4, "d"))]
        q = lambda i: piece(a, chips["me"], cc, i)
        return [_remote(q(0), q(0), send.at[a, 0], recv.at[a, 0], devs["y"]),
                _remote(q(1), q(1), send.at[a, 3], recv.at[a, 3], devs["x"]),
                _remote(q(1), q(1), send.at[a, 1], recv.at[a, 1], devs["y"]),
                _remote(q(0), q(0), send.at[a, 2], recv.at[a, 2], devs["x"])]

    def landed(a, k, outs, send, recv):
        cc, chips, devs, piece = ctx(outs)
        if not split[a]:
            got = piece(a, chips[{0: "y", 2: "x", 4: "d"}[k]], cc)
        elif k < 6:
            got = piece(a, chips[("y", "y", "x", "x", "d", "d")[k]], cc, (0, 1, 0, 1, 0, 1)[k])
        else:
            got = piece(a, chips[("y", "x", "d")[k - 6]], 1 - cc)
        return _remote(got, got, send.at[a, k], recv.at[a, k], devs["s"])

    def passed_on(a, outs, send, recv):
        cc, chips, devs, piece = ctx(outs)
        from_y, from_x = piece(a, chips["y"], cc, 0), piece(a, chips["x"], cc, 1)
        return [_remote(from_y, from_y, send.at[a, 4], recv.at[a, 4], devs["x"]),
                _remote(from_x, from_x, send.at[a, 5], recv.at[a, 5], devs["y"])]

    def to_sibling(a, outs, send, recv):
        cc, chips, devs, piece = ctx(outs)
        return [_remote(piece(a, chips[t], cc), piece(a, chips[t], cc), send.at[a, 6 + i], recv.at[a, 6 + i], devs["s"])
                for i, t in enumerate(("y", "x", "d"))]

    def start(ins, outs, scr):
        send, recv = scr
        per_item = [directs(a, outs, send, recv) for a in range(n)]
        for rank in range(4):
            for cps in per_item:
                if rank < len(cps):
                    cps[rank].start()

    def mid(ins, outs, scr):
        send, recv = scr
        for a in range(n):
            if split[a]:
                fy, fx = passed_on(a, outs, send, recv)
                landed(a, 0, outs, send, recv).wait_recv()
                fy.start()
                landed(a, 3, outs, send, recv).wait_recv()
                fx.start()

    def finish(ins, outs, scr):
        send, recv = scr
        for a in range(n):
            if split[a]:
                for k in (1, 2, 4, 5):
                    landed(a, k, outs, send, recv).wait_recv()
                for cp in to_sibling(a, outs, send, recv):
                    cp.start()
            else:
                for k in (0, 2, 4):
                    landed(a, k, outs, send, recv).wait_recv()
        for a in range(n):
            if split[a]:
                for k in (6, 7, 8):
                    landed(a, k, outs, send, recv).wait_recv()
            cps = directs(a, outs, send, recv)
            if split[a]:
                cps += passed_on(a, outs, send, recv) + to_sibling(a, outs, send, recv)
            for cp in cps:
                cp.wait_send()

    return _Comm(bufs, [jax.ShapeDtypeStruct(b.shape, b.dtype) for b in bufs], {i: i for i in range(n)},
                 [pltpu.SemaphoreType.DMA((n, 9)), pltpu.SemaphoreType.DMA((n, 9))], start, finish, mid)


def _small_gather(arrs):
    n = len(arrs)

    def make_items(ins, outs, scr):
        x, y, cc = _place()
        items = []
        for a in range(n):
            dst = functools.partial(lambda o, kj, pc: o.at[2 * kj + pc], outs[a])
            items.append(dict(src=ins[a], dst=dst, local=(ins[a], scr[a], outs[a].at[4 * x + 2 * y + cc]),
                              sibling=True, forward=True))
        return items

    start, finish = _gather_phases(make_items)
    return _Comm(arrs, [jax.ShapeDtypeStruct((N_DEV,) + a.shape, a.dtype) for a in arrs], {},
                 [pltpu.VMEM(a.shape, a.dtype) for a in arrs] + _gather_sems(n), start, finish)


HBM = pl.BlockSpec(memory_space=pltpu.HBM)
SEM = pl.BlockSpec(memory_space=pltpu.SEMAPHORE)
DATAFLOW = pltpu.SideEffectType.DATAFLOW_SIDE_EFFECTING


class _SemGrid:
    def __init__(self, refs, cols):
        self.refs, self.cols = refs, cols

    @property
    def at(self):
        return self

    def __getitem__(self, idx):
        return self.refs[idx[0] * self.cols + idx[1]]


def _split_start(name, srcs, lands, sem_shape, copies):
    n, k = len(srcs), len(lands)
    ns = sem_shape[0] * sem_shape[1]

    def body(*refs):
        src_refs, land_refs = refs[:n], refs[n:n + k]
        send = _SemGrid(refs[n + k:n + k + ns], sem_shape[1])
        recv = _SemGrid(refs[n + k + ns:n + k + 2 * ns], sem_shape[1])
        token = refs[-1]
        for cp in copies(src_refs, land_refs, send, recv):
            cp.start()
        token[...] = jnp.zeros(token.shape, F32)

    hbm = lambda a: pltpu.with_memory_space_constraint(a, pltpu.HBM)
    zones = [lax.empty(l.shape, l.dtype) for l in lands]
    out = pl.pallas_call(
        body, name=name,
        out_shape=[pltpu.SemaphoreType.DMA(())] * (2 * ns)
        + [pltpu.HBM(a.shape, a.dtype) for a in list(srcs) + list(lands)] + [jax.ShapeDtypeStruct((8, 128), F32)],
        in_specs=[HBM] * (n + k), out_specs=[SEM] * (2 * ns) + [HBM] * (n + k) + [VMEM],
        input_output_aliases={i: 2 * ns + i for i in range(n + k)},
        compiler_params=pltpu.CompilerParams(has_side_effects=DATAFLOW),
    )(*[hbm(a) for a in srcs], *[hbm(z) for z in zones])
    return out[:-1], out[-1]


def _split_wait(name, state, n, sem_shape, copies, after):
    ns = sem_shape[0] * sem_shape[1]
    sems, bufs = state[:2 * ns], state[2 * ns:]
    k = len(bufs) - n

    def body(*refs):
        src_refs, land_refs = refs[:n], refs[n:n + k]
        send = _SemGrid(refs[n + k:n + k + ns], sem_shape[1])
        recv = _SemGrid(refs[n + k + ns:n + k + 2 * ns], sem_shape[1])
        cps = copies(src_refs, land_refs, send, recv)
        for cp in cps:
            cp.wait_send()
        for cp in cps:
            cp.wait_recv()

    out = pl.pallas_call(
        body, name=name,
        out_shape=[pltpu.HBM(a.shape, a.dtype) for a in bufs],
        in_specs=[HBM] * (n + k) + [SEM] * (2 * ns) + [ANY] * len(after), out_specs=[HBM] * (n + k),
        input_output_aliases={i: i for i in range(n + k)},
        compiler_params=pltpu.CompilerParams(has_side_effects=DATAFLOW),
    )(*bufs, *sems, *after)
    return out[n:]


def _direct_phases(copies):
    def start(ins, outs, scr):
        for cp in copies(ins, outs, *scr):
            cp.start()

    def finish(ins, outs, scr):
        cps = copies(ins, outs, *scr)
        for cp in cps:
            cp.wait_recv()
        for cp in cps:
            cp.wait_send()

    return start, finish


def _sibling_halves(gs):
    n = len(gs)

    def copies(ins, outs, send, recv):
        x, y, cc = _place()
        cps = []
        for a in range(n):
            h = gs[a].shape[1] // 2
            cps.append(_remote(ins[a].at[:, pl.ds((1 - cc) * h, h), :], outs[a], send.at[a], recv.at[a],
                               (x, y, 1 - cc)))
        return cps

    start, finish = _direct_phases(copies)
    return _Comm(gs, [jax.ShapeDtypeStruct((N_CHIP, g.shape[1] // 2, g.shape[2]), F32) for g in gs], {},
                 [pltpu.SemaphoreType.DMA((n,)), pltpu.SemaphoreType.DMA((n,))], start, finish)


def _exchange_parts(pbs):
    n = len(pbs)

    def copies(ins, outs, send, recv):
        x, y, cc = _place()
        cps = []
        for a in range(n):
            for r in range(1, N_CHIP):
                kx, ky = _flip(x, y, r)
                cps.append(_remote(ins[a].at[2 * kx + ky], outs[a].at[r - 1], send.at[a, r - 1], recv.at[a, r - 1],
                                   (kx, ky, cc)))
        return cps

    lands = [jax.ShapeDtypeStruct((N_CHIP - 1,) + p.shape[1:], p.dtype) for p in pbs]
    return lands, (n, N_CHIP - 1), copies


def _exchange_partials(pbs):
    lands, sem_shape, copies = _exchange_parts(pbs)
    start, finish = _direct_phases(copies)
    return _Comm(pbs, lands, {}, [pltpu.SemaphoreType.DMA(sem_shape), pltpu.SemaphoreType.DMA(sem_shape)],
                 start, finish)


def _join_halves(fulls):
    n = len(fulls)

    def copies(ins, outs, send, recv):
        x, y, cc = _place()
        cps = []
        for a in range(n):
            h = fulls[a].shape[0] // 2
            mine = outs[a].at[pl.ds(cc * h, h), :]
            cps.append(_remote(mine, mine, send.at[a], recv.at[a], (x, y, 1 - cc)))
        return cps

    start, finish = _direct_phases(copies)
    return _Comm(fulls, [jax.ShapeDtypeStruct(f.shape, F32) for f in fulls], {i: i for i in range(n)},
                 [pltpu.SemaphoreType.DMA((n,)), pltpu.SemaphoreType.DMA((n,))], start, finish)


def _cast_weights(place, shards, dww, wg):
    n = len(shards)

    def body(pref, *refs):
        ins, outs = refs[:n + 2], refs[n + 2:]
        for a in range(n):
            outs[a][0] = ins[a][...].astype(MXU_DTYPE)
        outs[n][0] = ins[n][...]
        outs[n + 1][...] = ins[n + 1][...].astype(MXU_DTYPE)

    full = lambda a: pl.BlockSpec(a.shape, lambda i, pref: (0,) * a.ndim)
    slot = lambda a: pl.BlockSpec((1,) + a.shape, lambda i, pref: (pref[1],) + (0,) * a.ndim)
    arrs = list(shards) + [dww, wg]
    return pl.pallas_call(
        body, name="cast_weights",
        grid_spec=pltpu.PrefetchScalarGridSpec(
            num_scalar_prefetch=1, grid=(1,), in_specs=[full(a) for a in arrs],
            out_specs=[slot(a) for a in arrs[:n + 1]] + [full(wg)]),
        out_shape=[jax.ShapeDtypeStruct((N_CHIP,) + a.shape, MXU_DTYPE) for a in shards]
        + [jax.ShapeDtypeStruct((N_CHIP,) + dww.shape, F32), jax.ShapeDtypeStruct(wg.shape, MXU_DTYPE)],
        compiler_params=pltpu.CompilerParams(dimension_semantics=("arbitrary",)),
    )(place, *arrs)


def _mixer_fwd(x, mod, g1, w_in, dww, dwb, lng, lnb, w_pw, wg, pscale, w_out, comm=None):
    s = x.shape[0]
    ts = _token_tile(s)
    nt = s // ts

    def body(x_ref, mod_ref, g1_ref, win_ref, dww_ref, dwb_ref, lng_ref, lnb_ref, wpw_ref, wg_ref, ps_ref,
             wout_ref, x2_ref, y_ref, u_ref, z_ref, rstd_ref, p_ref, ycat_ref, gpad, vpad):
        i = pl.program_id(0)

        @pl.when(i == 0)
        def _():
            gpad[0:CONV_HALO, :] = jnp.zeros((CONV_HALO, CONV_W), F32)
            vpad[0:POOL_HALO, :] = jnp.zeros((POOL_HALO, POOL_W), F32)

        xt = x_ref[...]
        sh1 = mod_ref[0:1, :]
        sc1 = mod_ref[1:2, :]
        gt1 = mod_ref[2:3, :]
        r1 = lax.rsqrt(jnp.mean(xt * xt, axis=-1, keepdims=True) + EPS)
        h1 = (xt * r1 * g1_ref[...]) * (1.0 + sc1) + sh1
        h1b = h1.astype(MXU_DTYPE)
        u = jnp.concatenate([_dot(h1b, win_ref[j]) for j in range(N_CHIP)], axis=1)
        u_ref[...] = u
        a = u[:, :CONV_W]
        g = u[:, CONV_W:2 * CONV_W]
        v = u[:, 2 * CONV_W:]

        gpad[CONV_HALO:CONV_HALO + ts, :] = a * _sigmoid(g)
        cv = jnp.broadcast_to(dwb_ref[...], (ts, CONV_W))
        off = CONV_HALO - (CONV_K - 1)
        for k in range(CONV_K):
            cv = cv + dww_ref[k:k + 1, :] * gpad[off + k:off + k + ts, :]
        gpad[0:CONV_HALO, :] = gpad[ts:ts + CONV_HALO, :]

        mu = jnp.mean(cv, axis=-1, keepdims=True)
        cc = cv - mu
        rstd = lax.rsqrt(jnp.mean(cc * cc, axis=-1, keepdims=True) + EPS)
        z = cc * rstd
        z_ref[...] = z
        rstd_ref[...] = rstd
        ln = z * lng_ref[...] + lnb_ref[...]
        sw = ln * _sigmoid(ln)
        yconv = _dot(sw, wpw_ref[...])

        vpad[POOL_HALO:POOL_HALO + ts, :] = v
        t = i * ts + lax.broadcasted_iota(jnp.int32, (ts, 1), 0)
        ps, ypool = [], []
        for gi, w in enumerate(POOL_WINDOWS):
            cols = slice(gi * POOL_G, (gi + 1) * POOL_G)
            acc = vpad[POOL_HALO:POOL_HALO + ts, cols]
            for d in range(1, w):
                acc = acc + vpad[POOL_HALO - d:POOL_HALO - d + ts, cols]
            cnt = jnp.minimum(t + 1, w).astype(F32)
            pg = (acc / cnt - v[:, cols]).astype(MXU_DTYPE)
            ps.append(pg)
            ypool.append(_dot(pg, wg_ref[gi]))
        vpad[0:POOL_HALO, :] = vpad[ts:ts + POOL_HALO, :]
        p_ref[...] = jnp.concatenate(ps, axis=1)
        ypool = jnp.concatenate(ypool, axis=1) * ps_ref[...]

        ycat = jnp.concatenate([yconv, ypool], axis=1).astype(MXU_DTYPE)
        ycat_ref[...] = ycat
        y = _dot(ycat, wout_ref[...])
        y_ref[...] = y
        x2_ref[...] = xt + gt1 * y

    tile = lambda w: pl.BlockSpec((ts, w), lambda i: (i, 0))
    return _call(
        body, name="mixer_fwd", grid=(nt,),
        in_specs=[tile(D_MODEL), _full(mod.shape), _full(g1.shape), _full(w_in.shape), _full(dww.shape),
                  _full(dwb.shape), _full(lng.shape), _full(lnb.shape), _full(w_pw.shape), _full(wg.shape),
                  _full(pscale.shape), _full(w_out.shape)],
        out_specs=[tile(D_MODEL), tile(D_MODEL), tile(IN_W), tile(CONV_W), tile(1), tile(POOL_W), tile(D_MODEL)],
        out_shape=[jax.ShapeDtypeStruct((s, D_MODEL), F32), jax.ShapeDtypeStruct((s, D_MODEL), F32),
                   jax.ShapeDtypeStruct((s, IN_W), F32), jax.ShapeDtypeStruct((s, CONV_W), F32),
                   jax.ShapeDtypeStruct((s, 1), F32), jax.ShapeDtypeStruct((s, POOL_W), MXU_DTYPE),
                   jax.ShapeDtypeStruct((s, D_MODEL), MXU_DTYPE)],
        scratch_shapes=[pltpu.VMEM((ts + CONV_HALO, CONV_W), F32), pltpu.VMEM((ts + POOL_HALO, POOL_W), F32)],
        args=(x, mod, g1, w_in, dww, dwb, lng, lnb, w_pw, wg, pscale, w_out), comm=comm)


def _ffn(x2, tgt, mod, g2, gf, w_gate, w_up, w_down):
    s = x2.shape[0]
    ts = _token_tile(s)
    nt = s // ts
    fb = w_gate.shape[1]

    def body(x2_ref, tgt_ref, mod_ref, g2_ref, gf_ref, wgt_ref, wup_ref, wdn_ref,
             dx2_ref, h2_ref, df_ref, act_ref, dgg_ref, duu_ref, vec_ref, gg_s, uu_s):
        i = pl.program_id(0)

        @pl.when(i == 0)
        def _():
            vec_ref[...] = jnp.zeros(vec_ref.shape, F32)

        x2t = x2_ref[...]
        sh2 = mod_ref[3:4, :]
        sc2 = mod_ref[4:5, :]
        gt2 = mod_ref[5:6, :]
        g2v = g2_ref[...]
        gfv = gf_ref[...]
        r2 = lax.rsqrt(jnp.mean(x2t * x2t, axis=-1, keepdims=True) + EPS)
        xh2 = x2t * r2
        n2 = xh2 * g2v
        h2b = (n2 * (1.0 + sc2) + sh2).astype(MXU_DTYPE)
        h2_ref[...] = h2b
        f = jnp.zeros((ts, D_MODEL), F32)
        for j in range(N_CHIP):
            gg = _dot_nt(h2b, wgt_ref[j])
            uu = _dot_nt(h2b, wup_ref[j])
            gg_s[j] = gg
            uu_s[j] = uu
            actb = (gg * _sigmoid(gg) * uu).astype(MXU_DTYPE)
            act_ref[j] = actb
            f = f + _dot(actb, wdn_ref[j])
        x3 = x2t + gt2 * f
        r3 = lax.rsqrt(jnp.mean(x3 * x3, axis=-1, keepdims=True) + EPS)
        xh3 = x3 * r3
        diff = xh3 * gfv - tgt_ref[...]
        dout = diff * (1.0 / D_MODEL)
        dn3 = dout * gfv
        dx3 = r3 * (dn3 - xh3 * jnp.mean(dn3 * xh3, axis=-1, keepdims=True))
        dfb = (dx3 * gt2).astype(MXU_DTYPE)
        df_ref[...] = dfb
        dh2 = jnp.zeros((ts, D_MODEL), F32)
        for j in range(N_CHIP):
            dact = _dot_nt(dfb, wdn_ref[j])
            gg = gg_s[j]
            uu = uu_s[j]
            sg = _sigmoid(gg)
            duu = (dact * (gg * sg)).astype(MXU_DTYPE)
            dgg = (dact * uu * (sg * (1.0 + gg * (1.0 - sg)))).astype(MXU_DTYPE)
            duu_ref[j] = duu
            dgg_ref[j] = dgg
            dh2 = dh2 + _dot(dgg, wgt_ref[j]) + _dot(duu, wup_ref[j])
        dn2 = dh2 * (1.0 + sc2)
        dxh2 = dn2 * g2v
        dx2_ref[...] = dx3 + r2 * (dxh2 - xh2 * jnp.mean(dxh2 * xh2, axis=-1, keepdims=True))

        col = lambda a: jnp.sum(a, axis=0, keepdims=True)
        vec_ref[0:1, :] += col(dout * xh3)
        vec_ref[1:2, :] += col(dx3 * f)
        vec_ref[2:3, :] += col(dh2)
        vec_ref[3:4, :] += col(dh2 * n2)
        vec_ref[4:5, :] += col(dn2 * xh2)
        vec_ref[5:6, :] += col(diff * diff)

    tile = lambda w: pl.BlockSpec((ts, w), lambda i: (i, 0))
    tile3 = pl.BlockSpec((N_CHIP, ts, fb), lambda i: (0, i, 0))
    once = lambda a: pl.BlockSpec(a.shape, lambda i: (0,) * a.ndim, pipeline_mode=pl.Buffered(1))
    hid = jax.ShapeDtypeStruct((N_CHIP, s, fb), MXU_DTYPE)
    return pl.pallas_call(
        body, name="ffn", grid=(nt,),
        in_specs=[tile(D_MODEL), tile(D_MODEL), _full(mod.shape), _full(g2.shape), _full(gf.shape),
                  once(w_gate), once(w_up), once(w_down)],
        out_specs=[tile(D_MODEL), tile(D_MODEL), tile(D_MODEL), tile3, tile3, tile3, _full((8, D_MODEL))],
        out_shape=[jax.ShapeDtypeStruct((s, D_MODEL), F32), jax.ShapeDtypeStruct((s, D_MODEL), MXU_DTYPE),
                   jax.ShapeDtypeStruct((s, D_MODEL), MXU_DTYPE), hid, hid, hid,
                   jax.ShapeDtypeStruct((8, D_MODEL), F32)],
        scratch_shapes=[pltpu.VMEM((N_CHIP, ts, fb), F32), pltpu.VMEM((N_CHIP, ts, fb), F32)],
        compiler_params=pltpu.CompilerParams(dimension_semantics=("arbitrary",)),
    )(x2, tgt, mod, g2, gf, w_gate, w_up, w_down)


def _mixer_bwd(dx2, x, y, u, z, rstd, p, mod, g1, w_in, dww, lng, lnb, w_pw, wg, pscale, w_out, comm=None):
    s = x.shape[0]
    ts = _token_tile(s)
    nt = s // ts

    def body(dx2_ref, x_ref, y_ref, u_ref, z_ref, rstd_ref, p_ref, mod_ref, g1_ref, win_ref, dww_ref, lng_ref,
             lnb_ref, wpw_ref, wg_ref, ps_ref, wout_ref,
             gx_ref, h1_ref, du_ref, dy_ref, sw_ref, dyc_ref, dyp_ref, vd_ref, vc_ref, ddw_ref, dcpad, dppad):
        i = pl.program_id(0)
        tix = nt - 1 - i

        @pl.when(i == 0)
        def _():
            vd_ref[...] = jnp.zeros(vd_ref.shape, F32)
            vc_ref[...] = jnp.zeros(vc_ref.shape, F32)
            ddw_ref[...] = jnp.zeros(ddw_ref.shape, F32)
            dcpad[ts:ts + CONV_HALO, :] = jnp.zeros((CONV_HALO, CONV_W), F32)
            dppad[ts:ts + POOL_HALO, :] = jnp.zeros((POOL_HALO, POOL_W), F32)

        col = lambda a: jnp.sum(a, axis=0, keepdims=True)
        sh1 = mod_ref[0:1, :]
        sc1 = mod_ref[1:2, :]
        gt1 = mod_ref[2:3, :]
        dx2t = dx2_ref[...]
        vd_ref[0:1, :] += col(dx2t * y_ref[...])
        dyb = (dx2t * gt1).astype(MXU_DTYPE)
        dy_ref[...] = dyb
        dycat = _dot_nt(dyb, wout_ref[...])
        dyconv = dycat[:, :CONV_W]
        dypool = dycat[:, CONV_W:]

        pt = p_ref[...]
        t = tix * ts + lax.broadcasted_iota(jnp.int32, (ts, 1), 0)
        psc = ps_ref[...]
        dypb = (dypool * psc).astype(MXU_DTYPE)
        dyp_ref[...] = dypb
        dps, ypre = [], []
        for gi, w in enumerate(POOL_WINDOWS):
            cols = slice(gi * POOL_G, (gi + 1) * POOL_G)
            ypre.append(_dot(pt[:, cols], wg_ref[gi]))
            dpg = _dot_nt(dypb[:, cols], wg_ref[gi])
            dps.append(dpg)
            cnt = jnp.minimum(t + 1, w).astype(F32)
            dppad[0:ts, cols] = dpg / cnt
        vc_ref[0:1, :] += col(dypool * jnp.concatenate(ypre, axis=1))
        dvs = []
        for gi, w in enumerate(POOL_WINDOWS):
            cols = slice(gi * POOL_G, (gi + 1) * POOL_G)
            acc = dppad[0:ts, cols]
            for d in range(1, w):
                acc = acc + dppad[d:d + ts, cols]
            dvs.append(acc - dps[gi])
        dv = jnp.concatenate(dvs, axis=1)
        dppad[ts:ts + POOL_HALO, :] = dppad[0:POOL_HALO, :]

        zt = z_ref[...]
        lngv = lng_ref[...]
        ln = zt * lngv + lnb_ref[...]
        sg = _sigmoid(ln)
        swb = (ln * sg).astype(MXU_DTYPE)
        sw_ref[...] = swb
        dycb = dyconv.astype(MXU_DTYPE)
        dyc_ref[...] = dycb
        dln = _dot_nt(dycb, wpw_ref[...]) * (sg * (1.0 + ln * (1.0 - sg)))
        vc_ref[1:2, :] += col(dln * zt)
        vc_ref[2:3, :] += col(dln)
        dz = dln * lngv
        dcv = rstd_ref[...] * (dz - jnp.mean(dz, axis=-1, keepdims=True)
                               - zt * jnp.mean(dz * zt, axis=-1, keepdims=True))
        vc_ref[3:4, :] += col(dcv)
        dcpad[0:ts, :] = dcv
        ut = u_ref[...]
        a = ut[:, :CONV_W]
        g = ut[:, CONV_W:2 * CONV_W]
        sgg = _sigmoid(g)
        glu = a * sgg
        dglu = jnp.zeros((ts, CONV_W), F32)
        for k in range(CONV_K):
            sh = dcpad[CONV_K - 1 - k:CONV_K - 1 - k + ts, :]
            dglu = dglu + dww_ref[k:k + 1, :] * sh
            ddw_ref[k:k + 1, :] += col(glu * sh)
        dcpad[ts:ts + CONV_HALO, :] = dcpad[0:CONV_HALO, :]
        da = dglu * sgg
        dg = dglu * a * sgg * (1.0 - sgg)
        dub = jnp.concatenate([da, dg, dv], axis=1).astype(MXU_DTYPE)
        du_ref[...] = dub
        cw = IN_W // N_CHIP
        dh1 = jnp.zeros((ts, D_MODEL), F32)
        for j in range(N_CHIP):
            dh1 = dh1 + _dot_nt(dub[:, j * cw:(j + 1) * cw], win_ref[j])

        xt = x_ref[...]
        g1v = g1_ref[...]
        r1 = lax.rsqrt(jnp.mean(xt * xt, axis=-1, keepdims=True) + EPS)
        xh1 = xt * r1
        n1 = xh1 * g1v
        h1_ref[...] = (n1 * (1.0 + sc1) + sh1).astype(MXU_DTYPE)
        vd_ref[1:2, :] += col(dh1)
        vd_ref[2:3, :] += col(dh1 * n1)
        dn1 = dh1 * (1.0 + sc1)
        vd_ref[3:4, :] += col(dn1 * xh1)
        dxh = dn1 * g1v
        gx_ref[...] = dx2t + r1 * (dxh - xh1 * jnp.mean(dxh * xh1, axis=-1, keepdims=True))

    tile = lambda w: pl.BlockSpec((ts, w), lambda i: (nt - 1 - i, 0))
    bf = lambda w: jax.ShapeDtypeStruct((s, w), MXU_DTYPE)
    return _call(
        body, name="mixer_bwd", grid=(nt,),
        in_specs=[tile(D_MODEL), tile(D_MODEL), tile(D_MODEL), tile(IN_W), tile(CONV_W), tile(1), tile(POOL_W),
                  _full(mod.shape), _full(g1.shape), _full(w_in.shape), _full(dww.shape), _full(lng.shape),
                  _full(lnb.shape), _full(w_pw.shape), _full(wg.shape), _full(pscale.shape), _full(w_out.shape)],
        out_specs=[tile(D_MODEL), tile(D_MODEL), tile(IN_W), tile(D_MODEL), tile(CONV_W), tile(CONV_W),
                   tile(POOL_W), _full((8, D_MODEL)), _full((8, CONV_W)), _full((32, CONV_W))],
        out_shape=[jax.ShapeDtypeStruct((s, D_MODEL), F32), bf(D_MODEL), bf(IN_W), bf(D_MODEL), bf(CONV_W),
                   bf(CONV_W), bf(POOL_W), jax.ShapeDtypeStruct((8, D_MODEL), F32),
                   jax.ShapeDtypeStruct((8, CONV_W), F32), jax.ShapeDtypeStruct((32, CONV_W), F32)],
        scratch_shapes=[pltpu.VMEM((ts + CONV_HALO, CONV_W), F32), pltpu.VMEM((ts + POOL_HALO, POOL_W), F32)],
        args=(dx2, x, y, u, z, rstd, p, mod, g1, w_in, dww, lng, lnb, w_pw, wg, pscale, w_out), comm=comm)


def _dw(name, a, a_spec, b, b_spec, nb, mb, nbk, comm=None):
    def body(a_ref, b_ref, o_ref):
        av = a_ref[...]
        bv = b_ref[...]
        av = av.reshape(av.shape[-2:])
        bv = bv.reshape(bv.shape[-2:])
        o_ref[0] = _dot_tn(av, bv)

    (out,), rest = _call(
        body, name=name, grid=(nb,), in_specs=[a_spec, b_spec],
        out_specs=[pl.BlockSpec((1, mb, nbk), lambda j: (j, 0, 0))],
        out_shape=[jax.ShapeDtypeStruct((nb, mb, nbk), F32)], args=(a, b), comm=comm)
    return out, rest


def _ada_fwd(c, w_ada, b4, comm=None):
    nc = w_ada.shape[1]

    def body(start_comm, c_ref, w_ref, b4_ref, mod_ref, cact_ref, call, part, parts, send1, recv1, send2, recv2):
        x, y, cc = _place()
        b = 4 * x + 2 * y + cc
        j = 2 * x + y
        call[b] = c_ref[...]
        sends = []
        for r in range(1, N_DEV):
            dev = ((1 - x) if r & 4 else x, (1 - y) if r & 2 else y, (1 - cc) if r & 1 else cc)
            cp = _remote(call.at[b], call.at[b], send1.at[r - 1], recv1.at[r - 1], dev)
            cp.start()
            sends.append(cp)
        for r in range(1, N_DEV):
            src_b = lax.bitwise_xor(b, r)
            _remote(call.at[src_b], call.at[src_b], send1.at[r - 1], recv1.at[r - 1], (x, y, cc)).wait_recv()
        for cp in sends:
            cp.wait_send()
        start_comm()
        for i in range(N_DEV):
            ci = call[i]
            cact_ref[i:i + 1, :] = ci * _sigmoid(ci)
        part[...] = jnp.dot(cact_ref[...], w_ref[...], preferred_element_type=F32, precision=lax.Precision.HIGHEST)
        sends = []
        for r in range(1, N_CHIP):
            kx, ky = _flip(x, y, r)
            cp = _remote(part, parts.at[j], send2.at[r - 1], recv2.at[r - 1], (kx, ky, cc))
            cp.start()
            sends.append(cp)
        parts[j] = part[...]
        for r in range(1, N_CHIP):
            kx, ky = _flip(x, y, r)
            kj = 2 * kx + ky
            _remote(part, parts.at[kj], send2.at[r - 1], recv2.at[r - 1], (x, y, cc)).wait_recv()
        for cp in sends:
            cp.wait_send()
        mine = lax.broadcasted_iota(jnp.int32, (N_DEV, 1), 0) == b
        for k in range(N_CHIP):
            row = jnp.sum(jnp.where(mine, parts[k], 0.0), axis=0, keepdims=True)
            mod_ref[k:k + 1, :] = row + b4_ref[k:k + 1, :]

    return _call(
        body, name="ada_fwd", grid=(1,),
        in_specs=[VMEM, VMEM, VMEM], out_specs=[VMEM, VMEM],
        out_shape=[jax.ShapeDtypeStruct((N_CHIP, nc), F32), jax.ShapeDtypeStruct((N_DEV, D_MODEL), F32)],
        scratch_shapes=[pltpu.VMEM((N_DEV, 1, D_MODEL), F32), pltpu.VMEM((N_DEV, nc), F32),
                        pltpu.VMEM((N_CHIP, N_DEV, nc), F32),
                        pltpu.SemaphoreType.DMA((N_DEV - 1,)), pltpu.SemaphoreType.DMA((N_DEV - 1,)),
                        pltpu.SemaphoreType.DMA((N_CHIP - 1,)), pltpu.SemaphoreType.DMA((N_CHIP - 1,))],
        args=(c, w_ada, b4), comm=comm, body_starts=True)


def _chip_partials(name, place, gs, rs, comm=None):
    n = len(gs)

    def body(pref, *refs):
        g_refs, r_refs = refs[:n], refs[n:2 * n]
        pb_refs, own_refs = refs[2 * n:3 * n], refs[3 * n:]
        jj = pl.program_id(0)
        for a in range(n):
            sm = g_refs[a][0] + r_refs[a][0]
            pb_refs[a][0] = sm.astype(MXU_DTYPE)

            @pl.when(jj == pref[1])
            def _(a=a, sm=sm):
                own_refs[a][...] = sm

    halves = [(g.shape[1] // 2, g.shape[2]) for g in gs]
    in_specs = [pl.BlockSpec((1, h, w), lambda jj, pref: (jj, pref[0], 0)) for h, w in halves]
    in_specs += [pl.BlockSpec((1, h, w), lambda jj, pref: (jj, 0, 0)) for h, w in halves]
    out_specs = [pl.BlockSpec((1, h, w), lambda jj, pref: (jj, 0, 0)) for h, w in halves]
    out_specs += [pl.BlockSpec((h, w), lambda jj, pref: (0, 0)) for h, w in halves]
    out, rest = _call(
        body, name=name, grid=(N_CHIP,), in_specs=in_specs, out_specs=out_specs,
        out_shape=[jax.ShapeDtypeStruct((N_CHIP, h, w), MXU_DTYPE) for h, w in halves]
        + [jax.ShapeDtypeStruct((h, w), F32) for h, w in halves],
        args=(*gs, *rs), prefetch=(place,), comm=comm)
    return (out[:n], out[n:]), rest


def _sum_partials(name, place, owns, recvd, comm=None):
    n = len(owns)

    def body(pref, *refs):
        o_refs, r_refs, out_refs = refs[:n], refs[n:2 * n], refs[2 * n:]
        for a in range(n):
            acc = o_refs[a][...]
            for r in range(N_CHIP - 1):
                acc = acc + r_refs[a][r].astype(F32)
            out_refs[a][...] = acc

    full = lambda a: pl.BlockSpec(a.shape, lambda i, pref: (0,) * a.ndim)
    return _call(
        body, name=name, grid=(1,), in_specs=[full(a) for a in list(owns) + list(recvd)],
        out_specs=[pl.BlockSpec(o.shape, lambda i, pref: (pref[0], 0)) for o in owns],
        out_shape=[jax.ShapeDtypeStruct((2 * o.shape[0], o.shape[1]), F32) for o in owns],
        args=(*owns, *recvd), prefetch=(place,), comm=comm)


def _adamw_math(w, g, m, v):
    m = ADAM_B1 * m + (1.0 - ADAM_B1) * g
    v = ADAM_B2 * v + (1.0 - ADAM_B2) * (g * g)
    m_hat = m / (1.0 - ADAM_B1 ** ADAM_STEP)
    v_hat = v / (1.0 - ADAM_B2 ** ADAM_STEP)
    delta = -ADAM_LR * (m_hat / (jnp.sqrt(v_hat) + ADAM_EPS) + ADAM_WD * w)
    return delta, m, v


def _row_tile(rows):
    for t in (512, 352, 256, 128):
        if rows % t == 0:
            return t
    return rows


def _adamw(name, wgmv, steps, after=()):
    n = len(wgmv)

    def body(*refs):
        ins, outs = refs[:4 * n], refs[4 * n:]
        for i in range(n):
            w_ref, g_ref, m_ref, v_ref = ins[4 * i:4 * i + 4]
            d_ref, nm_ref, nv_ref = outs[3 * i:3 * i + 3]
            d_ref[...], nm_ref[...], nv_ref[...] = _adamw_math(w_ref[...], g_ref[...], m_ref[...], v_ref[...])

    in_specs, out_specs, out_shape, args = [], [], [], []
    for w, g, m, v in wgmv:
        rows, cols = w.shape
        spec = pl.BlockSpec((rows // steps, cols), lambda i: (i, 0))
        in_specs += [spec] * 4
        out_specs += [spec] * 3
        out_shape += [jax.ShapeDtypeStruct(w.shape, F32)] * 3
        args += [w, g, m, v]
    res, _ = _call(body, name=name, grid=(steps,), in_specs=in_specs, out_specs=out_specs, out_shape=out_shape,
                   args=args, after=after)
    return [res[3 * i:3 * i + 3] for i in range(n)]


def _adamw_ada(place, cact, dmod, w, m, v, after=()):
    rows, cols = w.shape
    tr = _row_tile(rows)

    def body(pref, ca_ref, dm_ref, w_ref, m_ref, v_ref, g_ref, d_ref, nm_ref, nv_ref):
        g = lax.dot_general(ca_ref[...], dm_ref[...], (((0,), (0,)), ((), ())), preferred_element_type=F32,
                            precision=lax.Precision.HIGHEST)
        g_ref[...] = g
        d_ref[...], nm_ref[...], nv_ref[...] = _adamw_math(w_ref[...], g, m_ref[...], v_ref[...])

    spec = pl.BlockSpec((tr, cols), lambda i, pref: (i, 0))
    return _call(
        body, name="adamw_ada", grid=(rows // tr,),
        in_specs=[pl.BlockSpec((N_DEV, tr), lambda i, pref: (0, i)),
                  pl.BlockSpec((N_DEV, cols), lambda i, pref: (0, pref[1])), spec, spec, spec],
        out_specs=[spec] * 4, out_shape=[jax.ShapeDtypeStruct(w.shape, F32)] * 4,
        args=(cact, dmod, w, m, v), prefetch=(place,), after=after)[0]


def _adamw_small(place, vf_all, vd_all, vc_all, ddw_all, gwg_all, wmv):
    nw = len(wmv)
    flat = [a for t in wmv for a in t]

    def body(pref, vf_ref, vd_ref, vc_ref, ddw_ref, gwg_ref, *refs):
        w_refs = refs[:3 * nw]
        loss_ref, dmod_ref = refs[3 * nw], refs[3 * nw + 1]
        o_refs = refs[3 * nw + 2:]
        j = pref[1]

        def total(ref):
            acc = ref[0]
            for b in range(1, N_DEV):
                acc = acc + ref[b]
            return acc

        vf, vd, vc, ddw, gwg = total(vf_ref), total(vd_ref), total(vc_ref), total(ddw_ref), total(gwg_ref)
        loss_ref[...] = (0.5 / D_MODEL) * jnp.sum(vf[5:6, :], axis=1, keepdims=True)
        order = ((vd_ref, 1), (vd_ref, 2), (vd_ref, 0), (vf_ref, 2), (vf_ref, 3), (vf_ref, 1))
        for b in range(N_DEV):
            for q, (ref, row) in enumerate(order):
                dmod_ref[b:b + 1, q * D_MODEL:(q + 1) * D_MODEL] = ref[b, row:row + 1, :]
        dm = dmod_ref[...]
        g_bada = dm[0:1, :]
        for b in range(1, N_DEV):
            g_bada = g_bada + dm[b:b + 1, :]
        g_dww = jnp.zeros((32, POOL_G), F32)
        for k in range(N_CHIP):
            g_dww = g_dww + jnp.where(j == k, ddw[:, k * POOL_G:(k + 1) * POOL_G], 0.0)
        grads = [g_bada, vd[3:4, :], g_dww, vc[3:4, :], vc[1:2, :], vc[2:3, :], gwg, vc[0:1, :], vf[4:5, :],
                 vf[0:1, :]]
        for i, g in enumerate(grads):
            w_ref, m_ref, v_ref = w_refs[3 * i:3 * i + 3]
            d, nm, nv = _adamw_math(w_ref[...], g, m_ref[...], v_ref[...])
            o_refs[4 * i][...] = g
            o_refs[4 * i + 1][...] = d
            o_refs[4 * i + 2][...] = nm
            o_refs[4 * i + 3][...] = nv

    gathered = [vf_all, vd_all, vc_all, ddw_all, gwg_all]
    outs = [jax.ShapeDtypeStruct((1, 1), F32), jax.ShapeDtypeStruct((N_DEV, 6 * D_MODEL), F32)]
    for w, _, _ in wmv:
        outs += [jax.ShapeDtypeStruct(w.shape, F32)] * 4
    full = lambda a: pl.BlockSpec(a.shape, lambda i, pref: (0,) * a.ndim)
    res = pl.pallas_call(
        body, name="adamw_small",
        grid_spec=pltpu.PrefetchScalarGridSpec(
            num_scalar_prefetch=1, grid=(1,),
            in_specs=[full(a) for a in gathered + flat], out_specs=[full(o) for o in outs]),
        out_shape=outs,
        compiler_params=pltpu.CompilerParams(dimension_semantics=("arbitrary",)),
    )(place, *gathered, *flat)
    return res[0], res[1], [res[2 + 4 * i:6 + 4 * i] for i in range(nw)]


def kernel(x, c, w_ada, b_ada, g_norm1, w_in, dw_w, dw_b, conv_ln_g, conv_ln_b, w_conv_pw, w_pool_group, pool_scale, w_out, g_norm2, w_ffn_gate, w_ffn_up, w_ffn_down, g_final, loss_target, m_w_ada, m_b_ada, m_g_norm1, m_w_in, m_dw_w, m_dw_b, m_conv_ln_g, m_conv_ln_b, m_w_conv_pw, m_w_pool_group, m_pool_scale, m_w_out, m_g_norm2, m_w_ffn_gate, m_w_ffn_up, m_w_ffn_down, m_g_final, v_w_ada, v_b_ada, v_g_norm1, v_w_in, v_dw_w, v_dw_b, v_conv_ln_g, v_conv_ln_b, v_w_conv_pw, v_w_pool_group, v_pool_scale, v_w_out, v_g_norm2, v_w_ffn_gate, v_w_ffn_up, v_w_ffn_down, v_g_final):
    xi, yi, ci = _place()
    place = jnp.stack([ci, 2 * xi + yi]).astype(jnp.int32)
    n_ada = w_ada.shape[2]

    tr = lambda a: jnp.transpose(a[0])
    big = [w_in[0], w_conv_pw[0], w_out[0], tr(w_ffn_gate), tr(w_ffn_up), w_ffn_down[0]]
    b_in, b_pw, b_out, b_gate, b_up, b_down, b_dww, wg_b = _cast_weights(place, big, dw_w[0], w_pool_group[0])

    (mod4, cact), (win_g, wpw_g, wout_g, dww_g) = _ada_fwd(
        c, w_ada[0], b_ada.reshape(N_CHIP, n_ada),
        comm=_weights_gather([b_in, b_pw, b_out, b_dww], [True, True, True, False]))
    mod = mod4.reshape(6, D_MODEL)
    dww_full = jnp.pad(jnp.concatenate([dww_g[k] for k in range(N_CHIP)], axis=1), ((0, 1), (0, 0)))
    w_pw = wpw_g.reshape(CONV_W, CONV_W)
    w_o = wout_g.reshape(D_MODEL, D_MODEL)
    xs, tgt, gf = x[0], loss_target[0], g_final.reshape(1, D_MODEL)
    s = xs.shape[0]
    fb = b_gate.shape[1]

    (x2, y, u, z, rstd, p, ycat), (wgate_g, wup_g, wdown_g) = _mixer_fwd(
        xs, mod, g_norm1, win_g, dww_full, dw_b, conv_ln_g, conv_ln_b, w_pw, wg_b, pool_scale, w_o,
        comm=_weights_gather([b_gate, b_up, b_down], [True, True, True]))
    dx2, h2, df, act, dgg, duu, vec_f = _ffn(x2, tgt, mod, g_norm2, gf, wgate_g, wup_g, wdown_g)

    whole = lambda w: pl.BlockSpec((s, w), lambda j: (0, 0))
    cols = lambda w: pl.BlockSpec((s, w), lambda j: (0, j))
    hid = pl.BlockSpec((1, s, fb), lambda j: (j, 0, 0))
    c_gate, _ = _dw("dw_gate", dgg, hid, h2, whole(D_MODEL), N_CHIP, fb, D_MODEL)
    c_up, (r_gate,) = _dw("dw_up", duu, hid, h2, whole(D_MODEL), N_CHIP, fb, D_MODEL, comm=_sibling_halves([c_gate]))
    ((pb_gate,), (own_gate,)), _ = _chip_partials("partials_gate", place, [c_gate], [r_gate])
    c_down, (r_up, rc_gate) = _dw("dw_down", act, hid, df, whole(D_MODEL), N_CHIP, fb, D_MODEL,
                                  comm=_both(_sibling_halves([c_up]), _exchange_partials([pb_gate])))
    ((pb_up,), (own_up,)), (r_down,) = _chip_partials("partials_up", place, [c_up], [r_up],
                                                      comm=_sibling_halves([c_down]))
    ((pb_down,), (own_down,)), _ = _chip_partials("partials_down", place, [c_down], [r_down])
    (gx, h1, du, dy, sw, dyc, dyp, vec_d, vec_c, ddw), (rc_up, rc_down) = _mixer_bwd(
        dx2, xs, y, u, z, rstd, p, mod, g_norm1, win_g, dww_full, conv_ln_g, conv_ln_b, w_pw, wg_b, pool_scale, w_o,
        comm=_exchange_partials([pb_up, pb_down]))

    g_wg, _ = _dw("dw_wg", p, cols(POOL_G), dyp, cols(POOL_G), len(POOL_WINDOWS), POOL_G, POOL_G)
    c_in, (vf_all, vd_all, vc_all, ddw_all, gwg_all) = _dw(
        "dw_in", h1, whole(D_MODEL), du, cols(IN_W // N_CHIP), N_CHIP, D_MODEL, IN_W // N_CHIP,
        comm=_small_gather([vec_f, vec_d, vec_c, ddw, g_wg]))
    c_out, (r_in,) = _dw("dw_out", ycat, cols(D_MODEL // N_CHIP), dy, whole(D_MODEL), N_CHIP, D_MODEL // N_CHIP,
                         D_MODEL, comm=_sibling_halves([c_in]))
    c_pw, (r_out,) = _dw("dw_pw", sw, cols(CONV_W // N_CHIP), dyc, whole(CONV_W), N_CHIP, CONV_W // N_CHIP, CONV_W,
                         comm=_sibling_halves([c_out]))

    ffn_fulls, (r_pw,) = _sum_partials("sum_ffn", place, [own_gate, own_up, own_down], [rc_gate, rc_up, rc_down],
                                       comm=_sibling_halves([c_pw]))
    (pbs_mix, owns_mix), (g_gate, g_up, g_down) = _chip_partials(
        "partials_mix", place, [c_in, c_pw, c_out], [r_in, r_pw, r_out], comm=_join_halves(ffn_fulls))

    pad_rows = lambda a: jnp.pad(a[0], ((0, 1), (0, 0)))
    row = lambda a: a.reshape(1, -1)
    small = [(b_ada, m_b_ada, v_b_ada), (g_norm1, m_g_norm1, v_g_norm1),
             (pad_rows(dw_w), pad_rows(m_dw_w), pad_rows(v_dw_w)), (dw_b, m_dw_b, v_dw_b),
             (conv_ln_g, m_conv_ln_g, v_conv_ln_g), (conv_ln_b, m_conv_ln_b, v_conv_ln_b),
             (w_pool_group[0], m_w_pool_group[0], v_w_pool_group[0]), (pool_scale, m_pool_scale, v_pool_scale),
             (g_norm2, m_g_norm2, v_g_norm2), (row(g_final), row(m_g_final), row(v_g_final))]
    loss, dmod, small_out = _adamw_small(place, vf_all, vd_all, vc_all, ddw_all, gwg_all, small)
    (o_bada, o_g1, o_dww, o_dwb, o_lng, o_lnb, o_wg, o_ps, o_g2, o_gf) = small_out
    o_dww = [a[:CONV_K] for a in o_dww]
    o_gf = [a.reshape(D_MODEL) for a in o_gf]
    lead = lambda outs: [a[None] for a in outs]

    lands, sem_shape, copies = _exchange_parts(pbs_mix)
    state, token = _split_start("exchange_mix_start", pbs_mix, lands, sem_shape, copies)
    u_gate, u_up, u_down = _adamw(
        "adamw_ffn", [(tr(w_ffn_gate), g_gate, tr(m_w_ffn_gate), tr(v_w_ffn_gate)),
                      (tr(w_ffn_up), g_up, tr(m_w_ffn_up), tr(v_w_ffn_up)),
                      (w_ffn_down[0], g_down, m_w_ffn_down[0], v_w_ffn_down[0])],
        steps=4, after=(token,))
    o_gate = [jnp.transpose(o) for o in [g_gate] + list(u_gate)]
    o_up = [jnp.transpose(o) for o in [g_up] + list(u_up)]
    o_down = [g_down] + list(u_down)
    o_ada = _adamw_ada(place, cact, dmod, w_ada[0], m_w_ada[0], v_w_ada[0], after=(token,))
    rc_mix = _split_wait("exchange_mix_wait", state, len(pbs_mix), sem_shape, copies, after=(o_ada[1], u_down[0]))
    mix_fulls, _ = _sum_partials("sum_mix", place, owns_mix, rc_mix)
    g_in, g_pw, g_out = _comm_only("join_mix", _join_halves(mix_fulls))
    u_in, u_pw, u_out = _adamw(
        "adamw_mix", [(w_in[0], g_in, m_w_in[0], v_w_in[0]), (w_conv_pw[0], g_pw, m_w_conv_pw[0], v_w_conv_pw[0]),
                      (w_out[0], g_out, m_w_out[0], v_w_out[0])], steps=4)
    o_in, o_pw, o_out = [g_in] + list(u_in), [g_pw] + list(u_pw), [g_out] + list(u_out)

    per_weight = [lead(o_ada), o_bada, o_g1, lead(o_in), lead(o_dww), o_dwb, o_lng, o_lnb, lead(o_pw), lead(o_wg),
                  o_ps, lead(o_out), o_g2, lead(o_gate), lead(o_up), lead(o_down), o_gf]
    result = [loss.reshape(()), gx[None]]
    for kind in range(4):
        result += [o[kind] for o in per_weight]
    return tuple(result)
```

```python
import functools

import jax
import jax.numpy as jnp
from jax import lax
from jax.experimental import pallas as pl
from jax.experimental.pallas import tpu as pltpu

F32 = jnp.float32
MXU_DTYPE = jnp.bfloat16
EPS = 1e-6

D_MODEL = 1024
CONV_W = 512
POOL_W = 512
CONV_K = 31
POOL_WINDOWS = (2, 4, 8, 16)
POOL_G = 128
IN_W = 2 * CONV_W + POOL_W
N_CHIP = 4
N_DEV = 8
CONV_HALO = 32
POOL_HALO = 16

ADAM_LR = 0.001
ADAM_B1 = 0.9
ADAM_B2 = 0.999
ADAM_EPS = 1e-08
ADAM_WD = 0.01
ADAM_STEP = 10

MESH = pl.DeviceIdType.MESH
ANY = pl.BlockSpec(memory_space=pl.ANY)
VMEM = pl.BlockSpec(memory_space=pltpu.VMEM)


def _dot(a, b):
    return jnp.dot(a.astype(MXU_DTYPE), b.astype(MXU_DTYPE), preferred_element_type=F32)


def _dot_nt(a, b):
    return lax.dot_general(a.astype(MXU_DTYPE), b.astype(MXU_DTYPE), (((1,), (1,)), ((), ())),
                           preferred_element_type=F32)


def _dot_tn(a, b):
    return lax.dot_general(a.astype(MXU_DTYPE), b.astype(MXU_DTYPE), (((0,), (0,)), ((), ())),
                           preferred_element_type=F32)


def _sigmoid(v):
    return 1.0 / (1.0 + jnp.exp(-v))


def _full(shape):
    n = len(shape)
    return pl.BlockSpec(shape, lambda *_: (0,) * n)


def _token_tile(s):
    return 256 if s % 256 == 0 else s


def _place():
    return lax.axis_index("x"), lax.axis_index("y"), lax.axis_index("c")


def _flip(x, y, r):
    return ((1 - x) if r & 2 else x, (1 - y) if r & 1 else y)


def _remote(src, dst, send_sem, recv_sem, dev):
    return pltpu.make_async_remote_copy(src_ref=src, dst_ref=dst, send_sem=send_sem, recv_sem=recv_sem,
                                        device_id=dev, device_id_type=MESH)


class _Comm:
    def __init__(self, ins, outs, aliases, scratch, start, finish, mid=None):
        self.ins, self.outs, self.aliases, self.scratch = list(ins), list(outs), dict(aliases), list(scratch)
        self.start, self.finish = start, finish
        self.mid = mid


def _both(a, b):
    na, nao, nas = len(a.ins), len(a.outs), len(a.scratch)
    aliases = dict(a.aliases)
    aliases.update({na + i: nao + o for i, o in b.aliases.items()})

    def start(ins, outs, scr):
        a.start(ins[:na], outs[:nao], scr[:nas])
        b.start(ins[na:], outs[nao:], scr[nas:])

    def finish(ins, outs, scr):
        a.finish(ins[:na], outs[:nao], scr[:nas])
        b.finish(ins[na:], outs[nao:], scr[nas:])

    def mid(ins, outs, scr):
        if a.mid:
            a.mid(ins[:na], outs[:nao], scr[:nas])
        if b.mid:
            b.mid(ins[na:], outs[nao:], scr[nas:])

    return _Comm(a.ins + b.ins, a.outs + b.outs, aliases, a.scratch + b.scratch, start, finish,
                 mid if (a.mid or b.mid) else None)


def _call(body, *, name, grid, in_specs, out_specs, out_shape, args, scratch_shapes=(), prefetch=(), comm=None,
          body_starts=False, after=()):
    in_specs = list(in_specs) + [ANY] * len(after)
    args = list(args) + list(after)
    n_pre, n_in, n_out, n_scr = len(prefetch), len(in_specs), len(out_specs), len(scratch_shapes)
    n_body_in = n_in - len(after)
    c_ins = comm.ins if comm else []
    c_outs = comm.outs if comm else []
    c_scr = comm.scratch if comm else []
    last = grid[0] - 1

    def wrapped(*refs):
        pre, refs = refs[:n_pre], refs[n_pre:]
        ins, cin = refs[:n_body_in], refs[n_in:n_in + len(c_ins)]
        refs = refs[n_in + len(c_ins):]
        outs, cout = refs[:n_out], refs[n_out:n_out + len(c_outs)]
        refs = refs[n_out + len(c_outs):]
        scr, cscr = refs[:n_scr], refs[n_scr:]
        step = pl.program_id(0)
        if comm and not body_starts:
            @pl.when(step == 0)
            def _():
                comm.start(cin, cout, cscr)

        has_mid = comm is not None and comm.mid is not None
        mid_step = grid[0] // 2 if grid[0] >= 4 else None
        if has_mid and mid_step is not None:
            @pl.when(step == mid_step)
            def _():
                comm.mid(cin, cout, cscr)

        if body_starts:
            body(lambda: comm.start(cin, cout, cscr) if comm else None, *pre, *ins, *outs, *scr)
        else:
            body(*pre, *ins, *outs, *scr)
        if comm:
            @pl.when(step == last)
            def _():
                if has_mid and mid_step is None:
                    comm.mid(cin, cout, cscr)
                comm.finish(cin, cout, cscr)

    aliases = {n_pre + n_in + a: n_out + b for a, b in (comm.aliases if comm else {}).items()}
    res = pl.pallas_call(
        wrapped, name=name,
        grid_spec=pltpu.PrefetchScalarGridSpec(
            num_scalar_prefetch=n_pre, grid=grid, in_specs=list(in_specs) + [ANY] * len(c_ins),
            out_specs=list(out_specs) + [ANY] * len(c_outs), scratch_shapes=list(scratch_shapes) + list(c_scr)),
        out_shape=list(out_shape) + list(c_outs),
        input_output_aliases=aliases,
        compiler_params=pltpu.CompilerParams(dimension_semantics=("arbitrary",)),
    )(*prefetch, *args, *c_ins)
    return res[:n_out], res[n_out:]


def _comm_only(name, comm):
    return _call(lambda: None, name=name, grid=(1,), in_specs=[], out_specs=[], out_shape=[], args=[], comm=comm)[1]


def _gather_phases(make_items):
    def own_sends(items, send, recv):
        x, y, cc = _place()
        j = 2 * x + y
        cps = []
        for a, it in enumerate(items):
            if it["sibling"]:
                cps.append(_remote(it["src"], it["dst"](j, cc), send.at[a, 0], recv.at[a, 0], (x, y, 1 - cc)))
            for r in range(1, N_CHIP):
                kx, ky = _flip(x, y, r)
                cps.append(_remote(it["src"], it["dst"](j, cc), send.at[a, r], recv.at[a, r], (kx, ky, cc)))
        return cps

    def start(ins, outs, scr):
        items = make_items(ins, outs, scr)
        send, recv, lsem = scr[-3:]
        for a, it in enumerate(items):
            if it["local"] is not None:
                src, stage, dst = it["local"]
                lc = pltpu.make_async_copy(src, stage, lsem.at[a])
                lc.start()
                lc.wait()
                pltpu.make_async_copy(stage, dst, lsem.at[a]).start()
        for cp in own_sends(items, send, recv):
            cp.start()

    def finish(ins, outs, scr):
        items = make_items(ins, outs, scr)
        send, recv, lsem = scr[-3:]
        x, y, cc = _place()
        j = 2 * x + y
        sib = (x, y, 1 - cc)
        forwards = []
        for a, it in enumerate(items):
            for r in range(1, N_CHIP):
                kx, ky = _flip(x, y, r)
                got = it["dst"](2 * kx + ky, cc)
                _remote(got, got, send.at[a, r], recv.at[a, r], sib).wait_recv()
                if it["forward"]:
                    cp = _remote(got, got, send.at[a, 3 + r], recv.at[a, 3 + r], sib)
                    cp.start()
                    forwards.append(cp)
        for a, it in enumerate(items):
            if it["sibling"]:
                got = it["dst"](j, 1 - cc)
                _remote(got, got, send.at[a, 0], recv.at[a, 0], sib).wait_recv()
            if it["forward"]:
                for r in range(1, N_CHIP):
                    kx, ky = _flip(x, y, r)
                    got = it["dst"](2 * kx + ky, 1 - cc)
                    _remote(got, got, send.at[a, 3 + r], recv.at[a, 3 + r], sib).wait_recv()
        for cp in own_sends(items, send, recv) + forwards:
            cp.wait_send()
        for a, it in enumerate(items):
            if it["local"] is not None:
                src, stage, dst = it["local"]
                pltpu.make_async_copy(stage, dst, lsem.at[a]).wait()

    return start, finish


def _gather_sems(n):
    return [pltpu.SemaphoreType.DMA((n, 7)), pltpu.SemaphoreType.DMA((n, 7)), pltpu.SemaphoreType.DMA((n,))]


def _weights_gather(bufs, split):
    n = len(bufs)

    def ctx(outs):
        x, y, cc = _place()
        chips = dict(me=2 * x + y, y=2 * x + (1 - y), x=2 * (1 - x) + y, d=2 * (1 - x) + (1 - y))
        devs = dict(y=(x, 1 - y, cc), x=(1 - x, y, cc), d=(1 - x, 1 - y, cc), s=(x, y, 1 - cc))

        def piece(a, kj, pc, q=None):
            if not split[a]:
                return outs[a].at[kj]
            h = bufs[a].shape[1] // 2
            if q is None:
                return outs[a].at[kj, pl.ds(pc * h, h), :]
            return outs[a].at[kj, pl.ds(pc * h + q * (h // 2), h // 2), :]

        return cc, chips, devs, piece

    def directs(a, outs, send, recv):
        cc, chips, devs, piece = ctx(outs)
        if not split[a]:
            whole = piece(a, chips["me"], cc)
            return [_remote(whole, whole, send.at[a, k], recv.at[a, k], devs[t]) for k, t in ((0, "y"), (2, "x"), (4, "d"))]
        q = lambda i: piece(a, chips["me"], cc, i)
        return [_remote(q(0), q(0), send.at[a, 0], recv.at[a, 0], devs["y"]),
                _remote(q(1), q(1), send.at[a, 3], recv.at[a, 3], devs["x"]),
                _remote(q(1), q(1), send.at[a, 1], recv.at[a, 1], devs["y"]),
                _remote(q(0), q(0), send.at[a, 2], recv.at[a, 2], devs["x"])]

    def landed(a, k, outs, send, recv):
        cc, chips, devs, piece = ctx(outs)
        if not split[a]:
            got = piece(a, chips[{0: "y", 2: "x", 4: "d"}[k]], cc)
        elif k < 6:
            got = piece(a, chips[("y", "y", "x", "x", "d", "d")[k]], cc, (0, 1, 0, 1, 0, 1)[k])
        else:
            got = piece(a, chips[("y", "x", "d")[k - 6]], 1 - cc)
        return _remote(got, got, send.at[a, k], recv.at[a, k], devs["s"])

    def passed_on(a, outs, send, recv):
        cc, chips, devs, piece = ctx(outs)
        from_y, from_x = piece(a, chips["y"], cc, 0), piece(a, chips["x"], cc, 1)
        return [_remote(from_y, from_y, send.at[a, 4], recv.at[a, 4], devs["x"]),
                _remote(from_x, from_x, send.at[a, 5], recv.at[a, 5], devs["y"])]

    def to_sibling(a, outs, send, recv):
        cc, chips, devs, piece = ctx(outs)
        return [_remote(piece(a, chips[t], cc), piece(a, chips[t], cc), send.at[a, 6 + i], recv.at[a, 6 + i], devs["s"])
                for i, t in enumerate(("y", "x", "d"))]

    def start(ins, outs, scr):
        send, recv = scr
        per_item = [directs(a, outs, send, recv) for a in range(n)]
        for rank in range(4):
            for cps in per_item:
                if rank < len(cps):
                    cps[rank].start()

    def mid(ins, outs, scr):
        send, recv = scr
        for a in range(n):
            if split[a]:
                fy, fx = passed_on(a, outs, send, recv)
                landed(a, 0, outs, send, recv).wait_recv()
                fy.start()
                landed(a, 3, outs, send, recv).wait_recv()
                fx.start()

    def finish(ins, outs, scr):
        send, recv = scr
        for a in range(n):
            if split[a]:
                for k in (1, 2, 4, 5):
                    landed(a, k, outs, send, recv).wait_recv()
                for cp in to_sibling(a, outs, send, recv):
                    cp.start()
            else:
                for k in (0, 2, 4):
                    landed(a, k, outs, send, recv).wait_recv()
        for a in range(n):
            if split[a]:
                for k in (6, 7, 8):
                    landed(a, k, outs, send, recv).wait_recv()
            cps = directs(a, outs, send, recv)
            if split[a]:
                cps += passed_on(a, outs, send, recv) + to_sibling(a, outs, send, recv)
            for cp in cps:
                cp.wait_send()

    return _Comm(bufs, [jax.ShapeDtypeStruct(b.shape, b.dtype) for b in bufs], {i: i for i in range(n)},
                 [pltpu.SemaphoreType.DMA((n, 9)), pltpu.SemaphoreType.DMA((n, 9))], start, finish, mid)


def _small_gather(arrs):
    n = len(arrs)

    def make_items(ins, outs, scr):
        x, y, cc = _place()
        items = []
        for a in range(n):
            dst = functools.partial(lambda o, kj, pc: o.at[2 * kj + pc], outs[a])
            items.append(dict(src=ins[a], dst=dst, local=(ins[a], scr[a], outs[a].at[4 * x + 2 * y + cc]),
                              sibling=True, forward=True))
        return items

    start, finish = _gather_phases(make_items)
    return _Comm(arrs, [jax.ShapeDtypeStruct((N_DEV,) + a.shape, a.dtype) for a in arrs], {},
                 [pltpu.VMEM(a.shape, a.dtype) for a in arrs] + _gather_sems(n), start, finish)


HBM = pl.BlockSpec(memory_space=pltpu.HBM)
SEM = pl.BlockSpec(memory_space=pltpu.SEMAPHORE)
DATAFLOW = pltpu.SideEffectType.DATAFLOW_SIDE_EFFECTING


class _SemGrid:
    def __init__(self, refs, cols):
        self.refs, self.cols = refs, cols

    @property
    def at(self):
        return self

    def __getitem__(self, idx):
        return self.refs[idx[0] * self.cols + idx[1]]


def _split_start(name, srcs, lands, sem_shape, copies, zeroed=False):
    n, k = len(srcs), len(lands)
    ns = sem_shape[0] * sem_shape[1]

    def body(*refs):
        src_refs, land_refs = refs[:n], refs[n:n + k]
        send = _SemGrid(refs[n + k:n + k + ns], sem_shape[1])
        recv = _SemGrid(refs[n + k + ns:n + k + 2 * ns], sem_shape[1])
        token = refs[-1]
        for cp in copies(src_refs, land_refs, send, recv):
            cp.start()
        token[...] = jnp.zeros(token.shape, F32)

    hbm = lambda a: pltpu.with_memory_space_constraint(a, pltpu.HBM)
    zones = [jnp.zeros(l.shape, l.dtype) if zeroed else lax.empty(l.shape, l.dtype) for l in lands]
    out = pl.pallas_call(
        body, name=name,
        out_shape=[pltpu.SemaphoreType.DMA(())] * (2 * ns)
        + [pltpu.HBM(a.shape, a.dtype) for a in list(srcs) + list(lands)] + [jax.ShapeDtypeStruct((8, 128), F32)],
        in_specs=[HBM] * (n + k), out_specs=[SEM] * (2 * ns) + [HBM] * (n + k) + [VMEM],
        input_output_aliases={i: 2 * ns + i for i in range(n + k)},
        compiler_params=pltpu.CompilerParams(has_side_effects=DATAFLOW),
    )(*[hbm(a) for a in srcs], *[hbm(z) for z in zones])
    return out[:-1], out[-1]


def _split_wait(name, state, n, sem_shape, copies, after):
    ns = sem_shape[0] * sem_shape[1]
    sems, bufs = state[:2 * ns], state[2 * ns:]
    k = len(bufs) - n

    def body(*refs):
        src_refs, land_refs = refs[:n], refs[n:n + k]
        send = _SemGrid(refs[n + k:n + k + ns], sem_shape[1])
        recv = _SemGrid(refs[n + k + ns:n + k + 2 * ns], sem_shape[1])
        cps = copies(src_refs, land_refs, send, recv)
        for cp in cps:
            cp.wait_send()
        for cp in cps:
            cp.wait_recv()

    out = pl.pallas_call(
        body, name=name,
        out_shape=[pltpu.HBM(a.shape, a.dtype) for a in bufs],
        in_specs=[HBM] * (n + k) + [SEM] * (2 * ns) + [ANY] * len(after), out_specs=[HBM] * (n + k),
        input_output_aliases={i: i for i in range(n + k)},
        compiler_params=pltpu.CompilerParams(has_side_effects=DATAFLOW),
    )(*bufs, *sems, *after)
    return out[n:]


def _direct_phases(copies):
    def start(ins, outs, scr):
        for cp in copies(ins, outs, *scr):
            cp.start()

    def finish(ins, outs, scr):
        cps = copies(ins, outs, *scr)
        for cp in cps:
            cp.wait_recv()
        for cp in cps:
            cp.wait_send()

    return start, finish


def _sibling_halves(gs):
    n = len(gs)

    def copies(ins, outs, send, recv):
        x, y, cc = _place()
        cps = []
        for a in range(n):
            h = gs[a].shape[1] // 2
            cps.append(_remote(ins[a].at[:, pl.ds((1 - cc) * h, h), :], outs[a], send.at[a], recv.at[a],
                               (x, y, 1 - cc)))
        return cps

    start, finish = _direct_phases(copies)
    return _Comm(gs, [jax.ShapeDtypeStruct((N_CHIP, g.shape[1] // 2, g.shape[2]), F32) for g in gs], {},
                 [pltpu.SemaphoreType.DMA((n,)), pltpu.SemaphoreType.DMA((n,))], start, finish)


def _exchange_parts(pbs):
    n = len(pbs)

    def copies(ins, outs, send, recv):
        x, y, cc = _place()
        cps = []
        for a in range(n):
            for r in range(1, N_CHIP):
                kx, ky = _flip(x, y, r)
                cps.append(_remote(ins[a].at[2 * kx + ky], outs[a].at[r - 1], send.at[a, r - 1], recv.at[a, r - 1],
                                   (kx, ky, cc)))
        return cps

    lands = [jax.ShapeDtypeStruct((N_CHIP - 1,) + p.shape[1:], p.dtype) for p in pbs]
    return lands, (n, N_CHIP - 1), copies


def _small_parts(arrs):
    n = len(arrs)

    def copies(ins, outs, send, recv):
        x, y, cc = _place()
        b = 4 * x + 2 * y + cc
        cps = []
        for a in range(n):
            for r in range(1, N_DEV):
                dev = ((1 - x) if r & 4 else x, (1 - y) if r & 2 else y, (1 - cc) if r & 1 else cc)
                cps.append(_remote(ins[a], outs[a].at[b], send.at[a, r - 1], recv.at[a, r - 1], dev))
        return cps

    lands = [jax.ShapeDtypeStruct((N_DEV,) + a.shape, a.dtype) for a in arrs]
    return lands, (n, N_DEV - 1), copies


def _exchange_partials(pbs):
    lands, sem_shape, copies = _exchange_parts(pbs)
    start, finish = _direct_phases(copies)
    return _Comm(pbs, lands, {}, [pltpu.SemaphoreType.DMA(sem_shape), pltpu.SemaphoreType.DMA(sem_shape)],
                 start, finish)


def _join_halves(fulls):
    n = len(fulls)

    def copies(ins, outs, send, recv):
        x, y, cc = _place()
        cps = []
        for a in range(n):
            h = fulls[a].shape[0] // 2
            mine = outs[a].at[pl.ds(cc * h, h), :]
            cps.append(_remote(mine, mine, send.at[a], recv.at[a], (x, y, 1 - cc)))
        return cps

    start, finish = _direct_phases(copies)
    return _Comm(fulls, [jax.ShapeDtypeStruct(f.shape, F32) for f in fulls], {i: i for i in range(n)},
                 [pltpu.SemaphoreType.DMA((n,)), pltpu.SemaphoreType.DMA((n,))], start, finish)


def _cast_weights(place, shards, dww, wg):
    n = len(shards)

    def body(pref, *refs):
        ins, outs = refs[:n + 2], refs[n + 2:]
        for a in range(n):
            outs[a][0] = ins[a][...].astype(MXU_DTYPE)
        outs[n][0] = ins[n][...]
        outs[n + 1][...] = ins[n + 1][...].astype(MXU_DTYPE)

    full = lambda a: pl.BlockSpec(a.shape, lambda i, pref: (0,) * a.ndim)
    slot = lambda a: pl.BlockSpec((1,) + a.shape, lambda i, pref: (pref[1],) + (0,) * a.ndim)
    arrs = list(shards) + [dww, wg]
    return pl.pallas_call(
        body, name="cast_weights",
        grid_spec=pltpu.PrefetchScalarGridSpec(
            num_scalar_prefetch=1, grid=(1,), in_specs=[full(a) for a in arrs],
            out_specs=[slot(a) for a in arrs[:n + 1]] + [full(wg)]),
        out_shape=[jax.ShapeDtypeStruct((N_CHIP,) + a.shape, MXU_DTYPE) for a in shards]
        + [jax.ShapeDtypeStruct((N_CHIP,) + dww.shape, F32), jax.ShapeDtypeStruct(wg.shape, MXU_DTYPE)],
        compiler_params=pltpu.CompilerParams(dimension_semantics=("arbitrary",)),
    )(place, *arrs)


def _mixer_fwd(x, mod, g1, w_in, dww, dwb, lng, lnb, w_pw, wg, pscale, w_out, comm=None):
    s = x.shape[0]
    ts = _token_tile(s)
    nt = s // ts

    def body(x_ref, mod_ref, g1_ref, win_ref, dww_ref, dwb_ref, lng_ref, lnb_ref, wpw_ref, wg_ref, ps_ref,
             wout_ref, x2_ref, y_ref, u_ref, z_ref, rstd_ref, p_ref, ycat_ref, gpad, vpad):
        i = pl.program_id(0)

        @pl.when(i == 0)
        def _():
            gpad[0:CONV_HALO, :] = jnp.zeros((CONV_HALO, CONV_W), F32)
            vpad[0:POOL_HALO, :] = jnp.zeros((POOL_HALO, POOL_W), F32)

        xt = x_ref[...]
        sh1 = mod_ref[0:1, :]
        sc1 = mod_ref[1:2, :]
        gt1 = mod_ref[2:3, :]
        r1 = lax.rsqrt(jnp.mean(xt * xt, axis=-1, keepdims=True) + EPS)
        h1 = (xt * r1 * g1_ref[...]) * (1.0 + sc1) + sh1
        h1b = h1.astype(MXU_DTYPE)
        u = jnp.concatenate([_dot(h1b, win_ref[j]) for j in range(N_CHIP)], axis=1)
        u_ref[...] = u
        a = u[:, :CONV_W]
        g = u[:, CONV_W:2 * CONV_W]
        v = u[:, 2 * CONV_W:]

        gpad[CONV_HALO:CONV_HALO + ts, :] = a * _sigmoid(g)
        cv = jnp.broadcast_to(dwb_ref[...], (ts, CONV_W))
        off = CONV_HALO - (CONV_K - 1)
        for k in range(CONV_K):
            cv = cv + dww_ref[k:k + 1, :] * gpad[off + k:off + k + ts, :]
        gpad[0:CONV_HALO, :] = gpad[ts:ts + CONV_HALO, :]

        mu = jnp.mean(cv, axis=-1, keepdims=True)
        cc = cv - mu
        rstd = lax.rsqrt(jnp.mean(cc * cc, axis=-1, keepdims=True) + EPS)
        z = cc * rstd
        z_ref[...] = z
        rstd_ref[...] = rstd
        ln = z * lng_ref[...] + lnb_ref[...]
        sw = ln * _sigmoid(ln)
        yconv = _dot(sw, wpw_ref[...])

        vpad[POOL_HALO:POOL_HALO + ts, :] = v
        t = i * ts + lax.broadcasted_iota(jnp.int32, (ts, 1), 0)
        ps, ypool = [], []
        for gi, w in enumerate(POOL_WINDOWS):
            cols = slice(gi * POOL_G, (gi + 1) * POOL_G)
            acc = vpad[POOL_HALO:POOL_HALO + ts, cols]
            for d in range(1, w):
                acc = acc + vpad[POOL_HALO - d:POOL_HALO - d + ts, cols]
            cnt = jnp.minimum(t + 1, w).astype(F32)
            pg = (acc / cnt - v[:, cols]).astype(MXU_DTYPE)
            ps.append(pg)
            ypool.append(_dot(pg, wg_ref[gi]))
        vpad[0:POOL_HALO, :] = vpad[ts:ts + POOL_HALO, :]
        p_ref[...] = jnp.concatenate(ps, axis=1)
        ypool = jnp.concatenate(ypool, axis=1) * ps_ref[...]

        ycat = jnp.concatenate([yconv, ypool], axis=1).astype(MXU_DTYPE)
        ycat_ref[...] = ycat
        y = _dot(ycat, wout_ref[...])
        y_ref[...] = y
        x2_ref[...] = xt + gt1 * y

    tile = lambda w: pl.BlockSpec((ts, w), lambda i: (i, 0))
    return _call(
        body, name="mixer_fwd", grid=(nt,),
        in_specs=[tile(D_MODEL), _full(mod.shape), _full(g1.shape), _full(w_in.shape), _full(dww.shape),
                  _full(dwb.shape), _full(lng.shape), _full(lnb.shape), _full(w_pw.shape), _full(wg.shape),
                  _full(pscale.shape), _full(w_out.shape)],
        out_specs=[tile(D_MODEL), tile(D_MODEL), tile(IN_W), tile(CONV_W), tile(1), tile(POOL_W), tile(D_MODEL)],
        out_shape=[jax.ShapeDtypeStruct((s, D_MODEL), F32), jax.ShapeDtypeStruct((s, D_MODEL), F32),
                   jax.ShapeDtypeStruct((s, IN_W), F32), jax.ShapeDtypeStruct((s, CONV_W), F32),
                   jax.ShapeDtypeStruct((s, 1), F32), jax.ShapeDtypeStruct((s, POOL_W), MXU_DTYPE),
                   jax.ShapeDtypeStruct((s, D_MODEL), MXU_DTYPE)],
        scratch_shapes=[pltpu.VMEM((ts + CONV_HALO, CONV_W), F32), pltpu.VMEM((ts + POOL_HALO, POOL_W), F32)],
        args=(x, mod, g1, w_in, dww, dwb, lng, lnb, w_pw, wg, pscale, w_out), comm=comm)


def _ffn(x2, tgt, mod, g2, gf, w_gate, w_up, w_down):
    s = x2.shape[0]
    ts = _token_tile(s)
    nt = s // ts
    fb = w_gate.shape[1]

    def body(x2_ref, tgt_ref, mod_ref, g2_ref, gf_ref, wgt_ref, wup_ref, wdn_ref,
             dx2_ref, h2_ref, df_ref, act_ref, dgg_ref, duu_ref, vec_ref, gg_s, uu_s):
        i = pl.program_id(0)

        @pl.when(i == 0)
        def _():
            vec_ref[...] = jnp.zeros(vec_ref.shape, F32)

        x2t = x2_ref[...]
        sh2 = mod_ref[3:4, :]
        sc2 = mod_ref[4:5, :]
        gt2 = mod_ref[5:6, :]
        g2v = g2_ref[...]
        gfv = gf_ref[...]
        r2 = lax.rsqrt(jnp.mean(x2t * x2t, axis=-1, keepdims=True) + EPS)
        xh2 = x2t * r2
        n2 = xh2 * g2v
        h2b = (n2 * (1.0 + sc2) + sh2).astype(MXU_DTYPE)
        h2_ref[...] = h2b
        f = jnp.zeros((ts, D_MODEL), F32)
        for j in range(N_CHIP):
            gg = _dot_nt(h2b, wgt_ref[j])
            uu = _dot_nt(h2b, wup_ref[j])
            gg_s[j] = gg
            uu_s[j] = uu
            actb = (gg * _sigmoid(gg) * uu).astype(MXU_DTYPE)
            act_ref[j] = actb
            f = f + _dot(actb, wdn_ref[j])
        x3 = x2t + gt2 * f
        r3 = lax.rsqrt(jnp.mean(x3 * x3, axis=-1, keepdims=True) + EPS)
        xh3 = x3 * r3
        diff = xh3 * gfv - tgt_ref[...]
        dout = diff * (1.0 / D_MODEL)
        dn3 = dout * gfv
        dx3 = r3 * (dn3 - xh3 * jnp.mean(dn3 * xh3, axis=-1, keepdims=True))
        dfb = (dx3 * gt2).astype(MXU_DTYPE)
        df_ref[...] = dfb
        dh2 = jnp.zeros((ts, D_MODEL), F32)
        for j in range(N_CHIP):
            dact = _dot_nt(dfb, wdn_ref[j])
            gg = gg_s[j]
            uu = uu_s[j]
            sg = _sigmoid(gg)
            duu = (dact * (gg * sg)).astype(MXU_DTYPE)
            dgg = (dact * uu * (sg * (1.0 + gg * (1.0 - sg)))).astype(MXU_DTYPE)
            duu_ref[j] = duu
            dgg_ref[j] = dgg
            dh2 = dh2 + _dot(dgg, wgt_ref[j]) + _dot(duu, wup_ref[j])
        dn2 = dh2 * (1.0 + sc2)
        dxh2 = dn2 * g2v
        dx2_ref[...] = dx3 + r2 * (dxh2 - xh2 * jnp.mean(dxh2 * xh2, axis=-1, keepdims=True))

        col = lambda a: jnp.sum(a, axis=0, keepdims=True)
        vec_ref[0:1, :] += col(dout * xh3)
        vec_ref[1:2, :] += col(dx3 * f)
        vec_ref[2:3, :] += col(dh2)
        vec_ref[3:4, :] += col(dh2 * n2)
        vec_ref[4:5, :] += col(dn2 * xh2)
        vec_ref[5:6, :] += col(diff * diff)

    tile = lambda w: pl.BlockSpec((ts, w), lambda i: (i, 0))
    tile3 = pl.BlockSpec((N_CHIP, ts, fb), lambda i: (0, i, 0))
    once = lambda a: pl.BlockSpec(a.shape, lambda i: (0,) * a.ndim, pipeline_mode=pl.Buffered(1))
    hid = jax.ShapeDtypeStruct((N_CHIP, s, fb), MXU_DTYPE)
    return pl.pallas_call(
        body, name="ffn", grid=(nt,),
        in_specs=[tile(D_MODEL), tile(D_MODEL), _full(mod.shape), _full(g2.shape), _full(gf.shape),
                  once(w_gate), once(w_up), once(w_down)],
        out_specs=[tile(D_MODEL), tile(D_MODEL), tile(D_MODEL), tile3, tile3, tile3, _full((8, D_MODEL))],
        out_shape=[jax.ShapeDtypeStruct((s, D_MODEL), F32), jax.ShapeDtypeStruct((s, D_MODEL), MXU_DTYPE),
                   jax.ShapeDtypeStruct((s, D_MODEL), MXU_DTYPE), hid, hid, hid,
                   jax.ShapeDtypeStruct((8, D_MODEL), F32)],
        scratch_shapes=[pltpu.VMEM((N_CHIP, ts, fb), F32), pltpu.VMEM((N_CHIP, ts, fb), F32)],
        compiler_params=pltpu.CompilerParams(dimension_semantics=("arbitrary",)),
    )(x2, tgt, mod, g2, gf, w_gate, w_up, w_down)


def _mixer_bwd(dx2, x, y, u, z, rstd, p, mod, g1, w_in, dww, lng, lnb, w_pw, wg, pscale, w_out, comm=None, after=()):
    s = x.shape[0]
    ts = _token_tile(s)
    nt = s // ts

    def body(dx2_ref, x_ref, y_ref, u_ref, z_ref, rstd_ref, p_ref, mod_ref, g1_ref, win_ref, dww_ref, lng_ref,
             lnb_ref, wpw_ref, wg_ref, ps_ref, wout_ref,
             gx_ref, h1_ref, du_ref, dy_ref, sw_ref, dyc_ref, dyp_ref, vd_ref, vc_ref, ddw_ref, dcpad, dppad):
        i = pl.program_id(0)
        tix = nt - 1 - i

        @pl.when(i == 0)
        def _():
            vd_ref[...] = jnp.zeros(vd_ref.shape, F32)
            vc_ref[...] = jnp.zeros(vc_ref.shape, F32)
            ddw_ref[...] = jnp.zeros(ddw_ref.shape, F32)
            dcpad[ts:ts + CONV_HALO, :] = jnp.zeros((CONV_HALO, CONV_W), F32)
            dppad[ts:ts + POOL_HALO, :] = jnp.zeros((POOL_HALO, POOL_W), F32)

        col = lambda a: jnp.sum(a, axis=0, keepdims=True)
        sh1 = mod_ref[0:1, :]
        sc1 = mod_ref[1:2, :]
        gt1 = mod_ref[2:3, :]
        dx2t = dx2_ref[...]
        vd_ref[0:1, :] += col(dx2t * y_ref[...])
        dyb = (dx2t * gt1).astype(MXU_DTYPE)
        dy_ref[...] = dyb
        dycat = _dot_nt(dyb, wout_ref[...])
        dyconv = dycat[:, :CONV_W]
        dypool = dycat[:, CONV_W:]

        pt = p_ref[...]
        t = tix * ts + lax.broadcasted_iota(jnp.int32, (ts, 1), 0)
        psc = ps_ref[...]
        dypb = (dypool * psc).astype(MXU_DTYPE)
        dyp_ref[...] = dypb
        dps, ypre = [], []
        for gi, w in enumerate(POOL_WINDOWS):
            cols = slice(gi * POOL_G, (gi + 1) * POOL_G)
            ypre.append(_dot(pt[:, cols], wg_ref[gi]))
            dpg = _dot_nt(dypb[:, cols], wg_ref[gi])
            dps.append(dpg)
            cnt = jnp.minimum(t + 1, w).astype(F32)
            dppad[0:ts, cols] = dpg / cnt
        vc_ref[0:1, :] += col(dypool * jnp.concatenate(ypre, axis=1))
        dvs = []
        for gi, w in enumerate(POOL_WINDOWS):
            cols = slice(gi * POOL_G, (gi + 1) * POOL_G)
            acc = dppad[0:ts, cols]
            for d in range(1, w):
                acc = acc + dppad[d:d + ts, cols]
            dvs.append(acc - dps[gi])
        dv = jnp.concatenate(dvs, axis=1)
        dppad[ts:ts + POOL_HALO, :] = dppad[0:POOL_HALO, :]

        zt = z_ref[...]
        lngv = lng_ref[...]
        ln = zt * lngv + lnb_ref[...]
        sg = _sigmoid(ln)
        swb = (ln * sg).astype(MXU_DTYPE)
        sw_ref[...] = swb
        dycb = dyconv.astype(MXU_DTYPE)
        dyc_ref[...] = dycb
        dln = _dot_nt(dycb, wpw_ref[...]) * (sg * (1.0 + ln * (1.0 - sg)))
        vc_ref[1:2, :] += col(dln * zt)
        vc_ref[2:3, :] += col(dln)
        dz = dln * lngv
        dcv = rstd_ref[...] * (dz - jnp.mean(dz, axis=-1, keepdims=True)
                               - zt * jnp.mean(dz * zt, axis=-1, keepdims=True))
        vc_ref[3:4, :] += col(dcv)
        dcpad[0:ts, :] = dcv
        ut = u_ref[...]
        a = ut[:, :CONV_W]
        g = ut[:, CONV_W:2 * CONV_W]
        sgg = _sigmoid(g)
        glu = a * sgg
        dglu = jnp.zeros((ts, CONV_W), F32)
        for k in range(CONV_K):
            sh = dcpad[CONV_K - 1 - k:CONV_K - 1 - k + ts, :]
            dglu = dglu + dww_ref[k:k + 1, :] * sh
            ddw_ref[k:k + 1, :] += col(glu * sh)
        dcpad[ts:ts + CONV_HALO, :] = dcpad[0:CONV_HALO, :]
        da = dglu * sgg
        dg = dglu * a * sgg * (1.0 - sgg)
        dub = jnp.concatenate([da, dg, dv], axis=1).astype(MXU_DTYPE)
        du_ref[...] = dub
        cw = IN_W // N_CHIP
        dh1 = jnp.zeros((ts, D_MODEL), F32)
        for j in range(N_CHIP):
            dh1 = dh1 + _dot_nt(dub[:, j * cw:(j + 1) * cw], win_ref[j])

        xt = x_ref[...]
        g1v = g1_ref[...]
        r1 = lax.rsqrt(jnp.mean(xt * xt, axis=-1, keepdims=True) + EPS)
        xh1 = xt * r1
        n1 = xh1 * g1v
        h1_ref[...] = (n1 * (1.0 + sc1) + sh1).astype(MXU_DTYPE)
        vd_ref[1:2, :] += col(dh1)
        vd_ref[2:3, :] += col(dh1 * n1)
        dn1 = dh1 * (1.0 + sc1)
        vd_ref[3:4, :] += col(dn1 * xh1)
        dxh = dn1 * g1v
        gx_ref[...] = dx2t + r1 * (dxh - xh1 * jnp.mean(dxh * xh1, axis=-1, keepdims=True))

    tile = lambda w: pl.BlockSpec((ts, w), lambda i: (nt - 1 - i, 0))
    bf = lambda w: jax.ShapeDtypeStruct((s, w), MXU_DTYPE)
    return _call(
        body, name="mixer_bwd", grid=(nt,),
        in_specs=[tile(D_MODEL), tile(D_MODEL), tile(D_MODEL), tile(IN_W), tile(CONV_W), tile(1), tile(POOL_W),
                  _full(mod.shape), _full(g1.shape), _full(w_in.shape), _full(dww.shape), _full(lng.shape),
                  _full(lnb.shape), _full(w_pw.shape), _full(wg.shape), _full(pscale.shape), _full(w_out.shape)],
        out_specs=[tile(D_MODEL), tile(D_MODEL), tile(IN_W), tile(D_MODEL), tile(CONV_W), tile(CONV_W),
                   tile(POOL_W), _full((8, D_MODEL)), _full((8, CONV_W)), _full((32, CONV_W))],
        out_shape=[jax.ShapeDtypeStruct((s, D_MODEL), F32), bf(D_MODEL), bf(IN_W), bf(D_MODEL), bf(CONV_W),
                   bf(CONV_W), bf(POOL_W), jax.ShapeDtypeStruct((8, D_MODEL), F32),
                   jax.ShapeDtypeStruct((8, CONV_W), F32), jax.ShapeDtypeStruct((32, CONV_W), F32)],
        scratch_shapes=[pltpu.VMEM((ts + CONV_HALO, CONV_W), F32), pltpu.VMEM((ts + POOL_HALO, POOL_W), F32)],
        args=(dx2, x, y, u, z, rstd, p, mod, g1, w_in, dww, lng, lnb, w_pw, wg, pscale, w_out), comm=comm,
        after=after)


def _dw(name, a, a_spec, b, b_spec, nb, mb, nbk, comm=None, after=()):
    def body(a_ref, b_ref, o_ref):
        av = a_ref[...]
        bv = b_ref[...]
        av = av.reshape(av.shape[-2:])
        bv = bv.reshape(bv.shape[-2:])
        o_ref[0] = _dot_tn(av, bv)

    (out,), rest = _call(
        body, name=name, grid=(nb,), in_specs=[a_spec, b_spec],
        out_specs=[pl.BlockSpec((1, mb, nbk), lambda j: (j, 0, 0))],
        out_shape=[jax.ShapeDtypeStruct((nb, mb, nbk), F32)], args=(a, b), comm=comm, after=after)
    return out, rest


def _ada_fwd(c, w_ada, b4, comm=None):
    nc = w_ada.shape[1]

    def body(start_comm, c_ref, w_ref, b4_ref, mod_ref, cact_ref, call, part, parts, send1, recv1, send2, recv2):
        x, y, cc = _place()
        b = 4 * x + 2 * y + cc
        j = 2 * x + y
        call[b] = c_ref[...]
        sends = []
        for r in range(1, N_DEV):
            dev = ((1 - x) if r & 4 else x, (1 - y) if r & 2 else y, (1 - cc) if r & 1 else cc)
            cp = _remote(call.at[b], call.at[b], send1.at[r - 1], recv1.at[r - 1], dev)
            cp.start()
            sends.append(cp)
        for r in range(1, N_DEV):
            src_b = lax.bitwise_xor(b, r)
            _remote(call.at[src_b], call.at[src_b], send1.at[r - 1], recv1.at[r - 1], (x, y, cc)).wait_recv()
        for cp in sends:
            cp.wait_send()
        start_comm()
        for i in range(N_DEV):
            ci = call[i]
            cact_ref[i:i + 1, :] = ci * _sigmoid(ci)
        part[...] = jnp.dot(cact_ref[...], w_ref[...], preferred_element_type=F32, precision=lax.Precision.HIGHEST)
        sends = []
        for r in range(1, N_CHIP):
            kx, ky = _flip(x, y, r)
            cp = _remote(part, parts.at[j], send2.at[r - 1], recv2.at[r - 1], (kx, ky, cc))
            cp.start()
            sends.append(cp)
        parts[j] = part[...]
        for r in range(1, N_CHIP):
            kx, ky = _flip(x, y, r)
            kj = 2 * kx + ky
            _remote(part, parts.at[kj], send2.at[r - 1], recv2.at[r - 1], (x, y, cc)).wait_recv()
        for cp in sends:
            cp.wait_send()
        mine = lax.broadcasted_iota(jnp.int32, (N_DEV, 1), 0) == b
        for k in range(N_CHIP):
            row = jnp.sum(jnp.where(mine, parts[k], 0.0), axis=0, keepdims=True)
            mod_ref[k:k + 1, :] = row + b4_ref[k:k + 1, :]

    return _call(
        body, name="ada_fwd", grid=(1,),
        in_specs=[VMEM, VMEM, VMEM], out_specs=[VMEM, VMEM],
        out_shape=[jax.ShapeDtypeStruct((N_CHIP, nc), F32), jax.ShapeDtypeStruct((N_DEV, D_MODEL), F32)],
        scratch_shapes=[pltpu.VMEM((N_DEV, 1, D_MODEL), F32), pltpu.VMEM((N_DEV, nc), F32),
                        pltpu.VMEM((N_CHIP, N_DEV, nc), F32),
                        pltpu.SemaphoreType.DMA((N_DEV - 1,)), pltpu.SemaphoreType.DMA((N_DEV - 1,)),
                        pltpu.SemaphoreType.DMA((N_CHIP - 1,)), pltpu.SemaphoreType.DMA((N_CHIP - 1,))],
        args=(c, w_ada, b4), comm=comm, body_starts=True)


def _chip_partials(name, place, gs, rs, comm=None):
    n = len(gs)

    def body(pref, *refs):
        g_refs, r_refs = refs[:n], refs[n:2 * n]
        pb_refs, own_refs = refs[2 * n:3 * n], refs[3 * n:]
        jj = pl.program_id(0)
        for a in range(n):
            sm = g_refs[a][0] + r_refs[a][0]
            pb_refs[a][0] = sm.astype(MXU_DTYPE)

            @pl.when(jj == pref[1])
            def _(a=a, sm=sm):
                own_refs[a][...] = sm

    halves = [(g.shape[1] // 2, g.shape[2]) for g in gs]
    in_specs = [pl.BlockSpec((1, h, w), lambda jj, pref: (jj, pref[0], 0)) for h, w in halves]
    in_specs += [pl.BlockSpec((1, h, w), lambda jj, pref: (jj, 0, 0)) for h, w in halves]
    out_specs = [pl.BlockSpec((1, h, w), lambda jj, pref: (jj, 0, 0)) for h, w in halves]
    out_specs += [pl.BlockSpec((h, w), lambda jj, pref: (0, 0)) for h, w in halves]
    out, rest = _call(
        body, name=name, grid=(N_CHIP,), in_specs=in_specs, out_specs=out_specs,
        out_shape=[jax.ShapeDtypeStruct((N_CHIP, h, w), MXU_DTYPE) for h, w in halves]
        + [jax.ShapeDtypeStruct((h, w), F32) for h, w in halves],
        args=(*gs, *rs), prefetch=(place,), comm=comm)
    return (out[:n], out[n:]), rest


def _sum_partials(name, place, owns, recvd, comm=None):
    n = len(owns)

    def body(pref, *refs):
        o_refs, r_refs, out_refs = refs[:n], refs[n:2 * n], refs[2 * n:]
        for a in range(n):
            acc = o_refs[a][...]
            for r in range(N_CHIP - 1):
                acc = acc + r_refs[a][r].astype(F32)
            out_refs[a][...] = acc

    full = lambda a: pl.BlockSpec(a.shape, lambda i, pref: (0,) * a.ndim)
    return _call(
        body, name=name, grid=(1,), in_specs=[full(a) for a in list(owns) + list(recvd)],
        out_specs=[pl.BlockSpec(o.shape, lambda i, pref: (pref[0], 0)) for o in owns],
        out_shape=[jax.ShapeDtypeStruct((2 * o.shape[0], o.shape[1]), F32) for o in owns],
        args=(*owns, *recvd), prefetch=(place,), comm=comm)


def _adamw_math(w, g, m, v):
    m = ADAM_B1 * m + (1.0 - ADAM_B1) * g
    v = ADAM_B2 * v + (1.0 - ADAM_B2) * (g * g)
    m_hat = m / (1.0 - ADAM_B1 ** ADAM_STEP)
    v_hat = v / (1.0 - ADAM_B2 ** ADAM_STEP)
    delta = -ADAM_LR * (m_hat / (jnp.sqrt(v_hat) + ADAM_EPS) + ADAM_WD * w)
    return delta, m, v


def _row_tile(rows):
    for t in (512, 352, 256, 128):
        if rows % t == 0:
            return t
    return rows


def _adamw(name, wgmv, steps, after=()):
    n = len(wgmv)

    def body(*refs):
        ins, outs = refs[:4 * n], refs[4 * n:]
        for i in range(n):
            w_ref, g_ref, m_ref, v_ref = ins[4 * i:4 * i + 4]
            d_ref, nm_ref, nv_ref = outs[3 * i:3 * i + 3]
            d_ref[...], nm_ref[...], nv_ref[...] = _adamw_math(w_ref[...], g_ref[...], m_ref[...], v_ref[...])

    in_specs, out_specs, out_shape, args = [], [], [], []
    for w, g, m, v in wgmv:
        rows, cols = w.shape
        spec = pl.BlockSpec((rows // steps, cols), lambda i: (i, 0))
        in_specs += [spec] * 4
        out_specs += [spec] * 3
        out_shape += [jax.ShapeDtypeStruct(w.shape, F32)] * 3
        args += [w, g, m, v]
    res, _ = _call(body, name=name, grid=(steps,), in_specs=in_specs, out_specs=out_specs, out_shape=out_shape,
                   args=args, after=after)
    return [res[3 * i:3 * i + 3] for i in range(n)]


def _adamw_ada(place, cact, dmod, w, m, v, after=()):
    rows, cols = w.shape
    tr = _row_tile(rows)

    def body(pref, ca_ref, dm_ref, w_ref, m_ref, v_ref, g_ref, d_ref, nm_ref, nv_ref):
        g = lax.dot_general(ca_ref[...], dm_ref[...], (((0,), (0,)), ((), ())), preferred_element_type=F32,
                            precision=lax.Precision.HIGHEST)
        g_ref[...] = g
        d_ref[...], nm_ref[...], nv_ref[...] = _adamw_math(w_ref[...], g, m_ref[...], v_ref[...])

    spec = pl.BlockSpec((tr, cols), lambda i, pref: (i, 0))
    return _call(
        body, name="adamw_ada", grid=(rows // tr,),
        in_specs=[pl.BlockSpec((N_DEV, tr), lambda i, pref: (0, i)),
                  pl.BlockSpec((N_DEV, cols), lambda i, pref: (0, pref[1])), spec, spec, spec],
        out_specs=[spec] * 4, out_shape=[jax.ShapeDtypeStruct(w.shape, F32)] * 4,
        args=(cact, dmod, w, m, v), prefetch=(place,), after=after)[0]


def _adamw_small(place, owns, gathered, wmv):
    nw = len(wmv)
    flat = [a for t in wmv for a in t]

    def body(pref, *refs):
        own_refs, all_refs, refs = refs[:5], refs[5:10], refs[10:]
        w_refs = refs[:3 * nw]
        loss_ref, dmod_ref = refs[3 * nw], refs[3 * nw + 1]
        o_refs = refs[3 * nw + 2:]
        j = pref[1]
        me = 2 * pref[1] + pref[0]

        def total(i):
            acc = None
            for b in range(N_DEV):
                blk = jnp.where(me == b, own_refs[i][...], all_refs[i][b])
                acc = blk if acc is None else acc + blk
            return acc

        vf, vd, vc, ddw, gwg = [total(i) for i in range(5)]
        loss_ref[...] = (0.5 / D_MODEL) * jnp.sum(vf[5:6, :], axis=1, keepdims=True)
        order = ((1, 1), (1, 2), (1, 0), (0, 2), (0, 3), (0, 1))
        for b in range(N_DEV):
            for q, (i, row) in enumerate(order):
                dmod_ref[b:b + 1, q * D_MODEL:(q + 1) * D_MODEL] = jnp.where(
                    me == b, own_refs[i][row:row + 1, :], all_refs[i][b, row:row + 1, :])
        dm = dmod_ref[...]
        g_bada = dm[0:1, :]
        for b in range(1, N_DEV):
            g_bada = g_bada + dm[b:b + 1, :]
        g_dww = jnp.zeros((32, POOL_G), F32)
        for k in range(N_CHIP):
            g_dww = g_dww + jnp.where(j == k, ddw[:, k * POOL_G:(k + 1) * POOL_G], 0.0)
        grads = [g_bada, vd[3:4, :], g_dww, vc[3:4, :], vc[1:2, :], vc[2:3, :], gwg, vc[0:1, :], vf[4:5, :],
                 vf[0:1, :]]
        for i, g in enumerate(grads):
            w_ref, m_ref, v_ref = w_refs[3 * i:3 * i + 3]
            d, nm, nv = _adamw_math(w_ref[...], g, m_ref[...], v_ref[...])
            o_refs[4 * i][...] = g
            o_refs[4 * i + 1][...] = d
            o_refs[4 * i + 2][...] = nm
            o_refs[4 * i + 3][...] = nv

    outs = [jax.ShapeDtypeStruct((1, 1), F32), jax.ShapeDtypeStruct((N_DEV, 6 * D_MODEL), F32)]
    for w, _, _ in wmv:
        outs += [jax.ShapeDtypeStruct(w.shape, F32)] * 4
    full = lambda a: pl.BlockSpec(a.shape, lambda i, pref: (0,) * a.ndim)
    args = list(owns) + list(gathered) + flat
    res, _ = _call(body, name="adamw_small", grid=(1,), in_specs=[full(a) for a in args],
                   out_specs=[full(o) for o in outs], out_shape=outs, args=args, prefetch=(place,))
    return res[0], res[1], [res[2 + 4 * i:6 + 4 * i] for i in range(nw)]


def kernel(x, c, w_ada, b_ada, g_norm1, w_in, dw_w, dw_b, conv_ln_g, conv_ln_b, w_conv_pw, w_pool_group, pool_scale, w_out, g_norm2, w_ffn_gate, w_ffn_up, w_ffn_down, g_final, loss_target, m_w_ada, m_b_ada, m_g_norm1, m_w_in, m_dw_w, m_dw_b, m_conv_ln_g, m_conv_ln_b, m_w_conv_pw, m_w_pool_group, m_pool_scale, m_w_out, m_g_norm2, m_w_ffn_gate, m_w_ffn_up, m_w_ffn_down, m_g_final, v_w_ada, v_b_ada, v_g_norm1, v_w_in, v_dw_w, v_dw_b, v_conv_ln_g, v_conv_ln_b, v_w_conv_pw, v_w_pool_group, v_pool_scale, v_w_out, v_g_norm2, v_w_ffn_gate, v_w_ffn_up, v_w_ffn_down, v_g_final):
    xi, yi, ci = _place()
    place = jnp.stack([ci, 2 * xi + yi]).astype(jnp.int32)
    n_ada = w_ada.shape[2]

    tr = lambda a: jnp.transpose(a[0])
    big = [w_in[0], w_conv_pw[0], w_out[0], tr(w_ffn_gate), tr(w_ffn_up), w_ffn_down[0]]
    b_in, b_pw, b_out, b_gate, b_up, b_down, b_dww, wg_b = _cast_weights(place, big, dw_w[0], w_pool_group[0])

    (mod4, cact), (win_g, wpw_g, wout_g, dww_g) = _ada_fwd(
        c, w_ada[0], b_ada.reshape(N_CHIP, n_ada),
        comm=_weights_gather([b_in, b_pw, b_out, b_dww], [True, True, True, False]))
    mod = mod4.reshape(6, D_MODEL)
    dww_full = jnp.pad(jnp.concatenate([dww_g[k] for k in range(N_CHIP)], axis=1), ((0, 1), (0, 0)))
    w_pw = wpw_g.reshape(CONV_W, CONV_W)
    w_o = wout_g.reshape(D_MODEL, D_MODEL)
    xs, tgt, gf = x[0], loss_target[0], g_final.reshape(1, D_MODEL)
    s = xs.shape[0]
    fb = b_gate.shape[1]

    (x2, y, u, z, rstd, p, ycat), (wgate_g, wup_g, wdown_g) = _mixer_fwd(
        xs, mod, g_norm1, win_g, dww_full, dw_b, conv_ln_g, conv_ln_b, w_pw, wg_b, pool_scale, w_o,
        comm=_weights_gather([b_gate, b_up, b_down], [True, True, True]))
    dx2, h2, df, act, dgg, duu, vec_f = _ffn(x2, tgt, mod, g_norm2, gf, wgate_g, wup_g, wdown_g)

    whole = lambda w: pl.BlockSpec((s, w), lambda j: (0, 0))
    cols = lambda w: pl.BlockSpec((s, w), lambda j: (0, j))
    hid = pl.BlockSpec((1, s, fb), lambda j: (j, 0, 0))
    c_gate, _ = _dw("dw_gate", dgg, hid, h2, whole(D_MODEL), N_CHIP, fb, D_MODEL)
    c_up, (r_gate,) = _dw("dw_up", duu, hid, h2, whole(D_MODEL), N_CHIP, fb, D_MODEL, comm=_sibling_halves([c_gate]))
    ((pb_gate,), (own_gate,)), _ = _chip_partials("partials_gate", place, [c_gate], [r_gate])
    ex_gate = _exchange_parts([pb_gate])
    st_gate, tok_gate = _split_start("exchange_gate_start", [pb_gate], *ex_gate)
    c_down, (r_up,) = _dw("dw_down", act, hid, df, whole(D_MODEL), N_CHIP, fb, D_MODEL,
                          comm=_sibling_halves([c_up]), after=(tok_gate,))
    ((pb_up,), (own_up,)), (r_down,) = _chip_partials("partials_up", place, [c_up], [r_up],
                                                      comm=_sibling_halves([c_down]))
    ((pb_down,), (own_down,)), _ = _chip_partials("partials_down", place, [c_down], [r_down])
    ex_ud = _exchange_parts([pb_up, pb_down])
    st_ud, tok_ud = _split_start("exchange_up_down_start", [pb_up, pb_down], *ex_ud)
    (gx, h1, du, dy, sw, dyc, dyp, vec_d, vec_c, ddw), _ = _mixer_bwd(
        dx2, xs, y, u, z, rstd, p, mod, g_norm1, win_g, dww_full, conv_ln_g, conv_ln_b, w_pw, wg_b, pool_scale, w_o,
        after=(tok_ud,))

    g_wg, _ = _dw("dw_wg", p, cols(POOL_G), dyp, cols(POOL_G), len(POOL_WINDOWS), POOL_G, POOL_G)
    small_own = [vec_f, vec_d, vec_c, ddw, g_wg]
    ex_small = _small_parts(small_own)
    st_small, tok_small = _split_start("small_grads_start", small_own, *ex_small, zeroed=True)
    c_in, _ = _dw("dw_in", h1, whole(D_MODEL), du, cols(IN_W // N_CHIP), N_CHIP, D_MODEL, IN_W // N_CHIP,
                  after=(tok_small,))
    c_out, (r_in,) = _dw("dw_out", ycat, cols(D_MODEL // N_CHIP), dy, whole(D_MODEL), N_CHIP, D_MODEL // N_CHIP,
                         D_MODEL, comm=_sibling_halves([c_in]))
    c_pw, (r_out,) = _dw("dw_pw", sw, cols(CONV_W // N_CHIP), dyc, whole(CONV_W), N_CHIP, CONV_W // N_CHIP, CONV_W,
                         comm=_sibling_halves([c_out]))
    (rc_gate,) = _split_wait("exchange_gate_wait", st_gate, 1, ex_gate[1], ex_gate[2], after=(c_pw,))
    rc_up, rc_down = _split_wait("exchange_up_down_wait", st_ud, 2, ex_ud[1], ex_ud[2], after=(c_pw, rc_gate))
    small_all = _split_wait("small_grads_wait", st_small, len(small_own), ex_small[1], ex_small[2],
                            after=(c_pw, rc_down))

    ffn_fulls, (r_pw,) = _sum_partials("sum_ffn", place, [own_gate, own_up, own_down], [rc_gate, rc_up, rc_down],
                                       comm=_sibling_halves([c_pw]))
    (pbs_mix, owns_mix), (g_gate, g_up, g_down) = _chip_partials(
        "partials_mix", place, [c_in, c_pw, c_out], [r_in, r_pw, r_out], comm=_join_halves(ffn_fulls))

    pad_rows = lambda a: jnp.pad(a[0], ((0, 1), (0, 0)))
    row = lambda a: a.reshape(1, -1)
    small = [(b_ada, m_b_ada, v_b_ada), (g_norm1, m_g_norm1, v_g_norm1),
             (pad_rows(dw_w), pad_rows(m_dw_w), pad_rows(v_dw_w)), (dw_b, m_dw_b, v_dw_b),
             (conv_ln_g, m_conv_ln_g, v_conv_ln_g), (conv_ln_b, m_conv_ln_b, v_conv_ln_b),
             (w_pool_group[0], m_w_pool_group[0], v_w_pool_group[0]), (pool_scale, m_pool_scale, v_pool_scale),
             (g_norm2, m_g_norm2, v_g_norm2), (row(g_final), row(m_g_final), row(v_g_final))]
    loss, dmod, small_out = _adamw_small(place, small_own, small_all, small)
    (o_bada, o_g1, o_dww, o_dwb, o_lng, o_lnb, o_wg, o_ps, o_g2, o_gf) = small_out
    o_dww = [a[:CONV_K] for a in o_dww]
    o_gf = [a.reshape(D_MODEL) for a in o_gf]
    lead = lambda outs: [a[None] for a in outs]

    lands, sem_shape, copies = _exchange_parts(pbs_mix)
    state, token = _split_start("exchange_mix_start", pbs_mix, lands, sem_shape, copies)
    u_gate, u_up, u_down = _adamw(
        "adamw_ffn", [(tr(w_ffn_gate), g_gate, tr(m_w_ffn_gate), tr(v_w_ffn_gate)),
                      (tr(w_ffn_up), g_up, tr(m_w_ffn_up), tr(v_w_ffn_up)),
                      (w_ffn_down[0], g_down, m_w_ffn_down[0], v_w_ffn_down[0])],
        steps=4, after=(token,))
    o_gate = [jnp.transpose(o) for o in [g_gate] + list(u_gate)]
    o_up = [jnp.transpose(o) for o in [g_up] + list(u_up)]
    o_down = [g_down] + list(u_down)
    o_ada = _adamw_ada(place, cact, dmod, w_ada[0], m_w_ada[0], v_w_ada[0], after=(token,))
    rc_mix = _split_wait("exchange_mix_wait", state, len(pbs_mix), sem_shape, copies, after=(o_ada[1], u_down[0]))
    mix_fulls, _ = _sum_partials("sum_mix", place, owns_mix, rc_mix)
    g_in, g_pw, g_out = _comm_only("join_mix", _join_halves(mix_fulls))
    u_in, u_pw, u_out = _adamw(
        "adamw_mix", [(w_in[0], g_in, m_w_in[0], v_w_in[0]), (w_conv_pw[0], g_pw, m_w_conv_pw[0], v_w_conv_pw[0]),
                      (w_out[0], g_out, m_w_out[0], v_w_out[0])], steps=4)
    o_in, o_pw, o_out = [g_in] + list(u_in), [g_pw] + list(u_pw), [g_out] + list(u_out)

    per_weight = [lead(o_ada), o_bada, o_g1, lead(o_in), lead(o_dww), o_dwb, o_lng, o_lnb, lead(o_pw), lead(o_wg),
                  o_ps, lead(o_out), o_g2, lead(o_gate), lead(o_up), lead(o_down), o_gf]
    result = [loss.reshape(()), gx[None]]
    for kind in range(4):
        result += [o[kind] for o in per_weight]
    return tuple(result)
```

```python
import functools

import jax
import jax.numpy as jnp
from jax import lax
from jax.experimental import pallas as pl
from jax.experimental.pallas import tpu as pltpu

F32 = jnp.float32
MXU_DTYPE = jnp.bfloat16
EPS = 1e-6

D_MODEL = 1024
CONV_W = 512
POOL_W = 512
CONV_K = 31
POOL_WINDOWS = (2, 4, 8, 16)
POOL_G = 128
IN_W = 2 * CONV_W + POOL_W
N_CHIP = 4
N_DEV = 8
CONV_HALO = 32
POOL_HALO = 16

ADAM_LR = 0.001
ADAM_B1 = 0.9
ADAM_B2 = 0.999
ADAM_EPS = 1e-08
ADAM_WD = 0.01
ADAM_STEP = 10

MESH = pl.DeviceIdType.MESH
ANY = pl.BlockSpec(memory_space=pl.ANY)
VMEM = pl.BlockSpec(memory_space=pltpu.VMEM)


def _dot(a, b):
    return jnp.dot(a.astype(MXU_DTYPE), b.astype(MXU_DTYPE), preferred_element_type=F32)


def _dot_nt(a, b):
    return lax.dot_general(a.astype(MXU_DTYPE), b.astype(MXU_DTYPE), (((1,), (1,)), ((), ())),
                           preferred_element_type=F32)


def _dot_tn(a, b):
    return lax.dot_general(a.astype(MXU_DTYPE), b.astype(MXU_DTYPE), (((0,), (0,)), ((), ())),
                           preferred_element_type=F32)


def _sigmoid(v):
    return 1.0 / (1.0 + jnp.exp(-v))


def _full(shape):
    n = len(shape)
    return pl.BlockSpec(shape, lambda *_: (0,) * n)


def _token_tile(s):
    return 256 if s % 256 == 0 else s


SUBLANES = 8


def _row_shifts(pad_ref, shifted_ref, rows):
    for r in range(1, SUBLANES):
        shifted_ref[r - 1] = pad_ref[r:r + rows, :]

    def window(i, n):
        r, base = i % SUBLANES, i - i % SUBLANES
        if r == 0:
            return pad_ref[base:base + n, :]
        return shifted_ref[r - 1, base:base + n, :]

    return window


def _place():
    return lax.axis_index("x"), lax.axis_index("y"), lax.axis_index("c")


def _flip(x, y, r):
    return ((1 - x) if r & 2 else x, (1 - y) if r & 1 else y)


def _remote(src, dst, send_sem, recv_sem, dev):
    return pltpu.make_async_remote_copy(src_ref=src, dst_ref=dst, send_sem=send_sem, recv_sem=recv_sem,
                                        device_id=dev, device_id_type=MESH)


class _Comm:
    def __init__(self, ins, outs, aliases, scratch, start, finish, mid=None):
        self.ins, self.outs, self.aliases, self.scratch = list(ins), list(outs), dict(aliases), list(scratch)
        self.start, self.finish = start, finish
        self.mid = mid


def _both(a, b):
    na, nao, nas = len(a.ins), len(a.outs), len(a.scratch)
    aliases = dict(a.aliases)
    aliases.update({na + i: nao + o for i, o in b.aliases.items()})

    def start(ins, outs, scr):
        a.start(ins[:na], outs[:nao], scr[:nas])
        b.start(ins[na:], outs[nao:], scr[nas:])

    def finish(ins, outs, scr):
        a.finish(ins[:na], outs[:nao], scr[:nas])
        b.finish(ins[na:], outs[nao:], scr[nas:])

    def mid(ins, outs, scr):
        if a.mid:
            a.mid(ins[:na], outs[:nao], scr[:nas])
        if b.mid:
            b.mid(ins[na:], outs[nao:], scr[nas:])

    return _Comm(a.ins + b.ins, a.outs + b.outs, aliases, a.scratch + b.scratch, start, finish,
                 mid if (a.mid or b.mid) else None)


def _call(body, *, name, grid, in_specs, out_specs, out_shape, args, scratch_shapes=(), prefetch=(), comm=None,
          body_starts=False, after=()):
    in_specs = list(in_specs) + [ANY] * len(after)
    args = list(args) + list(after)
    n_pre, n_in, n_out, n_scr = len(prefetch), len(in_specs), len(out_specs), len(scratch_shapes)
    n_body_in = n_in - len(after)
    c_ins = comm.ins if comm else []
    c_outs = comm.outs if comm else []
    c_scr = comm.scratch if comm else []
    last = grid[0] - 1

    def wrapped(*refs):
        pre, refs = refs[:n_pre], refs[n_pre:]
        ins, cin = refs[:n_body_in], refs[n_in:n_in + len(c_ins)]
        refs = refs[n_in + len(c_ins):]
        outs, cout = refs[:n_out], refs[n_out:n_out + len(c_outs)]
        refs = refs[n_out + len(c_outs):]
        scr, cscr = refs[:n_scr], refs[n_scr:]
        step = pl.program_id(0)
        if comm and not body_starts:
            @pl.when(step == 0)
            def _():
                comm.start(cin, cout, cscr)

        has_mid = comm is not None and comm.mid is not None
        mid_step = grid[0] // 2 if grid[0] >= 4 else None
        if has_mid and mid_step is not None:
            @pl.when(step == mid_step)
            def _():
                comm.mid(cin, cout, cscr)

        if body_starts:
            body(lambda: comm.start(cin, cout, cscr) if comm else None, *pre, *ins, *outs, *scr)
        else:
            body(*pre, *ins, *outs, *scr)
        if comm:
            @pl.when(step == last)
            def _():
                if has_mid and mid_step is None:
                    comm.mid(cin, cout, cscr)
                comm.finish(cin, cout, cscr)

    aliases = {n_pre + n_in + a: n_out + b for a, b in (comm.aliases if comm else {}).items()}
    res = pl.pallas_call(
        wrapped, name=name,
        grid_spec=pltpu.PrefetchScalarGridSpec(
            num_scalar_prefetch=n_pre, grid=grid, in_specs=list(in_specs) + [ANY] * len(c_ins),
            out_specs=list(out_specs) + [ANY] * len(c_outs), scratch_shapes=list(scratch_shapes) + list(c_scr)),
        out_shape=list(out_shape) + list(c_outs),
        input_output_aliases=aliases,
        compiler_params=pltpu.CompilerParams(dimension_semantics=("arbitrary",)),
    )(*prefetch, *args, *c_ins)
    return res[:n_out], res[n_out:]


def _comm_only(name, comm):
    return _call(lambda: None, name=name, grid=(1,), in_specs=[], out_specs=[], out_shape=[], args=[], comm=comm)[1]


def _gather_phases(make_items):
    def own_sends(items, send, recv):
        x, y, cc = _place()
        j = 2 * x + y
        cps = []
        for a, it in enumerate(items):
            if it["sibling"]:
                cps.append(_remote(it["src"], it["dst"](j, cc), send.at[a, 0], recv.at[a, 0], (x, y, 1 - cc)))
            for r in range(1, N_CHIP):
                kx, ky = _flip(x, y, r)
                cps.append(_remote(it["src"], it["dst"](j, cc), send.at[a, r], recv.at[a, r], (kx, ky, cc)))
        return cps

    def start(ins, outs, scr):
        items = make_items(ins, outs, scr)
        send, recv, lsem = scr[-3:]
        for a, it in enumerate(items):
            if it["local"] is not None:
                src, stage, dst = it["local"]
                lc = pltpu.make_async_copy(src, stage, lsem.at[a])
                lc.start()
                lc.wait()
                pltpu.make_async_copy(stage, dst, lsem.at[a]).start()
        for cp in own_sends(items, send, recv):
            cp.start()

    def finish(ins, outs, scr):
        items = make_items(ins, outs, scr)
        send, recv, lsem = scr[-3:]
        x, y, cc = _place()
        j = 2 * x + y
        sib = (x, y, 1 - cc)
        forwards = []
        for a, it in enumerate(items):
            for r in range(1, N_CHIP):
                kx, ky = _flip(x, y, r)
                got = it["dst"](2 * kx + ky, cc)
                _remote(got, got, send.at[a, r], recv.at[a, r], sib).wait_recv()
                if it["forward"]:
                    cp = _remote(got, got, send.at[a, 3 + r], recv.at[a, 3 + r], sib)
                    cp.start()
                    forwards.append(cp)
        for a, it in enumerate(items):
            if it["sibling"]:
                got = it["dst"](j, 1 - cc)
                _remote(got, got, send.at[a, 0], recv.at[a, 0], sib).wait_recv()
            if it["forward"]:
                for r in range(1, N_CHIP):
                    kx, ky = _flip(x, y, r)
                    got = it["dst"](2 * kx + ky, 1 - cc)
                    _remote(got, got, send.at[a, 3 + r], recv.at[a, 3 + r], sib).wait_recv()
        for cp in own_sends(items, send, recv) + forwards:
            cp.wait_send()
        for a, it in enumerate(items):
            if it["local"] is not None:
                src, stage, dst = it["local"]
                pltpu.make_async_copy(stage, dst, lsem.at[a]).wait()

    return start, finish


def _gather_sems(n):
    return [pltpu.SemaphoreType.DMA((n, 7)), pltpu.SemaphoreType.DMA((n, 7)), pltpu.SemaphoreType.DMA((n,))]


def _weights_gather(bufs, split):
    n = len(bufs)

    def ctx(outs):
        x, y, cc = _place()
        chips = dict(me=2 * x + y, y=2 * x + (1 - y), x=2 * (1 - x) + y, d=2 * (1 - x) + (1 - y))
        devs = dict(y=(x, 1 - y, cc), x=(1 - x, y, cc), d=(1 - x, 1 - y, cc), s=(x, y, 1 - cc))

        def piece(a, kj, pc, q=None):
            if not split[a]:
                return outs[a].at[kj]
            h = bufs[a].shape[1] // 2
            if q is None:
                return outs[a].at[kj, pl.ds(pc * h, h), :]
            return outs[a].at[kj, pl.ds(pc * h + q * (h // 2), h // 2), :]

        return cc, chips, devs, piece

    def directs(a, outs, send, recv):
        cc, chips, devs, piece = ctx(outs)
        if not split[a]:
            whole = piece(a, chips["me"], cc)
            return [_remote(whole, whole, send.at[a, k], recv.at[a, k], devs[t]) for k, t in ((0, "y"), (2, "x"), (4, "d"))]
        q = lambda i: piece(a, chips["me"], cc, i)
        return [_remote(q(0), q(0), send.at[a, 0], recv.at[a, 0], devs["y"]),
                _remote(q(1), q(1), send.at[a, 3], recv.at[a, 3], devs["x"]),
                _remote(q(1), q(1), send.at[a, 1], recv.at[a, 1], devs["y"]),
                _remote(q(0), q(0), send.at[a, 2], recv.at[a, 2], devs["x"])]

    def landed(a, k, outs, send, recv):
        cc, chips, devs, piece = ctx(outs)
        if not split[a]:
            got = piece(a, chips[{0: "y", 2: "x", 4: "d"}[k]], cc)
        elif k < 6:
            got = piece(a, chips[("y", "y", "x", "x", "d", "d")[k]], cc, (0, 1, 0, 1, 0, 1)[k])
        else:
            got = piece(a, chips[("y", "x", "d")[k - 6]], 1 - cc)
        return _remote(got, got, send.at[a, k], recv.at[a, k], devs["s"])

    def passed_on(a, outs, send, recv):
        cc, chips, devs, piece = ctx(outs)
        from_y, from_x = piece(a, chips["y"], cc, 0), piece(a, chips["x"], cc, 1)
        return [_remote(from_y, from_y, send.at[a, 4], recv.at[a, 4], devs["x"]),
                _remote(from_x, from_x, send.at[a, 5], recv.at[a, 5], devs["y"])]

    def to_sibling(a, outs, send, recv):
        cc, chips, devs, piece = ctx(outs)
        return [_remote(piece(a, chips[t], cc), piece(a, chips[t], cc), send.at[a, 6 + i], recv.at[a, 6 + i], devs["s"])
                for i, t in enumerate(("y", "x", "d"))]

    def start(ins, outs, scr):
        send, recv = scr
        per_item = [directs(a, outs, send, recv) for a in range(n)]
        for rank in range(4):
            for cps in per_item:
                if rank < len(cps):
                    cps[rank].start()

    def mid(ins, outs, scr):
        send, recv = scr
        for a in range(n):
            if split[a]:
                fy, fx = passed_on(a, outs, send, recv)
                landed(a, 0, outs, send, recv).wait_recv()
                fy.start()
                landed(a, 3, outs, send, recv).wait_recv()
                fx.start()

    def finish(ins, outs, scr):
        send, recv = scr
        for a in range(n):
            if split[a]:
                for k in (1, 2, 4, 5):
                    landed(a, k, outs, send, recv).wait_recv()
                for cp in to_sibling(a, outs, send, recv):
                    cp.start()
            else:
                for k in (0, 2, 4):
                    landed(a, k, outs, send, recv).wait_recv()
        for a in range(n):
            if split[a]:
                for k in (6, 7, 8):
                    landed(a, k, outs, send, recv).wait_recv()
            cps = directs(a, outs, send, recv)
            if split[a]:
                cps += passed_on(a, outs, send, recv) + to_sibling(a, outs, send, recv)
            for cp in cps:
                cp.wait_send()

    return _Comm(bufs, [jax.ShapeDtypeStruct(b.shape, b.dtype) for b in bufs], {i: i for i in range(n)},
                 [pltpu.SemaphoreType.DMA((n, 9)), pltpu.SemaphoreType.DMA((n, 9))], start, finish, mid)


def _small_gather(arrs):
    n = len(arrs)

    def make_items(ins, outs, scr):
        x, y, cc = _place()
        items = []
        for a in range(n):
            dst = functools.partial(lambda o, kj, pc: o.at[2 * kj + pc], outs[a])
            items.append(dict(src=ins[a], dst=dst, local=(ins[a], scr[a], outs[a].at[4 * x + 2 * y + cc]),
                              sibling=True, forward=True))
        return items

    start, finish = _gather_phases(make_items)
    return _Comm(arrs, [jax.ShapeDtypeStruct((N_DEV,) + a.shape, a.dtype) for a in arrs], {},
                 [pltpu.VMEM(a.shape, a.dtype) for a in arrs] + _gather_sems(n), start, finish)


HBM = pl.BlockSpec(memory_space=pltpu.HBM)
SEM = pl.BlockSpec(memory_space=pltpu.SEMAPHORE)
DATAFLOW = pltpu.SideEffectType.DATAFLOW_SIDE_EFFECTING


class _SemGrid:
    def __init__(self, refs, cols):
        self.refs, self.cols = refs, cols

    @property
    def at(self):
        return self

    def __getitem__(self, idx):
        return self.refs[idx[0] * self.cols + idx[1]]


def _split_start(name, srcs, lands, sem_shape, copies, zeroed=False):
    n, k = len(srcs), len(lands)
    ns = sem_shape[0] * sem_shape[1]

    def body(*refs):
        src_refs, land_refs = refs[:n], refs[n:n + k]
        send = _SemGrid(refs[n + k:n + k + ns], sem_shape[1])
        recv = _SemGrid(refs[n + k + ns:n + k + 2 * ns], sem_shape[1])
        token = refs[-1]
        for cp in copies(src_refs, land_refs, send, recv):
            cp.start()
        token[...] = jnp.zeros(token.shape, F32)

    hbm = lambda a: pltpu.with_memory_space_constraint(a, pltpu.HBM)
    zones = [jnp.zeros(l.shape, l.dtype) if zeroed else lax.empty(l.shape, l.dtype) for l in lands]
    out = pl.pallas_call(
        body, name=name,
        out_shape=[pltpu.SemaphoreType.DMA(())] * (2 * ns)
        + [pltpu.HBM(a.shape, a.dtype) for a in list(srcs) + list(lands)] + [jax.ShapeDtypeStruct((8, 128), F32)],
        in_specs=[HBM] * (n + k), out_specs=[SEM] * (2 * ns) + [HBM] * (n + k) + [VMEM],
        input_output_aliases={i: 2 * ns + i for i in range(n + k)},
        compiler_params=pltpu.CompilerParams(has_side_effects=DATAFLOW),
    )(*[hbm(a) for a in srcs], *[hbm(z) for z in zones])
    return out[:-1], out[-1]


def _split_wait(name, state, n, sem_shape, copies, after):
    ns = sem_shape[0] * sem_shape[1]
    sems, bufs = state[:2 * ns], state[2 * ns:]
    k = len(bufs) - n

    def body(*refs):
        src_refs, land_refs = refs[:n], refs[n:n + k]
        send = _SemGrid(refs[n + k:n + k + ns], sem_shape[1])
        recv = _SemGrid(refs[n + k + ns:n + k + 2 * ns], sem_shape[1])
        cps = copies(src_refs, land_refs, send, recv)
        for cp in cps:
            cp.wait_send()
        for cp in cps:
            cp.wait_recv()

    out = pl.pallas_call(
        body, name=name,
        out_shape=[pltpu.HBM(a.shape, a.dtype) for a in bufs],
        in_specs=[HBM] * (n + k) + [SEM] * (2 * ns) + [ANY] * len(after), out_specs=[HBM] * (n + k),
        input_output_aliases={i: i for i in range(n + k)},
        compiler_params=pltpu.CompilerParams(has_side_effects=DATAFLOW),
    )(*bufs, *sems, *after)
    return out[n:]


def _direct_phases(copies):
    def start(ins, outs, scr):
        for cp in copies(ins, outs, *scr):
            cp.start()

    def finish(ins, outs, scr):
        cps = copies(ins, outs, *scr)
        for cp in cps:
            cp.wait_recv()
        for cp in cps:
            cp.wait_send()

    return start, finish


def _sibling_halves(gs):
    n = len(gs)

    def copies(ins, outs, send, recv):
        x, y, cc = _place()
        cps = []
        for a in range(n):
            h = gs[a].shape[1] // 2
            cps.append(_remote(ins[a].at[:, pl.ds((1 - cc) * h, h), :], outs[a], send.at[a], recv.at[a],
                               (x, y, 1 - cc)))
        return cps

    start, finish = _direct_phases(copies)
    return _Comm(gs, [jax.ShapeDtypeStruct((N_CHIP, g.shape[1] // 2, g.shape[2]), F32) for g in gs], {},
                 [pltpu.SemaphoreType.DMA((n,)), pltpu.SemaphoreType.DMA((n,))], start, finish)


def _exchange_parts(pbs):
    n = len(pbs)

    def copies(ins, outs, send, recv):
        x, y, cc = _place()
        cps = []
        for a in range(n):
            for r in range(1, N_CHIP):
                kx, ky = _flip(x, y, r)
                cps.append(_remote(ins[a].at[2 * kx + ky], outs[a].at[r - 1], send.at[a, r - 1], recv.at[a, r - 1],
                                   (kx, ky, cc)))
        return cps

    lands = [jax.ShapeDtypeStruct((N_CHIP - 1,) + p.shape[1:], p.dtype) for p in pbs]
    return lands, (n, N_CHIP - 1), copies


def _small_parts(arrs):
    n = len(arrs)

    def copies(ins, outs, send, recv):
        x, y, cc = _place()
        b = 4 * x + 2 * y + cc
        cps = []
        for a in range(n):
            for r in range(1, N_DEV):
                dev = ((1 - x) if r & 4 else x, (1 - y) if r & 2 else y, (1 - cc) if r & 1 else cc)
                cps.append(_remote(ins[a], outs[a].at[b], send.at[a, r - 1], recv.at[a, r - 1], dev))
        return cps

    lands = [jax.ShapeDtypeStruct((N_DEV,) + a.shape, a.dtype) for a in arrs]
    return lands, (n, N_DEV - 1), copies


def _exchange_partials(pbs):
    lands, sem_shape, copies = _exchange_parts(pbs)
    start, finish = _direct_phases(copies)
    return _Comm(pbs, lands, {}, [pltpu.SemaphoreType.DMA(sem_shape), pltpu.SemaphoreType.DMA(sem_shape)],
                 start, finish)


def _join_halves(fulls):
    n = len(fulls)

    def copies(ins, outs, send, recv):
        x, y, cc = _place()
        cps = []
        for a in range(n):
            h = fulls[a].shape[0] // 2
            mine = outs[a].at[pl.ds(cc * h, h), :]
            cps.append(_remote(mine, mine, send.at[a], recv.at[a], (x, y, 1 - cc)))
        return cps

    start, finish = _direct_phases(copies)
    return _Comm(fulls, [jax.ShapeDtypeStruct(f.shape, F32) for f in fulls], {i: i for i in range(n)},
                 [pltpu.SemaphoreType.DMA((n,)), pltpu.SemaphoreType.DMA((n,))], start, finish)


def _cast_weights(place, shards, dww, wg):
    n = len(shards)

    def body(pref, *refs):
        ins, outs = refs[:n + 2], refs[n + 2:]
        for a in range(n):
            outs[a][0] = ins[a][...].astype(MXU_DTYPE)
        outs[n][0] = ins[n][...]
        outs[n + 1][...] = ins[n + 1][...].astype(MXU_DTYPE)

    full = lambda a: pl.BlockSpec(a.shape, lambda i, pref: (0,) * a.ndim)
    slot = lambda a: pl.BlockSpec((1,) + a.shape, lambda i, pref: (pref[1],) + (0,) * a.ndim)
    arrs = list(shards) + [dww, wg]
    return pl.pallas_call(
        body, name="cast_weights",
        grid_spec=pltpu.PrefetchScalarGridSpec(
            num_scalar_prefetch=1, grid=(1,), in_specs=[full(a) for a in arrs],
            out_specs=[slot(a) for a in arrs[:n + 1]] + [full(wg)]),
        out_shape=[jax.ShapeDtypeStruct((N_CHIP,) + a.shape, MXU_DTYPE) for a in shards]
        + [jax.ShapeDtypeStruct((N_CHIP,) + dww.shape, F32), jax.ShapeDtypeStruct(wg.shape, MXU_DTYPE)],
        compiler_params=pltpu.CompilerParams(dimension_semantics=("arbitrary",)),
    )(place, *arrs)


def _mixer_fwd(x, mod, g1, w_in, dww, dwb, lng, lnb, w_pw, wg, pscale, w_out, comm=None):
    s = x.shape[0]
    ts = _token_tile(s)
    nt = s // ts

    def body(x_ref, mod_ref, g1_ref, win_ref, dww_ref, dwb_ref, lng_ref, lnb_ref, wpw_ref, wg_ref, ps_ref,
             wout_ref, x2_ref, y_ref, u_ref, z_ref, rstd_ref, p_ref, ycat_ref, gpad, vpad, gshift):
        i = pl.program_id(0)

        @pl.when(i == 0)
        def _():
            gpad[0:CONV_HALO, :] = jnp.zeros((CONV_HALO, CONV_W), F32)
            vpad[0:POOL_HALO, :] = jnp.zeros((POOL_HALO, POOL_W), F32)

        xt = x_ref[...]
        sh1 = mod_ref[0:1, :]
        sc1 = mod_ref[1:2, :]
        gt1 = mod_ref[2:3, :]
        r1 = lax.rsqrt(jnp.mean(xt * xt, axis=-1, keepdims=True) + EPS)
        h1 = (xt * r1 * g1_ref[...]) * (1.0 + sc1) + sh1
        h1b = h1.astype(MXU_DTYPE)
        u = jnp.concatenate([_dot(h1b, win_ref[j]) for j in range(N_CHIP)], axis=1)
        u_ref[...] = u
        a = u[:, :CONV_W]
        g = u[:, CONV_W:2 * CONV_W]
        v = u[:, 2 * CONV_W:]

        gpad[CONV_HALO:CONV_HALO + ts, :] = a * _sigmoid(g)
        window = _row_shifts(gpad, gshift, ts + CONV_HALO - SUBLANES)
        cv = jnp.broadcast_to(dwb_ref[...], (ts, CONV_W))
        off = CONV_HALO - (CONV_K - 1)
        for k in range(CONV_K):
            cv = cv + dww_ref[k:k + 1, :] * window(off + k, ts)
        gpad[0:CONV_HALO, :] = gpad[ts:ts + CONV_HALO, :]

        mu = jnp.mean(cv, axis=-1, keepdims=True)
        cc = cv - mu
        rstd = lax.rsqrt(jnp.mean(cc * cc, axis=-1, keepdims=True) + EPS)
        z = cc * rstd
        z_ref[...] = z
        rstd_ref[...] = rstd
        ln = z * lng_ref[...] + lnb_ref[...]
        sw = ln * _sigmoid(ln)
        yconv = _dot(sw, wpw_ref[...])

        vpad[POOL_HALO:POOL_HALO + ts, :] = v
        t = i * ts + lax.broadcasted_iota(jnp.int32, (ts, 1), 0)
        ps, ypool = [], []
        for gi, w in enumerate(POOL_WINDOWS):
            cols = slice(gi * POOL_G, (gi + 1) * POOL_G)
            acc = vpad[POOL_HALO:POOL_HALO + ts, cols]
            for d in range(1, w):
                acc = acc + vpad[POOL_HALO - d:POOL_HALO - d + ts, cols]
            cnt = jnp.minimum(t + 1, w).astype(F32)
            pg = (acc / cnt - v[:, cols]).astype(MXU_DTYPE)
            ps.append(pg)
            ypool.append(_dot(pg, wg_ref[gi]))
        vpad[0:POOL_HALO, :] = vpad[ts:ts + POOL_HALO, :]
        p_ref[...] = jnp.concatenate(ps, axis=1)
        ypool = jnp.concatenate(ypool, axis=1) * ps_ref[...]

        ycat = jnp.concatenate([yconv, ypool], axis=1).astype(MXU_DTYPE)
        ycat_ref[...] = ycat
        y = _dot(ycat, wout_ref[...])
        y_ref[...] = y
        x2_ref[...] = xt + gt1 * y

    tile = lambda w: pl.BlockSpec((ts, w), lambda i: (i, 0))
    return _call(
        body, name="mixer_fwd", grid=(nt,),
        in_specs=[tile(D_MODEL), _full(mod.shape), _full(g1.shape), _full(w_in.shape), _full(dww.shape),
                  _full(dwb.shape), _full(lng.shape), _full(lnb.shape), _full(w_pw.shape), _full(wg.shape),
                  _full(pscale.shape), _full(w_out.shape)],
        out_specs=[tile(D_MODEL), tile(D_MODEL), tile(IN_W), tile(CONV_W), tile(1), tile(POOL_W), tile(D_MODEL)],
        out_shape=[jax.ShapeDtypeStruct((s, D_MODEL), F32), jax.ShapeDtypeStruct((s, D_MODEL), F32),
                   jax.ShapeDtypeStruct((s, IN_W), F32), jax.ShapeDtypeStruct((s, CONV_W), F32),
                   jax.ShapeDtypeStruct((s, 1), F32), jax.ShapeDtypeStruct((s, POOL_W), MXU_DTYPE),
                   jax.ShapeDtypeStruct((s, D_MODEL), MXU_DTYPE)],
        scratch_shapes=[pltpu.VMEM((ts + CONV_HALO, CONV_W), F32), pltpu.VMEM((ts + POOL_HALO, POOL_W), F32),
                        pltpu.VMEM((SUBLANES - 1, ts + CONV_HALO - SUBLANES, CONV_W), F32)],
        args=(x, mod, g1, w_in, dww, dwb, lng, lnb, w_pw, wg, pscale, w_out), comm=comm)


def _ffn(x2, tgt, mod, g2, gf, w_gate, w_up, w_down):
    s = x2.shape[0]
    ts = _token_tile(s)
    nt = s // ts
    fb = w_gate.shape[1]

    def body(x2_ref, tgt_ref, mod_ref, g2_ref, gf_ref, wgt_ref, wup_ref, wdn_ref,
             dx2_ref, h2_ref, df_ref, act_ref, dgg_ref, duu_ref, vec_ref, gg_s, uu_s):
        i = pl.program_id(0)

        @pl.when(i == 0)
        def _():
            vec_ref[...] = jnp.zeros(vec_ref.shape, F32)

        x2t = x2_ref[...]
        sh2 = mod_ref[3:4, :]
        sc2 = mod_ref[4:5, :]
        gt2 = mod_ref[5:6, :]
        g2v = g2_ref[...]
        gfv = gf_ref[...]
        r2 = lax.rsqrt(jnp.mean(x2t * x2t, axis=-1, keepdims=True) + EPS)
        xh2 = x2t * r2
        n2 = xh2 * g2v
        h2b = (n2 * (1.0 + sc2) + sh2).astype(MXU_DTYPE)
        h2_ref[...] = h2b
        f = jnp.zeros((ts, D_MODEL), F32)
        for j in range(N_CHIP):
            gg = _dot_nt(h2b, wgt_ref[j])
            uu = _dot_nt(h2b, wup_ref[j])
            gg_s[j] = gg
            uu_s[j] = uu
            actb = (gg * _sigmoid(gg) * uu).astype(MXU_DTYPE)
            act_ref[j] = actb
            f = f + _dot(actb, wdn_ref[j])
        x3 = x2t + gt2 * f
        r3 = lax.rsqrt(jnp.mean(x3 * x3, axis=-1, keepdims=True) + EPS)
        xh3 = x3 * r3
        diff = xh3 * gfv - tgt_ref[...]
        dout = diff * (1.0 / D_MODEL)
        dn3 = dout * gfv
        dx3 = r3 * (dn3 - xh3 * jnp.mean(dn3 * xh3, axis=-1, keepdims=True))
        dfb = (dx3 * gt2).astype(MXU_DTYPE)
        df_ref[...] = dfb
        dh2 = jnp.zeros((ts, D_MODEL), F32)
        for j in range(N_CHIP):
            dact = _dot_nt(dfb, wdn_ref[j])
            gg = gg_s[j]
            uu = uu_s[j]
            sg = _sigmoid(gg)
            duu = (dact * (gg * sg)).astype(MXU_DTYPE)
            dgg = (dact * uu * (sg * (1.0 + gg * (1.0 - sg)))).astype(MXU_DTYPE)
            duu_ref[j] = duu
            dgg_ref[j] = dgg
            dh2 = dh2 + _dot(dgg, wgt_ref[j]) + _dot(duu, wup_ref[j])
        dn2 = dh2 * (1.0 + sc2)
        dxh2 = dn2 * g2v
        dx2_ref[...] = dx3 + r2 * (dxh2 - xh2 * jnp.mean(dxh2 * xh2, axis=-1, keepdims=True))

        col = lambda a: jnp.sum(a, axis=0, keepdims=True)
        vec_ref[0:1, :] += col(dout * xh3)
        vec_ref[1:2, :] += col(dx3 * f)
        vec_ref[2:3, :] += col(dh2)
        vec_ref[3:4, :] += col(dh2 * n2)
        vec_ref[4:5, :] += col(dn2 * xh2)
        vec_ref[5:6, :] += col(diff * diff)

    tile = lambda w: pl.BlockSpec((ts, w), lambda i: (i, 0))
    tile3 = pl.BlockSpec((N_CHIP, ts, fb), lambda i: (0, i, 0))
    once = lambda a: pl.BlockSpec(a.shape, lambda i: (0,) * a.ndim, pipeline_mode=pl.Buffered(1))
    hid = jax.ShapeDtypeStruct((N_CHIP, s, fb), MXU_DTYPE)
    return pl.pallas_call(
        body, name="ffn", grid=(nt,),
        in_specs=[tile(D_MODEL), tile(D_MODEL), _full(mod.shape), _full(g2.shape), _full(gf.shape),
                  once(w_gate), once(w_up), once(w_down)],
        out_specs=[tile(D_MODEL), tile(D_MODEL), tile(D_MODEL), tile3, tile3, tile3, _full((8, D_MODEL))],
        out_shape=[jax.ShapeDtypeStruct((s, D_MODEL), F32), jax.ShapeDtypeStruct((s, D_MODEL), MXU_DTYPE),
                   jax.ShapeDtypeStruct((s, D_MODEL), MXU_DTYPE), hid, hid, hid,
                   jax.ShapeDtypeStruct((8, D_MODEL), F32)],
        scratch_shapes=[pltpu.VMEM((N_CHIP, ts, fb), F32), pltpu.VMEM((N_CHIP, ts, fb), F32)],
        compiler_params=pltpu.CompilerParams(dimension_semantics=("arbitrary",)),
    )(x2, tgt, mod, g2, gf, w_gate, w_up, w_down)


def _mixer_bwd(dx2, x, y, u, z, rstd, p, mod, g1, w_in, dww, lng, lnb, w_pw, wg, pscale, w_out, comm=None, after=()):
    s = x.shape[0]
    ts = _token_tile(s)
    nt = s // ts

    def body(dx2_ref, x_ref, y_ref, u_ref, z_ref, rstd_ref, p_ref, mod_ref, g1_ref, win_ref, dww_ref, lng_ref,
             lnb_ref, wpw_ref, wg_ref, ps_ref, wout_ref,
             gx_ref, h1_ref, du_ref, dy_ref, sw_ref, dyc_ref, dyp_ref, vd_ref, vc_ref, ddw_ref, dcpad, dppad,
             dshift):
        i = pl.program_id(0)
        tix = nt - 1 - i

        @pl.when(i == 0)
        def _():
            vd_ref[...] = jnp.zeros(vd_ref.shape, F32)
            vc_ref[...] = jnp.zeros(vc_ref.shape, F32)
            ddw_ref[...] = jnp.zeros(ddw_ref.shape, F32)
            dcpad[ts:ts + CONV_HALO, :] = jnp.zeros((CONV_HALO, CONV_W), F32)
            dppad[ts:ts + POOL_HALO, :] = jnp.zeros((POOL_HALO, POOL_W), F32)

        col = lambda a: jnp.sum(a, axis=0, keepdims=True)
        sh1 = mod_ref[0:1, :]
        sc1 = mod_ref[1:2, :]
        gt1 = mod_ref[2:3, :]
        dx2t = dx2_ref[...]
        vd_ref[0:1, :] += col(dx2t * y_ref[...])
        dyb = (dx2t * gt1).astype(MXU_DTYPE)
        dy_ref[...] = dyb
        dycat = _dot_nt(dyb, wout_ref[...])
        dyconv = dycat[:, :CONV_W]
        dypool = dycat[:, CONV_W:]

        pt = p_ref[...]
        t = tix * ts + lax.broadcasted_iota(jnp.int32, (ts, 1), 0)
        psc = ps_ref[...]
        dypb = (dypool * psc).astype(MXU_DTYPE)
        dyp_ref[...] = dypb
        dps, ypre = [], []
        for gi, w in enumerate(POOL_WINDOWS):
            cols = slice(gi * POOL_G, (gi + 1) * POOL_G)
            ypre.append(_dot(pt[:, cols], wg_ref[gi]))
            dpg = _dot_nt(dypb[:, cols], wg_ref[gi])
            dps.append(dpg)
            cnt = jnp.minimum(t + 1, w).astype(F32)
            dppad[0:ts, cols] = dpg / cnt
        vc_ref[0:1, :] += col(dypool * jnp.concatenate(ypre, axis=1))
        dvs = []
        for gi, w in enumerate(POOL_WINDOWS):
            cols = slice(gi * POOL_G, (gi + 1) * POOL_G)
            acc = dppad[0:ts, cols]
            for d in range(1, w):
                acc = acc + dppad[d:d + ts, cols]
            dvs.append(acc - dps[gi])
        dv = jnp.concatenate(dvs, axis=1)
        dppad[ts:ts + POOL_HALO, :] = dppad[0:POOL_HALO, :]

        zt = z_ref[...]
        lngv = lng_ref[...]
        ln = zt * lngv + lnb_ref[...]
        sg = _sigmoid(ln)
        swb = (ln * sg).astype(MXU_DTYPE)
        sw_ref[...] = swb
        dycb = dyconv.astype(MXU_DTYPE)
        dyc_ref[...] = dycb
        dln = _dot_nt(dycb, wpw_ref[...]) * (sg * (1.0 + ln * (1.0 - sg)))
        vc_ref[1:2, :] += col(dln * zt)
        vc_ref[2:3, :] += col(dln)
        dz = dln * lngv
        dcv = rstd_ref[...] * (dz - jnp.mean(dz, axis=-1, keepdims=True)
                               - zt * jnp.mean(dz * zt, axis=-1, keepdims=True))
        vc_ref[3:4, :] += col(dcv)
        dcpad[0:ts, :] = dcv
        ut = u_ref[...]
        a = ut[:, :CONV_W]
        g = ut[:, CONV_W:2 * CONV_W]
        sgg = _sigmoid(g)
        glu = a * sgg
        window = _row_shifts(dcpad, dshift, ts + CONV_HALO - SUBLANES)
        dglu = jnp.zeros((ts, CONV_W), F32)
        for k in range(CONV_K):
            sh = window(CONV_K - 1 - k, ts)
            dglu = dglu + dww_ref[k:k + 1, :] * sh
            ddw_ref[k:k + 1, :] += col(glu * sh)
        dcpad[ts:ts + CONV_HALO, :] = dcpad[0:CONV_HALO, :]
        da = dglu * sgg
        dg = dglu * a * sgg * (1.0 - sgg)
        dub = jnp.concatenate([da, dg, dv], axis=1).astype(MXU_DTYPE)
        du_ref[...] = dub
        cw = IN_W // N_CHIP
        dh1 = jnp.zeros((ts, D_MODEL), F32)
        for j in range(N_CHIP):
            dh1 = dh1 + _dot_nt(dub[:, j * cw:(j + 1) * cw], win_ref[j])

        xt = x_ref[...]
        g1v = g1_ref[...]
        r1 = lax.rsqrt(jnp.mean(xt * xt, axis=-1, keepdims=True) + EPS)
        xh1 = xt * r1
        n1 = xh1 * g1v
        h1_ref[...] = (n1 * (1.0 + sc1) + sh1).astype(MXU_DTYPE)
        vd_ref[1:2, :] += col(dh1)
        vd_ref[2:3, :] += col(dh1 * n1)
        dn1 = dh1 * (1.0 + sc1)
        vd_ref[3:4, :] += col(dn1 * xh1)
        dxh = dn1 * g1v
        gx_ref[...] = dx2t + r1 * (dxh - xh1 * jnp.mean(dxh * xh1, axis=-1, keepdims=True))

    tile = lambda w: pl.BlockSpec((ts, w), lambda i: (nt - 1 - i, 0))
    bf = lambda w: jax.ShapeDtypeStruct((s, w), MXU_DTYPE)
    return _call(
        body, name="mixer_bwd", grid=(nt,),
        in_specs=[tile(D_MODEL), tile(D_MODEL), tile(D_MODEL), tile(IN_W), tile(CONV_W), tile(1), tile(POOL_W),
                  _full(mod.shape), _full(g1.shape), _full(w_in.shape), _full(dww.shape), _full(lng.shape),
                  _full(lnb.shape), _full(w_pw.shape), _full(wg.shape), _full(pscale.shape), _full(w_out.shape)],
        out_specs=[tile(D_MODEL), tile(D_MODEL), tile(IN_W), tile(D_MODEL), tile(CONV_W), tile(CONV_W),
                   tile(POOL_W), _full((8, D_MODEL)), _full((8, CONV_W)), _full((32, CONV_W))],
        out_shape=[jax.ShapeDtypeStruct((s, D_MODEL), F32), bf(D_MODEL), bf(IN_W), bf(D_MODEL), bf(CONV_W),
                   bf(CONV_W), bf(POOL_W), jax.ShapeDtypeStruct((8, D_MODEL), F32),
                   jax.ShapeDtypeStruct((8, CONV_W), F32), jax.ShapeDtypeStruct((32, CONV_W), F32)],
        scratch_shapes=[pltpu.VMEM((ts + CONV_HALO, CONV_W), F32), pltpu.VMEM((ts + POOL_HALO, POOL_W), F32),
                        pltpu.VMEM((SUBLANES - 1, ts + CONV_HALO - SUBLANES, CONV_W), F32)],
        args=(dx2, x, y, u, z, rstd, p, mod, g1, w_in, dww, lng, lnb, w_pw, wg, pscale, w_out), comm=comm,
        after=after)


def _dw(name, a, a_spec, b, b_spec, nb, mb, nbk, comm=None, after=()):
    def body(a_ref, b_ref, o_ref):
        av = a_ref[...]
        bv = b_ref[...]
        av = av.reshape(av.shape[-2:])
        bv = bv.reshape(bv.shape[-2:])
        o_ref[0] = _dot_tn(av, bv)

    (out,), rest = _call(
        body, name=name, grid=(nb,), in_specs=[a_spec, b_spec],
        out_specs=[pl.BlockSpec((1, mb, nbk), lambda j: (j, 0, 0))],
        out_shape=[jax.ShapeDtypeStruct((nb, mb, nbk), F32)], args=(a, b), comm=comm, after=after)
    return out, rest


def _ada_fwd(c, w_ada, b4, comm=None):
    nc = w_ada.shape[1]

    def body(start_comm, c_ref, w_ref, b4_ref, mod_ref, cact_ref, call, part, parts, send1, recv1, send2, recv2):
        x, y, cc = _place()
        b = 4 * x + 2 * y + cc
        j = 2 * x + y
        call[b] = c_ref[...]
        sends = []
        for r in range(1, N_DEV):
            dev = ((1 - x) if r & 4 else x, (1 - y) if r & 2 else y, (1 - cc) if r & 1 else cc)
            cp = _remote(call.at[b], call.at[b], send1.at[r - 1], recv1.at[r - 1], dev)
            cp.start()
            sends.append(cp)
        for r in range(1, N_DEV):
            src_b = lax.bitwise_xor(b, r)
            _remote(call.at[src_b], call.at[src_b], send1.at[r - 1], recv1.at[r - 1], (x, y, cc)).wait_recv()
        for cp in sends:
            cp.wait_send()
        start_comm()
        for i in range(N_DEV):
            ci = call[i]
            cact_ref[i:i + 1, :] = ci * _sigmoid(ci)
        part[...] = jnp.dot(cact_ref[...], w_ref[...], preferred_element_type=F32, precision=lax.Precision.HIGHEST)
        sends = []
        for r in range(1, N_CHIP):
            kx, ky = _flip(x, y, r)
            cp = _remote(part, parts.at[j], send2.at[r - 1], recv2.at[r - 1], (kx, ky, cc))
            cp.start()
            sends.append(cp)
        parts[j] = part[...]
        for r in range(1, N_CHIP):
            kx, ky = _flip(x, y, r)
            kj = 2 * kx + ky
            _remote(part, parts.at[kj], send2.at[r - 1], recv2.at[r - 1], (x, y, cc)).wait_recv()
        for cp in sends:
            cp.wait_send()
        mine = lax.broadcasted_iota(jnp.int32, (N_DEV, 1), 0) == b
        for k in range(N_CHIP):
            row = jnp.sum(jnp.where(mine, parts[k], 0.0), axis=0, keepdims=True)
            mod_ref[k:k + 1, :] = row + b4_ref[k:k + 1, :]

    return _call(
        body, name="ada_fwd", grid=(1,),
        in_specs=[VMEM, VMEM, VMEM], out_specs=[VMEM, VMEM],
        out_shape=[jax.ShapeDtypeStruct((N_CHIP, nc), F32), jax.ShapeDtypeStruct((N_DEV, D_MODEL), F32)],
        scratch_shapes=[pltpu.VMEM((N_DEV, 1, D_MODEL), F32), pltpu.VMEM((N_DEV, nc), F32),
                        pltpu.VMEM((N_CHIP, N_DEV, nc), F32),
                        pltpu.SemaphoreType.DMA((N_DEV - 1,)), pltpu.SemaphoreType.DMA((N_DEV - 1,)),
                        pltpu.SemaphoreType.DMA((N_CHIP - 1,)), pltpu.SemaphoreType.DMA((N_CHIP - 1,))],
        args=(c, w_ada, b4), comm=comm, body_starts=True)


def _chip_partials(name, place, gs, rs, comm=None):
    n = len(gs)

    def body(pref, *refs):
        g_refs, r_refs = refs[:n], refs[n:2 * n]
        pb_refs, own_refs = refs[2 * n:3 * n], refs[3 * n:]
        jj = pl.program_id(0)
        for a in range(n):
            sm = g_refs[a][0] + r_refs[a][0]
            pb_refs[a][0] = sm.astype(MXU_DTYPE)

            @pl.when(jj == pref[1])
            def _(a=a, sm=sm):
                own_refs[a][...] = sm

    halves = [(g.shape[1] // 2, g.shape[2]) for g in gs]
    in_specs = [pl.BlockSpec((1, h, w), lambda jj, pref: (jj, pref[0], 0)) for h, w in halves]
    in_specs += [pl.BlockSpec((1, h, w), lambda jj, pref: (jj, 0, 0)) for h, w in halves]
    out_specs = [pl.BlockSpec((1, h, w), lambda jj, pref: (jj, 0, 0)) for h, w in halves]
    out_specs += [pl.BlockSpec((h, w), lambda jj, pref: (0, 0)) for h, w in halves]
    out, rest = _call(
        body, name=name, grid=(N_CHIP,), in_specs=in_specs, out_specs=out_specs,
        out_shape=[jax.ShapeDtypeStruct((N_CHIP, h, w), MXU_DTYPE) for h, w in halves]
        + [jax.ShapeDtypeStruct((h, w), F32) for h, w in halves],
        args=(*gs, *rs), prefetch=(place,), comm=comm)
    return (out[:n], out[n:]), rest


def _sum_partials(name, place, owns, recvd, comm=None):
    n = len(owns)

    def body(pref, *refs):
        o_refs, r_refs, out_refs = refs[:n], refs[n:2 * n], refs[2 * n:]
        for a in range(n):
            acc = o_refs[a][...]
            for r in range(N_CHIP - 1):
                acc = acc + r_refs[a][r].astype(F32)
            out_refs[a][...] = acc

    full = lambda a: pl.BlockSpec(a.shape, lambda i, pref: (0,) * a.ndim)
    return _call(
        body, name=name, grid=(1,), in_specs=[full(a) for a in list(owns) + list(recvd)],
        out_specs=[pl.BlockSpec(o.shape, lambda i, pref: (pref[0], 0)) for o in owns],
        out_shape=[jax.ShapeDtypeStruct((2 * o.shape[0], o.shape[1]), F32) for o in owns],
        args=(*owns, *recvd), prefetch=(place,), comm=comm)


def _adamw_math(w, g, m, v):
    m = ADAM_B1 * m + (1.0 - ADAM_B1) * g
    v = ADAM_B2 * v + (1.0 - ADAM_B2) * (g * g)
    m_hat = m / (1.0 - ADAM_B1 ** ADAM_STEP)
    v_hat = v / (1.0 - ADAM_B2 ** ADAM_STEP)
    delta = -ADAM_LR * (m_hat / (jnp.sqrt(v_hat) + ADAM_EPS) + ADAM_WD * w)
    return delta, m, v


def _row_tile(rows):
    for t in (512, 352, 256, 128):
        if rows % t == 0:
            return t
    return rows


def _adamw(name, wgmv, steps, after=()):
    n = len(wgmv)

    def body(*refs):
        ins, outs = refs[:4 * n], refs[4 * n:]
        for i in range(n):
            w_ref, g_ref, m_ref, v_ref = ins[4 * i:4 * i + 4]
            d_ref, nm_ref, nv_ref = outs[3 * i:3 * i + 3]
            d_ref[...], nm_ref[...], nv_ref[...] = _adamw_math(w_ref[...], g_ref[...], m_ref[...], v_ref[...])

    in_specs, out_specs, out_shape, args = [], [], [], []
    for w, g, m, v in wgmv:
        rows, cols = w.shape
        spec = pl.BlockSpec((rows // steps, cols), lambda i: (i, 0))
        in_specs += [spec] * 4
        out_specs += [spec] * 3
        out_shape += [jax.ShapeDtypeStruct(w.shape, F32)] * 3
        args += [w, g, m, v]
    res, _ = _call(body, name=name, grid=(steps,), in_specs=in_specs, out_specs=out_specs, out_shape=out_shape,
                   args=args, after=after)
    return [res[3 * i:3 * i + 3] for i in range(n)]


def _adamw_ada(place, cact, dmod, w, m, v, after=()):
    rows, cols = w.shape
    tr = _row_tile(rows)

    def body(pref, ca_ref, dm_ref, w_ref, m_ref, v_ref, g_ref, d_ref, nm_ref, nv_ref):
        g = lax.dot_general(ca_ref[...], dm_ref[...], (((0,), (0,)), ((), ())), preferred_element_type=F32,
                            precision=lax.Precision.HIGHEST)
        g_ref[...] = g
        d_ref[...], nm_ref[...], nv_ref[...] = _adamw_math(w_ref[...], g, m_ref[...], v_ref[...])

    spec = pl.BlockSpec((tr, cols), lambda i, pref: (i, 0))
    return _call(
        body, name="adamw_ada", grid=(rows // tr,),
        in_specs=[pl.BlockSpec((N_DEV, tr), lambda i, pref: (0, i)),
                  pl.BlockSpec((N_DEV, cols), lambda i, pref: (0, pref[1])), spec, spec, spec],
        out_specs=[spec] * 4, out_shape=[jax.ShapeDtypeStruct(w.shape, F32)] * 4,
        args=(cact, dmod, w, m, v), prefetch=(place,), after=after)[0]


def _adamw_small(place, owns, gathered, wmv):
    nw = len(wmv)
    flat = [a for t in wmv for a in t]

    def body(pref, *refs):
        own_refs, all_refs, refs = refs[:5], refs[5:10], refs[10:]
        w_refs = refs[:3 * nw]
        loss_ref, dmod_ref = refs[3 * nw], refs[3 * nw + 1]
        o_refs = refs[3 * nw + 2:]
        j = pref[1]
        me = 2 * pref[1] + pref[0]

        def total(i):
            acc = None
            for b in range(N_DEV):
                blk = jnp.where(me == b, own_refs[i][...], all_refs[i][b])
                acc = blk if acc is None else acc + blk
            return acc

        vf, vd, vc, ddw, gwg = [total(i) for i in range(5)]
        loss_ref[...] = (0.5 / D_MODEL) * jnp.sum(vf[5:6, :], axis=1, keepdims=True)
        order = ((1, 1), (1, 2), (1, 0), (0, 2), (0, 3), (0, 1))
        for b in range(N_DEV):
            for q, (i, row) in enumerate(order):
                dmod_ref[b:b + 1, q * D_MODEL:(q + 1) * D_MODEL] = jnp.where(
                    me == b, own_refs[i][row:row + 1, :], all_refs[i][b, row:row + 1, :])
        dm = dmod_ref[...]
        g_bada = dm[0:1, :]
        for b in range(1, N_DEV):
            g_bada = g_bada + dm[b:b + 1, :]
        g_dww = jnp.zeros((32, POOL_G), F32)
        for k in range(N_CHIP):
            g_dww = g_dww + jnp.where(j == k, ddw[:, k * POOL_G:(k + 1) * POOL_G], 0.0)
        grads = [g_bada, vd[3:4, :], g_dww, vc[3:4, :], vc[1:2, :], vc[2:3, :], gwg, vc[0:1, :], vf[4:5, :],
                 vf[0:1, :]]
        for i, g in enumerate(grads):
            w_ref, m_ref, v_ref = w_refs[3 * i:3 * i + 3]
            d, nm, nv = _adamw_math(w_ref[...], g, m_ref[...], v_ref[...])
            o_refs[4 * i][...] = g
            o_refs[4 * i + 1][...] = d
            o_refs[4 * i + 2][...] = nm
            o_refs[4 * i + 3][...] = nv

    outs = [jax.ShapeDtypeStruct((1, 1), F32), jax.ShapeDtypeStruct((N_DEV, 6 * D_MODEL), F32)]
    for w, _, _ in wmv:
        outs += [jax.ShapeDtypeStruct(w.shape, F32)] * 4
    full = lambda a: pl.BlockSpec(a.shape, lambda i, pref: (0,) * a.ndim)
    args = list(owns) + list(gathered) + flat
    res, _ = _call(body, name="adamw_small", grid=(1,), in_specs=[full(a) for a in args],
                   out_specs=[full(o) for o in outs], out_shape=outs, args=args, prefetch=(place,))
    return res[0], res[1], [res[2 + 4 * i:6 + 4 * i] for i in range(nw)]


def kernel(x, c, w_ada, b_ada, g_norm1, w_in, dw_w, dw_b, conv_ln_g, conv_ln_b, w_conv_pw, w_pool_group, pool_scale, w_out, g_norm2, w_ffn_gate, w_ffn_up, w_ffn_down, g_final, loss_target, m_w_ada, m_b_ada, m_g_norm1, m_w_in, m_dw_w, m_dw_b, m_conv_ln_g, m_conv_ln_b, m_w_conv_pw, m_w_pool_group, m_pool_scale, m_w_out, m_g_norm2, m_w_ffn_gate, m_w_ffn_up, m_w_ffn_down, m_g_final, v_w_ada, v_b_ada, v_g_norm1, v_w_in, v_dw_w, v_dw_b, v_conv_ln_g, v_conv_ln_b, v_w_conv_pw, v_w_pool_group, v_pool_scale, v_w_out, v_g_norm2, v_w_ffn_gate, v_w_ffn_up, v_w_ffn_down, v_g_final):
    xi, yi, ci = _place()
    place = jnp.stack([ci, 2 * xi + yi]).astype(jnp.int32)
    n_ada = w_ada.shape[2]

    tr = lambda a: jnp.transpose(a[0])
    big = [w_in[0], w_conv_pw[0], w_out[0], tr(w_ffn_gate), tr(w_ffn_up), w_ffn_down[0]]
    b_in, b_pw, b_out, b_gate, b_up, b_down, b_dww, wg_b = _cast_weights(place, big, dw_w[0], w_pool_group[0])

    (mod4, cact), (win_g, wpw_g, wout_g, dww_g) = _ada_fwd(
        c, w_ada[0], b_ada.reshape(N_CHIP, n_ada),
        comm=_weights_gather([b_in, b_pw, b_out, b_dww], [True, True, True, False]))
    mod = mod4.reshape(6, D_MODEL)
    dww_full = jnp.pad(jnp.concatenate([dww_g[k] for k in range(N_CHIP)], axis=1), ((0, 1), (0, 0)))
    w_pw = wpw_g.reshape(CONV_W, CONV_W)
    w_o = wout_g.reshape(D_MODEL, D_MODEL)
    xs, tgt, gf = x[0], loss_target[0], g_final.reshape(1, D_MODEL)
    s = xs.shape[0]
    fb = b_gate.shape[1]

    (x2, y, u, z, rstd, p, ycat), (wgate_g, wup_g, wdown_g) = _mixer_fwd(
        xs, mod, g_norm1, win_g, dww_full, dw_b, conv_ln_g, conv_ln_b, w_pw, wg_b, pool_scale, w_o,
        comm=_weights_gather([b_gate, b_up, b_down], [True, True, True]))
    dx2, h2, df, act, dgg, duu, vec_f = _ffn(x2, tgt, mod, g_norm2, gf, wgate_g, wup_g, wdown_g)

    whole = lambda w: pl.BlockSpec((s, w), lambda j: (0, 0))
    cols = lambda w: pl.BlockSpec((s, w), lambda j: (0, j))
    hid = pl.BlockSpec((1, s, fb), lambda j: (j, 0, 0))
    c_gate, _ = _dw("dw_gate", dgg, hid, h2, whole(D_MODEL), N_CHIP, fb, D_MODEL)
    c_up, (r_gate,) = _dw("dw_up", duu, hid, h2, whole(D_MODEL), N_CHIP, fb, D_MODEL, comm=_sibling_halves([c_gate]))
    ((pb_gate,), (own_gate,)), _ = _chip_partials("partials_gate", place, [c_gate], [r_gate])
    ex_gate = _exchange_parts([pb_gate])
    st_gate, tok_gate = _split_start("exchange_gate_start", [pb_gate], *ex_gate)
    c_down, (r_up,) = _dw("dw_down", act, hid, df, whole(D_MODEL), N_CHIP, fb, D_MODEL,
                          comm=_sibling_halves([c_up]), after=(tok_gate,))
    ((pb_up,), (own_up,)), (r_down,) = _chip_partials("partials_up", place, [c_up], [r_up],
                                                      comm=_sibling_halves([c_down]))
    ((pb_down,), (own_down,)), _ = _chip_partials("partials_down", place, [c_down], [r_down])
    ex_ud = _exchange_parts([pb_up, pb_down])
    st_ud, tok_ud = _split_start("exchange_up_down_start", [pb_up, pb_down], *ex_ud)
    (gx, h1, du, dy, sw, dyc, dyp, vec_d, vec_c, ddw), _ = _mixer_bwd(
        dx2, xs, y, u, z, rstd, p, mod, g_norm1, win_g, dww_full, conv_ln_g, conv_ln_b, w_pw, wg_b, pool_scale, w_o,
        after=(tok_ud,))

    g_wg, _ = _dw("dw_wg", p, cols(POOL_G), dyp, cols(POOL_G), len(POOL_WINDOWS), POOL_G, POOL_G)
    small_own = [vec_f, vec_d, vec_c, ddw, g_wg]
    ex_small = _small_parts(small_own)
    st_small, tok_small = _split_start("small_grads_start", small_own, *ex_small, zeroed=True)
    c_in, _ = _dw("dw_in", h1, whole(D_MODEL), du, cols(IN_W // N_CHIP), N_CHIP, D_MODEL, IN_W // N_CHIP,
                  after=(tok_small,))
    c_out, (r_in,) = _dw("dw_out", ycat, cols(D_MODEL // N_CHIP), dy, whole(D_MODEL), N_CHIP, D_MODEL // N_CHIP,
                         D_MODEL, comm=_sibling_halves([c_in]))
    c_pw, (r_out,) = _dw("dw_pw", sw, cols(CONV_W // N_CHIP), dyc, whole(CONV_W), N_CHIP, CONV_W // N_CHIP, CONV_W,
                         comm=_sibling_halves([c_out]))
    (rc_gate,) = _split_wait("exchange_gate_wait", st_gate, 1, ex_gate[1], ex_gate[2], after=(c_pw,))
    rc_up, rc_down = _split_wait("exchange_up_down_wait", st_ud, 2, ex_ud[1], ex_ud[2], after=(c_pw, rc_gate))
    small_all = _split_wait("small_grads_wait", st_small, len(small_own), ex_small[1], ex_small[2],
                            after=(c_pw, rc_down))

    ffn_fulls, (r_pw,) = _sum_partials("sum_ffn", place, [own_gate, own_up, own_down], [rc_gate, rc_up, rc_down],
                                       comm=_sibling_halves([c_pw]))
    (pbs_mix, owns_mix), (g_gate, g_up, g_down) = _chip_partials(
        "partials_mix", place, [c_in, c_pw, c_out], [r_in, r_pw, r_out], comm=_join_halves(ffn_fulls))

    pad_rows = lambda a: jnp.pad(a[0], ((0, 1), (0, 0)))
    row = lambda a: a.reshape(1, -1)
    small = [(b_ada, m_b_ada, v_b_ada), (g_norm1, m_g_norm1, v_g_norm1),
             (pad_rows(dw_w), pad_rows(m_dw_w), pad_rows(v_dw_w)), (dw_b, m_dw_b, v_dw_b),
             (conv_ln_g, m_conv_ln_g, v_conv_ln_g), (conv_ln_b, m_conv_ln_b, v_conv_ln_b),
             (w_pool_group[0], m_w_pool_group[0], v_w_pool_group[0]), (pool_scale, m_pool_scale, v_pool_scale),
             (g_norm2, m_g_norm2, v_g_norm2), (row(g_final), row(m_g_final), row(v_g_final))]
    loss, dmod, small_out = _adamw_small(place, small_own, small_all, small)
    (o_bada, o_g1, o_dww, o_dwb, o_lng, o_lnb, o_wg, o_ps, o_g2, o_gf) = small_out
    o_dww = [a[:CONV_K] for a in o_dww]
    o_gf = [a.reshape(D_MODEL) for a in o_gf]
    lead = lambda outs: [a[None] for a in outs]

    lands, sem_shape, copies = _exchange_parts(pbs_mix)
    state, token = _split_start("exchange_mix_start", pbs_mix, lands, sem_shape, copies)
    u_gate, u_up, u_down = _adamw(
        "adamw_ffn", [(tr(w_ffn_gate), g_gate, tr(m_w_ffn_gate), tr(v_w_ffn_gate)),
                      (tr(w_ffn_up), g_up, tr(m_w_ffn_up), tr(v_w_ffn_up)),
                      (w_ffn_down[0], g_down, m_w_ffn_down[0], v_w_ffn_down[0])],
        steps=4, after=(token,))
    o_gate = [jnp.transpose(o) for o in [g_gate] + list(u_gate)]
    o_up = [jnp.transpose(o) for o in [g_up] + list(u_up)]
    o_down = [g_down] + list(u_down)
    o_ada = _adamw_ada(place, cact, dmod, w_ada[0], m_w_ada[0], v_w_ada[0], after=(token,))
    rc_mix = _split_wait("exchange_mix_wait", state, len(pbs_mix), sem_shape, copies, after=(o_ada[1], u_down[0]))
    mix_fulls, _ = _sum_partials("sum_mix", place, owns_mix, rc_mix)
    g_in, g_pw, g_out = _comm_only("join_mix", _join_halves(mix_fulls))
    u_in, u_pw, u_out = _adamw(
        "adamw_mix", [(w_in[0], g_in, m_w_in[0], v_w_in[0]), (w_conv_pw[0], g_pw, m_w_conv_pw[0], v_w_conv_pw[0]),
                      (w_out[0], g_out, m_w_out[0], v_w_out[0])], steps=4)
    o_in, o_pw, o_out = [g_in] + list(u_in), [g_pw] + list(u_pw), [g_out] + list(u_out)

    per_weight = [lead(o_ada), o_bada, o_g1, lead(o_in), lead(o_dww), o_dwb, o_lng, o_lnb, lead(o_pw), lead(o_wg),
                  o_ps, lead(o_out), o_g2, lead(o_gate), lead(o_up), lead(o_down), o_gf]
    result = [loss.reshape(()), gx[None]]
    for kind in range(4):
        result += [o[kind] for o in per_weight]
    return tuple(result)
```

```python
import functools

import jax
import jax.numpy as jnp
from jax import lax
from jax.experimental import pallas as pl
from jax.experimental.pallas import tpu as pltpu

F32 = jnp.float32
MXU_DTYPE = jnp.bfloat16
EPS = 1e-6

D_MODEL = 1024
CONV_W = 512
POOL_W = 512
CONV_K = 31
POOL_WINDOWS = (2, 4, 8, 16)
POOL_G = 128
IN_W = 2 * CONV_W + POOL_W
N_CHIP = 4
N_DEV = 8
CONV_HALO = 32
POOL_HALO = 16

ADAM_LR = 0.001
ADAM_B1 = 0.9
ADAM_B2 = 0.999
ADAM_EPS = 1e-08
ADAM_WD = 0.01
ADAM_STEP = 10

MESH = pl.DeviceIdType.MESH
ANY = pl.BlockSpec(memory_space=pl.ANY)
VMEM = pl.BlockSpec(memory_space=pltpu.VMEM)


def _dot(a, b):
    return jnp.dot(a.astype(MXU_DTYPE), b.astype(MXU_DTYPE), preferred_element_type=F32)


def _dot_nt(a, b):
    return lax.dot_general(a.astype(MXU_DTYPE), b.astype(MXU_DTYPE), (((1,), (1,)), ((), ())),
                           preferred_element_type=F32)


def _dot_tn(a, b):
    return lax.dot_general(a.astype(MXU_DTYPE), b.astype(MXU_DTYPE), (((0,), (0,)), ((), ())),
                           preferred_element_type=F32)


def _sigmoid(v):
    return 1.0 / (1.0 + jnp.exp(-v))


def _full(shape):
    n = len(shape)
    return pl.BlockSpec(shape, lambda *_: (0,) * n)


def _token_tile(s):
    return 256 if s % 256 == 0 else s


SUBLANES = 8


def _row_shifts(pad_ref, shifted_ref, rows):
    for r in range(1, SUBLANES):
        shifted_ref[r - 1] = pad_ref[r:r + rows, :]

    def window(i, n):
        r, base = i % SUBLANES, i - i % SUBLANES
        if r == 0:
            return pad_ref[base:base + n, :]
        return shifted_ref[r - 1, base:base + n, :]

    return window


def _place():
    return lax.axis_index("x"), lax.axis_index("y"), lax.axis_index("c")


def _flip(x, y, r):
    return ((1 - x) if r & 2 else x, (1 - y) if r & 1 else y)


def _remote(src, dst, send_sem, recv_sem, dev):
    return pltpu.make_async_remote_copy(src_ref=src, dst_ref=dst, send_sem=send_sem, recv_sem=recv_sem,
                                        device_id=dev, device_id_type=MESH)


class _Comm:
    def __init__(self, ins, outs, aliases, scratch, start, finish, mid=None):
        self.ins, self.outs, self.aliases, self.scratch = list(ins), list(outs), dict(aliases), list(scratch)
        self.start, self.finish = start, finish
        self.mid = mid


def _both(a, b):
    na, nao, nas = len(a.ins), len(a.outs), len(a.scratch)
    aliases = dict(a.aliases)
    aliases.update({na + i: nao + o for i, o in b.aliases.items()})

    def start(ins, outs, scr):
        a.start(ins[:na], outs[:nao], scr[:nas])
        b.start(ins[na:], outs[nao:], scr[nas:])

    def finish(ins, outs, scr):
        a.finish(ins[:na], outs[:nao], scr[:nas])
        b.finish(ins[na:], outs[nao:], scr[nas:])

    def mid(ins, outs, scr):
        if a.mid:
            a.mid(ins[:na], outs[:nao], scr[:nas])
        if b.mid:
            b.mid(ins[na:], outs[nao:], scr[nas:])

    return _Comm(a.ins + b.ins, a.outs + b.outs, aliases, a.scratch + b.scratch, start, finish,
                 mid if (a.mid or b.mid) else None)


def _call(body, *, name, grid, in_specs, out_specs, out_shape, args, scratch_shapes=(), prefetch=(), comm=None,
          body_starts=False, after=()):
    in_specs = list(in_specs) + [ANY] * len(after)
    args = list(args) + list(after)
    n_pre, n_in, n_out, n_scr = len(prefetch), len(in_specs), len(out_specs), len(scratch_shapes)
    n_body_in = n_in - len(after)
    c_ins = comm.ins if comm else []
    c_outs = comm.outs if comm else []
    c_scr = comm.scratch if comm else []
    last = grid[0] - 1

    def wrapped(*refs):
        pre, refs = refs[:n_pre], refs[n_pre:]
        ins, cin = refs[:n_body_in], refs[n_in:n_in + len(c_ins)]
        refs = refs[n_in + len(c_ins):]
        outs, cout = refs[:n_out], refs[n_out:n_out + len(c_outs)]
        refs = refs[n_out + len(c_outs):]
        scr, cscr = refs[:n_scr], refs[n_scr:]
        step = pl.program_id(0)
        if comm and not body_starts:
            @pl.when(step == 0)
            def _():
                comm.start(cin, cout, cscr)

        has_mid = comm is not None and comm.mid is not None
        mid_step = grid[0] // 2 if grid[0] >= 4 else None
        if has_mid and mid_step is not None:
            @pl.when(step == mid_step)
            def _():
                comm.mid(cin, cout, cscr)

        if body_starts:
            body(lambda: comm.start(cin, cout, cscr) if comm else None, *pre, *ins, *outs, *scr)
        else:
            body(*pre, *ins, *outs, *scr)
        if comm:
            @pl.when(step == last)
            def _():
                if has_mid and mid_step is None:
                    comm.mid(cin, cout, cscr)
                comm.finish(cin, cout, cscr)

    aliases = {n_pre + n_in + a: n_out + b for a, b in (comm.aliases if comm else {}).items()}
    res = pl.pallas_call(
        wrapped, name=name,
        grid_spec=pltpu.PrefetchScalarGridSpec(
            num_scalar_prefetch=n_pre, grid=grid, in_specs=list(in_specs) + [ANY] * len(c_ins),
            out_specs=list(out_specs) + [ANY] * len(c_outs), scratch_shapes=list(scratch_shapes) + list(c_scr)),
        out_shape=list(out_shape) + list(c_outs),
        input_output_aliases=aliases,
        compiler_params=pltpu.CompilerParams(dimension_semantics=("arbitrary",)),
    )(*prefetch, *args, *c_ins)
    return res[:n_out], res[n_out:]


def _comm_only(name, comm):
    return _call(lambda: None, name=name, grid=(1,), in_specs=[], out_specs=[], out_shape=[], args=[], comm=comm)[1]


def _weights_gather(bufs, split):
    n = len(bufs)

    def ctx(outs):
        x, y, cc = _place()
        chips = dict(me=2 * x + y, y=2 * x + (1 - y), x=2 * (1 - x) + y, d=2 * (1 - x) + (1 - y))
        devs = dict(y=(x, 1 - y, cc), x=(1 - x, y, cc), d=(1 - x, 1 - y, cc), s=(x, y, 1 - cc))

        def piece(a, kj, pc, q=None):
            if not split[a]:
                return outs[a].at[kj]
            h = bufs[a].shape[1] // 2
            if q is None:
                return outs[a].at[kj, pl.ds(pc * h, h), :]
            return outs[a].at[kj, pl.ds(pc * h + q * (h // 2), h // 2), :]

        return cc, chips, devs, piece

    def directs(a, outs, send, recv):
        cc, chips, devs, piece = ctx(outs)
        if not split[a]:
            whole = piece(a, chips["me"], cc)
            return [_remote(whole, whole, send.at[a, k], recv.at[a, k], devs[t]) for k, t in ((0, "y"), (2, "x"), (4, "d"))]
        q = lambda i: piece(a, chips["me"], cc, i)
        return [_remote(q(0), q(0), send.at[a, 0], recv.at[a, 0], devs["y"]),
                _remote(q(1), q(1), send.at[a, 3], recv.at[a, 3], devs["x"]),
                _remote(q(1), q(1), send.at[a, 1], recv.at[a, 1], devs["y"]),
                _remote(q(0), q(0), send.at[a, 2], recv.at[a, 2], devs["x"])]

    def landed(a, k, outs, send, recv):
        cc, chips, devs, piece = ctx(outs)
        if not split[a]:
            got = piece(a, chips[{0: "y", 2: "x", 4: "d"}[k]], cc)
        elif k < 6:
            got = piece(a, chips[("y", "y", "x", "x", "d", "d")[k]], cc, (0, 1, 0, 1, 0, 1)[k])
        else:
            got = piece(a, chips[("y", "x", "d")[k - 6]], 1 - cc)
        return _remote(got, got, send.at[a, k], recv.at[a, k], devs["s"])

    def passed_on(a, outs, send, recv):
        cc, chips, devs, piece = ctx(outs)
        from_y, from_x = piece(a, chips["y"], cc, 0), piece(a, chips["x"], cc, 1)
        return [_remote(from_y, from_y, send.at[a, 4], recv.at[a, 4], devs["x"]),
                _remote(from_x, from_x, send.at[a, 5], recv.at[a, 5], devs["y"])]

    def to_sibling(a, outs, send, recv, which=(0, 1, 2)):
        cc, chips, devs, piece = ctx(outs)
        halves = [piece(a, chips[("y", "x", "d")[i]], cc) for i in which]
        return [_remote(hf, hf, send.at[a, 6 + i], recv.at[a, 6 + i], devs["s"]) for i, hf in zip(which, halves)]

    def start(ins, outs, scr):
        send, recv = scr
        per_item = [directs(a, outs, send, recv) for a in range(n)]
        for rank in range(4):
            for cps in per_item:
                if rank < len(cps):
                    cps[rank].start()

    def mid(ins, outs, scr):
        send, recv = scr
        for a in range(n):
            if split[a]:
                fy, fx = passed_on(a, outs, send, recv)
                landed(a, 0, outs, send, recv).wait_recv()
                fy.start()
                landed(a, 3, outs, send, recv).wait_recv()
                fx.start()

    def finish(ins, outs, scr):
        send, recv = scr
        for a in range(n):
            if split[a]:
                to_y, to_x = to_sibling(a, outs, send, recv, which=(0, 1))
                landed(a, 1, outs, send, recv).wait_recv()
                to_y.start()
                landed(a, 2, outs, send, recv).wait_recv()
                to_x.start()
        for a in range(n):
            if split[a]:
                for k in (4, 5):
                    landed(a, k, outs, send, recv).wait_recv()
                to_sibling(a, outs, send, recv, which=(2,))[0].start()
            else:
                for k in (0, 2, 4):
                    landed(a, k, outs, send, recv).wait_recv()
        for a in range(n):
            if split[a]:
                for k in (6, 7, 8):
                    landed(a, k, outs, send, recv).wait_recv()
            cps = directs(a, outs, send, recv)
            if split[a]:
                cps += passed_on(a, outs, send, recv) + to_sibling(a, outs, send, recv)
            for cp in cps:
                cp.wait_send()

    return _Comm(bufs, [jax.ShapeDtypeStruct(b.shape, b.dtype) for b in bufs], {i: i for i in range(n)},
                 [pltpu.SemaphoreType.DMA((n, 9)), pltpu.SemaphoreType.DMA((n, 9))], start, finish, mid)


HBM =pl.BlockSpec(memory_space=pltpu.HBM)
SEM = pl.BlockSpec(memory_space=pltpu.SEMAPHORE)
DATAFLOW = pltpu.SideEffectType.DATAFLOW_SIDE_EFFECTING


class _SemGrid:
    def __init__(self, refs, cols):
        self.refs, self.cols = refs, cols

    @property
    def at(self):
        return self

    def __getitem__(self, idx):
        return self.refs[idx[0] * self.cols + idx[1]]


def _split_start(name, srcs, lands, sem_shape, copies, zeroed=False):
    n, k = len(srcs), len(lands)
    ns = sem_shape[0] * sem_shape[1]

    def body(*refs):
        src_refs, land_refs = refs[:n], refs[n:n + k]
        send = _SemGrid(refs[n + k:n + k + ns], sem_shape[1])
        recv = _SemGrid(refs[n + k + ns:n + k + 2 * ns], sem_shape[1])
        token = refs[-1]
        for cp in copies(src_refs, land_refs, send, recv):
            cp.start()
        token[...] = jnp.zeros(token.shape, F32)

    hbm = lambda a: pltpu.with_memory_space_constraint(a, pltpu.HBM)
    zones = [jnp.zeros(l.shape, l.dtype) if zeroed else lax.empty(l.shape, l.dtype) for l in lands]
    out = pl.pallas_call(
        body, name=name,
        out_shape=[pltpu.SemaphoreType.DMA(())] * (2 * ns)
        + [pltpu.HBM(a.shape, a.dtype) for a in list(srcs) + list(lands)] + [jax.ShapeDtypeStruct((8, 128), F32)],
        in_specs=[HBM] * (n + k), out_specs=[SEM] * (2 * ns) + [HBM] * (n + k) + [VMEM],
        input_output_aliases={i: 2 * ns + i for i in range(n + k)},
        compiler_params=pltpu.CompilerParams(has_side_effects=DATAFLOW),
    )(*[hbm(a) for a in srcs], *[hbm(z) for z in zones])
    return out[:-1], out[-1]


def _split_wait(name, state, n, sem_shape, copies, after):
    ns = sem_shape[0] * sem_shape[1]
    sems, bufs = state[:2 * ns], state[2 * ns:]
    k = len(bufs) - n

    def body(*refs):
        src_refs, land_refs = refs[:n], refs[n:n + k]
        send = _SemGrid(refs[n + k:n + k + ns], sem_shape[1])
        recv = _SemGrid(refs[n + k + ns:n + k + 2 * ns], sem_shape[1])
        cps = copies(src_refs, land_refs, send, recv)
        for cp in cps:
            cp.wait_send()
        for cp in cps:
            cp.wait_recv()

    out = pl.pallas_call(
        body, name=name,
        out_shape=[pltpu.HBM(a.shape, a.dtype) for a in bufs],
        in_specs=[HBM] * (n + k) + [SEM] * (2 * ns) + [ANY] * len(after), out_specs=[HBM] * (n + k),
        input_output_aliases={i: i for i in range(n + k)},
        compiler_params=pltpu.CompilerParams(has_side_effects=DATAFLOW),
    )(*bufs, *sems, *after)
    return out[n:]


def _direct_phases(copies):
    def start(ins, outs, scr):
        for cp in copies(ins, outs, *scr):
            cp.start()

    def finish(ins, outs, scr):
        cps = copies(ins, outs, *scr)
        for cp in cps:
            cp.wait_recv()
        for cp in cps:
            cp.wait_send()

    return start, finish


def _sibling_halves(gs):
    n = len(gs)

    def copies(ins, outs, send, recv):
        x, y, cc = _place()
        cps = []
        for a in range(n):
            h = gs[a].shape[1] // 2
            cps.append(_remote(ins[a].at[:, pl.ds((1 - cc) * h, h), :], outs[a], send.at[a], recv.at[a],
                               (x, y, 1 - cc)))
        return cps

    start, finish = _direct_phases(copies)
    return _Comm(gs, [jax.ShapeDtypeStruct((N_CHIP, g.shape[1] // 2, g.shape[2]), F32) for g in gs], {},
                 [pltpu.SemaphoreType.DMA((n,)), pltpu.SemaphoreType.DMA((n,))], start, finish)


def _exchange_parts(pbs):
    n = len(pbs)

    def copies(ins, outs, send, recv):
        x, y, cc = _place()
        cps = []
        for a in range(n):
            for r in range(1, N_CHIP):
                kx, ky = _flip(x, y, r)
                cps.append(_remote(ins[a].at[2 * kx + ky], outs[a].at[r - 1], send.at[a, r - 1], recv.at[a, r - 1],
                                   (kx, ky, cc)))
        return cps

    lands = [jax.ShapeDtypeStruct((N_CHIP - 1,) + p.shape[1:], p.dtype) for p in pbs]
    return lands, (n, N_CHIP - 1), copies


def _small_parts(arrs):
    n = len(arrs)

    def copies(ins, outs, send, recv):
        x, y, cc = _place()
        b = 4 * x + 2 * y + cc
        cps = []
        for a in range(n):
            for r in range(1, N_DEV):
                dev = ((1 - x) if r & 4 else x, (1 - y) if r & 2 else y, (1 - cc) if r & 1 else cc)
                cps.append(_remote(ins[a], outs[a].at[b], send.at[a, r - 1], recv.at[a, r - 1], dev))
        return cps

    lands = [jax.ShapeDtypeStruct((N_DEV,) + a.shape, a.dtype) for a in arrs]
    return lands, (n, N_DEV - 1), copies


def _exchange_partials(pbs):
    lands, sem_shape, copies = _exchange_parts(pbs)
    start, finish = _direct_phases(copies)
    return _Comm(pbs, lands, {}, [pltpu.SemaphoreType.DMA(sem_shape), pltpu.SemaphoreType.DMA(sem_shape)],
                 start, finish)


def _join_halves(fulls):
    n = len(fulls)

    def copies(ins, outs, send, recv):
        x, y, cc = _place()
        cps = []
        for a in range(n):
            h = fulls[a].shape[0] // 2
            mine = outs[a].at[pl.ds(cc * h, h), :]
            cps.append(_remote(mine, mine, send.at[a], recv.at[a], (x, y, 1 - cc)))
        return cps

    start, finish = _direct_phases(copies)
    return _Comm(fulls, [jax.ShapeDtypeStruct(f.shape, F32) for f in fulls], {i: i for i in range(n)},
                 [pltpu.SemaphoreType.DMA((n,)), pltpu.SemaphoreType.DMA((n,))], start, finish)


def _cast_weights(place, shards, dww, wg):
    n = len(shards)

    def body(pref, *refs):
        ins, outs = refs[:n + 2], refs[n + 2:]
        for a in range(n):
            outs[a][0] = ins[a][...].astype(MXU_DTYPE)
        outs[n][0] = ins[n][...]
        outs[n + 1][...] = ins[n + 1][...].astype(MXU_DTYPE)

    full = lambda a: pl.BlockSpec(a.shape, lambda i, pref: (0,) * a.ndim)
    slot = lambda a: pl.BlockSpec((1,) + a.shape, lambda i, pref: (pref[1],) + (0,) * a.ndim)
    arrs = list(shards) + [dww, wg]
    return pl.pallas_call(
        body, name="cast_weights",
        grid_spec=pltpu.PrefetchScalarGridSpec(
            num_scalar_prefetch=1, grid=(1,), in_specs=[full(a) for a in arrs],
            out_specs=[slot(a) for a in arrs[:n + 1]] + [full(wg)]),
        out_shape=[jax.ShapeDtypeStruct((N_CHIP,) + a.shape, MXU_DTYPE) for a in shards]
        + [jax.ShapeDtypeStruct((N_CHIP,) + dww.shape, F32), jax.ShapeDtypeStruct(wg.shape, MXU_DTYPE)],
        compiler_params=pltpu.CompilerParams(dimension_semantics=("arbitrary",)),
    )(place, *arrs)


def _mixer_fwd(x, mod, g1, w_in, dww, dwb, lng, lnb, w_pw, wg, pscale, w_out, comm=None):
    s = x.shape[0]
    ts = _token_tile(s)
    nt = s // ts

    def body(x_ref, mod_ref, g1_ref, win_ref, dww_ref, dwb_ref, lng_ref, lnb_ref, wpw_ref, wg_ref, ps_ref,
             wout_ref, x2_ref, y_ref, u_ref, z_ref, rstd_ref, p_ref, ycat_ref, gpad, vpad, gshift):
        i = pl.program_id(0)

        @pl.when(i == 0)
        def _():
            gpad[0:CONV_HALO, :] = jnp.zeros((CONV_HALO, CONV_W), F32)
            vpad[0:POOL_HALO, :] = jnp.zeros((POOL_HALO, POOL_W), F32)

        xt = x_ref[...]
        sh1 = mod_ref[0:1, :]
        sc1 = mod_ref[1:2, :]
        gt1 = mod_ref[2:3, :]
        r1 = lax.rsqrt(jnp.mean(xt * xt, axis=-1, keepdims=True) + EPS)
        h1 = (xt * r1 * g1_ref[...]) * (1.0 + sc1) + sh1
        h1b = h1.astype(MXU_DTYPE)
        u = jnp.concatenate([_dot(h1b, win_ref[j]) for j in range(N_CHIP)], axis=1)
        u_ref[...] = u
        a = u[:, :CONV_W]
        g = u[:, CONV_W:2 * CONV_W]
        v = u[:, 2 * CONV_W:]

        gpad[CONV_HALO:CONV_HALO + ts, :] = a * _sigmoid(g)
        window = _row_shifts(gpad, gshift, ts + CONV_HALO - SUBLANES)
        cv = jnp.broadcast_to(dwb_ref[...], (ts, CONV_W))
        off = CONV_HALO - (CONV_K - 1)
        for k in range(CONV_K):
            cv = cv + dww_ref[k:k + 1, :] * window(off + k, ts)
        gpad[0:CONV_HALO, :] = gpad[ts:ts + CONV_HALO, :]

        mu = jnp.mean(cv, axis=-1, keepdims=True)
        cc = cv - mu
        rstd = lax.rsqrt(jnp.mean(cc * cc, axis=-1, keepdims=True) + EPS)
        z = cc * rstd
        z_ref[...] = z
        rstd_ref[...] = rstd
        ln = z * lng_ref[...] + lnb_ref[...]
        sw = ln * _sigmoid(ln)
        yconv = _dot(sw, wpw_ref[...])

        vpad[POOL_HALO:POOL_HALO + ts, :] = v
        t = i * ts + lax.broadcasted_iota(jnp.int32, (ts, 1), 0)
        ps, ypool = [], []
        for gi, w in enumerate(POOL_WINDOWS):
            cols = slice(gi * POOL_G, (gi + 1) * POOL_G)
            acc = vpad[POOL_HALO:POOL_HALO + ts, cols]
            for d in range(1, w):
                acc = acc + vpad[POOL_HALO - d:POOL_HALO - d + ts, cols]
            cnt = jnp.minimum(t + 1, w).astype(F32)
            pg = (acc / cnt - v[:, cols]).astype(MXU_DTYPE)
            ps.append(pg)
            ypool.append(_dot(pg, wg_ref[gi]))
        vpad[0:POOL_HALO, :] = vpad[ts:ts + POOL_HALO, :]
        p_ref[...] = jnp.concatenate(ps, axis=1)
        ypool = jnp.concatenate(ypool, axis=1) * ps_ref[...]

        ycat = jnp.concatenate([yconv, ypool], axis=1).astype(MXU_DTYPE)
        ycat_ref[...] = ycat
        y = _dot(ycat, wout_ref[...])
        y_ref[...] = y
        x2_ref[...] = xt + gt1 * y

    tile = lambda w: pl.BlockSpec((ts, w), lambda i: (i, 0))
    return _call(
        body, name="mixer_fwd", grid=(nt,),
        in_specs=[tile(D_MODEL), _full(mod.shape), _full(g1.shape), _full(w_in.shape), _full(dww.shape),
                  _full(dwb.shape), _full(lng.shape), _full(lnb.shape), _full(w_pw.shape), _full(wg.shape),
                  _full(pscale.shape), _full(w_out.shape)],
        out_specs=[tile(D_MODEL), tile(D_MODEL), tile(IN_W), tile(CONV_W), tile(1), tile(POOL_W), tile(D_MODEL)],
        out_shape=[jax.ShapeDtypeStruct((s, D_MODEL), F32), jax.ShapeDtypeStruct((s, D_MODEL), F32),
                   jax.ShapeDtypeStruct((s, IN_W), F32), jax.ShapeDtypeStruct((s, CONV_W), F32),
                   jax.ShapeDtypeStruct((s, 1), F32), jax.ShapeDtypeStruct((s, POOL_W), MXU_DTYPE),
                   jax.ShapeDtypeStruct((s, D_MODEL), MXU_DTYPE)],
        scratch_shapes=[pltpu.VMEM((ts + CONV_HALO, CONV_W), F32), pltpu.VMEM((ts + POOL_HALO, POOL_W), F32),
                        pltpu.VMEM((SUBLANES - 1, ts + CONV_HALO - SUBLANES, CONV_W), F32)],
        args=(x, mod, g1, w_in, dww, dwb, lng, lnb, w_pw, wg, pscale, w_out), comm=comm)


def _ffn(x2, tgt, mod, g2, gf, w_gate, w_up, w_down):
    s = x2.shape[0]
    ts = _token_tile(s)
    nt = s // ts
    fb = w_gate.shape[1]

    def body(x2_ref, tgt_ref, mod_ref, g2_ref, gf_ref, wgt_ref, wup_ref, wdn_ref,
             dx2_ref, h2_ref, df_ref, act_ref, dgg_ref, duu_ref, vec_ref, gg_s, uu_s):
        i = pl.program_id(0)

        @pl.when(i == 0)
        def _():
            vec_ref[...] = jnp.zeros(vec_ref.shape, F32)

        x2t = x2_ref[...]
        sh2 = mod_ref[3:4, :]
        sc2 = mod_ref[4:5, :]
        gt2 = mod_ref[5:6, :]
        g2v = g2_ref[...]
        gfv = gf_ref[...]
        r2 = lax.rsqrt(jnp.mean(x2t * x2t, axis=-1, keepdims=True) + EPS)
        xh2 = x2t * r2
        n2 = xh2 * g2v
        h2b = (n2 * (1.0 + sc2) + sh2).astype(MXU_DTYPE)
        h2_ref[...] = h2b
        f = jnp.zeros((ts, D_MODEL), F32)
        for j in range(N_CHIP):
            gg = _dot_nt(h2b, wgt_ref[j])
            uu = _dot_nt(h2b, wup_ref[j])
            gg_s[j] = gg
            uu_s[j] = uu
            actb = (gg * _sigmoid(gg) * uu).astype(MXU_DTYPE)
            act_ref[j] = actb
            f = f + _dot(actb, wdn_ref[j])
        x3 = x2t + gt2 * f
        r3 = lax.rsqrt(jnp.mean(x3 * x3, axis=-1, keepdims=True) + EPS)
        xh3 = x3 * r3
        diff = xh3 * gfv - tgt_ref[...]
        dout = diff * (1.0 / D_MODEL)
        dn3 = dout * gfv
        dx3 = r3 * (dn3 - xh3 * jnp.mean(dn3 * xh3, axis=-1, keepdims=True))
        dfb = (dx3 * gt2).astype(MXU_DTYPE)
        df_ref[...] = dfb
        dh2 = jnp.zeros((ts, D_MODEL), F32)
        for j in range(N_CHIP):
            dact = _dot_nt(dfb, wdn_ref[j])
            gg = gg_s[j]
            uu = uu_s[j]
            sg = _sigmoid(gg)
            duu = (dact * (gg * sg)).astype(MXU_DTYPE)
            dgg = (dact * uu * (sg * (1.0 + gg * (1.0 - sg)))).astype(MXU_DTYPE)
            duu_ref[j] = duu
            dgg_ref[j] = dgg
            dh2 = dh2 + _dot(dgg, wgt_ref[j]) + _dot(duu, wup_ref[j])
        dn2 = dh2 * (1.0 + sc2)
        dxh2 = dn2 * g2v
        dx2_ref[...] = dx3 + r2 * (dxh2 - xh2 * jnp.mean(dxh2 * xh2, axis=-1, keepdims=True))

        col = lambda a: jnp.sum(a, axis=0, keepdims=True)
        vec_ref[0:1, :] += col(dout * xh3)
        vec_ref[1:2, :] += col(dx3 * f)
        vec_ref[2:3, :] += col(dh2)
        vec_ref[3:4, :] += col(dh2 * n2)
        vec_ref[4:5, :] += col(dn2 * xh2)
        vec_ref[5:6, :] += col(diff * diff)

    tile = lambda w: pl.BlockSpec((ts, w), lambda i: (i, 0))
    tile3 = pl.BlockSpec((N_CHIP, ts, fb), lambda i: (0, i, 0))
    once = lambda a: pl.BlockSpec(a.shape, lambda i: (0,) * a.ndim, pipeline_mode=pl.Buffered(1))
    hid = jax.ShapeDtypeStruct((N_CHIP, s, fb), MXU_DTYPE)
    return pl.pallas_call(
        body, name="ffn", grid=(nt,),
        in_specs=[tile(D_MODEL), tile(D_MODEL), _full(mod.shape), _full(g2.shape), _full(gf.shape),
                  once(w_gate), once(w_up), once(w_down)],
        out_specs=[tile(D_MODEL), tile(D_MODEL), tile(D_MODEL), tile3, tile3, tile3, _full((8, D_MODEL))],
        out_shape=[jax.ShapeDtypeStruct((s, D_MODEL), F32), jax.ShapeDtypeStruct((s, D_MODEL), MXU_DTYPE),
                   jax.ShapeDtypeStruct((s, D_MODEL), MXU_DTYPE), hid, hid, hid,
                   jax.ShapeDtypeStruct((8, D_MODEL), F32)],
        scratch_shapes=[pltpu.VMEM((N_CHIP, ts, fb), F32), pltpu.VMEM((N_CHIP, ts, fb), F32)],
        compiler_params=pltpu.CompilerParams(dimension_semantics=("arbitrary",)),
    )(x2, tgt, mod, g2, gf, w_gate, w_up, w_down)


def _mixer_bwd(dx2, x, y, u, z, rstd, p, mod, g1, w_in, dww, lng, lnb, w_pw, wg, pscale, w_out, comm=None, after=()):
    s = x.shape[0]
    ts = _token_tile(s)
    nt = s // ts

    def body(dx2_ref, x_ref, y_ref, u_ref, z_ref, rstd_ref, p_ref, mod_ref, g1_ref, win_ref, dww_ref, lng_ref,
             lnb_ref, wpw_ref, wg_ref, ps_ref, wout_ref,
             gx_ref, h1_ref, du_ref, dy_ref, sw_ref, dyc_ref, dyp_ref, vd_ref, vc_ref, ddw_ref, dcpad, dppad,
             dshift):
        i = pl.program_id(0)
        tix = nt - 1 - i

        @pl.when(i == 0)
        def _():
            vd_ref[...] = jnp.zeros(vd_ref.shape, F32)
            vc_ref[...] = jnp.zeros(vc_ref.shape, F32)
            ddw_ref[...] = jnp.zeros(ddw_ref.shape, F32)
            dcpad[ts:ts + CONV_HALO, :] = jnp.zeros((CONV_HALO, CONV_W), F32)
            dppad[ts:ts + POOL_HALO, :] = jnp.zeros((POOL_HALO, POOL_W), F32)

        col = lambda a: jnp.sum(a, axis=0, keepdims=True)
        sh1 = mod_ref[0:1, :]
        sc1 = mod_ref[1:2, :]
        gt1 = mod_ref[2:3, :]
        dx2t = dx2_ref[...]
        vd_ref[0:1, :] += col(dx2t * y_ref[...])
        dyb = (dx2t * gt1).astype(MXU_DTYPE)
        dy_ref[...] = dyb
        dycat = _dot_nt(dyb, wout_ref[...])
        dyconv = dycat[:, :CONV_W]
        dypool = dycat[:, CONV_W:]

        pt = p_ref[...]
        t = tix * ts + lax.broadcasted_iota(jnp.int32, (ts, 1), 0)
        psc = ps_ref[...]
        dypb = (dypool * psc).astype(MXU_DTYPE)
        dyp_ref[...] = dypb
        dps, ypre = [], []
        for gi, w in enumerate(POOL_WINDOWS):
            cols = slice(gi * POOL_G, (gi + 1) * POOL_G)
            ypre.append(_dot(pt[:, cols], wg_ref[gi]))
            dpg = _dot_nt(dypb[:, cols], wg_ref[gi])
            dps.append(dpg)
            cnt = jnp.minimum(t + 1, w).astype(F32)
            dppad[0:ts, cols] = dpg / cnt
        vc_ref[0:1, :] += col(dypool * jnp.concatenate(ypre, axis=1))
        dvs = []
        for gi, w in enumerate(POOL_WINDOWS):
            cols = slice(gi * POOL_G, (gi + 1) * POOL_G)
            acc = dppad[0:ts, cols]
            for d in range(1, w):
                acc = acc + dppad[d:d + ts, cols]
            dvs.append(acc - dps[gi])
        dv = jnp.concatenate(dvs, axis=1)
        dppad[ts:ts + POOL_HALO, :] = dppad[0:POOL_HALO, :]

        zt = z_ref[...]
        lngv = lng_ref[...]
        ln = zt * lngv + lnb_ref[...]
        sg = _sigmoid(ln)
        swb = (ln * sg).astype(MXU_DTYPE)
        sw_ref[...] = swb
        dycb = dyconv.astype(MXU_DTYPE)
        dyc_ref[...] = dycb
        dln = _dot_nt(dycb, wpw_ref[...]) * (sg * (1.0 + ln * (1.0 - sg)))
        vc_ref[1:2, :] += col(dln * zt)
        vc_ref[2:3, :] += col(dln)
        dz = dln * lngv
        dcv = rstd_ref[...] * (dz - jnp.mean(dz, axis=-1, keepdims=True)
                               - zt * jnp.mean(dz * zt, axis=-1, keepdims=True))
        vc_ref[3:4, :] += col(dcv)
        dcpad[0:ts, :] = dcv
        ut = u_ref[...]
        a = ut[:, :CONV_W]
        g = ut[:, CONV_W:2 * CONV_W]
        sgg = _sigmoid(g)
        glu = a * sgg
        window = _row_shifts(dcpad, dshift, ts + CONV_HALO - SUBLANES)
        dglu = jnp.zeros((ts, CONV_W), F32)
        for k in range(CONV_K):
            sh = window(CONV_K - 1 - k, ts)
            dglu = dglu + dww_ref[k:k + 1, :] * sh
            ddw_ref[k:k + 1, :] += col(glu * sh)
        dcpad[ts:ts + CONV_HALO, :] = dcpad[0:CONV_HALO, :]
        da = dglu * sgg
        dg = dglu * a * sgg * (1.0 - sgg)
        dub = jnp.concatenate([da, dg, dv], axis=1).astype(MXU_DTYPE)
        du_ref[...] = dub
        cw = IN_W // N_CHIP
        dh1 = jnp.zeros((ts, D_MODEL), F32)
        for j in range(N_CHIP):
            dh1 = dh1 + _dot_nt(dub[:, j * cw:(j + 1) * cw], win_ref[j])

        xt = x_ref[...]
        g1v = g1_ref[...]
        r1 = lax.rsqrt(jnp.mean(xt * xt, axis=-1, keepdims=True) + EPS)
        xh1 = xt * r1
        n1 = xh1 * g1v
        h1_ref[...] = (n1 * (1.0 + sc1) + sh1).astype(MXU_DTYPE)
        vd_ref[1:2, :] += col(dh1)
        vd_ref[2:3, :] += col(dh1 * n1)
        dn1 = dh1 * (1.0 + sc1)
        vd_ref[3:4, :] += col(dn1 * xh1)
        dxh = dn1 * g1v
        gx_ref[...] = dx2t + r1 * (dxh - xh1 * jnp.mean(dxh * xh1, axis=-1, keepdims=True))

    tile = lambda w: pl.BlockSpec((ts, w), lambda i: (nt - 1 - i, 0))
    bf = lambda w: jax.ShapeDtypeStruct((s, w), MXU_DTYPE)
    return _call(
        body, name="mixer_bwd", grid=(nt,),
        in_specs=[tile(D_MODEL), tile(D_MODEL), tile(D_MODEL), tile(IN_W), tile(CONV_W), tile(1), tile(POOL_W),
                  _full(mod.shape), _full(g1.shape), _full(w_in.shape), _full(dww.shape), _full(lng.shape),
                  _full(lnb.shape), _full(w_pw.shape), _full(wg.shape), _full(pscale.shape), _full(w_out.shape)],
        out_specs=[tile(D_MODEL), tile(D_MODEL), tile(IN_W), tile(D_MODEL), tile(CONV_W), tile(CONV_W),
                   tile(POOL_W), _full((8, D_MODEL)), _full((8, CONV_W)), _full((32, CONV_W))],
        out_shape=[jax.ShapeDtypeStruct((s, D_MODEL), F32), bf(D_MODEL), bf(IN_W), bf(D_MODEL), bf(CONV_W),
                   bf(CONV_W), bf(POOL_W), jax.ShapeDtypeStruct((8, D_MODEL), F32),
                   jax.ShapeDtypeStruct((8, CONV_W), F32), jax.ShapeDtypeStruct((32, CONV_W), F32)],
        scratch_shapes=[pltpu.VMEM((ts + CONV_HALO, CONV_W), F32), pltpu.VMEM((ts + POOL_HALO, POOL_W), F32),
                        pltpu.VMEM((SUBLANES - 1, ts + CONV_HALO - SUBLANES, CONV_W), F32)],
        args=(dx2, x, y, u, z, rstd, p, mod, g1, w_in, dww, lng, lnb, w_pw, wg, pscale, w_out), comm=comm,
        after=after)


def _dw(name, a, a_spec, b, b_spec, nb, mb, nbk, comm=None, after=()):
    def body(a_ref, b_ref, o_ref):
        av = a_ref[...]
        bv = b_ref[...]
        av = av.reshape(av.shape[-2:])
        bv = bv.reshape(bv.shape[-2:])
        o_ref[0] = _dot_tn(av, bv)

    (out,), rest = _call(
        body, name=name, grid=(nb,), in_specs=[a_spec, b_spec],
        out_specs=[pl.BlockSpec((1, mb, nbk), lambda j: (j, 0, 0))],
        out_shape=[jax.ShapeDtypeStruct((nb, mb, nbk), F32)], args=(a, b), comm=comm, after=after)
    return out, rest


def _dw_mixer(p, dyp, ycat, dy, sw, dyc, h1, du):
    s = p.shape[0]
    ng = len(POOL_WINDOWS)
    assert ng == N_CHIP

    def body(p_ref, dyp_ref, ycat_ref, dy_ref, sw_ref, dyc_ref, h1_ref, du_ref, wg_ref, out_ref, pw_ref, in_ref):
        wg_ref[0] = _dot_tn(p_ref[...], dyp_ref[...])
        out_ref[0] = _dot_tn(ycat_ref[...], dy_ref[...])
        pw_ref[0] = _dot_tn(sw_ref[...], dyc_ref[...])
        in_ref[0] = _dot_tn(h1_ref[...], du_ref[...])

    whole = lambda w: pl.BlockSpec((s, w), lambda j: (0, 0))
    cols = lambda w: pl.BlockSpec((s, w), lambda j: (0, j))
    blk = lambda m, n: pl.BlockSpec((1, m, n), lambda j: (j, 0, 0))
    shapes = [(POOL_G, POOL_G), (D_MODEL // N_CHIP, D_MODEL), (CONV_W // N_CHIP, CONV_W), (D_MODEL, IN_W // N_CHIP)]
    res, _ = _call(
        body, name="dw_mixer", grid=(N_CHIP,),
        in_specs=[cols(POOL_G), cols(POOL_G), cols(D_MODEL // N_CHIP), whole(D_MODEL), cols(CONV_W // N_CHIP),
                  whole(CONV_W), whole(D_MODEL), cols(IN_W // N_CHIP)],
        out_specs=[blk(m, n) for m, n in shapes],
        out_shape=[jax.ShapeDtypeStruct((N_CHIP, m, n), F32) for m, n in shapes],
        args=(p, dyp, ycat, dy, sw, dyc, h1, du))
    return res


def _ada_fwd(c, w_ada, b4, comm=None):
    nc = w_ada.shape[1]

    def body(start_comm, c_ref, w_ref, b4_ref, mod_ref, cact_ref, call, part, parts, send1, recv1, send2, recv2):
        x, y, cc = _place()
        b = 4 * x + 2 * y + cc
        j = 2 * x + y
        call[b] = c_ref[...]
        sends = []
        for r in range(1, N_DEV):
            dev = ((1 - x) if r & 4 else x, (1 - y) if r & 2 else y, (1 - cc) if r & 1 else cc)
            cp = _remote(call.at[b], call.at[b], send1.at[r - 1], recv1.at[r - 1], dev)
            cp.start()
            sends.append(cp)
        for r in range(1, N_DEV):
            src_b = lax.bitwise_xor(b, r)
            _remote(call.at[src_b], call.at[src_b], send1.at[r - 1], recv1.at[r - 1], (x, y, cc)).wait_recv()
        for cp in sends:
            cp.wait_send()
        start_comm()
        for i in range(N_DEV):
            ci = call[i]
            cact_ref[i:i + 1, :] = ci * _sigmoid(ci)
        part[...] = jnp.dot(cact_ref[...], w_ref[...], preferred_element_type=F32, precision=lax.Precision.HIGHEST)
        sends = []
        for r in range(1, N_CHIP):
            kx, ky = _flip(x, y, r)
            cp = _remote(part, parts.at[j], send2.at[r - 1], recv2.at[r - 1], (kx, ky, cc))
            cp.start()
            sends.append(cp)
        parts[j] = part[...]
        for r in range(1, N_CHIP):
            kx, ky = _flip(x, y, r)
            kj = 2 * kx + ky
            _remote(part, parts.at[kj], send2.at[r - 1], recv2.at[r - 1], (x, y, cc)).wait_recv()
        for cp in sends:
            cp.wait_send()
        mine = lax.broadcasted_iota(jnp.int32, (N_DEV, 1), 0) == b
        for k in range(N_CHIP):
            row = jnp.sum(jnp.where(mine, parts[k], 0.0), axis=0, keepdims=True)
            mod_ref[k:k + 1, :] = row + b4_ref[k:k + 1, :]

    return _call(
        body, name="ada_fwd", grid=(1,),
        in_specs=[VMEM, VMEM, VMEM], out_specs=[VMEM, VMEM],
        out_shape=[jax.ShapeDtypeStruct((N_CHIP, nc), F32), jax.ShapeDtypeStruct((N_DEV, D_MODEL), F32)],
        scratch_shapes=[pltpu.VMEM((N_DEV, 1, D_MODEL), F32), pltpu.VMEM((N_DEV, nc), F32),
                        pltpu.VMEM((N_CHIP, N_DEV, nc), F32),
                        pltpu.SemaphoreType.DMA((N_DEV - 1,)), pltpu.SemaphoreType.DMA((N_DEV - 1,)),
                        pltpu.SemaphoreType.DMA((N_CHIP - 1,)), pltpu.SemaphoreType.DMA((N_CHIP - 1,))],
        args=(c, w_ada, b4), comm=comm, body_starts=True)


def _chip_partials(name, place, gs, rs, comm=None):
    n = len(gs)

    def body(pref, *refs):
        g_refs, r_refs = refs[:n], refs[n:2 * n]
        pb_refs, own_refs = refs[2 * n:3 * n], refs[3 * n:]
        jj = pl.program_id(0)
        for a in range(n):
            sm = g_refs[a][0] + r_refs[a][0]
            pb_refs[a][0] = sm.astype(MXU_DTYPE)

            @pl.when(jj == pref[1])
            def _(a=a, sm=sm):
                own_refs[a][...] = sm

    halves = [(g.shape[1] // 2, g.shape[2]) for g in gs]
    in_specs = [pl.BlockSpec((1, h, w), lambda jj, pref: (jj, pref[0], 0)) for h, w in halves]
    in_specs += [pl.BlockSpec((1, h, w), lambda jj, pref: (jj, 0, 0)) for h, w in halves]
    out_specs = [pl.BlockSpec((1, h, w), lambda jj, pref: (jj, 0, 0)) for h, w in halves]
    out_specs += [pl.BlockSpec((h, w), lambda jj, pref: (0, 0)) for h, w in halves]
    out, rest = _call(
        body, name=name, grid=(N_CHIP,), in_specs=in_specs, out_specs=out_specs,
        out_shape=[jax.ShapeDtypeStruct((N_CHIP, h, w), MXU_DTYPE) for h, w in halves]
        + [jax.ShapeDtypeStruct((h, w), F32) for h, w in halves],
        args=(*gs, *rs), prefetch=(place,), comm=comm)
    return (out[:n], out[n:]), rest


def _sum_partials(name, place, owns, recvd, comm=None):
    n = len(owns)

    def body(pref, *refs):
        o_refs, r_refs, out_refs = refs[:n], refs[n:2 * n], refs[2 * n:]
        for a in range(n):
            acc = o_refs[a][...]
            for r in range(N_CHIP - 1):
                acc = acc + r_refs[a][r].astype(F32)
            out_refs[a][...] = acc

    full = lambda a: pl.BlockSpec(a.shape, lambda i, pref: (0,) * a.ndim)
    return _call(
        body, name=name, grid=(1,), in_specs=[full(a) for a in list(owns) + list(recvd)],
        out_specs=[pl.BlockSpec(o.shape, lambda i, pref: (pref[0], 0)) for o in owns],
        out_shape=[jax.ShapeDtypeStruct((2 * o.shape[0], o.shape[1]), F32) for o in owns],
        args=(*owns, *recvd), prefetch=(place,), comm=comm)


def _adamw_math(w, g, m, v):
    m = ADAM_B1 * m + (1.0 - ADAM_B1) * g
    v = ADAM_B2 * v + (1.0 - ADAM_B2) * (g * g)
    m_hat = m / (1.0 - ADAM_B1 ** ADAM_STEP)
    v_hat = v / (1.0 - ADAM_B2 ** ADAM_STEP)
    delta = -ADAM_LR * (m_hat / (jnp.sqrt(v_hat) + ADAM_EPS) + ADAM_WD * w)
    return delta, m, v


def _row_tile(rows):
    for t in (512, 352, 256, 128):
        if rows % t == 0:
            return t
    return rows


def _adamw(name, wgmv, steps, after=()):
    n = len(wgmv)

    def body(*refs):
        ins, outs = refs[:4 * n], refs[4 * n:]
        for i in range(n):
            w_ref, g_ref, m_ref, v_ref = ins[4 * i:4 * i + 4]
            d_ref, nm_ref, nv_ref = outs[3 * i:3 * i + 3]
            d_ref[...], nm_ref[...], nv_ref[...] = _adamw_math(w_ref[...], g_ref[...], m_ref[...], v_ref[...])

    in_specs, out_specs, out_shape, args = [], [], [], []
    for w, g, m, v in wgmv:
        rows, cols = w.shape
        spec = pl.BlockSpec((rows // steps, cols), lambda i: (i, 0))
        in_specs += [spec] * 4
        out_specs += [spec] * 3
        out_shape += [jax.ShapeDtypeStruct(w.shape, F32)] * 3
        args += [w, g, m, v]
    res, _ = _call(body, name=name, grid=(steps,), in_specs=in_specs, out_specs=out_specs, out_shape=out_shape,
                   args=args, after=after)
    return [res[3 * i:3 * i + 3] for i in range(n)]


def _adamw_ada(place, cact, dmod, w, m, v, after=()):
    rows, cols = w.shape
    tr = _row_tile(rows)

    def body(pref, ca_ref, dm_ref, w_ref, m_ref, v_ref, g_ref, d_ref, nm_ref, nv_ref):
        g = lax.dot_general(ca_ref[...], dm_ref[...], (((0,), (0,)), ((), ())), preferred_element_type=F32,
                            precision=lax.Precision.HIGHEST)
        g_ref[...] = g
        d_ref[...], nm_ref[...], nv_ref[...] = _adamw_math(w_ref[...], g, m_ref[...], v_ref[...])

    spec = pl.BlockSpec((tr, cols), lambda i, pref: (i, 0))
    return _call(
        body, name="adamw_ada", grid=(rows // tr,),
        in_specs=[pl.BlockSpec((N_DEV, tr), lambda i, pref: (0, i)),
                  pl.BlockSpec((N_DEV, cols), lambda i, pref: (0, pref[1])), spec, spec, spec],
        out_specs=[spec] * 4, out_shape=[jax.ShapeDtypeStruct(w.shape, F32)] * 4,
        args=(cact, dmod, w, m, v), prefetch=(place,), after=after)[0]


def _adamw_small(place, owns, gathered, wmv):
    nw = len(wmv)
    flat = [a for t in wmv for a in t]

    def body(pref, *refs):
        own_refs, all_refs, refs = refs[:5], refs[5:10], refs[10:]
        w_refs = refs[:3 * nw]
        loss_ref, dmod_ref = refs[3 * nw], refs[3 * nw + 1]
        o_refs = refs[3 * nw + 2:]
        j = pref[1]
        me = 2 * pref[1] + pref[0]

        def total(i):
            acc = None
            for b in range(N_DEV):
                blk = jnp.where(me == b, own_refs[i][...], all_refs[i][b])
                acc = blk if acc is None else acc + blk
            return acc

        vf, vd, vc, ddw, gwg = [total(i) for i in range(5)]
        loss_ref[...] = (0.5 / D_MODEL) * jnp.sum(vf[5:6, :], axis=1, keepdims=True)
        order = ((1, 1), (1, 2), (1, 0), (0, 2), (0, 3), (0, 1))
        for b in range(N_DEV):
            for q, (i, row) in enumerate(order):
                dmod_ref[b:b + 1, q * D_MODEL:(q + 1) * D_MODEL] = jnp.where(
                    me == b, own_refs[i][row:row + 1, :], all_refs[i][b, row:row + 1, :])
        dm = dmod_ref[...]
        g_bada = dm[0:1, :]
        for b in range(1, N_DEV):
            g_bada = g_bada + dm[b:b + 1, :]
        g_dww = jnp.zeros((32, POOL_G), F32)
        for k in range(N_CHIP):
            g_dww = g_dww + jnp.where(j == k, ddw[:, k * POOL_G:(k + 1) * POOL_G], 0.0)
        grads = [g_bada, vd[3:4, :], g_dww, vc[3:4, :], vc[1:2, :], vc[2:3, :], gwg, vc[0:1, :], vf[4:5, :],
                 vf[0:1, :]]
        for i, g in enumerate(grads):
            w_ref, m_ref, v_ref = w_refs[3 * i:3 * i + 3]
            d, nm, nv = _adamw_math(w_ref[...], g, m_ref[...], v_ref[...])
            o_refs[4 * i][...] = g
            o_refs[4 * i + 1][...] = d
            o_refs[4 * i + 2][...] = nm
            o_refs[4 * i + 3][...] = nv

    outs = [jax.ShapeDtypeStruct((1, 1), F32), jax.ShapeDtypeStruct((N_DEV, 6 * D_MODEL), F32)]
    for w, _, _ in wmv:
        outs += [jax.ShapeDtypeStruct(w.shape, F32)] * 4
    full = lambda a: pl.BlockSpec(a.shape, lambda i, pref: (0,) * a.ndim)
    args = list(owns) + list(gathered) + flat
    res, _ = _call(body, name="adamw_small", grid=(1,), in_specs=[full(a) for a in args],
                   out_specs=[full(o) for o in outs], out_shape=outs, args=args, prefetch=(place,))
    return res[0], res[1], [res[2 + 4 * i:6 + 4 * i] for i in range(nw)]


def kernel(x, c, w_ada, b_ada, g_norm1, w_in, dw_w, dw_b, conv_ln_g, conv_ln_b, w_conv_pw, w_pool_group, pool_scale, w_out, g_norm2, w_ffn_gate, w_ffn_up, w_ffn_down, g_final, loss_target, m_w_ada, m_b_ada, m_g_norm1, m_w_in, m_dw_w, m_dw_b, m_conv_ln_g, m_conv_ln_b, m_w_conv_pw, m_w_pool_group, m_pool_scale, m_w_out, m_g_norm2, m_w_ffn_gate, m_w_ffn_up, m_w_ffn_down, m_g_final, v_w_ada, v_b_ada, v_g_norm1, v_w_in, v_dw_w, v_dw_b, v_conv_ln_g, v_conv_ln_b, v_w_conv_pw, v_w_pool_group, v_pool_scale, v_w_out, v_g_norm2, v_w_ffn_gate, v_w_ffn_up, v_w_ffn_down, v_g_final):
    xi, yi, ci = _place()
    place = jnp.stack([ci, 2 * xi + yi]).astype(jnp.int32)
    n_ada = w_ada.shape[2]

    tr = lambda a: jnp.transpose(a[0])
    big = [w_in[0], w_conv_pw[0], w_out[0], tr(w_ffn_gate), tr(w_ffn_up), w_ffn_down[0]]
    b_in, b_pw, b_out, b_gate, b_up, b_down, b_dww, wg_b = _cast_weights(place, big, dw_w[0], w_pool_group[0])

    (mod4, cact), (win_g, wpw_g, wout_g, dww_g) = _ada_fwd(
        c, w_ada[0], b_ada.reshape(N_CHIP, n_ada),
        comm=_weights_gather([b_in, b_pw, b_out, b_dww], [True, True, True, False]))
    mod = mod4.reshape(6, D_MODEL)
    dww_full = jnp.pad(jnp.concatenate([dww_g[k] for k in range(N_CHIP)], axis=1), ((0, 1), (0, 0)))
    w_pw = wpw_g.reshape(CONV_W, CONV_W)
    w_o = wout_g.reshape(D_MODEL, D_MODEL)
    xs, tgt, gf = x[0], loss_target[0], g_final.reshape(1, D_MODEL)
    s = xs.shape[0]
    fb = b_gate.shape[1]

    (x2, y, u, z, rstd, p, ycat), (wgate_g, wup_g, wdown_g) = _mixer_fwd(
        xs, mod, g_norm1, win_g, dww_full, dw_b, conv_ln_g, conv_ln_b, w_pw, wg_b, pool_scale, w_o,
        comm=_weights_gather([b_gate, b_up, b_down], [True, True, True]))
    dx2, h2, df, act, dgg, duu, vec_f = _ffn(x2, tgt, mod, g_norm2, gf, wgate_g, wup_g, wdown_g)

    whole = lambda w: pl.BlockSpec((s, w), lambda j: (0, 0))
    cols = lambda w: pl.BlockSpec((s, w), lambda j: (0, j))
    hid = pl.BlockSpec((1, s, fb), lambda j: (j, 0, 0))
    c_gate, _ = _dw("dw_gate", dgg, hid, h2, whole(D_MODEL), N_CHIP, fb, D_MODEL)
    c_up, (r_gate,) = _dw("dw_up", duu, hid, h2, whole(D_MODEL), N_CHIP, fb, D_MODEL, comm=_sibling_halves([c_gate]))
    ((pb_gate,), (own_gate,)), _ = _chip_partials("partials_gate", place, [c_gate], [r_gate])
    ex_gate = _exchange_parts([pb_gate])
    st_gate, tok_gate = _split_start("exchange_gate_start", [pb_gate], *ex_gate)
    c_down, (r_up,) = _dw("dw_down", act, hid, df, whole(D_MODEL), N_CHIP, fb, D_MODEL,
                          comm=_sibling_halves([c_up]), after=(tok_gate,))
    ((pb_up,), (own_up,)), (r_down,) = _chip_partials("partials_up", place, [c_up], [r_up],
                                                      comm=_sibling_halves([c_down]))
    ((pb_down,), (own_down,)), _ = _chip_partials("partials_down", place, [c_down], [r_down])
    ex_ud = _exchange_parts([pb_up, pb_down])
    st_ud, tok_ud = _split_start("exchange_up_down_start", [pb_up, pb_down], *ex_ud)
    (gx, h1, du, dy, sw, dyc, dyp, vec_d, vec_c, ddw), _ = _mixer_bwd(
        dx2, xs, y, u, z, rstd, p, mod, g_norm1, win_g, dww_full, conv_ln_g, conv_ln_b, w_pw, wg_b, pool_scale, w_o,
        after=(tok_ud,))

    g_wg, c_out, c_pw, c_in = _dw_mixer(p, dyp, ycat, dy, sw, dyc, h1, du)
    small_own = [vec_f, vec_d, vec_c, ddw, g_wg]
    ex_small = _small_parts(small_own)
    st_small, tok_small = _split_start("small_grads_start", small_own, *ex_small, zeroed=True)
    (rc_gate,) = _split_wait("exchange_gate_wait", st_gate, 1, ex_gate[1], ex_gate[2], after=(c_in, tok_small))
    rc_up, rc_down = _split_wait("exchange_up_down_wait", st_ud, 2, ex_ud[1], ex_ud[2], after=(c_in, rc_gate))

    ffn_fulls, (r_in, r_pw, r_out) = _sum_partials(
        "sum_ffn", place, [own_gate, own_up, own_down], [rc_gate, rc_up, rc_down],
        comm=_sibling_halves([c_in, c_pw, c_out]))
    (pbs_mix, owns_mix), (g_gate, g_up, g_down) = _chip_partials(
        "partials_mix", place, [c_in, c_pw, c_out], [r_in, r_pw, r_out], comm=_join_halves(ffn_fulls))
    small_all = _split_wait("small_grads_wait", st_small, len(small_own), ex_small[1], ex_small[2],
                            after=(g_down,))

    pad_rows = lambda a: jnp.pad(a[0], ((0, 1), (0, 0)))
    row = lambda a: a.reshape(1, -1)
    small = [(b_ada, m_b_ada, v_b_ada), (g_norm1, m_g_norm1, v_g_norm1),
             (pad_rows(dw_w), pad_rows(m_dw_w), pad_rows(v_dw_w)), (dw_b, m_dw_b, v_dw_b),
             (conv_ln_g, m_conv_ln_g, v_conv_ln_g), (conv_ln_b, m_conv_ln_b, v_conv_ln_b),
             (w_pool_group[0], m_w_pool_group[0], v_w_pool_group[0]), (pool_scale, m_pool_scale, v_pool_scale),
             (g_norm2, m_g_norm2, v_g_norm2), (row(g_final), row(m_g_final), row(v_g_final))]
    loss, dmod, small_out = _adamw_small(place, small_own, small_all, small)
    (o_bada, o_g1, o_dww, o_dwb, o_lng, o_lnb, o_wg, o_ps, o_g2, o_gf) = small_out
    o_dww = [a[:CONV_K] for a in o_dww]
    o_gf = [a.reshape(D_MODEL) for a in o_gf]
    lead = lambda outs: [a[None] for a in outs]

    lands, sem_shape, copies = _exchange_parts(pbs_mix)
    state, token = _split_start("exchange_mix_start", pbs_mix, lands, sem_shape, copies)
    u_gate, u_up, u_down = _adamw(
        "adamw_ffn", [(tr(w_ffn_gate), g_gate, tr(m_w_ffn_gate), tr(v_w_ffn_gate)),
                      (tr(w_ffn_up), g_up, tr(m_w_ffn_up), tr(v_w_ffn_up)),
                      (w_ffn_down[0], g_down, m_w_ffn_down[0], v_w_ffn_down[0])],
        steps=4, after=(token,))
    o_gate = [jnp.transpose(o) for o in [g_gate] + list(u_gate)]
    o_up = [jnp.transpose(o) for o in [g_up] + list(u_up)]
    o_down = [g_down] + list(u_down)
    o_ada = _adamw_ada(place, cact, dmod, w_ada[0], m_w_ada[0], v_w_ada[0], after=(token,))
    rc_mix = _split_wait("exchange_mix_wait", state, len(pbs_mix), sem_shape, copies, after=(o_ada[1], u_down[0]))
    mix_fulls, _ = _sum_partials("sum_mix", place, owns_mix, rc_mix)
    g_in, g_pw, g_out = _comm_only("join_mix", _join_halves(mix_fulls))
    u_in, u_pw, u_out = _adamw(
        "adamw_mix", [(w_in[0], g_in, m_w_in[0], v_w_in[0]), (w_conv_pw[0], g_pw, m_w_conv_pw[0], v_w_conv_pw[0]),
                      (w_out[0], g_out, m_w_out[0], v_w_out[0])], steps=4)
    o_in, o_pw, o_out = [g_in] + list(u_in), [g_pw] + list(u_pw), [g_out] + list(u_out)

    per_weight = [lead(o_ada), o_bada, o_g1, lead(o_in), lead(o_dww), o_dwb, o_lng, o_lnb, lead(o_pw), lead(o_wg),
                  o_ps, lead(o_out), o_g2, lead(o_gate), lead(o_up), lead(o_down), o_gf]
    result = [loss.reshape(()), gx[None]]
    for kind in range(4):
        result += [o[kind] for o in per_weight]
    return tuple(result)
```

```python
import functools

import jax
import jax.numpy as jnp
from jax import lax
from jax.experimental import pallas as pl
from jax.experimental.pallas import tpu as pltpu

F32 = jnp.float32
MXU_DTYPE = jnp.bfloat16
EPS = 1e-6

D_MODEL = 1024
CONV_W = 512
POOL_W = 512
CONV_K = 31
POOL_WINDOWS = (2, 4, 8, 16)
POOL_G = 128
IN_W = 2 * CONV_W + POOL_W
N_CHIP = 4
N_DEV = 8
CONV_HALO = 32
POOL_HALO = 16

ADAM_LR = 0.001
ADAM_B1 = 0.9
ADAM_B2 = 0.999
ADAM_EPS = 1e-08
ADAM_WD = 0.01
ADAM_STEP = 10

MESH = pl.DeviceIdType.MESH
ANY = pl.BlockSpec(memory_space=pl.ANY)
VMEM = pl.BlockSpec(memory_space=pltpu.VMEM)


def _dot(a, b):
    return jnp.dot(a.astype(MXU_DTYPE), b.astype(MXU_DTYPE), preferred_element_type=F32)


def _dot_nt(a, b):
    return lax.dot_general(a.astype(MXU_DTYPE), b.astype(MXU_DTYPE), (((1,), (1,)), ((), ())),
                           preferred_element_type=F32)


def _dot_tn(a, b):
    return lax.dot_general(a.astype(MXU_DTYPE), b.astype(MXU_DTYPE), (((0,), (0,)), ((), ())),
                           preferred_element_type=F32)


def _sigmoid(v):
    return 1.0 / (1.0 + jnp.exp(-v))


def _full(shape):
    n = len(shape)
    return pl.BlockSpec(shape, lambda *_: (0,) * n)


def _token_tile(s):
    return 256 if s % 256 == 0 else s


SUBLANES = 8


def _row_shifts(pad_ref, shifted_ref, rows):
    for r in range(1, SUBLANES):
        shifted_ref[r - 1] = pad_ref[r:r + rows, :]

    def window(i, n):
        r, base = i % SUBLANES, i - i % SUBLANES
        if r == 0:
            return pad_ref[base:base + n, :]
        return shifted_ref[r - 1, base:base + n, :]

    return window


def _place():
    return lax.axis_index("x"), lax.axis_index("y"), lax.axis_index("c")


def _flip(x, y, r):
    return ((1 - x) if r & 2 else x, (1 - y) if r & 1 else y)


def _remote(src, dst, send_sem, recv_sem, dev):
    return pltpu.make_async_remote_copy(src_ref=src, dst_ref=dst, send_sem=send_sem, recv_sem=recv_sem,
                                        device_id=dev, device_id_type=MESH)


class _Comm:
    def __init__(self, ins, outs, aliases, scratch, start, finish, mid=None):
        self.ins, self.outs, self.aliases, self.scratch = list(ins), list(outs), dict(aliases), list(scratch)
        self.start, self.finish = start, finish
        self.mid = mid


def _both(a, b):
    na, nao, nas = len(a.ins), len(a.outs), len(a.scratch)
    aliases = dict(a.aliases)
    aliases.update({na + i: nao + o for i, o in b.aliases.items()})

    def start(ins, outs, scr):
        a.start(ins[:na], outs[:nao], scr[:nas])
        b.start(ins[na:], outs[nao:], scr[nas:])

    def finish(ins, outs, scr):
        a.finish(ins[:na], outs[:nao], scr[:nas])
        b.finish(ins[na:], outs[nao:], scr[nas:])

    def mid(ins, outs, scr):
        if a.mid:
            a.mid(ins[:na], outs[:nao], scr[:nas])
        if b.mid:
            b.mid(ins[na:], outs[nao:], scr[nas:])

    return _Comm(a.ins + b.ins, a.outs + b.outs, aliases, a.scratch + b.scratch, start, finish,
                 mid if (a.mid or b.mid) else None)


def _call(body, *, name, grid, in_specs, out_specs, out_shape, args, scratch_shapes=(), prefetch=(), comm=None,
          body_starts=False, after=()):
    in_specs = list(in_specs) + [ANY] * len(after)
    args = list(args) + list(after)
    n_pre, n_in, n_out, n_scr = len(prefetch), len(in_specs), len(out_specs), len(scratch_shapes)
    n_body_in = n_in - len(after)
    c_ins = comm.ins if comm else []
    c_outs = comm.outs if comm else []
    c_scr = comm.scratch if comm else []
    last = grid[0] - 1

    def wrapped(*refs):
        pre, refs = refs[:n_pre], refs[n_pre:]
        ins, cin = refs[:n_body_in], refs[n_in:n_in + len(c_ins)]
        refs = refs[n_in + len(c_ins):]
        outs, cout = refs[:n_out], refs[n_out:n_out + len(c_outs)]
        refs = refs[n_out + len(c_outs):]
        scr, cscr = refs[:n_scr], refs[n_scr:]
        step = pl.program_id(0)
        if comm and not body_starts:
            @pl.when(step == 0)
            def _():
                comm.start(cin, cout, cscr)

        has_mid = comm is not None and comm.mid is not None
        mid_step = grid[0] // 2 if grid[0] >= 4 else None
        if has_mid and mid_step is not None:
            @pl.when(step == mid_step)
            def _():
                comm.mid(cin, cout, cscr)

        if body_starts:
            body(lambda: comm.start(cin, cout, cscr), cout, *pre, *ins, *outs, *scr)
        else:
            body(*pre, *ins, *outs, *scr)
        if comm:
            @pl.when(step == last)
            def _():
                if has_mid and mid_step is None:
                    comm.mid(cin, cout, cscr)
                comm.finish(cin, cout, cscr)

    aliases = {n_pre + n_in + a: n_out + b for a, b in (comm.aliases if comm else {}).items()}
    res = pl.pallas_call(
        wrapped, name=name,
        grid_spec=pltpu.PrefetchScalarGridSpec(
            num_scalar_prefetch=n_pre, grid=grid, in_specs=list(in_specs) + [ANY] * len(c_ins),
            out_specs=list(out_specs) + [ANY] * len(c_outs), scratch_shapes=list(scratch_shapes) + list(c_scr)),
        out_shape=list(out_shape) + list(c_outs),
        input_output_aliases=aliases,
        compiler_params=pltpu.CompilerParams(dimension_semantics=("arbitrary",)),
    )(*prefetch, *args, *c_ins)
    return res[:n_out], res[n_out:]


def _comm_only(name, comm):
    return _call(lambda: None, name=name, grid=(1,), in_specs=[], out_specs=[], out_shape=[], args=[], comm=comm)[1]


def _weights_gather(bufs, split):
    n = len(bufs)

    def ctx(outs):
        x, y, cc = _place()
        chips = dict(me=2 * x + y, y=2 * x + (1 - y), x=2 * (1 - x) + y, d=2 * (1 - x) + (1 - y))
        devs = dict(y=(x, 1 - y, cc), x=(1 - x, y, cc), d=(1 - x, 1 - y, cc), s=(x, y, 1 - cc))

        def piece(a, kj, pc, q=None):
            if not split[a]:
                return outs[a].at[kj]
            h = bufs[a].shape[1] // 2
            if q is None:
                return outs[a].at[kj, pl.ds(pc * h, h), :]
            return outs[a].at[kj, pl.ds(pc * h + q * (h // 2), h // 2), :]

        return cc, chips, devs, piece

    def directs(a, outs, send, recv):
        cc, chips, devs, piece = ctx(outs)
        if not split[a]:
            whole = piece(a, chips["me"], cc)
            return [_remote(whole, whole, send.at[a, k], recv.at[a, k], devs[t]) for k, t in ((0, "y"), (2, "x"), (4, "d"))]
        q = lambda i: piece(a, chips["me"], cc, i)
        return [_remote(q(0), q(0), send.at[a, 0], recv.at[a, 0], devs["y"]),
                _remote(q(1), q(1), send.at[a, 3], recv.at[a, 3], devs["x"]),
                _remote(q(1), q(1), send.at[a, 1], recv.at[a, 1], devs["y"]),
                _remote(q(0), q(0), send.at[a, 2], recv.at[a, 2], devs["x"])]

    def landed(a, k, outs, send, recv):
        cc, chips, devs, piece = ctx(outs)
        if not split[a]:
            got = piece(a, chips[{0: "y", 2: "x", 4: "d"}[k]], cc)
        elif k < 6:
            got = piece(a, chips[("y", "y", "x", "x", "d", "d")[k]], cc, (0, 1, 0, 1, 0, 1)[k])
        else:
            got = piece(a, chips[("y", "x", "d")[k - 6]], 1 - cc)
        return _remote(got, got, send.at[a, k], recv.at[a, k], devs["s"])

    def passed_on(a, outs, send, recv):
        cc, chips, devs, piece = ctx(outs)
        from_y, from_x = piece(a, chips["y"], cc, 0), piece(a, chips["x"], cc, 1)
        return [_remote(from_y, from_y, send.at[a, 4], recv.at[a, 4], devs["x"]),
                _remote(from_x, from_x, send.at[a, 5], recv.at[a, 5], devs["y"])]

    def to_sibling(a, outs, send, recv, which=(0, 1, 2)):
        cc, chips, devs, piece = ctx(outs)
        halves = [piece(a, chips[("y", "x", "d")[i]], cc) for i in which]
        return [_remote(hf, hf, send.at[a, 6 + i], recv.at[a, 6 + i], devs["s"]) for i, hf in zip(which, halves)]

    def start(ins, outs, scr):
        send, recv = scr
        per_item = [directs(a, outs, send, recv) for a in range(n)]
        for rank in range(4):
            for cps in per_item:
                if rank < len(cps):
                    cps[rank].start()

    def mid(ins, outs, scr):
        send, recv = scr
        for a in range(n):
            if split[a]:
                fy, fx = passed_on(a, outs, send, recv)
                landed(a, 0, outs, send, recv).wait_recv()
                fy.start()
                landed(a, 3, outs, send, recv).wait_recv()
                fx.start()

    def finish(ins, outs, scr):
        send, recv = scr
        for a in range(n):
            if split[a]:
                to_y, to_x = to_sibling(a, outs, send, recv, which=(0, 1))
                landed(a, 1, outs, send, recv).wait_recv()
                to_y.start()
                landed(a, 2, outs, send, recv).wait_recv()
                to_x.start()
        for a in range(n):
            if split[a]:
                for k in (4, 5):
                    landed(a, k, outs, send, recv).wait_recv()
                to_sibling(a, outs, send, recv, which=(2,))[0].start()
            else:
                for k in (0, 2, 4):
                    landed(a, k, outs, send, recv).wait_recv()
        for a in range(n):
            if split[a]:
                for k in (6, 7, 8):
                    landed(a, k, outs, send, recv).wait_recv()
            cps = directs(a, outs, send, recv)
            if split[a]:
                cps += passed_on(a, outs, send, recv) + to_sibling(a, outs, send, recv)
            for cp in cps:
                cp.wait_send()

    return _Comm(bufs, [jax.ShapeDtypeStruct(b.shape, b.dtype) for b in bufs], {i: i for i in range(n)},
                 [pltpu.SemaphoreType.DMA((n, 9)), pltpu.SemaphoreType.DMA((n, 9))], start, finish, mid)


HBM =pl.BlockSpec(memory_space=pltpu.HBM)
SEM = pl.BlockSpec(memory_space=pltpu.SEMAPHORE)
DATAFLOW = pltpu.SideEffectType.DATAFLOW_SIDE_EFFECTING


class _SemGrid:
    def __init__(self, refs, cols):
        self.refs, self.cols = refs, cols

    @property
    def at(self):
        return self

    def __getitem__(self, idx):
        return self.refs[idx[0] * self.cols + idx[1]]


def _split_start(name, srcs, lands, sem_shape, copies, zeroed=False):
    n, k = len(srcs), len(lands)
    ns = sem_shape[0] * sem_shape[1]

    def body(*refs):
        src_refs, land_refs = refs[:n], refs[n:n + k]
        send = _SemGrid(refs[n + k:n + k + ns], sem_shape[1])
        recv = _SemGrid(refs[n + k + ns:n + k + 2 * ns], sem_shape[1])
        token = refs[-1]
        for cp in copies(src_refs, land_refs, send, recv):
            cp.start()
        token[...] = jnp.zeros(token.shape, F32)

    hbm = lambda a: pltpu.with_memory_space_constraint(a, pltpu.HBM)
    zones = [jnp.zeros(l.shape, l.dtype) if zeroed else lax.empty(l.shape, l.dtype) for l in lands]
    out = pl.pallas_call(
        body, name=name,
        out_shape=[pltpu.SemaphoreType.DMA(())] * (2 * ns)
        + [pltpu.HBM(a.shape, a.dtype) for a in list(srcs) + list(lands)] + [jax.ShapeDtypeStruct((8, 128), F32)],
        in_specs=[HBM] * (n + k), out_specs=[SEM] * (2 * ns) + [HBM] * (n + k) + [VMEM],
        input_output_aliases={i: 2 * ns + i for i in range(n + k)},
        compiler_params=pltpu.CompilerParams(has_side_effects=DATAFLOW),
    )(*[hbm(a) for a in srcs], *[hbm(z) for z in zones])
    return out[:-1], out[-1]


def _split_wait(name, state, n, sem_shape, copies, after):
    ns = sem_shape[0] * sem_shape[1]
    sems, bufs = state[:2 * ns], state[2 * ns:]
    k = len(bufs) - n

    def body(*refs):
        src_refs, land_refs = refs[:n], refs[n:n + k]
        send = _SemGrid(refs[n + k:n + k + ns], sem_shape[1])
        recv = _SemGrid(refs[n + k + ns:n + k + 2 * ns], sem_shape[1])
        cps = copies(src_refs, land_refs, send, recv)
        for cp in cps:
            cp.wait_send()
        for cp in cps:
            cp.wait_recv()

    out = pl.pallas_call(
        body, name=name,
        out_shape=[pltpu.HBM(a.shape, a.dtype) for a in bufs],
        in_specs=[HBM] * (n + k) + [SEM] * (2 * ns) + [ANY] * len(after), out_specs=[HBM] * (n + k),
        input_output_aliases={i: i for i in range(n + k)},
        compiler_params=pltpu.CompilerParams(has_side_effects=DATAFLOW),
    )(*bufs, *sems, *after)
    return out[n:]


def _direct_phases(copies):
    def start(ins, outs, scr):
        for cp in copies(ins, outs, *scr):
            cp.start()

    def finish(ins, outs, scr):
        cps = copies(ins, outs, *scr)
        for cp in cps:
            cp.wait_recv()
        for cp in cps:
            cp.wait_send()

    return start, finish


def _sibling_halves(gs):
    n = len(gs)

    def copies(ins, outs, send, recv):
        x, y, cc = _place()
        cps = []
        for a in range(n):
            h = gs[a].shape[1] // 2
            cps.append(_remote(ins[a].at[:, pl.ds((1 - cc) * h, h), :], outs[a], send.at[a], recv.at[a],
                               (x, y, 1 - cc)))
        return cps

    start, finish = _direct_phases(copies)
    return _Comm(gs, [jax.ShapeDtypeStruct((N_CHIP, g.shape[1] // 2, g.shape[2]), F32) for g in gs], {},
                 [pltpu.SemaphoreType.DMA((n,)), pltpu.SemaphoreType.DMA((n,))], start, finish)


def _exchange_parts(pbs):
    n = len(pbs)

    def copies(ins, outs, send, recv):
        x, y, cc = _place()
        cps = []
        for a in range(n):
            for r in range(1, N_CHIP):
                kx, ky = _flip(x, y, r)
                cps.append(_remote(ins[a].at[2 * kx + ky], outs[a].at[r - 1], send.at[a, r - 1], recv.at[a, r - 1],
                                   (kx, ky, cc)))
        return cps

    lands = [jax.ShapeDtypeStruct((N_CHIP - 1,) + p.shape[1:], p.dtype) for p in pbs]
    return lands, (n, N_CHIP - 1), copies


def _small_parts(arrs):
    n = len(arrs)

    def copies(ins, outs, send, recv):
        x, y, cc = _place()
        b = 4 * x + 2 * y + cc
        cps = []
        for a in range(n):
            for r in range(1, N_DEV):
                dev = ((1 - x) if r & 4 else x, (1 - y) if r & 2 else y, (1 - cc) if r & 1 else cc)
                cps.append(_remote(ins[a], outs[a].at[b], send.at[a, r - 1], recv.at[a, r - 1], dev))
        return cps

    lands = [jax.ShapeDtypeStruct((N_DEV,) + a.shape, a.dtype) for a in arrs]
    return lands, (n, N_DEV - 1), copies


def _exchange_partials(pbs):
    lands, sem_shape, copies = _exchange_parts(pbs)
    start, finish = _direct_phases(copies)
    return _Comm(pbs, lands, {}, [pltpu.SemaphoreType.DMA(sem_shape), pltpu.SemaphoreType.DMA(sem_shape)],
                 start, finish)


def _join_halves(fulls):
    n = len(fulls)

    def copies(ins, outs, send, recv):
        x, y, cc = _place()
        cps = []
        for a in range(n):
            h = fulls[a].shape[0] // 2
            mine = outs[a].at[pl.ds(cc * h, h), :]
            cps.append(_remote(mine, mine, send.at[a], recv.at[a], (x, y, 1 - cc)))
        return cps

    start, finish = _direct_phases(copies)
    return _Comm(fulls, [jax.ShapeDtypeStruct(f.shape, F32) for f in fulls], {i: i for i in range(n)},
                 [pltpu.SemaphoreType.DMA((n,)), pltpu.SemaphoreType.DMA((n,))], start, finish)


def _mixer_fwd(x, mod, g1, w_in, dww, dwb, lng, lnb, w_pw, wg, pscale, w_out, comm=None):
    s = x.shape[0]
    ts = _token_tile(s)
    nt = s // ts

    def body(x_ref, mod_ref, g1_ref, win_ref, dww_ref, dwb_ref, lng_ref, lnb_ref, wpw_ref, wg_ref, ps_ref,
             wout_ref, x2_ref, y_ref, u_ref, z_ref, rstd_ref, p_ref, ycat_ref, gpad, vpad, gshift):
        i = pl.program_id(0)

        @pl.when(i == 0)
        def _():
            gpad[0:CONV_HALO, :] = jnp.zeros((CONV_HALO, CONV_W), F32)
            vpad[0:POOL_HALO, :] = jnp.zeros((POOL_HALO, POOL_W), F32)

        xt = x_ref[...]
        sh1 = mod_ref[0:1, :]
        sc1 = mod_ref[1:2, :]
        gt1 = mod_ref[2:3, :]
        r1 = lax.rsqrt(jnp.mean(xt * xt, axis=-1, keepdims=True) + EPS)
        h1 = (xt * r1 * g1_ref[...]) * (1.0 + sc1) + sh1
        h1b = h1.astype(MXU_DTYPE)
        u = jnp.concatenate([_dot(h1b, win_ref[j]) for j in range(N_CHIP)], axis=1)
        u_ref[...] = u
        a = u[:, :CONV_W]
        g = u[:, CONV_W:2 * CONV_W]
        v = u[:, 2 * CONV_W:]

        gpad[CONV_HALO:CONV_HALO + ts, :] = a * _sigmoid(g)
        window = _row_shifts(gpad, gshift, ts + CONV_HALO - SUBLANES)
        cv = jnp.broadcast_to(dwb_ref[...], (ts, CONV_W))
        off = CONV_HALO - (CONV_K - 1)
        for k in range(CONV_K):
            cv = cv + dww_ref[k:k + 1, :] * window(off + k, ts)
        gpad[0:CONV_HALO, :] = gpad[ts:ts + CONV_HALO, :]

        mu = jnp.mean(cv, axis=-1, keepdims=True)
        cc = cv - mu
        rstd = lax.rsqrt(jnp.mean(cc * cc, axis=-1, keepdims=True) + EPS)
        z = cc * rstd
        z_ref[...] = z
        rstd_ref[...] = rstd
        ln = z * lng_ref[...] + lnb_ref[...]
        sw = ln * _sigmoid(ln)
        yconv = _dot(sw, wpw_ref[...])

        vpad[POOL_HALO:POOL_HALO + ts, :] = v
        t = i * ts + lax.broadcasted_iota(jnp.int32, (ts, 1), 0)
        ps, ypool = [], []
        for gi, w in enumerate(POOL_WINDOWS):
            cols = slice(gi * POOL_G, (gi + 1) * POOL_G)
            acc = vpad[POOL_HALO:POOL_HALO + ts, cols]
            for d in range(1, w):
                acc = acc + vpad[POOL_HALO - d:POOL_HALO - d + ts, cols]
            cnt = jnp.minimum(t + 1, w).astype(F32)
            pg = (acc / cnt - v[:, cols]).astype(MXU_DTYPE)
            ps.append(pg)
            ypool.append(_dot(pg, wg_ref[gi]))
        vpad[0:POOL_HALO, :] = vpad[ts:ts + POOL_HALO, :]
        p_ref[...] = jnp.concatenate(ps, axis=1)
        ypool = jnp.concatenate(ypool, axis=1) * ps_ref[...]

        ycat = jnp.concatenate([yconv, ypool], axis=1).astype(MXU_DTYPE)
        ycat_ref[...] = ycat
        y = _dot(ycat, wout_ref[...])
        y_ref[...] = y
        x2_ref[...] = xt + gt1 * y

    tile = lambda w: pl.BlockSpec((ts, w), lambda i: (i, 0))
    return _call(
        body, name="mixer_fwd", grid=(nt,),
        in_specs=[tile(D_MODEL), _full(mod.shape), _full(g1.shape), _full(w_in.shape), _full(dww.shape),
                  _full(dwb.shape), _full(lng.shape), _full(lnb.shape), _full(w_pw.shape), _full(wg.shape),
                  _full(pscale.shape), _full(w_out.shape)],
        out_specs=[tile(D_MODEL), tile(D_MODEL), tile(IN_W), tile(CONV_W), tile(1), tile(POOL_W), tile(D_MODEL)],
        out_shape=[jax.ShapeDtypeStruct((s, D_MODEL), F32), jax.ShapeDtypeStruct((s, D_MODEL), F32),
                   jax.ShapeDtypeStruct((s, IN_W), F32), jax.ShapeDtypeStruct((s, CONV_W), F32),
                   jax.ShapeDtypeStruct((s, 1), F32), jax.ShapeDtypeStruct((s, POOL_W), MXU_DTYPE),
                   jax.ShapeDtypeStruct((s, D_MODEL), MXU_DTYPE)],
        scratch_shapes=[pltpu.VMEM((ts + CONV_HALO, CONV_W), F32), pltpu.VMEM((ts + POOL_HALO, POOL_W), F32),
                        pltpu.VMEM((SUBLANES - 1, ts + CONV_HALO - SUBLANES, CONV_W), F32)],
        args=(x, mod, g1, w_in, dww, dwb, lng, lnb, w_pw, wg, pscale, w_out), comm=comm)


def _ffn(x2, tgt, mod, g2, gf, w_gate, w_up, w_down):
    s = x2.shape[0]
    ts = _token_tile(s)
    nt = s // ts
    fb = w_gate.shape[1]

    def body(x2_ref, tgt_ref, mod_ref, g2_ref, gf_ref, wgt_ref, wup_ref, wdn_ref,
             dx2_ref, h2_ref, df_ref, act_ref, dgg_ref, duu_ref, vec_ref, gg_s, uu_s):
        i = pl.program_id(0)

        @pl.when(i == 0)
        def _():
            vec_ref[...] = jnp.zeros(vec_ref.shape, F32)

        x2t = x2_ref[...]
        sh2 = mod_ref[3:4, :]
        sc2 = mod_ref[4:5, :]
        gt2 = mod_ref[5:6, :]
        g2v = g2_ref[...]
        gfv = gf_ref[...]
        r2 = lax.rsqrt(jnp.mean(x2t * x2t, axis=-1, keepdims=True) + EPS)
        xh2 = x2t * r2
        n2 = xh2 * g2v
        h2b = (n2 * (1.0 + sc2) + sh2).astype(MXU_DTYPE)
        h2_ref[...] = h2b
        f = jnp.zeros((ts, D_MODEL), F32)
        for j in range(N_CHIP):
            gg = _dot_nt(h2b, wgt_ref[j])
            uu = _dot_nt(h2b, wup_ref[j])
            gg_s[j] = gg
            uu_s[j] = uu
            actb = (gg * _sigmoid(gg) * uu).astype(MXU_DTYPE)
            act_ref[j] = actb
            f = f + _dot(actb, wdn_ref[j])
        x3 = x2t + gt2 * f
        r3 = lax.rsqrt(jnp.mean(x3 * x3, axis=-1, keepdims=True) + EPS)
        xh3 = x3 * r3
        diff = xh3 * gfv - tgt_ref[...]
        dout = diff * (1.0 / D_MODEL)
        dn3 = dout * gfv
        dx3 = r3 * (dn3 - xh3 * jnp.mean(dn3 * xh3, axis=-1, keepdims=True))
        dfb = (dx3 * gt2).astype(MXU_DTYPE)
        df_ref[...] = dfb
        dh2 = jnp.zeros((ts, D_MODEL), F32)
        for j in range(N_CHIP):
            dact = _dot_nt(dfb, wdn_ref[j])
            gg = gg_s[j]
            uu = uu_s[j]
            sg = _sigmoid(gg)
            duu = (dact * (gg * sg)).astype(MXU_DTYPE)
            dgg = (dact * uu * (sg * (1.0 + gg * (1.0 - sg)))).astype(MXU_DTYPE)
            duu_ref[j] = duu
            dgg_ref[j] = dgg
            dh2 = dh2 + _dot(dgg, wgt_ref[j]) + _dot(duu, wup_ref[j])
        dn2 = dh2 * (1.0 + sc2)
        dxh2 = dn2 * g2v
        dx2_ref[...] = dx3 + r2 * (dxh2 - xh2 * jnp.mean(dxh2 * xh2, axis=-1, keepdims=True))

        col = lambda a: jnp.sum(a, axis=0, keepdims=True)
        vec_ref[0:1, :] += col(dout * xh3)
        vec_ref[1:2, :] += col(dx3 * f)
        vec_ref[2:3, :] += col(dh2)
        vec_ref[3:4, :] += col(dh2 * n2)
        vec_ref[4:5, :] += col(dn2 * xh2)
        vec_ref[5:6, :] += col(diff * diff)

    tile = lambda w: pl.BlockSpec((ts, w), lambda i: (i, 0))
    tile3 = pl.BlockSpec((N_CHIP, ts, fb), lambda i: (0, i, 0))
    once = lambda a: pl.BlockSpec(a.shape, lambda i: (0,) * a.ndim, pipeline_mode=pl.Buffered(1))
    hid = jax.ShapeDtypeStruct((N_CHIP, s, fb), MXU_DTYPE)
    return pl.pallas_call(
        body, name="ffn", grid=(nt,),
        in_specs=[tile(D_MODEL), tile(D_MODEL), _full(mod.shape), _full(g2.shape), _full(gf.shape),
                  once(w_gate), once(w_up), once(w_down)],
        out_specs=[tile(D_MODEL), tile(D_MODEL), tile(D_MODEL), tile3, tile3, tile3, _full((8, D_MODEL))],
        out_shape=[jax.ShapeDtypeStruct((s, D_MODEL), F32), jax.ShapeDtypeStruct((s, D_MODEL), MXU_DTYPE),
                   jax.ShapeDtypeStruct((s, D_MODEL), MXU_DTYPE), hid, hid, hid,
                   jax.ShapeDtypeStruct((8, D_MODEL), F32)],
        scratch_shapes=[pltpu.VMEM((N_CHIP, ts, fb), F32), pltpu.VMEM((N_CHIP, ts, fb), F32)],
        compiler_params=pltpu.CompilerParams(dimension_semantics=("arbitrary",)),
    )(x2, tgt, mod, g2, gf, w_gate, w_up, w_down)


def _mixer_bwd(dx2, x, y, u, z, rstd, p, mod, g1, w_in, dww, lng, lnb, w_pw, wg, pscale, w_out, comm=None, after=()):
    s = x.shape[0]
    ts = _token_tile(s)
    nt = s // ts

    def body(dx2_ref, x_ref, y_ref, u_ref, z_ref, rstd_ref, p_ref, mod_ref, g1_ref, win_ref, dww_ref, lng_ref,
             lnb_ref, wpw_ref, wg_ref, ps_ref, wout_ref,
             gx_ref, h1_ref, du_ref, dy_ref, sw_ref, dyc_ref, dyp_ref, vd_ref, vc_ref, ddw_ref, dcpad, dppad,
             dshift):
        i = pl.program_id(0)
        tix = nt - 1 - i

        @pl.when(i == 0)
        def _():
            vd_ref[...] = jnp.zeros(vd_ref.shape, F32)
            vc_ref[...] = jnp.zeros(vc_ref.shape, F32)
            ddw_ref[...] = jnp.zeros(ddw_ref.shape, F32)
            dcpad[ts:ts + CONV_HALO, :] = jnp.zeros((CONV_HALO, CONV_W), F32)
            dppad[ts:ts + POOL_HALO, :] = jnp.zeros((POOL_HALO, POOL_W), F32)

        col = lambda a: jnp.sum(a, axis=0, keepdims=True)
        sh1 = mod_ref[0:1, :]
        sc1 = mod_ref[1:2, :]
        gt1 = mod_ref[2:3, :]
        dx2t = dx2_ref[...]
        vd_ref[0:1, :] += col(dx2t * y_ref[...])
        dyb = (dx2t * gt1).astype(MXU_DTYPE)
        dy_ref[...] = dyb
        dycat = _dot_nt(dyb, wout_ref[...])
        dyconv = dycat[:, :CONV_W]
        dypool = dycat[:, CONV_W:]

        pt = p_ref[...]
        t = tix * ts + lax.broadcasted_iota(jnp.int32, (ts, 1), 0)
        psc = ps_ref[...]
        dypb = (dypool * psc).astype(MXU_DTYPE)
        dyp_ref[...] = dypb
        dps, ypre = [], []
        for gi, w in enumerate(POOL_WINDOWS):
            cols = slice(gi * POOL_G, (gi + 1) * POOL_G)
            ypre.append(_dot(pt[:, cols], wg_ref[gi]))
            dpg = _dot_nt(dypb[:, cols], wg_ref[gi])
            dps.append(dpg)
            cnt = jnp.minimum(t + 1, w).astype(F32)
            dppad[0:ts, cols] = dpg / cnt
        vc_ref[0:1, :] += col(dypool * jnp.concatenate(ypre, axis=1))
        dvs = []
        for gi, w in enumerate(POOL_WINDOWS):
            cols = slice(gi * POOL_G, (gi + 1) * POOL_G)
            acc = dppad[0:ts, cols]
            for d in range(1, w):
                acc = acc + dppad[d:d + ts, cols]
            dvs.append(acc - dps[gi])
        dv = jnp.concatenate(dvs, axis=1)
        dppad[ts:ts + POOL_HALO, :] = dppad[0:POOL_HALO, :]

        zt = z_ref[...]
        lngv = lng_ref[...]
        ln = zt * lngv + lnb_ref[...]
        sg = _sigmoid(ln)
        swb = (ln * sg).astype(MXU_DTYPE)
        sw_ref[...] = swb
        dycb = dyconv.astype(MXU_DTYPE)
        dyc_ref[...] = dycb
        dln = _dot_nt(dycb, wpw_ref[...]) * (sg * (1.0 + ln * (1.0 - sg)))
        vc_ref[1:2, :] += col(dln * zt)
        vc_ref[2:3, :] += col(dln)
        dz = dln * lngv
        dcv = rstd_ref[...] * (dz - jnp.mean(dz, axis=-1, keepdims=True)
                               - zt * jnp.mean(dz * zt, axis=-1, keepdims=True))
        vc_ref[3:4, :] += col(dcv)
        dcpad[0:ts, :] = dcv
        ut = u_ref[...]
        a = ut[:, :CONV_W]
        g = ut[:, CONV_W:2 * CONV_W]
        sgg = _sigmoid(g)
        glu = a * sgg
        window = _row_shifts(dcpad, dshift, ts + CONV_HALO - SUBLANES)
        dglu = jnp.zeros((ts, CONV_W), F32)
        for k in range(CONV_K):
            sh = window(CONV_K - 1 - k, ts)
            dglu = dglu + dww_ref[k:k + 1, :] * sh
            ddw_ref[k:k + 1, :] += col(glu * sh)
        dcpad[ts:ts + CONV_HALO, :] = dcpad[0:CONV_HALO, :]
        da = dglu * sgg
        dg = dglu * a * sgg * (1.0 - sgg)
        dub = jnp.concatenate([da, dg, dv], axis=1).astype(MXU_DTYPE)
        du_ref[...] = dub
        cw = IN_W // N_CHIP
        dh1 = jnp.zeros((ts, D_MODEL), F32)
        for j in range(N_CHIP):
            dh1 = dh1 + _dot_nt(dub[:, j * cw:(j + 1) * cw], win_ref[j])

        xt = x_ref[...]
        g1v = g1_ref[...]
        r1 = lax.rsqrt(jnp.mean(xt * xt, axis=-1, keepdims=True) + EPS)
        xh1 = xt * r1
        n1 = xh1 * g1v
        h1_ref[...] = (n1 * (1.0 + sc1) + sh1).astype(MXU_DTYPE)
        vd_ref[1:2, :] += col(dh1)
        vd_ref[2:3, :] += col(dh1 * n1)
        dn1 = dh1 * (1.0 + sc1)
        vd_ref[3:4, :] += col(dn1 * xh1)
        dxh = dn1 * g1v
        gx_ref[...] = dx2t + r1 * (dxh - xh1 * jnp.mean(dxh * xh1, axis=-1, keepdims=True))

    tile = lambda w: pl.BlockSpec((ts, w), lambda i: (nt - 1 - i, 0))
    bf = lambda w: jax.ShapeDtypeStruct((s, w), MXU_DTYPE)
    return _call(
        body, name="mixer_bwd", grid=(nt,),
        in_specs=[tile(D_MODEL), tile(D_MODEL), tile(D_MODEL), tile(IN_W), tile(CONV_W), tile(1), tile(POOL_W),
                  _full(mod.shape), _full(g1.shape), _full(w_in.shape), _full(dww.shape), _full(lng.shape),
                  _full(lnb.shape), _full(w_pw.shape), _full(wg.shape), _full(pscale.shape), _full(w_out.shape)],
        out_specs=[tile(D_MODEL), tile(D_MODEL), tile(IN_W), tile(D_MODEL), tile(CONV_W), tile(CONV_W),
                   tile(POOL_W), _full((8, D_MODEL)), _full((8, CONV_W)), _full((32, CONV_W))],
        out_shape=[jax.ShapeDtypeStruct((s, D_MODEL), F32), bf(D_MODEL), bf(IN_W), bf(D_MODEL), bf(CONV_W),
                   bf(CONV_W), bf(POOL_W), jax.ShapeDtypeStruct((8, D_MODEL), F32),
                   jax.ShapeDtypeStruct((8, CONV_W), F32), jax.ShapeDtypeStruct((32, CONV_W), F32)],
        scratch_shapes=[pltpu.VMEM((ts + CONV_HALO, CONV_W), F32), pltpu.VMEM((ts + POOL_HALO, POOL_W), F32),
                        pltpu.VMEM((SUBLANES - 1, ts + CONV_HALO - SUBLANES, CONV_W), F32)],
        args=(dx2, x, y, u, z, rstd, p, mod, g1, w_in, dww, lng, lnb, w_pw, wg, pscale, w_out), comm=comm,
        after=after)


def _dw(name, a, a_spec, b, b_spec, nb, mb, nbk, comm=None, after=()):
    def body(a_ref, b_ref, o_ref):
        av = a_ref[...]
        bv = b_ref[...]
        av = av.reshape(av.shape[-2:])
        bv = bv.reshape(bv.shape[-2:])
        o_ref[0] = _dot_tn(av, bv)

    (out,), rest = _call(
        body, name=name, grid=(nb,), in_specs=[a_spec, b_spec],
        out_specs=[pl.BlockSpec((1, mb, nbk), lambda j: (j, 0, 0))],
        out_shape=[jax.ShapeDtypeStruct((nb, mb, nbk), F32)], args=(a, b), comm=comm, after=after)
    return out, rest


def _dw_mixer(p, dyp, ycat, dy, sw, dyc, h1, du):
    s = p.shape[0]
    ng = len(POOL_WINDOWS)
    assert ng == N_CHIP

    def body(p_ref, dyp_ref, ycat_ref, dy_ref, sw_ref, dyc_ref, h1_ref, du_ref, wg_ref, out_ref, pw_ref, in_ref):
        wg_ref[0] = _dot_tn(p_ref[...], dyp_ref[...])
        out_ref[0] = _dot_tn(ycat_ref[...], dy_ref[...])
        pw_ref[0] = _dot_tn(sw_ref[...], dyc_ref[...])
        in_ref[0] = _dot_tn(h1_ref[...], du_ref[...])

    whole = lambda w: pl.BlockSpec((s, w), lambda j: (0, 0))
    cols = lambda w: pl.BlockSpec((s, w), lambda j: (0, j))
    blk = lambda m, n: pl.BlockSpec((1, m, n), lambda j: (j, 0, 0))
    shapes = [(POOL_G, POOL_G), (D_MODEL // N_CHIP, D_MODEL), (CONV_W // N_CHIP, CONV_W), (D_MODEL, IN_W // N_CHIP)]
    res, _ = _call(
        body, name="dw_mixer", grid=(N_CHIP,),
        in_specs=[cols(POOL_G), cols(POOL_G), cols(D_MODEL // N_CHIP), whole(D_MODEL), cols(CONV_W // N_CHIP),
                  whole(CONV_W), whole(D_MODEL), cols(IN_W // N_CHIP)],
        out_specs=[blk(m, n) for m, n in shapes],
        out_shape=[jax.ShapeDtypeStruct((N_CHIP, m, n), F32) for m, n in shapes],
        args=(p, dyp, ycat, dy, sw, dyc, h1, du))
    return res


def _ada_fwd(c, w_ada, b4, first, later, dww, wg, comm):
    nc = w_ada.shape[1]
    nf, nl = len(first), len(later)
    shards = list(first) + list(later)

    def body(start_comm, gathered, c_ref, w_ref, b4_ref, *refs):
        shard_refs, refs = refs[:nf + nl], refs[nf + nl:]
        dww_ref, wg_ref, mod_ref, cact_ref, wgb_ref = refs[:5]
        later_refs, refs = refs[5:5 + nl], refs[5 + nl:]
        call, part, parts = refs[:3]
        stages, refs = refs[3:3 + nf + nl], refs[3 + nf + nl:]
        send1, recv1, send2, recv2, lsem = refs
        x, y, cc = _place()
        b = 4 * x + 2 * y + cc
        j = 2 * x + y

        def slot_copies(lo, hi):
            cps = []
            for a in range(lo, hi):
                dst = gathered[a] if a < nf else later_refs[a - nf]
                cps.append(pltpu.make_async_copy(stages[a], dst.at[j], lsem.at[a]))
            return cps

        call[b] = c_ref[...]
        sends = []
        for r in range(1, N_DEV):
            dev = ((1 - x) if r & 4 else x, (1 - y) if r & 2 else y, (1 - cc) if r & 1 else cc)
            cp = _remote(call.at[b], call.at[b], send1.at[r - 1], recv1.at[r - 1], dev)
            cp.start()
            sends.append(cp)
        for a in range(nf):
            stages[a][...] = shard_refs[a][...].astype(MXU_DTYPE)
        dww_copy = pltpu.make_async_copy(dww_ref, gathered[nf].at[j], lsem.at[nf + nl])
        dww_copy.start()
        for cp in slot_copies(0, nf):
            cp.start()
        for r in range(1, N_DEV):
            src_b = lax.bitwise_xor(b, r)
            _remote(call.at[src_b], call.at[src_b], send1.at[r - 1], recv1.at[r - 1], (x, y, cc)).wait_recv()
        for cp in sends:
            cp.wait_send()
        for cp in slot_copies(0, nf):
            cp.wait()
        dww_copy.wait()
        start_comm()
        for i in range(N_DEV):
            ci = call[i]
            cact_ref[i:i + 1, :] = ci * _sigmoid(ci)
        part[...] = jnp.dot(cact_ref[...], w_ref[...], preferred_element_type=F32, precision=lax.Precision.HIGHEST)
        sends = []
        for r in range(1, N_CHIP):
            kx, ky = _flip(x, y, r)
            cp = _remote(part, parts.at[j], send2.at[r - 1], recv2.at[r - 1], (kx, ky, cc))
            cp.start()
            sends.append(cp)
        parts[j] = part[...]
        for a in range(nf, nf + nl):
            stages[a][...] = shard_refs[a][...].astype(MXU_DTYPE)
        for cp in slot_copies(nf, nf + nl):
            cp.start()
        wgb_ref[...] = wg_ref[...].astype(MXU_DTYPE)
        for r in range(1, N_CHIP):
            kx, ky = _flip(x, y, r)
            kj = 2 * kx + ky
            _remote(part, parts.at[kj], send2.at[r - 1], recv2.at[r - 1], (x, y, cc)).wait_recv()
        for cp in sends:
            cp.wait_send()
        mine = lax.broadcasted_iota(jnp.int32, (N_DEV, 1), 0) == b
        for k in range(N_CHIP):
            row = jnp.sum(jnp.where(mine, parts[k], 0.0), axis=0, keepdims=True)
            mod_ref[k:k + 1, :] = row + b4_ref[k:k + 1, :]
        for cp in slot_copies(nf, nf + nl):
            cp.wait()

    res, rest = _call(
        body, name="ada_fwd", grid=(1,),
        in_specs=[VMEM] * (5 + nf + nl), out_specs=[VMEM, VMEM, VMEM] + [ANY] * nl,
        out_shape=[jax.ShapeDtypeStruct((N_CHIP, nc), F32), jax.ShapeDtypeStruct((N_DEV, D_MODEL), F32),
                   jax.ShapeDtypeStruct(wg.shape, MXU_DTYPE)]
        + [jax.ShapeDtypeStruct((N_CHIP,) + a.shape, MXU_DTYPE) for a in later],
        scratch_shapes=[pltpu.VMEM((N_DEV, 1, D_MODEL), F32), pltpu.VMEM((N_DEV, nc), F32),
                        pltpu.VMEM((N_CHIP, N_DEV, nc), F32)]
        + [pltpu.VMEM(a.shape, MXU_DTYPE) for a in shards]
        + [pltpu.SemaphoreType.DMA((N_DEV - 1,)), pltpu.SemaphoreType.DMA((N_DEV - 1,)),
           pltpu.SemaphoreType.DMA((N_CHIP - 1,)), pltpu.SemaphoreType.DMA((N_CHIP - 1,)),
           pltpu.SemaphoreType.DMA((nf + nl + 1,))],
        args=(c, w_ada, b4, *shards, dww, wg), comm=comm, body_starts=True)
    return (res[0], res[1], res[2], res[3:]), rest


def _chip_partials(name, place, gs, rs, comm=None):
    n = len(gs)

    def body(pref, *refs):
        g_refs, r_refs = refs[:n], refs[n:2 * n]
        pb_refs, own_refs = refs[2 * n:3 * n], refs[3 * n:]
        jj = pl.program_id(0)
        for a in range(n):
            sm = g_refs[a][0] + r_refs[a][0]
            pb_refs[a][0] = sm.astype(MXU_DTYPE)

            @pl.when(jj == pref[1])
            def _(a=a, sm=sm):
                own_refs[a][...] = sm

    halves = [(g.shape[1] // 2, g.shape[2]) for g in gs]
    in_specs = [pl.BlockSpec((1, h, w), lambda jj, pref: (jj, pref[0], 0)) for h, w in halves]
    in_specs += [pl.BlockSpec((1, h, w), lambda jj, pref: (jj, 0, 0)) for h, w in halves]
    out_specs = [pl.BlockSpec((1, h, w), lambda jj, pref: (jj, 0, 0)) for h, w in halves]
    out_specs += [pl.BlockSpec((h, w), lambda jj, pref: (0, 0)) for h, w in halves]
    out, rest = _call(
        body, name=name, grid=(N_CHIP,), in_specs=in_specs, out_specs=out_specs,
        out_shape=[jax.ShapeDtypeStruct((N_CHIP, h, w), MXU_DTYPE) for h, w in halves]
        + [jax.ShapeDtypeStruct((h, w), F32) for h, w in halves],
        args=(*gs, *rs), prefetch=(place,), comm=comm)
    return (out[:n], out[n:]), rest


def _sum_partials(name, place, owns, recvd, comm=None):
    n = len(owns)

    def body(pref, *refs):
        o_refs, r_refs, out_refs = refs[:n], refs[n:2 * n], refs[2 * n:]
        for a in range(n):
            acc = o_refs[a][...]
            for r in range(N_CHIP - 1):
                acc = acc + r_refs[a][r].astype(F32)
            out_refs[a][...] = acc

    full = lambda a: pl.BlockSpec(a.shape, lambda i, pref: (0,) * a.ndim)
    return _call(
        body, name=name, grid=(1,), in_specs=[full(a) for a in list(owns) + list(recvd)],
        out_specs=[pl.BlockSpec(o.shape, lambda i, pref: (pref[0], 0)) for o in owns],
        out_shape=[jax.ShapeDtypeStruct((2 * o.shape[0], o.shape[1]), F32) for o in owns],
        args=(*owns, *recvd), prefetch=(place,), comm=comm)


def _adamw_math(w, g, m, v):
    m = ADAM_B1 * m + (1.0 - ADAM_B1) * g
    v = ADAM_B2 * v + (1.0 - ADAM_B2) * (g * g)
    m_hat = m / (1.0 - ADAM_B1 ** ADAM_STEP)
    v_hat = v / (1.0 - ADAM_B2 ** ADAM_STEP)
    delta = -ADAM_LR * (m_hat / (jnp.sqrt(v_hat) + ADAM_EPS) + ADAM_WD * w)
    return delta, m, v


def _row_tile(rows):
    for t in (512, 352, 256, 128):
        if rows % t == 0:
            return t
    return rows


def _adamw(name, wgmv, steps, after=()):
    n = len(wgmv)

    def body(*refs):
        ins, outs = refs[:4 * n], refs[4 * n:]
        for i in range(n):
            w_ref, g_ref, m_ref, v_ref = ins[4 * i:4 * i + 4]
            d_ref, nm_ref, nv_ref = outs[3 * i:3 * i + 3]
            d_ref[...], nm_ref[...], nv_ref[...] = _adamw_math(w_ref[...], g_ref[...], m_ref[...], v_ref[...])

    in_specs, out_specs, out_shape, args = [], [], [], []
    for w, g, m, v in wgmv:
        rows, cols = w.shape
        spec = pl.BlockSpec((rows // steps, cols), lambda i: (i, 0))
        in_specs += [spec] * 4
        out_specs += [spec] * 3
        out_shape += [jax.ShapeDtypeStruct(w.shape, F32)] * 3
        args += [w, g, m, v]
    res, _ = _call(body, name=name, grid=(steps,), in_specs=in_specs, out_specs=out_specs, out_shape=out_shape,
                   args=args, after=after)
    return [res[3 * i:3 * i + 3] for i in range(n)]


def _adamw_ada(place, cact, dmod, w, m, v, after=()):
    rows, cols = w.shape
    tr = _row_tile(rows)

    def body(pref, ca_ref, dm_ref, w_ref, m_ref, v_ref, g_ref, d_ref, nm_ref, nv_ref):
        g = lax.dot_general(ca_ref[...], dm_ref[...], (((0,), (0,)), ((), ())), preferred_element_type=F32,
                            precision=lax.Precision.HIGHEST)
        g_ref[...] = g
        d_ref[...], nm_ref[...], nv_ref[...] = _adamw_math(w_ref[...], g, m_ref[...], v_ref[...])

    spec = pl.BlockSpec((tr, cols), lambda i, pref: (i, 0))
    return _call(
        body, name="adamw_ada", grid=(rows // tr,),
        in_specs=[pl.BlockSpec((N_DEV, tr), lambda i, pref: (0, i)),
                  pl.BlockSpec((N_DEV, cols), lambda i, pref: (0, pref[1])), spec, spec, spec],
        out_specs=[spec] * 4, out_shape=[jax.ShapeDtypeStruct(w.shape, F32)] * 4,
        args=(cact, dmod, w, m, v), prefetch=(place,), after=after)[0]


def _adamw_small(place, owns, gathered, wmv):
    nw = len(wmv)
    flat = [a for t in wmv for a in t]

    def body(pref, *refs):
        own_refs, all_refs, refs = refs[:5], refs[5:10], refs[10:]
        w_refs = refs[:3 * nw]
        loss_ref, dmod_ref = refs[3 * nw], refs[3 * nw + 1]
        o_refs = refs[3 * nw + 2:]
        j = pref[1]
        me = 2 * pref[1] + pref[0]

        def total(i):
            acc = None
            for b in range(N_DEV):
                blk = jnp.where(me == b, own_refs[i][...], all_refs[i][b])
                acc = blk if acc is None else acc + blk
            return acc

        vf, vd, vc, ddw, gwg = [total(i) for i in range(5)]
        loss_ref[...] = (0.5 / D_MODEL) * jnp.sum(vf[5:6, :], axis=1, keepdims=True)
        order = ((1, 1), (1, 2), (1, 0), (0, 2), (0, 3), (0, 1))
        for b in range(N_DEV):
            for q, (i, row) in enumerate(order):
                dmod_ref[b:b + 1, q * D_MODEL:(q + 1) * D_MODEL] = jnp.where(
                    me == b, own_refs[i][row:row + 1, :], all_refs[i][b, row:row + 1, :])
        dm = dmod_ref[...]
        g_bada = dm[0:1, :]
        for b in range(1, N_DEV):
            g_bada = g_bada + dm[b:b + 1, :]
        g_dww = jnp.zeros((32, POOL_G), F32)
        for k in range(N_CHIP):
            g_dww = g_dww + jnp.where(j == k, ddw[:, k * POOL_G:(k + 1) * POOL_G], 0.0)
        grads = [g_bada, vd[3:4, :], g_dww, vc[3:4, :], vc[1:2, :], vc[2:3, :], gwg, vc[0:1, :], vf[4:5, :],
                 vf[0:1, :]]
        for i, g in enumerate(grads):
            w_ref, m_ref, v_ref = w_refs[3 * i:3 * i + 3]
            d, nm, nv = _adamw_math(w_ref[...], g, m_ref[...], v_ref[...])
            o_refs[4 * i][...] = g
            o_refs[4 * i + 1][...] = d
            o_refs[4 * i + 2][...] = nm
            o_refs[4 * i + 3][...] = nv

    outs = [jax.ShapeDtypeStruct((1, 1), F32), jax.ShapeDtypeStruct((N_DEV, 6 * D_MODEL), F32)]
    for w, _, _ in wmv:
        outs += [jax.ShapeDtypeStruct(w.shape, F32)] * 4
    full = lambda a: pl.BlockSpec(a.shape, lambda i, pref: (0,) * a.ndim)
    args = list(owns) + list(gathered) + flat
    res, _ = _call(body, name="adamw_small", grid=(1,), in_specs=[full(a) for a in args],
                   out_specs=[full(o) for o in outs], out_shape=outs, args=args, prefetch=(place,))
    return res[0], res[1], [res[2 + 4 * i:6 + 4 * i] for i in range(nw)]


def kernel(x, c, w_ada, b_ada, g_norm1, w_in, dw_w, dw_b, conv_ln_g, conv_ln_b, w_conv_pw, w_pool_group, pool_scale, w_out, g_norm2, w_ffn_gate, w_ffn_up, w_ffn_down, g_final, loss_target, m_w_ada, m_b_ada, m_g_norm1, m_w_in, m_dw_w, m_dw_b, m_conv_ln_g, m_conv_ln_b, m_w_conv_pw, m_w_pool_group, m_pool_scale, m_w_out, m_g_norm2, m_w_ffn_gate, m_w_ffn_up, m_w_ffn_down, m_g_final, v_w_ada, v_b_ada, v_g_norm1, v_w_in, v_dw_w, v_dw_b, v_conv_ln_g, v_conv_ln_b, v_w_conv_pw, v_w_pool_group, v_pool_scale, v_w_out, v_g_norm2, v_w_ffn_gate, v_w_ffn_up, v_w_ffn_down, v_g_final):
    xi, yi, ci = _place()
    place = jnp.stack([ci, 2 * xi + yi]).astype(jnp.int32)
    n_ada = w_ada.shape[2]

    tr = lambda a: jnp.transpose(a[0])
    mixer_shards = [w_in[0], w_conv_pw[0], w_out[0]]
    ffn_shards = [tr(w_ffn_gate), tr(w_ffn_up), w_ffn_down[0]]
    slots = [lax.empty((N_CHIP,) + a.shape, MXU_DTYPE) for a in mixer_shards] + [lax.empty((N_CHIP,) + dw_w.shape[1:], F32)]

    (mod4, cact, wg_b, (b_gate, b_up, b_down)), (win_g, wpw_g, wout_g, dww_g) = _ada_fwd(
        c, w_ada[0], b_ada.reshape(N_CHIP, n_ada), mixer_shards, ffn_shards, dw_w[0], w_pool_group[0],
        comm=_weights_gather(slots, [True, True, True, False]))
    mod = mod4.reshape(6, D_MODEL)
    dww_full = jnp.pad(jnp.concatenate([dww_g[k] for k in range(N_CHIP)], axis=1), ((0, 1), (0, 0)))
    w_pw = wpw_g.reshape(CONV_W, CONV_W)
    w_o = wout_g.reshape(D_MODEL, D_MODEL)
    xs, tgt, gf = x[0], loss_target[0], g_final.reshape(1, D_MODEL)
    s = xs.shape[0]
    fb = b_gate.shape[1]

    (x2, y, u, z, rstd, p, ycat), (wgate_g, wup_g, wdown_g) = _mixer_fwd(
        xs, mod, g_norm1, win_g, dww_full, dw_b, conv_ln_g, conv_ln_b, w_pw, wg_b, pool_scale, w_o,
        comm=_weights_gather([b_gate, b_up, b_down], [True, True, True]))
    dx2, h2, df, act, dgg, duu, vec_f = _ffn(x2, tgt, mod, g_norm2, gf, wgate_g, wup_g, wdown_g)

    whole = lambda w: pl.BlockSpec((s, w), lambda j: (0, 0))
    cols = lambda w: pl.BlockSpec((s, w), lambda j: (0, j))
    hid = pl.BlockSpec((1, s, fb), lambda j: (j, 0, 0))
    c_gate, _ = _dw("dw_gate", dgg, hid, h2, whole(D_MODEL), N_CHIP, fb, D_MODEL)
    c_up, (r_gate,) = _dw("dw_up", duu, hid, h2, whole(D_MODEL), N_CHIP, fb, D_MODEL, comm=_sibling_halves([c_gate]))
    ((pb_gate,), (own_gate,)), _ = _chip_partials("partials_gate", place, [c_gate], [r_gate])
    ex_gate = _exchange_parts([pb_gate])
    st_gate, tok_gate = _split_start("exchange_gate_start", [pb_gate], *ex_gate)
    c_down, (r_up,) = _dw("dw_down", act, hid, df, whole(D_MODEL), N_CHIP, fb, D_MODEL,
                          comm=_sibling_halves([c_up]), after=(tok_gate,))
    ((pb_up,), (own_up,)), (r_down,) = _chip_partials("partials_up", place, [c_up], [r_up],
                                                      comm=_sibling_halves([c_down]))
    ((pb_down,), (own_down,)), _ = _chip_partials("partials_down", place, [c_down], [r_down])
    ex_ud = _exchange_parts([pb_up, pb_down])
    st_ud, tok_ud = _split_start("exchange_up_down_start", [pb_up, pb_down], *ex_ud)
    (gx, h1, du, dy, sw, dyc, dyp, vec_d, vec_c, ddw), _ = _mixer_bwd(
        dx2, xs, y, u, z, rstd, p, mod, g_norm1, win_g, dww_full, conv_ln_g, conv_ln_b, w_pw, wg_b, pool_scale, w_o,
        after=(tok_ud,))

    g_wg, c_out, c_pw, c_in = _dw_mixer(p, dyp, ycat, dy, sw, dyc, h1, du)
    small_own = [vec_f, vec_d, vec_c, ddw, g_wg]
    ex_small = _small_parts(small_own)
    st_small, tok_small = _split_start("small_grads_start", small_own, *ex_small, zeroed=True)
    (rc_gate,) = _split_wait("exchange_gate_wait", st_gate, 1, ex_gate[1], ex_gate[2], after=(c_in, tok_small))
    rc_up, rc_down = _split_wait("exchange_up_down_wait", st_ud, 2, ex_ud[1], ex_ud[2], after=(c_in, rc_gate))

    ffn_fulls, (r_in, r_pw, r_out) = _sum_partials(
        "sum_ffn", place, [own_gate, own_up, own_down], [rc_gate, rc_up, rc_down],
        comm=_sibling_halves([c_in, c_pw, c_out]))
    (pbs_mix, owns_mix), (g_gate, g_up, g_down) = _chip_partials(
        "partials_mix", place, [c_in, c_pw, c_out], [r_in, r_pw, r_out], comm=_join_halves(ffn_fulls))
    small_all = _split_wait("small_grads_wait", st_small, len(small_own), ex_small[1], ex_small[2],
                            after=(g_down,))

    pad_rows = lambda a: jnp.pad(a[0], ((0, 1), (0, 0)))
    row = lambda a: a.reshape(1, -1)
    small = [(b_ada, m_b_ada, v_b_ada), (g_norm1, m_g_norm1, v_g_norm1),
             (pad_rows(dw_w), pad_rows(m_dw_w), pad_rows(v_dw_w)), (dw_b, m_dw_b, v_dw_b),
             (conv_ln_g, m_conv_ln_g, v_conv_ln_g), (conv_ln_b, m_conv_ln_b, v_conv_ln_b),
             (w_pool_group[0], m_w_pool_group[0], v_w_pool_group[0]), (pool_scale, m_pool_scale, v_pool_scale),
             (g_norm2, m_g_norm2, v_g_norm2), (row(g_final), row(m_g_final), row(v_g_final))]
    loss, dmod, small_out = _adamw_small(place, small_own, small_all, small)
    (o_bada, o_g1, o_dww, o_dwb, o_lng, o_lnb, o_wg, o_ps, o_g2, o_gf) = small_out
    o_dww = [a[:CONV_K] for a in o_dww]
    o_gf = [a.reshape(D_MODEL) for a in o_gf]
    lead = lambda outs: [a[None] for a in outs]

    lands, sem_shape, copies = _exchange_parts(pbs_mix)
    state, token = _split_start("exchange_mix_start", pbs_mix, lands, sem_shape, copies)
    u_gate, u_up, u_down = _adamw(
        "adamw_ffn", [(tr(w_ffn_gate), g_gate, tr(m_w_ffn_gate), tr(v_w_ffn_gate)),
                      (tr(w_ffn_up), g_up, tr(m_w_ffn_up), tr(v_w_ffn_up)),
                      (w_ffn_down[0], g_down, m_w_ffn_down[0], v_w_ffn_down[0])],
        steps=4, after=(token,))
    o_gate = [jnp.transpose(o) for o in [g_gate] + list(u_gate)]
    o_up = [jnp.transpose(o) for o in [g_up] + list(u_up)]
    o_down = [g_down] + list(u_down)
    o_ada = _adamw_ada(place, cact, dmod, w_ada[0], m_w_ada[0], v_w_ada[0], after=(token,))
    rc_mix = _split_wait("exchange_mix_wait", state, len(pbs_mix), sem_shape, copies, after=(o_ada[1], u_down[0]))
    mix_fulls, _ = _sum_partials("sum_mix", place, owns_mix, rc_mix)
    g_in, g_pw, g_out = _comm_only("join_mix", _join_halves(mix_fulls))
    u_in, u_pw, u_out = _adamw(
        "adamw_mix", [(w_in[0], g_in, m_w_in[0], v_w_in[0]), (w_conv_pw[0], g_pw, m_w_conv_pw[0], v_w_conv_pw[0]),
                      (w_out[0], g_out, m_w_out[0], v_w_out[0])], steps=4)
    o_in, o_pw, o_out = [g_in] + list(u_in), [g_pw] + list(u_pw), [g_out] + list(u_out)

    per_weight = [lead(o_ada), o_bada, o_g1, lead(o_in), lead(o_dww), o_dwb, o_lng, o_lnb, lead(o_pw), lead(o_wg),
                  o_ps, lead(o_out), o_g2, lead(o_gate), lead(o_up), lead(o_down), o_gf]
    result = [loss.reshape(()), gx[None]]
    for kind in range(4):
        result += [o[kind] for o in per_weight]
    return tuple(result)
```

```python
import functools

import jax
import jax.numpy as jnp
from jax import lax
from jax.experimental import pallas as pl
from jax.experimental.pallas import tpu as pltpu

F32 = jnp.float32
MXU_DTYPE = jnp.bfloat16
EPS = 1e-6

D_MODEL = 1024
CONV_W = 512
POOL_W = 512
CONV_K = 31
POOL_WINDOWS = (2, 4, 8, 16)
POOL_G = 128
IN_W = 2 * CONV_W + POOL_W
N_CHIP = 4
N_DEV = 8
CONV_HALO = 32
POOL_HALO = 16

ADAM_LR = 0.001
ADAM_B1 = 0.9
ADAM_B2 = 0.999
ADAM_EPS = 1e-08
ADAM_WD = 0.01
ADAM_STEP = 10

MESH = pl.DeviceIdType.MESH
ANY = pl.BlockSpec(memory_space=pl.ANY)
VMEM = pl.BlockSpec(memory_space=pltpu.VMEM)


def _dot(a, b):
    return jnp.dot(a.astype(MXU_DTYPE), b.astype(MXU_DTYPE), preferred_element_type=F32)


def _dot_nt(a, b):
    return lax.dot_general(a.astype(MXU_DTYPE), b.astype(MXU_DTYPE), (((1,), (1,)), ((), ())),
                           preferred_element_type=F32)


def _dot_tn(a, b):
    return lax.dot_general(a.astype(MXU_DTYPE), b.astype(MXU_DTYPE), (((0,), (0,)), ((), ())),
                           preferred_element_type=F32)


def _sigmoid(v):
    return 1.0 / (1.0 + jnp.exp(-v))


def _full(shape):
    n = len(shape)
    return pl.BlockSpec(shape, lambda *_: (0,) * n)


def _token_tile(s):
    return 256 if s % 256 == 0 else s


SUBLANES = 8


def _row_shifts(pad_ref, shifted_ref, rows):
    for r in range(1, SUBLANES):
        shifted_ref[r - 1] = pad_ref[r:r + rows, :]

    def window(i, n):
        r, base = i % SUBLANES, i - i % SUBLANES
        if r == 0:
            return pad_ref[base:base + n, :]
        return shifted_ref[r - 1, base:base + n, :]

    return window


def _place():
    return lax.axis_index("x"), lax.axis_index("y"), lax.axis_index("c")


def _flip(x, y, r):
    return ((1 - x) if r & 2 else x, (1 - y) if r & 1 else y)


def _remote(src, dst, send_sem, recv_sem, dev):
    return pltpu.make_async_remote_copy(src_ref=src, dst_ref=dst, send_sem=send_sem, recv_sem=recv_sem,
                                        device_id=dev, device_id_type=MESH)


class _Comm:
    def __init__(self, ins, outs, aliases, scratch, start, finish, mid=None):
        self.ins, self.outs, self.aliases, self.scratch = list(ins), list(outs), dict(aliases), list(scratch)
        self.start, self.finish = start, finish
        self.mid = mid


def _both(a, b):
    na, nao, nas = len(a.ins), len(a.outs), len(a.scratch)
    aliases = dict(a.aliases)
    aliases.update({na + i: nao + o for i, o in b.aliases.items()})

    def start(ins, outs, scr):
        a.start(ins[:na], outs[:nao], scr[:nas])
        b.start(ins[na:], outs[nao:], scr[nas:])

    def finish(ins, outs, scr):
        a.finish(ins[:na], outs[:nao], scr[:nas])
        b.finish(ins[na:], outs[nao:], scr[nas:])

    def mid(ins, outs, scr):
        if a.mid:
            a.mid(ins[:na], outs[:nao], scr[:nas])
        if b.mid:
            b.mid(ins[na:], outs[nao:], scr[nas:])

    return _Comm(a.ins + b.ins, a.outs + b.outs, aliases, a.scratch + b.scratch, start, finish,
                 mid if (a.mid or b.mid) else None)


def _call(body, *, name, grid, in_specs, out_specs, out_shape, args, scratch_shapes=(), prefetch=(), comm=None,
          body_starts=False, after=()):
    in_specs = list(in_specs) + [ANY] * len(after)
    args = list(args) + list(after)
    n_pre, n_in, n_out, n_scr = len(prefetch), len(in_specs), len(out_specs), len(scratch_shapes)
    n_body_in = n_in - len(after)
    c_ins = comm.ins if comm else []
    c_outs = comm.outs if comm else []
    c_scr = comm.scratch if comm else []
    last = grid[0] - 1

    def wrapped(*refs):
        pre, refs = refs[:n_pre], refs[n_pre:]
        ins, cin = refs[:n_body_in], refs[n_in:n_in + len(c_ins)]
        refs = refs[n_in + len(c_ins):]
        outs, cout = refs[:n_out], refs[n_out:n_out + len(c_outs)]
        refs = refs[n_out + len(c_outs):]
        scr, cscr = refs[:n_scr], refs[n_scr:]
        step = pl.program_id(0)
        if comm and not body_starts:
            @pl.when(step == 0)
            def _():
                comm.start(cin, cout, cscr)

        has_mid = comm is not None and comm.mid is not None
        mid_step = grid[0] // 2 if grid[0] >= 4 else None
        if has_mid and mid_step is not None:
            @pl.when(step == mid_step)
            def _():
                comm.mid(cin, cout, cscr)

        if body_starts:
            body(lambda: comm.start(cin, cout, cscr), cout, *pre, *ins, *outs, *scr)
        else:
            body(*pre, *ins, *outs, *scr)
        if comm:
            @pl.when(step == last)
            def _():
                if has_mid and mid_step is None:
                    comm.mid(cin, cout, cscr)
                comm.finish(cin, cout, cscr)

    aliases = {n_pre + n_in + a: n_out + b for a, b in (comm.aliases if comm else {}).items()}
    res = pl.pallas_call(
        wrapped, name=name,
        grid_spec=pltpu.PrefetchScalarGridSpec(
            num_scalar_prefetch=n_pre, grid=grid, in_specs=list(in_specs) + [ANY] * len(c_ins),
            out_specs=list(out_specs) + [ANY] * len(c_outs), scratch_shapes=list(scratch_shapes) + list(c_scr)),
        out_shape=list(out_shape) + list(c_outs),
        input_output_aliases=aliases,
        compiler_params=pltpu.CompilerParams(dimension_semantics=("arbitrary",)),
    )(*prefetch, *args, *c_ins)
    return res[:n_out], res[n_out:]


def _comm_only(name, comm):
    return _call(lambda: None, name=name, grid=(1,), in_specs=[], out_specs=[], out_shape=[], args=[], comm=comm)[1]


def _weights_gather(bufs, split):
    n = len(bufs)

    def ctx(outs):
        x, y, cc = _place()
        chips = dict(me=2 * x + y, y=2 * x + (1 - y), x=2 * (1 - x) + y, d=2 * (1 - x) + (1 - y))
        devs = dict(y=(x, 1 - y, cc), x=(1 - x, y, cc), d=(1 - x, 1 - y, cc), s=(x, y, 1 - cc))

        def piece(a, kj, pc, q=None):
            if not split[a]:
                return outs[a].at[kj]
            h = bufs[a].shape[1] // 2
            if q is None:
                return outs[a].at[kj, pl.ds(pc * h, h), :]
            return outs[a].at[kj, pl.ds(pc * h + q * (h // 2), h // 2), :]

        return cc, chips, devs, piece

    def directs(a, outs, send, recv):
        cc, chips, devs, piece = ctx(outs)
        if not split[a]:
            whole = piece(a, chips["me"], cc)
            return [_remote(whole, whole, send.at[a, k], recv.at[a, k], devs[t]) for k, t in ((0, "y"), (2, "x"), (4, "d"))]
        q = lambda i: piece(a, chips["me"], cc, i)
        return [_remote(q(0), q(0), send.at[a, 0], recv.at[a, 0], devs["y"]),
                _remote(q(1), q(1), send.at[a, 3], recv.at[a, 3], devs["x"]),
                _remote(q(1), q(1), send.at[a, 1], recv.at[a, 1], devs["y"]),
                _remote(q(0), q(0), send.at[a, 2], recv.at[a, 2], devs["x"])]

    def landed(a, k, outs, send, recv):
        cc, chips, devs, piece = ctx(outs)
        if not split[a]:
            got = piece(a, chips[{0: "y", 2: "x", 4: "d"}[k]], cc)
        elif k < 6:
            got = piece(a, chips[("y", "y", "x", "x", "d", "d")[k]], cc, (0, 1, 0, 1, 0, 1)[k])
        else:
            got = piece(a, chips[("y", "x", "d")[k - 6]], 1 - cc)
        return _remote(got, got, send.at[a, k], recv.at[a, k], devs["s"])

    def passed_on(a, outs, send, recv):
        cc, chips, devs, piece = ctx(outs)
        from_y, from_x = piece(a, chips["y"], cc, 0), piece(a, chips["x"], cc, 1)
        return [_remote(from_y, from_y, send.at[a, 4], recv.at[a, 4], devs["x"]),
                _remote(from_x, from_x, send.at[a, 5], recv.at[a, 5], devs["y"])]

    def to_sibling(a, outs, send, recv, which=(0, 1, 2)):
        cc, chips, devs, piece = ctx(outs)
        halves = [piece(a, chips[("y", "x", "d")[i]], cc) for i in which]
        return [_remote(hf, hf, send.at[a, 6 + i], recv.at[a, 6 + i], devs["s"]) for i, hf in zip(which, halves)]

    def start(ins, outs, scr):
        send, recv = scr
        per_item = [directs(a, outs, send, recv) for a in range(n)]
        for rank in range(4):
            for cps in per_item:
                if rank < len(cps):
                    cps[rank].start()

    def mid(ins, outs, scr):
        send, recv = scr
        for a in range(n):
            if split[a]:
                fy, fx = passed_on(a, outs, send, recv)
                landed(a, 0, outs, send, recv).wait_recv()
                fy.start()
                landed(a, 3, outs, send, recv).wait_recv()
                fx.start()

    def finish(ins, outs, scr):
        send, recv = scr
        for a in range(n):
            if split[a]:
                to_y, to_x = to_sibling(a, outs, send, recv, which=(0, 1))
                landed(a, 1, outs, send, recv).wait_recv()
                to_y.start()
                landed(a, 2, outs, send, recv).wait_recv()
                to_x.start()
        for a in range(n):
            if split[a]:
                for k in (4, 5):
                    landed(a, k, outs, send, recv).wait_recv()
                to_sibling(a, outs, send, recv, which=(2,))[0].start()
            else:
                for k in (0, 2, 4):
                    landed(a, k, outs, send, recv).wait_recv()
        for a in range(n):
            if split[a]:
                for k in (6, 7, 8):
                    landed(a, k, outs, send, recv).wait_recv()
            cps = directs(a, outs, send, recv)
            if split[a]:
                cps += passed_on(a, outs, send, recv) + to_sibling(a, outs, send, recv)
            for cp in cps:
                cp.wait_send()

    return _Comm(bufs, [jax.ShapeDtypeStruct(b.shape, b.dtype) for b in bufs], {i: i for i in range(n)},
                 [pltpu.SemaphoreType.DMA((n, 9)), pltpu.SemaphoreType.DMA((n, 9))], start, finish, mid)


HBM =pl.BlockSpec(memory_space=pltpu.HBM)
SEM = pl.BlockSpec(memory_space=pltpu.SEMAPHORE)
DATAFLOW = pltpu.SideEffectType.DATAFLOW_SIDE_EFFECTING


class _SemGrid:
    def __init__(self, refs, cols):
        self.refs, self.cols = refs, cols

    @property
    def at(self):
        return self

    def __getitem__(self, idx):
        return self.refs[idx[0] * self.cols + idx[1]]


def _split_start(name, srcs, lands, sem_shape, copies, zeroed=False):
    n, k = len(srcs), len(lands)
    ns = sem_shape[0] * sem_shape[1]

    def body(*refs):
        src_refs, land_refs = refs[:n], refs[n:n + k]
        send = _SemGrid(refs[n + k:n + k + ns], sem_shape[1])
        recv = _SemGrid(refs[n + k + ns:n + k + 2 * ns], sem_shape[1])
        token = refs[-1]
        for cp in copies(src_refs, land_refs, send, recv):
            cp.start()
        token[...] = jnp.zeros(token.shape, F32)

    hbm = lambda a: pltpu.with_memory_space_constraint(a, pltpu.HBM)
    zones = [jnp.zeros(l.shape, l.dtype) if zeroed else lax.empty(l.shape, l.dtype) for l in lands]
    out = pl.pallas_call(
        body, name=name,
        out_shape=[pltpu.SemaphoreType.DMA(())] * (2 * ns)
        + [pltpu.HBM(a.shape, a.dtype) for a in list(srcs) + list(lands)] + [jax.ShapeDtypeStruct((8, 128), F32)],
        in_specs=[HBM] * (n + k), out_specs=[SEM] * (2 * ns) + [HBM] * (n + k) + [VMEM],
        input_output_aliases={i: 2 * ns + i for i in range(n + k)},
        compiler_params=pltpu.CompilerParams(has_side_effects=DATAFLOW),
    )(*[hbm(a) for a in srcs], *[hbm(z) for z in zones])
    return out[:-1], out[-1]


def _split_wait(name, state, n, sem_shape, copies, after, in_place=False):
    ns = sem_shape[0] * sem_shape[1]
    sems, bufs = state[:2 * ns], state[2 * ns:]
    k = len(bufs) - n

    def body(*refs):
        src_refs, land_refs = refs[:n], refs[n:n + k]
        send = _SemGrid(refs[n + k:n + k + ns], sem_shape[1])
        recv = _SemGrid(refs[n + k + ns:n + k + 2 * ns], sem_shape[1])
        cps = copies(src_refs, land_refs, send, recv)
        for cp in cps:
            cp.wait_send()
        for cp in cps:
            cp.wait_recv()

    out = pl.pallas_call(
        body, name=name,
        out_shape=[pltpu.HBM(a.shape, a.dtype) for a in bufs],
        in_specs=[HBM] * (n + k) + [SEM] * (2 * ns) + [ANY] * len(after), out_specs=[HBM] * (n + k),
        input_output_aliases={i: i for i in range(n + k)},
        compiler_params=pltpu.CompilerParams(has_side_effects=DATAFLOW),
    )(*bufs, *sems, *after)
    return out[:n] if in_place else out[n:]


def _direct_phases(copies):
    def start(ins, outs, scr):
        for cp in copies(ins, outs, *scr):
            cp.start()

    def finish(ins, outs, scr):
        cps = copies(ins, outs, *scr)
        for cp in cps:
            cp.wait_recv()
        for cp in cps:
            cp.wait_send()

    return start, finish


def _sibling_parts(gs):
    n = len(gs)

    def copies(ins, outs, send, recv):
        x, y, cc = _place()
        cps = []
        for a in range(n):
            h = gs[a].shape[1] // 2
            cps.append(_remote(ins[a].at[:, pl.ds((1 - cc) * h, h), :], outs[a], send.at[a, 0], recv.at[a, 0],
                               (x, y, 1 - cc)))
        return cps

    return [jax.ShapeDtypeStruct((N_CHIP, g.shape[1] // 2, g.shape[2]), F32) for g in gs], (n, 1), copies


def _sibling_halves(gs):
    lands, sem_shape, copies = _sibling_parts(gs)
    start, finish = _direct_phases(copies)
    return _Comm(gs, lands, {}, [pltpu.SemaphoreType.DMA(sem_shape), pltpu.SemaphoreType.DMA(sem_shape)],
                 start, finish)


def _exchange_parts(pbs):
    n = len(pbs)

    def copies(ins, outs, send, recv):
        x, y, cc = _place()
        cps = []
        for a in range(n):
            for r in range(1, N_CHIP):
                kx, ky = _flip(x, y, r)
                cps.append(_remote(ins[a].at[2 * kx + ky], outs[a].at[r - 1], send.at[a, r - 1], recv.at[a, r - 1],
                                   (kx, ky, cc)))
        return cps

    lands = [jax.ShapeDtypeStruct((N_CHIP - 1,) + p.shape[1:], p.dtype) for p in pbs]
    return lands, (n, N_CHIP - 1), copies


def _small_parts(arrs):
    n = len(arrs)

    def copies(ins, outs, send, recv):
        x, y, cc = _place()
        b = 4 * x + 2 * y + cc
        cps = []
        for a in range(n):
            for r in range(1, N_DEV):
                dev = ((1 - x) if r & 4 else x, (1 - y) if r & 2 else y, (1 - cc) if r & 1 else cc)
                cps.append(_remote(ins[a], outs[a].at[b], send.at[a, r - 1], recv.at[a, r - 1], dev))
        return cps

    lands = [jax.ShapeDtypeStruct((N_DEV,) + a.shape, a.dtype) for a in arrs]
    return lands, (n, N_DEV - 1), copies


def _exchange_partials(pbs):
    lands, sem_shape, copies = _exchange_parts(pbs)
    start, finish = _direct_phases(copies)
    return _Comm(pbs, lands, {}, [pltpu.SemaphoreType.DMA(sem_shape), pltpu.SemaphoreType.DMA(sem_shape)],
                 start, finish)


def _join_parts(fulls):
    n = len(fulls)

    def copies(bufs, send, recv):
        x, y, cc = _place()
        cps = []
        for a in range(n):
            h = fulls[a].shape[0] // 2
            mine = bufs[a].at[pl.ds(cc * h, h), :]
            cps.append(_remote(mine, mine, send.at[a, 0], recv.at[a, 0], (x, y, 1 - cc)))
        return cps

    return (n, 1), copies


def _join_halves(fulls):
    sem_shape, copies = _join_parts(fulls)
    start, finish = _direct_phases(lambda ins, outs, send, recv: copies(outs, send, recv))
    return _Comm(fulls, [jax.ShapeDtypeStruct(f.shape, F32) for f in fulls], {i: i for i in range(len(fulls))},
                 [pltpu.SemaphoreType.DMA(sem_shape), pltpu.SemaphoreType.DMA(sem_shape)], start, finish)


def _mixer_fwd(x, mod, g1, w_in, dww, dwb, lng, lnb, w_pw, wg, pscale, w_out, comm=None):
    s = x.shape[0]
    ts = _token_tile(s)
    nt = s // ts

    def body(x_ref, mod_ref, g1_ref, win_ref, dww_ref, dwb_ref, lng_ref, lnb_ref, wpw_ref, wg_ref, ps_ref,
             wout_ref, x2_ref, y_ref, u_ref, z_ref, rstd_ref, p_ref, ycat_ref, gpad, vpad, gshift):
        i = pl.program_id(0)

        @pl.when(i == 0)
        def _():
            gpad[0:CONV_HALO, :] = jnp.zeros((CONV_HALO, CONV_W), F32)
            vpad[0:POOL_HALO, :] = jnp.zeros((POOL_HALO, POOL_W), F32)

        xt = x_ref[...]
        sh1 = mod_ref[0:1, :]
        sc1 = mod_ref[1:2, :]
        gt1 = mod_ref[2:3, :]
        r1 = lax.rsqrt(jnp.mean(xt * xt, axis=-1, keepdims=True) + EPS)
        h1 = (xt * r1 * g1_ref[...]) * (1.0 + sc1) + sh1
        h1b = h1.astype(MXU_DTYPE)
        u = jnp.concatenate([_dot(h1b, win_ref[j]) for j in range(N_CHIP)], axis=1)
        u_ref[...] = u
        a = u[:, :CONV_W]
        g = u[:, CONV_W:2 * CONV_W]
        v = u[:, 2 * CONV_W:]

        gpad[CONV_HALO:CONV_HALO + ts, :] = a * _sigmoid(g)
        window = _row_shifts(gpad, gshift, ts + CONV_HALO - SUBLANES)
        cv = jnp.broadcast_to(dwb_ref[...], (ts, CONV_W))
        off = CONV_HALO - (CONV_K - 1)
        for k in range(CONV_K):
            cv = cv + dww_ref[k:k + 1, :] * window(off + k, ts)
        gpad[0:CONV_HALO, :] = gpad[ts:ts + CONV_HALO, :]

        mu = jnp.mean(cv, axis=-1, keepdims=True)
        cc = cv - mu
        rstd = lax.rsqrt(jnp.mean(cc * cc, axis=-1, keepdims=True) + EPS)
        z = cc * rstd
        z_ref[...] = z
        rstd_ref[...] = rstd
        ln = z * lng_ref[...] + lnb_ref[...]
        sw = ln * _sigmoid(ln)
        yconv = _dot(sw, wpw_ref[...])

        vpad[POOL_HALO:POOL_HALO + ts, :] = v
        t = i * ts + lax.broadcasted_iota(jnp.int32, (ts, 1), 0)
        ps, ypool = [], []
        for gi, w in enumerate(POOL_WINDOWS):
            cols = slice(gi * POOL_G, (gi + 1) * POOL_G)
            acc = vpad[POOL_HALO:POOL_HALO + ts, cols]
            for d in range(1, w):
                acc = acc + vpad[POOL_HALO - d:POOL_HALO - d + ts, cols]
            cnt = jnp.minimum(t + 1, w).astype(F32)
            pg = (acc / cnt - v[:, cols]).astype(MXU_DTYPE)
            ps.append(pg)
            ypool.append(_dot(pg, wg_ref[gi]))
        vpad[0:POOL_HALO, :] = vpad[ts:ts + POOL_HALO, :]
        p_ref[...] = jnp.concatenate(ps, axis=1)
        ypool = jnp.concatenate(ypool, axis=1) * ps_ref[...]

        ycat = jnp.concatenate([yconv, ypool], axis=1).astype(MXU_DTYPE)
        ycat_ref[...] = ycat
        y = _dot(ycat, wout_ref[...])
        y_ref[...] = y
        x2_ref[...] = xt + gt1 * y

    tile = lambda w: pl.BlockSpec((ts, w), lambda i: (i, 0))
    return _call(
        body, name="mixer_fwd", grid=(nt,),
        in_specs=[tile(D_MODEL), _full(mod.shape), _full(g1.shape), _full(w_in.shape), _full(dww.shape),
                  _full(dwb.shape), _full(lng.shape), _full(lnb.shape), _full(w_pw.shape), _full(wg.shape),
                  _full(pscale.shape), _full(w_out.shape)],
        out_specs=[tile(D_MODEL), tile(D_MODEL), tile(IN_W), tile(CONV_W), tile(1), tile(POOL_W), tile(D_MODEL)],
        out_shape=[jax.ShapeDtypeStruct((s, D_MODEL), F32), jax.ShapeDtypeStruct((s, D_MODEL), F32),
                   jax.ShapeDtypeStruct((s, IN_W), F32), jax.ShapeDtypeStruct((s, CONV_W), F32),
                   jax.ShapeDtypeStruct((s, 1), F32), jax.ShapeDtypeStruct((s, POOL_W), MXU_DTYPE),
                   jax.ShapeDtypeStruct((s, D_MODEL), MXU_DTYPE)],
        scratch_shapes=[pltpu.VMEM((ts + CONV_HALO, CONV_W), F32), pltpu.VMEM((ts + POOL_HALO, POOL_W), F32),
                        pltpu.VMEM((SUBLANES - 1, ts + CONV_HALO - SUBLANES, CONV_W), F32)],
        args=(x, mod, g1, w_in, dww, dwb, lng, lnb, w_pw, wg, pscale, w_out), comm=comm)


def _ffn(x2, tgt, mod, g2, gf, w_gate, w_up, w_down):
    s = x2.shape[0]
    ts = _token_tile(s)
    nt = s // ts
    fb = w_gate.shape[1]

    def body(x2_ref, tgt_ref, mod_ref, g2_ref, gf_ref, wgt_ref, wup_ref, wdn_ref,
             dx2_ref, h2_ref, df_ref, act_ref, dgg_ref, duu_ref, vec_ref, gg_s, uu_s):
        i = pl.program_id(0)

        @pl.when(i == 0)
        def _():
            vec_ref[...] = jnp.zeros(vec_ref.shape, F32)

        x2t = x2_ref[...]
        sh2 = mod_ref[3:4, :]
        sc2 = mod_ref[4:5, :]
        gt2 = mod_ref[5:6, :]
        g2v = g2_ref[...]
        gfv = gf_ref[...]
        r2 = lax.rsqrt(jnp.mean(x2t * x2t, axis=-1, keepdims=True) + EPS)
        xh2 = x2t * r2
        n2 = xh2 * g2v
        h2b = (n2 * (1.0 + sc2) + sh2).astype(MXU_DTYPE)
        h2_ref[...] = h2b
        f = jnp.zeros((ts, D_MODEL), F32)
        for j in range(N_CHIP):
            gg = _dot_nt(h2b, wgt_ref[j])
            uu = _dot_nt(h2b, wup_ref[j])
            gg_s[j] = gg
            uu_s[j] = uu
            actb = (gg * _sigmoid(gg) * uu).astype(MXU_DTYPE)
            act_ref[j] = actb
            f = f + _dot(actb, wdn_ref[j])
        x3 = x2t + gt2 * f
        r3 = lax.rsqrt(jnp.mean(x3 * x3, axis=-1, keepdims=True) + EPS)
        xh3 = x3 * r3
        diff = xh3 * gfv - tgt_ref[...]
        dout = diff * (1.0 / D_MODEL)
        dn3 = dout * gfv
        dx3 = r3 * (dn3 - xh3 * jnp.mean(dn3 * xh3, axis=-1, keepdims=True))
        dfb = (dx3 * gt2).astype(MXU_DTYPE)
        df_ref[...] = dfb
        dh2 = jnp.zeros((ts, D_MODEL), F32)
        for j in range(N_CHIP):
            dact = _dot_nt(dfb, wdn_ref[j])
            gg = gg_s[j]
            uu = uu_s[j]
            sg = _sigmoid(gg)
            duu = (dact * (gg * sg)).astype(MXU_DTYPE)
            dgg = (dact * uu * (sg * (1.0 + gg * (1.0 - sg)))).astype(MXU_DTYPE)
            duu_ref[j] = duu
            dgg_ref[j] = dgg
            dh2 = dh2 + _dot(dgg, wgt_ref[j]) + _dot(duu, wup_ref[j])
        dn2 = dh2 * (1.0 + sc2)
        dxh2 = dn2 * g2v
        dx2_ref[...] = dx3 + r2 * (dxh2 - xh2 * jnp.mean(dxh2 * xh2, axis=-1, keepdims=True))

        col = lambda a: jnp.sum(a, axis=0, keepdims=True)
        vec_ref[0:1, :] += col(dout * xh3)
        vec_ref[1:2, :] += col(dx3 * f)
        vec_ref[2:3, :] += col(dh2)
        vec_ref[3:4, :] += col(dh2 * n2)
        vec_ref[4:5, :] += col(dn2 * xh2)
        vec_ref[5:6, :] += col(diff * diff)

    tile = lambda w: pl.BlockSpec((ts, w), lambda i: (i, 0))
    tile3 = pl.BlockSpec((N_CHIP, ts, fb), lambda i: (0, i, 0))
    once = lambda a: pl.BlockSpec(a.shape, lambda i: (0,) * a.ndim, pipeline_mode=pl.Buffered(1))
    hid = jax.ShapeDtypeStruct((N_CHIP, s, fb), MXU_DTYPE)
    return pl.pallas_call(
        body, name="ffn", grid=(nt,),
        in_specs=[tile(D_MODEL), tile(D_MODEL), _full(mod.shape), _full(g2.shape), _full(gf.shape),
                  once(w_gate), once(w_up), once(w_down)],
        out_specs=[tile(D_MODEL), tile(D_MODEL), tile(D_MODEL), tile3, tile3, tile3, _full((8, D_MODEL))],
        out_shape=[jax.ShapeDtypeStruct((s, D_MODEL), F32), jax.ShapeDtypeStruct((s, D_MODEL), MXU_DTYPE),
                   jax.ShapeDtypeStruct((s, D_MODEL), MXU_DTYPE), hid, hid, hid,
                   jax.ShapeDtypeStruct((8, D_MODEL), F32)],
        scratch_shapes=[pltpu.VMEM((N_CHIP, ts, fb), F32), pltpu.VMEM((N_CHIP, ts, fb), F32)],
        compiler_params=pltpu.CompilerParams(dimension_semantics=("arbitrary",)),
    )(x2, tgt, mod, g2, gf, w_gate, w_up, w_down)


def _mixer_bwd(dx2, x, y, u, z, rstd, p, mod, g1, w_in, dww, lng, lnb, w_pw, wg, pscale, w_out, comm=None, after=()):
    s = x.shape[0]
    ts = _token_tile(s)
    nt = s // ts

    def body(dx2_ref, x_ref, y_ref, u_ref, z_ref, rstd_ref, p_ref, mod_ref, g1_ref, win_ref, dww_ref, lng_ref,
             lnb_ref, wpw_ref, wg_ref, ps_ref, wout_ref,
             gx_ref, h1_ref, du_ref, dy_ref, sw_ref, dyc_ref, dyp_ref, vd_ref, vc_ref, ddw_ref, dcpad, dppad,
             dshift):
        i = pl.program_id(0)
        tix = nt - 1 - i

        @pl.when(i == 0)
        def _():
            vd_ref[...] = jnp.zeros(vd_ref.shape, F32)
            vc_ref[...] = jnp.zeros(vc_ref.shape, F32)
            ddw_ref[...] = jnp.zeros(ddw_ref.shape, F32)
            dcpad[ts:ts + CONV_HALO, :] = jnp.zeros((CONV_HALO, CONV_W), F32)
            dppad[ts:ts + POOL_HALO, :] = jnp.zeros((POOL_HALO, POOL_W), F32)

        col = lambda a: jnp.sum(a, axis=0, keepdims=True)
        sh1 = mod_ref[0:1, :]
        sc1 = mod_ref[1:2, :]
        gt1 = mod_ref[2:3, :]
        dx2t = dx2_ref[...]
        vd_ref[0:1, :] += col(dx2t * y_ref[...])
        dyb = (dx2t * gt1).astype(MXU_DTYPE)
        dy_ref[...] = dyb
        dycat = _dot_nt(dyb, wout_ref[...])
        dyconv = dycat[:, :CONV_W]
        dypool = dycat[:, CONV_W:]

        pt = p_ref[...]
        t = tix * ts + lax.broadcasted_iota(jnp.int32, (ts, 1), 0)
        psc = ps_ref[...]
        dypb = (dypool * psc).astype(MXU_DTYPE)
        dyp_ref[...] = dypb
        dps, ypre = [], []
        for gi, w in enumerate(POOL_WINDOWS):
            cols = slice(gi * POOL_G, (gi + 1) * POOL_G)
            ypre.append(_dot(pt[:, cols], wg_ref[gi]))
            dpg = _dot_nt(dypb[:, cols], wg_ref[gi])
            dps.append(dpg)
            cnt = jnp.minimum(t + 1, w).astype(F32)
            dppad[0:ts, cols] = dpg / cnt
        vc_ref[0:1, :] += col(dypool * jnp.concatenate(ypre, axis=1))
        dvs = []
        for gi, w in enumerate(POOL_WINDOWS):
            cols = slice(gi * POOL_G, (gi + 1) * POOL_G)
            acc = dppad[0:ts, cols]
            for d in range(1, w):
                acc = acc + dppad[d:d + ts, cols]
            dvs.append(acc - dps[gi])
        dv = jnp.concatenate(dvs, axis=1)
        dppad[ts:ts + POOL_HALO, :] = dppad[0:POOL_HALO, :]

        zt = z_ref[...]
        lngv = lng_ref[...]
        ln = zt * lngv + lnb_ref[...]
        sg = _sigmoid(ln)
        swb = (ln * sg).astype(MXU_DTYPE)
        sw_ref[...] = swb
        dycb = dyconv.astype(MXU_DTYPE)
        dyc_ref[...] = dycb
        dln = _dot_nt(dycb, wpw_ref[...]) * (sg * (1.0 + ln * (1.0 - sg)))
        vc_ref[1:2, :] += col(dln * zt)
        vc_ref[2:3, :] += col(dln)
        dz = dln * lngv
        dcv = rstd_ref[...] * (dz - jnp.mean(dz, axis=-1, keepdims=True)
                               - zt * jnp.mean(dz * zt, axis=-1, keepdims=True))
        vc_ref[3:4, :] += col(dcv)
        dcpad[0:ts, :] = dcv
        ut = u_ref[...]
        a = ut[:, :CONV_W]
        g = ut[:, CONV_W:2 * CONV_W]
        sgg = _sigmoid(g)
        glu = a * sgg
        window = _row_shifts(dcpad, dshift, ts + CONV_HALO - SUBLANES)
        dglu = jnp.zeros((ts, CONV_W), F32)
        for k in range(CONV_K):
            sh = window(CONV_K - 1 - k, ts)
            dglu = dglu + dww_ref[k:k + 1, :] * sh
            ddw_ref[k:k + 1, :] += col(glu * sh)
        dcpad[ts:ts + CONV_HALO, :] = dcpad[0:CONV_HALO, :]
        da = dglu * sgg
        dg = dglu * a * sgg * (1.0 - sgg)
        dub = jnp.concatenate([da, dg, dv], axis=1).astype(MXU_DTYPE)
        du_ref[...] = dub
        cw = IN_W // N_CHIP
        dh1 = jnp.zeros((ts, D_MODEL), F32)
        for j in range(N_CHIP):
            dh1 = dh1 + _dot_nt(dub[:, j * cw:(j + 1) * cw], win_ref[j])

        xt = x_ref[...]
        g1v = g1_ref[...]
        r1 = lax.rsqrt(jnp.mean(xt * xt, axis=-1, keepdims=True) + EPS)
        xh1 = xt * r1
        n1 = xh1 * g1v
        h1_ref[...] = (n1 * (1.0 + sc1) + sh1).astype(MXU_DTYPE)
        vd_ref[1:2, :] += col(dh1)
        vd_ref[2:3, :] += col(dh1 * n1)
        dn1 = dh1 * (1.0 + sc1)
        vd_ref[3:4, :] += col(dn1 * xh1)
        dxh = dn1 * g1v
        gx_ref[...] = dx2t + r1 * (dxh - xh1 * jnp.mean(dxh * xh1, axis=-1, keepdims=True))

    tile = lambda w: pl.BlockSpec((ts, w), lambda i: (nt - 1 - i, 0))
    bf = lambda w: jax.ShapeDtypeStruct((s, w), MXU_DTYPE)
    return _call(
        body, name="mixer_bwd", grid=(nt,),
        in_specs=[tile(D_MODEL), tile(D_MODEL), tile(D_MODEL), tile(IN_W), tile(CONV_W), tile(1), tile(POOL_W),
                  _full(mod.shape), _full(g1.shape), _full(w_in.shape), _full(dww.shape), _full(lng.shape),
                  _full(lnb.shape), _full(w_pw.shape), _full(wg.shape), _full(pscale.shape), _full(w_out.shape)],
        out_specs=[tile(D_MODEL), tile(D_MODEL), tile(IN_W), tile(D_MODEL), tile(CONV_W), tile(CONV_W),
                   tile(POOL_W), _full((8, D_MODEL)), _full((8, CONV_W)), _full((32, CONV_W))],
        out_shape=[jax.ShapeDtypeStruct((s, D_MODEL), F32), bf(D_MODEL), bf(IN_W), bf(D_MODEL), bf(CONV_W),
                   bf(CONV_W), bf(POOL_W), jax.ShapeDtypeStruct((8, D_MODEL), F32),
                   jax.ShapeDtypeStruct((8, CONV_W), F32), jax.ShapeDtypeStruct((32, CONV_W), F32)],
        scratch_shapes=[pltpu.VMEM((ts + CONV_HALO, CONV_W), F32), pltpu.VMEM((ts + POOL_HALO, POOL_W), F32),
                        pltpu.VMEM((SUBLANES - 1, ts + CONV_HALO - SUBLANES, CONV_W), F32)],
        args=(dx2, x, y, u, z, rstd, p, mod, g1, w_in, dww, lng, lnb, w_pw, wg, pscale, w_out), comm=comm,
        after=after)


def _dw(name, a, a_spec, b, b_spec, nb, mb, nbk, comm=None, after=()):
    def body(a_ref, b_ref, o_ref):
        av = a_ref[...]
        bv = b_ref[...]
        av = av.reshape(av.shape[-2:])
        bv = bv.reshape(bv.shape[-2:])
        o_ref[0] = _dot_tn(av, bv)

    (out,), rest = _call(
        body, name=name, grid=(nb,), in_specs=[a_spec, b_spec],
        out_specs=[pl.BlockSpec((1, mb, nbk), lambda j: (j, 0, 0))],
        out_shape=[jax.ShapeDtypeStruct((nb, mb, nbk), F32)], args=(a, b), comm=comm, after=after)
    return out, rest


def _dw_mixer(p, dyp, ycat, dy, sw, dyc, h1, du):
    s = p.shape[0]
    ng = len(POOL_WINDOWS)
    assert ng == N_CHIP

    def body(p_ref, dyp_ref, ycat_ref, dy_ref, sw_ref, dyc_ref, h1_ref, du_ref, wg_ref, out_ref, pw_ref, in_ref):
        wg_ref[0] = _dot_tn(p_ref[...], dyp_ref[...])
        out_ref[0] = _dot_tn(ycat_ref[...], dy_ref[...])
        pw_ref[0] = _dot_tn(sw_ref[...], dyc_ref[...])
        in_ref[0] = _dot_tn(h1_ref[...], du_ref[...])

    whole = lambda w: pl.BlockSpec((s, w), lambda j: (0, 0))
    cols = lambda w: pl.BlockSpec((s, w), lambda j: (0, j))
    blk = lambda m, n: pl.BlockSpec((1, m, n), lambda j: (j, 0, 0))
    shapes = [(POOL_G, POOL_G), (D_MODEL // N_CHIP, D_MODEL), (CONV_W // N_CHIP, CONV_W), (D_MODEL, IN_W // N_CHIP)]
    res, _ = _call(
        body, name="dw_mixer", grid=(N_CHIP,),
        in_specs=[cols(POOL_G), cols(POOL_G), cols(D_MODEL // N_CHIP), whole(D_MODEL), cols(CONV_W // N_CHIP),
                  whole(CONV_W), whole(D_MODEL), cols(IN_W // N_CHIP)],
        out_specs=[blk(m, n) for m, n in shapes],
        out_shape=[jax.ShapeDtypeStruct((N_CHIP, m, n), F32) for m, n in shapes],
        args=(p, dyp, ycat, dy, sw, dyc, h1, du))
    return res


def _ada_fwd(c, w_ada, b4, first, later, dww, wg, comm):
    nc = w_ada.shape[1]
    nf, nl = len(first), len(later)
    shards = list(first) + list(later)

    def body(start_comm, gathered, c_ref, w_ref, b4_ref, *refs):
        shard_refs, refs = refs[:nf + nl], refs[nf + nl:]
        dww_ref, wg_ref, mod_ref, cact_ref, wgb_ref = refs[:5]
        later_refs, refs = refs[5:5 + nl], refs[5 + nl:]
        call, part, parts = refs[:3]
        stages, refs = refs[3:3 + nf + nl], refs[3 + nf + nl:]
        send1, recv1, send2, recv2, lsem = refs
        x, y, cc = _place()
        b = 4 * x + 2 * y + cc
        j = 2 * x + y

        def slot_copies(lo, hi):
            cps = []
            for a in range(lo, hi):
                dst = gathered[a] if a < nf else later_refs[a - nf]
                cps.append(pltpu.make_async_copy(stages[a], dst.at[j], lsem.at[a]))
            return cps

        call[b] = c_ref[...]
        sends = []
        for r in range(1, N_DEV):
            dev = ((1 - x) if r & 4 else x, (1 - y) if r & 2 else y, (1 - cc) if r & 1 else cc)
            cp = _remote(call.at[b], call.at[b], send1.at[r - 1], recv1.at[r - 1], dev)
            cp.start()
            sends.append(cp)
        for a in range(nf):
            stages[a][...] = shard_refs[a][...].astype(MXU_DTYPE)
        dww_copy = pltpu.make_async_copy(dww_ref, gathered[nf].at[j], lsem.at[nf + nl])
        dww_copy.start()
        for cp in slot_copies(0, nf):
            cp.start()
        for r in range(1, N_DEV):
            src_b = lax.bitwise_xor(b, r)
            _remote(call.at[src_b], call.at[src_b], send1.at[r - 1], recv1.at[r - 1], (x, y, cc)).wait_recv()
        for cp in sends:
            cp.wait_send()
        for cp in slot_copies(0, nf):
            cp.wait()
        dww_copy.wait()
        start_comm()
        for i in range(N_DEV):
            ci = call[i]
            cact_ref[i:i + 1, :] = ci * _sigmoid(ci)
        part[...] = jnp.dot(cact_ref[...], w_ref[...], preferred_element_type=F32, precision=lax.Precision.HIGHEST)
        sends = []
        for r in range(1, N_CHIP):
            kx, ky = _flip(x, y, r)
            cp = _remote(part, parts.at[j], send2.at[r - 1], recv2.at[r - 1], (kx, ky, cc))
            cp.start()
            sends.append(cp)
        parts[j] = part[...]
        for a in range(nf, nf + nl):
            stages[a][...] = shard_refs[a][...].astype(MXU_DTYPE)
        for cp in slot_copies(nf, nf + nl):
            cp.start()
        wgb_ref[...] = wg_ref[...].astype(MXU_DTYPE)
        for r in range(1, N_CHIP):
            kx, ky = _flip(x, y, r)
            kj = 2 * kx + ky
            _remote(part, parts.at[kj], send2.at[r - 1], recv2.at[r - 1], (x, y, cc)).wait_recv()
        for cp in sends:
            cp.wait_send()
        mine = lax.broadcasted_iota(jnp.int32, (N_DEV, 1), 0) == b
        for k in range(N_CHIP):
            row = jnp.sum(jnp.where(mine, parts[k], 0.0), axis=0, keepdims=True)
            mod_ref[k:k + 1, :] = row + b4_ref[k:k + 1, :]
        for cp in slot_copies(nf, nf + nl):
            cp.wait()

    res, rest = _call(
        body, name="ada_fwd", grid=(1,),
        in_specs=[VMEM] * (5 + nf + nl), out_specs=[VMEM, VMEM, VMEM] + [ANY] * nl,
        out_shape=[jax.ShapeDtypeStruct((N_CHIP, nc), F32), jax.ShapeDtypeStruct((N_DEV, D_MODEL), F32),
                   jax.ShapeDtypeStruct(wg.shape, MXU_DTYPE)]
        + [jax.ShapeDtypeStruct((N_CHIP,) + a.shape, MXU_DTYPE) for a in later],
        scratch_shapes=[pltpu.VMEM((N_DEV, 1, D_MODEL), F32), pltpu.VMEM((N_DEV, nc), F32),
                        pltpu.VMEM((N_CHIP, N_DEV, nc), F32)]
        + [pltpu.VMEM(a.shape, MXU_DTYPE) for a in shards]
        + [pltpu.SemaphoreType.DMA((N_DEV - 1,)), pltpu.SemaphoreType.DMA((N_DEV - 1,)),
           pltpu.SemaphoreType.DMA((N_CHIP - 1,)), pltpu.SemaphoreType.DMA((N_CHIP - 1,)),
           pltpu.SemaphoreType.DMA((nf + nl + 1,))],
        args=(c, w_ada, b4, *shards, dww, wg), comm=comm, body_starts=True)
    return (res[0], res[1], res[2], res[3:]), rest


def _chip_partials(name, place, gs, rs, comm=None, after=()):
    n = len(gs)

    def body(pref, *refs):
        g_refs, r_refs = refs[:n], refs[n:2 * n]
        pb_refs, own_refs = refs[2 * n:3 * n], refs[3 * n:]
        jj = pl.program_id(0)
        for a in range(n):
            sm = g_refs[a][0] + r_refs[a][0]
            pb_refs[a][0] = sm.astype(MXU_DTYPE)

            @pl.when(jj == pref[1])
            def _(a=a, sm=sm):
                own_refs[a][...] = sm

    halves = [(g.shape[1] // 2, g.shape[2]) for g in gs]
    in_specs = [pl.BlockSpec((1, h, w), lambda jj, pref: (jj, pref[0], 0)) for h, w in halves]
    in_specs += [pl.BlockSpec((1, h, w), lambda jj, pref: (jj, 0, 0)) for h, w in halves]
    out_specs = [pl.BlockSpec((1, h, w), lambda jj, pref: (jj, 0, 0)) for h, w in halves]
    out_specs += [pl.BlockSpec((h, w), lambda jj, pref: (0, 0)) for h, w in halves]
    out, rest = _call(
        body, name=name, grid=(N_CHIP,), in_specs=in_specs, out_specs=out_specs,
        out_shape=[jax.ShapeDtypeStruct((N_CHIP, h, w), MXU_DTYPE) for h, w in halves]
        + [jax.ShapeDtypeStruct((h, w), F32) for h, w in halves],
        args=(*gs, *rs), prefetch=(place,), comm=comm, after=after)
    return (out[:n], out[n:]), rest


def _sum_partials(name, place, owns, recvd, comm=None, after=()):
    n = len(owns)

    def body(pref, *refs):
        o_refs, r_refs, out_refs = refs[:n], refs[n:2 * n], refs[2 * n:]
        for a in range(n):
            acc = o_refs[a][...]
            for r in range(N_CHIP - 1):
                acc = acc + r_refs[a][r].astype(F32)
            out_refs[a][...] = acc

    full = lambda a: pl.BlockSpec(a.shape, lambda i, pref: (0,) * a.ndim)
    return _call(
        body, name=name, grid=(1,), in_specs=[full(a) for a in list(owns) + list(recvd)],
        out_specs=[pl.BlockSpec(o.shape, lambda i, pref: (pref[0], 0)) for o in owns],
        out_shape=[jax.ShapeDtypeStruct((2 * o.shape[0], o.shape[1]), F32) for o in owns],
        args=(*owns, *recvd), prefetch=(place,), comm=comm, after=after)


def _adamw_math(w, g, m, v):
    m = ADAM_B1 * m + (1.0 - ADAM_B1) * g
    v = ADAM_B2 * v + (1.0 - ADAM_B2) * (g * g)
    m_hat = m / (1.0 - ADAM_B1 ** ADAM_STEP)
    v_hat = v / (1.0 - ADAM_B2 ** ADAM_STEP)
    delta = -ADAM_LR * (m_hat / (jnp.sqrt(v_hat) + ADAM_EPS) + ADAM_WD * w)
    return delta, m, v


def _row_tile(rows):
    for t in (512, 352, 256, 128):
        if rows % t == 0:
            return t
    return rows


def _adamw(name, wgmv, steps, after=()):
    n = len(wgmv)

    def body(*refs):
        ins, outs = refs[:4 * n], refs[4 * n:]
        for i in range(n):
            w_ref, g_ref, m_ref, v_ref = ins[4 * i:4 * i + 4]
            d_ref, nm_ref, nv_ref = outs[3 * i:3 * i + 3]
            d_ref[...], nm_ref[...], nv_ref[...] = _adamw_math(w_ref[...], g_ref[...], m_ref[...], v_ref[...])

    in_specs, out_specs, out_shape, args = [], [], [], []
    for w, g, m, v in wgmv:
        rows, cols = w.shape
        spec = pl.BlockSpec((rows // steps, cols), lambda i: (i, 0))
        in_specs += [spec] * 4
        out_specs += [spec] * 3
        out_shape += [jax.ShapeDtypeStruct(w.shape, F32)] * 3
        args += [w, g, m, v]
    res, _ = _call(body, name=name, grid=(steps,), in_specs=in_specs, out_specs=out_specs, out_shape=out_shape,
                   args=args, after=after)
    return [res[3 * i:3 * i + 3] for i in range(n)]


def _adamw_ada(place, cact, dmod, w, m, v, after=()):
    rows, cols = w.shape
    tr = _row_tile(rows)

    def body(pref, ca_ref, dm_ref, w_ref, m_ref, v_ref, g_ref, d_ref, nm_ref, nv_ref):
        g = lax.dot_general(ca_ref[...], dm_ref[...], (((0,), (0,)), ((), ())), preferred_element_type=F32,
                            precision=lax.Precision.HIGHEST)
        g_ref[...] = g
        d_ref[...], nm_ref[...], nv_ref[...] = _adamw_math(w_ref[...], g, m_ref[...], v_ref[...])

    spec = pl.BlockSpec((tr, cols), lambda i, pref: (i, 0))
    return _call(
        body, name="adamw_ada", grid=(rows // tr,),
        in_specs=[pl.BlockSpec((N_DEV, tr), lambda i, pref: (0, i)),
                  pl.BlockSpec((N_DEV, cols), lambda i, pref: (0, pref[1])), spec, spec, spec],
        out_specs=[spec] * 4, out_shape=[jax.ShapeDtypeStruct(w.shape, F32)] * 4,
        args=(cact, dmod, w, m, v), prefetch=(place,), after=after)[0]


def _adamw_small(place, owns, gathered, wmv):
    nw = len(wmv)
    flat = [a for t in wmv for a in t]

    def body(pref, *refs):
        own_refs, all_refs, refs = refs[:5], refs[5:10], refs[10:]
        w_refs = refs[:3 * nw]
        loss_ref, dmod_ref = refs[3 * nw], refs[3 * nw + 1]
        o_refs = refs[3 * nw + 2:]
        j = pref[1]
        me = 2 * pref[1] + pref[0]

        def total(i):
            acc = None
            for b in range(N_DEV):
                blk = jnp.where(me == b, own_refs[i][...], all_refs[i][b])
                acc = blk if acc is None else acc + blk
            return acc

        vf, vd, vc, ddw, gwg = [total(i) for i in range(5)]
        loss_ref[...] = (0.5 / D_MODEL) * jnp.sum(vf[5:6, :], axis=1, keepdims=True)
        order = ((1, 1), (1, 2), (1, 0), (0, 2), (0, 3), (0, 1))
        for b in range(N_DEV):
            for q, (i, row) in enumerate(order):
                dmod_ref[b:b + 1, q * D_MODEL:(q + 1) * D_MODEL] = jnp.where(
                    me == b, own_refs[i][row:row + 1, :], all_refs[i][b, row:row + 1, :])
        dm = dmod_ref[...]
        g_bada = dm[0:1, :]
        for b in range(1, N_DEV):
            g_bada = g_bada + dm[b:b + 1, :]
        g_dww = jnp.zeros((32, POOL_G), F32)
        for k in range(N_CHIP):
            g_dww = g_dww + jnp.where(j == k, ddw[:, k * POOL_G:(k + 1) * POOL_G], 0.0)
        grads = [g_bada, vd[3:4, :], g_dww, vc[3:4, :], vc[1:2, :], vc[2:3, :], gwg, vc[0:1, :], vf[4:5, :],
                 vf[0:1, :]]
        for i, g in enumerate(grads):
            w_ref, m_ref, v_ref = w_refs[3 * i:3 * i + 3]
            d, nm, nv = _adamw_math(w_ref[...], g, m_ref[...], v_ref[...])
            o_refs[4 * i][...] = g
            o_refs[4 * i + 1][...] = d
            o_refs[4 * i + 2][...] = nm
            o_refs[4 * i + 3][...] = nv

    outs = [jax.ShapeDtypeStruct((1, 1), F32), jax.ShapeDtypeStruct((N_DEV, 6 * D_MODEL), F32)]
    for w, _, _ in wmv:
        outs += [jax.ShapeDtypeStruct(w.shape, F32)] * 4
    full = lambda a: pl.BlockSpec(a.shape, lambda i, pref: (0,) * a.ndim)
    args = list(owns) + list(gathered) + flat
    res, _ = _call(body, name="adamw_small", grid=(1,), in_specs=[full(a) for a in args],
                   out_specs=[full(o) for o in outs], out_shape=outs, args=args, prefetch=(place,))
    return res[0], res[1], [res[2 + 4 * i:6 + 4 * i] for i in range(nw)]


def kernel(x, c, w_ada, b_ada, g_norm1, w_in, dw_w, dw_b, conv_ln_g, conv_ln_b, w_conv_pw, w_pool_group, pool_scale, w_out, g_norm2, w_ffn_gate, w_ffn_up, w_ffn_down, g_final, loss_target, m_w_ada, m_b_ada, m_g_norm1, m_w_in, m_dw_w, m_dw_b, m_conv_ln_g, m_conv_ln_b, m_w_conv_pw, m_w_pool_group, m_pool_scale, m_w_out, m_g_norm2, m_w_ffn_gate, m_w_ffn_up, m_w_ffn_down, m_g_final, v_w_ada, v_b_ada, v_g_norm1, v_w_in, v_dw_w, v_dw_b, v_conv_ln_g, v_conv_ln_b, v_w_conv_pw, v_w_pool_group, v_pool_scale, v_w_out, v_g_norm2, v_w_ffn_gate, v_w_ffn_up, v_w_ffn_down, v_g_final):
    xi, yi, ci = _place()
    place = jnp.stack([ci, 2 * xi + yi]).astype(jnp.int32)
    n_ada = w_ada.shape[2]

    tr = lambda a: jnp.transpose(a[0])
    mixer_shards = [w_in[0], w_conv_pw[0], w_out[0]]
    ffn_shards = [tr(w_ffn_gate), tr(w_ffn_up), w_ffn_down[0]]
    slots = [lax.empty((N_CHIP,) + a.shape, MXU_DTYPE) for a in mixer_shards] + [lax.empty((N_CHIP,) + dw_w.shape[1:], F32)]

    (mod4, cact, wg_b, (b_gate, b_up, b_down)), (win_g, wpw_g, wout_g, dww_g) = _ada_fwd(
        c, w_ada[0], b_ada.reshape(N_CHIP, n_ada), mixer_shards, ffn_shards, dw_w[0], w_pool_group[0],
        comm=_weights_gather(slots, [True, True, True, False]))
    mod = mod4.reshape(6, D_MODEL)
    dww_full = jnp.pad(jnp.concatenate([dww_g[k] for k in range(N_CHIP)], axis=1), ((0, 1), (0, 0)))
    w_pw = wpw_g.reshape(CONV_W, CONV_W)
    w_o = wout_g.reshape(D_MODEL, D_MODEL)
    xs, tgt, gf = x[0], loss_target[0], g_final.reshape(1, D_MODEL)
    s = xs.shape[0]
    fb = b_gate.shape[1]

    (x2, y, u, z, rstd, p, ycat), (wgate_g, wup_g, wdown_g) = _mixer_fwd(
        xs, mod, g_norm1, win_g, dww_full, dw_b, conv_ln_g, conv_ln_b, w_pw, wg_b, pool_scale, w_o,
        comm=_weights_gather([b_gate, b_up, b_down], [True, True, True]))
    dx2, h2, df, act, dgg, duu, vec_f = _ffn(x2, tgt, mod, g_norm2, gf, wgate_g, wup_g, wdown_g)

    whole = lambda w: pl.BlockSpec((s, w), lambda j: (0, 0))
    cols = lambda w: pl.BlockSpec((s, w), lambda j: (0, j))
    hid = pl.BlockSpec((1, s, fb), lambda j: (j, 0, 0))
    c_gate, _ = _dw("dw_gate", dgg, hid, h2, whole(D_MODEL), N_CHIP, fb, D_MODEL)
    c_up, (r_gate,) = _dw("dw_up", duu, hid, h2, whole(D_MODEL), N_CHIP, fb, D_MODEL, comm=_sibling_halves([c_gate]))
    ((pb_gate,), (own_gate,)), _ = _chip_partials("partials_gate", place, [c_gate], [r_gate])
    ex_gate = _exchange_parts([pb_gate])
    st_gate, tok_gate = _split_start("exchange_gate_start", [pb_gate], *ex_gate)
    c_down, (r_up,) = _dw("dw_down", act, hid, df, whole(D_MODEL), N_CHIP, fb, D_MODEL,
                          comm=_sibling_halves([c_up]), after=(tok_gate,))
    sib_down = _sibling_parts([c_down])
    st_sd, tok_sd = _split_start("sibling_down_start", [c_down], *sib_down)
    ((pb_up,), (own_up,)), _ = _chip_partials("partials_up", place, [c_up], [r_up], after=(tok_sd,))
    (r_down,) = _split_wait("sibling_down_wait", st_sd, 1, sib_down[1], sib_down[2], after=(pb_up,))
    ((pb_down,), (own_down,)), _ = _chip_partials("partials_down", place, [c_down], [r_down])
    ex_ud = _exchange_parts([pb_up, pb_down])
    st_ud, tok_ud = _split_start("exchange_up_down_start", [pb_up, pb_down], *ex_ud)
    (gx, h1, du, dy, sw, dyc, dyp, vec_d, vec_c, ddw), _ = _mixer_bwd(
        dx2, xs, y, u, z, rstd, p, mod, g_norm1, win_g, dww_full, conv_ln_g, conv_ln_b, w_pw, wg_b, pool_scale, w_o,
        after=(tok_ud,))

    g_wg, c_out, c_pw, c_in = _dw_mixer(p, dyp, ycat, dy, sw, dyc, h1, du)
    small_own = [vec_f, vec_d, vec_c, ddw, g_wg]
    ex_small = _small_parts(small_own)
    st_small, tok_small = _split_start("small_grads_start", small_own, *ex_small, zeroed=True)
    mix = [c_in, c_pw, c_out]
    sib_mix = _sibling_parts(mix)
    st_sm, tok_sm = _split_start("sibling_mix_start", mix, *sib_mix)
    (rc_gate,) = _split_wait("exchange_gate_wait", st_gate, 1, ex_gate[1], ex_gate[2],
                             after=(c_in, tok_small, tok_sm))
    rc_up, rc_down = _split_wait("exchange_up_down_wait", st_ud, 2, ex_ud[1], ex_ud[2], after=(c_in, rc_gate))

    ffn_fulls, _ = _sum_partials("sum_ffn", place, [own_gate, own_up, own_down], [rc_gate, rc_up, rc_down])
    r_in, r_pw, r_out = _split_wait("sibling_mix_wait", st_sm, len(mix), sib_mix[1], sib_mix[2],
                                    after=(ffn_fulls[0],))
    join_sems, join_copies = _join_parts(ffn_fulls)
    join_in_place = lambda bufs, lands, send, recv: join_copies(bufs, send, recv)
    st_jf, tok_jf = _split_start("join_ffn_start", ffn_fulls, [], join_sems, join_in_place)
    (pbs_mix, owns_mix), _ = _chip_partials("partials_mix", place, mix, [r_in, r_pw, r_out], after=(tok_jf,))
    g_gate, g_up, g_down = _split_wait("join_ffn_wait", st_jf, len(ffn_fulls), join_sems, join_in_place,
                                       after=(pbs_mix[0],), in_place=True)
    small_all = _split_wait("small_grads_wait", st_small, len(small_own), ex_small[1], ex_small[2],
                            after=(g_down,))

    pad_rows = lambda a: jnp.pad(a[0], ((0, 1), (0, 0)))
    row = lambda a: a.reshape(1, -1)
    small = [(b_ada, m_b_ada, v_b_ada), (g_norm1, m_g_norm1, v_g_norm1),
             (pad_rows(dw_w), pad_rows(m_dw_w), pad_rows(v_dw_w)), (dw_b, m_dw_b, v_dw_b),
             (conv_ln_g, m_conv_ln_g, v_conv_ln_g), (conv_ln_b, m_conv_ln_b, v_conv_ln_b),
             (w_pool_group[0], m_w_pool_group[0], v_w_pool_group[0]), (pool_scale, m_pool_scale, v_pool_scale),
             (g_norm2, m_g_norm2, v_g_norm2), (row(g_final), row(m_g_final), row(v_g_final))]
    loss, dmod, small_out = _adamw_small(place, small_own, small_all, small)
    (o_bada, o_g1, o_dww, o_dwb, o_lng, o_lnb, o_wg, o_ps, o_g2, o_gf) = small_out
    o_dww = [a[:CONV_K] for a in o_dww]
    o_gf = [a.reshape(D_MODEL) for a in o_gf]
    lead = lambda outs: [a[None] for a in outs]

    lands, sem_shape, copies = _exchange_parts(pbs_mix)
    state, token = _split_start("exchange_mix_start", pbs_mix, lands, sem_shape, copies)
    u_gate, u_up, u_down = _adamw(
        "adamw_ffn", [(tr(w_ffn_gate), g_gate, tr(m_w_ffn_gate), tr(v_w_ffn_gate)),
                      (tr(w_ffn_up), g_up, tr(m_w_ffn_up), tr(v_w_ffn_up)),
                      (w_ffn_down[0], g_down, m_w_ffn_down[0], v_w_ffn_down[0])],
        steps=4, after=(token,))
    o_gate = [jnp.transpose(o) for o in [g_gate] + list(u_gate)]
    o_up = [jnp.transpose(o) for o in [g_up] + list(u_up)]
    o_down = [g_down] + list(u_down)
    o_ada = _adamw_ada(place, cact, dmod, w_ada[0], m_w_ada[0], v_w_ada[0], after=(token,))
    rc_mix = _split_wait("exchange_mix_wait", state, len(pbs_mix), sem_shape, copies, after=(o_ada[1], u_down[0]))
    mix_fulls, _ = _sum_partials("sum_mix", place, owns_mix, rc_mix)
    g_in, g_pw, g_out = _comm_only("join_mix", _join_halves(mix_fulls))
    u_in, u_pw, u_out = _adamw(
        "adamw_mix", [(w_in[0], g_in, m_w_in[0], v_w_in[0]), (w_conv_pw[0], g_pw, m_w_conv_pw[0], v_w_conv_pw[0]),
                      (w_out[0], g_out, m_w_out[0], v_w_out[0])], steps=4)
    o_in, o_pw, o_out = [g_in] + list(u_in), [g_pw] + list(u_pw), [g_out] + list(u_out)

    per_weight = [lead(o_ada), o_bada, o_g1, lead(o_in), lead(o_dww), o_dwb, o_lng, o_lnb, lead(o_pw), lead(o_wg),
                  o_ps, lead(o_out), o_g2, lead(o_gate), lead(o_up), lead(o_down), o_gf]
    result = [loss.reshape(()), gx[None]]
    for kind in range(4):
        result += [o[kind] for o in per_weight]
    return tuple(result)
```

```python
import functools

import jax
import jax.numpy as jnp
from jax import lax
from jax.experimental import pallas as pl
from jax.experimental.pallas import tpu as pltpu

F32 = jnp.float32
MXU_DTYPE = jnp.bfloat16
EPS = 1e-6

D_MODEL = 1024
CONV_W = 512
POOL_W = 512
CONV_K = 31
POOL_WINDOWS = (2, 4, 8, 16)
POOL_G = 128
IN_W = 2 * CONV_W + POOL_W
N_CHIP = 4
N_DEV = 8
CONV_HALO = 32
POOL_HALO = 16

ADAM_LR = 0.001
ADAM_B1 = 0.9
ADAM_B2 = 0.999
ADAM_EPS = 1e-08
ADAM_WD = 0.01
ADAM_STEP = 10

MESH = pl.DeviceIdType.MESH
ANY = pl.BlockSpec(memory_space=pl.ANY)
VMEM = pl.BlockSpec(memory_space=pltpu.VMEM)


def _dot(a, b):
    return jnp.dot(a.astype(MXU_DTYPE), b.astype(MXU_DTYPE), preferred_element_type=F32)


def _dot_nt(a, b):
    return lax.dot_general(a.astype(MXU_DTYPE), b.astype(MXU_DTYPE), (((1,), (1,)), ((), ())),
                           preferred_element_type=F32)


def _dot_tn(a, b):
    return lax.dot_general(a.astype(MXU_DTYPE), b.astype(MXU_DTYPE), (((0,), (0,)), ((), ())),
                           preferred_element_type=F32)


def _sigmoid(v):
    return 1.0 / (1.0 + jnp.exp(-v))


def _full(shape):
    n = len(shape)
    return pl.BlockSpec(shape, lambda *_: (0,) * n)


def _token_tile(s):
    return 256 if s % 256 == 0 else s


SUBLANES = 8


def _row_shifts(pad_ref, shifted_ref, rows):
    for r in range(1, SUBLANES):
        shifted_ref[r - 1] = pad_ref[r:r + rows, :]

    def window(i, n):
        r, base = i % SUBLANES, i - i % SUBLANES
        if r == 0:
            return pad_ref[base:base + n, :]
        return shifted_ref[r - 1, base:base + n, :]

    return window


def _place():
    return lax.axis_index("x"), lax.axis_index("y"), lax.axis_index("c")


def _flip(x, y, r):
    return ((1 - x) if r & 2 else x, (1 - y) if r & 1 else y)


def _remote(src, dst, send_sem, recv_sem, dev):
    return pltpu.make_async_remote_copy(src_ref=src, dst_ref=dst, send_sem=send_sem, recv_sem=recv_sem,
                                        device_id=dev, device_id_type=MESH)


class _Comm:
    def __init__(self, ins, outs, aliases, scratch, start, finish, mid=None):
        self.ins, self.outs, self.aliases, self.scratch = list(ins), list(outs), dict(aliases), list(scratch)
        self.start, self.finish = start, finish
        self.mid = mid


def _both(a, b):
    na, nao, nas = len(a.ins), len(a.outs), len(a.scratch)
    aliases = dict(a.aliases)
    aliases.update({na + i: nao + o for i, o in b.aliases.items()})

    def start(ins, outs, scr):
        a.start(ins[:na], outs[:nao], scr[:nas])
        b.start(ins[na:], outs[nao:], scr[nas:])

    def finish(ins, outs, scr):
        a.finish(ins[:na], outs[:nao], scr[:nas])
        b.finish(ins[na:], outs[nao:], scr[nas:])

    def mid(ins, outs, scr):
        if a.mid:
            a.mid(ins[:na], outs[:nao], scr[:nas])
        if b.mid:
            b.mid(ins[na:], outs[nao:], scr[nas:])

    return _Comm(a.ins + b.ins, a.outs + b.outs, aliases, a.scratch + b.scratch, start, finish,
                 mid if (a.mid or b.mid) else None)


def _call(body, *, name, grid, in_specs, out_specs, out_shape, args, scratch_shapes=(), prefetch=(), comm=None,
          body_starts=False, after=()):
    in_specs = list(in_specs) + [ANY] * len(after)
    args = list(args) + list(after)
    n_pre, n_in, n_out, n_scr = len(prefetch), len(in_specs), len(out_specs), len(scratch_shapes)
    n_body_in = n_in - len(after)
    c_ins = comm.ins if comm else []
    c_outs = comm.outs if comm else []
    c_scr = comm.scratch if comm else []
    last = grid[0] - 1

    def wrapped(*refs):
        pre, refs = refs[:n_pre], refs[n_pre:]
        ins, cin = refs[:n_body_in], refs[n_in:n_in + len(c_ins)]
        refs = refs[n_in + len(c_ins):]
        outs, cout = refs[:n_out], refs[n_out:n_out + len(c_outs)]
        refs = refs[n_out + len(c_outs):]
        scr, cscr = refs[:n_scr], refs[n_scr:]
        step = pl.program_id(0)
        if comm and not body_starts:
            @pl.when(step == 0)
            def _():
                comm.start(cin, cout, cscr)

        has_mid = comm is not None and comm.mid is not None
        mid_step = grid[0] // 2 if grid[0] >= 4 else None
        if has_mid and mid_step is not None:
            @pl.when(step == mid_step)
            def _():
                comm.mid(cin, cout, cscr)

        if body_starts:
            body(lambda: comm.start(cin, cout, cscr), cout, *pre, *ins, *outs, *scr)
        else:
            body(*pre, *ins, *outs, *scr)
        if comm:
            @pl.when(step == last)
            def _():
                if has_mid and mid_step is None:
                    comm.mid(cin, cout, cscr)
                comm.finish(cin, cout, cscr)

    aliases = {n_pre + n_in + a: n_out + b for a, b in (comm.aliases if comm else {}).items()}
    res = pl.pallas_call(
        wrapped, name=name,
        grid_spec=pltpu.PrefetchScalarGridSpec(
            num_scalar_prefetch=n_pre, grid=grid, in_specs=list(in_specs) + [ANY] * len(c_ins),
            out_specs=list(out_specs) + [ANY] * len(c_outs), scratch_shapes=list(scratch_shapes) + list(c_scr)),
        out_shape=list(out_shape) + list(c_outs),
        input_output_aliases=aliases,
        compiler_params=pltpu.CompilerParams(dimension_semantics=("arbitrary",)),
    )(*prefetch, *args, *c_ins)
    return res[:n_out], res[n_out:]


def _comm_only(name, comm):
    return _call(lambda: None, name=name, grid=(1,), in_specs=[], out_specs=[], out_shape=[], args=[], comm=comm)[1]


def _weights_gather(bufs, split):
    n = len(bufs)

    def ctx(outs):
        x, y, cc = _place()
        chips = dict(me=2 * x + y, y=2 * x + (1 - y), x=2 * (1 - x) + y, d=2 * (1 - x) + (1 - y))
        devs = dict(y=(x, 1 - y, cc), x=(1 - x, y, cc), d=(1 - x, 1 - y, cc), s=(x, y, 1 - cc))

        def piece(a, kj, pc, q=None):
            if not split[a]:
                return outs[a].at[kj]
            h = bufs[a].shape[1] // 2
            if q is None:
                return outs[a].at[kj, pl.ds(pc * h, h), :]
            return outs[a].at[kj, pl.ds(pc * h + q * (h // 2), h // 2), :]

        return cc, chips, devs, piece

    def directs(a, outs, send, recv):
        cc, chips, devs, piece = ctx(outs)
        if not split[a]:
            whole = piece(a, chips["me"], cc)
            return [_remote(whole, whole, send.at[a, k], recv.at[a, k], devs[t]) for k, t in ((0, "y"), (2, "x"), (4, "d"))]
        q = lambda i: piece(a, chips["me"], cc, i)
        return [_remote(q(0), q(0), send.at[a, 0], recv.at[a, 0], devs["y"]),
                _remote(q(1), q(1), send.at[a, 3], recv.at[a, 3], devs["x"]),
                _remote(q(1), q(1), send.at[a, 1], recv.at[a, 1], devs["y"]),
                _remote(q(0), q(0), send.at[a, 2], recv.at[a, 2], devs["x"])]

    def landed(a, k, outs, send, recv):
        cc, chips, devs, piece = ctx(outs)
        if not split[a]:
            got = piece(a, chips[{0: "y", 2: "x", 4: "d"}[k]], cc)
        elif k < 6:
            got = piece(a, chips[("y", "y", "x", "x", "d", "d")[k]], cc, (0, 1, 0, 1, 0, 1)[k])
        else:
            got = piece(a, chips[("y", "x", "d")[k - 6]], 1 - cc)
        return _remote(got, got, send.at[a, k], recv.at[a, k], devs["s"])

    def passed_on(a, outs, send, recv):
        cc, chips, devs, piece = ctx(outs)
        from_y, from_x = piece(a, chips["y"], cc, 0), piece(a, chips["x"], cc, 1)
        return [_remote(from_y, from_y, send.at[a, 4], recv.at[a, 4], devs["x"]),
                _remote(from_x, from_x, send.at[a, 5], recv.at[a, 5], devs["y"])]

    def to_sibling(a, outs, send, recv, which=(0, 1, 2)):
        cc, chips, devs, piece = ctx(outs)
        halves = [piece(a, chips[("y", "x", "d")[i]], cc) for i in which]
        return [_remote(hf, hf, send.at[a, 6 + i], recv.at[a, 6 + i], devs["s"]) for i, hf in zip(which, halves)]

    def start(ins, outs, scr):
        send, recv = scr
        per_item = [directs(a, outs, send, recv) for a in range(n)]
        for rank in range(4):
            for cps in per_item:
                if rank < len(cps):
                    cps[rank].start()

    def mid(ins, outs, scr):
        send, recv = scr
        for a in range(n):
            if split[a]:
                fy, fx = passed_on(a, outs, send, recv)
                landed(a, 0, outs, send, recv).wait_recv()
                fy.start()
                landed(a, 3, outs, send, recv).wait_recv()
                fx.start()

    def finish(ins, outs, scr):
        send, recv = scr
        for a in range(n):
            if split[a]:
                to_y, to_x = to_sibling(a, outs, send, recv, which=(0, 1))
                landed(a, 1, outs, send, recv).wait_recv()
                to_y.start()
                landed(a, 2, outs, send, recv).wait_recv()
                to_x.start()
        for a in range(n):
            if split[a]:
                for k in (4, 5):
                    landed(a, k, outs, send, recv).wait_recv()
                to_sibling(a, outs, send, recv, which=(2,))[0].start()
            else:
                for k in (0, 2, 4):
                    landed(a, k, outs, send, recv).wait_recv()
        for a in range(n):
            if split[a]:
                for k in (6, 7, 8):
                    landed(a, k, outs, send, recv).wait_recv()
            cps = directs(a, outs, send, recv)
            if split[a]:
                cps += passed_on(a, outs, send, recv) + to_sibling(a, outs, send, recv)
            for cp in cps:
                cp.wait_send()

    return _Comm(bufs, [jax.ShapeDtypeStruct(b.shape, b.dtype) for b in bufs], {i: i for i in range(n)},
                 [pltpu.SemaphoreType.DMA((n, 9)), pltpu.SemaphoreType.DMA((n, 9))], start, finish, mid)


HBM =pl.BlockSpec(memory_space=pltpu.HBM)
SEM = pl.BlockSpec(memory_space=pltpu.SEMAPHORE)
DATAFLOW = pltpu.SideEffectType.DATAFLOW_SIDE_EFFECTING


class _SemGrid:
    def __init__(self, refs, cols):
        self.refs, self.cols = refs, cols

    @property
    def at(self):
        return self

    def __getitem__(self, idx):
        return self.refs[idx[0] * self.cols + idx[1]]


def _split_start(name, srcs, lands, sem_shape, copies, zeroed=False):
    n, k = len(srcs), len(lands)
    ns = sem_shape[0] * sem_shape[1]

    def body(*refs):
        src_refs, land_refs = refs[:n], refs[n:n + k]
        send = _SemGrid(refs[n + k:n + k + ns], sem_shape[1])
        recv = _SemGrid(refs[n + k + ns:n + k + 2 * ns], sem_shape[1])
        token = refs[-1]
        for cp in copies(src_refs, land_refs, send, recv):
            cp.start()
        token[...] = jnp.zeros(token.shape, F32)

    hbm = lambda a: pltpu.with_memory_space_constraint(a, pltpu.HBM)
    zones = [jnp.zeros(l.shape, l.dtype) if zeroed else lax.empty(l.shape, l.dtype) for l in lands]
    out = pl.pallas_call(
        body, name=name,
        out_shape=[pltpu.SemaphoreType.DMA(())] * (2 * ns)
        + [pltpu.HBM(a.shape, a.dtype) for a in list(srcs) + list(lands)] + [jax.ShapeDtypeStruct((8, 128), F32)],
        in_specs=[HBM] * (n + k), out_specs=[SEM] * (2 * ns) + [HBM] * (n + k) + [VMEM],
        input_output_aliases={i: 2 * ns + i for i in range(n + k)},
        compiler_params=pltpu.CompilerParams(has_side_effects=DATAFLOW),
    )(*[hbm(a) for a in srcs], *[hbm(z) for z in zones])
    return out[:-1], out[-1]


def _split_wait(name, state, n, sem_shape, copies, after, in_place=False, with_sources=False):
    ns = sem_shape[0] * sem_shape[1]
    sems, bufs = state[:2 * ns], state[2 * ns:]
    k = len(bufs) - n

    def body(*refs):
        src_refs, land_refs = refs[:n], refs[n:n + k]
        send = _SemGrid(refs[n + k:n + k + ns], sem_shape[1])
        recv = _SemGrid(refs[n + k + ns:n + k + 2 * ns], sem_shape[1])
        cps = copies(src_refs, land_refs, send, recv)
        for cp in cps:
            cp.wait_send()
        for cp in cps:
            cp.wait_recv()

    out = pl.pallas_call(
        body, name=name,
        out_shape=[pltpu.HBM(a.shape, a.dtype) for a in bufs],
        in_specs=[HBM] * (n + k) + [SEM] * (2 * ns) + [ANY] * len(after), out_specs=[HBM] * (n + k),
        input_output_aliases={i: i for i in range(n + k)},
        compiler_params=pltpu.CompilerParams(has_side_effects=DATAFLOW),
    )(*bufs, *sems, *after)
    if with_sources:
        return out[:n], out[n:]
    return out[:n] if in_place else out[n:]


def _direct_phases(copies):
    def start(ins, outs, scr):
        for cp in copies(ins, outs, *scr):
            cp.start()

    def finish(ins, outs, scr):
        cps = copies(ins, outs, *scr)
        for cp in cps:
            cp.wait_recv()
        for cp in cps:
            cp.wait_send()

    return start, finish


def _sibling_parts(gs):
    n = len(gs)

    def copies(ins, outs, send, recv):
        x, y, cc = _place()
        cps = []
        for a in range(n):
            h = gs[a].shape[1] // 2
            cps.append(_remote(ins[a].at[:, pl.ds((1 - cc) * h, h), :], outs[a], send.at[a, 0], recv.at[a, 0],
                               (x, y, 1 - cc)))
        return cps

    return [jax.ShapeDtypeStruct((N_CHIP, g.shape[1] // 2, g.shape[2]), F32) for g in gs], (n, 1), copies


def _sibling_halves(gs):
    lands, sem_shape, copies = _sibling_parts(gs)
    start, finish = _direct_phases(copies)
    return _Comm(gs, lands, {}, [pltpu.SemaphoreType.DMA(sem_shape), pltpu.SemaphoreType.DMA(sem_shape)],
                 start, finish)


def _exchange_parts(pbs):
    n = len(pbs)

    def copies(ins, outs, send, recv):
        x, y, cc = _place()
        cps = []
        for a in range(n):
            for r in range(1, N_CHIP):
                kx, ky = _flip(x, y, r)
                cps.append(_remote(ins[a].at[2 * kx + ky], outs[a].at[r - 1], send.at[a, r - 1], recv.at[a, r - 1],
                                   (kx, ky, cc)))
        return cps

    lands = [jax.ShapeDtypeStruct((N_CHIP - 1,) + p.shape[1:], p.dtype) for p in pbs]
    return lands, (n, N_CHIP - 1), copies


def _small_parts(arrs):
    n = len(arrs)

    def copies(ins, outs, send, recv):
        x, y, cc = _place()
        b = 4 * x + 2 * y + cc
        cps = []
        for a in range(n):
            for r in range(1, N_DEV):
                dev = ((1 - x) if r & 4 else x, (1 - y) if r & 2 else y, (1 - cc) if r & 1 else cc)
                cps.append(_remote(ins[a], outs[a].at[b], send.at[a, r - 1], recv.at[a, r - 1], dev))
        return cps

    lands = [jax.ShapeDtypeStruct((N_DEV,) + a.shape, a.dtype) for a in arrs]
    return lands, (n, N_DEV - 1), copies


def _exchange_partials(pbs):
    lands, sem_shape, copies = _exchange_parts(pbs)
    start, finish = _direct_phases(copies)
    return _Comm(pbs, lands, {}, [pltpu.SemaphoreType.DMA(sem_shape), pltpu.SemaphoreType.DMA(sem_shape)],
                 start, finish)


def _join_parts(fulls):
    n = len(fulls)

    def copies(bufs, send, recv):
        x, y, cc = _place()
        cps = []
        for a in range(n):
            h = fulls[a].shape[0] // 2
            mine = bufs[a].at[pl.ds(cc * h, h), :]
            cps.append(_remote(mine, mine, send.at[a, 0], recv.at[a, 0], (x, y, 1 - cc)))
        return cps

    return (n, 1), copies


def _join_halves(fulls):
    sem_shape, copies = _join_parts(fulls)
    start, finish = _direct_phases(lambda ins, outs, send, recv: copies(outs, send, recv))
    return _Comm(fulls, [jax.ShapeDtypeStruct(f.shape, F32) for f in fulls], {i: i for i in range(len(fulls))},
                 [pltpu.SemaphoreType.DMA(sem_shape), pltpu.SemaphoreType.DMA(sem_shape)], start, finish)


def _mixer_fwd(x, mod, g1, w_in, dww, dwb, lng, lnb, w_pw, wg, pscale, w_out, comm=None):
    s = x.shape[0]
    ts = _token_tile(s)
    nt = s // ts

    def body(x_ref, mod_ref, g1_ref, win_ref, dww_ref, dwb_ref, lng_ref, lnb_ref, wpw_ref, wg_ref, ps_ref,
             wout_ref, x2_ref, y_ref, u_ref, z_ref, rstd_ref, p_ref, ycat_ref, gpad, vpad, gshift):
        i = pl.program_id(0)

        @pl.when(i == 0)
        def _():
            gpad[0:CONV_HALO, :] = jnp.zeros((CONV_HALO, CONV_W), F32)
            vpad[0:POOL_HALO, :] = jnp.zeros((POOL_HALO, POOL_W), F32)

        xt = x_ref[...]
        sh1 = mod_ref[0:1, :]
        sc1 = mod_ref[1:2, :]
        gt1 = mod_ref[2:3, :]
        r1 = lax.rsqrt(jnp.mean(xt * xt, axis=-1, keepdims=True) + EPS)
        h1 = (xt * r1 * g1_ref[...]) * (1.0 + sc1) + sh1
        h1b = h1.astype(MXU_DTYPE)
        u = jnp.concatenate([_dot(h1b, win_ref[j]) for j in range(N_CHIP)], axis=1)
        u_ref[...] = u
        a = u[:, :CONV_W]
        g = u[:, CONV_W:2 * CONV_W]
        v = u[:, 2 * CONV_W:]

        gpad[CONV_HALO:CONV_HALO + ts, :] = a * _sigmoid(g)
        window = _row_shifts(gpad, gshift, ts + CONV_HALO - SUBLANES)
        cv = jnp.broadcast_to(dwb_ref[...], (ts, CONV_W))
        off = CONV_HALO - (CONV_K - 1)
        for k in range(CONV_K):
            cv = cv + dww_ref[k:k + 1, :] * window(off + k, ts)
        gpad[0:CONV_HALO, :] = gpad[ts:ts + CONV_HALO, :]

        mu = jnp.mean(cv, axis=-1, keepdims=True)
        cc = cv - mu
        rstd = lax.rsqrt(jnp.mean(cc * cc, axis=-1, keepdims=True) + EPS)
        z = cc * rstd
        z_ref[...] = z
        rstd_ref[...] = rstd
        ln = z * lng_ref[...] + lnb_ref[...]
        sw = ln * _sigmoid(ln)
        yconv = _dot(sw, wpw_ref[...])

        vpad[POOL_HALO:POOL_HALO + ts, :] = v
        t = i * ts + lax.broadcasted_iota(jnp.int32, (ts, 1), 0)
        ps, ypool = [], []
        for gi, w in enumerate(POOL_WINDOWS):
            cols = slice(gi * POOL_G, (gi + 1) * POOL_G)
            acc = vpad[POOL_HALO:POOL_HALO + ts, cols]
            for d in range(1, w):
                acc = acc + vpad[POOL_HALO - d:POOL_HALO - d + ts, cols]
            cnt = jnp.minimum(t + 1, w).astype(F32)
            pg = (acc / cnt - v[:, cols]).astype(MXU_DTYPE)
            ps.append(pg)
            ypool.append(_dot(pg, wg_ref[gi]))
        vpad[0:POOL_HALO, :] = vpad[ts:ts + POOL_HALO, :]
        p_ref[...] = jnp.concatenate(ps, axis=1)
        ypool = jnp.concatenate(ypool, axis=1) * ps_ref[...]

        ycat = jnp.concatenate([yconv, ypool], axis=1).astype(MXU_DTYPE)
        ycat_ref[...] = ycat
        y = _dot(ycat, wout_ref[...])
        y_ref[...] = y
        x2_ref[...] = xt + gt1 * y

    tile = lambda w: pl.BlockSpec((ts, w), lambda i: (i, 0))
    return _call(
        body, name="mixer_fwd", grid=(nt,),
        in_specs=[tile(D_MODEL), _full(mod.shape), _full(g1.shape), _full(w_in.shape), _full(dww.shape),
                  _full(dwb.shape), _full(lng.shape), _full(lnb.shape), _full(w_pw.shape), _full(wg.shape),
                  _full(pscale.shape), _full(w_out.shape)],
        out_specs=[tile(D_MODEL), tile(D_MODEL), tile(IN_W), tile(CONV_W), tile(1), tile(POOL_W), tile(D_MODEL)],
        out_shape=[jax.ShapeDtypeStruct((s, D_MODEL), F32), jax.ShapeDtypeStruct((s, D_MODEL), F32),
                   jax.ShapeDtypeStruct((s, IN_W), F32), jax.ShapeDtypeStruct((s, CONV_W), F32),
                   jax.ShapeDtypeStruct((s, 1), F32), jax.ShapeDtypeStruct((s, POOL_W), MXU_DTYPE),
                   jax.ShapeDtypeStruct((s, D_MODEL), MXU_DTYPE)],
        scratch_shapes=[pltpu.VMEM((ts + CONV_HALO, CONV_W), F32), pltpu.VMEM((ts + POOL_HALO, POOL_W), F32),
                        pltpu.VMEM((SUBLANES - 1, ts + CONV_HALO - SUBLANES, CONV_W), F32)],
        args=(x, mod, g1, w_in, dww, dwb, lng, lnb, w_pw, wg, pscale, w_out), comm=comm)


def _ffn(x2, tgt, mod, g2, gf, w_gate, w_up, w_down):
    s = x2.shape[0]
    ts = _token_tile(s)
    nt = s // ts
    fb = w_gate.shape[1]

    def body(x2_ref, tgt_ref, mod_ref, g2_ref, gf_ref, wgt_ref, wup_ref, wdn_ref,
             dx2_ref, h2_ref, df_ref, act_ref, dgg_ref, duu_ref, vec_ref, gg_s, uu_s):
        i = pl.program_id(0)

        @pl.when(i == 0)
        def _():
            vec_ref[...] = jnp.zeros(vec_ref.shape, F32)

        x2t = x2_ref[...]
        sh2 = mod_ref[3:4, :]
        sc2 = mod_ref[4:5, :]
        gt2 = mod_ref[5:6, :]
        g2v = g2_ref[...]
        gfv = gf_ref[...]
        r2 = lax.rsqrt(jnp.mean(x2t * x2t, axis=-1, keepdims=True) + EPS)
        xh2 = x2t * r2
        n2 = xh2 * g2v
        h2b = (n2 * (1.0 + sc2) + sh2).astype(MXU_DTYPE)
        h2_ref[...] = h2b
        f = jnp.zeros((ts, D_MODEL), F32)
        for j in range(N_CHIP):
            gg = _dot_nt(h2b, wgt_ref[j])
            uu = _dot_nt(h2b, wup_ref[j])
            gg_s[j] = gg
            uu_s[j] = uu
            actb = (gg * _sigmoid(gg) * uu).astype(MXU_DTYPE)
            act_ref[j] = actb
            f = f + _dot(actb, wdn_ref[j])
        x3 = x2t + gt2 * f
        r3 = lax.rsqrt(jnp.mean(x3 * x3, axis=-1, keepdims=True) + EPS)
        xh3 = x3 * r3
        diff = xh3 * gfv - tgt_ref[...]
        dout = diff * (1.0 / D_MODEL)
        dn3 = dout * gfv
        dx3 = r3 * (dn3 - xh3 * jnp.mean(dn3 * xh3, axis=-1, keepdims=True))
        dfb = (dx3 * gt2).astype(MXU_DTYPE)
        df_ref[...] = dfb
        dh2 = jnp.zeros((ts, D_MODEL), F32)
        for j in range(N_CHIP):
            dact = _dot_nt(dfb, wdn_ref[j])
            gg = gg_s[j]
            uu = uu_s[j]
            sg = _sigmoid(gg)
            duu = (dact * (gg * sg)).astype(MXU_DTYPE)
            dgg = (dact * uu * (sg * (1.0 + gg * (1.0 - sg)))).astype(MXU_DTYPE)
            duu_ref[j] = duu
            dgg_ref[j] = dgg
            dh2 = dh2 + _dot(dgg, wgt_ref[j]) + _dot(duu, wup_ref[j])
        dn2 = dh2 * (1.0 + sc2)
        dxh2 = dn2 * g2v
        dx2_ref[...] = dx3 + r2 * (dxh2 - xh2 * jnp.mean(dxh2 * xh2, axis=-1, keepdims=True))

        col = lambda a: jnp.sum(a, axis=0, keepdims=True)
        vec_ref[0:1, :] += col(dout * xh3)
        vec_ref[1:2, :] += col(dx3 * f)
        vec_ref[2:3, :] += col(dh2)
        vec_ref[3:4, :] += col(dh2 * n2)
        vec_ref[4:5, :] += col(dn2 * xh2)
        vec_ref[5:6, :] += col(diff * diff)

    tile = lambda w: pl.BlockSpec((ts, w), lambda i: (i, 0))
    tile3 = pl.BlockSpec((N_CHIP, ts, fb), lambda i: (0, i, 0))
    once = lambda a: pl.BlockSpec(a.shape, lambda i: (0,) * a.ndim, pipeline_mode=pl.Buffered(1))
    hid = jax.ShapeDtypeStruct((N_CHIP, s, fb), MXU_DTYPE)
    return pl.pallas_call(
        body, name="ffn", grid=(nt,),
        in_specs=[tile(D_MODEL), tile(D_MODEL), _full(mod.shape), _full(g2.shape), _full(gf.shape),
                  once(w_gate), once(w_up), once(w_down)],
        out_specs=[tile(D_MODEL), tile(D_MODEL), tile(D_MODEL), tile3, tile3, tile3, _full((8, D_MODEL))],
        out_shape=[jax.ShapeDtypeStruct((s, D_MODEL), F32), jax.ShapeDtypeStruct((s, D_MODEL), MXU_DTYPE),
                   jax.ShapeDtypeStruct((s, D_MODEL), MXU_DTYPE), hid, hid, hid,
                   jax.ShapeDtypeStruct((8, D_MODEL), F32)],
        scratch_shapes=[pltpu.VMEM((N_CHIP, ts, fb), F32), pltpu.VMEM((N_CHIP, ts, fb), F32)],
        compiler_params=pltpu.CompilerParams(dimension_semantics=("arbitrary",)),
    )(x2, tgt, mod, g2, gf, w_gate, w_up, w_down)


def _mixer_bwd(dx2, x, y, u, z, rstd, p, mod, g1, w_in, dww, lng, lnb, w_pw, wg, pscale, w_out, comm=None, after=()):
    s = x.shape[0]
    ts = _token_tile(s)
    nt = s // ts

    def body(dx2_ref, x_ref, y_ref, u_ref, z_ref, rstd_ref, p_ref, mod_ref, g1_ref, win_ref, dww_ref, lng_ref,
             lnb_ref, wpw_ref, wg_ref, ps_ref, wout_ref,
             gx_ref, h1_ref, du_ref, dy_ref, sw_ref, dyc_ref, dyp_ref, vd_ref, vc_ref, ddw_ref, dcpad, dppad,
             dshift):
        i = pl.program_id(0)
        tix = nt - 1 - i

        @pl.when(i == 0)
        def _():
            vd_ref[...] = jnp.zeros(vd_ref.shape, F32)
            vc_ref[...] = jnp.zeros(vc_ref.shape, F32)
            ddw_ref[...] = jnp.zeros(ddw_ref.shape, F32)
            dcpad[ts:ts + CONV_HALO, :] = jnp.zeros((CONV_HALO, CONV_W), F32)
            dppad[ts:ts + POOL_HALO, :] = jnp.zeros((POOL_HALO, POOL_W), F32)

        col = lambda a: jnp.sum(a, axis=0, keepdims=True)
        sh1 = mod_ref[0:1, :]
        sc1 = mod_ref[1:2, :]
        gt1 = mod_ref[2:3, :]
        dx2t = dx2_ref[...]
        vd_ref[0:1, :] += col(dx2t * y_ref[...])
        dyb = (dx2t * gt1).astype(MXU_DTYPE)
        dy_ref[...] = dyb
        dycat = _dot_nt(dyb, wout_ref[...])
        dyconv = dycat[:, :CONV_W]
        dypool = dycat[:, CONV_W:]

        pt = p_ref[...]
        t = tix * ts + lax.broadcasted_iota(jnp.int32, (ts, 1), 0)
        psc = ps_ref[...]
        dypb = (dypool * psc).astype(MXU_DTYPE)
        dyp_ref[...] = dypb
        dps, ypre = [], []
        for gi, w in enumerate(POOL_WINDOWS):
            cols = slice(gi * POOL_G, (gi + 1) * POOL_G)
            ypre.append(_dot(pt[:, cols], wg_ref[gi]))
            dpg = _dot_nt(dypb[:, cols], wg_ref[gi])
            dps.append(dpg)
            cnt = jnp.minimum(t + 1, w).astype(F32)
            dppad[0:ts, cols] = dpg / cnt
        vc_ref[0:1, :] += col(dypool * jnp.concatenate(ypre, axis=1))
        dvs = []
        for gi, w in enumerate(POOL_WINDOWS):
            cols = slice(gi * POOL_G, (gi + 1) * POOL_G)
            acc = dppad[0:ts, cols]
            for d in range(1, w):
                acc = acc + dppad[d:d + ts, cols]
            dvs.append(acc - dps[gi])
        dv = jnp.concatenate(dvs, axis=1)
        dppad[ts:ts + POOL_HALO, :] = dppad[0:POOL_HALO, :]

        zt = z_ref[...]
        lngv = lng_ref[...]
        ln = zt * lngv + lnb_ref[...]
        sg = _sigmoid(ln)
        swb = (ln * sg).astype(MXU_DTYPE)
        sw_ref[...] = swb
        dycb = dyconv.astype(MXU_DTYPE)
        dyc_ref[...] = dycb
        dln = _dot_nt(dycb, wpw_ref[...]) * (sg * (1.0 + ln * (1.0 - sg)))
        vc_ref[1:2, :] += col(dln * zt)
        vc_ref[2:3, :] += col(dln)
        dz = dln * lngv
        dcv = rstd_ref[...] * (dz - jnp.mean(dz, axis=-1, keepdims=True)
                               - zt * jnp.mean(dz * zt, axis=-1, keepdims=True))
        vc_ref[3:4, :] += col(dcv)
        dcpad[0:ts, :] = dcv
        ut = u_ref[...]
        a = ut[:, :CONV_W]
        g = ut[:, CONV_W:2 * CONV_W]
        sgg = _sigmoid(g)
        glu = a * sgg
        window = _row_shifts(dcpad, dshift, ts + CONV_HALO - SUBLANES)
        dglu = jnp.zeros((ts, CONV_W), F32)
        for k in range(CONV_K):
            sh = window(CONV_K - 1 - k, ts)
            dglu = dglu + dww_ref[k:k + 1, :] * sh
            ddw_ref[k:k + 1, :] += col(glu * sh)
        dcpad[ts:ts + CONV_HALO, :] = dcpad[0:CONV_HALO, :]
        da = dglu * sgg
        dg = dglu * a * sgg * (1.0 - sgg)
        dub = jnp.concatenate([da, dg, dv], axis=1).astype(MXU_DTYPE)
        du_ref[...] = dub
        cw = IN_W // N_CHIP
        dh1 = jnp.zeros((ts, D_MODEL), F32)
        for j in range(N_CHIP):
            dh1 = dh1 + _dot_nt(dub[:, j * cw:(j + 1) * cw], win_ref[j])

        xt = x_ref[...]
        g1v = g1_ref[...]
        r1 = lax.rsqrt(jnp.mean(xt * xt, axis=-1, keepdims=True) + EPS)
        xh1 = xt * r1
        n1 = xh1 * g1v
        h1_ref[...] = (n1 * (1.0 + sc1) + sh1).astype(MXU_DTYPE)
        vd_ref[1:2, :] += col(dh1)
        vd_ref[2:3, :] += col(dh1 * n1)
        dn1 = dh1 * (1.0 + sc1)
        vd_ref[3:4, :] += col(dn1 * xh1)
        dxh = dn1 * g1v
        gx_ref[...] = dx2t + r1 * (dxh - xh1 * jnp.mean(dxh * xh1, axis=-1, keepdims=True))

    tile = lambda w: pl.BlockSpec((ts, w), lambda i: (nt - 1 - i, 0))
    bf = lambda w: jax.ShapeDtypeStruct((s, w), MXU_DTYPE)
    return _call(
        body, name="mixer_bwd", grid=(nt,),
        in_specs=[tile(D_MODEL), tile(D_MODEL), tile(D_MODEL), tile(IN_W), tile(CONV_W), tile(1), tile(POOL_W),
                  _full(mod.shape), _full(g1.shape), _full(w_in.shape), _full(dww.shape), _full(lng.shape),
                  _full(lnb.shape), _full(w_pw.shape), _full(wg.shape), _full(pscale.shape), _full(w_out.shape)],
        out_specs=[tile(D_MODEL), tile(D_MODEL), tile(IN_W), tile(D_MODEL), tile(CONV_W), tile(CONV_W),
                   tile(POOL_W), _full((8, D_MODEL)), _full((8, CONV_W)), _full((32, CONV_W))],
        out_shape=[jax.ShapeDtypeStruct((s, D_MODEL), F32), bf(D_MODEL), bf(IN_W), bf(D_MODEL), bf(CONV_W),
                   bf(CONV_W), bf(POOL_W), jax.ShapeDtypeStruct((8, D_MODEL), F32),
                   jax.ShapeDtypeStruct((8, CONV_W), F32), jax.ShapeDtypeStruct((32, CONV_W), F32)],
        scratch_shapes=[pltpu.VMEM((ts + CONV_HALO, CONV_W), F32), pltpu.VMEM((ts + POOL_HALO, POOL_W), F32),
                        pltpu.VMEM((SUBLANES - 1, ts + CONV_HALO - SUBLANES, CONV_W), F32)],
        args=(dx2, x, y, u, z, rstd, p, mod, g1, w_in, dww, lng, lnb, w_pw, wg, pscale, w_out), comm=comm,
        after=after)


def _dw(name, a, a_spec, b, b_spec, nb, mb, nbk, comm=None, after=()):
    def body(a_ref, b_ref, o_ref):
        av = a_ref[...]
        bv = b_ref[...]
        av = av.reshape(av.shape[-2:])
        bv = bv.reshape(bv.shape[-2:])
        o_ref[0] = _dot_tn(av, bv)

    (out,), rest = _call(
        body, name=name, grid=(nb,), in_specs=[a_spec, b_spec],
        out_specs=[pl.BlockSpec((1, mb, nbk), lambda j: (j, 0, 0))],
        out_shape=[jax.ShapeDtypeStruct((nb, mb, nbk), F32)], args=(a, b), comm=comm, after=after)
    return out, rest


def _dw_mixer(ycat, dy, sw, dyc, h1, du, after=()):
    s = ycat.shape[0]

    def body(ycat_ref, dy_ref, sw_ref, dyc_ref, h1_ref, du_ref, out_ref, pw_ref, in_ref):
        out_ref[0] = _dot_tn(ycat_ref[...], dy_ref[...])
        pw_ref[0] = _dot_tn(sw_ref[...], dyc_ref[...])
        in_ref[0] = _dot_tn(h1_ref[...], du_ref[...])

    whole = lambda w: pl.BlockSpec((s, w), lambda j: (0, 0))
    cols = lambda w: pl.BlockSpec((s, w), lambda j: (0, j))
    blk = lambda m, n: pl.BlockSpec((1, m, n), lambda j: (j, 0, 0))
    shapes = [(D_MODEL // N_CHIP, D_MODEL), (CONV_W // N_CHIP, CONV_W), (D_MODEL, IN_W // N_CHIP)]
    res, _ = _call(
        body, name="dw_mixer", grid=(N_CHIP,),
        in_specs=[cols(D_MODEL // N_CHIP), whole(D_MODEL), cols(CONV_W // N_CHIP), whole(CONV_W), whole(D_MODEL),
                  cols(IN_W // N_CHIP)],
        out_specs=[blk(m, n) for m, n in shapes],
        out_shape=[jax.ShapeDtypeStruct((N_CHIP, m, n), F32) for m, n in shapes],
        args=(ycat, dy, sw, dyc, h1, du), after=after)
    return res


def _ada_fwd(c, w_ada, b4, first, later, dww, wg, comm):
    nc = w_ada.shape[1]
    nf, nl = len(first), len(later)
    shards = list(first) + list(later)

    def body(start_comm, gathered, c_ref, w_ref, b4_ref, *refs):
        shard_refs, refs = refs[:nf + nl], refs[nf + nl:]
        dww_ref, wg_ref, mod_ref, cact_ref, wgb_ref = refs[:5]
        later_refs, refs = refs[5:5 + nl], refs[5 + nl:]
        call, part, parts = refs[:3]
        stages, refs = refs[3:3 + nf + nl], refs[3 + nf + nl:]
        send1, recv1, send2, recv2, lsem = refs
        x, y, cc = _place()
        b = 4 * x + 2 * y + cc
        j = 2 * x + y

        def slot_copies(lo, hi):
            cps = []
            for a in range(lo, hi):
                dst = gathered[a] if a < nf else later_refs[a - nf]
                cps.append(pltpu.make_async_copy(stages[a], dst.at[j], lsem.at[a]))
            return cps

        call[b] = c_ref[...]
        sends = []
        for r in range(1, N_DEV):
            dev = ((1 - x) if r & 4 else x, (1 - y) if r & 2 else y, (1 - cc) if r & 1 else cc)
            cp = _remote(call.at[b], call.at[b], send1.at[r - 1], recv1.at[r - 1], dev)
            cp.start()
            sends.append(cp)
        for a in range(nf):
            stages[a][...] = shard_refs[a][...].astype(MXU_DTYPE)
        dww_copy = pltpu.make_async_copy(dww_ref, gathered[nf].at[j], lsem.at[nf + nl])
        dww_copy.start()
        for cp in slot_copies(0, nf):
            cp.start()
        for r in range(1, N_DEV):
            src_b = lax.bitwise_xor(b, r)
            _remote(call.at[src_b], call.at[src_b], send1.at[r - 1], recv1.at[r - 1], (x, y, cc)).wait_recv()
        for cp in sends:
            cp.wait_send()
        for cp in slot_copies(0, nf):
            cp.wait()
        dww_copy.wait()
        start_comm()
        for i in range(N_DEV):
            ci = call[i]
            cact_ref[i:i + 1, :] = ci * _sigmoid(ci)
        part[...] = jnp.dot(cact_ref[...], w_ref[...], preferred_element_type=F32, precision=lax.Precision.HIGHEST)
        sends = []
        for r in range(1, N_CHIP):
            kx, ky = _flip(x, y, r)
            cp = _remote(part, parts.at[j], send2.at[r - 1], recv2.at[r - 1], (kx, ky, cc))
            cp.start()
            sends.append(cp)
        parts[j] = part[...]
        for a in range(nf, nf + nl):
            stages[a][...] = shard_refs[a][...].astype(MXU_DTYPE)
        for cp in slot_copies(nf, nf + nl):
            cp.start()
        wgb_ref[...] = wg_ref[...].astype(MXU_DTYPE)
        for r in range(1, N_CHIP):
            kx, ky = _flip(x, y, r)
            kj = 2 * kx + ky
            _remote(part, parts.at[kj], send2.at[r - 1], recv2.at[r - 1], (x, y, cc)).wait_recv()
        for cp in sends:
            cp.wait_send()
        mine = lax.broadcasted_iota(jnp.int32, (N_DEV, 1), 0) == b
        for k in range(N_CHIP):
            row = jnp.sum(jnp.where(mine, parts[k], 0.0), axis=0, keepdims=True)
            mod_ref[k:k + 1, :] = row + b4_ref[k:k + 1, :]
        for cp in slot_copies(nf, nf + nl):
            cp.wait()

    res, rest = _call(
        body, name="ada_fwd", grid=(1,),
        in_specs=[VMEM] * (5 + nf + nl), out_specs=[VMEM, VMEM, VMEM] + [ANY] * nl,
        out_shape=[jax.ShapeDtypeStruct((N_CHIP, nc), F32), jax.ShapeDtypeStruct((N_DEV, D_MODEL), F32),
                   jax.ShapeDtypeStruct(wg.shape, MXU_DTYPE)]
        + [jax.ShapeDtypeStruct((N_CHIP,) + a.shape, MXU_DTYPE) for a in later],
        scratch_shapes=[pltpu.VMEM((N_DEV, 1, D_MODEL), F32), pltpu.VMEM((N_DEV, nc), F32),
                        pltpu.VMEM((N_CHIP, N_DEV, nc), F32)]
        + [pltpu.VMEM(a.shape, MXU_DTYPE) for a in shards]
        + [pltpu.SemaphoreType.DMA((N_DEV - 1,)), pltpu.SemaphoreType.DMA((N_DEV - 1,)),
           pltpu.SemaphoreType.DMA((N_CHIP - 1,)), pltpu.SemaphoreType.DMA((N_CHIP - 1,)),
           pltpu.SemaphoreType.DMA((nf + nl + 1,))],
        args=(c, w_ada, b4, *shards, dww, wg), comm=comm, body_starts=True)
    return (res[0], res[1], res[2], res[3:]), rest


def _chip_partials(name, place, gs, rs, comm=None, after=()):
    n = len(gs)

    def body(pref, *refs):
        g_refs, r_refs = refs[:n], refs[n:2 * n]
        pb_refs, own_refs = refs[2 * n:3 * n], refs[3 * n:]
        jj = pl.program_id(0)
        for a in range(n):
            sm = g_refs[a][0] + r_refs[a][0]
            pb_refs[a][0] = sm.astype(MXU_DTYPE)

            @pl.when(jj == pref[1])
            def _(a=a, sm=sm):
                own_refs[a][...] = sm

    halves = [(g.shape[1] // 2, g.shape[2]) for g in gs]
    in_specs = [pl.BlockSpec((1, h, w), lambda jj, pref: (jj, pref[0], 0)) for h, w in halves]
    in_specs += [pl.BlockSpec((1, h, w), lambda jj, pref: (jj, 0, 0)) for h, w in halves]
    out_specs = [pl.BlockSpec((1, h, w), lambda jj, pref: (jj, 0, 0)) for h, w in halves]
    out_specs += [pl.BlockSpec((h, w), lambda jj, pref: (0, 0)) for h, w in halves]
    out, rest = _call(
        body, name=name, grid=(N_CHIP,), in_specs=in_specs, out_specs=out_specs,
        out_shape=[jax.ShapeDtypeStruct((N_CHIP, h, w), MXU_DTYPE) for h, w in halves]
        + [jax.ShapeDtypeStruct((h, w), F32) for h, w in halves],
        args=(*gs, *rs), prefetch=(place,), comm=comm, after=after)
    return (out[:n], out[n:]), rest


def _sum_partials(name, place, owns, recvd, comm=None, after=()):
    n = len(owns)

    def body(pref, *refs):
        o_refs, r_refs, out_refs = refs[:n], refs[n:2 * n], refs[2 * n:]
        for a in range(n):
            acc = o_refs[a][...]
            for r in range(N_CHIP - 1):
                acc = acc + r_refs[a][r].astype(F32)
            out_refs[a][...] = acc

    full = lambda a: pl.BlockSpec(a.shape, lambda i, pref: (0,) * a.ndim)
    return _call(
        body, name=name, grid=(1,), in_specs=[full(a) for a in list(owns) + list(recvd)],
        out_specs=[pl.BlockSpec(o.shape, lambda i, pref: (pref[0], 0)) for o in owns],
        out_shape=[jax.ShapeDtypeStruct((2 * o.shape[0], o.shape[1]), F32) for o in owns],
        args=(*owns, *recvd), prefetch=(place,), comm=comm, after=after)


def _adamw_math(w, g, m, v):
    m = ADAM_B1 * m + (1.0 - ADAM_B1) * g
    v = ADAM_B2 * v + (1.0 - ADAM_B2) * (g * g)
    m_hat = m / (1.0 - ADAM_B1 ** ADAM_STEP)
    v_hat = v / (1.0 - ADAM_B2 ** ADAM_STEP)
    delta = -ADAM_LR * (m_hat / (jnp.sqrt(v_hat) + ADAM_EPS) + ADAM_WD * w)
    return delta, m, v


def _row_tile(rows):
    for t in (512, 352, 256, 128):
        if rows % t == 0:
            return t
    return rows


def _adamw(name, wgmv, steps, after=()):
    n = len(wgmv)

    def body(*refs):
        ins, outs = refs[:4 * n], refs[4 * n:]
        for i in range(n):
            w_ref, g_ref, m_ref, v_ref = ins[4 * i:4 * i + 4]
            d_ref, nm_ref, nv_ref = outs[3 * i:3 * i + 3]
            d_ref[...], nm_ref[...], nv_ref[...] = _adamw_math(w_ref[...], g_ref[...], m_ref[...], v_ref[...])

    in_specs, out_specs, out_shape, args = [], [], [], []
    for w, g, m, v in wgmv:
        rows, cols = w.shape
        spec = pl.BlockSpec((rows // steps, cols), lambda i: (i, 0))
        in_specs += [spec] * 4
        out_specs += [spec] * 3
        out_shape += [jax.ShapeDtypeStruct(w.shape, F32)] * 3
        args += [w, g, m, v]
    res, _ = _call(body, name=name, grid=(steps,), in_specs=in_specs, out_specs=out_specs, out_shape=out_shape,
                   args=args, after=after)
    return [res[3 * i:3 * i + 3] for i in range(n)]


def _adamw_ada(place, cact, dmod, w, m, v, after=()):
    rows, cols = w.shape
    tr = _row_tile(rows)

    def body(pref, ca_ref, dm_ref, w_ref, m_ref, v_ref, g_ref, d_ref, nm_ref, nv_ref):
        g = lax.dot_general(ca_ref[...], dm_ref[...], (((0,), (0,)), ((), ())), preferred_element_type=F32,
                            precision=lax.Precision.HIGHEST)
        g_ref[...] = g
        d_ref[...], nm_ref[...], nv_ref[...] = _adamw_math(w_ref[...], g, m_ref[...], v_ref[...])

    spec = pl.BlockSpec((tr, cols), lambda i, pref: (i, 0))
    return _call(
        body, name="adamw_ada", grid=(rows // tr,),
        in_specs=[pl.BlockSpec((N_DEV, tr), lambda i, pref: (0, i)),
                  pl.BlockSpec((N_DEV, cols), lambda i, pref: (0, pref[1])), spec, spec, spec],
        out_specs=[spec] * 4, out_shape=[jax.ShapeDtypeStruct(w.shape, F32)] * 4,
        args=(cact, dmod, w, m, v), prefetch=(place,), after=after)[0]


def _adamw_small(place, owns, gathered, wmv):
    nw = len(wmv)
    flat = [a for t in wmv for a in t]

    def body(pref, *refs):
        own_refs, all_refs, refs = refs[:5], refs[5:10], refs[10:]
        w_refs = refs[:3 * nw]
        loss_ref, dmod_ref = refs[3 * nw], refs[3 * nw + 1]
        o_refs = refs[3 * nw + 2:]
        j = pref[1]
        me = 2 * pref[1] + pref[0]

        def total(i):
            acc = None
            for b in range(N_DEV):
                blk = jnp.where(me == b, own_refs[i][...], all_refs[i][b])
                acc = blk if acc is None else acc + blk
            return acc

        vf, vd, vc, ddw, gwg = [total(i) for i in range(5)]
        loss_ref[...] = (0.5 / D_MODEL) * jnp.sum(vf[5:6, :], axis=1, keepdims=True)
        order = ((1, 1), (1, 2), (1, 0), (0, 2), (0, 3), (0, 1))
        for b in range(N_DEV):
            for q, (i, row) in enumerate(order):
                dmod_ref[b:b + 1, q * D_MODEL:(q + 1) * D_MODEL] = jnp.where(
                    me == b, own_refs[i][row:row + 1, :], all_refs[i][b, row:row + 1, :])
        dm = dmod_ref[...]
        g_bada = dm[0:1, :]
        for b in range(1, N_DEV):
            g_bada = g_bada + dm[b:b + 1, :]
        g_dww = jnp.zeros((32, POOL_G), F32)
        for k in range(N_CHIP):
            g_dww = g_dww + jnp.where(j == k, ddw[:, k * POOL_G:(k + 1) * POOL_G], 0.0)
        grads = [g_bada, vd[3:4, :], g_dww, vc[3:4, :], vc[1:2, :], vc[2:3, :], gwg, vc[0:1, :], vf[4:5, :],
                 vf[0:1, :]]
        for i, g in enumerate(grads):
            w_ref, m_ref, v_ref = w_refs[3 * i:3 * i + 3]
            d, nm, nv = _adamw_math(w_ref[...], g, m_ref[...], v_ref[...])
            o_refs[4 * i][...] = g
            o_refs[4 * i + 1][...] = d
            o_refs[4 * i + 2][...] = nm
            o_refs[4 * i + 3][...] = nv

    outs = [jax.ShapeDtypeStruct((1, 1), F32), jax.ShapeDtypeStruct((N_DEV, 6 * D_MODEL), F32)]
    for w, _, _ in wmv:
        outs += [jax.ShapeDtypeStruct(w.shape, F32)] * 4
    full = lambda a: pl.BlockSpec(a.shape, lambda i, pref: (0,) * a.ndim)
    args = list(owns) + list(gathered) + flat
    res, _ = _call(body, name="adamw_small", grid=(1,), in_specs=[full(a) for a in args],
                   out_specs=[full(o) for o in outs], out_shape=outs, args=args, prefetch=(place,))
    return res[0], res[1], [res[2 + 4 * i:6 + 4 * i] for i in range(nw)]


def kernel(x, c, w_ada, b_ada, g_norm1, w_in, dw_w, dw_b, conv_ln_g, conv_ln_b, w_conv_pw, w_pool_group, pool_scale, w_out, g_norm2, w_ffn_gate, w_ffn_up, w_ffn_down, g_final, loss_target, m_w_ada, m_b_ada, m_g_norm1, m_w_in, m_dw_w, m_dw_b, m_conv_ln_g, m_conv_ln_b, m_w_conv_pw, m_w_pool_group, m_pool_scale, m_w_out, m_g_norm2, m_w_ffn_gate, m_w_ffn_up, m_w_ffn_down, m_g_final, v_w_ada, v_b_ada, v_g_norm1, v_w_in, v_dw_w, v_dw_b, v_conv_ln_g, v_conv_ln_b, v_w_conv_pw, v_w_pool_group, v_pool_scale, v_w_out, v_g_norm2, v_w_ffn_gate, v_w_ffn_up, v_w_ffn_down, v_g_final):
    xi, yi, ci = _place()
    place = jnp.stack([ci, 2 * xi + yi]).astype(jnp.int32)
    n_ada = w_ada.shape[2]

    tr = lambda a: jnp.transpose(a[0])
    mixer_shards = [w_in[0], w_conv_pw[0], w_out[0]]
    ffn_shards = [tr(w_ffn_gate), tr(w_ffn_up), w_ffn_down[0]]
    slots = [lax.empty((N_CHIP,) + a.shape, MXU_DTYPE) for a in mixer_shards] + [lax.empty((N_CHIP,) + dw_w.shape[1:], F32)]

    (mod4, cact, wg_b, (b_gate, b_up, b_down)), (win_g, wpw_g, wout_g, dww_g) = _ada_fwd(
        c, w_ada[0], b_ada.reshape(N_CHIP, n_ada), mixer_shards, ffn_shards, dw_w[0], w_pool_group[0],
        comm=_weights_gather(slots, [True, True, True, False]))
    mod = mod4.reshape(6, D_MODEL)
    dww_full = jnp.pad(jnp.concatenate([dww_g[k] for k in range(N_CHIP)], axis=1), ((0, 1), (0, 0)))
    w_pw = wpw_g.reshape(CONV_W, CONV_W)
    w_o = wout_g.reshape(D_MODEL, D_MODEL)
    xs, tgt, gf = x[0], loss_target[0], g_final.reshape(1, D_MODEL)
    s = xs.shape[0]
    fb = b_gate.shape[1]

    (x2, y, u, z, rstd, p, ycat), (wgate_g, wup_g, wdown_g) = _mixer_fwd(
        xs, mod, g_norm1, win_g, dww_full, dw_b, conv_ln_g, conv_ln_b, w_pw, wg_b, pool_scale, w_o,
        comm=_weights_gather([b_gate, b_up, b_down], [True, True, True]))
    dx2, h2, df, act, dgg, duu, vec_f = _ffn(x2, tgt, mod, g_norm2, gf, wgate_g, wup_g, wdown_g)

    whole = lambda w: pl.BlockSpec((s, w), lambda j: (0, 0))
    cols = lambda w: pl.BlockSpec((s, w), lambda j: (0, j))
    hid = pl.BlockSpec((1, s, fb), lambda j: (j, 0, 0))
    c_gate, _ = _dw("dw_gate", dgg, hid, h2, whole(D_MODEL), N_CHIP, fb, D_MODEL)
    c_up, (r_gate,) = _dw("dw_up", duu, hid, h2, whole(D_MODEL), N_CHIP, fb, D_MODEL, comm=_sibling_halves([c_gate]))
    ((pb_gate,), (own_gate,)), _ = _chip_partials("partials_gate", place, [c_gate], [r_gate])
    ex_gate = _exchange_parts([pb_gate])
    st_gate, tok_gate = _split_start("exchange_gate_start", [pb_gate], *ex_gate)
    c_down, (r_up,) = _dw("dw_down", act, hid, df, whole(D_MODEL), N_CHIP, fb, D_MODEL,
                          comm=_sibling_halves([c_up]), after=(tok_gate,))
    sib_down = _sibling_parts([c_down])
    st_sd, tok_sd = _split_start("sibling_down_start", [c_down], *sib_down)
    ((pb_up,), (own_up,)), _ = _chip_partials("partials_up", place, [c_up], [r_up], after=(tok_sd,))
    (c_down,), (r_down,) = _split_wait("sibling_down_wait", st_sd, 1, sib_down[1], sib_down[2], after=(pb_up,),
                                       with_sources=True)
    ((pb_down,), (own_down,)), _ = _chip_partials("partials_down", place, [c_down], [r_down])
    ex_ud = _exchange_parts([pb_up, pb_down])
    st_ud, tok_ud = _split_start("exchange_up_down_start", [pb_up, pb_down], *ex_ud)
    (gx, h1, du, dy, sw, dyc, dyp, vec_d, vec_c, ddw), _ = _mixer_bwd(
        dx2, xs, y, u, z, rstd, p, mod, g_norm1, win_g, dww_full, conv_ln_g, conv_ln_b, w_pw, wg_b, pool_scale, w_o,
        after=(tok_ud,))

    g_wg, _ = _dw("dw_wg", p, cols(POOL_G), dyp, cols(POOL_G), len(POOL_WINDOWS), POOL_G, POOL_G)
    small_own = [vec_f, vec_d, vec_c, ddw, g_wg]
    ex_small = _small_parts(small_own)
    st_small, tok_small = _split_start("small_grads_start", small_own, *ex_small, zeroed=True)
    c_out, c_pw, c_in = _dw_mixer(ycat, dy, sw, dyc, h1, du, after=(tok_small,))
    mix = [c_in, c_pw, c_out]
    sib_mix = _sibling_parts(mix)
    st_sm, tok_sm = _split_start("sibling_mix_start", mix, *sib_mix)
    (rc_gate,) = _split_wait("exchange_gate_wait", st_gate, 1, ex_gate[1], ex_gate[2], after=(tok_sm,))
    rc_up, rc_down = _split_wait("exchange_up_down_wait", st_ud, 2, ex_ud[1], ex_ud[2], after=(tok_sm, rc_gate))

    ffn_fulls, _ = _sum_partials("sum_ffn", place, [own_gate, own_up, own_down], [rc_gate, rc_up, rc_down])
    join_sems, join_copies = _join_parts(ffn_fulls)
    join_in_place = lambda bufs, lands, send, recv: join_copies(bufs, send, recv)
    st_jf, tok_jf = _split_start("join_ffn_start", ffn_fulls, [], join_sems, join_in_place)
    mix, (r_in, r_pw, r_out) = _split_wait("sibling_mix_wait", st_sm, len(mix), sib_mix[1], sib_mix[2],
                                           after=(tok_jf,), with_sources=True)
    (pbs_mix, owns_mix), _ = _chip_partials("partials_mix", place, mix, [r_in, r_pw, r_out])
    g_gate, g_up, g_down = _split_wait("join_ffn_wait", st_jf, len(ffn_fulls), join_sems, join_in_place,
                                       after=(pbs_mix[0],), in_place=True)

    pad_rows = lambda a: jnp.pad(a[0], ((0, 1), (0, 0)))
    row = lambda a: a.reshape(1, -1)
    small = [(b_ada, m_b_ada, v_b_ada), (g_norm1, m_g_norm1, v_g_norm1),
             (pad_rows(dw_w), pad_rows(m_dw_w), pad_rows(v_dw_w)), (dw_b, m_dw_b, v_dw_b),
             (conv_ln_g, m_conv_ln_g, v_conv_ln_g), (conv_ln_b, m_conv_ln_b, v_conv_ln_b),
             (w_pool_group[0], m_w_pool_group[0], v_w_pool_group[0]), (pool_scale, m_pool_scale, v_pool_scale),
             (g_norm2, m_g_norm2, v_g_norm2), (row(g_final), row(m_g_final), row(v_g_final))]
    lead = lambda outs: [a[None] for a in outs]

    lands, sem_shape, copies = _exchange_parts(pbs_mix)
    state, token = _split_start("exchange_mix_start", pbs_mix, lands, sem_shape, copies)
    u_gate, u_up, u_down = _adamw(
        "adamw_ffn", [(tr(w_ffn_gate), g_gate, tr(m_w_ffn_gate), tr(v_w_ffn_gate)),
                      (tr(w_ffn_up), g_up, tr(m_w_ffn_up), tr(v_w_ffn_up)),
                      (w_ffn_down[0], g_down, m_w_ffn_down[0], v_w_ffn_down[0])],
        steps=4, after=(token,))
    o_gate = [jnp.transpose(o) for o in [g_gate] + list(u_gate)]
    o_up = [jnp.transpose(o) for o in [g_up] + list(u_up)]
    o_down = [g_down] + list(u_down)
    small_own, small_all = _split_wait("small_grads_wait", st_small, len(small_own), ex_small[1], ex_small[2],
                                       after=(u_down[0],), with_sources=True)
    loss, dmod, small_out = _adamw_small(place, small_own, small_all, small)
    (o_bada, o_g1, o_dww, o_dwb, o_lng, o_lnb, o_wg, o_ps, o_g2, o_gf) = small_out
    o_dww = [a[:CONV_K] for a in o_dww]
    o_gf = [a.reshape(D_MODEL) for a in o_gf]
    o_ada = _adamw_ada(place, cact, dmod, w_ada[0], m_w_ada[0], v_w_ada[0], after=(token,))
    rc_mix = _split_wait("exchange_mix_wait", state, len(pbs_mix), sem_shape, copies, after=(o_ada[1], u_down[0]))
    mix_fulls, _ = _sum_partials("sum_mix", place, owns_mix, rc_mix)
    g_in, g_pw, g_out = _comm_only("join_mix", _join_halves(mix_fulls))
    u_in, u_pw, u_out = _adamw(
        "adamw_mix", [(w_in[0], g_in, m_w_in[0], v_w_in[0]), (w_conv_pw[0], g_pw, m_w_conv_pw[0], v_w_conv_pw[0]),
                      (w_out[0], g_out, m_w_out[0], v_w_out[0])], steps=4)
    o_in, o_pw, o_out = [g_in] + list(u_in), [g_pw] + list(u_pw), [g_out] + list(u_out)

    per_weight = [lead(o_ada), o_bada, o_g1, lead(o_in), lead(o_dww), o_dwb, o_lng, o_lnb, lead(o_pw), lead(o_wg),
                  o_ps, lead(o_out), o_g2, lead(o_gate), lead(o_up), lead(o_down), o_gf]
    result = [loss.reshape(()), gx[None]]
    for kind in range(4):
        result += [o[kind] for o in per_weight]
    return tuple(result)
```

```python
import functools

import jax
import jax.numpy as jnp
from jax import lax
from jax.experimental import pallas as pl
from jax.experimental.pallas import tpu as pltpu

F32 = jnp.float32
MXU_DTYPE = jnp.bfloat16
EPS = 1e-6

D_MODEL = 1024
CONV_W = 512
POOL_W = 512
CONV_K = 31
POOL_WINDOWS = (2, 4, 8, 16)
POOL_G = 128
IN_W = 2 * CONV_W + POOL_W
N_CHIP = 4
N_DEV = 8
CONV_HALO = 32
POOL_HALO = 16

ADAM_LR = 0.001
ADAM_B1 = 0.9
ADAM_B2 = 0.999
ADAM_EPS = 1e-08
ADAM_WD = 0.01
ADAM_STEP = 10

MESH = pl.DeviceIdType.MESH
ANY = pl.BlockSpec(memory_space=pl.ANY)
VMEM = pl.BlockSpec(memory_space=pltpu.VMEM)


def _dot(a, b):
    return jnp.dot(a.astype(MXU_DTYPE), b.astype(MXU_DTYPE), preferred_element_type=F32)


def _dot_nt(a, b):
    return lax.dot_general(a.astype(MXU_DTYPE), b.astype(MXU_DTYPE), (((1,), (1,)), ((), ())),
                           preferred_element_type=F32)


def _dot_tn(a, b):
    return lax.dot_general(a.astype(MXU_DTYPE), b.astype(MXU_DTYPE), (((0,), (0,)), ((), ())),
                           preferred_element_type=F32)


def _sigmoid(v):
    return 1.0 / (1.0 + jnp.exp(-v))


def _full(shape):
    n = len(shape)
    return pl.BlockSpec(shape, lambda *_: (0,) * n)


def _token_tile(s):
    return 256 if s % 256 == 0 else s


SUBLANES = 8


def _row_shifts(pad_ref, shifted_ref, rows):
    for r in range(1, SUBLANES):
        shifted_ref[r - 1] = pad_ref[r:r + rows, :]

    def window(i, n):
        r, base = i % SUBLANES, i - i % SUBLANES
        if r == 0:
            return pad_ref[base:base + n, :]
        return shifted_ref[r - 1, base:base + n, :]

    return window


def _place():
    return lax.axis_index("x"), lax.axis_index("y"), lax.axis_index("c")


def _flip(x, y, r):
    return ((1 - x) if r & 2 else x, (1 - y) if r & 1 else y)


def _remote(src, dst, send_sem, recv_sem, dev):
    return pltpu.make_async_remote_copy(src_ref=src, dst_ref=dst, send_sem=send_sem, recv_sem=recv_sem,
                                        device_id=dev, device_id_type=MESH)


class _Comm:
    def __init__(self, ins, outs, aliases, scratch, start, finish, mid=None):
        self.ins, self.outs, self.aliases, self.scratch = list(ins), list(outs), dict(aliases), list(scratch)
        self.start, self.finish = start, finish
        self.mid = mid


def _both(a, b):
    na, nao, nas = len(a.ins), len(a.outs), len(a.scratch)
    aliases = dict(a.aliases)
    aliases.update({na + i: nao + o for i, o in b.aliases.items()})

    def start(ins, outs, scr):
        a.start(ins[:na], outs[:nao], scr[:nas])
        b.start(ins[na:], outs[nao:], scr[nas:])

    def finish(ins, outs, scr):
        a.finish(ins[:na], outs[:nao], scr[:nas])
        b.finish(ins[na:], outs[nao:], scr[nas:])

    def mid(ins, outs, scr):
        if a.mid:
            a.mid(ins[:na], outs[:nao], scr[:nas])
        if b.mid:
            b.mid(ins[na:], outs[nao:], scr[nas:])

    return _Comm(a.ins + b.ins, a.outs + b.outs, aliases, a.scratch + b.scratch, start, finish,
                 mid if (a.mid or b.mid) else None)


def _call(body, *, name, grid, in_specs, out_specs, out_shape, args, scratch_shapes=(), prefetch=(), comm=None,
          body_starts=False, after=()):
    in_specs = list(in_specs) + [ANY] * len(after)
    args = list(args) + list(after)
    n_pre, n_in, n_out, n_scr = len(prefetch), len(in_specs), len(out_specs), len(scratch_shapes)
    n_body_in = n_in - len(after)
    c_ins = comm.ins if comm else []
    c_outs = comm.outs if comm else []
    c_scr = comm.scratch if comm else []
    last = grid[0] - 1

    def wrapped(*refs):
        pre, refs = refs[:n_pre], refs[n_pre:]
        ins, cin = refs[:n_body_in], refs[n_in:n_in + len(c_ins)]
        refs = refs[n_in + len(c_ins):]
        outs, cout = refs[:n_out], refs[n_out:n_out + len(c_outs)]
        refs = refs[n_out + len(c_outs):]
        scr, cscr = refs[:n_scr], refs[n_scr:]
        step = pl.program_id(0)
        if comm and not body_starts:
            @pl.when(step == 0)
            def _():
                comm.start(cin, cout, cscr)

        has_mid = comm is not None and comm.mid is not None
        mid_step = grid[0] // 2 if grid[0] >= 4 else None
        if has_mid and mid_step is not None:
            @pl.when(step == mid_step)
            def _():
                comm.mid(cin, cout, cscr)

        if body_starts:
            body(lambda: comm.start(cin, cout, cscr), lambda: comm.mid(cin, cout, cscr), cout,
                 *pre, *ins, *outs, *scr)
        else:
            body(*pre, *ins, *outs, *scr)
        if comm:
            @pl.when(step == last)
            def _():
                if has_mid and mid_step is None and not body_starts:
                    comm.mid(cin, cout, cscr)
                comm.finish(cin, cout, cscr)

    aliases = {n_pre + n_in + a: n_out + b for a, b in (comm.aliases if comm else {}).items()}
    res = pl.pallas_call(
        wrapped, name=name,
        grid_spec=pltpu.PrefetchScalarGridSpec(
            num_scalar_prefetch=n_pre, grid=grid, in_specs=list(in_specs) + [ANY] * len(c_ins),
            out_specs=list(out_specs) + [ANY] * len(c_outs), scratch_shapes=list(scratch_shapes) + list(c_scr)),
        out_shape=list(out_shape) + list(c_outs),
        input_output_aliases=aliases,
        compiler_params=pltpu.CompilerParams(dimension_semantics=("arbitrary",)),
    )(*prefetch, *args, *c_ins)
    return res[:n_out], res[n_out:]


def _comm_only(name, comm):
    return _call(lambda: None, name=name, grid=(1,), in_specs=[], out_specs=[], out_shape=[], args=[], comm=comm)[1]


def _weights_gather(bufs, split):
    n = len(bufs)

    def ctx(outs):
        x, y, cc = _place()
        chips = dict(me=2 * x + y, y=2 * x + (1 - y), x=2 * (1 - x) + y, d=2 * (1 - x) + (1 - y))
        devs = dict(y=(x, 1 - y, cc), x=(1 - x, y, cc), d=(1 - x, 1 - y, cc), s=(x, y, 1 - cc))

        def piece(a, kj, pc, q=None):
            if not split[a]:
                return outs[a].at[kj]
            h = bufs[a].shape[1] // 2
            if q is None:
                return outs[a].at[kj, pl.ds(pc * h, h), :]
            return outs[a].at[kj, pl.ds(pc * h + q * (h // 2), h // 2), :]

        return cc, chips, devs, piece

    def directs(a, outs, send, recv):
        cc, chips, devs, piece = ctx(outs)
        if not split[a]:
            whole = piece(a, chips["me"], cc)
            return [_remote(whole, whole, send.at[a, k], recv.at[a, k], devs[t]) for k, t in ((0, "y"), (2, "x"), (4, "d"))]
        q = lambda i: piece(a, chips["me"], cc, i)
        return [_remote(q(0), q(0), send.at[a, 0], recv.at[a, 0], devs["y"]),
                _remote(q(1), q(1), send.at[a, 3], recv.at[a, 3], devs["x"]),
                _remote(q(1), q(1), send.at[a, 1], recv.at[a, 1], devs["y"]),
                _remote(q(0), q(0), send.at[a, 2], recv.at[a, 2], devs["x"])]

    def landed(a, k, outs, send, recv):
        cc, chips, devs, piece = ctx(outs)
        if not split[a]:
            got = piece(a, chips[{0: "y", 2: "x", 4: "d"}[k]], cc)
        elif k < 6:
            got = piece(a, chips[("y", "y", "x", "x", "d", "d")[k]], cc, (0, 1, 0, 1, 0, 1)[k])
        else:
            got = piece(a, chips[("y", "x", "d")[k - 6]], 1 - cc)
        return _remote(got, got, send.at[a, k], recv.at[a, k], devs["s"])

    def passed_on(a, outs, send, recv):
        cc, chips, devs, piece = ctx(outs)
        from_y, from_x = piece(a, chips["y"], cc, 0), piece(a, chips["x"], cc, 1)
        return [_remote(from_y, from_y, send.at[a, 4], recv.at[a, 4], devs["x"]),
                _remote(from_x, from_x, send.at[a, 5], recv.at[a, 5], devs["y"])]

    def to_sibling(a, outs, send, recv, which=(0, 1, 2)):
        cc, chips, devs, piece = ctx(outs)
        halves = [piece(a, chips[("y", "x", "d")[i]], cc) for i in which]
        return [_remote(hf, hf, send.at[a, 6 + i], recv.at[a, 6 + i], devs["s"]) for i, hf in zip(which, halves)]

    def start(ins, outs, scr):
        send, recv = scr
        per_item = [directs(a, outs, send, recv) for a in range(n)]
        for rank in range(4):
            for cps in per_item:
                if rank < len(cps):
                    cps[rank].start()

    def mid(ins, outs, scr):
        send, recv = scr
        for a in range(n):
            if split[a]:
                fy, fx = passed_on(a, outs, send, recv)
                landed(a, 0, outs, send, recv).wait_recv()
                fy.start()
                landed(a, 3, outs, send, recv).wait_recv()
                fx.start()

    def finish(ins, outs, scr):
        send, recv = scr
        for a in range(n):
            if split[a]:
                to_y, to_x = to_sibling(a, outs, send, recv, which=(0, 1))
                landed(a, 1, outs, send, recv).wait_recv()
                to_y.start()
                landed(a, 2, outs, send, recv).wait_recv()
                to_x.start()
        for a in range(n):
            if split[a]:
                for k in (4, 5):
                    landed(a, k, outs, send, recv).wait_recv()
                to_sibling(a, outs, send, recv, which=(2,))[0].start()
            else:
                for k in (0, 2, 4):
                    landed(a, k, outs, send, recv).wait_recv()
        for a in range(n):
            if split[a]:
                for k in (6, 7, 8):
                    landed(a, k, outs, send, recv).wait_recv()
            cps = directs(a, outs, send, recv)
            if split[a]:
                cps += passed_on(a, outs, send, recv) + to_sibling(a, outs, send, recv)
            for cp in cps:
                cp.wait_send()

    return _Comm(bufs, [jax.ShapeDtypeStruct(b.shape, b.dtype) for b in bufs], {i: i for i in range(n)},
                 [pltpu.SemaphoreType.DMA((n, 9)), pltpu.SemaphoreType.DMA((n, 9))], start, finish, mid)


HBM =pl.BlockSpec(memory_space=pltpu.HBM)
SEM = pl.BlockSpec(memory_space=pltpu.SEMAPHORE)
DATAFLOW = pltpu.SideEffectType.DATAFLOW_SIDE_EFFECTING


class _SemGrid:
    def __init__(self, refs, cols):
        self.refs, self.cols = refs, cols

    @property
    def at(self):
        return self

    def __getitem__(self, idx):
        return self.refs[idx[0] * self.cols + idx[1]]


def _split_start(name, srcs, lands, sem_shape, copies, zeroed=False):
    n, k = len(srcs), len(lands)
    ns = sem_shape[0] * sem_shape[1]

    def body(*refs):
        src_refs, land_refs = refs[:n], refs[n:n + k]
        send = _SemGrid(refs[n + k:n + k + ns], sem_shape[1])
        recv = _SemGrid(refs[n + k + ns:n + k + 2 * ns], sem_shape[1])
        token = refs[-1]
        for cp in copies(src_refs, land_refs, send, recv):
            cp.start()
        token[...] = jnp.zeros(token.shape, F32)

    hbm = lambda a: pltpu.with_memory_space_constraint(a, pltpu.HBM)
    zones = [jnp.zeros(l.shape, l.dtype) if zeroed else lax.empty(l.shape, l.dtype) for l in lands]
    out = pl.pallas_call(
        body, name=name,
        out_shape=[pltpu.SemaphoreType.DMA(())] * (2 * ns)
        + [pltpu.HBM(a.shape, a.dtype) for a in list(srcs) + list(lands)] + [jax.ShapeDtypeStruct((8, 128), F32)],
        in_specs=[HBM] * (n + k), out_specs=[SEM] * (2 * ns) + [HBM] * (n + k) + [VMEM],
        input_output_aliases={i: 2 * ns + i for i in range(n + k)},
        compiler_params=pltpu.CompilerParams(has_side_effects=DATAFLOW),
    )(*[hbm(a) for a in srcs], *[hbm(z) for z in zones])
    return out[:-1], out[-1]


def _split_wait(name, state, n, sem_shape, copies, after, in_place=False, with_sources=False):
    ns = sem_shape[0] * sem_shape[1]
    sems, bufs = state[:2 * ns], state[2 * ns:]
    k = len(bufs) - n

    def body(*refs):
        src_refs, land_refs = refs[:n], refs[n:n + k]
        send = _SemGrid(refs[n + k:n + k + ns], sem_shape[1])
        recv = _SemGrid(refs[n + k + ns:n + k + 2 * ns], sem_shape[1])
        cps = copies(src_refs, land_refs, send, recv)
        for cp in cps:
            cp.wait_send()
        for cp in cps:
            cp.wait_recv()

    out = pl.pallas_call(
        body, name=name,
        out_shape=[pltpu.HBM(a.shape, a.dtype) for a in bufs],
        in_specs=[HBM] * (n + k) + [SEM] * (2 * ns) + [ANY] * len(after), out_specs=[HBM] * (n + k),
        input_output_aliases={i: i for i in range(n + k)},
        compiler_params=pltpu.CompilerParams(has_side_effects=DATAFLOW),
    )(*bufs, *sems, *after)
    if with_sources:
        return out[:n], out[n:]
    return out[:n] if in_place else out[n:]


def _direct_phases(copies):
    def start(ins, outs, scr):
        for cp in copies(ins, outs, *scr):
            cp.start()

    def finish(ins, outs, scr):
        cps = copies(ins, outs, *scr)
        for cp in cps:
            cp.wait_recv()
        for cp in cps:
            cp.wait_send()

    return start, finish


def _sibling_parts(gs):
    n = len(gs)

    def copies(ins, outs, send, recv):
        x, y, cc = _place()
        cps = []
        for a in range(n):
            h = gs[a].shape[1] // 2
            cps.append(_remote(ins[a].at[:, pl.ds((1 - cc) * h, h), :], outs[a], send.at[a, 0], recv.at[a, 0],
                               (x, y, 1 - cc)))
        return cps

    return [jax.ShapeDtypeStruct((N_CHIP, g.shape[1] // 2, g.shape[2]), F32) for g in gs], (n, 1), copies


def _sibling_halves(gs):
    lands, sem_shape, copies = _sibling_parts(gs)
    start, finish = _direct_phases(copies)
    return _Comm(gs, lands, {}, [pltpu.SemaphoreType.DMA(sem_shape), pltpu.SemaphoreType.DMA(sem_shape)],
                 start, finish)


def _exchange_parts(pbs):
    n = len(pbs)

    def copies(ins, outs, send, recv):
        x, y, cc = _place()
        cps = []
        for a in range(n):
            for r in range(1, N_CHIP):
                kx, ky = _flip(x, y, r)
                cps.append(_remote(ins[a].at[2 * kx + ky], outs[a].at[r - 1], send.at[a, r - 1], recv.at[a, r - 1],
                                   (kx, ky, cc)))
        return cps

    lands = [jax.ShapeDtypeStruct((N_CHIP - 1,) + p.shape[1:], p.dtype) for p in pbs]
    return lands, (n, N_CHIP - 1), copies


def _small_parts(arrs):
    n = len(arrs)

    def copies(ins, outs, send, recv):
        x, y, cc = _place()
        b = 4 * x + 2 * y + cc
        cps = []
        for a in range(n):
            for r in range(1, N_DEV):
                dev = ((1 - x) if r & 4 else x, (1 - y) if r & 2 else y, (1 - cc) if r & 1 else cc)
                cps.append(_remote(ins[a], outs[a].at[b], send.at[a, r - 1], recv.at[a, r - 1], dev))
        return cps

    lands = [jax.ShapeDtypeStruct((N_DEV,) + a.shape, a.dtype) for a in arrs]
    return lands, (n, N_DEV - 1), copies


def _exchange_partials(pbs):
    lands, sem_shape, copies = _exchange_parts(pbs)
    start, finish = _direct_phases(copies)
    return _Comm(pbs, lands, {}, [pltpu.SemaphoreType.DMA(sem_shape), pltpu.SemaphoreType.DMA(sem_shape)],
                 start, finish)


def _join_parts(fulls):
    n = len(fulls)

    def copies(bufs, send, recv):
        x, y, cc = _place()
        cps = []
        for a in range(n):
            h = fulls[a].shape[0] // 2
            mine = bufs[a].at[pl.ds(cc * h, h), :]
            cps.append(_remote(mine, mine, send.at[a, 0], recv.at[a, 0], (x, y, 1 - cc)))
        return cps

    return (n, 1), copies


def _join_halves(fulls):
    sem_shape, copies = _join_parts(fulls)
    start, finish = _direct_phases(lambda ins, outs, send, recv: copies(outs, send, recv))
    return _Comm(fulls, [jax.ShapeDtypeStruct(f.shape, F32) for f in fulls], {i: i for i in range(len(fulls))},
                 [pltpu.SemaphoreType.DMA(sem_shape), pltpu.SemaphoreType.DMA(sem_shape)], start, finish)


def _mixer_fwd(x, mod, g1, w_in, dww, dwb, lng, lnb, w_pw, wg, pscale, w_out, comm=None):
    s = x.shape[0]
    ts = _token_tile(s)
    nt = s // ts

    def body(x_ref, mod_ref, g1_ref, win_ref, dww_ref, dwb_ref, lng_ref, lnb_ref, wpw_ref, wg_ref, ps_ref,
             wout_ref, x2_ref, y_ref, u_ref, z_ref, rstd_ref, p_ref, ycat_ref, gpad, vpad, gshift):
        i = pl.program_id(0)

        @pl.when(i == 0)
        def _():
            gpad[0:CONV_HALO, :] = jnp.zeros((CONV_HALO, CONV_W), F32)
            vpad[0:POOL_HALO, :] = jnp.zeros((POOL_HALO, POOL_W), F32)

        xt = x_ref[...]
        sh1 = mod_ref[0:1, :]
        sc1 = mod_ref[1:2, :]
        gt1 = mod_ref[2:3, :]
        r1 = lax.rsqrt(jnp.mean(xt * xt, axis=-1, keepdims=True) + EPS)
        h1 = (xt * r1 * g1_ref[...]) * (1.0 + sc1) + sh1
        h1b = h1.astype(MXU_DTYPE)
        u = jnp.concatenate([_dot(h1b, win_ref[j]) for j in range(N_CHIP)], axis=1)
        u_ref[...] = u
        a = u[:, :CONV_W]
        g = u[:, CONV_W:2 * CONV_W]
        v = u[:, 2 * CONV_W:]

        gpad[CONV_HALO:CONV_HALO + ts, :] = a * _sigmoid(g)
        window = _row_shifts(gpad, gshift, ts + CONV_HALO - SUBLANES)
        cv = jnp.broadcast_to(dwb_ref[...], (ts, CONV_W))
        off = CONV_HALO - (CONV_K - 1)
        for k in range(CONV_K):
            cv = cv + dww_ref[k:k + 1, :] * window(off + k, ts)
        gpad[0:CONV_HALO, :] = gpad[ts:ts + CONV_HALO, :]

        mu = jnp.mean(cv, axis=-1, keepdims=True)
        cc = cv - mu
        rstd = lax.rsqrt(jnp.mean(cc * cc, axis=-1, keepdims=True) + EPS)
        z = cc * rstd
        z_ref[...] = z
        rstd_ref[...] = rstd
        ln = z * lng_ref[...] + lnb_ref[...]
        sw = ln * _sigmoid(ln)
        yconv = _dot(sw, wpw_ref[...])

        vpad[POOL_HALO:POOL_HALO + ts, :] = v
        t = i * ts + lax.broadcasted_iota(jnp.int32, (ts, 1), 0)
        ps, ypool = [], []
        for gi, w in enumerate(POOL_WINDOWS):
            cols = slice(gi * POOL_G, (gi + 1) * POOL_G)
            acc = vpad[POOL_HALO:POOL_HALO + ts, cols]
            for d in range(1, w):
                acc = acc + vpad[POOL_HALO - d:POOL_HALO - d + ts, cols]
            cnt = jnp.minimum(t + 1, w).astype(F32)
            pg = (acc / cnt - v[:, cols]).astype(MXU_DTYPE)
            ps.append(pg)
            ypool.append(_dot(pg, wg_ref[gi]))
        vpad[0:POOL_HALO, :] = vpad[ts:ts + POOL_HALO, :]
        p_ref[...] = jnp.concatenate(ps, axis=1)
        ypool = jnp.concatenate(ypool, axis=1) * ps_ref[...]

        ycat = jnp.concatenate([yconv, ypool], axis=1).astype(MXU_DTYPE)
        ycat_ref[...] = ycat
        y = _dot(ycat, wout_ref[...])
        y_ref[...] = y
        x2_ref[...] = xt + gt1 * y

    tile = lambda w: pl.BlockSpec((ts, w), lambda i: (i, 0))
    return _call(
        body, name="mixer_fwd", grid=(nt,),
        in_specs=[tile(D_MODEL), _full(mod.shape), _full(g1.shape), _full(w_in.shape), _full(dww.shape),
                  _full(dwb.shape), _full(lng.shape), _full(lnb.shape), _full(w_pw.shape), _full(wg.shape),
                  _full(pscale.shape), _full(w_out.shape)],
        out_specs=[tile(D_MODEL), tile(D_MODEL), tile(IN_W), tile(CONV_W), tile(1), tile(POOL_W), tile(D_MODEL)],
        out_shape=[jax.ShapeDtypeStruct((s, D_MODEL), F32), jax.ShapeDtypeStruct((s, D_MODEL), F32),
                   jax.ShapeDtypeStruct((s, IN_W), F32), jax.ShapeDtypeStruct((s, CONV_W), F32),
                   jax.ShapeDtypeStruct((s, 1), F32), jax.ShapeDtypeStruct((s, POOL_W), MXU_DTYPE),
                   jax.ShapeDtypeStruct((s, D_MODEL), MXU_DTYPE)],
        scratch_shapes=[pltpu.VMEM((ts + CONV_HALO, CONV_W), F32), pltpu.VMEM((ts + POOL_HALO, POOL_W), F32),
                        pltpu.VMEM((SUBLANES - 1, ts + CONV_HALO - SUBLANES, CONV_W), F32)],
        args=(x, mod, g1, w_in, dww, dwb, lng, lnb, w_pw, wg, pscale, w_out), comm=comm)


def _ffn(x2, tgt, mod, g2, gf, w_gate, w_up, w_down):
    s = x2.shape[0]
    ts = _token_tile(s)
    nt = s // ts
    fb = w_gate.shape[1]

    def body(x2_ref, tgt_ref, mod_ref, g2_ref, gf_ref, wgt_ref, wup_ref, wdn_ref,
             dx2_ref, h2_ref, df_ref, act_ref, dgg_ref, duu_ref, vec_ref, gg_s, uu_s):
        i = pl.program_id(0)

        @pl.when(i == 0)
        def _():
            vec_ref[...] = jnp.zeros(vec_ref.shape, F32)

        x2t = x2_ref[...]
        sh2 = mod_ref[3:4, :]
        sc2 = mod_ref[4:5, :]
        gt2 = mod_ref[5:6, :]
        g2v = g2_ref[...]
        gfv = gf_ref[...]
        r2 = lax.rsqrt(jnp.mean(x2t * x2t, axis=-1, keepdims=True) + EPS)
        xh2 = x2t * r2
        n2 = xh2 * g2v
        h2b = (n2 * (1.0 + sc2) + sh2).astype(MXU_DTYPE)
        h2_ref[...] = h2b
        f = jnp.zeros((ts, D_MODEL), F32)
        for j in range(N_CHIP):
            gg = _dot_nt(h2b, wgt_ref[j])
            uu = _dot_nt(h2b, wup_ref[j])
            gg_s[j] = gg
            uu_s[j] = uu
            actb = (gg * _sigmoid(gg) * uu).astype(MXU_DTYPE)
            act_ref[j] = actb
            f = f + _dot(actb, wdn_ref[j])
        x3 = x2t + gt2 * f
        r3 = lax.rsqrt(jnp.mean(x3 * x3, axis=-1, keepdims=True) + EPS)
        xh3 = x3 * r3
        diff = xh3 * gfv - tgt_ref[...]
        dout = diff * (1.0 / D_MODEL)
        dn3 = dout * gfv
        dx3 = r3 * (dn3 - xh3 * jnp.mean(dn3 * xh3, axis=-1, keepdims=True))
        dfb = (dx3 * gt2).astype(MXU_DTYPE)
        df_ref[...] = dfb
        dh2 = jnp.zeros((ts, D_MODEL), F32)
        for j in range(N_CHIP):
            dact = _dot_nt(dfb, wdn_ref[j])
            gg = gg_s[j]
            uu = uu_s[j]
            sg = _sigmoid(gg)
            duu = (dact * (gg * sg)).astype(MXU_DTYPE)
            dgg = (dact * uu * (sg * (1.0 + gg * (1.0 - sg)))).astype(MXU_DTYPE)
            duu_ref[j] = duu
            dgg_ref[j] = dgg
            dh2 = dh2 + _dot(dgg, wgt_ref[j]) + _dot(duu, wup_ref[j])
        dn2 = dh2 * (1.0 + sc2)
        dxh2 = dn2 * g2v
        dx2_ref[...] = dx3 + r2 * (dxh2 - xh2 * jnp.mean(dxh2 * xh2, axis=-1, keepdims=True))

        col = lambda a: jnp.sum(a, axis=0, keepdims=True)
        vec_ref[0:1, :] += col(dout * xh3)
        vec_ref[1:2, :] += col(dx3 * f)
        vec_ref[2:3, :] += col(dh2)
        vec_ref[3:4, :] += col(dh2 * n2)
        vec_ref[4:5, :] += col(dn2 * xh2)
        vec_ref[5:6, :] += col(diff * diff)

    tile = lambda w: pl.BlockSpec((ts, w), lambda i: (i, 0))
    tile3 = pl.BlockSpec((N_CHIP, ts, fb), lambda i: (0, i, 0))
    once = lambda a: pl.BlockSpec(a.shape, lambda i: (0,) * a.ndim, pipeline_mode=pl.Buffered(1))
    hid = jax.ShapeDtypeStruct((N_CHIP, s, fb), MXU_DTYPE)
    return pl.pallas_call(
        body, name="ffn", grid=(nt,),
        in_specs=[tile(D_MODEL), tile(D_MODEL), _full(mod.shape), _full(g2.shape), _full(gf.shape),
                  once(w_gate), once(w_up), once(w_down)],
        out_specs=[tile(D_MODEL), tile(D_MODEL), tile(D_MODEL), tile3, tile3, tile3, _full((8, D_MODEL))],
        out_shape=[jax.ShapeDtypeStruct((s, D_MODEL), F32), jax.ShapeDtypeStruct((s, D_MODEL), MXU_DTYPE),
                   jax.ShapeDtypeStruct((s, D_MODEL), MXU_DTYPE), hid, hid, hid,
                   jax.ShapeDtypeStruct((8, D_MODEL), F32)],
        scratch_shapes=[pltpu.VMEM((N_CHIP, ts, fb), F32), pltpu.VMEM((N_CHIP, ts, fb), F32)],
        compiler_params=pltpu.CompilerParams(dimension_semantics=("arbitrary",)),
    )(x2, tgt, mod, g2, gf, w_gate, w_up, w_down)


def _mixer_bwd(dx2, x, y, u, z, rstd, p, mod, g1, w_in, dww, lng, lnb, w_pw, wg, pscale, w_out, comm=None, after=()):
    s = x.shape[0]
    ts = _token_tile(s)
    nt = s // ts

    def body(dx2_ref, x_ref, y_ref, u_ref, z_ref, rstd_ref, p_ref, mod_ref, g1_ref, win_ref, dww_ref, lng_ref,
             lnb_ref, wpw_ref, wg_ref, ps_ref, wout_ref,
             gx_ref, h1_ref, du_ref, dy_ref, sw_ref, dyc_ref, dyp_ref, vd_ref, vc_ref, ddw_ref, dcpad, dppad,
             dshift):
        i = pl.program_id(0)
        tix = nt - 1 - i

        @pl.when(i == 0)
        def _():
            vd_ref[...] = jnp.zeros(vd_ref.shape, F32)
            vc_ref[...] = jnp.zeros(vc_ref.shape, F32)
            ddw_ref[...] = jnp.zeros(ddw_ref.shape, F32)
            dcpad[ts:ts + CONV_HALO, :] = jnp.zeros((CONV_HALO, CONV_W), F32)
            dppad[ts:ts + POOL_HALO, :] = jnp.zeros((POOL_HALO, POOL_W), F32)

        col = lambda a: jnp.sum(a, axis=0, keepdims=True)
        sh1 = mod_ref[0:1, :]
        sc1 = mod_ref[1:2, :]
        gt1 = mod_ref[2:3, :]
        dx2t = dx2_ref[...]
        vd_ref[0:1, :] += col(dx2t * y_ref[...])
        dyb = (dx2t * gt1).astype(MXU_DTYPE)
        dy_ref[...] = dyb
        dycat = _dot_nt(dyb, wout_ref[...])
        dyconv = dycat[:, :CONV_W]
        dypool = dycat[:, CONV_W:]

        pt = p_ref[...]
        t = tix * ts + lax.broadcasted_iota(jnp.int32, (ts, 1), 0)
        psc = ps_ref[...]
        dypb = (dypool * psc).astype(MXU_DTYPE)
        dyp_ref[...] = dypb
        dps, ypre = [], []
        for gi, w in enumerate(POOL_WINDOWS):
            cols = slice(gi * POOL_G, (gi + 1) * POOL_G)
            ypre.append(_dot(pt[:, cols], wg_ref[gi]))
            dpg = _dot_nt(dypb[:, cols], wg_ref[gi])
            dps.append(dpg)
            cnt = jnp.minimum(t + 1, w).astype(F32)
            dppad[0:ts, cols] = dpg / cnt
        vc_ref[0:1, :] += col(dypool * jnp.concatenate(ypre, axis=1))
        dvs = []
        for gi, w in enumerate(POOL_WINDOWS):
            cols = slice(gi * POOL_G, (gi + 1) * POOL_G)
            acc = dppad[0:ts, cols]
            for d in range(1, w):
                acc = acc + dppad[d:d + ts, cols]
            dvs.append(acc - dps[gi])
        dv = jnp.concatenate(dvs, axis=1)
        dppad[ts:ts + POOL_HALO, :] = dppad[0:POOL_HALO, :]

        zt = z_ref[...]
        lngv = lng_ref[...]
        ln = zt * lngv + lnb_ref[...]
        sg = _sigmoid(ln)
        swb = (ln * sg).astype(MXU_DTYPE)
        sw_ref[...] = swb
        dycb = dyconv.astype(MXU_DTYPE)
        dyc_ref[...] = dycb
        dln = _dot_nt(dycb, wpw_ref[...]) * (sg * (1.0 + ln * (1.0 - sg)))
        vc_ref[1:2, :] += col(dln * zt)
        vc_ref[2:3, :] += col(dln)
        dz = dln * lngv
        dcv = rstd_ref[...] * (dz - jnp.mean(dz, axis=-1, keepdims=True)
                               - zt * jnp.mean(dz * zt, axis=-1, keepdims=True))
        vc_ref[3:4, :] += col(dcv)
        dcpad[0:ts, :] = dcv
        ut = u_ref[...]
        a = ut[:, :CONV_W]
        g = ut[:, CONV_W:2 * CONV_W]
        sgg = _sigmoid(g)
        glu = a * sgg
        window = _row_shifts(dcpad, dshift, ts + CONV_HALO - SUBLANES)
        dglu = jnp.zeros((ts, CONV_W), F32)
        for k in range(CONV_K):
            sh = window(CONV_K - 1 - k, ts)
            dglu = dglu + dww_ref[k:k + 1, :] * sh
            ddw_ref[k:k + 1, :] += col(glu * sh)
        dcpad[ts:ts + CONV_HALO, :] = dcpad[0:CONV_HALO, :]
        da = dglu * sgg
        dg = dglu * a * sgg * (1.0 - sgg)
        dub = jnp.concatenate([da, dg, dv], axis=1).astype(MXU_DTYPE)
        du_ref[...] = dub
        cw = IN_W // N_CHIP
        dh1 = jnp.zeros((ts, D_MODEL), F32)
        for j in range(N_CHIP):
            dh1 = dh1 + _dot_nt(dub[:, j * cw:(j + 1) * cw], win_ref[j])

        xt = x_ref[...]
        g1v = g1_ref[...]
        r1 = lax.rsqrt(jnp.mean(xt * xt, axis=-1, keepdims=True) + EPS)
        xh1 = xt * r1
        n1 = xh1 * g1v
        h1_ref[...] = (n1 * (1.0 + sc1) + sh1).astype(MXU_DTYPE)
        vd_ref[1:2, :] += col(dh1)
        vd_ref[2:3, :] += col(dh1 * n1)
        dn1 = dh1 * (1.0 + sc1)
        vd_ref[3:4, :] += col(dn1 * xh1)
        dxh = dn1 * g1v
        gx_ref[...] = dx2t + r1 * (dxh - xh1 * jnp.mean(dxh * xh1, axis=-1, keepdims=True))

    tile = lambda w: pl.BlockSpec((ts, w), lambda i: (nt - 1 - i, 0))
    bf = lambda w: jax.ShapeDtypeStruct((s, w), MXU_DTYPE)
    return _call(
        body, name="mixer_bwd", grid=(nt,),
        in_specs=[tile(D_MODEL), tile(D_MODEL), tile(D_MODEL), tile(IN_W), tile(CONV_W), tile(1), tile(POOL_W),
                  _full(mod.shape), _full(g1.shape), _full(w_in.shape), _full(dww.shape), _full(lng.shape),
                  _full(lnb.shape), _full(w_pw.shape), _full(wg.shape), _full(pscale.shape), _full(w_out.shape)],
        out_specs=[tile(D_MODEL), tile(D_MODEL), tile(IN_W), tile(D_MODEL), tile(CONV_W), tile(CONV_W),
                   tile(POOL_W), _full((8, D_MODEL)), _full((8, CONV_W)), _full((32, CONV_W))],
        out_shape=[jax.ShapeDtypeStruct((s, D_MODEL), F32), bf(D_MODEL), bf(IN_W), bf(D_MODEL), bf(CONV_W),
                   bf(CONV_W), bf(POOL_W), jax.ShapeDtypeStruct((8, D_MODEL), F32),
                   jax.ShapeDtypeStruct((8, CONV_W), F32), jax.ShapeDtypeStruct((32, CONV_W), F32)],
        scratch_shapes=[pltpu.VMEM((ts + CONV_HALO, CONV_W), F32), pltpu.VMEM((ts + POOL_HALO, POOL_W), F32),
                        pltpu.VMEM((SUBLANES - 1, ts + CONV_HALO - SUBLANES, CONV_W), F32)],
        args=(dx2, x, y, u, z, rstd, p, mod, g1, w_in, dww, lng, lnb, w_pw, wg, pscale, w_out), comm=comm,
        after=after)


def _dw(name, a, a_spec, b, b_spec, nb, mb, nbk, comm=None, after=()):
    def body(a_ref, b_ref, o_ref):
        av = a_ref[...]
        bv = b_ref[...]
        av = av.reshape(av.shape[-2:])
        bv = bv.reshape(bv.shape[-2:])
        o_ref[0] = _dot_tn(av, bv)

    (out,), rest = _call(
        body, name=name, grid=(nb,), in_specs=[a_spec, b_spec],
        out_specs=[pl.BlockSpec((1, mb, nbk), lambda j: (j, 0, 0))],
        out_shape=[jax.ShapeDtypeStruct((nb, mb, nbk), F32)], args=(a, b), comm=comm, after=after)
    return out, rest


def _dw_mixer(ycat, dy, sw, dyc, h1, du, after=()):
    s = ycat.shape[0]

    def body(ycat_ref, dy_ref, sw_ref, dyc_ref, h1_ref, du_ref, out_ref, pw_ref, in_ref):
        out_ref[0] = _dot_tn(ycat_ref[...], dy_ref[...])
        pw_ref[0] = _dot_tn(sw_ref[...], dyc_ref[...])
        in_ref[0] = _dot_tn(h1_ref[...], du_ref[...])

    whole = lambda w: pl.BlockSpec((s, w), lambda j: (0, 0))
    cols = lambda w: pl.BlockSpec((s, w), lambda j: (0, j))
    blk = lambda m, n: pl.BlockSpec((1, m, n), lambda j: (j, 0, 0))
    shapes = [(D_MODEL // N_CHIP, D_MODEL), (CONV_W // N_CHIP, CONV_W), (D_MODEL, IN_W // N_CHIP)]
    res, _ = _call(
        body, name="dw_mixer", grid=(N_CHIP,),
        in_specs=[cols(D_MODEL // N_CHIP), whole(D_MODEL), cols(CONV_W // N_CHIP), whole(CONV_W), whole(D_MODEL),
                  cols(IN_W // N_CHIP)],
        out_specs=[blk(m, n) for m, n in shapes],
        out_shape=[jax.ShapeDtypeStruct((N_CHIP, m, n), F32) for m, n in shapes],
        args=(ycat, dy, sw, dyc, h1, du), after=after)
    return res


def _ada_fwd(c, w_ada, b4, first, later, dww, wg, comm):
    nc = w_ada.shape[1]
    nf, nl = len(first), len(later)
    shards = list(first) + list(later)

    def body(start_comm, mid_comm, gathered, c_ref, w_ref, b4_ref, *refs):
        shard_refs, refs = refs[:nf + nl], refs[nf + nl:]
        dww_ref, wg_ref, mod_ref, cact_ref, wgb_ref = refs[:5]
        later_refs, refs = refs[5:5 + nl], refs[5 + nl:]
        call, part, parts = refs[:3]
        stages, refs = refs[3:3 + nf + nl], refs[3 + nf + nl:]
        send1, recv1, send2, recv2, lsem = refs
        x, y, cc = _place()
        b = 4 * x + 2 * y + cc
        j = 2 * x + y

        def slot_copies(lo, hi):
            cps = []
            for a in range(lo, hi):
                dst = gathered[a] if a < nf else later_refs[a - nf]
                cps.append(pltpu.make_async_copy(stages[a], dst.at[j], lsem.at[a]))
            return cps

        call[b] = c_ref[...]
        sends = []
        for r in range(1, N_DEV):
            dev = ((1 - x) if r & 4 else x, (1 - y) if r & 2 else y, (1 - cc) if r & 1 else cc)
            cp = _remote(call.at[b], call.at[b], send1.at[r - 1], recv1.at[r - 1], dev)
            cp.start()
            sends.append(cp)
        for a in range(nf):
            stages[a][...] = shard_refs[a][...].astype(MXU_DTYPE)
        dww_copy = pltpu.make_async_copy(dww_ref, gathered[nf].at[j], lsem.at[nf + nl])
        dww_copy.start()
        for cp in slot_copies(0, nf):
            cp.start()
        for r in range(1, N_DEV):
            src_b = lax.bitwise_xor(b, r)
            _remote(call.at[src_b], call.at[src_b], send1.at[r - 1], recv1.at[r - 1], (x, y, cc)).wait_recv()
        for cp in sends:
            cp.wait_send()
        for cp in slot_copies(0, nf):
            cp.wait()
        dww_copy.wait()
        start_comm()
        for i in range(N_DEV):
            ci = call[i]
            cact_ref[i:i + 1, :] = ci * _sigmoid(ci)
        part[...] = jnp.dot(cact_ref[...], w_ref[...], preferred_element_type=F32, precision=lax.Precision.HIGHEST)
        sends = []
        for r in range(1, N_CHIP):
            kx, ky = _flip(x, y, r)
            cp = _remote(part, parts.at[j], send2.at[r - 1], recv2.at[r - 1], (kx, ky, cc))
            cp.start()
            sends.append(cp)
        parts[j] = part[...]
        for a in range(nf, nf + nl):
            stages[a][...] = shard_refs[a][...].astype(MXU_DTYPE)
        for cp in slot_copies(nf, nf + nl):
            cp.start()
        wgb_ref[...] = wg_ref[...].astype(MXU_DTYPE)
        mid_comm()
        for r in range(1, N_CHIP):
            kx, ky = _flip(x, y, r)
            kj = 2 * kx + ky
            _remote(part, parts.at[kj], send2.at[r - 1], recv2.at[r - 1], (x, y, cc)).wait_recv()
        for cp in sends:
            cp.wait_send()
        mine = lax.broadcasted_iota(jnp.int32, (N_DEV, 1), 0) == b
        for k in range(N_CHIP):
            row = jnp.sum(jnp.where(mine, parts[k], 0.0), axis=0, keepdims=True)
            mod_ref[k:k + 1, :] = row + b4_ref[k:k + 1, :]
        for cp in slot_copies(nf, nf + nl):
            cp.wait()

    res, rest = _call(
        body, name="ada_fwd", grid=(1,),
        in_specs=[VMEM] * (5 + nf + nl), out_specs=[VMEM, VMEM, VMEM] + [ANY] * nl,
        out_shape=[jax.ShapeDtypeStruct((N_CHIP, nc), F32), jax.ShapeDtypeStruct((N_DEV, D_MODEL), F32),
                   jax.ShapeDtypeStruct(wg.shape, MXU_DTYPE)]
        + [jax.ShapeDtypeStruct((N_CHIP,) + a.shape, MXU_DTYPE) for a in later],
        scratch_shapes=[pltpu.VMEM((N_DEV, 1, D_MODEL), F32), pltpu.VMEM((N_DEV, nc), F32),
                        pltpu.VMEM((N_CHIP, N_DEV, nc), F32)]
        + [pltpu.VMEM(a.shape, MXU_DTYPE) for a in shards]
        + [pltpu.SemaphoreType.DMA((N_DEV - 1,)), pltpu.SemaphoreType.DMA((N_DEV - 1,)),
           pltpu.SemaphoreType.DMA((N_CHIP - 1,)), pltpu.SemaphoreType.DMA((N_CHIP - 1,)),
           pltpu.SemaphoreType.DMA((nf + nl + 1,))],
        args=(c, w_ada, b4, *shards, dww, wg), comm=comm, body_starts=True)
    return (res[0], res[1], res[2], res[3:]), rest


def _chip_partials(name, place, gs, rs, comm=None, after=()):
    n = len(gs)

    def body(pref, *refs):
        g_refs, r_refs = refs[:n], refs[n:2 * n]
        pb_refs, own_refs = refs[2 * n:3 * n], refs[3 * n:]
        jj = pl.program_id(0)
        for a in range(n):
            sm = g_refs[a][0] + r_refs[a][0]
            pb_refs[a][0] = sm.astype(MXU_DTYPE)

            @pl.when(jj == pref[1])
            def _(a=a, sm=sm):
                own_refs[a][...] = sm

    halves = [(g.shape[1] // 2, g.shape[2]) for g in gs]
    in_specs = [pl.BlockSpec((1, h, w), lambda jj, pref: (jj, pref[0], 0)) for h, w in halves]
    in_specs += [pl.BlockSpec((1, h, w), lambda jj, pref: (jj, 0, 0)) for h, w in halves]
    out_specs = [pl.BlockSpec((1, h, w), lambda jj, pref: (jj, 0, 0)) for h, w in halves]
    out_specs += [pl.BlockSpec((h, w), lambda jj, pref: (0, 0)) for h, w in halves]
    out, rest = _call(
        body, name=name, grid=(N_CHIP,), in_specs=in_specs, out_specs=out_specs,
        out_shape=[jax.ShapeDtypeStruct((N_CHIP, h, w), MXU_DTYPE) for h, w in halves]
        + [jax.ShapeDtypeStruct((h, w), F32) for h, w in halves],
        args=(*gs, *rs), prefetch=(place,), comm=comm, after=after)
    return (out[:n], out[n:]), rest


def _sum_partials(name, place, owns, recvd, comm=None, after=()):
    n = len(owns)

    def body(pref, *refs):
        o_refs, r_refs, out_refs = refs[:n], refs[n:2 * n], refs[2 * n:]
        for a in range(n):
            acc = o_refs[a][...]
            for r in range(N_CHIP - 1):
                acc = acc + r_refs[a][r].astype(F32)
            out_refs[a][...] = acc

    full = lambda a: pl.BlockSpec(a.shape, lambda i, pref: (0,) * a.ndim)
    return _call(
        body, name=name, grid=(1,), in_specs=[full(a) for a in list(owns) + list(recvd)],
        out_specs=[pl.BlockSpec(o.shape, lambda i, pref: (pref[0], 0)) for o in owns],
        out_shape=[jax.ShapeDtypeStruct((2 * o.shape[0], o.shape[1]), F32) for o in owns],
        args=(*owns, *recvd), prefetch=(place,), comm=comm, after=after)


def _adamw_math(w, g, m, v):
    m = ADAM_B1 * m + (1.0 - ADAM_B1) * g
    v = ADAM_B2 * v + (1.0 - ADAM_B2) * (g * g)
    m_hat = m / (1.0 - ADAM_B1 ** ADAM_STEP)
    v_hat = v / (1.0 - ADAM_B2 ** ADAM_STEP)
    delta = -ADAM_LR * (m_hat / (jnp.sqrt(v_hat) + ADAM_EPS) + ADAM_WD * w)
    return delta, m, v


def _row_tile(rows):
    for t in (512, 352, 256, 128):
        if rows % t == 0:
            return t
    return rows


def _adamw(name, wgmv, steps, after=()):
    n = len(wgmv)

    def body(*refs):
        ins, outs = refs[:4 * n], refs[4 * n:]
        for i in range(n):
            w_ref, g_ref, m_ref, v_ref = ins[4 * i:4 * i + 4]
            d_ref, nm_ref, nv_ref = outs[3 * i:3 * i + 3]
            d_ref[...], nm_ref[...], nv_ref[...] = _adamw_math(w_ref[...], g_ref[...], m_ref[...], v_ref[...])

    in_specs, out_specs, out_shape, args = [], [], [], []
    for w, g, m, v in wgmv:
        rows, cols = w.shape
        spec = pl.BlockSpec((rows // steps, cols), lambda i: (i, 0))
        in_specs += [spec] * 4
        out_specs += [spec] * 3
        out_shape += [jax.ShapeDtypeStruct(w.shape, F32)] * 3
        args += [w, g, m, v]
    res, _ = _call(body, name=name, grid=(steps,), in_specs=in_specs, out_specs=out_specs, out_shape=out_shape,
                   args=args, after=after)
    return [res[3 * i:3 * i + 3] for i in range(n)]


def _adamw_ada(place, cact, dmod, w, m, v, after=()):
    rows, cols = w.shape
    tr = _row_tile(rows)

    def body(pref, ca_ref, dm_ref, w_ref, m_ref, v_ref, g_ref, d_ref, nm_ref, nv_ref):
        g = lax.dot_general(ca_ref[...], dm_ref[...], (((0,), (0,)), ((), ())), preferred_element_type=F32,
                            precision=lax.Precision.HIGHEST)
        g_ref[...] = g
        d_ref[...], nm_ref[...], nv_ref[...] = _adamw_math(w_ref[...], g, m_ref[...], v_ref[...])

    spec = pl.BlockSpec((tr, cols), lambda i, pref: (i, 0))
    return _call(
        body, name="adamw_ada", grid=(rows // tr,),
        in_specs=[pl.BlockSpec((N_DEV, tr), lambda i, pref: (0, i)),
                  pl.BlockSpec((N_DEV, cols), lambda i, pref: (0, pref[1])), spec, spec, spec],
        out_specs=[spec] * 4, out_shape=[jax.ShapeDtypeStruct(w.shape, F32)] * 4,
        args=(cact, dmod, w, m, v), prefetch=(place,), after=after)[0]


def _adamw_small(place, owns, gathered, wmv):
    nw = len(wmv)
    flat = [a for t in wmv for a in t]

    def body(pref, *refs):
        own_refs, all_refs, refs = refs[:5], refs[5:10], refs[10:]
        w_refs = refs[:3 * nw]
        loss_ref, dmod_ref = refs[3 * nw], refs[3 * nw + 1]
        o_refs = refs[3 * nw + 2:]
        j = pref[1]
        me = 2 * pref[1] + pref[0]

        def total(i):
            acc = None
            for b in range(N_DEV):
                blk = jnp.where(me == b, own_refs[i][...], all_refs[i][b])
                acc = blk if acc is None else acc + blk
            return acc

        vf, vd, vc, ddw, gwg = [total(i) for i in range(5)]
        loss_ref[...] = (0.5 / D_MODEL) * jnp.sum(vf[5:6, :], axis=1, keepdims=True)
        order = ((1, 1), (1, 2), (1, 0), (0, 2), (0, 3), (0, 1))
        for b in range(N_DEV):
            for q, (i, row) in enumerate(order):
                dmod_ref[b:b + 1, q * D_MODEL:(q + 1) * D_MODEL] = jnp.where(
                    me == b, own_refs[i][row:row + 1, :], all_refs[i][b, row:row + 1, :])
        dm = dmod_ref[...]
        g_bada = dm[0:1, :]
        for b in range(1, N_DEV):
            g_bada = g_bada + dm[b:b + 1, :]
        g_dww = jnp.zeros((32, POOL_G), F32)
        for k in range(N_CHIP):
            g_dww = g_dww + jnp.where(j == k, ddw[:, k * POOL_G:(k + 1) * POOL_G], 0.0)
        grads = [g_bada, vd[3:4, :], g_dww, vc[3:4, :], vc[1:2, :], vc[2:3, :], gwg, vc[0:1, :], vf[4:5, :],
                 vf[0:1, :]]
        for i, g in enumerate(grads):
            w_ref, m_ref, v_ref = w_refs[3 * i:3 * i + 3]
            d, nm, nv = _adamw_math(w_ref[...], g, m_ref[...], v_ref[...])
            o_refs[4 * i][...] = g
            o_refs[4 * i + 1][...] = d
            o_refs[4 * i + 2][...] = nm
            o_refs[4 * i + 3][...] = nv

    outs = [jax.ShapeDtypeStruct((1, 1), F32), jax.ShapeDtypeStruct((N_DEV, 6 * D_MODEL), F32)]
    for w, _, _ in wmv:
        outs += [jax.ShapeDtypeStruct(w.shape, F32)] * 4
    full = lambda a: pl.BlockSpec(a.shape, lambda i, pref: (0,) * a.ndim)
    args = list(owns) + list(gathered) + flat
    res, _ = _call(body, name="adamw_small", grid=(1,), in_specs=[full(a) for a in args],
                   out_specs=[full(o) for o in outs], out_shape=outs, args=args, prefetch=(place,))
    return res[0], res[1], [res[2 + 4 * i:6 + 4 * i] for i in range(nw)]


def kernel(x, c, w_ada, b_ada, g_norm1, w_in, dw_w, dw_b, conv_ln_g, conv_ln_b, w_conv_pw, w_pool_group, pool_scale, w_out, g_norm2, w_ffn_gate, w_ffn_up, w_ffn_down, g_final, loss_target, m_w_ada, m_b_ada, m_g_norm1, m_w_in, m_dw_w, m_dw_b, m_conv_ln_g, m_conv_ln_b, m_w_conv_pw, m_w_pool_group, m_pool_scale, m_w_out, m_g_norm2, m_w_ffn_gate, m_w_ffn_up, m_w_ffn_down, m_g_final, v_w_ada, v_b_ada, v_g_norm1, v_w_in, v_dw_w, v_dw_b, v_conv_ln_g, v_conv_ln_b, v_w_conv_pw, v_w_pool_group, v_pool_scale, v_w_out, v_g_norm2, v_w_ffn_gate, v_w_ffn_up, v_w_ffn_down, v_g_final):
    xi, yi, ci = _place()
    place = jnp.stack([ci, 2 * xi + yi]).astype(jnp.int32)
    n_ada = w_ada.shape[2]

    tr = lambda a: jnp.transpose(a[0])
    mixer_shards = [w_in[0], w_conv_pw[0], w_out[0]]
    ffn_shards = [tr(w_ffn_gate), tr(w_ffn_up), w_ffn_down[0]]
    slots = [lax.empty((N_CHIP,) + a.shape, MXU_DTYPE) for a in mixer_shards] + [lax.empty((N_CHIP,) + dw_w.shape[1:], F32)]

    (mod4, cact, wg_b, (b_gate, b_up, b_down)), (win_g, wpw_g, wout_g, dww_g) = _ada_fwd(
        c, w_ada[0], b_ada.reshape(N_CHIP, n_ada), mixer_shards, ffn_shards, dw_w[0], w_pool_group[0],
        comm=_weights_gather(slots, [True, True, True, False]))
    mod = mod4.reshape(6, D_MODEL)
    dww_full = jnp.pad(jnp.concatenate([dww_g[k] for k in range(N_CHIP)], axis=1), ((0, 1), (0, 0)))
    w_pw = wpw_g.reshape(CONV_W, CONV_W)
    w_o = wout_g.reshape(D_MODEL, D_MODEL)
    xs, tgt, gf = x[0], loss_target[0], g_final.reshape(1, D_MODEL)
    s = xs.shape[0]
    fb = b_gate.shape[1]

    (x2, y, u, z, rstd, p, ycat), (wgate_g, wup_g, wdown_g) = _mixer_fwd(
        xs, mod, g_norm1, win_g, dww_full, dw_b, conv_ln_g, conv_ln_b, w_pw, wg_b, pool_scale, w_o,
        comm=_weights_gather([b_gate, b_up, b_down], [True, True, True]))
    dx2, h2, df, act, dgg, duu, vec_f = _ffn(x2, tgt, mod, g_norm2, gf, wgate_g, wup_g, wdown_g)

    whole = lambda w: pl.BlockSpec((s, w), lambda j: (0, 0))
    cols = lambda w: pl.BlockSpec((s, w), lambda j: (0, j))
    hid = pl.BlockSpec((1, s, fb), lambda j: (j, 0, 0))
    c_gate, _ = _dw("dw_gate", dgg, hid, h2, whole(D_MODEL), N_CHIP, fb, D_MODEL)
    c_up, (r_gate,) = _dw("dw_up", duu, hid, h2, whole(D_MODEL), N_CHIP, fb, D_MODEL, comm=_sibling_halves([c_gate]))
    ((pb_gate,), (own_gate,)), _ = _chip_partials("partials_gate", place, [c_gate], [r_gate])
    ex_gate = _exchange_parts([pb_gate])
    st_gate, tok_gate = _split_start("exchange_gate_start", [pb_gate], *ex_gate)
    c_down, (r_up,) = _dw("dw_down", act, hid, df, whole(D_MODEL), N_CHIP, fb, D_MODEL,
                          comm=_sibling_halves([c_up]), after=(tok_gate,))
    sib_down = _sibling_parts([c_down])
    st_sd, tok_sd = _split_start("sibling_down_start", [c_down], *sib_down)
    ((pb_up,), (own_up,)), _ = _chip_partials("partials_up", place, [c_up], [r_up], after=(tok_sd,))
    (c_down,), (r_down,) = _split_wait("sibling_down_wait", st_sd, 1, sib_down[1], sib_down[2], after=(pb_up,),
                                       with_sources=True)
    ((pb_down,), (own_down,)), _ = _chip_partials("partials_down", place, [c_down], [r_down])
    ex_ud = _exchange_parts([pb_up, pb_down])
    st_ud, tok_ud = _split_start("exchange_up_down_start", [pb_up, pb_down], *ex_ud)
    (gx, h1, du, dy, sw, dyc, dyp, vec_d, vec_c, ddw), _ = _mixer_bwd(
        dx2, xs, y, u, z, rstd, p, mod, g_norm1, win_g, dww_full, conv_ln_g, conv_ln_b, w_pw, wg_b, pool_scale, w_o,
        after=(tok_ud,))

    g_wg, _ = _dw("dw_wg", p, cols(POOL_G), dyp, cols(POOL_G), len(POOL_WINDOWS), POOL_G, POOL_G)
    small_own = [vec_f, vec_d, vec_c, ddw, g_wg]
    ex_small = _small_parts(small_own)
    st_small, tok_small = _split_start("small_grads_start", small_own, *ex_small, zeroed=True)
    c_out, c_pw, c_in = _dw_mixer(ycat, dy, sw, dyc, h1, du, after=(tok_small,))
    mix = [c_in, c_pw, c_out]
    sib_mix = _sibling_parts(mix)
    st_sm, tok_sm = _split_start("sibling_mix_start", mix, *sib_mix)
    (rc_gate,) = _split_wait("exchange_gate_wait", st_gate, 1, ex_gate[1], ex_gate[2], after=(tok_sm,))
    rc_up, rc_down = _split_wait("exchange_up_down_wait", st_ud, 2, ex_ud[1], ex_ud[2], after=(tok_sm, rc_gate))

    ffn_fulls, _ = _sum_partials("sum_ffn", place, [own_gate, own_up, own_down], [rc_gate, rc_up, rc_down])
    join_sems, join_copies = _join_parts(ffn_fulls)
    join_in_place = lambda bufs, lands, send, recv: join_copies(bufs, send, recv)
    st_jf, tok_jf = _split_start("join_ffn_start", ffn_fulls, [], join_sems, join_in_place)
    mix, (r_in, r_pw, r_out) = _split_wait("sibling_mix_wait", st_sm, len(mix), sib_mix[1], sib_mix[2],
                                           after=(tok_jf,), with_sources=True)
    (pbs_mix, owns_mix), _ = _chip_partials("partials_mix", place, mix, [r_in, r_pw, r_out])
    g_gate, g_up, g_down = _split_wait("join_ffn_wait", st_jf, len(ffn_fulls), join_sems, join_in_place,
                                       after=(pbs_mix[0],), in_place=True)

    pad_rows = lambda a: jnp.pad(a[0], ((0, 1), (0, 0)))
    row = lambda a: a.reshape(1, -1)
    small = [(b_ada, m_b_ada, v_b_ada), (g_norm1, m_g_norm1, v_g_norm1),
             (pad_rows(dw_w), pad_rows(m_dw_w), pad_rows(v_dw_w)), (dw_b, m_dw_b, v_dw_b),
             (conv_ln_g, m_conv_ln_g, v_conv_ln_g), (conv_ln_b, m_conv_ln_b, v_conv_ln_b),
             (w_pool_group[0], m_w_pool_group[0], v_w_pool_group[0]), (pool_scale, m_pool_scale, v_pool_scale),
             (g_norm2, m_g_norm2, v_g_norm2), (row(g_final), row(m_g_final), row(v_g_final))]
    lead = lambda outs: [a[None] for a in outs]

    lands, sem_shape, copies = _exchange_parts(pbs_mix)
    state, token = _split_start("exchange_mix_start", pbs_mix, lands, sem_shape, copies)
    u_gate, u_up, u_down = _adamw(
        "adamw_ffn", [(tr(w_ffn_gate), g_gate, tr(m_w_ffn_gate), tr(v_w_ffn_gate)),
                      (tr(w_ffn_up), g_up, tr(m_w_ffn_up), tr(v_w_ffn_up)),
                      (w_ffn_down[0], g_down, m_w_ffn_down[0], v_w_ffn_down[0])],
        steps=4, after=(token,))
    o_gate = [jnp.transpose(o) for o in [g_gate] + list(u_gate)]
    o_up = [jnp.transpose(o) for o in [g_up] + list(u_up)]
    o_down = [g_down] + list(u_down)
    small_own, small_all = _split_wait("small_grads_wait", st_small, len(small_own), ex_small[1], ex_small[2],
                                       after=(u_down[0],), with_sources=True)
    loss, dmod, small_out = _adamw_small(place, small_own, small_all, small)
    (o_bada, o_g1, o_dww, o_dwb, o_lng, o_lnb, o_wg, o_ps, o_g2, o_gf) = small_out
    o_dww = [a[:CONV_K] for a in o_dww]
    o_gf = [a.reshape(D_MODEL) for a in o_gf]
    o_ada = _adamw_ada(place, cact, dmod, w_ada[0], m_w_ada[0], v_w_ada[0], after=(token,))
    rc_mix = _split_wait("exchange_mix_wait", state, len(pbs_mix), sem_shape, copies, after=(o_ada[1], u_down[0]))
    mix_fulls, _ = _sum_partials("sum_mix", place, owns_mix, rc_mix)
    g_in, g_pw, g_out = _comm_only("join_mix", _join_halves(mix_fulls))
    u_in, u_pw, u_out = _adamw(
        "adamw_mix", [(w_in[0], g_in, m_w_in[0], v_w_in[0]), (w_conv_pw[0], g_pw, m_w_conv_pw[0], v_w_conv_pw[0]),
                      (w_out[0], g_out, m_w_out[0], v_w_out[0])], steps=4)
    o_in, o_pw, o_out = [g_in] + list(u_in), [g_pw] + list(u_pw), [g_out] + list(u_out)

    per_weight = [lead(o_ada), o_bada, o_g1, lead(o_in), lead(o_dww), o_dwb, o_lng, o_lnb, lead(o_pw), lead(o_wg),
                  o_ps, lead(o_out), o_g2, lead(o_gate), lead(o_up), lead(o_down), o_gf]
    result = [loss.reshape(()), gx[None]]
    for kind in range(4):
        result += [o[kind] for o in per_weight]
    return tuple(result)
```

```python
import functools

import jax
import jax.numpy as jnp
from jax import lax
from jax.experimental import pallas as pl
from jax.experimental.pallas import tpu as pltpu

F32 = jnp.float32
MXU_DTYPE = jnp.bfloat16
EPS = 1e-6

D_MODEL = 1024
CONV_W = 512
POOL_W = 512
CONV_K = 31
POOL_WINDOWS = (2, 4, 8, 16)
POOL_G = 128
IN_W = 2 * CONV_W + POOL_W
N_CHIP = 4
N_DEV = 8
CONV_HALO = 32
POOL_HALO = 16

ADAM_LR = 0.001
ADAM_B1 = 0.9
ADAM_B2 = 0.999
ADAM_EPS = 1e-08
ADAM_WD = 0.01
ADAM_STEP = 10

MESH = pl.DeviceIdType.MESH
ANY = pl.BlockSpec(memory_space=pl.ANY)
VMEM = pl.BlockSpec(memory_space=pltpu.VMEM)


def _dot(a, b):
    return jnp.dot(a.astype(MXU_DTYPE), b.astype(MXU_DTYPE), preferred_element_type=F32)


def _dot_nt(a, b):
    return lax.dot_general(a.astype(MXU_DTYPE), b.astype(MXU_DTYPE), (((1,), (1,)), ((), ())),
                           preferred_element_type=F32)


def _dot_tn(a, b):
    return lax.dot_general(a.astype(MXU_DTYPE), b.astype(MXU_DTYPE), (((0,), (0,)), ((), ())),
                           preferred_element_type=F32)


def _sigmoid(v):
    return 1.0 / (1.0 + jnp.exp(-v))


def _full(shape):
    n = len(shape)
    return pl.BlockSpec(shape, lambda *_: (0,) * n)


def _token_tile(s):
    return 256 if s % 256 == 0 else s


SUBLANES = 8


def _row_shifts(pad_ref, shifted_ref, rows):
    for r in range(1, SUBLANES):
        shifted_ref[r - 1] = pad_ref[r:r + rows, :]

    def window(i, n):
        r, base = i % SUBLANES, i - i % SUBLANES
        if r == 0:
            return pad_ref[base:base + n, :]
        return shifted_ref[r - 1, base:base + n, :]

    return window


def _place():
    return lax.axis_index("x"), lax.axis_index("y"), lax.axis_index("c")


def _flip(x, y, r):
    return ((1 - x) if r & 2 else x, (1 - y) if r & 1 else y)


def _remote(src, dst, send_sem, recv_sem, dev):
    return pltpu.make_async_remote_copy(src_ref=src, dst_ref=dst, send_sem=send_sem, recv_sem=recv_sem,
                                        device_id=dev, device_id_type=MESH)


class _Comm:
    def __init__(self, ins, outs, aliases, scratch, start, finish, mid=None):
        self.ins, self.outs, self.aliases, self.scratch = list(ins), list(outs), dict(aliases), list(scratch)
        self.start, self.finish = start, finish
        self.mid = mid


def _both(a, b):
    na, nao, nas = len(a.ins), len(a.outs), len(a.scratch)
    aliases = dict(a.aliases)
    aliases.update({na + i: nao + o for i, o in b.aliases.items()})

    def start(ins, outs, scr):
        a.start(ins[:na], outs[:nao], scr[:nas])
        b.start(ins[na:], outs[nao:], scr[nas:])

    def finish(ins, outs, scr):
        a.finish(ins[:na], outs[:nao], scr[:nas])
        b.finish(ins[na:], outs[nao:], scr[nas:])

    def mid(ins, outs, scr):
        if a.mid:
            a.mid(ins[:na], outs[:nao], scr[:nas])
        if b.mid:
            b.mid(ins[na:], outs[nao:], scr[nas:])

    return _Comm(a.ins + b.ins, a.outs + b.outs, aliases, a.scratch + b.scratch, start, finish,
                 mid if (a.mid or b.mid) else None)


def _call(body, *, name, grid, in_specs, out_specs, out_shape, args, scratch_shapes=(), prefetch=(), comm=None,
          body_starts=False, after=()):
    in_specs = list(in_specs) + [ANY] * len(after)
    args = list(args) + list(after)
    n_pre, n_in, n_out, n_scr = len(prefetch), len(in_specs), len(out_specs), len(scratch_shapes)
    n_body_in = n_in - len(after)
    c_ins = comm.ins if comm else []
    c_outs = comm.outs if comm else []
    c_scr = comm.scratch if comm else []
    last = grid[0] - 1

    def wrapped(*refs):
        pre, refs = refs[:n_pre], refs[n_pre:]
        ins, cin = refs[:n_body_in], refs[n_in:n_in + len(c_ins)]
        refs = refs[n_in + len(c_ins):]
        outs, cout = refs[:n_out], refs[n_out:n_out + len(c_outs)]
        refs = refs[n_out + len(c_outs):]
        scr, cscr = refs[:n_scr], refs[n_scr:]
        step = pl.program_id(0)
        if comm and not body_starts:
            @pl.when(step == 0)
            def _():
                comm.start(cin, cout, cscr)

        has_mid = comm is not None and comm.mid is not None
        mid_step = grid[0] // 2 if grid[0] >= 4 else None
        if has_mid and mid_step is not None:
            @pl.when(step == mid_step)
            def _():
                comm.mid(cin, cout, cscr)

        if body_starts:
            body(lambda: comm.start(cin, cout, cscr), lambda: comm.mid(cin, cout, cscr), cout,
                 *pre, *ins, *outs, *scr)
        else:
            body(*pre, *ins, *outs, *scr)
        if comm:
            @pl.when(step == last)
            def _():
                if has_mid and mid_step is None and not body_starts:
                    comm.mid(cin, cout, cscr)
                comm.finish(cin, cout, cscr)

    aliases = {n_pre + n_in + a: n_out + b for a, b in (comm.aliases if comm else {}).items()}
    res = pl.pallas_call(
        wrapped, name=name,
        grid_spec=pltpu.PrefetchScalarGridSpec(
            num_scalar_prefetch=n_pre, grid=grid, in_specs=list(in_specs) + [ANY] * len(c_ins),
            out_specs=list(out_specs) + [ANY] * len(c_outs), scratch_shapes=list(scratch_shapes) + list(c_scr)),
        out_shape=list(out_shape) + list(c_outs),
        input_output_aliases=aliases,
        compiler_params=pltpu.CompilerParams(dimension_semantics=("arbitrary",)),
    )(*prefetch, *args, *c_ins)
    return res[:n_out], res[n_out:]


def _comm_only(name, comm):
    return _call(lambda: None, name=name, grid=(1,), in_specs=[], out_specs=[], out_shape=[], args=[], comm=comm)[1]


def _weights_gather(bufs, split):
    n = len(bufs)

    def ctx(outs):
        x, y, cc = _place()
        chips = dict(me=2 * x + y, y=2 * x + (1 - y), x=2 * (1 - x) + y, d=2 * (1 - x) + (1 - y))
        devs = dict(y=(x, 1 - y, cc), x=(1 - x, y, cc), d=(1 - x, 1 - y, cc), s=(x, y, 1 - cc))

        def piece(a, kj, pc, q=None):
            if not split[a]:
                return outs[a].at[kj]
            h = bufs[a].shape[1] // 2
            if q is None:
                return outs[a].at[kj, pl.ds(pc * h, h), :]
            return outs[a].at[kj, pl.ds(pc * h + q * (h // 2), h // 2), :]

        return cc, chips, devs, piece

    def directs(a, outs, send, recv):
        cc, chips, devs, piece = ctx(outs)
        if not split[a]:
            whole = piece(a, chips["me"], cc)
            return [_remote(whole, whole, send.at[a, k], recv.at[a, k], devs[t]) for k, t in ((0, "y"), (2, "x"), (4, "d"))]
        q = lambda i: piece(a, chips["me"], cc, i)
        return [_remote(q(0), q(0), send.at[a, 0], recv.at[a, 0], devs["y"]),
                _remote(q(1), q(1), send.at[a, 3], recv.at[a, 3], devs["x"]),
                _remote(q(1), q(1), send.at[a, 1], recv.at[a, 1], devs["y"]),
                _remote(q(0), q(0), send.at[a, 2], recv.at[a, 2], devs["x"])]

    def landed(a, k, outs, send, recv):
        cc, chips, devs, piece = ctx(outs)
        if not split[a]:
            got = piece(a, chips[{0: "y", 2: "x", 4: "d"}[k]], cc)
        elif k < 6:
            got = piece(a, chips[("y", "y", "x", "x", "d", "d")[k]], cc, (0, 1, 0, 1, 0, 1)[k])
        else:
            got = piece(a, chips[("y", "x", "d")[k - 6]], 1 - cc)
        return _remote(got, got, send.at[a, k], recv.at[a, k], devs["s"])

    def passed_on(a, outs, send, recv):
        cc, chips, devs, piece = ctx(outs)
        from_y, from_x = piece(a, chips["y"], cc, 0), piece(a, chips["x"], cc, 1)
        return [_remote(from_y, from_y, send.at[a, 4], recv.at[a, 4], devs["x"]),
                _remote(from_x, from_x, send.at[a, 5], recv.at[a, 5], devs["y"])]

    def to_sibling(a, outs, send, recv, which=(0, 1, 2)):
        cc, chips, devs, piece = ctx(outs)
        halves = [piece(a, chips[("y", "x", "d")[i]], cc) for i in which]
        return [_remote(hf, hf, send.at[a, 6 + i], recv.at[a, 6 + i], devs["s"]) for i, hf in zip(which, halves)]

    def start(ins, outs, scr):
        send, recv = scr
        per_item = [directs(a, outs, send, recv) for a in range(n)]
        for rank in range(4):
            for cps in per_item:
                if rank < len(cps):
                    cps[rank].start()

    def mid(ins, outs, scr):
        send, recv = scr
        for a in range(n):
            if split[a]:
                fy, fx = passed_on(a, outs, send, recv)
                landed(a, 0, outs, send, recv).wait_recv()
                fy.start()
                landed(a, 3, outs, send, recv).wait_recv()
                fx.start()

    def finish(ins, outs, scr):
        send, recv = scr
        for a in range(n):
            if split[a]:
                to_y, to_x = to_sibling(a, outs, send, recv, which=(0, 1))
                landed(a, 1, outs, send, recv).wait_recv()
                to_y.start()
                landed(a, 2, outs, send, recv).wait_recv()
                to_x.start()
        for a in range(n):
            if split[a]:
                for k in (4, 5):
                    landed(a, k, outs, send, recv).wait_recv()
                to_sibling(a, outs, send, recv, which=(2,))[0].start()
            else:
                for k in (0, 2, 4):
                    landed(a, k, outs, send, recv).wait_recv()
        for a in range(n):
            if split[a]:
                for k in (6, 7, 8):
                    landed(a, k, outs, send, recv).wait_recv()
            cps = directs(a, outs, send, recv)
            if split[a]:
                cps += passed_on(a, outs, send, recv) + to_sibling(a, outs, send, recv)
            for cp in cps:
                cp.wait_send()

    return _Comm(bufs, [jax.ShapeDtypeStruct(b.shape, b.dtype) for b in bufs], {i: i for i in range(n)},
                 [pltpu.SemaphoreType.DMA((n, 9)), pltpu.SemaphoreType.DMA((n, 9))], start, finish, mid)


HBM =pl.BlockSpec(memory_space=pltpu.HBM)
SEM = pl.BlockSpec(memory_space=pltpu.SEMAPHORE)
DATAFLOW = pltpu.SideEffectType.DATAFLOW_SIDE_EFFECTING


class _SemGrid:
    def __init__(self, refs, cols):
        self.refs, self.cols = refs, cols

    @property
    def at(self):
        return self

    def __getitem__(self, idx):
        return self.refs[idx[0] * self.cols + idx[1]]


def _split_start(name, srcs, lands, sem_shape, copies, zeroed=False):
    n, k = len(srcs), len(lands)
    ns = sem_shape[0] * sem_shape[1]

    def body(*refs):
        src_refs, land_refs = refs[:n], refs[n:n + k]
        send = _SemGrid(refs[n + k:n + k + ns], sem_shape[1])
        recv = _SemGrid(refs[n + k + ns:n + k + 2 * ns], sem_shape[1])
        token = refs[-1]
        for cp in copies(src_refs, land_refs, send, recv):
            cp.start()
        token[...] = jnp.zeros(token.shape, F32)

    hbm = lambda a: pltpu.with_memory_space_constraint(a, pltpu.HBM)
    zones = [jnp.zeros(l.shape, l.dtype) if zeroed else lax.empty(l.shape, l.dtype) for l in lands]
    out = pl.pallas_call(
        body, name=name,
        out_shape=[pltpu.SemaphoreType.DMA(())] * (2 * ns)
        + [pltpu.HBM(a.shape, a.dtype) for a in list(srcs) + list(lands)] + [jax.ShapeDtypeStruct((8, 128), F32)],
        in_specs=[HBM] * (n + k), out_specs=[SEM] * (2 * ns) + [HBM] * (n + k) + [VMEM],
        input_output_aliases={i: 2 * ns + i for i in range(n + k)},
        compiler_params=pltpu.CompilerParams(has_side_effects=DATAFLOW),
    )(*[hbm(a) for a in srcs], *[hbm(z) for z in zones])
    return out[:-1], out[-1]


def _split_wait(name, state, n, sem_shape, copies, after, in_place=False, with_sources=False):
    ns = sem_shape[0] * sem_shape[1]
    sems, bufs = state[:2 * ns], state[2 * ns:]
    k = len(bufs) - n

    def body(*refs):
        src_refs, land_refs = refs[:n], refs[n:n + k]
        send = _SemGrid(refs[n + k:n + k + ns], sem_shape[1])
        recv = _SemGrid(refs[n + k + ns:n + k + 2 * ns], sem_shape[1])
        cps = copies(src_refs, land_refs, send, recv)
        for cp in cps:
            cp.wait_send()
        for cp in cps:
            cp.wait_recv()

    out = pl.pallas_call(
        body, name=name,
        out_shape=[pltpu.HBM(a.shape, a.dtype) for a in bufs],
        in_specs=[HBM] * (n + k) + [SEM] * (2 * ns) + [ANY] * len(after), out_specs=[HBM] * (n + k),
        input_output_aliases={i: i for i in range(n + k)},
        compiler_params=pltpu.CompilerParams(has_side_effects=DATAFLOW),
    )(*bufs, *sems, *after)
    if with_sources:
        return out[:n], out[n:]
    return out[:n] if in_place else out[n:]


def _direct_phases(copies):
    def start(ins, outs, scr):
        for cp in copies(ins, outs, *scr):
            cp.start()

    def finish(ins, outs, scr):
        cps = copies(ins, outs, *scr)
        for cp in cps:
            cp.wait_recv()
        for cp in cps:
            cp.wait_send()

    return start, finish


def _sibling_parts(gs):
    n = len(gs)

    def copies(ins, outs, send, recv):
        x, y, cc = _place()
        cps = []
        for a in range(n):
            h = gs[a].shape[1] // 2
            cps.append(_remote(ins[a].at[:, pl.ds((1 - cc) * h, h), :], outs[a], send.at[a, 0], recv.at[a, 0],
                               (x, y, 1 - cc)))
        return cps

    return [jax.ShapeDtypeStruct((N_CHIP, g.shape[1] // 2, g.shape[2]), F32) for g in gs], (n, 1), copies


def _sibling_halves(gs):
    lands, sem_shape, copies = _sibling_parts(gs)
    start, finish = _direct_phases(copies)
    return _Comm(gs, lands, {}, [pltpu.SemaphoreType.DMA(sem_shape), pltpu.SemaphoreType.DMA(sem_shape)],
                 start, finish)


def _exchange_parts(pbs):
    n = len(pbs)

    def copies(ins, outs, send, recv):
        x, y, cc = _place()
        cps = []
        for a in range(n):
            for r in range(1, N_CHIP):
                kx, ky = _flip(x, y, r)
                cps.append(_remote(ins[a].at[2 * kx + ky], outs[a].at[r - 1], send.at[a, r - 1], recv.at[a, r - 1],
                                   (kx, ky, cc)))
        return cps

    lands = [jax.ShapeDtypeStruct((N_CHIP - 1,) + p.shape[1:], p.dtype) for p in pbs]
    return lands, (n, N_CHIP - 1), copies


def _small_parts(arrs):
    n = len(arrs)

    def copies(ins, outs, send, recv):
        x, y, cc = _place()
        b = 4 * x + 2 * y + cc
        cps = []
        for a in range(n):
            for r in range(1, N_DEV):
                dev = ((1 - x) if r & 4 else x, (1 - y) if r & 2 else y, (1 - cc) if r & 1 else cc)
                cps.append(_remote(ins[a], outs[a].at[b], send.at[a, r - 1], recv.at[a, r - 1], dev))
        return cps

    lands = [jax.ShapeDtypeStruct((N_DEV,) + a.shape, a.dtype) for a in arrs]
    return lands, (n, N_DEV - 1), copies


def _exchange_partials(pbs):
    lands, sem_shape, copies = _exchange_parts(pbs)
    start, finish = _direct_phases(copies)
    return _Comm(pbs, lands, {}, [pltpu.SemaphoreType.DMA(sem_shape), pltpu.SemaphoreType.DMA(sem_shape)],
                 start, finish)


def _join_parts(fulls):
    n = len(fulls)

    def copies(bufs, send, recv):
        x, y, cc = _place()
        cps = []
        for a in range(n):
            h = fulls[a].shape[0] // 2
            mine = bufs[a].at[pl.ds(cc * h, h), :]
            cps.append(_remote(mine, mine, send.at[a, 0], recv.at[a, 0], (x, y, 1 - cc)))
        return cps

    return (n, 1), copies


def _join_halves(fulls):
    sem_shape, copies = _join_parts(fulls)
    start, finish = _direct_phases(lambda ins, outs, send, recv: copies(outs, send, recv))
    return _Comm(fulls, [jax.ShapeDtypeStruct(f.shape, F32) for f in fulls], {i: i for i in range(len(fulls))},
                 [pltpu.SemaphoreType.DMA(sem_shape), pltpu.SemaphoreType.DMA(sem_shape)], start, finish)


def _mixer_fwd(x, mod, g1, w_in, dww, dwb, lng, lnb, w_pw, wg, pscale, w_out, comm=None):
    s = x.shape[0]
    ts = _token_tile(s)
    nt = s // ts

    def body(x_ref, mod_ref, g1_ref, win_ref, dww_ref, dwb_ref, lng_ref, lnb_ref, wpw_ref, wg_ref, ps_ref,
             wout_ref, x2_ref, y_ref, u_ref, z_ref, rstd_ref, p_ref, ycat_ref, gpad, vpad, gshift):
        i = pl.program_id(0)

        @pl.when(i == 0)
        def _():
            gpad[0:CONV_HALO, :] = jnp.zeros((CONV_HALO, CONV_W), F32)
            vpad[0:POOL_HALO, :] = jnp.zeros((POOL_HALO, POOL_W), F32)

        xt = x_ref[...]
        sh1 = mod_ref[0:1, :]
        sc1 = mod_ref[1:2, :]
        gt1 = mod_ref[2:3, :]
        r1 = lax.rsqrt(jnp.mean(xt * xt, axis=-1, keepdims=True) + EPS)
        h1 = (xt * r1 * g1_ref[...]) * (1.0 + sc1) + sh1
        h1b = h1.astype(MXU_DTYPE)
        u = jnp.concatenate([_dot(h1b, win_ref[j]) for j in range(N_CHIP)], axis=1)
        u_ref[...] = u
        a = u[:, :CONV_W]
        g = u[:, CONV_W:2 * CONV_W]
        v = u[:, 2 * CONV_W:]

        gpad[CONV_HALO:CONV_HALO + ts, :] = a * _sigmoid(g)
        window = _row_shifts(gpad, gshift, ts + CONV_HALO - SUBLANES)
        cv = jnp.broadcast_to(dwb_ref[...], (ts, CONV_W))
        off = CONV_HALO - (CONV_K - 1)
        for k in range(CONV_K):
            cv = cv + dww_ref[k:k + 1, :] * window(off + k, ts)
        gpad[0:CONV_HALO, :] = gpad[ts:ts + CONV_HALO, :]

        mu = jnp.mean(cv, axis=-1, keepdims=True)
        cc = cv - mu
        rstd = lax.rsqrt(jnp.mean(cc * cc, axis=-1, keepdims=True) + EPS)
        z = cc * rstd
        z_ref[...] = z
        rstd_ref[...] = rstd
        ln = z * lng_ref[...] + lnb_ref[...]
        sw = ln * _sigmoid(ln)
        yconv = _dot(sw, wpw_ref[...])

        vpad[POOL_HALO:POOL_HALO + ts, :] = v
        t = i * ts + lax.broadcasted_iota(jnp.int32, (ts, 1), 0)
        ps, ypool = [], []
        for gi, w in enumerate(POOL_WINDOWS):
            cols = slice(gi * POOL_G, (gi + 1) * POOL_G)
            acc = vpad[POOL_HALO:POOL_HALO + ts, cols]
            for d in range(1, w):
                acc = acc + vpad[POOL_HALO - d:POOL_HALO - d + ts, cols]
            cnt = jnp.minimum(t + 1, w).astype(F32)
            pg = (acc / cnt - v[:, cols]).astype(MXU_DTYPE)
            ps.append(pg)
            ypool.append(_dot(pg, wg_ref[gi]))
        vpad[0:POOL_HALO, :] = vpad[ts:ts + POOL_HALO, :]
        p_ref[...] = jnp.concatenate(ps, axis=1)
        ypool = jnp.concatenate(ypool, axis=1) * ps_ref[...]

        ycat = jnp.concatenate([yconv, ypool], axis=1).astype(MXU_DTYPE)
        ycat_ref[...] = ycat
        y = _dot(ycat, wout_ref[...])
        y_ref[...] = y
        x2_ref[...] = xt + gt1 * y

    tile = lambda w: pl.BlockSpec((ts, w), lambda i: (i, 0))
    return _call(
        body, name="mixer_fwd", grid=(nt,),
        in_specs=[tile(D_MODEL), _full(mod.shape), _full(g1.shape), _full(w_in.shape), _full(dww.shape),
                  _full(dwb.shape), _full(lng.shape), _full(lnb.shape), _full(w_pw.shape), _full(wg.shape),
                  _full(pscale.shape), _full(w_out.shape)],
        out_specs=[tile(D_MODEL), tile(D_MODEL), tile(IN_W), tile(CONV_W), tile(1), tile(POOL_W), tile(D_MODEL)],
        out_shape=[jax.ShapeDtypeStruct((s, D_MODEL), F32), jax.ShapeDtypeStruct((s, D_MODEL), F32),
                   jax.ShapeDtypeStruct((s, IN_W), F32), jax.ShapeDtypeStruct((s, CONV_W), F32),
                   jax.ShapeDtypeStruct((s, 1), F32), jax.ShapeDtypeStruct((s, POOL_W), MXU_DTYPE),
                   jax.ShapeDtypeStruct((s, D_MODEL), MXU_DTYPE)],
        scratch_shapes=[pltpu.VMEM((ts + CONV_HALO, CONV_W), F32), pltpu.VMEM((ts + POOL_HALO, POOL_W), F32),
                        pltpu.VMEM((SUBLANES - 1, ts + CONV_HALO - SUBLANES, CONV_W), F32)],
        args=(x, mod, g1, w_in, dww, dwb, lng, lnb, w_pw, wg, pscale, w_out), comm=comm)


def _ffn(x2, tgt, mod, g2, gf, w_gate, w_up, w_down):
    s = x2.shape[0]
    ts = _token_tile(s)
    nt = s // ts
    fb = w_gate.shape[1]

    def body(x2_ref, tgt_ref, mod_ref, g2_ref, gf_ref, wgt_ref, wup_ref, wdn_ref,
             dx2_ref, h2_ref, df_ref, act_ref, dgg_ref, duu_ref, vec_ref, gg_s, uu_s):
        i = pl.program_id(0)

        @pl.when(i == 0)
        def _():
            vec_ref[...] = jnp.zeros(vec_ref.shape, F32)

        x2t = x2_ref[...]
        sh2 = mod_ref[3:4, :]
        sc2 = mod_ref[4:5, :]
        gt2 = mod_ref[5:6, :]
        g2v = g2_ref[...]
        gfv = gf_ref[...]
        r2 = lax.rsqrt(jnp.mean(x2t * x2t, axis=-1, keepdims=True) + EPS)
        xh2 = x2t * r2
        n2 = xh2 * g2v
        h2b = (n2 * (1.0 + sc2) + sh2).astype(MXU_DTYPE)
        h2_ref[...] = h2b
        f = jnp.zeros((ts, D_MODEL), F32)
        for j in range(N_CHIP):
            gg = _dot_nt(h2b, wgt_ref[j])
            uu = _dot_nt(h2b, wup_ref[j])
            gg_s[j] = gg
            uu_s[j] = uu
            actb = (gg * _sigmoid(gg) * uu).astype(MXU_DTYPE)
            act_ref[j] = actb
            f = f + _dot(actb, wdn_ref[j])
        x3 = x2t + gt2 * f
        r3 = lax.rsqrt(jnp.mean(x3 * x3, axis=-1, keepdims=True) + EPS)
        xh3 = x3 * r3
        diff = xh3 * gfv - tgt_ref[...]
        dout = diff * (1.0 / D_MODEL)
        dn3 = dout * gfv
        dx3 = r3 * (dn3 - xh3 * jnp.mean(dn3 * xh3, axis=-1, keepdims=True))
        dfb = (dx3 * gt2).astype(MXU_DTYPE)
        df_ref[...] = dfb
        dh2 = jnp.zeros((ts, D_MODEL), F32)
        for j in range(N_CHIP):
            dact = _dot_nt(dfb, wdn_ref[j])
            gg = gg_s[j]
            uu = uu_s[j]
            sg = _sigmoid(gg)
            duu = (dact * (gg * sg)).astype(MXU_DTYPE)
            dgg = (dact * uu * (sg * (1.0 + gg * (1.0 - sg)))).astype(MXU_DTYPE)
            duu_ref[j] = duu
            dgg_ref[j] = dgg
            dh2 = dh2 + _dot(dgg, wgt_ref[j]) + _dot(duu, wup_ref[j])
        dn2 = dh2 * (1.0 + sc2)
        dxh2 = dn2 * g2v
        dx2_ref[...] = dx3 + r2 * (dxh2 - xh2 * jnp.mean(dxh2 * xh2, axis=-1, keepdims=True))

        col = lambda a: jnp.sum(a, axis=0, keepdims=True)
        vec_ref[0:1, :] += col(dout * xh3)
        vec_ref[1:2, :] += col(dx3 * f)
        vec_ref[2:3, :] += col(dh2)
        vec_ref[3:4, :] += col(dh2 * n2)
        vec_ref[4:5, :] += col(dn2 * xh2)
        vec_ref[5:6, :] += col(diff * diff)

    tile = lambda w: pl.BlockSpec((ts, w), lambda i: (i, 0))
    tile3 = pl.BlockSpec((N_CHIP, ts, fb), lambda i: (0, i, 0))
    once = lambda a: pl.BlockSpec(a.shape, lambda i: (0,) * a.ndim, pipeline_mode=pl.Buffered(1))
    hid = jax.ShapeDtypeStruct((N_CHIP, s, fb), MXU_DTYPE)
    return pl.pallas_call(
        body, name="ffn", grid=(nt,),
        in_specs=[tile(D_MODEL), tile(D_MODEL), _full(mod.shape), _full(g2.shape), _full(gf.shape),
                  once(w_gate), once(w_up), once(w_down)],
        out_specs=[tile(D_MODEL), tile(D_MODEL), tile(D_MODEL), tile3, tile3, tile3, _full((8, D_MODEL))],
        out_shape=[jax.ShapeDtypeStruct((s, D_MODEL), F32), jax.ShapeDtypeStruct((s, D_MODEL), MXU_DTYPE),
                   jax.ShapeDtypeStruct((s, D_MODEL), MXU_DTYPE), hid, hid, hid,
                   jax.ShapeDtypeStruct((8, D_MODEL), F32)],
        scratch_shapes=[pltpu.VMEM((N_CHIP, ts, fb), F32), pltpu.VMEM((N_CHIP, ts, fb), F32)],
        compiler_params=pltpu.CompilerParams(dimension_semantics=("arbitrary",)),
    )(x2, tgt, mod, g2, gf, w_gate, w_up, w_down)


def _mixer_bwd(dx2, x, y, u, z, rstd, p, mod, g1, w_in, dww, lng, lnb, w_pw, wg, pscale, w_out, comm=None, after=()):
    s = x.shape[0]
    ts = _token_tile(s)
    nt = s // ts

    def body(dx2_ref, x_ref, y_ref, u_ref, z_ref, rstd_ref, p_ref, mod_ref, g1_ref, win_ref, dww_ref, lng_ref,
             lnb_ref, wpw_ref, wg_ref, ps_ref, wout_ref,
             gx_ref, h1_ref, du_ref, dy_ref, sw_ref, dyc_ref, gwg_ref, vd_ref, vc_ref, ddw_ref, dcpad, dppad,
             dshift):
        i = pl.program_id(0)
        tix = nt - 1 - i

        @pl.when(i == 0)
        def _():
            gwg_ref[...] = jnp.zeros(gwg_ref.shape, F32)
            vd_ref[...] = jnp.zeros(vd_ref.shape, F32)
            vc_ref[...] = jnp.zeros(vc_ref.shape, F32)
            ddw_ref[...] = jnp.zeros(ddw_ref.shape, F32)
            dcpad[ts:ts + CONV_HALO, :] = jnp.zeros((CONV_HALO, CONV_W), F32)
            dppad[ts:ts + POOL_HALO, :] = jnp.zeros((POOL_HALO, POOL_W), F32)

        col = lambda a: jnp.sum(a, axis=0, keepdims=True)
        sh1 = mod_ref[0:1, :]
        sc1 = mod_ref[1:2, :]
        gt1 = mod_ref[2:3, :]
        dx2t = dx2_ref[...]
        vd_ref[0:1, :] += col(dx2t * y_ref[...])
        dyb = (dx2t * gt1).astype(MXU_DTYPE)
        dy_ref[...] = dyb
        dycat = _dot_nt(dyb, wout_ref[...])
        dyconv = dycat[:, :CONV_W]
        dypool = dycat[:, CONV_W:]

        pt = p_ref[...]
        t = tix * ts + lax.broadcasted_iota(jnp.int32, (ts, 1), 0)
        psc = ps_ref[...]
        dypb = (dypool * psc).astype(MXU_DTYPE)
        dps, ypre = [], []
        for gi, w in enumerate(POOL_WINDOWS):
            cols = slice(gi * POOL_G, (gi + 1) * POOL_G)
            gwg_ref[gi] += _dot_tn(pt[:, cols], dypb[:, cols])
            ypre.append(_dot(pt[:, cols], wg_ref[gi]))
            dpg = _dot_nt(dypb[:, cols], wg_ref[gi])
            dps.append(dpg)
            cnt = jnp.minimum(t + 1, w).astype(F32)
            dppad[0:ts, cols] = dpg / cnt
        vc_ref[0:1, :] += col(dypool * jnp.concatenate(ypre, axis=1))
        dvs = []
        for gi, w in enumerate(POOL_WINDOWS):
            cols = slice(gi * POOL_G, (gi + 1) * POOL_G)
            acc = dppad[0:ts, cols]
            for d in range(1, w):
                acc = acc + dppad[d:d + ts, cols]
            dvs.append(acc - dps[gi])
        dv = jnp.concatenate(dvs, axis=1)
        dppad[ts:ts + POOL_HALO, :] = dppad[0:POOL_HALO, :]

        zt = z_ref[...]
        lngv = lng_ref[...]
        ln = zt * lngv + lnb_ref[...]
        sg = _sigmoid(ln)
        swb = (ln * sg).astype(MXU_DTYPE)
        sw_ref[...] = swb
        dycb = dyconv.astype(MXU_DTYPE)
        dyc_ref[...] = dycb
        dln = _dot_nt(dycb, wpw_ref[...]) * (sg * (1.0 + ln * (1.0 - sg)))
        vc_ref[1:2, :] += col(dln * zt)
        vc_ref[2:3, :] += col(dln)
        dz = dln * lngv
        dcv = rstd_ref[...] * (dz - jnp.mean(dz, axis=-1, keepdims=True)
                               - zt * jnp.mean(dz * zt, axis=-1, keepdims=True))
        vc_ref[3:4, :] += col(dcv)
        dcpad[0:ts, :] = dcv
        ut = u_ref[...]
        a = ut[:, :CONV_W]
        g = ut[:, CONV_W:2 * CONV_W]
        sgg = _sigmoid(g)
        glu = a * sgg
        window = _row_shifts(dcpad, dshift, ts + CONV_HALO - SUBLANES)
        dglu = jnp.zeros((ts, CONV_W), F32)
        for k in range(CONV_K):
            sh = window(CONV_K - 1 - k, ts)
            dglu = dglu + dww_ref[k:k + 1, :] * sh
            ddw_ref[k:k + 1, :] += col(glu * sh)
        dcpad[ts:ts + CONV_HALO, :] = dcpad[0:CONV_HALO, :]
        da = dglu * sgg
        dg = dglu * a * sgg * (1.0 - sgg)
        dub = jnp.concatenate([da, dg, dv], axis=1).astype(MXU_DTYPE)
        du_ref[...] = dub
        cw = IN_W // N_CHIP
        dh1 = jnp.zeros((ts, D_MODEL), F32)
        for j in range(N_CHIP):
            dh1 = dh1 + _dot_nt(dub[:, j * cw:(j + 1) * cw], win_ref[j])

        xt = x_ref[...]
        g1v = g1_ref[...]
        r1 = lax.rsqrt(jnp.mean(xt * xt, axis=-1, keepdims=True) + EPS)
        xh1 = xt * r1
        n1 = xh1 * g1v
        h1_ref[...] = (n1 * (1.0 + sc1) + sh1).astype(MXU_DTYPE)
        vd_ref[1:2, :] += col(dh1)
        vd_ref[2:3, :] += col(dh1 * n1)
        dn1 = dh1 * (1.0 + sc1)
        vd_ref[3:4, :] += col(dn1 * xh1)
        dxh = dn1 * g1v
        gx_ref[...] = dx2t + r1 * (dxh - xh1 * jnp.mean(dxh * xh1, axis=-1, keepdims=True))

    tile = lambda w: pl.BlockSpec((ts, w), lambda i: (nt - 1 - i, 0))
    bf = lambda w: jax.ShapeDtypeStruct((s, w), MXU_DTYPE)
    return _call(
        body, name="mixer_bwd", grid=(nt,),
        in_specs=[tile(D_MODEL), tile(D_MODEL), tile(D_MODEL), tile(IN_W), tile(CONV_W), tile(1), tile(POOL_W),
                  _full(mod.shape), _full(g1.shape), _full(w_in.shape), _full(dww.shape), _full(lng.shape),
                  _full(lnb.shape), _full(w_pw.shape), _full(wg.shape), _full(pscale.shape), _full(w_out.shape)],
        out_specs=[tile(D_MODEL), tile(D_MODEL), tile(IN_W), tile(D_MODEL), tile(CONV_W), tile(CONV_W),
                   _full(wg.shape), _full((8, D_MODEL)), _full((8, CONV_W)), _full((32, CONV_W))],
        out_shape=[jax.ShapeDtypeStruct((s, D_MODEL), F32), bf(D_MODEL), bf(IN_W), bf(D_MODEL), bf(CONV_W),
                   bf(CONV_W), jax.ShapeDtypeStruct(wg.shape, F32), jax.ShapeDtypeStruct((8, D_MODEL), F32),
                   jax.ShapeDtypeStruct((8, CONV_W), F32), jax.ShapeDtypeStruct((32, CONV_W), F32)],
        scratch_shapes=[pltpu.VMEM((ts + CONV_HALO, CONV_W), F32), pltpu.VMEM((ts + POOL_HALO, POOL_W), F32),
                        pltpu.VMEM((SUBLANES - 1, ts + CONV_HALO - SUBLANES, CONV_W), F32)],
        args=(dx2, x, y, u, z, rstd, p, mod, g1, w_in, dww, lng, lnb, w_pw, wg, pscale, w_out), comm=comm,
        after=after)


def _dw(name, a, a_spec, b, b_spec, nb, mb, nbk, comm=None, after=()):
    def body(a_ref, b_ref, o_ref):
        av = a_ref[...]
        bv = b_ref[...]
        av = av.reshape(av.shape[-2:])
        bv = bv.reshape(bv.shape[-2:])
        o_ref[0] = _dot_tn(av, bv)

    (out,), rest = _call(
        body, name=name, grid=(nb,), in_specs=[a_spec, b_spec],
        out_specs=[pl.BlockSpec((1, mb, nbk), lambda j: (j, 0, 0))],
        out_shape=[jax.ShapeDtypeStruct((nb, mb, nbk), F32)], args=(a, b), comm=comm, after=after)
    return out, rest


def _dw_mixer(ycat, dy, sw, dyc, h1, du, after=()):
    s = ycat.shape[0]

    def body(ycat_ref, dy_ref, sw_ref, dyc_ref, h1_ref, du_ref, out_ref, pw_ref, in_ref):
        out_ref[0] = _dot_tn(ycat_ref[...], dy_ref[...])
        pw_ref[0] = _dot_tn(sw_ref[...], dyc_ref[...])
        in_ref[0] = _dot_tn(h1_ref[...], du_ref[...])

    whole = lambda w: pl.BlockSpec((s, w), lambda j: (0, 0))
    cols = lambda w: pl.BlockSpec((s, w), lambda j: (0, j))
    blk = lambda m, n: pl.BlockSpec((1, m, n), lambda j: (j, 0, 0))
    shapes = [(D_MODEL // N_CHIP, D_MODEL), (CONV_W // N_CHIP, CONV_W), (D_MODEL, IN_W // N_CHIP)]
    res, _ = _call(
        body, name="dw_mixer", grid=(N_CHIP,),
        in_specs=[cols(D_MODEL // N_CHIP), whole(D_MODEL), cols(CONV_W // N_CHIP), whole(CONV_W), whole(D_MODEL),
                  cols(IN_W // N_CHIP)],
        out_specs=[blk(m, n) for m, n in shapes],
        out_shape=[jax.ShapeDtypeStruct((N_CHIP, m, n), F32) for m, n in shapes],
        args=(ycat, dy, sw, dyc, h1, du), after=after)
    return res


def _ada_fwd(c, w_ada, b4, first, later, dww, wg, comm):
    nc = w_ada.shape[1]
    nf, nl = len(first), len(later)
    shards = list(first) + list(later)

    def body(start_comm, mid_comm, gathered, c_ref, w_ref, b4_ref, *refs):
        shard_refs, refs = refs[:nf + nl], refs[nf + nl:]
        dww_ref, wg_ref, mod_ref, cact_ref, wgb_ref = refs[:5]
        later_refs, refs = refs[5:5 + nl], refs[5 + nl:]
        call, part, parts = refs[:3]
        stages, refs = refs[3:3 + nf + nl], refs[3 + nf + nl:]
        send1, recv1, send2, recv2, lsem = refs
        x, y, cc = _place()
        b = 4 * x + 2 * y + cc
        j = 2 * x + y

        def slot_copies(lo, hi):
            cps = []
            for a in range(lo, hi):
                dst = gathered[a] if a < nf else later_refs[a - nf]
                cps.append(pltpu.make_async_copy(stages[a], dst.at[j], lsem.at[a]))
            return cps

        call[b] = c_ref[...]
        sends = []
        for r in range(1, N_DEV):
            dev = ((1 - x) if r & 4 else x, (1 - y) if r & 2 else y, (1 - cc) if r & 1 else cc)
            cp = _remote(call.at[b], call.at[b], send1.at[r - 1], recv1.at[r - 1], dev)
            cp.start()
            sends.append(cp)
        for a in range(nf):
            stages[a][...] = shard_refs[a][...].astype(MXU_DTYPE)
        dww_copy = pltpu.make_async_copy(dww_ref, gathered[nf].at[j], lsem.at[nf + nl])
        dww_copy.start()
        for cp in slot_copies(0, nf):
            cp.start()
        for r in range(1, N_DEV):
            src_b = lax.bitwise_xor(b, r)
            _remote(call.at[src_b], call.at[src_b], send1.at[r - 1], recv1.at[r - 1], (x, y, cc)).wait_recv()
        for cp in sends:
            cp.wait_send()
        for cp in slot_copies(0, nf):
            cp.wait()
        dww_copy.wait()
        start_comm()
        for i in range(N_DEV):
            ci = call[i]
            cact_ref[i:i + 1, :] = ci * _sigmoid(ci)
        part[...] = jnp.dot(cact_ref[...], w_ref[...], preferred_element_type=F32, precision=lax.Precision.HIGHEST)
        sends = []
        for r in range(1, N_CHIP):
            kx, ky = _flip(x, y, r)
            cp = _remote(part, parts.at[j], send2.at[r - 1], recv2.at[r - 1], (kx, ky, cc))
            cp.start()
            sends.append(cp)
        parts[j] = part[...]
        for a in range(nf, nf + nl):
            stages[a][...] = shard_refs[a][...].astype(MXU_DTYPE)
        for cp in slot_copies(nf, nf + nl):
            cp.start()
        wgb_ref[...] = wg_ref[...].astype(MXU_DTYPE)
        mid_comm()
        for r in range(1, N_CHIP):
            kx, ky = _flip(x, y, r)
            kj = 2 * kx + ky
            _remote(part, parts.at[kj], send2.at[r - 1], recv2.at[r - 1], (x, y, cc)).wait_recv()
        for cp in sends:
            cp.wait_send()
        mine = lax.broadcasted_iota(jnp.int32, (N_DEV, 1), 0) == b
        for k in range(N_CHIP):
            row = jnp.sum(jnp.where(mine, parts[k], 0.0), axis=0, keepdims=True) + b4_ref[k:k + 1, :]
            lo = k * nc
            while lo < (k + 1) * nc:
                q, at = lo // D_MODEL, lo % D_MODEL
                n = min(D_MODEL - at, (k + 1) * nc - lo)
                mod_ref[q:q + 1, at:at + n] = row[:, lo - k * nc:lo - k * nc + n]
                lo += n
        for cp in slot_copies(nf, nf + nl):
            cp.wait()

    res, rest = _call(
        body, name="ada_fwd", grid=(1,),
        in_specs=[VMEM] * (5 + nf + nl), out_specs=[VMEM, VMEM, VMEM] + [ANY] * nl,
        out_shape=[jax.ShapeDtypeStruct((N_CHIP * nc // D_MODEL, D_MODEL), F32),
                   jax.ShapeDtypeStruct((N_DEV, D_MODEL), F32),
                   jax.ShapeDtypeStruct(wg.shape, MXU_DTYPE)]
        + [jax.ShapeDtypeStruct((N_CHIP,) + a.shape, MXU_DTYPE) for a in later],
        scratch_shapes=[pltpu.VMEM((N_DEV, 1, D_MODEL), F32), pltpu.VMEM((N_DEV, nc), F32),
                        pltpu.VMEM((N_CHIP, N_DEV, nc), F32)]
        + [pltpu.VMEM(a.shape, MXU_DTYPE) for a in shards]
        + [pltpu.SemaphoreType.DMA((N_DEV - 1,)), pltpu.SemaphoreType.DMA((N_DEV - 1,)),
           pltpu.SemaphoreType.DMA((N_CHIP - 1,)), pltpu.SemaphoreType.DMA((N_CHIP - 1,)),
           pltpu.SemaphoreType.DMA((nf + nl + 1,))],
        args=(c, w_ada, b4, *shards, dww, wg), comm=comm, body_starts=True)
    return (res[0], res[1], res[2], res[3:]), rest


def _chip_partials(name, place, gs, rs, comm=None, after=()):
    n = len(gs)

    def body(pref, *refs):
        g_refs, r_refs = refs[:n], refs[n:2 * n]
        pb_refs, own_refs = refs[2 * n:3 * n], refs[3 * n:]
        jj = pl.program_id(0)
        for a in range(n):
            sm = g_refs[a][0] + r_refs[a][0]
            pb_refs[a][0] = sm.astype(MXU_DTYPE)

            @pl.when(jj == pref[1])
            def _(a=a, sm=sm):
                own_refs[a][...] = sm

    halves = [(g.shape[1] // 2, g.shape[2]) for g in gs]
    in_specs = [pl.BlockSpec((1, h, w), lambda jj, pref: (jj, pref[0], 0)) for h, w in halves]
    in_specs += [pl.BlockSpec((1, h, w), lambda jj, pref: (jj, 0, 0)) for h, w in halves]
    out_specs = [pl.BlockSpec((1, h, w), lambda jj, pref: (jj, 0, 0)) for h, w in halves]
    out_specs += [pl.BlockSpec((h, w), lambda jj, pref: (0, 0)) for h, w in halves]
    out, rest = _call(
        body, name=name, grid=(N_CHIP,), in_specs=in_specs, out_specs=out_specs,
        out_shape=[jax.ShapeDtypeStruct((N_CHIP, h, w), MXU_DTYPE) for h, w in halves]
        + [jax.ShapeDtypeStruct((h, w), F32) for h, w in halves],
        args=(*gs, *rs), prefetch=(place,), comm=comm, after=after)
    return (out[:n], out[n:]), rest


def _sum_partials(name, place, owns, recvd, comm=None, after=()):
    n = len(owns)

    def body(pref, *refs):
        o_refs, r_refs, out_refs = refs[:n], refs[n:2 * n], refs[2 * n:]
        for a in range(n):
            acc = o_refs[a][...]
            for r in range(N_CHIP - 1):
                acc = acc + r_refs[a][r].astype(F32)
            out_refs[a][...] = acc

    full = lambda a: pl.BlockSpec(a.shape, lambda i, pref: (0,) * a.ndim)
    return _call(
        body, name=name, grid=(1,), in_specs=[full(a) for a in list(owns) + list(recvd)],
        out_specs=[pl.BlockSpec(o.shape, lambda i, pref: (pref[0], 0)) for o in owns],
        out_shape=[jax.ShapeDtypeStruct((2 * o.shape[0], o.shape[1]), F32) for o in owns],
        args=(*owns, *recvd), prefetch=(place,), comm=comm, after=after)


def _adamw_math(w, g, m, v):
    m = ADAM_B1 * m + (1.0 - ADAM_B1) * g
    v = ADAM_B2 * v + (1.0 - ADAM_B2) * (g * g)
    m_hat = m / (1.0 - ADAM_B1 ** ADAM_STEP)
    v_hat = v / (1.0 - ADAM_B2 ** ADAM_STEP)
    delta = -ADAM_LR * (m_hat / (jnp.sqrt(v_hat) + ADAM_EPS) + ADAM_WD * w)
    return delta, m, v


def _row_tile(rows):
    for t in (512, 352, 256, 128):
        if rows % t == 0:
            return t
    return rows


def _adamw(name, wgmv, steps, after=()):
    n = len(wgmv)

    def body(*refs):
        ins, outs = refs[:4 * n], refs[4 * n:]
        for i in range(n):
            w_ref, g_ref, m_ref, v_ref = ins[4 * i:4 * i + 4]
            d_ref, nm_ref, nv_ref = outs[3 * i:3 * i + 3]
            d_ref[...], nm_ref[...], nv_ref[...] = _adamw_math(w_ref[...], g_ref[...], m_ref[...], v_ref[...])

    in_specs, out_specs, out_shape, args = [], [], [], []
    for w, g, m, v in wgmv:
        rows, cols = w.shape
        spec = pl.BlockSpec((rows // steps, cols), lambda i: (i, 0))
        in_specs += [spec] * 4
        out_specs += [spec] * 3
        out_shape += [jax.ShapeDtypeStruct(w.shape, F32)] * 3
        args += [w, g, m, v]
    res, _ = _call(body, name=name, grid=(steps,), in_specs=in_specs, out_specs=out_specs, out_shape=out_shape,
                   args=args, after=after)
    return [res[3 * i:3 * i + 3] for i in range(n)]


def _adamw_ada(place, cact, dmod, w, m, v, after=()):
    rows, cols = w.shape
    tr = _row_tile(rows)

    def body(pref, ca_ref, dm_ref, w_ref, m_ref, v_ref, g_ref, d_ref, nm_ref, nv_ref):
        g = lax.dot_general(ca_ref[...], dm_ref[...], (((0,), (0,)), ((), ())), preferred_element_type=F32,
                            precision=lax.Precision.HIGHEST)
        g_ref[...] = g
        d_ref[...], nm_ref[...], nv_ref[...] = _adamw_math(w_ref[...], g, m_ref[...], v_ref[...])

    spec = pl.BlockSpec((tr, cols), lambda i, pref: (i, 0))
    return _call(
        body, name="adamw_ada", grid=(rows // tr,),
        in_specs=[pl.BlockSpec((N_DEV, tr), lambda i, pref: (0, i)),
                  pl.BlockSpec((N_DEV, cols), lambda i, pref: (0, pref[1])), spec, spec, spec],
        out_specs=[spec] * 4, out_shape=[jax.ShapeDtypeStruct(w.shape, F32)] * 4,
        args=(cact, dmod, w, m, v), prefetch=(place,), after=after)[0]


def _adamw_small(place, owns, gathered, wmv):
    nw = len(wmv)
    flat = [a for t in wmv for a in t]

    def body(pref, *refs):
        own_refs, all_refs, refs = refs[:5], refs[5:10], refs[10:]
        w_refs = refs[:3 * nw]
        loss_ref, dmod_ref = refs[3 * nw], refs[3 * nw + 1]
        o_refs = refs[3 * nw + 2:]
        j = pref[1]
        me = 2 * pref[1] + pref[0]

        def total(i):
            acc = None
            for b in range(N_DEV):
                blk = jnp.where(me == b, own_refs[i][...], all_refs[i][b])
                acc = blk if acc is None else acc + blk
            return acc

        vf, vd, vc, ddw, gwg = [total(i) for i in range(5)]
        loss_ref[...] = (0.5 / D_MODEL) * jnp.sum(vf[5:6, :], axis=1, keepdims=True)
        order = ((1, 1), (1, 2), (1, 0), (0, 2), (0, 3), (0, 1))
        for b in range(N_DEV):
            for q, (i, row) in enumerate(order):
                dmod_ref[b:b + 1, q * D_MODEL:(q + 1) * D_MODEL] = jnp.where(
                    me == b, own_refs[i][row:row + 1, :], all_refs[i][b, row:row + 1, :])
        dm = dmod_ref[...]
        g_bada = dm[0:1, :]
        for b in range(1, N_DEV):
            g_bada = g_bada + dm[b:b + 1, :]
        g_dww = jnp.zeros((32, POOL_G), F32)
        for k in range(N_CHIP):
            g_dww = g_dww + jnp.where(j == k, ddw[:, k * POOL_G:(k + 1) * POOL_G], 0.0)
        grads = [g_bada, vd[3:4, :], g_dww, vc[3:4, :], vc[1:2, :], vc[2:3, :], gwg, vc[0:1, :], vf[4:5, :],
                 vf[0:1, :]]
        for i, g in enumerate(grads):
            w_ref, m_ref, v_ref = w_refs[3 * i:3 * i + 3]
            d, nm, nv = _adamw_math(w_ref[...], g, m_ref[...], v_ref[...])
            o_refs[4 * i][...] = g
            o_refs[4 * i + 1][...] = d
            o_refs[4 * i + 2][...] = nm
            o_refs[4 * i + 3][...] = nv

    outs = [jax.ShapeDtypeStruct((1, 1), F32), jax.ShapeDtypeStruct((N_DEV, 6 * D_MODEL), F32)]
    for w, _, _ in wmv:
        outs += [jax.ShapeDtypeStruct(w.shape, F32)] * 4
    full = lambda a: pl.BlockSpec(a.shape, lambda i, pref: (0,) * a.ndim)
    args = list(owns) + list(gathered) + flat
    res, _ = _call(body, name="adamw_small", grid=(1,), in_specs=[full(a) for a in args],
                   out_specs=[full(o) for o in outs], out_shape=outs, args=args, prefetch=(place,))
    return res[0], res[1], [res[2 + 4 * i:6 + 4 * i] for i in range(nw)]


def kernel(x, c, w_ada, b_ada, g_norm1, w_in, dw_w, dw_b, conv_ln_g, conv_ln_b, w_conv_pw, w_pool_group, pool_scale, w_out, g_norm2, w_ffn_gate, w_ffn_up, w_ffn_down, g_final, loss_target, m_w_ada, m_b_ada, m_g_norm1, m_w_in, m_dw_w, m_dw_b, m_conv_ln_g, m_conv_ln_b, m_w_conv_pw, m_w_pool_group, m_pool_scale, m_w_out, m_g_norm2, m_w_ffn_gate, m_w_ffn_up, m_w_ffn_down, m_g_final, v_w_ada, v_b_ada, v_g_norm1, v_w_in, v_dw_w, v_dw_b, v_conv_ln_g, v_conv_ln_b, v_w_conv_pw, v_w_pool_group, v_pool_scale, v_w_out, v_g_norm2, v_w_ffn_gate, v_w_ffn_up, v_w_ffn_down, v_g_final):
    xi, yi, ci = _place()
    place = jnp.stack([ci, 2 * xi + yi]).astype(jnp.int32)
    n_ada = w_ada.shape[2]

    tr = lambda a: jnp.transpose(a[0])
    mixer_shards = [w_in[0], w_conv_pw[0], w_out[0]]
    ffn_shards = [tr(w_ffn_gate), tr(w_ffn_up), w_ffn_down[0]]
    slots = [lax.empty((N_CHIP,) + a.shape, MXU_DTYPE) for a in mixer_shards] + [lax.empty((N_CHIP,) + dw_w.shape[1:], F32)]

    (mod, cact, wg_b, (b_gate, b_up, b_down)), (win_g, wpw_g, wout_g, dww_g) = _ada_fwd(
        c, w_ada[0], b_ada.reshape(N_CHIP, n_ada), mixer_shards, ffn_shards, dw_w[0], w_pool_group[0],
        comm=_weights_gather(slots, [True, True, True, False]))
    dww_full = jnp.pad(jnp.concatenate([dww_g[k] for k in range(N_CHIP)], axis=1), ((0, 1), (0, 0)))
    w_pw = wpw_g.reshape(CONV_W, CONV_W)
    w_o = wout_g.reshape(D_MODEL, D_MODEL)
    xs, tgt, gf = x[0], loss_target[0], g_final.reshape(1, D_MODEL)
    s = xs.shape[0]
    fb = b_gate.shape[1]

    (x2, y, u, z, rstd, p, ycat), (wgate_g, wup_g, wdown_g) = _mixer_fwd(
        xs, mod, g_norm1, win_g, dww_full, dw_b, conv_ln_g, conv_ln_b, w_pw, wg_b, pool_scale, w_o,
        comm=_weights_gather([b_gate, b_up, b_down], [True, True, True]))
    dx2, h2, df, act, dgg, duu, vec_f = _ffn(x2, tgt, mod, g_norm2, gf, wgate_g, wup_g, wdown_g)

    whole = lambda w: pl.BlockSpec((s, w), lambda j: (0, 0))
    hid = pl.BlockSpec((1, s, fb), lambda j: (j, 0, 0))
    c_gate, _ = _dw("dw_gate", dgg, hid, h2, whole(D_MODEL), N_CHIP, fb, D_MODEL)
    c_up, (r_gate,) = _dw("dw_up", duu, hid, h2, whole(D_MODEL), N_CHIP, fb, D_MODEL, comm=_sibling_halves([c_gate]))
    ((pb_gate,), (own_gate,)), _ = _chip_partials("partials_gate", place, [c_gate], [r_gate])
    ex_gate = _exchange_parts([pb_gate])
    st_gate, tok_gate = _split_start("exchange_gate_start", [pb_gate], *ex_gate)
    c_down, (r_up,) = _dw("dw_down", act, hid, df, whole(D_MODEL), N_CHIP, fb, D_MODEL,
                          comm=_sibling_halves([c_up]), after=(tok_gate,))
    sib_down = _sibling_parts([c_down])
    st_sd, tok_sd = _split_start("sibling_down_start", [c_down], *sib_down)
    ((pb_up,), (own_up,)), _ = _chip_partials("partials_up", place, [c_up], [r_up], after=(tok_sd,))
    (c_down,), (r_down,) = _split_wait("sibling_down_wait", st_sd, 1, sib_down[1], sib_down[2], after=(pb_up,),
                                       with_sources=True)
    ((pb_down,), (own_down,)), _ = _chip_partials("partials_down", place, [c_down], [r_down])
    ex_ud = _exchange_parts([pb_up, pb_down])
    st_ud, tok_ud = _split_start("exchange_up_down_start", [pb_up, pb_down], *ex_ud)
    (gx, h1, du, dy, sw, dyc, g_wg, vec_d, vec_c, ddw), _ = _mixer_bwd(
        dx2, xs, y, u, z, rstd, p, mod, g_norm1, win_g, dww_full, conv_ln_g, conv_ln_b, w_pw, wg_b, pool_scale, w_o,
        after=(tok_ud,))

    small_own = [vec_f, vec_d, vec_c, ddw, g_wg]
    ex_small = _small_parts(small_own)
    st_small, tok_small = _split_start("small_grads_start", small_own, *ex_small, zeroed=True)
    c_out, c_pw, c_in = _dw_mixer(ycat, dy, sw, dyc, h1, du, after=(tok_small,))
    mix = [c_in, c_pw, c_out]
    sib_mix = _sibling_parts(mix)
    st_sm, tok_sm = _split_start("sibling_mix_start", mix, *sib_mix)
    (rc_gate,) = _split_wait("exchange_gate_wait", st_gate, 1, ex_gate[1], ex_gate[2], after=(tok_sm,))
    rc_up, rc_down = _split_wait("exchange_up_down_wait", st_ud, 2, ex_ud[1], ex_ud[2], after=(tok_sm, rc_gate))

    ffn_fulls, _ = _sum_partials("sum_ffn", place, [own_gate, own_up, own_down], [rc_gate, rc_up, rc_down])
    join_sems, join_copies = _join_parts(ffn_fulls)
    join_in_place = lambda bufs, lands, send, recv: join_copies(bufs, send, recv)
    st_jf, tok_jf = _split_start("join_ffn_start", ffn_fulls, [], join_sems, join_in_place)
    mix, (r_in, r_pw, r_out) = _split_wait("sibling_mix_wait", st_sm, len(mix), sib_mix[1], sib_mix[2],
                                           after=(tok_jf,), with_sources=True)
    (pbs_mix, owns_mix), _ = _chip_partials("partials_mix", place, mix, [r_in, r_pw, r_out])
    g_gate, g_up, g_down = _split_wait("join_ffn_wait", st_jf, len(ffn_fulls), join_sems, join_in_place,
                                       after=(pbs_mix[0],), in_place=True)

    pad_rows = lambda a: jnp.pad(a[0], ((0, 1), (0, 0)))
    row = lambda a: a.reshape(1, -1)
    small = [(b_ada, m_b_ada, v_b_ada), (g_norm1, m_g_norm1, v_g_norm1),
             (pad_rows(dw_w), pad_rows(m_dw_w), pad_rows(v_dw_w)), (dw_b, m_dw_b, v_dw_b),
             (conv_ln_g, m_conv_ln_g, v_conv_ln_g), (conv_ln_b, m_conv_ln_b, v_conv_ln_b),
             (w_pool_group[0], m_w_pool_group[0], v_w_pool_group[0]), (pool_scale, m_pool_scale, v_pool_scale),
             (g_norm2, m_g_norm2, v_g_norm2), (row(g_final), row(m_g_final), row(v_g_final))]
    lead = lambda outs: [a[None] for a in outs]

    lands, sem_shape, copies = _exchange_parts(pbs_mix)
    state, token = _split_start("exchange_mix_start", pbs_mix, lands, sem_shape, copies)
    u_gate, u_up, u_down = _adamw(
        "adamw_ffn", [(tr(w_ffn_gate), g_gate, tr(m_w_ffn_gate), tr(v_w_ffn_gate)),
                      (tr(w_ffn_up), g_up, tr(m_w_ffn_up), tr(v_w_ffn_up)),
                      (w_ffn_down[0], g_down, m_w_ffn_down[0], v_w_ffn_down[0])],
        steps=4, after=(token,))
    o_gate = [jnp.transpose(o) for o in [g_gate] + list(u_gate)]
    o_up = [jnp.transpose(o) for o in [g_up] + list(u_up)]
    o_down = [g_down] + list(u_down)
    small_own, small_all = _split_wait("small_grads_wait", st_small, len(small_own), ex_small[1], ex_small[2],
                                       after=(u_down[0],), with_sources=True)
    loss, dmod, small_out = _adamw_small(place, small_own, small_all, small)
    (o_bada, o_g1, o_dww, o_dwb, o_lng, o_lnb, o_wg, o_ps, o_g2, o_gf) = small_out
    o_dww = [a[:CONV_K] for a in o_dww]
    o_gf = [a.reshape(D_MODEL) for a in o_gf]
    o_ada = _adamw_ada(place, cact, dmod, w_ada[0], m_w_ada[0], v_w_ada[0], after=(token,))
    rc_mix = _split_wait("exchange_mix_wait", state, len(pbs_mix), sem_shape, copies, after=(o_ada[1], u_down[0]))
    mix_fulls, _ = _sum_partials("sum_mix", place, owns_mix, rc_mix)
    g_in, g_pw, g_out = _comm_only("join_mix", _join_halves(mix_fulls))
    u_in, u_pw, u_out = _adamw(
        "adamw_mix", [(w_in[0], g_in, m_w_in[0], v_w_in[0]), (w_conv_pw[0], g_pw, m_w_conv_pw[0], v_w_conv_pw[0]),
                      (w_out[0], g_out, m_w_out[0], v_w_out[0])], steps=4)
    o_in, o_pw, o_out = [g_in] + list(u_in), [g_pw] + list(u_pw), [g_out] + list(u_out)

    per_weight = [lead(o_ada), o_bada, o_g1, lead(o_in), lead(o_dww), o_dwb, o_lng, o_lnb, lead(o_pw), lead(o_wg),
                  o_ps, lead(o_out), o_g2, lead(o_gate), lead(o_up), lead(o_down), o_gf]
    result = [loss.reshape(()), gx[None]]
    for kind in range(4):
        result += [o[kind] for o in per_weight]
    return tuple(result)
```

```python
import functools

import jax
import jax.numpy as jnp
from jax import lax
from jax.experimental import pallas as pl
from jax.experimental.pallas import tpu as pltpu

F32 = jnp.float32
MXU_DTYPE = jnp.bfloat16
EPS = 1e-6

D_MODEL = 1024
CONV_W = 512
POOL_W = 512
CONV_K = 31
POOL_WINDOWS = (2, 4, 8, 16)
POOL_G = 128
IN_W = 2 * CONV_W + POOL_W
N_CHIP = 4
N_DEV = 8
CONV_HALO = 32
POOL_HALO = 16

ADAM_LR = 0.001
ADAM_B1 = 0.9
ADAM_B2 = 0.999
ADAM_EPS = 1e-08
ADAM_WD = 0.01
ADAM_STEP = 10

MESH = pl.DeviceIdType.MESH
ANY = pl.BlockSpec(memory_space=pl.ANY)
VMEM = pl.BlockSpec(memory_space=pltpu.VMEM)


def _dot(a, b):
    return jnp.dot(a.astype(MXU_DTYPE), b.astype(MXU_DTYPE), preferred_element_type=F32)


def _dot_nt(a, b):
    return lax.dot_general(a.astype(MXU_DTYPE), b.astype(MXU_DTYPE), (((1,), (1,)), ((), ())),
                           preferred_element_type=F32)


def _dot_tn(a, b):
    return lax.dot_general(a.astype(MXU_DTYPE), b.astype(MXU_DTYPE), (((0,), (0,)), ((), ())),
                           preferred_element_type=F32)


def _sigmoid(v):
    return 1.0 / (1.0 + jnp.exp(-v))


def _full(shape):
    n = len(shape)
    return pl.BlockSpec(shape, lambda *_: (0,) * n)


def _token_tile(s):
    return 256 if s % 256 == 0 else s


SUBLANES = 8


def _row_shifts(pad_ref, shifted_ref, rows):
    for r in range(1, SUBLANES):
        shifted_ref[r - 1] = pad_ref[r:r + rows, :]

    def window(i, n):
        r, base = i % SUBLANES, i - i % SUBLANES
        if r == 0:
            return pad_ref[base:base + n, :]
        return shifted_ref[r - 1, base:base + n, :]

    return window


def _place():
    return lax.axis_index("x"), lax.axis_index("y"), lax.axis_index("c")


def _flip(x, y, r):
    return ((1 - x) if r & 2 else x, (1 - y) if r & 1 else y)


def _remote(src, dst, send_sem, recv_sem, dev):
    return pltpu.make_async_remote_copy(src_ref=src, dst_ref=dst, send_sem=send_sem, recv_sem=recv_sem,
                                        device_id=dev, device_id_type=MESH)


class _Comm:
    def __init__(self, ins, outs, aliases, scratch, start, finish, mid=None):
        self.ins, self.outs, self.aliases, self.scratch = list(ins), list(outs), dict(aliases), list(scratch)
        self.start, self.finish = start, finish
        self.mid = mid


def _both(a, b):
    na, nao, nas = len(a.ins), len(a.outs), len(a.scratch)
    aliases = dict(a.aliases)
    aliases.update({na + i: nao + o for i, o in b.aliases.items()})

    def start(ins, outs, scr):
        a.start(ins[:na], outs[:nao], scr[:nas])
        b.start(ins[na:], outs[nao:], scr[nas:])

    def finish(ins, outs, scr):
        a.finish(ins[:na], outs[:nao], scr[:nas])
        b.finish(ins[na:], outs[nao:], scr[nas:])

    def mid(ins, outs, scr):
        if a.mid:
            a.mid(ins[:na], outs[:nao], scr[:nas])
        if b.mid:
            b.mid(ins[na:], outs[nao:], scr[nas:])

    return _Comm(a.ins + b.ins, a.outs + b.outs, aliases, a.scratch + b.scratch, start, finish,
                 mid if (a.mid or b.mid) else None)


def _call(body, *, name, grid, in_specs, out_specs, out_shape, args, scratch_shapes=(), prefetch=(), comm=None,
          body_starts=False, after=()):
    in_specs = list(in_specs) + [ANY] * len(after)
    args = list(args) + list(after)
    n_pre, n_in, n_out, n_scr = len(prefetch), len(in_specs), len(out_specs), len(scratch_shapes)
    n_body_in = n_in - len(after)
    c_ins = comm.ins if comm else []
    c_outs = comm.outs if comm else []
    c_scr = comm.scratch if comm else []
    last = grid[0] - 1

    def wrapped(*refs):
        pre, refs = refs[:n_pre], refs[n_pre:]
        ins, cin = refs[:n_body_in], refs[n_in:n_in + len(c_ins)]
        refs = refs[n_in + len(c_ins):]
        outs, cout = refs[:n_out], refs[n_out:n_out + len(c_outs)]
        refs = refs[n_out + len(c_outs):]
        scr, cscr = refs[:n_scr], refs[n_scr:]
        step = pl.program_id(0)
        if comm and not body_starts:
            @pl.when(step == 0)
            def _():
                comm.start(cin, cout, cscr)

        has_mid = comm is not None and comm.mid is not None
        mid_step = grid[0] // 2 if grid[0] >= 4 else None
        if has_mid and mid_step is not None:
            @pl.when(step == mid_step)
            def _():
                comm.mid(cin, cout, cscr)

        if body_starts:
            body(lambda: comm.start(cin, cout, cscr), lambda: comm.mid(cin, cout, cscr), cout,
                 *pre, *ins, *outs, *scr)
        else:
            body(*pre, *ins, *outs, *scr)
        if comm:
            @pl.when(step == last)
            def _():
                if has_mid and mid_step is None and not body_starts:
                    comm.mid(cin, cout, cscr)
                comm.finish(cin, cout, cscr)

    aliases = {n_pre + n_in + a: n_out + b for a, b in (comm.aliases if comm else {}).items()}
    res = pl.pallas_call(
        wrapped, name=name,
        grid_spec=pltpu.PrefetchScalarGridSpec(
            num_scalar_prefetch=n_pre, grid=grid, in_specs=list(in_specs) + [ANY] * len(c_ins),
            out_specs=list(out_specs) + [ANY] * len(c_outs), scratch_shapes=list(scratch_shapes) + list(c_scr)),
        out_shape=list(out_shape) + list(c_outs),
        input_output_aliases=aliases,
        compiler_params=pltpu.CompilerParams(dimension_semantics=("arbitrary",)),
    )(*prefetch, *args, *c_ins)
    return res[:n_out], res[n_out:]


def _comm_only(name, comm):
    return _call(lambda: None, name=name, grid=(1,), in_specs=[], out_specs=[], out_shape=[], args=[], comm=comm)[1]


def _weights_gather(bufs, split):
    n = len(bufs)

    def ctx(outs):
        x, y, cc = _place()
        chips = dict(me=2 * x + y, y=2 * x + (1 - y), x=2 * (1 - x) + y, d=2 * (1 - x) + (1 - y))
        devs = dict(y=(x, 1 - y, cc), x=(1 - x, y, cc), d=(1 - x, 1 - y, cc), s=(x, y, 1 - cc))

        def piece(a, kj, pc, q=None):
            if not split[a]:
                return outs[a].at[kj]
            h = bufs[a].shape[1] // 2
            if q is None:
                return outs[a].at[kj, pl.ds(pc * h, h), :]
            return outs[a].at[kj, pl.ds(pc * h + q * (h // 2), h // 2), :]

        return cc, chips, devs, piece

    def directs(a, outs, send, recv):
        cc, chips, devs, piece = ctx(outs)
        if not split[a]:
            whole = piece(a, chips["me"], cc)
            return [_remote(whole, whole, send.at[a, k], recv.at[a, k], devs[t]) for k, t in ((0, "y"), (2, "x"), (4, "d"))]
        q = lambda i: piece(a, chips["me"], cc, i)
        return [_remote(q(0), q(0), send.at[a, 0], recv.at[a, 0], devs["y"]),
                _remote(q(1), q(1), send.at[a, 3], recv.at[a, 3], devs["x"]),
                _remote(q(1), q(1), send.at[a, 1], recv.at[a, 1], devs["y"]),
                _remote(q(0), q(0), send.at[a, 2], recv.at[a, 2], devs["x"])]

    def landed(a, k, outs, send, recv):
        cc, chips, devs, piece = ctx(outs)
        if not split[a]:
            got = piece(a, chips[{0: "y", 2: "x", 4: "d"}[k]], cc)
        elif k < 6:
            got = piece(a, chips[("y", "y", "x", "x", "d", "d")[k]], cc, (0, 1, 0, 1, 0, 1)[k])
        else:
            got = piece(a, chips[("y", "x", "d")[k - 6]], 1 - cc)
        return _remote(got, got, send.at[a, k], recv.at[a, k], devs["s"])

    def passed_on(a, outs, send, recv):
        cc, chips, devs, piece = ctx(outs)
        from_y, from_x = piece(a, chips["y"], cc, 0), piece(a, chips["x"], cc, 1)
        return [_remote(from_y, from_y, send.at[a, 4], recv.at[a, 4], devs["x"]),
                _remote(from_x, from_x, send.at[a, 5], recv.at[a, 5], devs["y"])]

    def to_sibling(a, outs, send, recv, which=(0, 1, 2)):
        cc, chips, devs, piece = ctx(outs)
        halves = [piece(a, chips[("y", "x", "d")[i]], cc) for i in which]
        return [_remote(hf, hf, send.at[a, 6 + i], recv.at[a, 6 + i], devs["s"]) for i, hf in zip(which, halves)]

    def start(ins, outs, scr):
        send, recv = scr
        per_item = [directs(a, outs, send, recv) for a in range(n)]
        for rank in range(4):
            for cps in per_item:
                if rank < len(cps):
                    cps[rank].start()

    def mid(ins, outs, scr):
        send, recv = scr
        for a in range(n):
            if split[a]:
                fy, fx = passed_on(a, outs, send, recv)
                landed(a, 0, outs, send, recv).wait_recv()
                fy.start()
                landed(a, 3, outs, send, recv).wait_recv()
                fx.start()

    def finish(ins, outs, scr):
        send, recv = scr
        for a in range(n):
            if split[a]:
                to_y, to_x = to_sibling(a, outs, send, recv, which=(0, 1))
                landed(a, 1, outs, send, recv).wait_recv()
                to_y.start()
                landed(a, 2, outs, send, recv).wait_recv()
                to_x.start()
        for a in range(n):
            if split[a]:
                for k in (4, 5):
                    landed(a, k, outs, send, recv).wait_recv()
                to_sibling(a, outs, send, recv, which=(2,))[0].start()
            else:
                for k in (0, 2, 4):
                    landed(a, k, outs, send, recv).wait_recv()
        for a in range(n):
            if split[a]:
                for k in (6, 7, 8):
                    landed(a, k, outs, send, recv).wait_recv()
            cps = directs(a, outs, send, recv)
            if split[a]:
                cps += passed_on(a, outs, send, recv) + to_sibling(a, outs, send, recv)
            for cp in cps:
                cp.wait_send()

    return _Comm(bufs, [jax.ShapeDtypeStruct(b.shape, b.dtype) for b in bufs], {i: i for i in range(n)},
                 [pltpu.SemaphoreType.DMA((n, 9)), pltpu.SemaphoreType.DMA((n, 9))], start, finish, mid)


HBM =pl.BlockSpec(memory_space=pltpu.HBM)
SEM = pl.BlockSpec(memory_space=pltpu.SEMAPHORE)
DATAFLOW = pltpu.SideEffectType.DATAFLOW_SIDE_EFFECTING


class _SemGrid:
    def __init__(self, refs, cols):
        self.refs, self.cols = refs, cols

    @property
    def at(self):
        return self

    def __getitem__(self, idx):
        return self.refs[idx[0] * self.cols + idx[1]]


def _split_start(name, srcs, lands, sem_shape, copies, zeroed=False):
    n, k = len(srcs), len(lands)
    ns = sem_shape[0] * sem_shape[1]

    def body(*refs):
        src_refs, land_refs = refs[:n], refs[n:n + k]
        send = _SemGrid(refs[n + k:n + k + ns], sem_shape[1])
        recv = _SemGrid(refs[n + k + ns:n + k + 2 * ns], sem_shape[1])
        token = refs[-1]
        for cp in copies(src_refs, land_refs, send, recv):
            cp.start()
        token[...] = jnp.zeros(token.shape, F32)

    hbm = lambda a: pltpu.with_memory_space_constraint(a, pltpu.HBM)
    zones = [jnp.zeros(l.shape, l.dtype) if zeroed else lax.empty(l.shape, l.dtype) for l in lands]
    out = pl.pallas_call(
        body, name=name,
        out_shape=[pltpu.SemaphoreType.DMA(())] * (2 * ns)
        + [pltpu.HBM(a.shape, a.dtype) for a in list(srcs) + list(lands)] + [jax.ShapeDtypeStruct((8, 128), F32)],
        in_specs=[HBM] * (n + k), out_specs=[SEM] * (2 * ns) + [HBM] * (n + k) + [VMEM],
        input_output_aliases={i: 2 * ns + i for i in range(n + k)},
        compiler_params=pltpu.CompilerParams(has_side_effects=DATAFLOW),
    )(*[hbm(a) for a in srcs], *[hbm(z) for z in zones])
    return out[:-1], out[-1]


def _split_wait(name, state, n, sem_shape, copies, after, in_place=False, with_sources=False):
    ns = sem_shape[0] * sem_shape[1]
    sems, bufs = state[:2 * ns], state[2 * ns:]
    k = len(bufs) - n

    def body(*refs):
        src_refs, land_refs = refs[:n], refs[n:n + k]
        send = _SemGrid(refs[n + k:n + k + ns], sem_shape[1])
        recv = _SemGrid(refs[n + k + ns:n + k + 2 * ns], sem_shape[1])
        cps = copies(src_refs, land_refs, send, recv)
        for cp in cps:
            cp.wait_send()
        for cp in cps:
            cp.wait_recv()

    out = pl.pallas_call(
        body, name=name,
        out_shape=[pltpu.HBM(a.shape, a.dtype) for a in bufs],
        in_specs=[HBM] * (n + k) + [SEM] * (2 * ns) + [ANY] * len(after), out_specs=[HBM] * (n + k),
        input_output_aliases={i: i for i in range(n + k)},
        compiler_params=pltpu.CompilerParams(has_side_effects=DATAFLOW),
    )(*bufs, *sems, *after)
    if with_sources:
        return out[:n], out[n:]
    return out[:n] if in_place else out[n:]


def _direct_phases(copies):
    def start(ins, outs, scr):
        for cp in copies(ins, outs, *scr):
            cp.start()

    def finish(ins, outs, scr):
        cps = copies(ins, outs, *scr)
        for cp in cps:
            cp.wait_recv()
        for cp in cps:
            cp.wait_send()

    return start, finish


def _sibling_parts(gs):
    n = len(gs)

    def copies(ins, outs, send, recv):
        x, y, cc = _place()
        cps = []
        for a in range(n):
            h = gs[a].shape[1] // 2
            cps.append(_remote(ins[a].at[:, pl.ds((1 - cc) * h, h), :], outs[a], send.at[a, 0], recv.at[a, 0],
                               (x, y, 1 - cc)))
        return cps

    return [jax.ShapeDtypeStruct((N_CHIP, g.shape[1] // 2, g.shape[2]), F32) for g in gs], (n, 1), copies


def _sibling_halves(gs):
    lands, sem_shape, copies = _sibling_parts(gs)
    start, finish = _direct_phases(copies)
    return _Comm(gs, lands, {}, [pltpu.SemaphoreType.DMA(sem_shape), pltpu.SemaphoreType.DMA(sem_shape)],
                 start, finish)


def _exchange_parts(pbs):
    n = len(pbs)

    def copies(ins, outs, send, recv):
        x, y, cc = _place()
        cps = []
        for a in range(n):
            for r in range(1, N_CHIP):
                kx, ky = _flip(x, y, r)
                cps.append(_remote(ins[a].at[2 * kx + ky], outs[a].at[r - 1], send.at[a, r - 1], recv.at[a, r - 1],
                                   (kx, ky, cc)))
        return cps

    lands = [jax.ShapeDtypeStruct((N_CHIP - 1,) + p.shape[1:], p.dtype) for p in pbs]
    return lands, (n, N_CHIP - 1), copies


def _small_parts(arrs):
    n = len(arrs)

    def copies(ins, outs, send, recv):
        x, y, cc = _place()
        b = 4 * x + 2 * y + cc
        cps = []
        for a in range(n):
            for r in range(1, N_DEV):
                dev = ((1 - x) if r & 4 else x, (1 - y) if r & 2 else y, (1 - cc) if r & 1 else cc)
                cps.append(_remote(ins[a], outs[a].at[b], send.at[a, r - 1], recv.at[a, r - 1], dev))
        return cps

    lands = [jax.ShapeDtypeStruct((N_DEV,) + a.shape, a.dtype) for a in arrs]
    return lands, (n, N_DEV - 1), copies


def _exchange_partials(pbs):
    lands, sem_shape, copies = _exchange_parts(pbs)
    start, finish = _direct_phases(copies)
    return _Comm(pbs, lands, {}, [pltpu.SemaphoreType.DMA(sem_shape), pltpu.SemaphoreType.DMA(sem_shape)],
                 start, finish)


def _join_parts(fulls):
    n = len(fulls)

    def copies(bufs, send, recv):
        x, y, cc = _place()
        cps = []
        for a in range(n):
            h = fulls[a].shape[0] // 2
            mine = bufs[a].at[pl.ds(cc * h, h), :]
            cps.append(_remote(mine, mine, send.at[a, 0], recv.at[a, 0], (x, y, 1 - cc)))
        return cps

    return (n, 1), copies


def _join_halves(fulls):
    sem_shape, copies = _join_parts(fulls)
    start, finish = _direct_phases(lambda ins, outs, send, recv: copies(outs, send, recv))
    return _Comm(fulls, [jax.ShapeDtypeStruct(f.shape, F32) for f in fulls], {i: i for i in range(len(fulls))},
                 [pltpu.SemaphoreType.DMA(sem_shape), pltpu.SemaphoreType.DMA(sem_shape)], start, finish)


def _mixer_fwd(x, mod, g1, w_in, dww, dwb, lng, lnb, w_pw, wg, pscale, w_out, comm=None):
    s = x.shape[0]
    ts = _token_tile(s)
    nt = s // ts

    def body(x_ref, mod_ref, g1_ref, win_ref, dww_ref, dwb_ref, lng_ref, lnb_ref, wpw_ref, wg_ref, ps_ref,
             wout_ref, x2_ref, y_ref, u_ref, z_ref, rstd_ref, p_ref, ycat_ref, gpad, vpad, gshift):
        i = pl.program_id(0)

        @pl.when(i == 0)
        def _():
            gpad[0:CONV_HALO, :] = jnp.zeros((CONV_HALO, CONV_W), F32)
            vpad[0:POOL_HALO, :] = jnp.zeros((POOL_HALO, POOL_W), F32)

        xt = x_ref[...]
        sh1 = mod_ref[0:1, :]
        sc1 = mod_ref[1:2, :]
        gt1 = mod_ref[2:3, :]
        r1 = lax.rsqrt(jnp.mean(xt * xt, axis=-1, keepdims=True) + EPS)
        h1 = (xt * r1 * g1_ref[...]) * (1.0 + sc1) + sh1
        h1b = h1.astype(MXU_DTYPE)
        u = jnp.concatenate([_dot(h1b, win_ref[j]) for j in range(N_CHIP)], axis=1)
        u_ref[...] = u
        a = u[:, :CONV_W]
        g = u[:, CONV_W:2 * CONV_W]
        v = u[:, 2 * CONV_W:]

        gpad[CONV_HALO:CONV_HALO + ts, :] = a * _sigmoid(g)
        window = _row_shifts(gpad, gshift, ts + CONV_HALO - SUBLANES)
        cv = jnp.broadcast_to(dwb_ref[...], (ts, CONV_W))
        off = CONV_HALO - (CONV_K - 1)
        for k in range(CONV_K):
            cv = cv + dww_ref[k:k + 1, :] * window(off + k, ts)
        gpad[0:CONV_HALO, :] = gpad[ts:ts + CONV_HALO, :]

        mu = jnp.mean(cv, axis=-1, keepdims=True)
        cc = cv - mu
        rstd = lax.rsqrt(jnp.mean(cc * cc, axis=-1, keepdims=True) + EPS)
        z = cc * rstd
        z_ref[...] = z
        rstd_ref[...] = rstd
        ln = z * lng_ref[...] + lnb_ref[...]
        sw = ln * _sigmoid(ln)
        yconv = _dot(sw, wpw_ref[...])

        vpad[POOL_HALO:POOL_HALO + ts, :] = v
        t = i * ts + lax.broadcasted_iota(jnp.int32, (ts, 1), 0)
        ps, ypool = [], []
        for gi, w in enumerate(POOL_WINDOWS):
            cols = slice(gi * POOL_G, (gi + 1) * POOL_G)
            acc = vpad[POOL_HALO:POOL_HALO + ts, cols]
            for d in range(1, w):
                acc = acc + vpad[POOL_HALO - d:POOL_HALO - d + ts, cols]
            cnt = jnp.minimum(t + 1, w).astype(F32)
            pg = (acc / cnt - v[:, cols]).astype(MXU_DTYPE)
            ps.append(pg)
            ypool.append(_dot(pg, wg_ref[gi]))
        vpad[0:POOL_HALO, :] = vpad[ts:ts + POOL_HALO, :]
        p_ref[...] = jnp.concatenate(ps, axis=1)
        ypool = jnp.concatenate(ypool, axis=1) * ps_ref[...]

        ycat = jnp.concatenate([yconv, ypool], axis=1).astype(MXU_DTYPE)
        ycat_ref[...] = ycat
        y = _dot(ycat, wout_ref[...])
        y_ref[...] = y
        x2_ref[...] = xt + gt1 * y

    tile = lambda w: pl.BlockSpec((ts, w), lambda i: (i, 0))
    return _call(
        body, name="mixer_fwd", grid=(nt,),
        in_specs=[tile(D_MODEL), _full(mod.shape), _full(g1.shape), _full(w_in.shape), _full(dww.shape),
                  _full(dwb.shape), _full(lng.shape), _full(lnb.shape), _full(w_pw.shape), _full(wg.shape),
                  _full(pscale.shape), _full(w_out.shape)],
        out_specs=[tile(D_MODEL), tile(D_MODEL), tile(IN_W), tile(CONV_W), tile(1), tile(POOL_W), tile(D_MODEL)],
        out_shape=[jax.ShapeDtypeStruct((s, D_MODEL), F32), jax.ShapeDtypeStruct((s, D_MODEL), F32),
                   jax.ShapeDtypeStruct((s, IN_W), F32), jax.ShapeDtypeStruct((s, CONV_W), F32),
                   jax.ShapeDtypeStruct((s, 1), F32), jax.ShapeDtypeStruct((s, POOL_W), MXU_DTYPE),
                   jax.ShapeDtypeStruct((s, D_MODEL), MXU_DTYPE)],
        scratch_shapes=[pltpu.VMEM((ts + CONV_HALO, CONV_W), F32), pltpu.VMEM((ts + POOL_HALO, POOL_W), F32),
                        pltpu.VMEM((SUBLANES - 1, ts + CONV_HALO - SUBLANES, CONV_W), F32)],
        args=(x, mod, g1, w_in, dww, dwb, lng, lnb, w_pw, wg, pscale, w_out), comm=comm)


def _ffn(x2, tgt, mod, g2, gf, w_gate, w_up, w_down):
    s = x2.shape[0]
    ts = _token_tile(s)
    nt = s // ts
    fb = w_gate.shape[1]

    def body(x2_ref, tgt_ref, mod_ref, g2_ref, gf_ref, wgt_ref, wup_ref, wdn_ref,
             dx2_ref, h2_ref, df_ref, act_ref, dgg_ref, duu_ref, vec_ref, gg_s, uu_s):
        i = pl.program_id(0)

        @pl.when(i == 0)
        def _():
            vec_ref[...] = jnp.zeros(vec_ref.shape, F32)

        x2t = x2_ref[...]
        sh2 = mod_ref[3:4, :]
        sc2 = mod_ref[4:5, :]
        gt2 = mod_ref[5:6, :]
        g2v = g2_ref[...]
        gfv = gf_ref[...]
        r2 = lax.rsqrt(jnp.mean(x2t * x2t, axis=-1, keepdims=True) + EPS)
        xh2 = x2t * r2
        n2 = xh2 * g2v
        h2b = (n2 * (1.0 + sc2) + sh2).astype(MXU_DTYPE)
        h2_ref[...] = h2b
        f = jnp.zeros((ts, D_MODEL), F32)
        for j in range(N_CHIP):
            gg = _dot_nt(h2b, wgt_ref[j])
            uu = _dot_nt(h2b, wup_ref[j])
            gg_s[j] = gg
            uu_s[j] = uu
            actb = (gg * _sigmoid(gg) * uu).astype(MXU_DTYPE)
            act_ref[j] = actb
            f = f + _dot(actb, wdn_ref[j])
        x3 = x2t + gt2 * f
        r3 = lax.rsqrt(jnp.mean(x3 * x3, axis=-1, keepdims=True) + EPS)
        xh3 = x3 * r3
        diff = xh3 * gfv - tgt_ref[...]
        dout = diff * (1.0 / D_MODEL)
        dn3 = dout * gfv
        dx3 = r3 * (dn3 - xh3 * jnp.mean(dn3 * xh3, axis=-1, keepdims=True))
        dfb = (dx3 * gt2).astype(MXU_DTYPE)
        df_ref[...] = dfb
        dh2 = jnp.zeros((ts, D_MODEL), F32)
        for j in range(N_CHIP):
            dact = _dot_nt(dfb, wdn_ref[j])
            gg = gg_s[j]
            uu = uu_s[j]
            sg = _sigmoid(gg)
            duu = (dact * (gg * sg)).astype(MXU_DTYPE)
            dgg = (dact * uu * (sg * (1.0 + gg * (1.0 - sg)))).astype(MXU_DTYPE)
            duu_ref[j] = duu
            dgg_ref[j] = dgg
            dh2 = dh2 + _dot(dgg, wgt_ref[j]) + _dot(duu, wup_ref[j])
        dn2 = dh2 * (1.0 + sc2)
        dxh2 = dn2 * g2v
        dx2_ref[...] = dx3 + r2 * (dxh2 - xh2 * jnp.mean(dxh2 * xh2, axis=-1, keepdims=True))

        col = lambda a: jnp.sum(a, axis=0, keepdims=True)
        vec_ref[0:1, :] += col(dout * xh3)
        vec_ref[1:2, :] += col(dx3 * f)
        vec_ref[2:3, :] += col(dh2)
        vec_ref[3:4, :] += col(dh2 * n2)
        vec_ref[4:5, :] += col(dn2 * xh2)
        vec_ref[5:6, :] += col(diff * diff)

    tile = lambda w: pl.BlockSpec((ts, w), lambda i: (i, 0))
    tile3 = pl.BlockSpec((N_CHIP, ts, fb), lambda i: (0, i, 0))
    once = lambda a: pl.BlockSpec(a.shape, lambda i: (0,) * a.ndim, pipeline_mode=pl.Buffered(1))
    hid = jax.ShapeDtypeStruct((N_CHIP, s, fb), MXU_DTYPE)
    return pl.pallas_call(
        body, name="ffn", grid=(nt,),
        in_specs=[tile(D_MODEL), tile(D_MODEL), _full(mod.shape), _full(g2.shape), _full(gf.shape),
                  once(w_gate), once(w_up), once(w_down)],
        out_specs=[tile(D_MODEL), tile(D_MODEL), tile(D_MODEL), tile3, tile3, tile3, _full((8, D_MODEL))],
        out_shape=[jax.ShapeDtypeStruct((s, D_MODEL), F32), jax.ShapeDtypeStruct((s, D_MODEL), MXU_DTYPE),
                   jax.ShapeDtypeStruct((s, D_MODEL), MXU_DTYPE), hid, hid, hid,
                   jax.ShapeDtypeStruct((8, D_MODEL), F32)],
        scratch_shapes=[pltpu.VMEM((N_CHIP, ts, fb), F32), pltpu.VMEM((N_CHIP, ts, fb), F32)],
        compiler_params=pltpu.CompilerParams(dimension_semantics=("arbitrary",)),
    )(x2, tgt, mod, g2, gf, w_gate, w_up, w_down)


def _mixer_bwd(dx2, x, y, u, z, rstd, p, mod, g1, w_in, dww, lng, lnb, w_pw, wg, pscale, w_out, comm=None, after=()):
    s = x.shape[0]
    ts = _token_tile(s)
    nt = s // ts

    def body(dx2_ref, x_ref, y_ref, u_ref, z_ref, rstd_ref, p_ref, mod_ref, g1_ref, win_ref, dww_ref, lng_ref,
             lnb_ref, wpw_ref, wg_ref, ps_ref, wout_ref,
             gx_ref, h1_ref, du_ref, dy_ref, gpw_ref, gwg_ref, vd_ref, vc_ref, ddw_ref, dcpad, dppad,
             dshift):
        i = pl.program_id(0)
        tix = nt - 1 - i

        @pl.when(i == 0)
        def _():
            gpw_ref[...] = jnp.zeros(gpw_ref.shape, F32)
            gwg_ref[...] = jnp.zeros(gwg_ref.shape, F32)
            vd_ref[...] = jnp.zeros(vd_ref.shape, F32)
            vc_ref[...] = jnp.zeros(vc_ref.shape, F32)
            ddw_ref[...] = jnp.zeros(ddw_ref.shape, F32)
            dcpad[ts:ts + CONV_HALO, :] = jnp.zeros((CONV_HALO, CONV_W), F32)
            dppad[ts:ts + POOL_HALO, :] = jnp.zeros((POOL_HALO, POOL_W), F32)

        col = lambda a: jnp.sum(a, axis=0, keepdims=True)
        sh1 = mod_ref[0:1, :]
        sc1 = mod_ref[1:2, :]
        gt1 = mod_ref[2:3, :]
        dx2t = dx2_ref[...]
        vd_ref[0:1, :] += col(dx2t * y_ref[...])
        dyb = (dx2t * gt1).astype(MXU_DTYPE)
        dy_ref[...] = dyb
        dycat = _dot_nt(dyb, wout_ref[...])
        dyconv = dycat[:, :CONV_W]
        dypool = dycat[:, CONV_W:]

        pt = p_ref[...]
        t = tix * ts + lax.broadcasted_iota(jnp.int32, (ts, 1), 0)
        psc = ps_ref[...]
        dypb = (dypool * psc).astype(MXU_DTYPE)
        dps, ypre = [], []
        for gi, w in enumerate(POOL_WINDOWS):
            cols = slice(gi * POOL_G, (gi + 1) * POOL_G)
            gwg_ref[gi] += _dot_tn(pt[:, cols], dypb[:, cols])
            ypre.append(_dot(pt[:, cols], wg_ref[gi]))
            dpg = _dot_nt(dypb[:, cols], wg_ref[gi])
            dps.append(dpg)
            cnt = jnp.minimum(t + 1, w).astype(F32)
            dppad[0:ts, cols] = dpg / cnt
        vc_ref[0:1, :] += col(dypool * jnp.concatenate(ypre, axis=1))
        dvs = []
        for gi, w in enumerate(POOL_WINDOWS):
            cols = slice(gi * POOL_G, (gi + 1) * POOL_G)
            acc = dppad[0:ts, cols]
            for d in range(1, w):
                acc = acc + dppad[d:d + ts, cols]
            dvs.append(acc - dps[gi])
        dv = jnp.concatenate(dvs, axis=1)
        dppad[ts:ts + POOL_HALO, :] = dppad[0:POOL_HALO, :]

        zt = z_ref[...]
        lngv = lng_ref[...]
        ln = zt * lngv + lnb_ref[...]
        sg = _sigmoid(ln)
        swb = (ln * sg).astype(MXU_DTYPE)
        dycb = dyconv.astype(MXU_DTYPE)
        gpw_ref[...] += _dot_tn(swb, dycb)
        dln = _dot_nt(dycb, wpw_ref[...]) * (sg * (1.0 + ln * (1.0 - sg)))
        vc_ref[1:2, :] += col(dln * zt)
        vc_ref[2:3, :] += col(dln)
        dz = dln * lngv
        dcv = rstd_ref[...] * (dz - jnp.mean(dz, axis=-1, keepdims=True)
                               - zt * jnp.mean(dz * zt, axis=-1, keepdims=True))
        vc_ref[3:4, :] += col(dcv)
        dcpad[0:ts, :] = dcv
        ut = u_ref[...]
        a = ut[:, :CONV_W]
        g = ut[:, CONV_W:2 * CONV_W]
        sgg = _sigmoid(g)
        glu = a * sgg
        window = _row_shifts(dcpad, dshift, ts + CONV_HALO - SUBLANES)
        dglu = jnp.zeros((ts, CONV_W), F32)
        for k in range(CONV_K):
            sh = window(CONV_K - 1 - k, ts)
            dglu = dglu + dww_ref[k:k + 1, :] * sh
            ddw_ref[k:k + 1, :] += col(glu * sh)
        dcpad[ts:ts + CONV_HALO, :] = dcpad[0:CONV_HALO, :]
        da = dglu * sgg
        dg = dglu * a * sgg * (1.0 - sgg)
        dub = jnp.concatenate([da, dg, dv], axis=1).astype(MXU_DTYPE)
        du_ref[...] = dub
        cw = IN_W // N_CHIP
        dh1 = jnp.zeros((ts, D_MODEL), F32)
        for j in range(N_CHIP):
            dh1 = dh1 + _dot_nt(dub[:, j * cw:(j + 1) * cw], win_ref[j])

        xt = x_ref[...]
        g1v = g1_ref[...]
        r1 = lax.rsqrt(jnp.mean(xt * xt, axis=-1, keepdims=True) + EPS)
        xh1 = xt * r1
        n1 = xh1 * g1v
        h1_ref[...] = (n1 * (1.0 + sc1) + sh1).astype(MXU_DTYPE)
        vd_ref[1:2, :] += col(dh1)
        vd_ref[2:3, :] += col(dh1 * n1)
        dn1 = dh1 * (1.0 + sc1)
        vd_ref[3:4, :] += col(dn1 * xh1)
        dxh = dn1 * g1v
        gx_ref[...] = dx2t + r1 * (dxh - xh1 * jnp.mean(dxh * xh1, axis=-1, keepdims=True))

    tile = lambda w: pl.BlockSpec((ts, w), lambda i: (nt - 1 - i, 0))
    bf = lambda w: jax.ShapeDtypeStruct((s, w), MXU_DTYPE)
    return _call(
        body, name="mixer_bwd", grid=(nt,),
        in_specs=[tile(D_MODEL), tile(D_MODEL), tile(D_MODEL), tile(IN_W), tile(CONV_W), tile(1), tile(POOL_W),
                  _full(mod.shape), _full(g1.shape), _full(w_in.shape), _full(dww.shape), _full(lng.shape),
                  _full(lnb.shape), _full(w_pw.shape), _full(wg.shape), _full(pscale.shape), _full(w_out.shape)],
        out_specs=[tile(D_MODEL), tile(D_MODEL), tile(IN_W), tile(D_MODEL), _full((CONV_W, CONV_W)),
                   _full(wg.shape), _full((8, D_MODEL)), _full((8, CONV_W)), _full((32, CONV_W))],
        out_shape=[jax.ShapeDtypeStruct((s, D_MODEL), F32), bf(D_MODEL), bf(IN_W), bf(D_MODEL),
                   jax.ShapeDtypeStruct((CONV_W, CONV_W), F32),
                   jax.ShapeDtypeStruct(wg.shape, F32), jax.ShapeDtypeStruct((8, D_MODEL), F32),
                   jax.ShapeDtypeStruct((8, CONV_W), F32), jax.ShapeDtypeStruct((32, CONV_W), F32)],
        scratch_shapes=[pltpu.VMEM((ts + CONV_HALO, CONV_W), F32), pltpu.VMEM((ts + POOL_HALO, POOL_W), F32),
                        pltpu.VMEM((SUBLANES - 1, ts + CONV_HALO - SUBLANES, CONV_W), F32)],
        args=(dx2, x, y, u, z, rstd, p, mod, g1, w_in, dww, lng, lnb, w_pw, wg, pscale, w_out), comm=comm,
        after=after)


def _dw(name, a, a_spec, b, b_spec, nb, mb, nbk, comm=None, after=()):
    def body(a_ref, b_ref, o_ref):
        av = a_ref[...]
        bv = b_ref[...]
        av = av.reshape(av.shape[-2:])
        bv = bv.reshape(bv.shape[-2:])
        o_ref[0] = _dot_tn(av, bv)

    (out,), rest = _call(
        body, name=name, grid=(nb,), in_specs=[a_spec, b_spec],
        out_specs=[pl.BlockSpec((1, mb, nbk), lambda j: (j, 0, 0))],
        out_shape=[jax.ShapeDtypeStruct((nb, mb, nbk), F32)], args=(a, b), comm=comm, after=after)
    return out, rest


def _dw_mixer(ycat, dy, h1, du, after=()):
    s = ycat.shape[0]

    def body(ycat_ref, dy_ref, h1_ref, du_ref, out_ref, in_ref):
        out_ref[0] = _dot_tn(ycat_ref[...], dy_ref[...])
        in_ref[0] = _dot_tn(h1_ref[...], du_ref[...])

    whole = lambda w: pl.BlockSpec((s, w), lambda j: (0, 0))
    cols = lambda w: pl.BlockSpec((s, w), lambda j: (0, j))
    blk = lambda m, n: pl.BlockSpec((1, m, n), lambda j: (j, 0, 0))
    shapes = [(D_MODEL // N_CHIP, D_MODEL), (D_MODEL, IN_W // N_CHIP)]
    res, _ = _call(
        body, name="dw_mixer", grid=(N_CHIP,),
        in_specs=[cols(D_MODEL // N_CHIP), whole(D_MODEL), whole(D_MODEL), cols(IN_W // N_CHIP)],
        out_specs=[blk(m, n) for m, n in shapes],
        out_shape=[jax.ShapeDtypeStruct((N_CHIP, m, n), F32) for m, n in shapes],
        args=(ycat, dy, h1, du), after=after)
    return res


def _ada_fwd(c, w_ada, b4, first, later, dww, wg, comm):
    nc = w_ada.shape[1]
    nf, nl = len(first), len(later)
    shards = list(first) + list(later)

    def body(start_comm, mid_comm, gathered, c_ref, w_ref, b4_ref, *refs):
        shard_refs, refs = refs[:nf + nl], refs[nf + nl:]
        dww_ref, wg_ref, mod_ref, cact_ref, wgb_ref = refs[:5]
        later_refs, refs = refs[5:5 + nl], refs[5 + nl:]
        call, part, parts = refs[:3]
        stages, refs = refs[3:3 + nf + nl], refs[3 + nf + nl:]
        send1, recv1, send2, recv2, lsem = refs
        x, y, cc = _place()
        b = 4 * x + 2 * y + cc
        j = 2 * x + y

        def slot_copies(lo, hi):
            cps = []
            for a in range(lo, hi):
                dst = gathered[a] if a < nf else later_refs[a - nf]
                cps.append(pltpu.make_async_copy(stages[a], dst.at[j], lsem.at[a]))
            return cps

        call[b] = c_ref[...]
        sends = []
        for r in range(1, N_DEV):
            dev = ((1 - x) if r & 4 else x, (1 - y) if r & 2 else y, (1 - cc) if r & 1 else cc)
            cp = _remote(call.at[b], call.at[b], send1.at[r - 1], recv1.at[r - 1], dev)
            cp.start()
            sends.append(cp)
        for a in range(nf):
            stages[a][...] = shard_refs[a][...].astype(MXU_DTYPE)
        dww_copy = pltpu.make_async_copy(dww_ref, gathered[nf].at[j], lsem.at[nf + nl])
        dww_copy.start()
        for cp in slot_copies(0, nf):
            cp.start()
        for r in range(1, N_DEV):
            src_b = lax.bitwise_xor(b, r)
            _remote(call.at[src_b], call.at[src_b], send1.at[r - 1], recv1.at[r - 1], (x, y, cc)).wait_recv()
        for cp in sends:
            cp.wait_send()
        for cp in slot_copies(0, nf):
            cp.wait()
        dww_copy.wait()
        start_comm()
        for i in range(N_DEV):
            ci = call[i]
            cact_ref[i:i + 1, :] = ci * _sigmoid(ci)
        part[...] = jnp.dot(cact_ref[...], w_ref[...], preferred_element_type=F32, precision=lax.Precision.HIGHEST)
        sends = []
        for r in range(1, N_CHIP):
            kx, ky = _flip(x, y, r)
            cp = _remote(part, parts.at[j], send2.at[r - 1], recv2.at[r - 1], (kx, ky, cc))
            cp.start()
            sends.append(cp)
        parts[j] = part[...]
        for a in range(nf, nf + nl):
            stages[a][...] = shard_refs[a][...].astype(MXU_DTYPE)
        for cp in slot_copies(nf, nf + nl):
            cp.start()
        wgb_ref[...] = wg_ref[...].astype(MXU_DTYPE)
        mid_comm()
        for r in range(1, N_CHIP):
            kx, ky = _flip(x, y, r)
            kj = 2 * kx + ky
            _remote(part, parts.at[kj], send2.at[r - 1], recv2.at[r - 1], (x, y, cc)).wait_recv()
        for cp in sends:
            cp.wait_send()
        mine = lax.broadcasted_iota(jnp.int32, (N_DEV, 1), 0) == b
        for k in range(N_CHIP):
            row = jnp.sum(jnp.where(mine, parts[k], 0.0), axis=0, keepdims=True) + b4_ref[k:k + 1, :]
            lo = k * nc
            while lo < (k + 1) * nc:
                q, at = lo // D_MODEL, lo % D_MODEL
                n = min(D_MODEL - at, (k + 1) * nc - lo)
                mod_ref[q:q + 1, at:at + n] = row[:, lo - k * nc:lo - k * nc + n]
                lo += n
        for cp in slot_copies(nf, nf + nl):
            cp.wait()

    res, rest = _call(
        body, name="ada_fwd", grid=(1,),
        in_specs=[VMEM] * (5 + nf + nl), out_specs=[VMEM, VMEM, VMEM] + [ANY] * nl,
        out_shape=[jax.ShapeDtypeStruct((N_CHIP * nc // D_MODEL, D_MODEL), F32),
                   jax.ShapeDtypeStruct((N_DEV, D_MODEL), F32),
                   jax.ShapeDtypeStruct(wg.shape, MXU_DTYPE)]
        + [jax.ShapeDtypeStruct((N_CHIP,) + a.shape, MXU_DTYPE) for a in later],
        scratch_shapes=[pltpu.VMEM((N_DEV, 1, D_MODEL), F32), pltpu.VMEM((N_DEV, nc), F32),
                        pltpu.VMEM((N_CHIP, N_DEV, nc), F32)]
        + [pltpu.VMEM(a.shape, MXU_DTYPE) for a in shards]
        + [pltpu.SemaphoreType.DMA((N_DEV - 1,)), pltpu.SemaphoreType.DMA((N_DEV - 1,)),
           pltpu.SemaphoreType.DMA((N_CHIP - 1,)), pltpu.SemaphoreType.DMA((N_CHIP - 1,)),
           pltpu.SemaphoreType.DMA((nf + nl + 1,))],
        args=(c, w_ada, b4, *shards, dww, wg), comm=comm, body_starts=True)
    return (res[0], res[1], res[2], res[3:]), rest


def _chip_partials(name, place, gs, rs, comm=None, after=()):
    n = len(gs)

    def body(pref, *refs):
        g_refs, r_refs = refs[:n], refs[n:2 * n]
        pb_refs, own_refs = refs[2 * n:3 * n], refs[3 * n:]
        jj = pl.program_id(0)
        for a in range(n):
            sm = g_refs[a][0] + r_refs[a][0]
            pb_refs[a][0] = sm.astype(MXU_DTYPE)

            @pl.when(jj == pref[1])
            def _(a=a, sm=sm):
                own_refs[a][...] = sm

    halves = [(g.shape[1] // 2, g.shape[2]) for g in gs]
    in_specs = [pl.BlockSpec((1, h, w), lambda jj, pref: (jj, pref[0], 0)) for h, w in halves]
    in_specs += [pl.BlockSpec((1, h, w), lambda jj, pref: (jj, 0, 0)) for h, w in halves]
    out_specs = [pl.BlockSpec((1, h, w), lambda jj, pref: (jj, 0, 0)) for h, w in halves]
    out_specs += [pl.BlockSpec((h, w), lambda jj, pref: (0, 0)) for h, w in halves]
    out, rest = _call(
        body, name=name, grid=(N_CHIP,), in_specs=in_specs, out_specs=out_specs,
        out_shape=[jax.ShapeDtypeStruct((N_CHIP, h, w), MXU_DTYPE) for h, w in halves]
        + [jax.ShapeDtypeStruct((h, w), F32) for h, w in halves],
        args=(*gs, *rs), prefetch=(place,), comm=comm, after=after)
    return (out[:n], out[n:]), rest


def _sum_partials(name, place, owns, recvd, comm=None, after=()):
    n = len(owns)

    def body(pref, *refs):
        o_refs, r_refs, out_refs = refs[:n], refs[n:2 * n], refs[2 * n:]
        for a in range(n):
            acc = o_refs[a][...]
            for r in range(N_CHIP - 1):
                acc = acc + r_refs[a][r].astype(F32)
            out_refs[a][...] = acc

    full = lambda a: pl.BlockSpec(a.shape, lambda i, pref: (0,) * a.ndim)
    return _call(
        body, name=name, grid=(1,), in_specs=[full(a) for a in list(owns) + list(recvd)],
        out_specs=[pl.BlockSpec(o.shape, lambda i, pref: (pref[0], 0)) for o in owns],
        out_shape=[jax.ShapeDtypeStruct((2 * o.shape[0], o.shape[1]), F32) for o in owns],
        args=(*owns, *recvd), prefetch=(place,), comm=comm, after=after)


def _adamw_math(w, g, m, v):
    m = ADAM_B1 * m + (1.0 - ADAM_B1) * g
    v = ADAM_B2 * v + (1.0 - ADAM_B2) * (g * g)
    m_hat = m / (1.0 - ADAM_B1 ** ADAM_STEP)
    v_hat = v / (1.0 - ADAM_B2 ** ADAM_STEP)
    delta = -ADAM_LR * (m_hat / (jnp.sqrt(v_hat) + ADAM_EPS) + ADAM_WD * w)
    return delta, m, v


def _row_tile(rows):
    for t in (512, 352, 256, 128):
        if rows % t == 0:
            return t
    return rows


def _adamw(name, wgmv, steps, after=()):
    n = len(wgmv)

    def body(*refs):
        ins, outs = refs[:4 * n], refs[4 * n:]
        for i in range(n):
            w_ref, g_ref, m_ref, v_ref = ins[4 * i:4 * i + 4]
            d_ref, nm_ref, nv_ref = outs[3 * i:3 * i + 3]
            d_ref[...], nm_ref[...], nv_ref[...] = _adamw_math(w_ref[...], g_ref[...], m_ref[...], v_ref[...])

    in_specs, out_specs, out_shape, args = [], [], [], []
    for w, g, m, v in wgmv:
        rows, cols = w.shape
        spec = pl.BlockSpec((rows // steps, cols), lambda i: (i, 0))
        in_specs += [spec] * 4
        out_specs += [spec] * 3
        out_shape += [jax.ShapeDtypeStruct(w.shape, F32)] * 3
        args += [w, g, m, v]
    res, _ = _call(body, name=name, grid=(steps,), in_specs=in_specs, out_specs=out_specs, out_shape=out_shape,
                   args=args, after=after)
    return [res[3 * i:3 * i + 3] for i in range(n)]


def _adamw_ada(place, cact, dmod, w, m, v, after=()):
    rows, cols = w.shape
    tr = _row_tile(rows)

    def body(pref, ca_ref, dm_ref, w_ref, m_ref, v_ref, g_ref, d_ref, nm_ref, nv_ref):
        g = lax.dot_general(ca_ref[...], dm_ref[...], (((0,), (0,)), ((), ())), preferred_element_type=F32,
                            precision=lax.Precision.HIGHEST)
        g_ref[...] = g
        d_ref[...], nm_ref[...], nv_ref[...] = _adamw_math(w_ref[...], g, m_ref[...], v_ref[...])

    spec = pl.BlockSpec((tr, cols), lambda i, pref: (i, 0))
    return _call(
        body, name="adamw_ada", grid=(rows // tr,),
        in_specs=[pl.BlockSpec((N_DEV, tr), lambda i, pref: (0, i)),
                  pl.BlockSpec((N_DEV, cols), lambda i, pref: (0, pref[1])), spec, spec, spec],
        out_specs=[spec] * 4, out_shape=[jax.ShapeDtypeStruct(w.shape, F32)] * 4,
        args=(cact, dmod, w, m, v), prefetch=(place,), after=after)[0]


def _adamw_small(place, owns, gathered, wmv):
    nw = len(wmv)
    flat = [a for t in wmv for a in t]

    def body(pref, *refs):
        own_refs, all_refs, refs = refs[:5], refs[5:10], refs[10:]
        w_refs = refs[:3 * nw]
        loss_ref, dmod_ref = refs[3 * nw], refs[3 * nw + 1]
        o_refs = refs[3 * nw + 2:]
        j = pref[1]
        me = 2 * pref[1] + pref[0]

        def total(i):
            acc = None
            for b in range(N_DEV):
                blk = jnp.where(me == b, own_refs[i][...], all_refs[i][b])
                acc = blk if acc is None else acc + blk
            return acc

        vf, vd, vc, ddw, gwg = [total(i) for i in range(5)]
        loss_ref[...] = (0.5 / D_MODEL) * jnp.sum(vf[5:6, :], axis=1, keepdims=True)
        order = ((1, 1), (1, 2), (1, 0), (0, 2), (0, 3), (0, 1))
        for b in range(N_DEV):
            for q, (i, row) in enumerate(order):
                dmod_ref[b:b + 1, q * D_MODEL:(q + 1) * D_MODEL] = jnp.where(
                    me == b, own_refs[i][row:row + 1, :], all_refs[i][b, row:row + 1, :])
        dm = dmod_ref[...]
        g_bada = dm[0:1, :]
        for b in range(1, N_DEV):
            g_bada = g_bada + dm[b:b + 1, :]
        g_dww = jnp.zeros((32, POOL_G), F32)
        for k in range(N_CHIP):
            g_dww = g_dww + jnp.where(j == k, ddw[:, k * POOL_G:(k + 1) * POOL_G], 0.0)
        grads = [g_bada, vd[3:4, :], g_dww, vc[3:4, :], vc[1:2, :], vc[2:3, :], gwg, vc[0:1, :], vf[4:5, :],
                 vf[0:1, :]]
        for i, g in enumerate(grads):
            w_ref, m_ref, v_ref = w_refs[3 * i:3 * i + 3]
            d, nm, nv = _adamw_math(w_ref[...], g, m_ref[...], v_ref[...])
            o_refs[4 * i][...] = g
            o_refs[4 * i + 1][...] = d
            o_refs[4 * i + 2][...] = nm
            o_refs[4 * i + 3][...] = nv

    outs = [jax.ShapeDtypeStruct((1, 1), F32), jax.ShapeDtypeStruct((N_DEV, 6 * D_MODEL), F32)]
    for w, _, _ in wmv:
        outs += [jax.ShapeDtypeStruct(w.shape, F32)] * 4
    full = lambda a: pl.BlockSpec(a.shape, lambda i, pref: (0,) * a.ndim)
    args = list(owns) + list(gathered) + flat
    res, _ = _call(body, name="adamw_small", grid=(1,), in_specs=[full(a) for a in args],
                   out_specs=[full(o) for o in outs], out_shape=outs, args=args, prefetch=(place,))
    return res[0], res[1], [res[2 + 4 * i:6 + 4 * i] for i in range(nw)]


def kernel(x, c, w_ada, b_ada, g_norm1, w_in, dw_w, dw_b, conv_ln_g, conv_ln_b, w_conv_pw, w_pool_group, pool_scale, w_out, g_norm2, w_ffn_gate, w_ffn_up, w_ffn_down, g_final, loss_target, m_w_ada, m_b_ada, m_g_norm1, m_w_in, m_dw_w, m_dw_b, m_conv_ln_g, m_conv_ln_b, m_w_conv_pw, m_w_pool_group, m_pool_scale, m_w_out, m_g_norm2, m_w_ffn_gate, m_w_ffn_up, m_w_ffn_down, m_g_final, v_w_ada, v_b_ada, v_g_norm1, v_w_in, v_dw_w, v_dw_b, v_conv_ln_g, v_conv_ln_b, v_w_conv_pw, v_w_pool_group, v_pool_scale, v_w_out, v_g_norm2, v_w_ffn_gate, v_w_ffn_up, v_w_ffn_down, v_g_final):
    xi, yi, ci = _place()
    place = jnp.stack([ci, 2 * xi + yi]).astype(jnp.int32)
    n_ada = w_ada.shape[2]

    tr = lambda a: jnp.transpose(a[0])
    mixer_shards = [w_in[0], w_conv_pw[0], w_out[0]]
    ffn_shards = [tr(w_ffn_gate), tr(w_ffn_up), w_ffn_down[0]]
    slots = [lax.empty((N_CHIP,) + a.shape, MXU_DTYPE) for a in mixer_shards] + [lax.empty((N_CHIP,) + dw_w.shape[1:], F32)]

    (mod, cact, wg_b, (b_gate, b_up, b_down)), (win_g, wpw_g, wout_g, dww_g) = _ada_fwd(
        c, w_ada[0], b_ada.reshape(N_CHIP, n_ada), mixer_shards, ffn_shards, dw_w[0], w_pool_group[0],
        comm=_weights_gather(slots, [True, True, True, False]))
    dww_full = jnp.pad(jnp.concatenate([dww_g[k] for k in range(N_CHIP)], axis=1), ((0, 1), (0, 0)))
    w_pw = wpw_g.reshape(CONV_W, CONV_W)
    w_o = wout_g.reshape(D_MODEL, D_MODEL)
    xs, tgt, gf = x[0], loss_target[0], g_final.reshape(1, D_MODEL)
    s = xs.shape[0]
    fb = b_gate.shape[1]

    (x2, y, u, z, rstd, p, ycat), (wgate_g, wup_g, wdown_g) = _mixer_fwd(
        xs, mod, g_norm1, win_g, dww_full, dw_b, conv_ln_g, conv_ln_b, w_pw, wg_b, pool_scale, w_o,
        comm=_weights_gather([b_gate, b_up, b_down], [True, True, True]))
    dx2, h2, df, act, dgg, duu, vec_f = _ffn(x2, tgt, mod, g_norm2, gf, wgate_g, wup_g, wdown_g)

    whole = lambda w: pl.BlockSpec((s, w), lambda j: (0, 0))
    hid = pl.BlockSpec((1, s, fb), lambda j: (j, 0, 0))
    c_gate, _ = _dw("dw_gate", dgg, hid, h2, whole(D_MODEL), N_CHIP, fb, D_MODEL)
    c_up, (r_gate,) = _dw("dw_up", duu, hid, h2, whole(D_MODEL), N_CHIP, fb, D_MODEL, comm=_sibling_halves([c_gate]))
    ((pb_gate,), (own_gate,)), _ = _chip_partials("partials_gate", place, [c_gate], [r_gate])
    ex_gate = _exchange_parts([pb_gate])
    st_gate, tok_gate = _split_start("exchange_gate_start", [pb_gate], *ex_gate)
    c_down, (r_up,) = _dw("dw_down", act, hid, df, whole(D_MODEL), N_CHIP, fb, D_MODEL,
                          comm=_sibling_halves([c_up]), after=(tok_gate,))
    sib_down = _sibling_parts([c_down])
    st_sd, tok_sd = _split_start("sibling_down_start", [c_down], *sib_down)
    ((pb_up,), (own_up,)), _ = _chip_partials("partials_up", place, [c_up], [r_up], after=(tok_sd,))
    (c_down,), (r_down,) = _split_wait("sibling_down_wait", st_sd, 1, sib_down[1], sib_down[2], after=(pb_up,),
                                       with_sources=True)
    ((pb_down,), (own_down,)), _ = _chip_partials("partials_down", place, [c_down], [r_down])
    ex_ud = _exchange_parts([pb_up, pb_down])
    st_ud, tok_ud = _split_start("exchange_up_down_start", [pb_up, pb_down], *ex_ud)
    (gx, h1, du, dy, c_pw, g_wg, vec_d, vec_c, ddw), _ = _mixer_bwd(
        dx2, xs, y, u, z, rstd, p, mod, g_norm1, win_g, dww_full, conv_ln_g, conv_ln_b, w_pw, wg_b, pool_scale, w_o,
        after=(tok_ud,))

    small_own = [vec_f, vec_d, vec_c, ddw, g_wg]
    ex_small = _small_parts(small_own)
    st_small, tok_small = _split_start("small_grads_start", small_own, *ex_small, zeroed=True)
    c_out, c_in = _dw_mixer(ycat, dy, h1, du, after=(tok_small,))
    mix = [c_in, c_pw.reshape(N_CHIP, CONV_W // N_CHIP, CONV_W), c_out]
    sib_mix = _sibling_parts(mix)
    st_sm, tok_sm = _split_start("sibling_mix_start", mix, *sib_mix)
    (rc_gate,) = _split_wait("exchange_gate_wait", st_gate, 1, ex_gate[1], ex_gate[2], after=(tok_sm,))
    rc_up, rc_down = _split_wait("exchange_up_down_wait", st_ud, 2, ex_ud[1], ex_ud[2], after=(tok_sm, rc_gate))

    ffn_fulls, _ = _sum_partials("sum_ffn", place, [own_gate, own_up, own_down], [rc_gate, rc_up, rc_down])
    join_sems, join_copies = _join_parts(ffn_fulls)
    join_in_place = lambda bufs, lands, send, recv: join_copies(bufs, send, recv)
    st_jf, tok_jf = _split_start("join_ffn_start", ffn_fulls, [], join_sems, join_in_place)
    mix, (r_in, r_pw, r_out) = _split_wait("sibling_mix_wait", st_sm, len(mix), sib_mix[1], sib_mix[2],
                                           after=(tok_jf,), with_sources=True)
    (pbs_mix, owns_mix), _ = _chip_partials("partials_mix", place, mix, [r_in, r_pw, r_out])
    g_gate, g_up, g_down = _split_wait("join_ffn_wait", st_jf, len(ffn_fulls), join_sems, join_in_place,
                                       after=(pbs_mix[0],), in_place=True)

    pad_rows = lambda a: jnp.pad(a[0], ((0, 1), (0, 0)))
    row = lambda a: a.reshape(1, -1)
    small = [(b_ada, m_b_ada, v_b_ada), (g_norm1, m_g_norm1, v_g_norm1),
             (pad_rows(dw_w), pad_rows(m_dw_w), pad_rows(v_dw_w)), (dw_b, m_dw_b, v_dw_b),
             (conv_ln_g, m_conv_ln_g, v_conv_ln_g), (conv_ln_b, m_conv_ln_b, v_conv_ln_b),
             (w_pool_group[0], m_w_pool_group[0], v_w_pool_group[0]), (pool_scale, m_pool_scale, v_pool_scale),
             (g_norm2, m_g_norm2, v_g_norm2), (row(g_final), row(m_g_final), row(v_g_final))]
    lead = lambda outs: [a[None] for a in outs]

    lands, sem_shape, copies = _exchange_parts(pbs_mix)
    state, token = _split_start("exchange_mix_start", pbs_mix, lands, sem_shape, copies)
    u_gate, u_up, u_down = _adamw(
        "adamw_ffn", [(tr(w_ffn_gate), g_gate, tr(m_w_ffn_gate), tr(v_w_ffn_gate)),
                      (tr(w_ffn_up), g_up, tr(m_w_ffn_up), tr(v_w_ffn_up)),
                      (w_ffn_down[0], g_down, m_w_ffn_down[0], v_w_ffn_down[0])],
        steps=4, after=(token,))
    o_gate = [jnp.transpose(o) for o in [g_gate] + list(u_gate)]
    o_up = [jnp.transpose(o) for o in [g_up] + list(u_up)]
    o_down = [g_down] + list(u_down)
    small_own, small_all = _split_wait("small_grads_wait", st_small, len(small_own), ex_small[1], ex_small[2],
                                       after=(u_down[0],), with_sources=True)
    loss, dmod, small_out = _adamw_small(place, small_own, small_all, small)
    (o_bada, o_g1, o_dww, o_dwb, o_lng, o_lnb, o_wg, o_ps, o_g2, o_gf) = small_out
    o_dww = [a[:CONV_K] for a in o_dww]
    o_gf = [a.reshape(D_MODEL) for a in o_gf]
    o_ada = _adamw_ada(place, cact, dmod, w_ada[0], m_w_ada[0], v_w_ada[0], after=(token,))
    rc_mix = _split_wait("exchange_mix_wait", state, len(pbs_mix), sem_shape, copies, after=(o_ada[1], u_down[0]))
    mix_fulls, _ = _sum_partials("sum_mix", place, owns_mix, rc_mix)
    g_in, g_pw, g_out = _comm_only("join_mix", _join_halves(mix_fulls))
    u_in, u_pw, u_out = _adamw(
        "adamw_mix", [(w_in[0], g_in, m_w_in[0], v_w_in[0]), (w_conv_pw[0], g_pw, m_w_conv_pw[0], v_w_conv_pw[0]),
                      (w_out[0], g_out, m_w_out[0], v_w_out[0])], steps=4)
    o_in, o_pw, o_out = [g_in] + list(u_in), [g_pw] + list(u_pw), [g_out] + list(u_out)

    per_weight = [lead(o_ada), o_bada, o_g1, lead(o_in), lead(o_dww), o_dwb, o_lng, o_lnb, lead(o_pw), lead(o_wg),
                  o_ps, lead(o_out), o_g2, lead(o_gate), lead(o_up), lead(o_down), o_gf]
    result = [loss.reshape(()), gx[None]]
    for kind in range(4):
        result += [o[kind] for o in per_weight]
    return tuple(result)
```

```python
import functools

import jax
import jax.numpy as jnp
from jax import lax
from jax.experimental import pallas as pl
from jax.experimental.pallas import tpu as pltpu

F32 = jnp.float32
MXU_DTYPE = jnp.bfloat16
EPS = 1e-6

D_MODEL = 1024
CONV_W = 512
POOL_W = 512
CONV_K = 31
POOL_WINDOWS = (2, 4, 8, 16)
POOL_G = 128
IN_W = 2 * CONV_W + POOL_W
N_CHIP = 4
N_DEV = 8
CONV_HALO = 32
POOL_HALO = 16

ADAM_LR = 0.001
ADAM_B1 = 0.9
ADAM_B2 = 0.999
ADAM_EPS = 1e-08
ADAM_WD = 0.01
ADAM_STEP = 10

MESH = pl.DeviceIdType.MESH
ANY = pl.BlockSpec(memory_space=pl.ANY)
VMEM = pl.BlockSpec(memory_space=pltpu.VMEM)


def _dot(a, b):
    return jnp.dot(a.astype(MXU_DTYPE), b.astype(MXU_DTYPE), preferred_element_type=F32)


def _dot_nt(a, b):
    return lax.dot_general(a.astype(MXU_DTYPE), b.astype(MXU_DTYPE), (((1,), (1,)), ((), ())),
                           preferred_element_type=F32)


def _dot_tn(a, b):
    return lax.dot_general(a.astype(MXU_DTYPE), b.astype(MXU_DTYPE), (((0,), (0,)), ((), ())),
                           preferred_element_type=F32)


def _sigmoid(v):
    return 1.0 / (1.0 + jnp.exp(-v))


def _full(shape):
    n = len(shape)
    return pl.BlockSpec(shape, lambda *_: (0,) * n)


def _token_tile(s):
    return 256 if s % 256 == 0 else s


SUBLANES = 8


def _row_shifts(pad_ref, shifted_ref, rows):
    for r in range(1, SUBLANES):
        shifted_ref[r - 1] = pad_ref[r:r + rows, :]

    def window(i, n):
        r, base = i % SUBLANES, i - i % SUBLANES
        if r == 0:
            return pad_ref[base:base + n, :]
        return shifted_ref[r - 1, base:base + n, :]

    return window


def _place():
    return lax.axis_index("x"), lax.axis_index("y"), lax.axis_index("c")


def _flip(x, y, r):
    return ((1 - x) if r & 2 else x, (1 - y) if r & 1 else y)


def _remote(src, dst, send_sem, recv_sem, dev):
    return pltpu.make_async_remote_copy(src_ref=src, dst_ref=dst, send_sem=send_sem, recv_sem=recv_sem,
                                        device_id=dev, device_id_type=MESH)


class _Comm:
    def __init__(self, ins, outs, aliases, scratch, start, finish, mid=None):
        self.ins, self.outs, self.aliases, self.scratch = list(ins), list(outs), dict(aliases), list(scratch)
        self.start, self.finish = start, finish
        self.mid = mid


def _both(a, b):
    na, nao, nas = len(a.ins), len(a.outs), len(a.scratch)
    aliases = dict(a.aliases)
    aliases.update({na + i: nao + o for i, o in b.aliases.items()})

    def start(ins, outs, scr):
        a.start(ins[:na], outs[:nao], scr[:nas])
        b.start(ins[na:], outs[nao:], scr[nas:])

    def finish(ins, outs, scr):
        a.finish(ins[:na], outs[:nao], scr[:nas])
        b.finish(ins[na:], outs[nao:], scr[nas:])

    def mid(ins, outs, scr):
        if a.mid:
            a.mid(ins[:na], outs[:nao], scr[:nas])
        if b.mid:
            b.mid(ins[na:], outs[nao:], scr[nas:])

    return _Comm(a.ins + b.ins, a.outs + b.outs, aliases, a.scratch + b.scratch, start, finish,
                 mid if (a.mid or b.mid) else None)


def _call(body, *, name, grid, in_specs, out_specs, out_shape, args, scratch_shapes=(), prefetch=(), comm=None,
          body_starts=False, after=()):
    in_specs = list(in_specs) + [ANY] * len(after)
    args = list(args) + list(after)
    n_pre, n_in, n_out, n_scr = len(prefetch), len(in_specs), len(out_specs), len(scratch_shapes)
    n_body_in = n_in - len(after)
    c_ins = comm.ins if comm else []
    c_outs = comm.outs if comm else []
    c_scr = comm.scratch if comm else []
    last = grid[0] - 1

    def wrapped(*refs):
        pre, refs = refs[:n_pre], refs[n_pre:]
        ins, cin = refs[:n_body_in], refs[n_in:n_in + len(c_ins)]
        refs = refs[n_in + len(c_ins):]
        outs, cout = refs[:n_out], refs[n_out:n_out + len(c_outs)]
        refs = refs[n_out + len(c_outs):]
        scr, cscr = refs[:n_scr], refs[n_scr:]
        step = pl.program_id(0)
        if comm and not body_starts:
            @pl.when(step == 0)
            def _():
                comm.start(cin, cout, cscr)

        has_mid = comm is not None and comm.mid is not None
        mid_step = grid[0] // 2 if grid[0] >= 4 else None
        if has_mid and mid_step is not None:
            @pl.when(step == mid_step)
            def _():
                comm.mid(cin, cout, cscr)

        if body_starts:
            body(lambda: comm.start(cin, cout, cscr), lambda: comm.mid(cin, cout, cscr), cout,
                 *pre, *ins, *outs, *scr)
        else:
            body(*pre, *ins, *outs, *scr)
        if comm:
            @pl.when(step == last)
            def _():
                if has_mid and mid_step is None and not body_starts:
                    comm.mid(cin, cout, cscr)
                comm.finish(cin, cout, cscr)

    aliases = {n_pre + n_in + a: n_out + b for a, b in (comm.aliases if comm else {}).items()}
    res = pl.pallas_call(
        wrapped, name=name,
        grid_spec=pltpu.PrefetchScalarGridSpec(
            num_scalar_prefetch=n_pre, grid=grid, in_specs=list(in_specs) + [ANY] * len(c_ins),
            out_specs=list(out_specs) + [ANY] * len(c_outs), scratch_shapes=list(scratch_shapes) + list(c_scr)),
        out_shape=list(out_shape) + list(c_outs),
        input_output_aliases=aliases,
        compiler_params=pltpu.CompilerParams(dimension_semantics=("arbitrary",)),
    )(*prefetch, *args, *c_ins)
    return res[:n_out], res[n_out:]


def _comm_only(name, comm):
    return _call(lambda: None, name=name, grid=(1,), in_specs=[], out_specs=[], out_shape=[], args=[], comm=comm)[1]


def _weights_gather(bufs, split):
    n = len(bufs)

    def ctx(outs):
        x, y, cc = _place()
        chips = dict(me=2 * x + y, y=2 * x + (1 - y), x=2 * (1 - x) + y, d=2 * (1 - x) + (1 - y))
        devs = dict(y=(x, 1 - y, cc), x=(1 - x, y, cc), d=(1 - x, 1 - y, cc), s=(x, y, 1 - cc))

        def piece(a, kj, pc, q=None):
            if not split[a]:
                return outs[a].at[kj]
            h = bufs[a].shape[1] // 2
            if q is None:
                return outs[a].at[kj, pl.ds(pc * h, h), :]
            return outs[a].at[kj, pl.ds(pc * h + q * (h // 2), h // 2), :]

        return cc, chips, devs, piece

    def directs(a, outs, send, recv):
        cc, chips, devs, piece = ctx(outs)
        if not split[a]:
            whole = piece(a, chips["me"], cc)
            return [_remote(whole, whole, send.at[a, k], recv.at[a, k], devs[t]) for k, t in ((0, "y"), (2, "x"), (4, "d"))]
        q = lambda i: piece(a, chips["me"], cc, i)
        return [_remote(q(0), q(0), send.at[a, 0], recv.at[a, 0], devs["y"]),
                _remote(q(1), q(1), send.at[a, 3], recv.at[a, 3], devs["x"]),
                _remote(q(1), q(1), send.at[a, 1], recv.at[a, 1], devs["y"]),
                _remote(q(0), q(0), send.at[a, 2], recv.at[a, 2], devs["x"])]

    def landed(a, k, outs, send, recv):
        cc, chips, devs, piece = ctx(outs)
        if not split[a]:
            got = piece(a, chips[{0: "y", 2: "x", 4: "d"}[k]], cc)
        elif k < 6:
            got = piece(a, chips[("y", "y", "x", "x", "d", "d")[k]], cc, (0, 1, 0, 1, 0, 1)[k])
        else:
            got = piece(a, chips[("y", "x", "d")[k - 6]], 1 - cc)
        return _remote(got, got, send.at[a, k], recv.at[a, k], devs["s"])

    def passed_on(a, outs, send, recv):
        cc, chips, devs, piece = ctx(outs)
        from_y, from_x = piece(a, chips["y"], cc, 0), piece(a, chips["x"], cc, 1)
        return [_remote(from_y, from_y, send.at[a, 4], recv.at[a, 4], devs["x"]),
                _remote(from_x, from_x, send.at[a, 5], recv.at[a, 5], devs["y"])]

    def to_sibling(a, outs, send, recv, which=(0, 1, 2)):
        cc, chips, devs, piece = ctx(outs)
        halves = [piece(a, chips[("y", "x", "d")[i]], cc) for i in which]
        return [_remote(hf, hf, send.at[a, 6 + i], recv.at[a, 6 + i], devs["s"]) for i, hf in zip(which, halves)]

    def start(ins, outs, scr):
        send, recv = scr
        per_item = [directs(a, outs, send, recv) for a in range(n)]
        for rank in range(4):
            for cps in per_item:
                if rank < len(cps):
                    cps[rank].start()

    def mid(ins, outs, scr):
        send, recv = scr
        for a in range(n):
            if split[a]:
                fy, fx = passed_on(a, outs, send, recv)
                landed(a, 0, outs, send, recv).wait_recv()
                fy.start()
                landed(a, 3, outs, send, recv).wait_recv()
                fx.start()

    def finish(ins, outs, scr):
        send, recv = scr
        for a in range(n):
            if split[a]:
                to_y, to_x = to_sibling(a, outs, send, recv, which=(0, 1))
                landed(a, 1, outs, send, recv).wait_recv()
                to_y.start()
                landed(a, 2, outs, send, recv).wait_recv()
                to_x.start()
        for a in range(n):
            if split[a]:
                for k in (4, 5):
                    landed(a, k, outs, send, recv).wait_recv()
                to_sibling(a, outs, send, recv, which=(2,))[0].start()
            else:
                for k in (0, 2, 4):
                    landed(a, k, outs, send, recv).wait_recv()
        for a in range(n):
            if split[a]:
                for k in (6, 7, 8):
                    landed(a, k, outs, send, recv).wait_recv()
            cps = directs(a, outs, send, recv)
            if split[a]:
                cps += passed_on(a, outs, send, recv) + to_sibling(a, outs, send, recv)
            for cp in cps:
                cp.wait_send()

    return _Comm(bufs, [jax.ShapeDtypeStruct(b.shape, b.dtype) for b in bufs], {i: i for i in range(n)},
                 [pltpu.SemaphoreType.DMA((n, 9)), pltpu.SemaphoreType.DMA((n, 9))], start, finish, mid)


HBM =pl.BlockSpec(memory_space=pltpu.HBM)
SEM = pl.BlockSpec(memory_space=pltpu.SEMAPHORE)
DATAFLOW = pltpu.SideEffectType.DATAFLOW_SIDE_EFFECTING


class _SemGrid:
    def __init__(self, refs, cols):
        self.refs, self.cols = refs, cols

    @property
    def at(self):
        return self

    def __getitem__(self, idx):
        return self.refs[idx[0] * self.cols + idx[1]]


def _split_start(name, srcs, lands, sem_shape, copies, zeroed=False):
    n, k = len(srcs), len(lands)
    ns = sem_shape[0] * sem_shape[1]

    def body(*refs):
        src_refs, land_refs = refs[:n], refs[n:n + k]
        send = _SemGrid(refs[n + k:n + k + ns], sem_shape[1])
        recv = _SemGrid(refs[n + k + ns:n + k + 2 * ns], sem_shape[1])
        token = refs[-1]
        for cp in copies(src_refs, land_refs, send, recv):
            cp.start()
        token[...] = jnp.zeros(token.shape, F32)

    hbm = lambda a: pltpu.with_memory_space_constraint(a, pltpu.HBM)
    zones = [jnp.zeros(l.shape, l.dtype) if zeroed else lax.empty(l.shape, l.dtype) for l in lands]
    out = pl.pallas_call(
        body, name=name,
        out_shape=[pltpu.SemaphoreType.DMA(())] * (2 * ns)
        + [pltpu.HBM(a.shape, a.dtype) for a in list(srcs) + list(lands)] + [jax.ShapeDtypeStruct((8, 128), F32)],
        in_specs=[HBM] * (n + k), out_specs=[SEM] * (2 * ns) + [HBM] * (n + k) + [VMEM],
        input_output_aliases={i: 2 * ns + i for i in range(n + k)},
        compiler_params=pltpu.CompilerParams(has_side_effects=DATAFLOW),
    )(*[hbm(a) for a in srcs], *[hbm(z) for z in zones])
    return out[:-1], out[-1]


def _split_wait(name, state, n, sem_shape, copies, after, in_place=False, with_sources=False):
    ns = sem_shape[0] * sem_shape[1]
    sems, bufs = state[:2 * ns], state[2 * ns:]
    k = len(bufs) - n

    def body(*refs):
        src_refs, land_refs = refs[:n], refs[n:n + k]
        send = _SemGrid(refs[n + k:n + k + ns], sem_shape[1])
        recv = _SemGrid(refs[n + k + ns:n + k + 2 * ns], sem_shape[1])
        cps = copies(src_refs, land_refs, send, recv)
        for cp in cps:
            cp.wait_send()
        for cp in cps:
            cp.wait_recv()

    out = pl.pallas_call(
        body, name=name,
        out_shape=[pltpu.HBM(a.shape, a.dtype) for a in bufs],
        in_specs=[HBM] * (n + k) + [SEM] * (2 * ns) + [ANY] * len(after), out_specs=[HBM] * (n + k),
        input_output_aliases={i: i for i in range(n + k)},
        compiler_params=pltpu.CompilerParams(has_side_effects=DATAFLOW),
    )(*bufs, *sems, *after)
    if with_sources:
        return out[:n], out[n:]
    return out[:n] if in_place else out[n:]


def _direct_phases(copies):
    def start(ins, outs, scr):
        for cp in copies(ins, outs, *scr):
            cp.start()

    def finish(ins, outs, scr):
        cps = copies(ins, outs, *scr)
        for cp in cps:
            cp.wait_recv()
        for cp in cps:
            cp.wait_send()

    return start, finish


def _sibling_parts(gs):
    n = len(gs)

    def copies(ins, outs, send, recv):
        x, y, cc = _place()
        cps = []
        for a in range(n):
            h = gs[a].shape[1] // 2
            cps.append(_remote(ins[a].at[:, pl.ds((1 - cc) * h, h), :], outs[a], send.at[a, 0], recv.at[a, 0],
                               (x, y, 1 - cc)))
        return cps

    return [jax.ShapeDtypeStruct((N_CHIP, g.shape[1] // 2, g.shape[2]), F32) for g in gs], (n, 1), copies


def _sibling_halves(gs):
    lands, sem_shape, copies = _sibling_parts(gs)
    start, finish = _direct_phases(copies)
    return _Comm(gs, lands, {}, [pltpu.SemaphoreType.DMA(sem_shape), pltpu.SemaphoreType.DMA(sem_shape)],
                 start, finish)


def _exchange_parts(pbs):
    n = len(pbs)

    def copies(ins, outs, send, recv):
        x, y, cc = _place()
        cps = []
        for a in range(n):
            for r in range(1, N_CHIP):
                kx, ky = _flip(x, y, r)
                cps.append(_remote(ins[a].at[2 * kx + ky], outs[a].at[r - 1], send.at[a, r - 1], recv.at[a, r - 1],
                                   (kx, ky, cc)))
        return cps

    lands = [jax.ShapeDtypeStruct((N_CHIP - 1,) + p.shape[1:], p.dtype) for p in pbs]
    return lands, (n, N_CHIP - 1), copies


def _small_parts(arrs):
    n = len(arrs)

    def copies(ins, outs, send, recv):
        x, y, cc = _place()
        b = 4 * x + 2 * y + cc
        cps = []
        for a in range(n):
            for r in range(1, N_DEV):
                dev = ((1 - x) if r & 4 else x, (1 - y) if r & 2 else y, (1 - cc) if r & 1 else cc)
                cps.append(_remote(ins[a], outs[a].at[b], send.at[a, r - 1], recv.at[a, r - 1], dev))
        return cps

    lands = [jax.ShapeDtypeStruct((N_DEV,) + a.shape, a.dtype) for a in arrs]
    return lands, (n, N_DEV - 1), copies


def _exchange_partials(pbs):
    lands, sem_shape, copies = _exchange_parts(pbs)
    start, finish = _direct_phases(copies)
    return _Comm(pbs, lands, {}, [pltpu.SemaphoreType.DMA(sem_shape), pltpu.SemaphoreType.DMA(sem_shape)],
                 start, finish)


def _join_parts(fulls):
    n = len(fulls)

    def copies(bufs, send, recv):
        x, y, cc = _place()
        cps = []
        for a in range(n):
            h = fulls[a].shape[0] // 2
            mine = bufs[a].at[pl.ds(cc * h, h), :]
            cps.append(_remote(mine, mine, send.at[a, 0], recv.at[a, 0], (x, y, 1 - cc)))
        return cps

    return (n, 1), copies


def _join_halves(fulls):
    sem_shape, copies = _join_parts(fulls)
    start, finish = _direct_phases(lambda ins, outs, send, recv: copies(outs, send, recv))
    return _Comm(fulls, [jax.ShapeDtypeStruct(f.shape, F32) for f in fulls], {i: i for i in range(len(fulls))},
                 [pltpu.SemaphoreType.DMA(sem_shape), pltpu.SemaphoreType.DMA(sem_shape)], start, finish)


def _mixer_fwd(x, mod, g1, w_in, dww, dwb, lng, lnb, w_pw, wg, pscale, w_out, comm=None):
    s = x.shape[0]
    ts = _token_tile(s)
    nt = s // ts

    def body(x_ref, mod_ref, g1_ref, win_ref, dww_ref, dwb_ref, lng_ref, lnb_ref, wpw_ref, wg_ref, ps_ref,
             wout_ref, x2_ref, y_ref, u_ref, z_ref, rstd_ref, p_ref, ycat_ref, gpad, vpad, gshift):
        i = pl.program_id(0)

        @pl.when(i == 0)
        def _():
            gpad[0:CONV_HALO, :] = jnp.zeros((CONV_HALO, CONV_W), F32)
            vpad[0:POOL_HALO, :] = jnp.zeros((POOL_HALO, POOL_W), F32)

        xt = x_ref[...]
        sh1 = mod_ref[0:1, :]
        sc1 = mod_ref[1:2, :]
        gt1 = mod_ref[2:3, :]
        r1 = lax.rsqrt(jnp.mean(xt * xt, axis=-1, keepdims=True) + EPS)
        h1 = (xt * r1 * g1_ref[...]) * (1.0 + sc1) + sh1
        h1b = h1.astype(MXU_DTYPE)
        u = jnp.concatenate([_dot(h1b, win_ref[j]) for j in range(N_CHIP)], axis=1)
        u_ref[...] = u
        a = u[:, :CONV_W]
        g = u[:, CONV_W:2 * CONV_W]
        v = u[:, 2 * CONV_W:]

        gpad[CONV_HALO:CONV_HALO + ts, :] = a * _sigmoid(g)
        window = _row_shifts(gpad, gshift, ts + CONV_HALO - SUBLANES)
        cv = jnp.broadcast_to(dwb_ref[...], (ts, CONV_W))
        off = CONV_HALO - (CONV_K - 1)
        for k in range(CONV_K):
            cv = cv + dww_ref[k:k + 1, :] * window(off + k, ts)
        gpad[0:CONV_HALO, :] = gpad[ts:ts + CONV_HALO, :]

        mu = jnp.mean(cv, axis=-1, keepdims=True)
        cc = cv - mu
        rstd = lax.rsqrt(jnp.mean(cc * cc, axis=-1, keepdims=True) + EPS)
        z = cc * rstd
        z_ref[...] = z
        rstd_ref[...] = rstd
        ln = z * lng_ref[...] + lnb_ref[...]
        sw = ln * _sigmoid(ln)
        yconv = _dot(sw, wpw_ref[...])

        vpad[POOL_HALO:POOL_HALO + ts, :] = v
        t = i * ts + lax.broadcasted_iota(jnp.int32, (ts, 1), 0)
        ps, ypool = [], []
        for gi, w in enumerate(POOL_WINDOWS):
            cols = slice(gi * POOL_G, (gi + 1) * POOL_G)
            acc = vpad[POOL_HALO:POOL_HALO + ts, cols]
            for d in range(1, w):
                acc = acc + vpad[POOL_HALO - d:POOL_HALO - d + ts, cols]
            cnt = jnp.minimum(t + 1, w).astype(F32)
            pg = (acc / cnt - v[:, cols]).astype(MXU_DTYPE)
            ps.append(pg)
            ypool.append(_dot(pg, wg_ref[gi]))
        vpad[0:POOL_HALO, :] = vpad[ts:ts + POOL_HALO, :]
        p_ref[...] = jnp.concatenate(ps, axis=1)
        ypool = jnp.concatenate(ypool, axis=1) * ps_ref[...]

        ycat = jnp.concatenate([yconv, ypool], axis=1).astype(MXU_DTYPE)
        ycat_ref[...] = ycat
        y = _dot(ycat, wout_ref[...])
        y_ref[...] = y
        x2_ref[...] = xt + gt1 * y

    tile = lambda w: pl.BlockSpec((ts, w), lambda i: (i, 0))
    return _call(
        body, name="mixer_fwd", grid=(nt,),
        in_specs=[tile(D_MODEL), _full(mod.shape), _full(g1.shape), _full(w_in.shape), _full(dww.shape),
                  _full(dwb.shape), _full(lng.shape), _full(lnb.shape), _full(w_pw.shape), _full(wg.shape),
                  _full(pscale.shape), _full(w_out.shape)],
        out_specs=[tile(D_MODEL), tile(D_MODEL), tile(IN_W), tile(CONV_W), tile(1), tile(POOL_W), tile(D_MODEL)],
        out_shape=[jax.ShapeDtypeStruct((s, D_MODEL), F32), jax.ShapeDtypeStruct((s, D_MODEL), F32),
                   jax.ShapeDtypeStruct((s, IN_W), F32), jax.ShapeDtypeStruct((s, CONV_W), F32),
                   jax.ShapeDtypeStruct((s, 1), F32), jax.ShapeDtypeStruct((s, POOL_W), MXU_DTYPE),
                   jax.ShapeDtypeStruct((s, D_MODEL), MXU_DTYPE)],
        scratch_shapes=[pltpu.VMEM((ts + CONV_HALO, CONV_W), F32), pltpu.VMEM((ts + POOL_HALO, POOL_W), F32),
                        pltpu.VMEM((SUBLANES - 1, ts + CONV_HALO - SUBLANES, CONV_W), F32)],
        args=(x, mod, g1, w_in, dww, dwb, lng, lnb, w_pw, wg, pscale, w_out), comm=comm)


def _ffn(x2, tgt, mod, g2, gf, w_gate, w_up, w_down):
    s = x2.shape[0]
    ts = _token_tile(s)
    nt = s // ts
    fb = w_gate.shape[1]

    def body(x2_ref, tgt_ref, mod_ref, g2_ref, gf_ref, wgt_ref, wup_ref, wdn_ref,
             dx2_ref, h2_ref, df_ref, act_ref, dgg_ref, duu_ref, vec_ref, gg_s, uu_s):
        i = pl.program_id(0)

        @pl.when(i == 0)
        def _():
            vec_ref[...] = jnp.zeros(vec_ref.shape, F32)

        x2t = x2_ref[...]
        sh2 = mod_ref[3:4, :]
        sc2 = mod_ref[4:5, :]
        gt2 = mod_ref[5:6, :]
        g2v = g2_ref[...]
        gfv = gf_ref[...]
        r2 = lax.rsqrt(jnp.mean(x2t * x2t, axis=-1, keepdims=True) + EPS)
        xh2 = x2t * r2
        n2 = xh2 * g2v
        h2b = (n2 * (1.0 + sc2) + sh2).astype(MXU_DTYPE)
        h2_ref[...] = h2b
        f = jnp.zeros((ts, D_MODEL), F32)
        for j in range(N_CHIP):
            gg = _dot_nt(h2b, wgt_ref[j])
            uu = _dot_nt(h2b, wup_ref[j])
            gg_s[j] = gg
            uu_s[j] = uu
            actb = (gg * _sigmoid(gg) * uu).astype(MXU_DTYPE)
            act_ref[j] = actb
            f = f + _dot(actb, wdn_ref[j])
        x3 = x2t + gt2 * f
        r3 = lax.rsqrt(jnp.mean(x3 * x3, axis=-1, keepdims=True) + EPS)
        xh3 = x3 * r3
        diff = xh3 * gfv - tgt_ref[...]
        dout = diff * (1.0 / D_MODEL)
        dn3 = dout * gfv
        dx3 = r3 * (dn3 - xh3 * jnp.mean(dn3 * xh3, axis=-1, keepdims=True))
        dfb = (dx3 * gt2).astype(MXU_DTYPE)
        df_ref[...] = dfb
        dh2 = jnp.zeros((ts, D_MODEL), F32)
        for j in range(N_CHIP):
            dact = _dot_nt(dfb, wdn_ref[j])
            gg = gg_s[j]
            uu = uu_s[j]
            sg = _sigmoid(gg)
            duu = (dact * (gg * sg)).astype(MXU_DTYPE)
            dgg = (dact * uu * (sg * (1.0 + gg * (1.0 - sg)))).astype(MXU_DTYPE)
            duu_ref[j] = duu
            dgg_ref[j] = dgg
            dh2 = dh2 + _dot(dgg, wgt_ref[j]) + _dot(duu, wup_ref[j])
        dn2 = dh2 * (1.0 + sc2)
        dxh2 = dn2 * g2v
        dx2_ref[...] = dx3 + r2 * (dxh2 - xh2 * jnp.mean(dxh2 * xh2, axis=-1, keepdims=True))

        col = lambda a: jnp.sum(a, axis=0, keepdims=True)
        vec_ref[0:1, :] += col(dout * xh3)
        vec_ref[1:2, :] += col(dx3 * f)
        vec_ref[2:3, :] += col(dh2)
        vec_ref[3:4, :] += col(dh2 * n2)
        vec_ref[4:5, :] += col(dn2 * xh2)
        vec_ref[5:6, :] += col(diff * diff)

    tile = lambda w: pl.BlockSpec((ts, w), lambda i: (i, 0))
    tile3 = pl.BlockSpec((N_CHIP, ts, fb), lambda i: (0, i, 0))
    once = lambda a: pl.BlockSpec(a.shape, lambda i: (0,) * a.ndim, pipeline_mode=pl.Buffered(1))
    hid = jax.ShapeDtypeStruct((N_CHIP, s, fb), MXU_DTYPE)
    return pl.pallas_call(
        body, name="ffn", grid=(nt,),
        in_specs=[tile(D_MODEL), tile(D_MODEL), _full(mod.shape), _full(g2.shape), _full(gf.shape),
                  once(w_gate), once(w_up), once(w_down)],
        out_specs=[tile(D_MODEL), tile(D_MODEL), tile(D_MODEL), tile3, tile3, tile3, _full((8, D_MODEL))],
        out_shape=[jax.ShapeDtypeStruct((s, D_MODEL), F32), jax.ShapeDtypeStruct((s, D_MODEL), MXU_DTYPE),
                   jax.ShapeDtypeStruct((s, D_MODEL), MXU_DTYPE), hid, hid, hid,
                   jax.ShapeDtypeStruct((8, D_MODEL), F32)],
        scratch_shapes=[pltpu.VMEM((N_CHIP, ts, fb), F32), pltpu.VMEM((N_CHIP, ts, fb), F32)],
        compiler_params=pltpu.CompilerParams(dimension_semantics=("arbitrary",)),
    )(x2, tgt, mod, g2, gf, w_gate, w_up, w_down)


def _mixer_bwd(dx2, x, y, u, z, rstd, p, mod, g1, w_in, dww, lng, lnb, w_pw, wg, pscale, w_out, comm=None, after=()):
    s = x.shape[0]
    ts = _token_tile(s)
    nt = s // ts

    def body(dx2_ref, x_ref, y_ref, u_ref, z_ref, rstd_ref, p_ref, mod_ref, g1_ref, win_ref, dww_ref, lng_ref,
             lnb_ref, wpw_ref, wg_ref, ps_ref, wout_ref,
             gx_ref, h1_ref, du_ref, dy_ref, gpw_ref, gwg_ref, vd_ref, vc_ref, ddw_ref, dcpad, dppad,
             dshift):
        i = pl.program_id(0)
        tix = nt - 1 - i

        @pl.when(i == 0)
        def _():
            gpw_ref[...] = jnp.zeros(gpw_ref.shape, F32)
            gwg_ref[...] = jnp.zeros(gwg_ref.shape, F32)
            vd_ref[...] = jnp.zeros(vd_ref.shape, F32)
            vc_ref[...] = jnp.zeros(vc_ref.shape, F32)
            ddw_ref[...] = jnp.zeros(ddw_ref.shape, F32)
            dcpad[ts:ts + CONV_HALO, :] = jnp.zeros((CONV_HALO, CONV_W), F32)
            dppad[ts:ts + POOL_HALO, :] = jnp.zeros((POOL_HALO, POOL_W), F32)

        col = lambda a: jnp.sum(a, axis=0, keepdims=True)
        sh1 = mod_ref[0:1, :]
        sc1 = mod_ref[1:2, :]
        gt1 = mod_ref[2:3, :]
        dx2t = dx2_ref[...]
        vd_ref[0:1, :] += col(dx2t * y_ref[...])
        dyb = (dx2t * gt1).astype(MXU_DTYPE)
        dy_ref[...] = dyb
        dycat = _dot_nt(dyb, wout_ref[...])
        dyconv = dycat[:, :CONV_W]
        dypool = dycat[:, CONV_W:]

        pt = p_ref[...]
        t = tix * ts + lax.broadcasted_iota(jnp.int32, (ts, 1), 0)
        psc = ps_ref[...]
        dypb = (dypool * psc).astype(MXU_DTYPE)
        dps, ypre = [], []
        for gi, w in enumerate(POOL_WINDOWS):
            cols = slice(gi * POOL_G, (gi + 1) * POOL_G)
            gwg_ref[gi] += _dot_tn(pt[:, cols], dypb[:, cols])
            ypre.append(_dot(pt[:, cols], wg_ref[gi]))
            dpg = _dot_nt(dypb[:, cols], wg_ref[gi])
            dps.append(dpg)
            cnt = jnp.minimum(t + 1, w).astype(F32)
            dppad[0:ts, cols] = dpg / cnt
        vc_ref[0:1, :] += col(dypool * jnp.concatenate(ypre, axis=1))
        dvs = []
        for gi, w in enumerate(POOL_WINDOWS):
            cols = slice(gi * POOL_G, (gi + 1) * POOL_G)
            acc = dppad[0:ts, cols]
            for d in range(1, w):
                acc = acc + dppad[d:d + ts, cols]
            dvs.append(acc - dps[gi])
        dv = jnp.concatenate(dvs, axis=1)
        dppad[ts:ts + POOL_HALO, :] = dppad[0:POOL_HALO, :]

        zt = z_ref[...]
        lngv = lng_ref[...]
        ln = zt * lngv + lnb_ref[...]
        sg = _sigmoid(ln)
        swb = (ln * sg).astype(MXU_DTYPE)
        dycb = dyconv.astype(MXU_DTYPE)
        gpw_ref[...] += _dot_tn(swb, dycb)
        dln = _dot_nt(dycb, wpw_ref[...]) * (sg * (1.0 + ln * (1.0 - sg)))
        vc_ref[1:2, :] += col(dln * zt)
        vc_ref[2:3, :] += col(dln)
        dz = dln * lngv
        dcv = rstd_ref[...] * (dz - jnp.mean(dz, axis=-1, keepdims=True)
                               - zt * jnp.mean(dz * zt, axis=-1, keepdims=True))
        vc_ref[3:4, :] += col(dcv)
        dcpad[0:ts, :] = dcv
        ut = u_ref[...]
        a = ut[:, :CONV_W]
        g = ut[:, CONV_W:2 * CONV_W]
        sgg = _sigmoid(g)
        glu = a * sgg
        window = _row_shifts(dcpad, dshift, ts + CONV_HALO - SUBLANES)
        dglu = jnp.zeros((ts, CONV_W), F32)
        for k in range(CONV_K):
            sh = window(CONV_K - 1 - k, ts)
            dglu = dglu + dww_ref[k:k + 1, :] * sh
            ddw_ref[k:k + 1, :] += col(glu * sh)
        dcpad[ts:ts + CONV_HALO, :] = dcpad[0:CONV_HALO, :]
        da = dglu * sgg
        dg = dglu * a * sgg * (1.0 - sgg)
        dub = jnp.concatenate([da, dg, dv], axis=1).astype(MXU_DTYPE)
        du_ref[...] = dub
        cw = IN_W // N_CHIP
        dh1 = jnp.zeros((ts, D_MODEL), F32)
        for j in range(N_CHIP):
            dh1 = dh1 + _dot_nt(dub[:, j * cw:(j + 1) * cw], win_ref[j])

        xt = x_ref[...]
        g1v = g1_ref[...]
        r1 = lax.rsqrt(jnp.mean(xt * xt, axis=-1, keepdims=True) + EPS)
        xh1 = xt * r1
        n1 = xh1 * g1v
        h1_ref[...] = (n1 * (1.0 + sc1) + sh1).astype(MXU_DTYPE)
        vd_ref[1:2, :] += col(dh1)
        vd_ref[2:3, :] += col(dh1 * n1)
        dn1 = dh1 * (1.0 + sc1)
        vd_ref[3:4, :] += col(dn1 * xh1)
        dxh = dn1 * g1v
        gx_ref[...] = dx2t + r1 * (dxh - xh1 * jnp.mean(dxh * xh1, axis=-1, keepdims=True))

    tile = lambda w: pl.BlockSpec((ts, w), lambda i: (nt - 1 - i, 0))
    bf = lambda w: jax.ShapeDtypeStruct((s, w), MXU_DTYPE)
    return _call(
        body, name="mixer_bwd", grid=(nt,),
        in_specs=[tile(D_MODEL), tile(D_MODEL), tile(D_MODEL), tile(IN_W), tile(CONV_W), tile(1), tile(POOL_W),
                  _full(mod.shape), _full(g1.shape), _full(w_in.shape), _full(dww.shape), _full(lng.shape),
                  _full(lnb.shape), _full(w_pw.shape), _full(wg.shape), _full(pscale.shape), _full(w_out.shape)],
        out_specs=[tile(D_MODEL), tile(D_MODEL), tile(IN_W), tile(D_MODEL), _full((CONV_W, CONV_W)),
                   _full(wg.shape), _full((8, D_MODEL)), _full((8, CONV_W)), _full((32, CONV_W))],
        out_shape=[jax.ShapeDtypeStruct((s, D_MODEL), F32), bf(D_MODEL), bf(IN_W), bf(D_MODEL),
                   jax.ShapeDtypeStruct((CONV_W, CONV_W), F32),
                   jax.ShapeDtypeStruct(wg.shape, F32), jax.ShapeDtypeStruct((8, D_MODEL), F32),
                   jax.ShapeDtypeStruct((8, CONV_W), F32), jax.ShapeDtypeStruct((32, CONV_W), F32)],
        scratch_shapes=[pltpu.VMEM((ts + CONV_HALO, CONV_W), F32), pltpu.VMEM((ts + POOL_HALO, POOL_W), F32),
                        pltpu.VMEM((SUBLANES - 1, ts + CONV_HALO - SUBLANES, CONV_W), F32)],
        args=(dx2, x, y, u, z, rstd, p, mod, g1, w_in, dww, lng, lnb, w_pw, wg, pscale, w_out), comm=comm,
        after=after)


def _dw(name, a, a_spec, b, b_spec, nb, mb, nbk, comm=None, after=()):
    def body(a_ref, b_ref, o_ref):
        av = a_ref[...]
        bv = b_ref[...]
        av = av.reshape(av.shape[-2:])
        bv = bv.reshape(bv.shape[-2:])
        o_ref[0] = _dot_tn(av, bv)

    (out,), rest = _call(
        body, name=name, grid=(nb,), in_specs=[a_spec, b_spec],
        out_specs=[pl.BlockSpec((1, mb, nbk), lambda j: (j, 0, 0))],
        out_shape=[jax.ShapeDtypeStruct((nb, mb, nbk), F32)], args=(a, b), comm=comm, after=after)
    return out, rest


def _dw_mixer(ycat, dy, h1, du, after=()):
    s = ycat.shape[0]

    def body(ycat_ref, dy_ref, h1_ref, du_ref, out_ref, in_ref):
        out_ref[0] = _dot_tn(ycat_ref[...], dy_ref[...])
        in_ref[0] = _dot_tn(h1_ref[...], du_ref[...])

    whole = lambda w: pl.BlockSpec((s, w), lambda j: (0, 0))
    cols = lambda w: pl.BlockSpec((s, w), lambda j: (0, j))
    blk = lambda m, n: pl.BlockSpec((1, m, n), lambda j: (j, 0, 0))
    shapes = [(D_MODEL // N_CHIP, D_MODEL), (D_MODEL, IN_W // N_CHIP)]
    res, _ = _call(
        body, name="dw_mixer", grid=(N_CHIP,),
        in_specs=[cols(D_MODEL // N_CHIP), whole(D_MODEL), whole(D_MODEL), cols(IN_W // N_CHIP)],
        out_specs=[blk(m, n) for m, n in shapes],
        out_shape=[jax.ShapeDtypeStruct((N_CHIP, m, n), F32) for m, n in shapes],
        args=(ycat, dy, h1, du), after=after)
    return res


def _ada_fwd(c, w_ada, b4, first, later, dww, wg, comm):
    nc = w_ada.shape[1]
    nf, nl = len(first), len(later)
    shards = list(first) + list(later)

    def body(start_comm, mid_comm, gathered, c_ref, w_ref, b4_ref, *refs):
        shard_refs, refs = refs[:nf + nl], refs[nf + nl:]
        dww_ref, wg_ref, mod_ref, cact_ref, wgb_ref = refs[:5]
        later_refs, refs = refs[5:5 + nl], refs[5 + nl:]
        call, part, parts = refs[:3]
        stages, refs = refs[3:3 + nf + nl], refs[3 + nf + nl:]
        fetched, refs = refs[:nl], refs[nl:]
        send1, recv1, send2, recv2, lsem, fsem = refs
        x, y, cc = _place()
        b = 4 * x + 2 * y + cc
        j = 2 * x + y
        fetches = [pltpu.make_async_copy(shard_refs[nf + a], fetched[a], fsem.at[a]) for a in range(nl)]
        for cp in fetches:
            cp.start()

        def slot_copies(lo, hi):
            cps = []
            for a in range(lo, hi):
                dst = gathered[a] if a < nf else later_refs[a - nf]
                cps.append(pltpu.make_async_copy(stages[a], dst.at[j], lsem.at[a]))
            return cps

        call[b] = c_ref[...]
        sends = []
        for r in range(1, N_DEV):
            dev = ((1 - x) if r & 4 else x, (1 - y) if r & 2 else y, (1 - cc) if r & 1 else cc)
            cp = _remote(call.at[b], call.at[b], send1.at[r - 1], recv1.at[r - 1], dev)
            cp.start()
            sends.append(cp)
        for a in range(nf):
            stages[a][...] = shard_refs[a][...].astype(MXU_DTYPE)
        dww_copy = pltpu.make_async_copy(dww_ref, gathered[nf].at[j], lsem.at[nf + nl])
        dww_copy.start()
        for cp in slot_copies(0, nf):
            cp.start()
        for r in range(1, N_DEV):
            src_b = lax.bitwise_xor(b, r)
            _remote(call.at[src_b], call.at[src_b], send1.at[r - 1], recv1.at[r - 1], (x, y, cc)).wait_recv()
        for cp in sends:
            cp.wait_send()
        for cp in slot_copies(0, nf):
            cp.wait()
        dww_copy.wait()
        start_comm()
        for i in range(N_DEV):
            ci = call[i]
            cact_ref[i:i + 1, :] = ci * _sigmoid(ci)
        part[...] = jnp.dot(cact_ref[...], w_ref[...], preferred_element_type=F32, precision=lax.Precision.HIGHEST)
        sends = []
        for r in range(1, N_CHIP):
            kx, ky = _flip(x, y, r)
            cp = _remote(part, parts.at[j], send2.at[r - 1], recv2.at[r - 1], (kx, ky, cc))
            cp.start()
            sends.append(cp)
        parts[j] = part[...]
        for a in range(nl):
            fetches[a].wait()
            stages[nf + a][...] = fetched[a][...].astype(MXU_DTYPE)
        for cp in slot_copies(nf, nf + nl):
            cp.start()
        wgb_ref[...] = wg_ref[...].astype(MXU_DTYPE)
        mid_comm()
        for r in range(1, N_CHIP):
            kx, ky = _flip(x, y, r)
            kj = 2 * kx + ky
            _remote(part, parts.at[kj], send2.at[r - 1], recv2.at[r - 1], (x, y, cc)).wait_recv()
        for cp in sends:
            cp.wait_send()
        mine = lax.broadcasted_iota(jnp.int32, (N_DEV, 1), 0) == b
        for k in range(N_CHIP):
            row = jnp.sum(jnp.where(mine, parts[k], 0.0), axis=0, keepdims=True) + b4_ref[k:k + 1, :]
            lo = k * nc
            while lo < (k + 1) * nc:
                q, at = lo // D_MODEL, lo % D_MODEL
                n = min(D_MODEL - at, (k + 1) * nc - lo)
                mod_ref[q:q + 1, at:at + n] = row[:, lo - k * nc:lo - k * nc + n]
                lo += n
        for cp in slot_copies(nf, nf + nl):
            cp.wait()

    res, rest = _call(
        body, name="ada_fwd", grid=(1,),
        in_specs=[VMEM] * (3 + nf) + [ANY] * nl + [VMEM] * 2, out_specs=[VMEM, VMEM, VMEM] + [ANY] * nl,
        out_shape=[jax.ShapeDtypeStruct((N_CHIP * nc // D_MODEL, D_MODEL), F32),
                   jax.ShapeDtypeStruct((N_DEV, D_MODEL), F32),
                   jax.ShapeDtypeStruct(wg.shape, MXU_DTYPE)]
        + [jax.ShapeDtypeStruct((N_CHIP,) + a.shape, MXU_DTYPE) for a in later],
        scratch_shapes=[pltpu.VMEM((N_DEV, 1, D_MODEL), F32), pltpu.VMEM((N_DEV, nc), F32),
                        pltpu.VMEM((N_CHIP, N_DEV, nc), F32)]
        + [pltpu.VMEM(a.shape, MXU_DTYPE) for a in shards] + [pltpu.VMEM(a.shape, F32) for a in later]
        + [pltpu.SemaphoreType.DMA((N_DEV - 1,)), pltpu.SemaphoreType.DMA((N_DEV - 1,)),
           pltpu.SemaphoreType.DMA((N_CHIP - 1,)), pltpu.SemaphoreType.DMA((N_CHIP - 1,)),
           pltpu.SemaphoreType.DMA((nf + nl + 1,)), pltpu.SemaphoreType.DMA((nl,))],
        args=(c, w_ada, b4, *shards, dww, wg), comm=comm, body_starts=True)
    return (res[0], res[1], res[2], res[3:]), rest


def _chip_partials(name, place, gs, rs, comm=None, after=()):
    n = len(gs)

    def body(pref, *refs):
        g_refs, r_refs = refs[:n], refs[n:2 * n]
        pb_refs, own_refs = refs[2 * n:3 * n], refs[3 * n:]
        jj = pl.program_id(0)
        for a in range(n):
            sm = g_refs[a][0] + r_refs[a][0]
            pb_refs[a][0] = sm.astype(MXU_DTYPE)

            @pl.when(jj == pref[1])
            def _(a=a, sm=sm):
                own_refs[a][...] = sm

    halves = [(g.shape[1] // 2, g.shape[2]) for g in gs]
    in_specs = [pl.BlockSpec((1, h, w), lambda jj, pref: (jj, pref[0], 0)) for h, w in halves]
    in_specs += [pl.BlockSpec((1, h, w), lambda jj, pref: (jj, 0, 0)) for h, w in halves]
    out_specs = [pl.BlockSpec((1, h, w), lambda jj, pref: (jj, 0, 0)) for h, w in halves]
    out_specs += [pl.BlockSpec((h, w), lambda jj, pref: (0, 0)) for h, w in halves]
    out, rest = _call(
        body, name=name, grid=(N_CHIP,), in_specs=in_specs, out_specs=out_specs,
        out_shape=[jax.ShapeDtypeStruct((N_CHIP, h, w), MXU_DTYPE) for h, w in halves]
        + [jax.ShapeDtypeStruct((h, w), F32) for h, w in halves],
        args=(*gs, *rs), prefetch=(place,), comm=comm, after=after)
    return (out[:n], out[n:]), rest


def _sum_partials(name, place, owns, recvd, comm=None, after=()):
    n = len(owns)

    def body(pref, *refs):
        o_refs, r_refs, out_refs = refs[:n], refs[n:2 * n], refs[2 * n:]
        for a in range(n):
            acc = o_refs[a][...]
            for r in range(N_CHIP - 1):
                acc = acc + r_refs[a][r].astype(F32)
            out_refs[a][...] = acc

    full = lambda a: pl.BlockSpec(a.shape, lambda i, pref: (0,) * a.ndim)
    return _call(
        body, name=name, grid=(1,), in_specs=[full(a) for a in list(owns) + list(recvd)],
        out_specs=[pl.BlockSpec(o.shape, lambda i, pref: (pref[0], 0)) for o in owns],
        out_shape=[jax.ShapeDtypeStruct((2 * o.shape[0], o.shape[1]), F32) for o in owns],
        args=(*owns, *recvd), prefetch=(place,), comm=comm, after=after)


def _adamw_math(w, g, m, v):
    m = ADAM_B1 * m + (1.0 - ADAM_B1) * g
    v = ADAM_B2 * v + (1.0 - ADAM_B2) * (g * g)
    m_hat = m / (1.0 - ADAM_B1 ** ADAM_STEP)
    v_hat = v / (1.0 - ADAM_B2 ** ADAM_STEP)
    delta = -ADAM_LR * (m_hat / (jnp.sqrt(v_hat) + ADAM_EPS) + ADAM_WD * w)
    return delta, m, v


def _row_tile(rows):
    for t in (512, 352, 256, 128):
        if rows % t == 0:
            return t
    return rows


def _adamw(name, wgmv, steps, after=()):
    n = len(wgmv)

    def body(*refs):
        ins, outs = refs[:4 * n], refs[4 * n:]
        for i in range(n):
            w_ref, g_ref, m_ref, v_ref = ins[4 * i:4 * i + 4]
            d_ref, nm_ref, nv_ref = outs[3 * i:3 * i + 3]
            d_ref[...], nm_ref[...], nv_ref[...] = _adamw_math(w_ref[...], g_ref[...], m_ref[...], v_ref[...])

    in_specs, out_specs, out_shape, args = [], [], [], []
    for w, g, m, v in wgmv:
        rows, cols = w.shape
        spec = pl.BlockSpec((rows // steps, cols), lambda i: (i, 0))
        in_specs += [spec] * 4
        out_specs += [spec] * 3
        out_shape += [jax.ShapeDtypeStruct(w.shape, F32)] * 3
        args += [w, g, m, v]
    res, _ = _call(body, name=name, grid=(steps,), in_specs=in_specs, out_specs=out_specs, out_shape=out_shape,
                   args=args, after=after)
    return [res[3 * i:3 * i + 3] for i in range(n)]


def _adamw_ada(place, cact, dmod, w, m, v, after=()):
    rows, cols = w.shape
    tr = _row_tile(rows)

    def body(pref, ca_ref, dm_ref, w_ref, m_ref, v_ref, g_ref, d_ref, nm_ref, nv_ref):
        g = lax.dot_general(ca_ref[...], dm_ref[...], (((0,), (0,)), ((), ())), preferred_element_type=F32,
                            precision=lax.Precision.HIGHEST)
        g_ref[...] = g
        d_ref[...], nm_ref[...], nv_ref[...] = _adamw_math(w_ref[...], g, m_ref[...], v_ref[...])

    spec = pl.BlockSpec((tr, cols), lambda i, pref: (i, 0))
    return _call(
        body, name="adamw_ada", grid=(rows // tr,),
        in_specs=[pl.BlockSpec((N_DEV, tr), lambda i, pref: (0, i)),
                  pl.BlockSpec((N_DEV, cols), lambda i, pref: (0, pref[1])), spec, spec, spec],
        out_specs=[spec] * 4, out_shape=[jax.ShapeDtypeStruct(w.shape, F32)] * 4,
        args=(cact, dmod, w, m, v), prefetch=(place,), after=after)[0]


def _adamw_small(place, owns, gathered, wmv):
    nw = len(wmv)
    flat = [a for t in wmv for a in t]

    def body(pref, *refs):
        own_refs, all_refs, refs = refs[:5], refs[5:10], refs[10:]
        w_refs = refs[:3 * nw]
        loss_ref, dmod_ref = refs[3 * nw], refs[3 * nw + 1]
        o_refs = refs[3 * nw + 2:]
        j = pref[1]
        me = 2 * pref[1] + pref[0]

        def total(i):
            acc = None
            for b in range(N_DEV):
                blk = jnp.where(me == b, own_refs[i][...], all_refs[i][b])
                acc = blk if acc is None else acc + blk
            return acc

        vf, vd, vc, ddw, gwg = [total(i) for i in range(5)]
        loss_ref[...] = (0.5 / D_MODEL) * jnp.sum(vf[5:6, :], axis=1, keepdims=True)
        order = ((1, 1), (1, 2), (1, 0), (0, 2), (0, 3), (0, 1))
        for b in range(N_DEV):
            for q, (i, row) in enumerate(order):
                dmod_ref[b:b + 1, q * D_MODEL:(q + 1) * D_MODEL] = jnp.where(
                    me == b, own_refs[i][row:row + 1, :], all_refs[i][b, row:row + 1, :])
        dm = dmod_ref[...]
        g_bada = dm[0:1, :]
        for b in range(1, N_DEV):
            g_bada = g_bada + dm[b:b + 1, :]
        g_dww = jnp.zeros((32, POOL_G), F32)
        for k in range(N_CHIP):
            g_dww = g_dww + jnp.where(j == k, ddw[:, k * POOL_G:(k + 1) * POOL_G], 0.0)
        grads = [g_bada, vd[3:4, :], g_dww, vc[3:4, :], vc[1:2, :], vc[2:3, :], gwg, vc[0:1, :], vf[4:5, :],
                 vf[0:1, :]]
        for i, g in enumerate(grads):
            w_ref, m_ref, v_ref = w_refs[3 * i:3 * i + 3]
            d, nm, nv = _adamw_math(w_ref[...], g, m_ref[...], v_ref[...])
            o_refs[4 * i][...] = g
            o_refs[4 * i + 1][...] = d
            o_refs[4 * i + 2][...] = nm
            o_refs[4 * i + 3][...] = nv

    outs = [jax.ShapeDtypeStruct((1, 1), F32), jax.ShapeDtypeStruct((N_DEV, 6 * D_MODEL), F32)]
    for w, _, _ in wmv:
        outs += [jax.ShapeDtypeStruct(w.shape, F32)] * 4
    full = lambda a: pl.BlockSpec(a.shape, lambda i, pref: (0,) * a.ndim)
    args = list(owns) + list(gathered) + flat
    res, _ = _call(body, name="adamw_small", grid=(1,), in_specs=[full(a) for a in args],
                   out_specs=[full(o) for o in outs], out_shape=outs, args=args, prefetch=(place,))
    return res[0], res[1], [res[2 + 4 * i:6 + 4 * i] for i in range(nw)]


def kernel(x, c, w_ada, b_ada, g_norm1, w_in, dw_w, dw_b, conv_ln_g, conv_ln_b, w_conv_pw, w_pool_group, pool_scale, w_out, g_norm2, w_ffn_gate, w_ffn_up, w_ffn_down, g_final, loss_target, m_w_ada, m_b_ada, m_g_norm1, m_w_in, m_dw_w, m_dw_b, m_conv_ln_g, m_conv_ln_b, m_w_conv_pw, m_w_pool_group, m_pool_scale, m_w_out, m_g_norm2, m_w_ffn_gate, m_w_ffn_up, m_w_ffn_down, m_g_final, v_w_ada, v_b_ada, v_g_norm1, v_w_in, v_dw_w, v_dw_b, v_conv_ln_g, v_conv_ln_b, v_w_conv_pw, v_w_pool_group, v_pool_scale, v_w_out, v_g_norm2, v_w_ffn_gate, v_w_ffn_up, v_w_ffn_down, v_g_final):
    xi, yi, ci = _place()
    place = jnp.stack([ci, 2 * xi + yi]).astype(jnp.int32)
    n_ada = w_ada.shape[2]

    tr = lambda a: jnp.transpose(a[0])
    mixer_shards = [w_in[0], w_conv_pw[0], w_out[0]]
    ffn_shards = [tr(w_ffn_gate), tr(w_ffn_up), w_ffn_down[0]]
    slots = [lax.empty((N_CHIP,) + a.shape, MXU_DTYPE) for a in mixer_shards] + [lax.empty((N_CHIP,) + dw_w.shape[1:], F32)]

    (mod, cact, wg_b, (b_gate, b_up, b_down)), (win_g, wpw_g, wout_g, dww_g) = _ada_fwd(
        c, w_ada[0], b_ada.reshape(N_CHIP, n_ada), mixer_shards, ffn_shards, dw_w[0], w_pool_group[0],
        comm=_weights_gather(slots, [True, True, True, False]))
    dww_full = jnp.pad(jnp.concatenate([dww_g[k] for k in range(N_CHIP)], axis=1), ((0, 1), (0, 0)))
    w_pw = wpw_g.reshape(CONV_W, CONV_W)
    w_o = wout_g.reshape(D_MODEL, D_MODEL)
    xs, tgt, gf = x[0], loss_target[0], g_final.reshape(1, D_MODEL)
    s = xs.shape[0]
    fb = b_gate.shape[1]

    (x2, y, u, z, rstd, p, ycat), (wgate_g, wup_g, wdown_g) = _mixer_fwd(
        xs, mod, g_norm1, win_g, dww_full, dw_b, conv_ln_g, conv_ln_b, w_pw, wg_b, pool_scale, w_o,
        comm=_weights_gather([b_gate, b_up, b_down], [True, True, True]))
    dx2, h2, df, act, dgg, duu, vec_f = _ffn(x2, tgt, mod, g_norm2, gf, wgate_g, wup_g, wdown_g)

    whole = lambda w: pl.BlockSpec((s, w), lambda j: (0, 0))
    hid = pl.BlockSpec((1, s, fb), lambda j: (j, 0, 0))
    c_gate, _ = _dw("dw_gate", dgg, hid, h2, whole(D_MODEL), N_CHIP, fb, D_MODEL)
    c_up, (r_gate,) = _dw("dw_up", duu, hid, h2, whole(D_MODEL), N_CHIP, fb, D_MODEL, comm=_sibling_halves([c_gate]))
    ((pb_gate,), (own_gate,)), _ = _chip_partials("partials_gate", place, [c_gate], [r_gate])
    ex_gate = _exchange_parts([pb_gate])
    st_gate, tok_gate = _split_start("exchange_gate_start", [pb_gate], *ex_gate)
    c_down, (r_up,) = _dw("dw_down", act, hid, df, whole(D_MODEL), N_CHIP, fb, D_MODEL,
                          comm=_sibling_halves([c_up]), after=(tok_gate,))
    sib_down = _sibling_parts([c_down])
    st_sd, tok_sd = _split_start("sibling_down_start", [c_down], *sib_down)
    ((pb_up,), (own_up,)), _ = _chip_partials("partials_up", place, [c_up], [r_up], after=(tok_sd,))
    (c_down,), (r_down,) = _split_wait("sibling_down_wait", st_sd, 1, sib_down[1], sib_down[2], after=(pb_up,),
                                       with_sources=True)
    ((pb_down,), (own_down,)), _ = _chip_partials("partials_down", place, [c_down], [r_down])
    ex_ud = _exchange_parts([pb_up, pb_down])
    st_ud, tok_ud = _split_start("exchange_up_down_start", [pb_up, pb_down], *ex_ud)
    (gx, h1, du, dy, c_pw, g_wg, vec_d, vec_c, ddw), _ = _mixer_bwd(
        dx2, xs, y, u, z, rstd, p, mod, g_norm1, win_g, dww_full, conv_ln_g, conv_ln_b, w_pw, wg_b, pool_scale, w_o,
        after=(tok_ud,))

    small_own = [vec_f, vec_d, vec_c, ddw, g_wg]
    ex_small = _small_parts(small_own)
    st_small, tok_small = _split_start("small_grads_start", small_own, *ex_small, zeroed=True)
    c_out, c_in = _dw_mixer(ycat, dy, h1, du, after=(tok_small,))
    mix = [c_in, c_pw.reshape(N_CHIP, CONV_W // N_CHIP, CONV_W), c_out]
    sib_mix = _sibling_parts(mix)
    st_sm, tok_sm = _split_start("sibling_mix_start", mix, *sib_mix)
    (rc_gate,) = _split_wait("exchange_gate_wait", st_gate, 1, ex_gate[1], ex_gate[2], after=(tok_sm,))
    rc_up, rc_down = _split_wait("exchange_up_down_wait", st_ud, 2, ex_ud[1], ex_ud[2], after=(tok_sm, rc_gate))

    ffn_fulls, _ = _sum_partials("sum_ffn", place, [own_gate, own_up, own_down], [rc_gate, rc_up, rc_down])
    join_sems, join_copies = _join_parts(ffn_fulls)
    join_in_place = lambda bufs, lands, send, recv: join_copies(bufs, send, recv)
    st_jf, tok_jf = _split_start("join_ffn_start", ffn_fulls, [], join_sems, join_in_place)
    mix, (r_in, r_pw, r_out) = _split_wait("sibling_mix_wait", st_sm, len(mix), sib_mix[1], sib_mix[2],
                                           after=(tok_jf,), with_sources=True)
    (pbs_mix, owns_mix), _ = _chip_partials("partials_mix", place, mix, [r_in, r_pw, r_out])
    g_gate, g_up, g_down = _split_wait("join_ffn_wait", st_jf, len(ffn_fulls), join_sems, join_in_place,
                                       after=(pbs_mix[0],), in_place=True)

    pad_rows = lambda a: jnp.pad(a[0], ((0, 1), (0, 0)))
    row = lambda a: a.reshape(1, -1)
    small = [(b_ada, m_b_ada, v_b_ada), (g_norm1, m_g_norm1, v_g_norm1),
             (pad_rows(dw_w), pad_rows(m_dw_w), pad_rows(v_dw_w)), (dw_b, m_dw_b, v_dw_b),
             (conv_ln_g, m_conv_ln_g, v_conv_ln_g), (conv_ln_b, m_conv_ln_b, v_conv_ln_b),
             (w_pool_group[0], m_w_pool_group[0], v_w_pool_group[0]), (pool_scale, m_pool_scale, v_pool_scale),
             (g_norm2, m_g_norm2, v_g_norm2), (row(g_final), row(m_g_final), row(v_g_final))]
    lead = lambda outs: [a[None] for a in outs]

    lands, sem_shape, copies = _exchange_parts(pbs_mix)
    state, token = _split_start("exchange_mix_start", pbs_mix, lands, sem_shape, copies)
    u_gate, u_up, u_down = _adamw(
        "adamw_ffn", [(tr(w_ffn_gate), g_gate, tr(m_w_ffn_gate), tr(v_w_ffn_gate)),
                      (tr(w_ffn_up), g_up, tr(m_w_ffn_up), tr(v_w_ffn_up)),
                      (w_ffn_down[0], g_down, m_w_ffn_down[0], v_w_ffn_down[0])],
        steps=4, after=(token,))
    o_gate = [jnp.transpose(o) for o in [g_gate] + list(u_gate)]
    o_up = [jnp.transpose(o) for o in [g_up] + list(u_up)]
    o_down = [g_down] + list(u_down)
    small_own, small_all = _split_wait("small_grads_wait", st_small, len(small_own), ex_small[1], ex_small[2],
                                       after=(u_down[0],), with_sources=True)
    loss, dmod, small_out = _adamw_small(place, small_own, small_all, small)
    (o_bada, o_g1, o_dww, o_dwb, o_lng, o_lnb, o_wg, o_ps, o_g2, o_gf) = small_out
    o_dww = [a[:CONV_K] for a in o_dww]
    o_gf = [a.reshape(D_MODEL) for a in o_gf]
    o_ada = _adamw_ada(place, cact, dmod, w_ada[0], m_w_ada[0], v_w_ada[0], after=(token,))
    rc_mix = _split_wait("exchange_mix_wait", state, len(pbs_mix), sem_shape, copies, after=(o_ada[1], u_down[0]))
    mix_fulls, _ = _sum_partials("sum_mix", place, owns_mix, rc_mix)
    g_in, g_pw, g_out = _comm_only("join_mix", _join_halves(mix_fulls))
    u_in, u_pw, u_out = _adamw(
        "adamw_mix", [(w_in[0], g_in, m_w_in[0], v_w_in[0]), (w_conv_pw[0], g_pw, m_w_conv_pw[0], v_w_conv_pw[0]),
                      (w_out[0], g_out, m_w_out[0], v_w_out[0])], steps=4)
    o_in, o_pw, o_out = [g_in] + list(u_in), [g_pw] + list(u_pw), [g_out] + list(u_out)

    per_weight = [lead(o_ada), o_bada, o_g1, lead(o_in), lead(o_dww), o_dwb, o_lng, o_lnb, lead(o_pw), lead(o_wg),
                  o_ps, lead(o_out), o_g2, lead(o_gate), lead(o_up), lead(o_down), o_gf]
    result = [loss.reshape(()), gx[None]]
    for kind in range(4):
        result += [o[kind] for o in per_weight]
    return tuple(result)
```

```python
import functools

import jax
import jax.numpy as jnp
from jax import lax
from jax.experimental import pallas as pl
from jax.experimental.pallas import tpu as pltpu

F32 = jnp.float32
MXU_DTYPE = jnp.bfloat16
EPS = 1e-6

D_MODEL = 1024
CONV_W = 512
POOL_W = 512
CONV_K = 31
POOL_WINDOWS = (2, 4, 8, 16)
POOL_G = 128
IN_W = 2 * CONV_W + POOL_W
N_CHIP = 4
N_DEV = 8
CONV_HALO = 32
POOL_HALO = 16

ADAM_LR = 0.001
ADAM_B1 = 0.9
ADAM_B2 = 0.999
ADAM_EPS = 1e-08
ADAM_WD = 0.01
ADAM_STEP = 10

MESH = pl.DeviceIdType.MESH
ANY = pl.BlockSpec(memory_space=pl.ANY)
VMEM = pl.BlockSpec(memory_space=pltpu.VMEM)


def _dot(a, b):
    return jnp.dot(a.astype(MXU_DTYPE), b.astype(MXU_DTYPE), preferred_element_type=F32)


def _dot_nt(a, b):
    return lax.dot_general(a.astype(MXU_DTYPE), b.astype(MXU_DTYPE), (((1,), (1,)), ((), ())),
                           preferred_element_type=F32)


def _dot_tn(a, b):
    return lax.dot_general(a.astype(MXU_DTYPE), b.astype(MXU_DTYPE), (((0,), (0,)), ((), ())),
                           preferred_element_type=F32)


def _sigmoid(v):
    return 1.0 / (1.0 + jnp.exp(-v))


def _full(shape):
    n = len(shape)
    return pl.BlockSpec(shape, lambda *_: (0,) * n)


def _token_tile(s):
    return 256 if s % 256 == 0 else s


SUBLANES = 8


def _row_shifts(pad_ref, shifted_ref, rows):
    for r in range(1, SUBLANES):
        shifted_ref[r - 1] = pad_ref[r:r + rows, :]

    def window(i, n):
        r, base = i % SUBLANES, i - i % SUBLANES
        if r == 0:
            return pad_ref[base:base + n, :]
        return shifted_ref[r - 1, base:base + n, :]

    return window


def _place():
    return lax.axis_index("x"), lax.axis_index("y"), lax.axis_index("c")


def _flip(x, y, r):
    return ((1 - x) if r & 2 else x, (1 - y) if r & 1 else y)


def _remote(src, dst, send_sem, recv_sem, dev):
    return pltpu.make_async_remote_copy(src_ref=src, dst_ref=dst, send_sem=send_sem, recv_sem=recv_sem,
                                        device_id=dev, device_id_type=MESH)


class _Comm:
    def __init__(self, ins, outs, aliases, scratch, start, finish, mid=None):
        self.ins, self.outs, self.aliases, self.scratch = list(ins), list(outs), dict(aliases), list(scratch)
        self.start, self.finish = start, finish
        self.mid = mid


def _both(a, b):
    na, nao, nas = len(a.ins), len(a.outs), len(a.scratch)
    aliases = dict(a.aliases)
    aliases.update({na + i: nao + o for i, o in b.aliases.items()})

    def start(ins, outs, scr):
        a.start(ins[:na], outs[:nao], scr[:nas])
        b.start(ins[na:], outs[nao:], scr[nas:])

    def finish(ins, outs, scr):
        a.finish(ins[:na], outs[:nao], scr[:nas])
        b.finish(ins[na:], outs[nao:], scr[nas:])

    def mid(ins, outs, scr):
        if a.mid:
            a.mid(ins[:na], outs[:nao], scr[:nas])
        if b.mid:
            b.mid(ins[na:], outs[nao:], scr[nas:])

    return _Comm(a.ins + b.ins, a.outs + b.outs, aliases, a.scratch + b.scratch, start, finish,
                 mid if (a.mid or b.mid) else None)


def _call(body, *, name, grid, in_specs, out_specs, out_shape, args, scratch_shapes=(), prefetch=(), comm=None,
          body_starts=False, after=()):
    in_specs = list(in_specs) + [ANY] * len(after)
    args = list(args) + list(after)
    n_pre, n_in, n_out, n_scr = len(prefetch), len(in_specs), len(out_specs), len(scratch_shapes)
    n_body_in = n_in - len(after)
    c_ins = comm.ins if comm else []
    c_outs = comm.outs if comm else []
    c_scr = comm.scratch if comm else []
    last = grid[0] - 1

    def wrapped(*refs):
        pre, refs = refs[:n_pre], refs[n_pre:]
        ins, cin = refs[:n_body_in], refs[n_in:n_in + len(c_ins)]
        refs = refs[n_in + len(c_ins):]
        outs, cout = refs[:n_out], refs[n_out:n_out + len(c_outs)]
        refs = refs[n_out + len(c_outs):]
        scr, cscr = refs[:n_scr], refs[n_scr:]
        step = pl.program_id(0)
        if comm and not body_starts:
            @pl.when(step == 0)
            def _():
                comm.start(cin, cout, cscr)

        has_mid = comm is not None and comm.mid is not None
        mid_step = grid[0] // 2 if grid[0] >= 4 else None
        if has_mid and mid_step is not None:
            @pl.when(step == mid_step)
            def _():
                comm.mid(cin, cout, cscr)

        if body_starts:
            body(lambda: comm.start(cin, cout, cscr), lambda: comm.mid(cin, cout, cscr), cout,
                 *pre, *ins, *outs, *scr)
        else:
            body(*pre, *ins, *outs, *scr)
        if comm:
            @pl.when(step == last)
            def _():
                if has_mid and mid_step is None and not body_starts:
                    comm.mid(cin, cout, cscr)
                comm.finish(cin, cout, cscr)

    aliases = {n_pre + n_in + a: n_out + b for a, b in (comm.aliases if comm else {}).items()}
    res = pl.pallas_call(
        wrapped, name=name,
        grid_spec=pltpu.PrefetchScalarGridSpec(
            num_scalar_prefetch=n_pre, grid=grid, in_specs=list(in_specs) + [ANY] * len(c_ins),
            out_specs=list(out_specs) + [ANY] * len(c_outs), scratch_shapes=list(scratch_shapes) + list(c_scr)),
        out_shape=list(out_shape) + list(c_outs),
        input_output_aliases=aliases,
        compiler_params=pltpu.CompilerParams(dimension_semantics=("arbitrary",)),
    )(*prefetch, *args, *c_ins)
    return res[:n_out], res[n_out:]


def _comm_only(name, comm):
    return _call(lambda: None, name=name, grid=(1,), in_specs=[], out_specs=[], out_shape=[], args=[], comm=comm)[1]


def _weights_gather(bufs, split):
    n = len(bufs)

    def ctx(outs):
        x, y, cc = _place()
        chips = dict(me=2 * x + y, y=2 * x + (1 - y), x=2 * (1 - x) + y, d=2 * (1 - x) + (1 - y))
        devs = dict(y=(x, 1 - y, cc), x=(1 - x, y, cc), d=(1 - x, 1 - y, cc), s=(x, y, 1 - cc))

        def piece(a, kj, pc, q=None):
            if not split[a]:
                return outs[a].at[kj]
            h = bufs[a].shape[1] // 2
            if q is None:
                return outs[a].at[kj, pl.ds(pc * h, h), :]
            return outs[a].at[kj, pl.ds(pc * h + q * (h // 2), h // 2), :]

        return cc, chips, devs, piece

    def directs(a, outs, send, recv):
        cc, chips, devs, piece = ctx(outs)
        if not split[a]:
            whole = piece(a, chips["me"], cc)
            return [_remote(whole, whole, send.at[a, k], recv.at[a, k], devs[t]) for k, t in ((0, "y"), (2, "x"), (4, "d"))]
        q = lambda i: piece(a, chips["me"], cc, i)
        return [_remote(q(0), q(0), send.at[a, 0], recv.at[a, 0], devs["y"]),
                _remote(q(1), q(1), send.at[a, 3], recv.at[a, 3], devs["x"]),
                _remote(q(1), q(1), send.at[a, 1], recv.at[a, 1], devs["y"]),
                _remote(q(0), q(0), send.at[a, 2], recv.at[a, 2], devs["x"])]

    def landed(a, k, outs, send, recv):
        cc, chips, devs, piece = ctx(outs)
        if not split[a]:
            got = piece(a, chips[{0: "y", 2: "x", 4: "d"}[k]], cc)
        elif k < 6:
            got = piece(a, chips[("y", "y", "x", "x", "d", "d")[k]], cc, (0, 1, 0, 1, 0, 1)[k])
        else:
            got = piece(a, chips[("y", "x", "d")[k - 6]], 1 - cc)
        return _remote(got, got, send.at[a, k], recv.at[a, k], devs["s"])

    def passed_on(a, outs, send, recv):
        cc, chips, devs, piece = ctx(outs)
        from_y, from_x = piece(a, chips["y"], cc, 0), piece(a, chips["x"], cc, 1)
        return [_remote(from_y, from_y, send.at[a, 4], recv.at[a, 4], devs["x"]),
                _remote(from_x, from_x, send.at[a, 5], recv.at[a, 5], devs["y"])]

    def to_sibling(a, outs, send, recv, which=(0, 1, 2)):
        cc, chips, devs, piece = ctx(outs)
        halves = [piece(a, chips[("y", "x", "d")[i]], cc) for i in which]
        return [_remote(hf, hf, send.at[a, 6 + i], recv.at[a, 6 + i], devs["s"]) for i, hf in zip(which, halves)]

    def start(ins, outs, scr):
        send, recv = scr
        per_item = [directs(a, outs, send, recv) for a in range(n)]
        for rank in range(4):
            for cps in per_item:
                if rank < len(cps):
                    cps[rank].start()

    def mid(ins, outs, scr):
        send, recv = scr
        for a in range(n):
            if split[a]:
                fy, fx = passed_on(a, outs, send, recv)
                landed(a, 0, outs, send, recv).wait_recv()
                fy.start()
                landed(a, 3, outs, send, recv).wait_recv()
                fx.start()

    def finish(ins, outs, scr):
        send, recv = scr
        for a in range(n):
            if split[a]:
                to_y, to_x = to_sibling(a, outs, send, recv, which=(0, 1))
                landed(a, 1, outs, send, recv).wait_recv()
                to_y.start()
                landed(a, 2, outs, send, recv).wait_recv()
                to_x.start()
        for a in range(n):
            if split[a]:
                for k in (4, 5):
                    landed(a, k, outs, send, recv).wait_recv()
                to_sibling(a, outs, send, recv, which=(2,))[0].start()
            else:
                for k in (0, 2, 4):
                    landed(a, k, outs, send, recv).wait_recv()
        for a in range(n):
            if split[a]:
                for k in (6, 7, 8):
                    landed(a, k, outs, send, recv).wait_recv()
            cps = directs(a, outs, send, recv)
            if split[a]:
                cps += passed_on(a, outs, send, recv) + to_sibling(a, outs, send, recv)
            for cp in cps:
                cp.wait_send()

    return _Comm(bufs, [jax.ShapeDtypeStruct(b.shape, b.dtype) for b in bufs], {i: i for i in range(n)},
                 [pltpu.SemaphoreType.DMA((n, 9)), pltpu.SemaphoreType.DMA((n, 9))], start, finish, mid)


HBM =pl.BlockSpec(memory_space=pltpu.HBM)
SEM = pl.BlockSpec(memory_space=pltpu.SEMAPHORE)
DATAFLOW = pltpu.SideEffectType.DATAFLOW_SIDE_EFFECTING


class _SemGrid:
    def __init__(self, refs, cols):
        self.refs, self.cols = refs, cols

    @property
    def at(self):
        return self

    def __getitem__(self, idx):
        return self.refs[idx[0] * self.cols + idx[1]]


def _split_start(name, srcs, lands, sem_shape, copies, zeroed=False):
    n, k = len(srcs), len(lands)
    ns = sem_shape[0] * sem_shape[1]

    def body(*refs):
        src_refs, land_refs = refs[:n], refs[n:n + k]
        send = _SemGrid(refs[n + k:n + k + ns], sem_shape[1])
        recv = _SemGrid(refs[n + k + ns:n + k + 2 * ns], sem_shape[1])
        token = refs[-1]
        for cp in copies(src_refs, land_refs, send, recv):
            cp.start()
        token[...] = jnp.zeros(token.shape, F32)

    hbm = lambda a: pltpu.with_memory_space_constraint(a, pltpu.HBM)
    zones = [jnp.zeros(l.shape, l.dtype) if zeroed else lax.empty(l.shape, l.dtype) for l in lands]
    out = pl.pallas_call(
        body, name=name,
        out_shape=[pltpu.SemaphoreType.DMA(())] * (2 * ns)
        + [pltpu.HBM(a.shape, a.dtype) for a in list(srcs) + list(lands)] + [jax.ShapeDtypeStruct((8, 128), F32)],
        in_specs=[HBM] * (n + k), out_specs=[SEM] * (2 * ns) + [HBM] * (n + k) + [VMEM],
        input_output_aliases={i: 2 * ns + i for i in range(n + k)},
        compiler_params=pltpu.CompilerParams(has_side_effects=DATAFLOW),
    )(*[hbm(a) for a in srcs], *[hbm(z) for z in zones])
    return out[:-1], out[-1]


def _split_wait(name, state, n, sem_shape, copies, after, in_place=False, with_sources=False):
    ns = sem_shape[0] * sem_shape[1]
    sems, bufs = state[:2 * ns], state[2 * ns:]
    k = len(bufs) - n

    def body(*refs):
        src_refs, land_refs = refs[:n], refs[n:n + k]
        send = _SemGrid(refs[n + k:n + k + ns], sem_shape[1])
        recv = _SemGrid(refs[n + k + ns:n + k + 2 * ns], sem_shape[1])
        cps = copies(src_refs, land_refs, send, recv)
        for cp in cps:
            cp.wait_send()
        for cp in cps:
            cp.wait_recv()

    out = pl.pallas_call(
        body, name=name,
        out_shape=[pltpu.HBM(a.shape, a.dtype) for a in bufs],
        in_specs=[HBM] * (n + k) + [SEM] * (2 * ns) + [ANY] * len(after), out_specs=[HBM] * (n + k),
        input_output_aliases={i: i for i in range(n + k)},
        compiler_params=pltpu.CompilerParams(has_side_effects=DATAFLOW),
    )(*bufs, *sems, *after)
    if with_sources:
        return out[:n], out[n:]
    return out[:n] if in_place else out[n:]


def _direct_phases(copies):
    def start(ins, outs, scr):
        for cp in copies(ins, outs, *scr):
            cp.start()

    def finish(ins, outs, scr):
        cps = copies(ins, outs, *scr)
        for cp in cps:
            cp.wait_recv()
        for cp in cps:
            cp.wait_send()

    return start, finish


def _sibling_parts(gs):
    n = len(gs)

    def copies(ins, outs, send, recv):
        x, y, cc = _place()
        cps = []
        for a in range(n):
            h = gs[a].shape[1] // 2
            cps.append(_remote(ins[a].at[:, pl.ds((1 - cc) * h, h), :], outs[a], send.at[a, 0], recv.at[a, 0],
                               (x, y, 1 - cc)))
        return cps

    return [jax.ShapeDtypeStruct((N_CHIP, g.shape[1] // 2, g.shape[2]), F32) for g in gs], (n, 1), copies


def _sibling_halves(gs):
    lands, sem_shape, copies = _sibling_parts(gs)
    start, finish = _direct_phases(copies)
    return _Comm(gs, lands, {}, [pltpu.SemaphoreType.DMA(sem_shape), pltpu.SemaphoreType.DMA(sem_shape)],
                 start, finish)


def _exchange_parts(pbs):
    n = len(pbs)

    def copies(ins, outs, send, recv):
        x, y, cc = _place()
        cps = []
        for a in range(n):
            for r in range(1, N_CHIP):
                kx, ky = _flip(x, y, r)
                cps.append(_remote(ins[a].at[2 * kx + ky], outs[a].at[r - 1], send.at[a, r - 1], recv.at[a, r - 1],
                                   (kx, ky, cc)))
        return cps

    lands = [jax.ShapeDtypeStruct((N_CHIP - 1,) + p.shape[1:], p.dtype) for p in pbs]
    return lands, (n, N_CHIP - 1), copies


def _small_parts(arrs):
    n = len(arrs)

    def copies(ins, outs, send, recv):
        x, y, cc = _place()
        b = 4 * x + 2 * y + cc
        cps = []
        for a in range(n):
            for r in range(1, N_DEV):
                dev = ((1 - x) if r & 4 else x, (1 - y) if r & 2 else y, (1 - cc) if r & 1 else cc)
                cps.append(_remote(ins[a], outs[a].at[b], send.at[a, r - 1], recv.at[a, r - 1], dev))
        return cps

    lands = [jax.ShapeDtypeStruct((N_DEV,) + a.shape, a.dtype) for a in arrs]
    return lands, (n, N_DEV - 1), copies


def _exchange_partials(pbs):
    lands, sem_shape, copies = _exchange_parts(pbs)
    start, finish = _direct_phases(copies)
    return _Comm(pbs, lands, {}, [pltpu.SemaphoreType.DMA(sem_shape), pltpu.SemaphoreType.DMA(sem_shape)],
                 start, finish)


def _join_parts(fulls):
    n = len(fulls)

    def copies(bufs, send, recv):
        x, y, cc = _place()
        cps = []
        for a in range(n):
            h = fulls[a].shape[0] // 2
            mine = bufs[a].at[pl.ds(cc * h, h), :]
            cps.append(_remote(mine, mine, send.at[a, 0], recv.at[a, 0], (x, y, 1 - cc)))
        return cps

    return (n, 1), copies


def _join_halves(fulls):
    sem_shape, copies = _join_parts(fulls)
    start, finish = _direct_phases(lambda ins, outs, send, recv: copies(outs, send, recv))
    return _Comm(fulls, [jax.ShapeDtypeStruct(f.shape, F32) for f in fulls], {i: i for i in range(len(fulls))},
                 [pltpu.SemaphoreType.DMA(sem_shape), pltpu.SemaphoreType.DMA(sem_shape)], start, finish)


def _mixer_fwd(x, mod, g1, w_in, dww, dwb, lng, lnb, w_pw, wg, pscale, w_out, comm=None):
    s = x.shape[0]
    ts = _token_tile(s)
    nt = s // ts

    def body(x_ref, mod_ref, g1_ref, win_ref, dww_ref, dwb_ref, lng_ref, lnb_ref, wpw_ref, wg_ref, ps_ref,
             wout_ref, x2_ref, y_ref, u_ref, z_ref, rstd_ref, p_ref, ycat_ref, gpad, vpad, gshift):
        i = pl.program_id(0)

        @pl.when(i == 0)
        def _():
            gpad[0:CONV_HALO, :] = jnp.zeros((CONV_HALO, CONV_W), F32)
            vpad[0:POOL_HALO, :] = jnp.zeros((POOL_HALO, POOL_W), F32)

        xt = x_ref[...]
        sh1 = mod_ref[0:1, :]
        sc1 = mod_ref[1:2, :]
        gt1 = mod_ref[2:3, :]
        r1 = lax.rsqrt(jnp.mean(xt * xt, axis=-1, keepdims=True) + EPS)
        h1 = (xt * r1 * g1_ref[...]) * (1.0 + sc1) + sh1
        h1b = h1.astype(MXU_DTYPE)
        u = jnp.concatenate([_dot(h1b, win_ref[j]) for j in range(N_CHIP)], axis=1)
        u_ref[...] = u
        a = u[:, :CONV_W]
        g = u[:, CONV_W:2 * CONV_W]
        v = u[:, 2 * CONV_W:]

        gpad[CONV_HALO:CONV_HALO + ts, :] = a * _sigmoid(g)
        window = _row_shifts(gpad, gshift, ts + CONV_HALO - SUBLANES)
        cv = jnp.broadcast_to(dwb_ref[...], (ts, CONV_W))
        off = CONV_HALO - (CONV_K - 1)
        for k in range(CONV_K):
            cv = cv + dww_ref[k:k + 1, :] * window(off + k, ts)
        gpad[0:CONV_HALO, :] = gpad[ts:ts + CONV_HALO, :]

        mu = jnp.mean(cv, axis=-1, keepdims=True)
        cc = cv - mu
        rstd = lax.rsqrt(jnp.mean(cc * cc, axis=-1, keepdims=True) + EPS)
        z = cc * rstd
        z_ref[...] = z
        rstd_ref[...] = rstd
        ln = z * lng_ref[...] + lnb_ref[...]
        sw = ln * _sigmoid(ln)
        yconv = _dot(sw, wpw_ref[...])

        vpad[POOL_HALO:POOL_HALO + ts, :] = v
        t = i * ts + lax.broadcasted_iota(jnp.int32, (ts, 1), 0)
        ps, ypool = [], []
        for gi, w in enumerate(POOL_WINDOWS):
            cols = slice(gi * POOL_G, (gi + 1) * POOL_G)
            acc = vpad[POOL_HALO:POOL_HALO + ts, cols]
            for d in range(1, w):
                acc = acc + vpad[POOL_HALO - d:POOL_HALO - d + ts, cols]
            cnt = jnp.minimum(t + 1, w).astype(F32)
            pg = (acc / cnt - v[:, cols]).astype(MXU_DTYPE)
            ps.append(pg)
            ypool.append(_dot(pg, wg_ref[gi]))
        vpad[0:POOL_HALO, :] = vpad[ts:ts + POOL_HALO, :]
        p_ref[...] = jnp.concatenate(ps, axis=1)
        ypool = jnp.concatenate(ypool, axis=1) * ps_ref[...]

        ycat = jnp.concatenate([yconv, ypool], axis=1).astype(MXU_DTYPE)
        ycat_ref[...] = ycat
        y = _dot(ycat, wout_ref[...])
        y_ref[...] = y
        x2_ref[...] = xt + gt1 * y

    tile = lambda w: pl.BlockSpec((ts, w), lambda i: (i, 0))
    return _call(
        body, name="mixer_fwd", grid=(nt,),
        in_specs=[tile(D_MODEL), _full(mod.shape), _full(g1.shape), _full(w_in.shape), _full(dww.shape),
                  _full(dwb.shape), _full(lng.shape), _full(lnb.shape), _full(w_pw.shape), _full(wg.shape),
                  _full(pscale.shape), _full(w_out.shape)],
        out_specs=[tile(D_MODEL), tile(D_MODEL), tile(IN_W), tile(CONV_W), tile(1), tile(POOL_W), tile(D_MODEL)],
        out_shape=[jax.ShapeDtypeStruct((s, D_MODEL), F32), jax.ShapeDtypeStruct((s, D_MODEL), F32),
                   jax.ShapeDtypeStruct((s, IN_W), F32), jax.ShapeDtypeStruct((s, CONV_W), F32),
                   jax.ShapeDtypeStruct((s, 1), F32), jax.ShapeDtypeStruct((s, POOL_W), MXU_DTYPE),
                   jax.ShapeDtypeStruct((s, D_MODEL), MXU_DTYPE)],
        scratch_shapes=[pltpu.VMEM((ts + CONV_HALO, CONV_W), F32), pltpu.VMEM((ts + POOL_HALO, POOL_W), F32),
                        pltpu.VMEM((SUBLANES - 1, ts + CONV_HALO - SUBLANES, CONV_W), F32)],
        args=(x, mod, g1, w_in, dww, dwb, lng, lnb, w_pw, wg, pscale, w_out), comm=comm)


def _ffn(x2, tgt, mod, g2, gf, w_gate, w_up, w_down):
    s = x2.shape[0]
    ts = _token_tile(s)
    nt = s // ts
    fb = w_gate.shape[1]

    def body(x2_ref, tgt_ref, mod_ref, g2_ref, gf_ref, wgt_ref, wup_ref, wdn_ref,
             dx2_ref, h2_ref, df_ref, act_ref, dgg_ref, duu_ref, vec_ref, gg_s, uu_s):
        i = pl.program_id(0)

        @pl.when(i == 0)
        def _():
            vec_ref[...] = jnp.zeros(vec_ref.shape, F32)

        x2t = x2_ref[...]
        sh2 = mod_ref[3:4, :]
        sc2 = mod_ref[4:5, :]
        gt2 = mod_ref[5:6, :]
        g2v = g2_ref[...]
        gfv = gf_ref[...]
        r2 = lax.rsqrt(jnp.mean(x2t * x2t, axis=-1, keepdims=True) + EPS)
        xh2 = x2t * r2
        n2 = xh2 * g2v
        h2b = (n2 * (1.0 + sc2) + sh2).astype(MXU_DTYPE)
        h2_ref[...] = h2b
        f = jnp.zeros((ts, D_MODEL), F32)
        for j in range(N_CHIP):
            gg = _dot_nt(h2b, wgt_ref[j])
            uu = _dot_nt(h2b, wup_ref[j])
            gg_s[j] = gg
            uu_s[j] = uu
            actb = (gg * _sigmoid(gg) * uu).astype(MXU_DTYPE)
            act_ref[j] = actb
            f = f + _dot(actb, wdn_ref[j])
        x3 = x2t + gt2 * f
        r3 = lax.rsqrt(jnp.mean(x3 * x3, axis=-1, keepdims=True) + EPS)
        xh3 = x3 * r3
        diff = xh3 * gfv - tgt_ref[...]
        dout = diff * (1.0 / D_MODEL)
        dn3 = dout * gfv
        dx3 = r3 * (dn3 - xh3 * jnp.mean(dn3 * xh3, axis=-1, keepdims=True))
        dfb = (dx3 * gt2).astype(MXU_DTYPE)
        df_ref[...] = dfb
        dh2 = jnp.zeros((ts, D_MODEL), F32)
        for j in range(N_CHIP):
            dact = _dot_nt(dfb, wdn_ref[j])
            gg = gg_s[j]
            uu = uu_s[j]
            sg = _sigmoid(gg)
            duu = (dact * (gg * sg)).astype(MXU_DTYPE)
            dgg = (dact * uu * (sg * (1.0 + gg * (1.0 - sg)))).astype(MXU_DTYPE)
            duu_ref[j] = duu
            dgg_ref[j] = dgg
            dh2 = dh2 + _dot(dgg, wgt_ref[j]) + _dot(duu, wup_ref[j])
        dn2 = dh2 * (1.0 + sc2)
        dxh2 = dn2 * g2v
        dx2_ref[...] = dx3 + r2 * (dxh2 - xh2 * jnp.mean(dxh2 * xh2, axis=-1, keepdims=True))

        col = lambda a: jnp.sum(a, axis=0, keepdims=True)
        vec_ref[0:1, :] += col(dout * xh3)
        vec_ref[1:2, :] += col(dx3 * f)
        vec_ref[2:3, :] += col(dh2)
        vec_ref[3:4, :] += col(dh2 * n2)
        vec_ref[4:5, :] += col(dn2 * xh2)
        vec_ref[5:6, :] += col(diff * diff)

    tile = lambda w: pl.BlockSpec((ts, w), lambda i: (i, 0))
    tile3 = pl.BlockSpec((N_CHIP, ts, fb), lambda i: (0, i, 0))
    once = lambda a: pl.BlockSpec(a.shape, lambda i: (0,) * a.ndim, pipeline_mode=pl.Buffered(1))
    hid = jax.ShapeDtypeStruct((N_CHIP, s, fb), MXU_DTYPE)
    return pl.pallas_call(
        body, name="ffn", grid=(nt,),
        in_specs=[tile(D_MODEL), tile(D_MODEL), _full(mod.shape), _full(g2.shape), _full(gf.shape),
                  once(w_gate), once(w_up), once(w_down)],
        out_specs=[tile(D_MODEL), tile(D_MODEL), tile(D_MODEL), tile3, tile3, tile3, _full((8, D_MODEL))],
        out_shape=[jax.ShapeDtypeStruct((s, D_MODEL), F32), jax.ShapeDtypeStruct((s, D_MODEL), MXU_DTYPE),
                   jax.ShapeDtypeStruct((s, D_MODEL), MXU_DTYPE), hid, hid, hid,
                   jax.ShapeDtypeStruct((8, D_MODEL), F32)],
        scratch_shapes=[pltpu.VMEM((N_CHIP, ts, fb), F32), pltpu.VMEM((N_CHIP, ts, fb), F32)],
        compiler_params=pltpu.CompilerParams(dimension_semantics=("arbitrary",)),
    )(x2, tgt, mod, g2, gf, w_gate, w_up, w_down)


def _mixer_bwd(dx2, x, y, u, z, rstd, p, mod, g1, w_in, dww, lng, lnb, w_pw, wg, pscale, w_out, comm=None, after=()):
    s = x.shape[0]
    ts = _token_tile(s)
    nt = s // ts

    def body(dx2_ref, x_ref, y_ref, u_ref, z_ref, rstd_ref, p_ref, mod_ref, g1_ref, win_ref, dww_ref, lng_ref,
             lnb_ref, wpw_ref, wg_ref, ps_ref, wout_ref,
             gx_ref, h1_ref, du_ref, dy_ref, gpw_ref, gwg_ref, vd_ref, vc_ref, ddw_ref, dcpad, dppad,
             dshift):
        i = pl.program_id(0)
        tix = nt - 1 - i

        @pl.when(i == 0)
        def _():
            gpw_ref[...] = jnp.zeros(gpw_ref.shape, F32)
            gwg_ref[...] = jnp.zeros(gwg_ref.shape, F32)
            vd_ref[...] = jnp.zeros(vd_ref.shape, F32)
            vc_ref[...] = jnp.zeros(vc_ref.shape, F32)
            ddw_ref[...] = jnp.zeros(ddw_ref.shape, F32)
            dcpad[ts:ts + CONV_HALO, :] = jnp.zeros((CONV_HALO, CONV_W), F32)
            dppad[ts:ts + POOL_HALO, :] = jnp.zeros((POOL_HALO, POOL_W), F32)

        col = lambda a: jnp.sum(a, axis=0, keepdims=True)
        sh1 = mod_ref[0:1, :]
        sc1 = mod_ref[1:2, :]
        gt1 = mod_ref[2:3, :]
        dx2t = dx2_ref[...]
        vd_ref[0:1, :] += col(dx2t * y_ref[...])
        dyb = (dx2t * gt1).astype(MXU_DTYPE)
        dy_ref[...] = dyb
        dycat = _dot_nt(dyb, wout_ref[...])
        dyconv = dycat[:, :CONV_W]
        dypool = dycat[:, CONV_W:]

        pt = p_ref[...]
        t = tix * ts + lax.broadcasted_iota(jnp.int32, (ts, 1), 0)
        psc = ps_ref[...]
        dypb = (dypool * psc).astype(MXU_DTYPE)
        dps, ypre = [], []
        for gi, w in enumerate(POOL_WINDOWS):
            cols = slice(gi * POOL_G, (gi + 1) * POOL_G)
            gwg_ref[gi] += _dot_tn(pt[:, cols], dypb[:, cols])
            ypre.append(_dot(pt[:, cols], wg_ref[gi]))
            dpg = _dot_nt(dypb[:, cols], wg_ref[gi])
            dps.append(dpg)
            cnt = jnp.minimum(t + 1, w).astype(F32)
            dppad[0:ts, cols] = dpg / cnt
        vc_ref[0:1, :] += col(dypool * jnp.concatenate(ypre, axis=1))
        dvs = []
        for gi, w in enumerate(POOL_WINDOWS):
            cols = slice(gi * POOL_G, (gi + 1) * POOL_G)
            acc = dppad[0:ts, cols]
            for d in range(1, w):
                acc = acc + dppad[d:d + ts, cols]
            dvs.append(acc - dps[gi])
        dv = jnp.concatenate(dvs, axis=1)
        dppad[ts:ts + POOL_HALO, :] = dppad[0:POOL_HALO, :]

        zt = z_ref[...]
        lngv = lng_ref[...]
        ln = zt * lngv + lnb_ref[...]
        sg = _sigmoid(ln)
        swb = (ln * sg).astype(MXU_DTYPE)
        dycb = dyconv.astype(MXU_DTYPE)
        gpw_ref[...] += _dot_tn(swb, dycb)
        dln = _dot_nt(dycb, wpw_ref[...]) * (sg * (1.0 + ln * (1.0 - sg)))
        vc_ref[1:2, :] += col(dln * zt)
        vc_ref[2:3, :] += col(dln)
        dz = dln * lngv
        dcv = rstd_ref[...] * (dz - jnp.mean(dz, axis=-1, keepdims=True)
                               - zt * jnp.mean(dz * zt, axis=-1, keepdims=True))
        vc_ref[3:4, :] += col(dcv)
        dcpad[0:ts, :] = dcv
        ut = u_ref[...]
        a = ut[:, :CONV_W]
        g = ut[:, CONV_W:2 * CONV_W]
        sgg = _sigmoid(g)
        glu = a * sgg
        window = _row_shifts(dcpad, dshift, ts + CONV_HALO - SUBLANES)
        dglu = jnp.zeros((ts, CONV_W), F32)
        for k in range(CONV_K):
            sh = window(CONV_K - 1 - k, ts)
            dglu = dglu + dww_ref[k:k + 1, :] * sh
            ddw_ref[k:k + 1, :] += col(glu * sh)
        dcpad[ts:ts + CONV_HALO, :] = dcpad[0:CONV_HALO, :]
        da = dglu * sgg
        dg = dglu * a * sgg * (1.0 - sgg)
        dub = jnp.concatenate([da, dg, dv], axis=1).astype(MXU_DTYPE)
        du_ref[...] = dub
        cw = IN_W // N_CHIP
        dh1 = jnp.zeros((ts, D_MODEL), F32)
        for j in range(N_CHIP):
            dh1 = dh1 + _dot_nt(dub[:, j * cw:(j + 1) * cw], win_ref[j])

        xt = x_ref[...]
        g1v = g1_ref[...]
        r1 = lax.rsqrt(jnp.mean(xt * xt, axis=-1, keepdims=True) + EPS)
        xh1 = xt * r1
        n1 = xh1 * g1v
        h1_ref[...] = (n1 * (1.0 + sc1) + sh1).astype(MXU_DTYPE)
        vd_ref[1:2, :] += col(dh1)
        vd_ref[2:3, :] += col(dh1 * n1)
        dn1 = dh1 * (1.0 + sc1)
        vd_ref[3:4, :] += col(dn1 * xh1)
        dxh = dn1 * g1v
        gx_ref[...] = dx2t + r1 * (dxh - xh1 * jnp.mean(dxh * xh1, axis=-1, keepdims=True))

    tile = lambda w: pl.BlockSpec((ts, w), lambda i: (nt - 1 - i, 0))
    bf = lambda w: jax.ShapeDtypeStruct((s, w), MXU_DTYPE)
    return _call(
        body, name="mixer_bwd", grid=(nt,),
        in_specs=[tile(D_MODEL), tile(D_MODEL), tile(D_MODEL), tile(IN_W), tile(CONV_W), tile(1), tile(POOL_W),
                  _full(mod.shape), _full(g1.shape), _full(w_in.shape), _full(dww.shape), _full(lng.shape),
                  _full(lnb.shape), _full(w_pw.shape), _full(wg.shape), _full(pscale.shape), _full(w_out.shape)],
        out_specs=[tile(D_MODEL), tile(D_MODEL), tile(IN_W), tile(D_MODEL), _full((CONV_W, CONV_W)),
                   _full(wg.shape), _full((8, D_MODEL)), _full((8, CONV_W)), _full((32, CONV_W))],
        out_shape=[jax.ShapeDtypeStruct((s, D_MODEL), F32), bf(D_MODEL), bf(IN_W), bf(D_MODEL),
                   jax.ShapeDtypeStruct((CONV_W, CONV_W), F32),
                   jax.ShapeDtypeStruct(wg.shape, F32), jax.ShapeDtypeStruct((8, D_MODEL), F32),
                   jax.ShapeDtypeStruct((8, CONV_W), F32), jax.ShapeDtypeStruct((32, CONV_W), F32)],
        scratch_shapes=[pltpu.VMEM((ts + CONV_HALO, CONV_W), F32), pltpu.VMEM((ts + POOL_HALO, POOL_W), F32),
                        pltpu.VMEM((SUBLANES - 1, ts + CONV_HALO - SUBLANES, CONV_W), F32)],
        args=(dx2, x, y, u, z, rstd, p, mod, g1, w_in, dww, lng, lnb, w_pw, wg, pscale, w_out), comm=comm,
        after=after)


def _dw(name, a, a_spec, b, b_spec, nb, mb, nbk, comm=None, after=()):
    def body(a_ref, b_ref, o_ref):
        av = a_ref[...]
        bv = b_ref[...]
        av = av.reshape(av.shape[-2:])
        bv = bv.reshape(bv.shape[-2:])
        o_ref[0] = _dot_tn(av, bv)

    (out,), rest = _call(
        body, name=name, grid=(nb,), in_specs=[a_spec, b_spec],
        out_specs=[pl.BlockSpec((1, mb, nbk), lambda j: (j, 0, 0))],
        out_shape=[jax.ShapeDtypeStruct((nb, mb, nbk), F32)], args=(a, b), comm=comm, after=after)
    return out, rest


def _dw_mixer(ycat, dy, h1, du, after=()):
    s = ycat.shape[0]

    def body(ycat_ref, dy_ref, h1_ref, du_ref, out_ref, in_ref):
        out_ref[0] = _dot_tn(ycat_ref[...], dy_ref[...])
        in_ref[0] = _dot_tn(h1_ref[...], du_ref[...])

    whole = lambda w: pl.BlockSpec((s, w), lambda j: (0, 0))
    cols = lambda w: pl.BlockSpec((s, w), lambda j: (0, j))
    blk = lambda m, n: pl.BlockSpec((1, m, n), lambda j: (j, 0, 0))
    shapes = [(D_MODEL // N_CHIP, D_MODEL), (D_MODEL, IN_W // N_CHIP)]
    res, _ = _call(
        body, name="dw_mixer", grid=(N_CHIP,),
        in_specs=[cols(D_MODEL // N_CHIP), whole(D_MODEL), whole(D_MODEL), cols(IN_W // N_CHIP)],
        out_specs=[blk(m, n) for m, n in shapes],
        out_shape=[jax.ShapeDtypeStruct((N_CHIP, m, n), F32) for m, n in shapes],
        args=(ycat, dy, h1, du), after=after)
    return res


def _ada_fwd(c, w_ada, b4, first, later, dww, wg, comm):
    nc = w_ada.shape[1]
    nf, nl = len(first), len(later)
    shards = list(first) + list(later)

    def body(start_comm, mid_comm, gathered, c_ref, w_ref, b4_ref, *refs):
        shard_refs, refs = refs[:nf + nl], refs[nf + nl:]
        dww_ref, wg_ref, mod_ref, cact_ref, wgb_ref = refs[:5]
        later_refs, refs = refs[5:5 + nl], refs[5 + nl:]
        call, part, parts = refs[:3]
        stages, refs = refs[3:3 + nf + nl], refs[3 + nf + nl:]
        fetched, refs = refs[:nl], refs[nl:]
        w_vmem, send1, recv1, send2, recv2, lsem, fsem = refs
        x, y, cc = _place()
        b = 4 * x + 2 * y + cc
        j = 2 * x + y
        w_fetch = pltpu.make_async_copy(w_ref, w_vmem, fsem.at[nl])
        w_fetch.start()
        fetches = [pltpu.make_async_copy(shard_refs[nf + a], fetched[a], fsem.at[a]) for a in range(nl)]
        for cp in fetches:
            cp.start()

        def slot_copies(lo, hi):
            cps = []
            for a in range(lo, hi):
                dst = gathered[a] if a < nf else later_refs[a - nf]
                cps.append(pltpu.make_async_copy(stages[a], dst.at[j], lsem.at[a]))
            return cps

        call[b] = c_ref[...]
        sends = []
        for r in range(1, N_DEV):
            dev = ((1 - x) if r & 4 else x, (1 - y) if r & 2 else y, (1 - cc) if r & 1 else cc)
            cp = _remote(call.at[b], call.at[b], send1.at[r - 1], recv1.at[r - 1], dev)
            cp.start()
            sends.append(cp)
        for a in range(nf):
            stages[a][...] = shard_refs[a][...].astype(MXU_DTYPE)
        dww_copy = pltpu.make_async_copy(dww_ref, gathered[nf].at[j], lsem.at[nf + nl])
        dww_copy.start()
        for cp in slot_copies(0, nf):
            cp.start()
        for r in range(1, N_DEV):
            src_b = lax.bitwise_xor(b, r)
            _remote(call.at[src_b], call.at[src_b], send1.at[r - 1], recv1.at[r - 1], (x, y, cc)).wait_recv()
        for cp in sends:
            cp.wait_send()
        for cp in slot_copies(0, nf):
            cp.wait()
        dww_copy.wait()
        start_comm()
        for i in range(N_DEV):
            ci = call[i]
            cact_ref[i:i + 1, :] = ci * _sigmoid(ci)
        w_fetch.wait()
        part[...] = jnp.dot(cact_ref[...], w_vmem[...], preferred_element_type=F32, precision=lax.Precision.HIGHEST)
        sends = []
        for r in range(1, N_CHIP):
            kx, ky = _flip(x, y, r)
            cp = _remote(part, parts.at[j], send2.at[r - 1], recv2.at[r - 1], (kx, ky, cc))
            cp.start()
            sends.append(cp)
        parts[j] = part[...]
        for a in range(nl):
            fetches[a].wait()
            stages[nf + a][...] = fetched[a][...].astype(MXU_DTYPE)
        for cp in slot_copies(nf, nf + nl):
            cp.start()
        wgb_ref[...] = wg_ref[...].astype(MXU_DTYPE)
        mid_comm()
        for r in range(1, N_CHIP):
            kx, ky = _flip(x, y, r)
            kj = 2 * kx + ky
            _remote(part, parts.at[kj], send2.at[r - 1], recv2.at[r - 1], (x, y, cc)).wait_recv()
        for cp in sends:
            cp.wait_send()
        mine = lax.broadcasted_iota(jnp.int32, (N_DEV, 1), 0) == b
        for k in range(N_CHIP):
            row = jnp.sum(jnp.where(mine, parts[k], 0.0), axis=0, keepdims=True) + b4_ref[k:k + 1, :]
            lo = k * nc
            while lo < (k + 1) * nc:
                q, at = lo // D_MODEL, lo % D_MODEL
                n = min(D_MODEL - at, (k + 1) * nc - lo)
                mod_ref[q:q + 1, at:at + n] = row[:, lo - k * nc:lo - k * nc + n]
                lo += n
        for cp in slot_copies(nf, nf + nl):
            cp.wait()

    res, rest = _call(
        body, name="ada_fwd", grid=(1,),
        in_specs=[VMEM, ANY, VMEM] + [VMEM] * nf + [ANY] * nl + [VMEM] * 2,
        out_specs=[VMEM, VMEM, VMEM] + [ANY] * nl,
        out_shape=[jax.ShapeDtypeStruct((N_CHIP * nc // D_MODEL, D_MODEL), F32),
                   jax.ShapeDtypeStruct((N_DEV, D_MODEL), F32),
                   jax.ShapeDtypeStruct(wg.shape, MXU_DTYPE)]
        + [jax.ShapeDtypeStruct((N_CHIP,) + a.shape, MXU_DTYPE) for a in later],
        scratch_shapes=[pltpu.VMEM((N_DEV, 1, D_MODEL), F32), pltpu.VMEM((N_DEV, nc), F32),
                        pltpu.VMEM((N_CHIP, N_DEV, nc), F32)]
        + [pltpu.VMEM(a.shape, MXU_DTYPE) for a in shards] + [pltpu.VMEM(a.shape, F32) for a in later]
        + [pltpu.VMEM(w_ada.shape, F32),
           pltpu.SemaphoreType.DMA((N_DEV - 1,)), pltpu.SemaphoreType.DMA((N_DEV - 1,)),
           pltpu.SemaphoreType.DMA((N_CHIP - 1,)), pltpu.SemaphoreType.DMA((N_CHIP - 1,)),
           pltpu.SemaphoreType.DMA((nf + nl + 1,)), pltpu.SemaphoreType.DMA((nl + 1,))],
        args=(c, w_ada, b4, *shards, dww, wg), comm=comm, body_starts=True)
    return (res[0], res[1], res[2], res[3:]), rest


def _chip_partials(name, place, gs, rs, comm=None, after=()):
    n = len(gs)

    def body(pref, *refs):
        g_refs, r_refs = refs[:n], refs[n:2 * n]
        pb_refs, own_refs = refs[2 * n:3 * n], refs[3 * n:]
        jj = pl.program_id(0)
        for a in range(n):
            sm = g_refs[a][0] + r_refs[a][0]
            pb_refs[a][0] = sm.astype(MXU_DTYPE)

            @pl.when(jj == pref[1])
            def _(a=a, sm=sm):
                own_refs[a][...] = sm

    halves = [(g.shape[1] // 2, g.shape[2]) for g in gs]
    in_specs = [pl.BlockSpec((1, h, w), lambda jj, pref: (jj, pref[0], 0)) for h, w in halves]
    in_specs += [pl.BlockSpec((1, h, w), lambda jj, pref: (jj, 0, 0)) for h, w in halves]
    out_specs = [pl.BlockSpec((1, h, w), lambda jj, pref: (jj, 0, 0)) for h, w in halves]
    out_specs += [pl.BlockSpec((h, w), lambda jj, pref: (0, 0)) for h, w in halves]
    out, rest = _call(
        body, name=name, grid=(N_CHIP,), in_specs=in_specs, out_specs=out_specs,
        out_shape=[jax.ShapeDtypeStruct((N_CHIP, h, w), MXU_DTYPE) for h, w in halves]
        + [jax.ShapeDtypeStruct((h, w), F32) for h, w in halves],
        args=(*gs, *rs), prefetch=(place,), comm=comm, after=after)
    return (out[:n], out[n:]), rest


def _sum_partials(name, place, owns, recvd, comm=None, after=()):
    n = len(owns)

    def body(pref, *refs):
        o_refs, r_refs, out_refs = refs[:n], refs[n:2 * n], refs[2 * n:]
        for a in range(n):
            acc = o_refs[a][...]
            for r in range(N_CHIP - 1):
                acc = acc + r_refs[a][r].astype(F32)
            out_refs[a][...] = acc

    full = lambda a: pl.BlockSpec(a.shape, lambda i, pref: (0,) * a.ndim)
    return _call(
        body, name=name, grid=(1,), in_specs=[full(a) for a in list(owns) + list(recvd)],
        out_specs=[pl.BlockSpec(o.shape, lambda i, pref: (pref[0], 0)) for o in owns],
        out_shape=[jax.ShapeDtypeStruct((2 * o.shape[0], o.shape[1]), F32) for o in owns],
        args=(*owns, *recvd), prefetch=(place,), comm=comm, after=after)


def _adamw_math(w, g, m, v):
    m = ADAM_B1 * m + (1.0 - ADAM_B1) * g
    v = ADAM_B2 * v + (1.0 - ADAM_B2) * (g * g)
    m_hat = m / (1.0 - ADAM_B1 ** ADAM_STEP)
    v_hat = v / (1.0 - ADAM_B2 ** ADAM_STEP)
    delta = -ADAM_LR * (m_hat / (jnp.sqrt(v_hat) + ADAM_EPS) + ADAM_WD * w)
    return delta, m, v


def _row_tile(rows):
    for t in (512, 352, 256, 128):
        if rows % t == 0:
            return t
    return rows


def _adamw(name, wgmv, steps, after=()):
    n = len(wgmv)

    def body(*refs):
        ins, outs = refs[:4 * n], refs[4 * n:]
        for i in range(n):
            w_ref, g_ref, m_ref, v_ref = ins[4 * i:4 * i + 4]
            d_ref, nm_ref, nv_ref = outs[3 * i:3 * i + 3]
            d_ref[...], nm_ref[...], nv_ref[...] = _adamw_math(w_ref[...], g_ref[...], m_ref[...], v_ref[...])

    in_specs, out_specs, out_shape, args = [], [], [], []
    for w, g, m, v in wgmv:
        rows, cols = w.shape
        spec = pl.BlockSpec((rows // steps, cols), lambda i: (i, 0))
        in_specs += [spec] * 4
        out_specs += [spec] * 3
        out_shape += [jax.ShapeDtypeStruct(w.shape, F32)] * 3
        args += [w, g, m, v]
    res, _ = _call(body, name=name, grid=(steps,), in_specs=in_specs, out_specs=out_specs, out_shape=out_shape,
                   args=args, after=after)
    return [res[3 * i:3 * i + 3] for i in range(n)]


def _adamw_ada(place, cact, dmod, w, m, v, after=()):
    rows, cols = w.shape
    tr = _row_tile(rows)

    def body(pref, ca_ref, dm_ref, w_ref, m_ref, v_ref, g_ref, d_ref, nm_ref, nv_ref):
        g = lax.dot_general(ca_ref[...], dm_ref[...], (((0,), (0,)), ((), ())), preferred_element_type=F32,
                            precision=lax.Precision.HIGHEST)
        g_ref[...] = g
        d_ref[...], nm_ref[...], nv_ref[...] = _adamw_math(w_ref[...], g, m_ref[...], v_ref[...])

    spec = pl.BlockSpec((tr, cols), lambda i, pref: (i, 0))
    return _call(
        body, name="adamw_ada", grid=(rows // tr,),
        in_specs=[pl.BlockSpec((N_DEV, tr), lambda i, pref: (0, i)),
                  pl.BlockSpec((N_DEV, cols), lambda i, pref: (0, pref[1])), spec, spec, spec],
        out_specs=[spec] * 4, out_shape=[jax.ShapeDtypeStruct(w.shape, F32)] * 4,
        args=(cact, dmod, w, m, v), prefetch=(place,), after=after)[0]


def _adamw_small(place, owns, gathered, wmv):
    nw = len(wmv)
    flat = [a for t in wmv for a in t]

    def body(pref, *refs):
        own_refs, all_refs, refs = refs[:5], refs[5:10], refs[10:]
        w_refs = refs[:3 * nw]
        loss_ref, dmod_ref = refs[3 * nw], refs[3 * nw + 1]
        o_refs = refs[3 * nw + 2:]
        j = pref[1]
        me = 2 * pref[1] + pref[0]

        def total(i):
            acc = None
            for b in range(N_DEV):
                blk = jnp.where(me == b, own_refs[i][...], all_refs[i][b])
                acc = blk if acc is None else acc + blk
            return acc

        vf, vd, vc, ddw, gwg = [total(i) for i in range(5)]
        loss_ref[...] = (0.5 / D_MODEL) * jnp.sum(vf[5:6, :], axis=1, keepdims=True)
        order = ((1, 1), (1, 2), (1, 0), (0, 2), (0, 3), (0, 1))
        for b in range(N_DEV):
            for q, (i, row) in enumerate(order):
                dmod_ref[b:b + 1, q * D_MODEL:(q + 1) * D_MODEL] = jnp.where(
                    me == b, own_refs[i][row:row + 1, :], all_refs[i][b, row:row + 1, :])
        dm = dmod_ref[...]
        g_bada = dm[0:1, :]
        for b in range(1, N_DEV):
            g_bada = g_bada + dm[b:b + 1, :]
        g_dww = jnp.zeros((32, POOL_G), F32)
        for k in range(N_CHIP):
            g_dww = g_dww + jnp.where(j == k, ddw[:, k * POOL_G:(k + 1) * POOL_G], 0.0)
        grads = [g_bada, vd[3:4, :], g_dww, vc[3:4, :], vc[1:2, :], vc[2:3, :], gwg, vc[0:1, :], vf[4:5, :],
                 vf[0:1, :]]
        for i, g in enumerate(grads):
            w_ref, m_ref, v_ref = w_refs[3 * i:3 * i + 3]
            d, nm, nv = _adamw_math(w_ref[...], g, m_ref[...], v_ref[...])
            o_refs[4 * i][...] = g
            o_refs[4 * i + 1][...] = d
            o_refs[4 * i + 2][...] = nm
            o_refs[4 * i + 3][...] = nv

    outs = [jax.ShapeDtypeStruct((1, 1), F32), jax.ShapeDtypeStruct((N_DEV, 6 * D_MODEL), F32)]
    for w, _, _ in wmv:
        outs += [jax.ShapeDtypeStruct(w.shape, F32)] * 4
    full = lambda a: pl.BlockSpec(a.shape, lambda i, pref: (0,) * a.ndim)
    args = list(owns) + list(gathered) + flat
    res, _ = _call(body, name="adamw_small", grid=(1,), in_specs=[full(a) for a in args],
                   out_specs=[full(o) for o in outs], out_shape=outs, args=args, prefetch=(place,))
    return res[0], res[1], [res[2 + 4 * i:6 + 4 * i] for i in range(nw)]


def kernel(x, c, w_ada, b_ada, g_norm1, w_in, dw_w, dw_b, conv_ln_g, conv_ln_b, w_conv_pw, w_pool_group, pool_scale, w_out, g_norm2, w_ffn_gate, w_ffn_up, w_ffn_down, g_final, loss_target, m_w_ada, m_b_ada, m_g_norm1, m_w_in, m_dw_w, m_dw_b, m_conv_ln_g, m_conv_ln_b, m_w_conv_pw, m_w_pool_group, m_pool_scale, m_w_out, m_g_norm2, m_w_ffn_gate, m_w_ffn_up, m_w_ffn_down, m_g_final, v_w_ada, v_b_ada, v_g_norm1, v_w_in, v_dw_w, v_dw_b, v_conv_ln_g, v_conv_ln_b, v_w_conv_pw, v_w_pool_group, v_pool_scale, v_w_out, v_g_norm2, v_w_ffn_gate, v_w_ffn_up, v_w_ffn_down, v_g_final):
    xi, yi, ci = _place()
    place = jnp.stack([ci, 2 * xi + yi]).astype(jnp.int32)
    n_ada = w_ada.shape[2]

    tr = lambda a: jnp.transpose(a[0])
    mixer_shards = [w_in[0], w_conv_pw[0], w_out[0]]
    ffn_shards = [tr(w_ffn_gate), tr(w_ffn_up), w_ffn_down[0]]
    slots = [lax.empty((N_CHIP,) + a.shape, MXU_DTYPE) for a in mixer_shards] + [lax.empty((N_CHIP,) + dw_w.shape[1:], F32)]

    (mod, cact, wg_b, (b_gate, b_up, b_down)), (win_g, wpw_g, wout_g, dww_g) = _ada_fwd(
        c, w_ada[0], b_ada.reshape(N_CHIP, n_ada), mixer_shards, ffn_shards, dw_w[0], w_pool_group[0],
        comm=_weights_gather(slots, [True, True, True, False]))
    dww_full = jnp.pad(jnp.concatenate([dww_g[k] for k in range(N_CHIP)], axis=1), ((0, 1), (0, 0)))
    w_pw = wpw_g.reshape(CONV_W, CONV_W)
    w_o = wout_g.reshape(D_MODEL, D_MODEL)
    xs, tgt, gf = x[0], loss_target[0], g_final.reshape(1, D_MODEL)
    s = xs.shape[0]
    fb = b_gate.shape[1]

    (x2, y, u, z, rstd, p, ycat), (wgate_g, wup_g, wdown_g) = _mixer_fwd(
        xs, mod, g_norm1, win_g, dww_full, dw_b, conv_ln_g, conv_ln_b, w_pw, wg_b, pool_scale, w_o,
        comm=_weights_gather([b_gate, b_up, b_down], [True, True, True]))
    dx2, h2, df, act, dgg, duu, vec_f = _ffn(x2, tgt, mod, g_norm2, gf, wgate_g, wup_g, wdown_g)

    whole = lambda w: pl.BlockSpec((s, w), lambda j: (0, 0))
    hid = pl.BlockSpec((1, s, fb), lambda j: (j, 0, 0))
    c_gate, _ = _dw("dw_gate", dgg, hid, h2, whole(D_MODEL), N_CHIP, fb, D_MODEL)
    c_up, (r_gate,) = _dw("dw_up", duu, hid, h2, whole(D_MODEL), N_CHIP, fb, D_MODEL, comm=_sibling_halves([c_gate]))
    ((pb_gate,), (own_gate,)), _ = _chip_partials("partials_gate", place, [c_gate], [r_gate])
    ex_gate = _exchange_parts([pb_gate])
    st_gate, tok_gate = _split_start("exchange_gate_start", [pb_gate], *ex_gate)
    c_down, (r_up,) = _dw("dw_down", act, hid, df, whole(D_MODEL), N_CHIP, fb, D_MODEL,
                          comm=_sibling_halves([c_up]), after=(tok_gate,))
    sib_down = _sibling_parts([c_down])
    st_sd, tok_sd = _split_start("sibling_down_start", [c_down], *sib_down)
    ((pb_up,), (own_up,)), _ = _chip_partials("partials_up", place, [c_up], [r_up], after=(tok_sd,))
    (c_down,), (r_down,) = _split_wait("sibling_down_wait", st_sd, 1, sib_down[1], sib_down[2], after=(pb_up,),
                                       with_sources=True)
    ((pb_down,), (own_down,)), _ = _chip_partials("partials_down", place, [c_down], [r_down])
    ex_ud = _exchange_parts([pb_up, pb_down])
    st_ud, tok_ud = _split_start("exchange_up_down_start", [pb_up, pb_down], *ex_ud)
    (gx, h1, du, dy, c_pw, g_wg, vec_d, vec_c, ddw), _ = _mixer_bwd(
        dx2, xs, y, u, z, rstd, p, mod, g_norm1, win_g, dww_full, conv_ln_g, conv_ln_b, w_pw, wg_b, pool_scale, w_o,
        after=(tok_ud,))

    small_own = [vec_f, vec_d, vec_c, ddw, g_wg]
    ex_small = _small_parts(small_own)
    st_small, tok_small = _split_start("small_grads_start", small_own, *ex_small, zeroed=True)
    c_out, c_in = _dw_mixer(ycat, dy, h1, du, after=(tok_small,))
    mix = [c_in, c_pw.reshape(N_CHIP, CONV_W // N_CHIP, CONV_W), c_out]
    sib_mix = _sibling_parts(mix)
    st_sm, tok_sm = _split_start("sibling_mix_start", mix, *sib_mix)
    (rc_gate,) = _split_wait("exchange_gate_wait", st_gate, 1, ex_gate[1], ex_gate[2], after=(tok_sm,))
    rc_up, rc_down = _split_wait("exchange_up_down_wait", st_ud, 2, ex_ud[1], ex_ud[2], after=(tok_sm, rc_gate))

    ffn_fulls, _ = _sum_partials("sum_ffn", place, [own_gate, own_up, own_down], [rc_gate, rc_up, rc_down])
    join_sems, join_copies = _join_parts(ffn_fulls)
    join_in_place = lambda bufs, lands, send, recv: join_copies(bufs, send, recv)
    st_jf, tok_jf = _split_start("join_ffn_start", ffn_fulls, [], join_sems, join_in_place)
    mix, (r_in, r_pw, r_out) = _split_wait("sibling_mix_wait", st_sm, len(mix), sib_mix[1], sib_mix[2],
                                           after=(tok_jf,), with_sources=True)
    (pbs_mix, owns_mix), _ = _chip_partials("partials_mix", place, mix, [r_in, r_pw, r_out])
    g_gate, g_up, g_down = _split_wait("join_ffn_wait", st_jf, len(ffn_fulls), join_sems, join_in_place,
                                       after=(pbs_mix[0],), in_place=True)

    pad_rows = lambda a: jnp.pad(a[0], ((0, 1), (0, 0)))
    row = lambda a: a.reshape(1, -1)
    small = [(b_ada, m_b_ada, v_b_ada), (g_norm1, m_g_norm1, v_g_norm1),
             (pad_rows(dw_w), pad_rows(m_dw_w), pad_rows(v_dw_w)), (dw_b, m_dw_b, v_dw_b),
             (conv_ln_g, m_conv_ln_g, v_conv_ln_g), (conv_ln_b, m_conv_ln_b, v_conv_ln_b),
             (w_pool_group[0], m_w_pool_group[0], v_w_pool_group[0]), (pool_scale, m_pool_scale, v_pool_scale),
             (g_norm2, m_g_norm2, v_g_norm2), (row(g_final), row(m_g_final), row(v_g_final))]
    lead = lambda outs: [a[None] for a in outs]

    lands, sem_shape, copies = _exchange_parts(pbs_mix)
    state, token = _split_start("exchange_mix_start", pbs_mix, lands, sem_shape, copies)
    u_gate, u_up, u_down = _adamw(
        "adamw_ffn", [(tr(w_ffn_gate), g_gate, tr(m_w_ffn_gate), tr(v_w_ffn_gate)),
                      (tr(w_ffn_up), g_up, tr(m_w_ffn_up), tr(v_w_ffn_up)),
                      (w_ffn_down[0], g_down, m_w_ffn_down[0], v_w_ffn_down[0])],
        steps=4, after=(token,))
    o_gate = [jnp.transpose(o) for o in [g_gate] + list(u_gate)]
    o_up = [jnp.transpose(o) for o in [g_up] + list(u_up)]
    o_down = [g_down] + list(u_down)
    small_own, small_all = _split_wait("small_grads_wait", st_small, len(small_own), ex_small[1], ex_small[2],
                                       after=(u_down[0],), with_sources=True)
    loss, dmod, small_out = _adamw_small(place, small_own, small_all, small)
    (o_bada, o_g1, o_dww, o_dwb, o_lng, o_lnb, o_wg, o_ps, o_g2, o_gf) = small_out
    o_dww = [a[:CONV_K] for a in o_dww]
    o_gf = [a.reshape(D_MODEL) for a in o_gf]
    o_ada = _adamw_ada(place, cact, dmod, w_ada[0], m_w_ada[0], v_w_ada[0], after=(token,))
    rc_mix = _split_wait("exchange_mix_wait", state, len(pbs_mix), sem_shape, copies, after=(o_ada[1], u_down[0]))
    mix_fulls, _ = _sum_partials("sum_mix", place, owns_mix, rc_mix)
    g_in, g_pw, g_out = _comm_only("join_mix", _join_halves(mix_fulls))
    u_in, u_pw, u_out = _adamw(
        "adamw_mix", [(w_in[0], g_in, m_w_in[0], v_w_in[0]), (w_conv_pw[0], g_pw, m_w_conv_pw[0], v_w_conv_pw[0]),
                      (w_out[0], g_out, m_w_out[0], v_w_out[0])], steps=4)
    o_in, o_pw, o_out = [g_in] + list(u_in), [g_pw] + list(u_pw), [g_out] + list(u_out)

    per_weight = [lead(o_ada), o_bada, o_g1, lead(o_in), lead(o_dww), o_dwb, o_lng, o_lnb, lead(o_pw), lead(o_wg),
                  o_ps, lead(o_out), o_g2, lead(o_gate), lead(o_up), lead(o_down), o_gf]
    result = [loss.reshape(()), gx[None]]
    for kind in range(4):
        result += [o[kind] for o in per_weight]
    return tuple(result)
```

```python
import functools

import jax
import jax.numpy as jnp
from jax import lax
from jax.experimental import pallas as pl
from jax.experimental.pallas import tpu as pltpu

F32 = jnp.float32
MXU_DTYPE = jnp.bfloat16
EPS = 1e-6

D_MODEL = 1024
CONV_W = 512
POOL_W = 512
CONV_K = 31
POOL_WINDOWS = (2, 4, 8, 16)
POOL_G = 128
IN_W = 2 * CONV_W + POOL_W
N_CHIP = 4
N_DEV = 8
CONV_HALO = 32
POOL_HALO = 16

ADAM_LR = 0.001
ADAM_B1 = 0.9
ADAM_B2 = 0.999
ADAM_EPS = 1e-08
ADAM_WD = 0.01
ADAM_STEP = 10

MESH = pl.DeviceIdType.MESH
ANY = pl.BlockSpec(memory_space=pl.ANY)
VMEM = pl.BlockSpec(memory_space=pltpu.VMEM)


def _dot(a, b):
    return jnp.dot(a.astype(MXU_DTYPE), b.astype(MXU_DTYPE), preferred_element_type=F32)


def _dot_nt(a, b):
    return lax.dot_general(a.astype(MXU_DTYPE), b.astype(MXU_DTYPE), (((1,), (1,)), ((), ())),
                           preferred_element_type=F32)


def _dot_tn(a, b):
    return lax.dot_general(a.astype(MXU_DTYPE), b.astype(MXU_DTYPE), (((0,), (0,)), ((), ())),
                           preferred_element_type=F32)


def _sigmoid(v):
    return 1.0 / (1.0 + jnp.exp(-v))


def _full(shape):
    n = len(shape)
    return pl.BlockSpec(shape, lambda *_: (0,) * n)


def _token_tile(s):
    return 256 if s % 256 == 0 else s


SUBLANES = 8


def _row_shifts(pad_ref, shifted_ref, rows):
    for r in range(1, SUBLANES):
        shifted_ref[r - 1] = pad_ref[r:r + rows, :]

    def window(i, n):
        r, base = i % SUBLANES, i - i % SUBLANES
        if r == 0:
            return pad_ref[base:base + n, :]
        return shifted_ref[r - 1, base:base + n, :]

    return window


def _place():
    return lax.axis_index("x"), lax.axis_index("y"), lax.axis_index("c")


def _flip(x, y, r):
    return ((1 - x) if r & 2 else x, (1 - y) if r & 1 else y)


def _remote(src, dst, send_sem, recv_sem, dev):
    return pltpu.make_async_remote_copy(src_ref=src, dst_ref=dst, send_sem=send_sem, recv_sem=recv_sem,
                                        device_id=dev, device_id_type=MESH)


class _Comm:
    def __init__(self, ins, outs, aliases, scratch, start, finish, mid=None):
        self.ins, self.outs, self.aliases, self.scratch = list(ins), list(outs), dict(aliases), list(scratch)
        self.start, self.finish = start, finish
        self.mid = mid


def _both(a, b):
    na, nao, nas = len(a.ins), len(a.outs), len(a.scratch)
    aliases = dict(a.aliases)
    aliases.update({na + i: nao + o for i, o in b.aliases.items()})

    def start(ins, outs, scr):
        a.start(ins[:na], outs[:nao], scr[:nas])
        b.start(ins[na:], outs[nao:], scr[nas:])

    def finish(ins, outs, scr):
        a.finish(ins[:na], outs[:nao], scr[:nas])
        b.finish(ins[na:], outs[nao:], scr[nas:])

    def mid(ins, outs, scr):
        if a.mid:
            a.mid(ins[:na], outs[:nao], scr[:nas])
        if b.mid:
            b.mid(ins[na:], outs[nao:], scr[nas:])

    return _Comm(a.ins + b.ins, a.outs + b.outs, aliases, a.scratch + b.scratch, start, finish,
                 mid if (a.mid or b.mid) else None)


def _call(body, *, name, grid, in_specs, out_specs, out_shape, args, scratch_shapes=(), prefetch=(), comm=None,
          body_starts=False, after=()):
    in_specs = list(in_specs) + [ANY] * len(after)
    args = list(args) + list(after)
    n_pre, n_in, n_out, n_scr = len(prefetch), len(in_specs), len(out_specs), len(scratch_shapes)
    n_body_in = n_in - len(after)
    c_ins = comm.ins if comm else []
    c_outs = comm.outs if comm else []
    c_scr = comm.scratch if comm else []
    last = grid[0] - 1

    def wrapped(*refs):
        pre, refs = refs[:n_pre], refs[n_pre:]
        ins, cin = refs[:n_body_in], refs[n_in:n_in + len(c_ins)]
        refs = refs[n_in + len(c_ins):]
        outs, cout = refs[:n_out], refs[n_out:n_out + len(c_outs)]
        refs = refs[n_out + len(c_outs):]
        scr, cscr = refs[:n_scr], refs[n_scr:]
        step = pl.program_id(0)
        if comm and not body_starts:
            @pl.when(step == 0)
            def _():
                comm.start(cin, cout, cscr)

        has_mid = comm is not None and comm.mid is not None
        mid_step = grid[0] // 2 if grid[0] >= 4 else None
        if has_mid and mid_step is not None:
            @pl.when(step == mid_step)
            def _():
                comm.mid(cin, cout, cscr)

        if body_starts:
            body(lambda: comm.start(cin, cout, cscr), lambda: comm.mid(cin, cout, cscr), cout,
                 *pre, *ins, *outs, *scr)
        else:
            body(*pre, *ins, *outs, *scr)
        if comm:
            @pl.when(step == last)
            def _():
                if has_mid and mid_step is None and not body_starts:
                    comm.mid(cin, cout, cscr)
                comm.finish(cin, cout, cscr)

    aliases = {n_pre + n_in + a: n_out + b for a, b in (comm.aliases if comm else {}).items()}
    res = pl.pallas_call(
        wrapped, name=name,
        grid_spec=pltpu.PrefetchScalarGridSpec(
            num_scalar_prefetch=n_pre, grid=grid, in_specs=list(in_specs) + [ANY] * len(c_ins),
            out_specs=list(out_specs) + [ANY] * len(c_outs), scratch_shapes=list(scratch_shapes) + list(c_scr)),
        out_shape=list(out_shape) + list(c_outs),
        input_output_aliases=aliases,
        compiler_params=pltpu.CompilerParams(dimension_semantics=("arbitrary",)),
    )(*prefetch, *args, *c_ins)
    return res[:n_out], res[n_out:]


def _comm_only(name, comm):
    return _call(lambda: None, name=name, grid=(1,), in_specs=[], out_specs=[], out_shape=[], args=[], comm=comm)[1]


def _weights_gather(bufs, split):
    n = len(bufs)

    def ctx(outs):
        x, y, cc = _place()
        chips = dict(me=2 * x + y, y=2 * x + (1 - y), x=2 * (1 - x) + y, d=2 * (1 - x) + (1 - y))
        devs = dict(y=(x, 1 - y, cc), x=(1 - x, y, cc), d=(1 - x, 1 - y, cc), s=(x, y, 1 - cc))

        def piece(a, kj, pc, q=None):
            if not split[a]:
                return outs[a].at[kj]
            h = bufs[a].shape[1] // 2
            if q is None:
                return outs[a].at[kj, pl.ds(pc * h, h), :]
            return outs[a].at[kj, pl.ds(pc * h + q * (h // 2), h // 2), :]

        return cc, chips, devs, piece

    def directs(a, outs, send, recv):
        cc, chips, devs, piece = ctx(outs)
        if not split[a]:
            whole = piece(a, chips["me"], cc)
            return [_remote(whole, whole, send.at[a, k], recv.at[a, k], devs[t]) for k, t in ((0, "y"), (2, "x"), (4, "d"))]
        q = lambda i: piece(a, chips["me"], cc, i)
        return [_remote(q(0), q(0), send.at[a, 0], recv.at[a, 0], devs["y"]),
                _remote(q(1), q(1), send.at[a, 3], recv.at[a, 3], devs["x"]),
                _remote(q(1), q(1), send.at[a, 1], recv.at[a, 1], devs["y"]),
                _remote(q(0), q(0), send.at[a, 2], recv.at[a, 2], devs["x"])]

    def landed(a, k, outs, send, recv):
        cc, chips, devs, piece = ctx(outs)
        if not split[a]:
            got = piece(a, chips[{0: "y", 2: "x", 4: "d"}[k]], cc)
        elif k < 6:
            got = piece(a, chips[("y", "y", "x", "x", "d", "d")[k]], cc, (0, 1, 0, 1, 0, 1)[k])
        else:
            got = piece(a, chips[("y", "x", "d")[k - 6]], 1 - cc)
        return _remote(got, got, send.at[a, k], recv.at[a, k], devs["s"])

    def passed_on(a, outs, send, recv):
        cc, chips, devs, piece = ctx(outs)
        from_y, from_x = piece(a, chips["y"], cc, 0), piece(a, chips["x"], cc, 1)
        return [_remote(from_y, from_y, send.at[a, 4], recv.at[a, 4], devs["x"]),
                _remote(from_x, from_x, send.at[a, 5], recv.at[a, 5], devs["y"])]

    def to_sibling(a, outs, send, recv, which=(0, 1, 2)):
        cc, chips, devs, piece = ctx(outs)
        halves = [piece(a, chips[("y", "x", "d")[i]], cc) for i in which]
        return [_remote(hf, hf, send.at[a, 6 + i], recv.at[a, 6 + i], devs["s"]) for i, hf in zip(which, halves)]

    def start(ins, outs, scr):
        send, recv = scr
        per_item = [directs(a, outs, send, recv) for a in range(n)]
        for rank in range(4):
            for cps in per_item:
                if rank < len(cps):
                    cps[rank].start()

    def mid(ins, outs, scr):
        send, recv = scr
        for a in range(n):
            if split[a]:
                fy, fx = passed_on(a, outs, send, recv)
                landed(a, 0, outs, send, recv).wait_recv()
                fy.start()
                landed(a, 3, outs, send, recv).wait_recv()
                fx.start()

    def finish(ins, outs, scr):
        send, recv = scr
        for a in range(n):
            if split[a]:
                to_y, to_x = to_sibling(a, outs, send, recv, which=(0, 1))
                landed(a, 1, outs, send, recv).wait_recv()
                to_y.start()
                landed(a, 2, outs, send, recv).wait_recv()
                to_x.start()
        for a in range(n):
            if split[a]:
                for k in (4, 5):
                    landed(a, k, outs, send, recv).wait_recv()
                to_sibling(a, outs, send, recv, which=(2,))[0].start()
            else:
                for k in (0, 2, 4):
                    landed(a, k, outs, send, recv).wait_recv()
        for a in range(n):
            if split[a]:
                for k in (6, 7, 8):
                    landed(a, k, outs, send, recv).wait_recv()
            cps = directs(a, outs, send, recv)
            if split[a]:
                cps += passed_on(a, outs, send, recv) + to_sibling(a, outs, send, recv)
            for cp in cps:
                cp.wait_send()

    return _Comm(bufs, [jax.ShapeDtypeStruct(b.shape, b.dtype) for b in bufs], {i: i for i in range(n)},
                 [pltpu.SemaphoreType.DMA((n, 9)), pltpu.SemaphoreType.DMA((n, 9))], start, finish, mid)


HBM =pl.BlockSpec(memory_space=pltpu.HBM)
SEM = pl.BlockSpec(memory_space=pltpu.SEMAPHORE)
DATAFLOW = pltpu.SideEffectType.DATAFLOW_SIDE_EFFECTING


class _SemGrid:
    def __init__(self, refs, cols):
        self.refs, self.cols = refs, cols

    @property
    def at(self):
        return self

    def __getitem__(self, idx):
        return self.refs[idx[0] * self.cols + idx[1]]


def _split_start(name, srcs, lands, sem_shape, copies, zeroed=False):
    n, k = len(srcs), len(lands)
    ns = sem_shape[0] * sem_shape[1]

    def body(*refs):
        src_refs, land_refs = refs[:n], refs[n:n + k]
        send = _SemGrid(refs[n + k:n + k + ns], sem_shape[1])
        recv = _SemGrid(refs[n + k + ns:n + k + 2 * ns], sem_shape[1])
        token = refs[-1]
        for cp in copies(src_refs, land_refs, send, recv):
            cp.start()
        token[...] = jnp.zeros(token.shape, F32)

    hbm = lambda a: pltpu.with_memory_space_constraint(a, pltpu.HBM)
    zones = [jnp.zeros(l.shape, l.dtype) if zeroed else lax.empty(l.shape, l.dtype) for l in lands]
    out = pl.pallas_call(
        body, name=name,
        out_shape=[pltpu.SemaphoreType.DMA(())] * (2 * ns)
        + [pltpu.HBM(a.shape, a.dtype) for a in list(srcs) + list(lands)] + [jax.ShapeDtypeStruct((8, 128), F32)],
        in_specs=[HBM] * (n + k), out_specs=[SEM] * (2 * ns) + [HBM] * (n + k) + [VMEM],
        input_output_aliases={i: 2 * ns + i for i in range(n + k)},
        compiler_params=pltpu.CompilerParams(has_side_effects=DATAFLOW),
    )(*[hbm(a) for a in srcs], *[hbm(z) for z in zones])
    return out[:-1], out[-1]


def _split_wait(name, state, n, sem_shape, copies, after, in_place=False, with_sources=False):
    ns = sem_shape[0] * sem_shape[1]
    sems, bufs = state[:2 * ns], state[2 * ns:]
    k = len(bufs) - n

    def body(*refs):
        src_refs, land_refs = refs[:n], refs[n:n + k]
        send = _SemGrid(refs[n + k:n + k + ns], sem_shape[1])
        recv = _SemGrid(refs[n + k + ns:n + k + 2 * ns], sem_shape[1])
        cps = copies(src_refs, land_refs, send, recv)
        for cp in cps:
            cp.wait_send()
        for cp in cps:
            cp.wait_recv()

    out = pl.pallas_call(
        body, name=name,
        out_shape=[pltpu.HBM(a.shape, a.dtype) for a in bufs],
        in_specs=[HBM] * (n + k) + [SEM] * (2 * ns) + [ANY] * len(after), out_specs=[HBM] * (n + k),
        input_output_aliases={i: i for i in range(n + k)},
        compiler_params=pltpu.CompilerParams(has_side_effects=DATAFLOW),
    )(*bufs, *sems, *after)
    if with_sources:
        return out[:n], out[n:]
    return out[:n] if in_place else out[n:]


def _direct_phases(copies):
    def start(ins, outs, scr):
        for cp in copies(ins, outs, *scr):
            cp.start()

    def finish(ins, outs, scr):
        cps = copies(ins, outs, *scr)
        for cp in cps:
            cp.wait_recv()
        for cp in cps:
            cp.wait_send()

    return start, finish


def _sibling_parts(gs):
    n = len(gs)

    def copies(ins, outs, send, recv):
        x, y, cc = _place()
        cps = []
        for a in range(n):
            h = gs[a].shape[1] // 2
            cps.append(_remote(ins[a].at[:, pl.ds((1 - cc) * h, h), :], outs[a], send.at[a, 0], recv.at[a, 0],
                               (x, y, 1 - cc)))
        return cps

    return [jax.ShapeDtypeStruct((N_CHIP, g.shape[1] // 2, g.shape[2]), F32) for g in gs], (n, 1), copies


def _sibling_halves(gs):
    lands, sem_shape, copies = _sibling_parts(gs)
    start, finish = _direct_phases(copies)
    return _Comm(gs, lands, {}, [pltpu.SemaphoreType.DMA(sem_shape), pltpu.SemaphoreType.DMA(sem_shape)],
                 start, finish)


def _exchange_parts(pbs):
    n = len(pbs)

    def copies(ins, outs, send, recv):
        x, y, cc = _place()
        cps = []
        for a in range(n):
            for r in range(1, N_CHIP):
                kx, ky = _flip(x, y, r)
                cps.append(_remote(ins[a].at[2 * kx + ky], outs[a].at[r - 1], send.at[a, r - 1], recv.at[a, r - 1],
                                   (kx, ky, cc)))
        return cps

    lands = [jax.ShapeDtypeStruct((N_CHIP - 1,) + p.shape[1:], p.dtype) for p in pbs]
    return lands, (n, N_CHIP - 1), copies


def _small_parts(arrs):
    n = len(arrs)

    def copies(ins, outs, send, recv):
        x, y, cc = _place()
        b = 4 * x + 2 * y + cc
        cps = []
        for a in range(n):
            for r in range(1, N_DEV):
                dev = ((1 - x) if r & 4 else x, (1 - y) if r & 2 else y, (1 - cc) if r & 1 else cc)
                cps.append(_remote(ins[a], outs[a].at[b], send.at[a, r - 1], recv.at[a, r - 1], dev))
        return cps

    lands = [jax.ShapeDtypeStruct((N_DEV,) + a.shape, a.dtype) for a in arrs]
    return lands, (n, N_DEV - 1), copies


def _exchange_partials(pbs):
    lands, sem_shape, copies = _exchange_parts(pbs)
    start, finish = _direct_phases(copies)
    return _Comm(pbs, lands, {}, [pltpu.SemaphoreType.DMA(sem_shape), pltpu.SemaphoreType.DMA(sem_shape)],
                 start, finish)


def _join_parts(fulls):
    n = len(fulls)

    def copies(bufs, send, recv):
        x, y, cc = _place()
        cps = []
        for a in range(n):
            h = fulls[a].shape[0] // 2
            mine = bufs[a].at[pl.ds(cc * h, h), :]
            cps.append(_remote(mine, mine, send.at[a, 0], recv.at[a, 0], (x, y, 1 - cc)))
        return cps

    return (n, 1), copies


def _join_halves(fulls):
    sem_shape, copies = _join_parts(fulls)
    start, finish = _direct_phases(lambda ins, outs, send, recv: copies(outs, send, recv))
    return _Comm(fulls, [jax.ShapeDtypeStruct(f.shape, F32) for f in fulls], {i: i for i in range(len(fulls))},
                 [pltpu.SemaphoreType.DMA(sem_shape), pltpu.SemaphoreType.DMA(sem_shape)], start, finish)


def _mixer_fwd(x, mod, g1, w_in, dww, dwb, lng, lnb, w_pw, wg, pscale, w_out, comm=None):
    s = x.shape[0]
    ts = _token_tile(s)
    nt = s // ts

    def body(x_ref, mod_ref, g1_ref, win_ref, dww_ref, dwb_ref, lng_ref, lnb_ref, wpw_ref, wg_ref, ps_ref,
             wout_ref, x2_ref, y_ref, u_ref, z_ref, rstd_ref, p_ref, ycat_ref, gpad, vpad, gshift):
        i = pl.program_id(0)

        @pl.when(i == 0)
        def _():
            gpad[0:CONV_HALO, :] = jnp.zeros((CONV_HALO, CONV_W), F32)
            vpad[0:POOL_HALO, :] = jnp.zeros((POOL_HALO, POOL_W), F32)

        xt = x_ref[...]
        sh1 = mod_ref[0:1, :]
        sc1 = mod_ref[1:2, :]
        gt1 = mod_ref[2:3, :]
        r1 = lax.rsqrt(jnp.mean(xt * xt, axis=-1, keepdims=True) + EPS)
        h1 = (xt * r1 * g1_ref[...]) * (1.0 + sc1) + sh1
        h1b = h1.astype(MXU_DTYPE)
        u = jnp.concatenate([_dot(h1b, win_ref[j]) for j in range(N_CHIP)], axis=1)
        u_ref[...] = u
        a = u[:, :CONV_W]
        g = u[:, CONV_W:2 * CONV_W]
        v = u[:, 2 * CONV_W:]

        gpad[CONV_HALO:CONV_HALO + ts, :] = a * _sigmoid(g)
        window = _row_shifts(gpad, gshift, ts + CONV_HALO - SUBLANES)
        cv = jnp.broadcast_to(dwb_ref[...], (ts, CONV_W))
        off = CONV_HALO - (CONV_K - 1)
        for k in range(CONV_K):
            cv = cv + dww_ref[k:k + 1, :] * window(off + k, ts)
        gpad[0:CONV_HALO, :] = gpad[ts:ts + CONV_HALO, :]

        mu = jnp.mean(cv, axis=-1, keepdims=True)
        cc = cv - mu
        rstd = lax.rsqrt(jnp.mean(cc * cc, axis=-1, keepdims=True) + EPS)
        z = cc * rstd
        z_ref[...] = z
        rstd_ref[...] = rstd
        ln = z * lng_ref[...] + lnb_ref[...]
        sw = ln * _sigmoid(ln)
        yconv = _dot(sw, wpw_ref[...])

        vpad[POOL_HALO:POOL_HALO + ts, :] = v
        t = i * ts + lax.broadcasted_iota(jnp.int32, (ts, 1), 0)
        ps, ypool = [], []
        for gi, w in enumerate(POOL_WINDOWS):
            cols = slice(gi * POOL_G, (gi + 1) * POOL_G)
            acc = vpad[POOL_HALO:POOL_HALO + ts, cols]
            for d in range(1, w):
                acc = acc + vpad[POOL_HALO - d:POOL_HALO - d + ts, cols]
            cnt = jnp.minimum(t + 1, w).astype(F32)
            pg = (acc / cnt - v[:, cols]).astype(MXU_DTYPE)
            ps.append(pg)
            ypool.append(_dot(pg, wg_ref[gi]))
        vpad[0:POOL_HALO, :] = vpad[ts:ts + POOL_HALO, :]
        p_ref[...] = jnp.concatenate(ps, axis=1)
        ypool = jnp.concatenate(ypool, axis=1) * ps_ref[...]

        ycat = jnp.concatenate([yconv, ypool], axis=1).astype(MXU_DTYPE)
        ycat_ref[...] = ycat
        y = _dot(ycat, wout_ref[...])
        y_ref[...] = y
        x2_ref[...] = xt + gt1 * y

    tile = lambda w: pl.BlockSpec((ts, w), lambda i: (i, 0))
    return _call(
        body, name="mixer_fwd", grid=(nt,),
        in_specs=[tile(D_MODEL), _full(mod.shape), _full(g1.shape), _full(w_in.shape), _full(dww.shape),
                  _full(dwb.shape), _full(lng.shape), _full(lnb.shape), _full(w_pw.shape), _full(wg.shape),
                  _full(pscale.shape), _full(w_out.shape)],
        out_specs=[tile(D_MODEL), tile(D_MODEL), tile(IN_W), tile(CONV_W), tile(1), tile(POOL_W), tile(D_MODEL)],
        out_shape=[jax.ShapeDtypeStruct((s, D_MODEL), F32), jax.ShapeDtypeStruct((s, D_MODEL), F32),
                   jax.ShapeDtypeStruct((s, IN_W), F32), jax.ShapeDtypeStruct((s, CONV_W), F32),
                   jax.ShapeDtypeStruct((s, 1), F32), jax.ShapeDtypeStruct((s, POOL_W), MXU_DTYPE),
                   jax.ShapeDtypeStruct((s, D_MODEL), MXU_DTYPE)],
        scratch_shapes=[pltpu.VMEM((ts + CONV_HALO, CONV_W), F32), pltpu.VMEM((ts + POOL_HALO, POOL_W), F32),
                        pltpu.VMEM((SUBLANES - 1, ts + CONV_HALO - SUBLANES, CONV_W), F32)],
        args=(x, mod, g1, w_in, dww, dwb, lng, lnb, w_pw, wg, pscale, w_out), comm=comm)


def _ffn(x2, tgt, mod, g2, gf, w_gate, w_up, w_down):
    s = x2.shape[0]
    ts = _token_tile(s)
    nt = s // ts
    fb = w_gate.shape[1]

    def body(x2_ref, tgt_ref, mod_ref, g2_ref, gf_ref, wgt_ref, wup_ref, wdn_ref,
             dx2_ref, h2_ref, df_ref, act_ref, dgg_ref, duu_ref, vec_ref, gg_s, uu_s):
        i = pl.program_id(0)

        @pl.when(i == 0)
        def _():
            vec_ref[...] = jnp.zeros(vec_ref.shape, F32)

        x2t = x2_ref[...]
        sh2 = mod_ref[3:4, :]
        sc2 = mod_ref[4:5, :]
        gt2 = mod_ref[5:6, :]
        g2v = g2_ref[...]
        gfv = gf_ref[...]
        r2 = lax.rsqrt(jnp.mean(x2t * x2t, axis=-1, keepdims=True) + EPS)
        xh2 = x2t * r2
        n2 = xh2 * g2v
        h2b = (n2 * (1.0 + sc2) + sh2).astype(MXU_DTYPE)
        h2_ref[...] = h2b
        f = jnp.zeros((ts, D_MODEL), F32)
        for j in range(N_CHIP):
            gg = _dot_nt(h2b, wgt_ref[j])
            uu = _dot_nt(h2b, wup_ref[j])
            gg_s[j] = gg
            uu_s[j] = uu
            actb = (gg * _sigmoid(gg) * uu).astype(MXU_DTYPE)
            act_ref[j] = actb
            f = f + _dot(actb, wdn_ref[j])
        x3 = x2t + gt2 * f
        r3 = lax.rsqrt(jnp.mean(x3 * x3, axis=-1, keepdims=True) + EPS)
        xh3 = x3 * r3
        diff = xh3 * gfv - tgt_ref[...]
        dout = diff * (1.0 / D_MODEL)
        dn3 = dout * gfv
        dx3 = r3 * (dn3 - xh3 * jnp.mean(dn3 * xh3, axis=-1, keepdims=True))
        dfb = (dx3 * gt2).astype(MXU_DTYPE)
        df_ref[...] = dfb
        dh2 = jnp.zeros((ts, D_MODEL), F32)
        for j in range(N_CHIP):
            dact = _dot_nt(dfb, wdn_ref[j])
            gg = gg_s[j]
            uu = uu_s[j]
            sg = _sigmoid(gg)
            duu = (dact * (gg * sg)).astype(MXU_DTYPE)
            dgg = (dact * uu * (sg * (1.0 + gg * (1.0 - sg)))).astype(MXU_DTYPE)
            duu_ref[j] = duu
            dgg_ref[j] = dgg
            dh2 = dh2 + _dot(dgg, wgt_ref[j]) + _dot(duu, wup_ref[j])
        dn2 = dh2 * (1.0 + sc2)
        dxh2 = dn2 * g2v
        dx2_ref[...] = dx3 + r2 * (dxh2 - xh2 * jnp.mean(dxh2 * xh2, axis=-1, keepdims=True))

        col = lambda a: jnp.sum(a, axis=0, keepdims=True)
        vec_ref[0:1, :] += col(dout * xh3)
        vec_ref[1:2, :] += col(dx3 * f)
        vec_ref[2:3, :] += col(dh2)
        vec_ref[3:4, :] += col(dh2 * n2)
        vec_ref[4:5, :] += col(dn2 * xh2)
        vec_ref[5:6, :] += col(diff * diff)

    tile = lambda w: pl.BlockSpec((ts, w), lambda i: (i, 0))
    tile3 = pl.BlockSpec((N_CHIP, ts, fb), lambda i: (0, i, 0))
    once = lambda a: pl.BlockSpec(a.shape, lambda i: (0,) * a.ndim, pipeline_mode=pl.Buffered(1))
    hid = jax.ShapeDtypeStruct((N_CHIP, s, fb), MXU_DTYPE)
    return pl.pallas_call(
        body, name="ffn", grid=(nt,),
        in_specs=[tile(D_MODEL), tile(D_MODEL), _full(mod.shape), _full(g2.shape), _full(gf.shape),
                  once(w_gate), once(w_up), once(w_down)],
        out_specs=[tile(D_MODEL), tile(D_MODEL), tile(D_MODEL), tile3, tile3, tile3, _full((8, D_MODEL))],
        out_shape=[jax.ShapeDtypeStruct((s, D_MODEL), F32), jax.ShapeDtypeStruct((s, D_MODEL), MXU_DTYPE),
                   jax.ShapeDtypeStruct((s, D_MODEL), MXU_DTYPE), hid, hid, hid,
                   jax.ShapeDtypeStruct((8, D_MODEL), F32)],
        scratch_shapes=[pltpu.VMEM((N_CHIP, ts, fb), F32), pltpu.VMEM((N_CHIP, ts, fb), F32)],
        compiler_params=pltpu.CompilerParams(dimension_semantics=("arbitrary",)),
    )(x2, tgt, mod, g2, gf, w_gate, w_up, w_down)


def _mixer_bwd(dx2, x, y, u, z, rstd, p, mod, g1, w_in, dww, lng, lnb, w_pw, wg, pscale, w_out, comm=None, after=()):
    s = x.shape[0]
    ts = _token_tile(s)
    nt = s // ts

    def body(dx2_ref, x_ref, y_ref, u_ref, z_ref, rstd_ref, p_ref, mod_ref, g1_ref, win_ref, dww_ref, lng_ref,
             lnb_ref, wpw_ref, wg_ref, ps_ref, wout_ref,
             gx_ref, h1_ref, du_ref, dy_ref, gpw_ref, gwg_ref, vd_ref, vc_ref, ddw_ref, dcpad, dppad,
             dshift):
        i = pl.program_id(0)
        tix = nt - 1 - i

        @pl.when(i == 0)
        def _():
            gpw_ref[...] = jnp.zeros(gpw_ref.shape, F32)
            gwg_ref[...] = jnp.zeros(gwg_ref.shape, F32)
            vd_ref[...] = jnp.zeros(vd_ref.shape, F32)
            vc_ref[...] = jnp.zeros(vc_ref.shape, F32)
            ddw_ref[...] = jnp.zeros(ddw_ref.shape, F32)
            dcpad[ts:ts + CONV_HALO, :] = jnp.zeros((CONV_HALO, CONV_W), F32)
            dppad[ts:ts + POOL_HALO, :] = jnp.zeros((POOL_HALO, POOL_W), F32)

        col = lambda a: jnp.sum(a, axis=0, keepdims=True)
        sh1 = mod_ref[0:1, :]
        sc1 = mod_ref[1:2, :]
        gt1 = mod_ref[2:3, :]
        dx2t = dx2_ref[...]
        vd_ref[0:1, :] += col(dx2t * y_ref[...])
        dyb = (dx2t * gt1).astype(MXU_DTYPE)
        dy_ref[...] = dyb
        dycat = _dot_nt(dyb, wout_ref[...])
        dyconv = dycat[:, :CONV_W]
        dypool = dycat[:, CONV_W:]

        pt = p_ref[...]
        t = tix * ts + lax.broadcasted_iota(jnp.int32, (ts, 1), 0)
        psc = ps_ref[...]
        dypb = (dypool * psc).astype(MXU_DTYPE)
        dps, ypre = [], []
        for gi, w in enumerate(POOL_WINDOWS):
            cols = slice(gi * POOL_G, (gi + 1) * POOL_G)
            gwg_ref[gi] += _dot_tn(pt[:, cols], dypb[:, cols])
            ypre.append(_dot(pt[:, cols], wg_ref[gi]))
            dpg = _dot_nt(dypb[:, cols], wg_ref[gi])
            dps.append(dpg)
            cnt = jnp.minimum(t + 1, w).astype(F32)
            dppad[0:ts, cols] = dpg / cnt
        vc_ref[0:1, :] += col(dypool * jnp.concatenate(ypre, axis=1))
        dvs = []
        for gi, w in enumerate(POOL_WINDOWS):
            cols = slice(gi * POOL_G, (gi + 1) * POOL_G)
            acc = dppad[0:ts, cols]
            for d in range(1, w):
                acc = acc + dppad[d:d + ts, cols]
            dvs.append(acc - dps[gi])
        dv = jnp.concatenate(dvs, axis=1)
        dppad[ts:ts + POOL_HALO, :] = dppad[0:POOL_HALO, :]

        zt = z_ref[...]
        lngv = lng_ref[...]
        ln = zt * lngv + lnb_ref[...]
        sg = _sigmoid(ln)
        swb = (ln * sg).astype(MXU_DTYPE)
        dycb = dyconv.astype(MXU_DTYPE)
        gpw_ref[...] += _dot_tn(swb, dycb)
        dln = _dot_nt(dycb, wpw_ref[...]) * (sg * (1.0 + ln * (1.0 - sg)))
        vc_ref[1:2, :] += col(dln * zt)
        vc_ref[2:3, :] += col(dln)
        dz = dln * lngv
        dcv = rstd_ref[...] * (dz - jnp.mean(dz, axis=-1, keepdims=True)
                               - zt * jnp.mean(dz * zt, axis=-1, keepdims=True))
        vc_ref[3:4, :] += col(dcv)
        dcpad[0:ts, :] = dcv
        ut = u_ref[...]
        a = ut[:, :CONV_W]
        g = ut[:, CONV_W:2 * CONV_W]
        sgg = _sigmoid(g)
        glu = a * sgg
        window = _row_shifts(dcpad, dshift, ts + CONV_HALO - SUBLANES)
        dglu = jnp.zeros((ts, CONV_W), F32)
        for k in range(CONV_K):
            sh = window(CONV_K - 1 - k, ts)
            dglu = dglu + dww_ref[k:k + 1, :] * sh
            ddw_ref[k:k + 1, :] += col(glu * sh)
        dcpad[ts:ts + CONV_HALO, :] = dcpad[0:CONV_HALO, :]
        da = dglu * sgg
        dg = dglu * a * sgg * (1.0 - sgg)
        dub = jnp.concatenate([da, dg, dv], axis=1).astype(MXU_DTYPE)
        du_ref[...] = dub
        cw = IN_W // N_CHIP
        dh1 = jnp.zeros((ts, D_MODEL), F32)
        for j in range(N_CHIP):
            dh1 = dh1 + _dot_nt(dub[:, j * cw:(j + 1) * cw], win_ref[j])

        xt = x_ref[...]
        g1v = g1_ref[...]
        r1 = lax.rsqrt(jnp.mean(xt * xt, axis=-1, keepdims=True) + EPS)
        xh1 = xt * r1
        n1 = xh1 * g1v
        h1_ref[...] = (n1 * (1.0 + sc1) + sh1).astype(MXU_DTYPE)
        vd_ref[1:2, :] += col(dh1)
        vd_ref[2:3, :] += col(dh1 * n1)
        dn1 = dh1 * (1.0 + sc1)
        vd_ref[3:4, :] += col(dn1 * xh1)
        dxh = dn1 * g1v
        gx_ref[...] = dx2t + r1 * (dxh - xh1 * jnp.mean(dxh * xh1, axis=-1, keepdims=True))

    tile = lambda w: pl.BlockSpec((ts, w), lambda i: (nt - 1 - i, 0))
    bf = lambda w: jax.ShapeDtypeStruct((s, w), MXU_DTYPE)
    return _call(
        body, name="mixer_bwd", grid=(nt,),
        in_specs=[tile(D_MODEL), tile(D_MODEL), tile(D_MODEL), tile(IN_W), tile(CONV_W), tile(1), tile(POOL_W),
                  _full(mod.shape), _full(g1.shape), _full(w_in.shape), _full(dww.shape), _full(lng.shape),
                  _full(lnb.shape), _full(w_pw.shape), _full(wg.shape), _full(pscale.shape), _full(w_out.shape)],
        out_specs=[tile(D_MODEL), tile(D_MODEL), tile(IN_W), tile(D_MODEL), _full((CONV_W, CONV_W)),
                   _full(wg.shape), _full((8, D_MODEL)), _full((8, CONV_W)), _full((32, CONV_W))],
        out_shape=[jax.ShapeDtypeStruct((s, D_MODEL), F32), bf(D_MODEL), bf(IN_W), bf(D_MODEL),
                   jax.ShapeDtypeStruct((CONV_W, CONV_W), F32),
                   jax.ShapeDtypeStruct(wg.shape, F32), jax.ShapeDtypeStruct((8, D_MODEL), F32),
                   jax.ShapeDtypeStruct((8, CONV_W), F32), jax.ShapeDtypeStruct((32, CONV_W), F32)],
        scratch_shapes=[pltpu.VMEM((ts + CONV_HALO, CONV_W), F32), pltpu.VMEM((ts + POOL_HALO, POOL_W), F32),
                        pltpu.VMEM((SUBLANES - 1, ts + CONV_HALO - SUBLANES, CONV_W), F32)],
        args=(dx2, x, y, u, z, rstd, p, mod, g1, w_in, dww, lng, lnb, w_pw, wg, pscale, w_out), comm=comm,
        after=after)


def _dw(name, a, a_spec, b, b_spec, nb, mb, nbk, comm=None, after=()):
    def body(a_ref, b_ref, o_ref):
        av = a_ref[...]
        bv = b_ref[...]
        av = av.reshape(av.shape[-2:])
        bv = bv.reshape(bv.shape[-2:])
        o_ref[0] = _dot_tn(av, bv)

    (out,), rest = _call(
        body, name=name, grid=(nb,), in_specs=[a_spec, b_spec],
        out_specs=[pl.BlockSpec((1, mb, nbk), lambda j: (j, 0, 0))],
        out_shape=[jax.ShapeDtypeStruct((nb, mb, nbk), F32)], args=(a, b), comm=comm, after=after)
    return out, rest


def _dw_mixer(ycat, dy, h1, du, after=()):
    s = ycat.shape[0]

    def body(ycat_ref, dy_ref, h1_ref, du_ref, out_ref, in_ref):
        out_ref[0] = _dot_tn(ycat_ref[...], dy_ref[...])
        in_ref[0] = _dot_tn(h1_ref[...], du_ref[...])

    whole = lambda w: pl.BlockSpec((s, w), lambda j: (0, 0))
    cols = lambda w: pl.BlockSpec((s, w), lambda j: (0, j))
    blk = lambda m, n: pl.BlockSpec((1, m, n), lambda j: (j, 0, 0))
    shapes = [(D_MODEL // N_CHIP, D_MODEL), (D_MODEL, IN_W // N_CHIP)]
    res, _ = _call(
        body, name="dw_mixer", grid=(N_CHIP,),
        in_specs=[cols(D_MODEL // N_CHIP), whole(D_MODEL), whole(D_MODEL), cols(IN_W // N_CHIP)],
        out_specs=[blk(m, n) for m, n in shapes],
        out_shape=[jax.ShapeDtypeStruct((N_CHIP, m, n), F32) for m, n in shapes],
        args=(ycat, dy, h1, du), after=after)
    return res


def _ada_fwd(c, w_ada, b4, first, later, dww, wg, comm):
    nc = w_ada.shape[1]
    nf, nl = len(first), len(later)
    shards = list(first) + list(later)

    def body(start_comm, mid_comm, gathered, c_ref, w_ref, b4_ref, *refs):
        shard_refs, refs = refs[:nf + nl], refs[nf + nl:]
        dww_ref, wg_ref, mod_ref, cact_ref, wgb_ref = refs[:5]
        later_refs, refs = refs[5:5 + nl], refs[5 + nl:]
        call, part, parts = refs[:3]
        stages, refs = refs[3:3 + nf + nl], refs[3 + nf + nl:]
        fetched, refs = refs[:nl], refs[nl:]
        w_vmem, send1, recv1, send2, recv2, lsem, fsem = refs
        x, y, cc = _place()
        b = 4 * x + 2 * y + cc
        j = 2 * x + y
        w_fetch = pltpu.make_async_copy(w_ref, w_vmem, fsem.at[nl])
        w_fetch.start()
        fetches = [pltpu.make_async_copy(shard_refs[nf + a], fetched[a], fsem.at[a]) for a in range(nl)]
        for cp in fetches:
            cp.start()

        def slot_copies(lo, hi):
            cps = []
            for a in range(lo, hi):
                dst = gathered[a] if a < nf else later_refs[a - nf]
                cps.append(pltpu.make_async_copy(stages[a], dst.at[j], lsem.at[a]))
            return cps

        call[b] = c_ref[...]
        sends = []
        for r in range(1, N_DEV):
            dev = ((1 - x) if r & 4 else x, (1 - y) if r & 2 else y, (1 - cc) if r & 1 else cc)
            cp = _remote(call.at[b], call.at[b], send1.at[r - 1], recv1.at[r - 1], dev)
            cp.start()
            sends.append(cp)
        for a in range(nf):
            stages[a][...] = shard_refs[a][...].astype(MXU_DTYPE)
        dww_copy = pltpu.make_async_copy(dww_ref, gathered[nf].at[j], lsem.at[nf + nl])
        dww_copy.start()
        for cp in slot_copies(0, nf):
            cp.start()
        for r in range(1, N_DEV):
            src_b = lax.bitwise_xor(b, r)
            _remote(call.at[src_b], call.at[src_b], send1.at[r - 1], recv1.at[r - 1], (x, y, cc)).wait_recv()
        for cp in sends:
            cp.wait_send()
        for cp in slot_copies(0, nf):
            cp.wait()
        dww_copy.wait()
        start_comm()
        for i in range(N_DEV):
            ci = call[i]
            cact_ref[i:i + 1, :] = ci * _sigmoid(ci)
        w_fetch.wait()
        part[...] = jnp.dot(cact_ref[...], w_vmem[...], preferred_element_type=F32, precision=lax.Precision.HIGHEST)
        sends = []
        for r in range(1, N_CHIP):
            kx, ky = _flip(x, y, r)
            cp = _remote(part, parts.at[j], send2.at[r - 1], recv2.at[r - 1], (kx, ky, cc))
            cp.start()
            sends.append(cp)
        parts[j] = part[...]
        for a in range(nl):
            fetches[a].wait()
            stages[nf + a][...] = fetched[a][...].astype(MXU_DTYPE)
        for cp in slot_copies(nf, nf + nl):
            cp.start()
        wgb_ref[...] = wg_ref[...].astype(MXU_DTYPE)
        mid_comm()
        for r in range(1, N_CHIP):
            kx, ky = _flip(x, y, r)
            kj = 2 * kx + ky
            _remote(part, parts.at[kj], send2.at[r - 1], recv2.at[r - 1], (x, y, cc)).wait_recv()
        for cp in sends:
            cp.wait_send()
        mine = lax.broadcasted_iota(jnp.int32, (N_DEV, 1), 0) == b
        for k in range(N_CHIP):
            row = jnp.sum(jnp.where(mine, parts[k], 0.0), axis=0, keepdims=True) + b4_ref[k:k + 1, :]
            lo = k * nc
            while lo < (k + 1) * nc:
                q, at = lo // D_MODEL, lo % D_MODEL
                n = min(D_MODEL - at, (k + 1) * nc - lo)
                mod_ref[q:q + 1, at:at + n] = row[:, lo - k * nc:lo - k * nc + n]
                lo += n
        for cp in slot_copies(nf, nf + nl):
            cp.wait()

    res, rest = _call(
        body, name="ada_fwd", grid=(1,),
        in_specs=[VMEM, ANY, VMEM] + [VMEM] * nf + [ANY] * nl + [VMEM] * 2,
        out_specs=[VMEM, VMEM, VMEM] + [ANY] * nl,
        out_shape=[jax.ShapeDtypeStruct((N_CHIP * nc // D_MODEL, D_MODEL), F32),
                   jax.ShapeDtypeStruct((N_DEV, D_MODEL), F32),
                   jax.ShapeDtypeStruct(wg.shape, MXU_DTYPE)]
        + [jax.ShapeDtypeStruct((N_CHIP,) + a.shape, MXU_DTYPE) for a in later],
        scratch_shapes=[pltpu.VMEM((N_DEV, 1, D_MODEL), F32), pltpu.VMEM((N_DEV, nc), F32),
                        pltpu.VMEM((N_CHIP, N_DEV, nc), F32)]
        + [pltpu.VMEM(a.shape, MXU_DTYPE) for a in shards] + [pltpu.VMEM(a.shape, F32) for a in later]
        + [pltpu.VMEM(w_ada.shape, F32),
           pltpu.SemaphoreType.DMA((N_DEV - 1,)), pltpu.SemaphoreType.DMA((N_DEV - 1,)),
           pltpu.SemaphoreType.DMA((N_CHIP - 1,)), pltpu.SemaphoreType.DMA((N_CHIP - 1,)),
           pltpu.SemaphoreType.DMA((nf + nl + 1,)), pltpu.SemaphoreType.DMA((nl + 1,))],
        args=(c, w_ada, b4, *shards, dww, wg), comm=comm, body_starts=True)
    return (res[0], res[1], res[2], res[3:]), rest


def _chip_partials(name, place, gs, rs, comm=None, after=()):
    n = len(gs)

    def body(pref, *refs):
        g_refs, r_refs = refs[:n], refs[n:2 * n]
        pb_refs, own_refs = refs[2 * n:3 * n], refs[3 * n:]
        jj = pl.program_id(0)
        for a in range(n):
            sm = g_refs[a][0] + r_refs[a][0]
            pb_refs[a][0] = sm.astype(MXU_DTYPE)

            @pl.when(jj == pref[1])
            def _(a=a, sm=sm):
                own_refs[a][...] = sm

    halves = [(g.shape[1] // 2, g.shape[2]) for g in gs]
    in_specs = [pl.BlockSpec((1, h, w), lambda jj, pref: (jj, pref[0], 0)) for h, w in halves]
    in_specs += [pl.BlockSpec((1, h, w), lambda jj, pref: (jj, 0, 0)) for h, w in halves]
    out_specs = [pl.BlockSpec((1, h, w), lambda jj, pref: (jj, 0, 0)) for h, w in halves]
    out_specs += [pl.BlockSpec((h, w), lambda jj, pref: (0, 0)) for h, w in halves]
    out, rest = _call(
        body, name=name, grid=(N_CHIP,), in_specs=in_specs, out_specs=out_specs,
        out_shape=[jax.ShapeDtypeStruct((N_CHIP, h, w), MXU_DTYPE) for h, w in halves]
        + [jax.ShapeDtypeStruct((h, w), F32) for h, w in halves],
        args=(*gs, *rs), prefetch=(place,), comm=comm, after=after)
    return (out[:n], out[n:]), rest


def _sum_partials(name, place, owns, recvd, comm=None, after=()):
    n = len(owns)

    def body(pref, *refs):
        o_refs, r_refs, out_refs = refs[:n], refs[n:2 * n], refs[2 * n:]
        for a in range(n):
            acc = o_refs[a][...]
            for r in range(N_CHIP - 1):
                acc = acc + r_refs[a][r].astype(F32)
            out_refs[a][...] = acc

    full = lambda a: pl.BlockSpec(a.shape, lambda i, pref: (0,) * a.ndim)
    return _call(
        body, name=name, grid=(1,), in_specs=[full(a) for a in list(owns) + list(recvd)],
        out_specs=[pl.BlockSpec(o.shape, lambda i, pref: (pref[0], 0)) for o in owns],
        out_shape=[jax.ShapeDtypeStruct((2 * o.shape[0], o.shape[1]), F32) for o in owns],
        args=(*owns, *recvd), prefetch=(place,), comm=comm, after=after)


def _adamw_math(w, g, m, v):
    m = ADAM_B1 * m + (1.0 - ADAM_B1) * g
    v = ADAM_B2 * v + (1.0 - ADAM_B2) * (g * g)
    m_hat = m / (1.0 - ADAM_B1 ** ADAM_STEP)
    v_hat = v / (1.0 - ADAM_B2 ** ADAM_STEP)
    delta = -ADAM_LR * (m_hat / (jnp.sqrt(v_hat) + ADAM_EPS) + ADAM_WD * w)
    return delta, m, v


def _row_tile(rows):
    for t in (512, 352, 256, 128):
        if rows % t == 0:
            return t
    return rows


def _adamw(name, wgmv, steps, after=()):
    n = len(wgmv)

    def body(*refs):
        ins, outs = refs[:4 * n], refs[4 * n:]
        for i in range(n):
            w_ref, g_ref, m_ref, v_ref = ins[4 * i:4 * i + 4]
            d_ref, nm_ref, nv_ref = outs[3 * i:3 * i + 3]
            d_ref[...], nm_ref[...], nv_ref[...] = _adamw_math(w_ref[...], g_ref[...], m_ref[...], v_ref[...])

    in_specs, out_specs, out_shape, args = [], [], [], []
    for w, g, m, v in wgmv:
        rows, cols = w.shape
        spec = pl.BlockSpec((rows // steps, cols), lambda i: (i, 0))
        in_specs += [spec] * 4
        out_specs += [spec] * 3
        out_shape += [jax.ShapeDtypeStruct(w.shape, F32)] * 3
        args += [w, g, m, v]
    res, _ = _call(body, name=name, grid=(steps,), in_specs=in_specs, out_specs=out_specs, out_shape=out_shape,
                   args=args, after=after)
    return [res[3 * i:3 * i + 3] for i in range(n)]


def _adamw_ada(place, cact, dmod, w, m, v, after=()):
    rows, cols = w.shape
    tr = _row_tile(rows)

    def body(pref, ca_ref, dm_ref, w_ref, m_ref, v_ref, g_ref, d_ref, nm_ref, nv_ref):
        g = lax.dot_general(ca_ref[...], dm_ref[...], (((0,), (0,)), ((), ())), preferred_element_type=F32,
                            precision=lax.Precision.HIGHEST)
        g_ref[...] = g
        d_ref[...], nm_ref[...], nv_ref[...] = _adamw_math(w_ref[...], g, m_ref[...], v_ref[...])

    spec = pl.BlockSpec((tr, cols), lambda i, pref: (i, 0))
    return _call(
        body, name="adamw_ada", grid=(rows // tr,),
        in_specs=[pl.BlockSpec((N_DEV, tr), lambda i, pref: (0, i)),
                  pl.BlockSpec((N_DEV, cols), lambda i, pref: (0, pref[1])), spec, spec, spec],
        out_specs=[spec] * 4, out_shape=[jax.ShapeDtypeStruct(w.shape, F32)] * 4,
        args=(cact, dmod, w, m, v), prefetch=(place,), after=after)[0]


def _adamw_small(place, owns, gathered, wmv):
    nw = len(wmv)
    flat = [a for t in wmv for a in t]

    def body(pref, *refs):
        own_refs, all_refs, refs = refs[:5], refs[5:10], refs[10:]
        w_refs = refs[:3 * nw]
        loss_ref, dmod_ref = refs[3 * nw], refs[3 * nw + 1]
        o_refs = refs[3 * nw + 2:]
        j = pref[1]
        me = 2 * pref[1] + pref[0]

        def total(i):
            acc = None
            for b in range(N_DEV):
                blk = jnp.where(me == b, own_refs[i][...], all_refs[i][b])
                acc = blk if acc is None else acc + blk
            return acc

        vf, vd, vc, ddw, gwg = [total(i) for i in range(5)]
        loss_ref[...] = (0.5 / D_MODEL) * jnp.sum(vf[5:6, :], axis=1, keepdims=True)
        order = ((1, 1), (1, 2), (1, 0), (0, 2), (0, 3), (0, 1))
        for b in range(N_DEV):
            for q, (i, row) in enumerate(order):
                dmod_ref[b:b + 1, q * D_MODEL:(q + 1) * D_MODEL] = jnp.where(
                    me == b, own_refs[i][row:row + 1, :], all_refs[i][b, row:row + 1, :])
        dm = dmod_ref[...]
        g_bada = dm[0:1, :]
        for b in range(1, N_DEV):
            g_bada = g_bada + dm[b:b + 1, :]
        g_dww = jnp.zeros((32, POOL_G), F32)
        for k in range(N_CHIP):
            g_dww = g_dww + jnp.where(j == k, ddw[:, k * POOL_G:(k + 1) * POOL_G], 0.0)
        grads = [g_bada, vd[3:4, :], g_dww, vc[3:4, :], vc[1:2, :], vc[2:3, :], gwg, vc[0:1, :], vf[4:5, :],
                 vf[0:1, :]]
        for i, g in enumerate(grads):
            w_ref, m_ref, v_ref = w_refs[3 * i:3 * i + 3]
            d, nm, nv = _adamw_math(w_ref[...], g, m_ref[...], v_ref[...])
            o_refs[4 * i][...] = g
            o_refs[4 * i + 1][...] = d
            o_refs[4 * i + 2][...] = nm
            o_refs[4 * i + 3][...] = nv

    outs = [jax.ShapeDtypeStruct((1, 1), F32), jax.ShapeDtypeStruct((N_DEV, 6 * D_MODEL), F32)]
    for w, _, _ in wmv:
        outs += [jax.ShapeDtypeStruct(w.shape, F32)] * 4
    full = lambda a: pl.BlockSpec(a.shape, lambda i, pref: (0,) * a.ndim)
    args = list(owns) + list(gathered) + flat
    res, _ = _call(body, name="adamw_small", grid=(1,), in_specs=[full(a) for a in args],
                   out_specs=[full(o) for o in outs], out_shape=outs, args=args, prefetch=(place,))
    return res[0], res[1], [res[2 + 4 * i:6 + 4 * i] for i in range(nw)]


def kernel(x, c, w_ada, b_ada, g_norm1, w_in, dw_w, dw_b, conv_ln_g, conv_ln_b, w_conv_pw, w_pool_group, pool_scale, w_out, g_norm2, w_ffn_gate, w_ffn_up, w_ffn_down, g_final, loss_target, m_w_ada, m_b_ada, m_g_norm1, m_w_in, m_dw_w, m_dw_b, m_conv_ln_g, m_conv_ln_b, m_w_conv_pw, m_w_pool_group, m_pool_scale, m_w_out, m_g_norm2, m_w_ffn_gate, m_w_ffn_up, m_w_ffn_down, m_g_final, v_w_ada, v_b_ada, v_g_norm1, v_w_in, v_dw_w, v_dw_b, v_conv_ln_g, v_conv_ln_b, v_w_conv_pw, v_w_pool_group, v_pool_scale, v_w_out, v_g_norm2, v_w_ffn_gate, v_w_ffn_up, v_w_ffn_down, v_g_final):
    xi, yi, ci = _place()
    place = jnp.stack([ci, 2 * xi + yi]).astype(jnp.int32)
    n_ada = w_ada.shape[2]

    tr = lambda a: jnp.transpose(a[0])
    mixer_shards = [w_in[0], w_conv_pw[0], w_out[0]]
    ffn_shards = [tr(w_ffn_gate), tr(w_ffn_up), w_ffn_down[0]]
    slots = [lax.empty((N_CHIP,) + a.shape, MXU_DTYPE) for a in mixer_shards] + [lax.empty((N_CHIP,) + dw_w.shape[1:], F32)]

    (mod, cact, wg_b, (b_gate, b_up, b_down)), (win_g, wpw_g, wout_g, dww_g) = _ada_fwd(
        c, w_ada[0], b_ada.reshape(N_CHIP, n_ada), mixer_shards, ffn_shards, dw_w[0], w_pool_group[0],
        comm=_weights_gather(slots, [True, True, True, False]))
    dww_full = jnp.pad(jnp.concatenate([dww_g[k] for k in range(N_CHIP)], axis=1), ((0, 1), (0, 0)))
    w_pw = wpw_g.reshape(CONV_W, CONV_W)
    w_o = wout_g.reshape(D_MODEL, D_MODEL)
    xs, tgt, gf = x[0], loss_target[0], g_final.reshape(1, D_MODEL)
    s = xs.shape[0]
    fb = b_gate.shape[1]

    (x2, y, u, z, rstd, p, ycat), (wgate_g, wup_g, wdown_g) = _mixer_fwd(
        xs, mod, g_norm1, win_g, dww_full, dw_b, conv_ln_g, conv_ln_b, w_pw, wg_b, pool_scale, w_o,
        comm=_weights_gather([b_gate, b_up, b_down], [True, True, True]))
    dx2, h2, df, act, dgg, duu, vec_f = _ffn(x2, tgt, mod, g_norm2, gf, wgate_g, wup_g, wdown_g)

    whole = lambda w: pl.BlockSpec((s, w), lambda j: (0, 0))
    hid = pl.BlockSpec((1, s, fb), lambda j: (j, 0, 0))
    c_gate, _ = _dw("dw_gate", dgg, hid, h2, whole(D_MODEL), N_CHIP, fb, D_MODEL)
    c_up, (r_gate,) = _dw("dw_up", duu, hid, h2, whole(D_MODEL), N_CHIP, fb, D_MODEL, comm=_sibling_halves([c_gate]))
    c_down, (r_up,) = _dw("dw_down", act, hid, df, whole(D_MODEL), N_CHIP, fb, D_MODEL,
                          comm=_sibling_halves([c_up]))
    sib_down = _sibling_parts([c_down])
    st_sd, tok_sd = _split_start("sibling_down_start", [c_down], *sib_down)
    ((pb_gate, pb_up), (own_gate, own_up)), _ = _chip_partials("partials_gate_up", place, [c_gate, c_up],
                                                               [r_gate, r_up], after=(tok_sd,))
    (c_down,), (r_down,) = _split_wait("sibling_down_wait", st_sd, 1, sib_down[1], sib_down[2], after=(pb_up,),
                                       with_sources=True)
    ((pb_down,), (own_down,)), _ = _chip_partials("partials_down", place, [c_down], [r_down])
    ex_ud = _exchange_parts([pb_gate, pb_up, pb_down])
    st_ud, tok_ud = _split_start("exchange_ffn_start", [pb_gate, pb_up, pb_down], *ex_ud)
    (gx, h1, du, dy, c_pw, g_wg, vec_d, vec_c, ddw), _ = _mixer_bwd(
        dx2, xs, y, u, z, rstd, p, mod, g_norm1, win_g, dww_full, conv_ln_g, conv_ln_b, w_pw, wg_b, pool_scale, w_o,
        after=(tok_ud,))

    small_own = [vec_f, vec_d, vec_c, ddw, g_wg]
    ex_small = _small_parts(small_own)
    st_small, tok_small = _split_start("small_grads_start", small_own, *ex_small, zeroed=True)
    c_out, c_in = _dw_mixer(ycat, dy, h1, du, after=(tok_small,))
    mix = [c_in, c_pw.reshape(N_CHIP, CONV_W // N_CHIP, CONV_W), c_out]
    sib_mix = _sibling_parts(mix)
    st_sm, tok_sm = _split_start("sibling_mix_start", mix, *sib_mix)
    rc_gate, rc_up, rc_down = _split_wait("exchange_ffn_wait", st_ud, 3, ex_ud[1], ex_ud[2], after=(tok_sm,))

    ffn_fulls, _ = _sum_partials("sum_ffn", place, [own_gate, own_up, own_down], [rc_gate, rc_up, rc_down])
    join_sems, join_copies = _join_parts(ffn_fulls)
    join_in_place = lambda bufs, lands, send, recv: join_copies(bufs, send, recv)
    st_jf, tok_jf = _split_start("join_ffn_start", ffn_fulls, [], join_sems, join_in_place)
    mix, (r_in, r_pw, r_out) = _split_wait("sibling_mix_wait", st_sm, len(mix), sib_mix[1], sib_mix[2],
                                           after=(tok_jf,), with_sources=True)
    (pbs_mix, owns_mix), _ = _chip_partials("partials_mix", place, mix, [r_in, r_pw, r_out])
    g_gate, g_up, g_down = _split_wait("join_ffn_wait", st_jf, len(ffn_fulls), join_sems, join_in_place,
                                       after=(pbs_mix[0],), in_place=True)

    pad_rows = lambda a: jnp.pad(a[0], ((0, 1), (0, 0)))
    row = lambda a: a.reshape(1, -1)
    small = [(b_ada, m_b_ada, v_b_ada), (g_norm1, m_g_norm1, v_g_norm1),
             (pad_rows(dw_w), pad_rows(m_dw_w), pad_rows(v_dw_w)), (dw_b, m_dw_b, v_dw_b),
             (conv_ln_g, m_conv_ln_g, v_conv_ln_g), (conv_ln_b, m_conv_ln_b, v_conv_ln_b),
             (w_pool_group[0], m_w_pool_group[0], v_w_pool_group[0]), (pool_scale, m_pool_scale, v_pool_scale),
             (g_norm2, m_g_norm2, v_g_norm2), (row(g_final), row(m_g_final), row(v_g_final))]
    lead = lambda outs: [a[None] for a in outs]

    lands, sem_shape, copies = _exchange_parts(pbs_mix)
    state, token = _split_start("exchange_mix_start", pbs_mix, lands, sem_shape, copies)
    u_gate, u_up, u_down = _adamw(
        "adamw_ffn", [(tr(w_ffn_gate), g_gate, tr(m_w_ffn_gate), tr(v_w_ffn_gate)),
                      (tr(w_ffn_up), g_up, tr(m_w_ffn_up), tr(v_w_ffn_up)),
                      (w_ffn_down[0], g_down, m_w_ffn_down[0], v_w_ffn_down[0])],
        steps=4, after=(token,))
    o_gate = [jnp.transpose(o) for o in [g_gate] + list(u_gate)]
    o_up = [jnp.transpose(o) for o in [g_up] + list(u_up)]
    o_down = [g_down] + list(u_down)
    small_own, small_all = _split_wait("small_grads_wait", st_small, len(small_own), ex_small[1], ex_small[2],
                                       after=(u_down[0],), with_sources=True)
    loss, dmod, small_out = _adamw_small(place, small_own, small_all, small)
    (o_bada, o_g1, o_dww, o_dwb, o_lng, o_lnb, o_wg, o_ps, o_g2, o_gf) = small_out
    o_dww = [a[:CONV_K] for a in o_dww]
    o_gf = [a.reshape(D_MODEL) for a in o_gf]
    o_ada = _adamw_ada(place, cact, dmod, w_ada[0], m_w_ada[0], v_w_ada[0], after=(token,))
    rc_mix = _split_wait("exchange_mix_wait", state, len(pbs_mix), sem_shape, copies, after=(o_ada[1], u_down[0]))
    mix_fulls, _ = _sum_partials("sum_mix", place, owns_mix, rc_mix)
    g_in, g_pw, g_out = _comm_only("join_mix", _join_halves(mix_fulls))
    u_in, u_pw, u_out = _adamw(
        "adamw_mix", [(w_in[0], g_in, m_w_in[0], v_w_in[0]), (w_conv_pw[0], g_pw, m_w_conv_pw[0], v_w_conv_pw[0]),
                      (w_out[0], g_out, m_w_out[0], v_w_out[0])], steps=4)
    o_in, o_pw, o_out = [g_in] + list(u_in), [g_pw] + list(u_pw), [g_out] + list(u_out)

    per_weight = [lead(o_ada), o_bada, o_g1, lead(o_in), lead(o_dww), o_dwb, o_lng, o_lnb, lead(o_pw), lead(o_wg),
                  o_ps, lead(o_out), o_g2, lead(o_gate), lead(o_up), lead(o_down), o_gf]
    result = [loss.reshape(()), gx[None]]
    for kind in range(4):
        result += [o[kind] for o in per_weight]
    return tuple(result)
```
